```python
import jax, jax.numpy as jnp
from jax import lax
import numpy as np

D_MODEL = 2048
BATCH = 1
SEQ = 8192
DEPTH = 1

PLE_DIM = 256
HGRN_HEADS = 8
HGRN_HEAD_DIM = 128
HGRN_WIDTH = HGRN_HEADS * HGRN_HEAD_DIM
HGRN_CHUNK = 64
MOBA_HEADS = 8
MOBA_HEAD_DIM = 128
MOBA_WIDTH = MOBA_HEADS * MOBA_HEAD_DIM
MOBA_BLOCK = 256
MOBA_TOPK = 3
MOBA_QBLOCK = 64
ROPE_THETA = 10000.0
N_GROUPS = 4
EXPERTS_PER_GROUP = 8
N_EXPERTS = N_GROUPS * EXPERTS_PER_GROUP
TOPK_IN_GROUP = 2
D_EXPERT = 512
MOE_ROW_BLOCK = 128
IN_COLS = 4 * HGRN_WIDTH + 3 * MOBA_WIDTH + 2 * D_MODEL
EPS = 1e-6
NEG_INF = -1e30

kernel_name = "hybrid_hgrn2_moba_hmoe_block"


def rmsnorm(x, g):
    x32 = x.astype(jnp.float32)
    y = x32 * lax.rsqrt(jnp.mean(x32 * x32, axis=-1, keepdims=True) + EPS) * g.astype(jnp.float32)
    return y.astype(x.dtype)


def rope(x, positions):
    half = x.shape[-1] // 2
    inv_freq = ROPE_THETA ** (-jnp.arange(half, dtype=jnp.float32) / half)
    ang = positions.astype(jnp.float32)[:, None] * inv_freq[None, :]
    cos = jnp.cos(ang)[None, :, None, :]
    sin = jnp.sin(ang)[None, :, None, :]
    x32 = x.astype(jnp.float32)
    x1, x2 = x32[..., :half], x32[..., half:]
    out = jnp.concatenate([x1 * cos - x2 * sin, x2 * cos + x1 * sin], axis=-1)
    return out.astype(x.dtype)


def hgrn2_mixer(q_raw, f_raw, i_raw, og_raw, lb, norm_g):
    B, T, _ = q_raw.shape
    H, dk, C = HGRN_HEADS, HGRN_HEAD_DIM, HGRN_CHUNK
    nc = T // C
    f32 = jnp.float32
    q = jax.nn.silu(q_raw.astype(f32))
    lb32 = lb.astype(f32)
    log_f = jnp.logaddexp(jnp.log(lb32), jnp.log1p(-lb32) + jax.nn.log_sigmoid(f_raw.astype(f32)))
    k = -jnp.expm1(log_f)
    v = i_raw.astype(f32)

    def to_chunks(a):
        return a.reshape(B, nc, C, H, dk).transpose(1, 0, 3, 2, 4)

    causal = jnp.tril(jnp.ones((C, C), dtype=bool))[:, :, None]

    def step(S, inp):
        qc, kc, vc, gc = inp
        b = jnp.cumsum(gc, axis=2)
        diff = b[:, :, :, None, :] - b[:, :, None, :, :]
        decay = jnp.where(causal, jnp.exp(jnp.where(causal, diff, 0.0)), 0.0)
        scores = jnp.einsum('bhtk,bhsk,bhtsk->bhts', qc, kc, decay)
        o = (jnp.einsum('bhts,bhsv->bhtv', scores, vc)
             + jnp.einsum('bhtk,bhkv->bhtv', qc * jnp.exp(b), S))
        b_last = b[:, :, -1:, :]
        S = (jnp.exp(b_last)[:, :, 0, :, None] * S
             + jnp.einsum('bhsk,bhsv->bhkv', kc * jnp.exp(b_last - b), vc))
        return S, o

    S0 = jnp.zeros((B, H, dk, dk), f32)
    _, o = lax.scan(step, S0, (to_chunks(q), to_chunks(k), to_chunks(v), to_chunks(log_f)))
    o = o.transpose(1, 0, 3, 2, 4).reshape(B, T, H, dk)
    o = o * lax.rsqrt(jnp.mean(o * o, axis=-1, keepdims=True) + EPS) * norm_g.astype(f32)
    o = o.reshape(B, T, H * dk) * jax.nn.silu(og_raw.astype(f32))
    return o.astype(q_raw.dtype)


def moba_mixer(q, k, v):
    B, T, H, hd = q.shape
    BS, QB = MOBA_BLOCK, MOBA_QBLOCK
    f32 = jnp.float32
    nb = -(-T // BS)
    t_pad = nb * BS
    nq = T // QB
    k_sel_n = min(MOBA_TOPK, nb)
    pad = ((0, 0), (0, t_pad - T), (0, 0), (0, 0))
    kbh = jnp.pad(k, pad).reshape(B, nb, BS, H, hd).transpose(0, 3, 1, 2, 4)
    vbh = jnp.pad(v, pad).reshape(B, nb, BS, H, hd).transpose(0, 3, 1, 2, 4)
    k_mean = jnp.mean(kbh.astype(f32), axis=3)
    scale = hd ** -0.5
    bi = jnp.arange(B)[:, None, None, None]
    hi = jnp.arange(H)[None, None, :, None]
    q_chunks = q.reshape(B, nq, QB, H, hd).transpose(1, 0, 2, 3, 4)

    def attend(args):
        qc, ci = args
        q_pos = ci * QB + jnp.arange(QB)
        cur = (ci * QB) // BS
        gate = jnp.einsum('bqhd,bhnd->bqhn', qc.astype(f32), k_mean)
        gate = jnp.where(jnp.arange(nb) < cur, gate, NEG_INF)
        _, sel = lax.top_k(gate, k_sel_n)
        valid = sel < cur
        kg = kbh[bi, hi, sel]
        vg = vbh[bi, hi, sel]
        s_sel = jnp.einsum('bqhd,bqhjsd->bqhjs', qc, kg).astype(f32) * scale
        s_sel = jnp.where(valid[..., None], s_sel, NEG_INF).reshape(B, QB, H, k_sel_n * BS)
        k_own = lax.dynamic_index_in_dim(kbh, cur, axis=2, keepdims=False)
        v_own = lax.dynamic_index_in_dim(vbh, cur, axis=2, keepdims=False)
        s_own = jnp.einsum('bqhd,bhsd->bqhs', qc, k_own).astype(f32) * scale
        key_pos = cur * BS + jnp.arange(BS)
        s_own = jnp.where((key_pos[None, :] <= q_pos[:, None])[None, :, None, :], s_own, NEG_INF)
        probs = jax.nn.softmax(jnp.concatenate([s_sel, s_own], axis=-1), axis=-1).astype(v.dtype)
        p_sel = probs[..., :k_sel_n * BS].reshape(B, QB, H, k_sel_n, BS)
        p_own = probs[..., k_sel_n * BS:]
        return (jnp.einsum('bqhjs,bqhjsd->bqhd', p_sel, vg)
                + jnp.einsum('bqhs,bhsd->bqhd', p_own, v_own))

    o = lax.map(attend, (q_chunks, jnp.arange(nq)))
    return o.transpose(1, 0, 2, 3, 4).reshape(B, T, H * hd)


def hier_moe(h, w_router_group, w_router_expert, w_gate, w_up, w_down):
    B, T, D = h.shape
    N = B * T
    f32 = jnp.float32
    hf = h.reshape(N, D)
    g_logits = (hf @ w_router_group).astype(f32)
    g_probs = jax.nn.softmax(g_logits, axis=-1)
    g_sel = jnp.argmax(g_logits, axis=-1)
    p_group = jnp.take_along_axis(g_probs, g_sel[:, None], axis=-1)
    e_logits = (hf @ w_router_expert).astype(f32).reshape(N, N_GROUPS, EXPERTS_PER_GROUP)
    e_logits = jnp.take_along_axis(e_logits, g_sel[:, None, None], axis=1)[:, 0]
    top_p, top_i = lax.top_k(jax.nn.softmax(e_logits, axis=-1), TOPK_IN_GROUP)
    weights = p_group * top_p / jnp.sum(top_p, axis=-1, keepdims=True)
    eid = g_sel[:, None] * EXPERTS_PER_GROUP + top_i

    R = MOE_ROW_BLOCK
    M = N * TOPK_IN_GROUP
    flat_e = eid.reshape(M).astype(jnp.int32)
    flat_tok = jnp.repeat(jnp.arange(N, dtype=jnp.int32), TOPK_IN_GROUP)
    flat_w = weights.reshape(M)
    order = jnp.argsort(flat_e)
    es, ts, ws = flat_e[order], flat_tok[order], flat_w[order]
    counts = jnp.bincount(flat_e, length=N_EXPERTS).astype(jnp.int32)
    padded = (counts + R - 1) // R * R
    start = jnp.cumsum(counts) - counts
    pad_end = jnp.cumsum(padded)
    pad_start = pad_end - padded
    dest = pad_start[es] + jnp.arange(M, dtype=jnp.int32) - start[es]
    n_blocks = -(-M // R) + N_EXPERTS
    m_pad = n_blocks * R
    tok_pad = jnp.zeros((m_pad,), jnp.int32).at[dest].set(ts)
    w_pad = jnp.zeros((m_pad,), f32).at[dest].set(ws)
    block_expert = jnp.minimum(
        jnp.searchsorted(pad_end, jnp.arange(n_blocks, dtype=jnp.int32) * R, side='right'),
        N_EXPERTS - 1)
    xb = hf[tok_pad].reshape(n_blocks, R, D)

    def expert_block(args):
        xr, e = args
        return (jax.nn.silu(xr @ w_gate[e]) * (xr @ w_up[e])) @ w_down[e]

    yb = lax.map(expert_block, (xb, block_expert)).reshape(m_pad, D)
    y = jnp.zeros((N, D), f32).at[tok_pad].add(yb.astype(f32) * w_pad[:, None])
    return y.reshape(B, T, D).astype(h.dtype)


def setup_inputs(seed: int = 0) -> dict:
    key = jax.random.key(seed)
    ks = jax.random.split(key, 20)
    f32 = jnp.float32
    nrm = lambda k, shape, s: jax.random.normal(k, shape, f32) * s
    gain = lambda k, shape: 1.0 + 0.02 * jax.random.normal(k, shape, f32)
    return {
        "x": nrm(ks[0], (BATCH, SEQ, D_MODEL), 1.0),
        "p": nrm(ks[1], (DEPTH, BATCH, SEQ, PLE_DIM), 1.0),
        "norm_mix_g": gain(ks[2], (DEPTH, D_MODEL)),
        "w_in": nrm(ks[3], (DEPTH, D_MODEL, IN_COLS), D_MODEL ** -0.5),
        "hgrn_lb_raw": nrm(ks[4], (DEPTH + 1, HGRN_WIDTH), 0.5),
        "hgrn_norm_g": gain(ks[5], (DEPTH, HGRN_HEAD_DIM)),
        "w_up_hgrn": nrm(ks[6], (DEPTH, HGRN_WIDTH, D_MODEL), HGRN_WIDTH ** -0.5),
        "w_up_moba": nrm(ks[7], (DEPTH, MOBA_WIDTH, D_MODEL), MOBA_WIDTH ** -0.5),
        "w_out": nrm(ks[8], (DEPTH, D_MODEL, D_MODEL), D_MODEL ** -0.5),
        "norm_ffn_g": gain(ks[9], (DEPTH, D_MODEL)),
        "w_router_group": nrm(ks[10], (DEPTH, D_MODEL, N_GROUPS), D_MODEL ** -0.5),
        "w_router_expert": nrm(ks[11], (DEPTH, D_MODEL, N_EXPERTS), D_MODEL ** -0.5),
        "w_exp_gate": nrm(ks[12], (DEPTH, N_EXPERTS, D_MODEL, D_EXPERT), D_MODEL ** -0.5),
        "w_exp_up": nrm(ks[13], (DEPTH, N_EXPERTS, D_MODEL, D_EXPERT), D_MODEL ** -0.5),
        "w_exp_down": nrm(ks[14], (DEPTH, N_EXPERTS, D_EXPERT, D_MODEL), D_EXPERT ** -0.5),
        "norm_ple_g": gain(ks[15], (DEPTH, D_MODEL)),
        "w_ple_gate": nrm(ks[16], (DEPTH, D_MODEL, D_MODEL), D_MODEL ** -0.5),
        "w_ple_proj": nrm(ks[17], (DEPTH, PLE_DIM, D_MODEL), PLE_DIM ** -0.5),
        "norm_final_g": gain(ks[18], (D_MODEL,)),
    }


def reference(x, p, norm_mix_g, w_in, hgrn_lb_raw, hgrn_norm_g, w_up_hgrn, w_up_moba, w_out,
              norm_ffn_g, w_router_group, w_router_expert, w_exp_gate, w_exp_up, w_exp_down,
              norm_ple_g, w_ple_gate, w_ple_proj, norm_final_g):
    B, T, D = x.shape
    positions = jnp.arange(T)
    lower_bounds = jnp.cumsum(jax.nn.softmax(hgrn_lb_raw.astype(jnp.float32), axis=0), axis=0)
    sizes = [HGRN_WIDTH] * 4 + [MOBA_WIDTH] * 3 + [D_MODEL] * 2
    offsets = []
    acc = 0
    for s in sizes[:-1]:
        acc += s
        offsets.append(acc)
    for i in range(DEPTH):
        h = rmsnorm(x, norm_mix_g[i])
        proj = h @ w_in[i]
        hq, hf_, hi_, hog, mq, mk, mv, gate_a, gate_b = jnp.split(proj, offsets, axis=-1)
        o_hgrn = hgrn2_mixer(hq, hf_, hi_, hog, lower_bounds[i], hgrn_norm_g[i])
        mq = rope(mq.reshape(B, T, MOBA_HEADS, MOBA_HEAD_DIM), positions)
        mk = rope(mk.reshape(B, T, MOBA_HEADS, MOBA_HEAD_DIM), positions)
        mv = mv.reshape(B, T, MOBA_HEADS, MOBA_HEAD_DIM)
        o_moba = moba_mixer(mq, mk, mv)
        merged = (jax.nn.sigmoid(gate_a) * (o_hgrn @ w_up_hgrn[i])
                  + jax.nn.sigmoid(gate_b) * (o_moba @ w_up_moba[i]))
        x = x + merged @ w_out[i]
        x = x + hier_moe(rmsnorm(x, norm_ffn_g[i]), w_router_group[i], w_router_expert[i],
                         w_exp_gate[i], w_exp_up[i], w_exp_down[i])
        hp = rmsnorm(x, norm_ple_g[i])
        x = x + jax.nn.sigmoid(hp @ w_ple_gate[i]) * (p[i] @ w_ple_proj[i])
    return rmsnorm(x, norm_final_g)
```

```python
import functools

import jax
import jax.numpy as jnp
from jax import lax
from jax.experimental import pallas as pl
from jax.experimental.pallas import tpu as pltpu

F32 = jnp.float32
BF16 = jnp.bfloat16
I32 = jnp.int32

D_MODEL = 2048
PLE_DIM = 256
HGRN_HEADS = 8
HGRN_HEAD_DIM = 128
HGRN_WIDTH = HGRN_HEADS * HGRN_HEAD_DIM
MOBA_HEADS = 8
MOBA_HEAD_DIM = 128
MOBA_WIDTH = MOBA_HEADS * MOBA_HEAD_DIM
MOBA_BLOCK = 256
MOBA_TOPK = 3
ROPE_THETA = 10000.0
N_GROUPS = 4
EXPERTS_PER_GROUP = 8
N_EXPERTS = N_GROUPS * EXPERTS_PER_GROUP
TOPK_IN_GROUP = 2
D_EXPERT = 512
EPS = 1e-6
NEG_INF = -1e30

V7X_LANES = 128
V7X_SUBLANES = 8
V7X_VMEM_BUDGET_BYTES = 56 * 1024 * 1024

HGRN_FOLD = 16
HGRN_CHUNK = V7X_SUBLANES * HGRN_FOLD
MOE_ROWS = 256


def _cparams(n_grid, vmem_bytes):
    return pltpu.CompilerParams(
        dimension_semantics=("arbitrary",) * n_grid,
        vmem_limit_bytes=int(min(max(vmem_bytes, 16 * 1024 * 1024), V7X_VMEM_BUDGET_BYTES)),
    )


def _nbytes(shape, dtype):
    n = 1
    for s in shape:
        n *= s
    return n * jnp.dtype(dtype).itemsize


def _rms(x, g):
    ms = jnp.mean(x * x, axis=-1, keepdims=True)
    return x * lax.rsqrt(ms + EPS) * g


def _rmsnorm_kernel(x_ref, g_ref, o_ref):
    o_ref[...] = _rms(x_ref[...], g_ref[...]).astype(o_ref.dtype)


def _rmsnorm(x, g, out_dtype, tm=512):
    T, D = x.shape
    return pl.pallas_call(
        _rmsnorm_kernel,
        out_shape=jax.ShapeDtypeStruct((T, D), out_dtype),
        grid=(T // tm,),
        in_specs=[pl.BlockSpec((tm, D), lambda i: (i, 0)), pl.BlockSpec((1, D), lambda i: (0, 0))],
        out_specs=pl.BlockSpec((tm, D), lambda i: (i, 0)),
        compiler_params=_cparams(1, 4 * _nbytes((tm, D), F32)),
        name="rmsnorm",
    )(x, g.reshape(1, D))


def _ep_identity(acc):
    return acc


def _ep_silu(acc):
    return acc * jax.nn.sigmoid(acc)


def _ep_sigmoid(acc):
    return jax.nn.sigmoid(acc)


def _ep_logf(acc, la_ref, lc_ref):
    ls = jnp.minimum(acc, 0.0) - jnp.log(1.0 + jnp.exp(-jnp.abs(acc)))
    u = la_ref[...]
    v = lc_ref[...] + ls
    return jnp.maximum(u, v) + jnp.log(1.0 + jnp.exp(-jnp.abs(u - v)))


def _ep_rope(acc, cos_ref, sin_ref, *, scale):
    cos = cos_ref[...]
    sin = sin_ref[...]
    outs = []
    for hh in range(acc.shape[1] // MOBA_HEAD_DIM):
        a = acc[:, hh * MOBA_HEAD_DIM:(hh + 1) * MOBA_HEAD_DIM]
        r = pltpu.roll(a, MOBA_HEAD_DIM // 2, axis=1)
        outs.append((a * cos + r * sin) * scale)
    return jnp.concatenate(outs, axis=1)


def _proj_kernel(h_ref, w_ref, *refs, epilogue):
    *extra, o_ref = refs
    acc = jnp.dot(h_ref[...], w_ref[...], preferred_element_type=F32)
    o_ref[...] = epilogue(acc, *extra).astype(o_ref.dtype)


def _proj(h, w, col0, ncols, epilogue, out_dtype, row_extras=(), col_extras=(), tm=1024, tn=1024):
    T, K = h.shape
    tn = min(tn, ncols)
    tm = min(tm, T)
    cb = col0 // tn
    in_specs = [
        pl.BlockSpec((tm, K), lambda j, i: (i, 0)),
        pl.BlockSpec((K, tn), lambda j, i: (0, cb + j)),
    ]
    for e in row_extras:
        in_specs.append(pl.BlockSpec((tm, e.shape[1]), lambda j, i: (i, 0)))
    for e in col_extras:
        in_specs.append(pl.BlockSpec((1, tn), lambda j, i: (0, j)))
    vmem = 2 * (_nbytes((tm, K), h.dtype) + _nbytes((K, tn), w.dtype) + _nbytes((tm, tn), out_dtype))
    vmem += 3 * _nbytes((tm, tn), F32)
    return pl.pallas_call(
        functools.partial(_proj_kernel, epilogue=epilogue),
        out_shape=jax.ShapeDtypeStruct((T, ncols), out_dtype),
        grid=(ncols // tn, T // tm),
        in_specs=in_specs,
        out_specs=pl.BlockSpec((tm, tn), lambda j, i: (i, j)),
        compiler_params=_cparams(2, vmem),
        name="proj",
    )(h, w, *row_extras, *col_extras)


def _hgrn_kernel(q_ref, g_ref, v_ref, og_ref, ng_ref, o_ref, st_ref):
    W, J, HD, NH = HGRN_WIDTH, HGRN_FOLD, HGRN_HEAD_DIM, HGRN_HEADS
    C = HGRN_CHUNK

    @pl.when(pl.program_id(0) == 0)
    def _():
        st_ref[...] = jnp.zeros_like(st_ref)

    a_idx = lax.broadcasted_iota(I32, (V7X_SUBLANES, W), 0)

    def cols(ref):
        return [ref[:, j * W:(j + 1) * W] for j in range(J)]

    def bcast_row(x, r):
        return jnp.broadcast_to(x[r:r + 1, :], x.shape)

    g = cols(g_ref)
    q = cols(q_ref)
    v = cols(v_ref)
    og = cols(og_ref)

    P = [g[0]]
    for j in range(1, J):
        P.append(P[-1] + g[j])
    R = P[J - 1]
    Y = R
    for s in (1, 2, 4):
        Y = Y + jnp.where(a_idx >= s, pltpu.roll(Y, s, axis=0), 0.0)
    X = Y - R
    bC = bcast_row(Y, V7X_SUBLANES - 1)

    fk = [1.0 - jnp.exp(g[j]) for j in range(J)]
    qe = [q[j] * jnp.exp(P[j] + X) for j in range(J)]
    ks = [fk[j] * jnp.exp(bC - (P[j] + X)) for j in range(J)]

    levels = [(0, q, fk)]
    for lvl, m in enumerate((1, 2, 4, 8), start=1):
        qr = [None] * J
        kr = [None] * J
        for j in range(J):
            jr = (j // (2 * m)) * (2 * m) + m - 1
            if (j % (2 * m)) >= m:
                qr[j] = q[j] * jnp.exp(P[j] - P[jr])
            else:
                kr[j] = fk[j] * jnp.exp(P[jr] - P[j])
        levels.append((lvl, qr, kr))
    y16 = jnp.where(a_idx % 2 == 0, Y, pltpu.roll(Y, 1, axis=0))
    y32 = jnp.where(a_idx < 4, bcast_row(Y, 1), bcast_row(Y, 5))
    y64 = bcast_row(Y, 3)
    big = ((y16, a_idx % 2 == 1), (y32, a_idx % 4 >= 2), (y64, a_idx >= 4))
    for lvl, (yr, upper) in enumerate(big, start=5):
        z = X - yr
        qr, kr = [], []
        for j in range(J):
            e = jnp.exp(-jnp.abs(P[j] + z))
            qr.append(jnp.where(upper, q[j] * e, 0.0))
            kr.append(jnp.where(upper, 0.0, fk[j] * e))
        levels.append((lvl, qr, kr))

    pr = lax.broadcasted_iota(I32, (C, C), 0)
    pc = lax.broadcasted_iota(I32, (C, C), 1)
    tr = HGRN_FOLD * (pr % V7X_SUBLANES) + pr // V7X_SUBLANES
    tc = HGRN_FOLD * (pc % V7X_SUBLANES) + pc // V7X_SUBLANES
    xr = tr ^ tc
    code = jnp.zeros((C, C), I32)
    for lvl in range(1, 8):
        code = jnp.where(xr >= (1 << (lvl - 1)), lvl, code)
    code = jnp.where(tc > tr, -1, code)

    zero_tile = jnp.zeros((V7X_SUBLANES, HD), F32)

    def mat(rows, h):
        return jnp.concatenate(
            [zero_tile if r is None else r[:, h * HD:(h + 1) * HD] for r in rows], axis=0)

    nt = (((1,), (1,)), ((), ()))
    tn = (((0,), (0,)), ((), ()))
    ebc = jnp.exp(bC)
    ng = ng_ref[...]
    for h in range(NH):
        a_mat = jnp.zeros((C, C), F32)
        for lvl, qr, kr in levels:
            s = lax.dot_general(mat(qr, h).astype(BF16), mat(kr, h).astype(BF16), nt,
                                preferred_element_type=F32)
            a_mat = jnp.where(code == lvl, s, a_mat)
        vh = mat(v, h).astype(BF16)
        st = st_ref[h]
        o = jnp.dot(a_mat.astype(BF16), vh, preferred_element_type=F32)
        o = o + lax.dot_general(mat(qe, h).astype(BF16), st.astype(BF16), nt, preferred_element_type=F32)
        o = _rms(o, ng) * mat(og, h)
        for j in range(J):
            o_ref[:, j * W + h * HD:j * W + (h + 1) * HD] = o[j * V7X_SUBLANES:(j + 1) * V7X_SUBLANES, :]
        st_ref[h] = st * ebc[0:1, h * HD:(h + 1) * HD] + lax.dot_general(
            vh, mat(ks, h).astype(BF16), tn, preferred_element_type=F32)


def _hgrn(q, logf, v, og, norm_g):
    T, W = q.shape
    rows = T // HGRN_FOLD
    fold = lambda a: a.reshape(rows, HGRN_FOLD * W)
    blk = pl.BlockSpec((V7X_SUBLANES, HGRN_FOLD * W), lambda c: (c, 0))
    vmem = 10 * _nbytes((V7X_SUBLANES, HGRN_FOLD * W), F32) + 40 * _nbytes((V7X_SUBLANES, HGRN_FOLD * W), F32)
    out = pl.pallas_call(
        _hgrn_kernel,
        out_shape=jax.ShapeDtypeStruct((rows, HGRN_FOLD * W), F32),
        grid=(T // HGRN_CHUNK,),
        in_specs=[blk, blk, blk, blk, pl.BlockSpec((1, HGRN_HEAD_DIM), lambda c: (0, 0))],
        out_specs=blk,
        scratch_shapes=[pltpu.VMEM((HGRN_HEADS, HGRN_HEAD_DIM, HGRN_HEAD_DIM), F32)],
        compiler_params=_cparams(1, vmem),
        name="hgrn",
    )(fold(q), fold(logf), fold(v), fold(og), norm_g.reshape(1, HGRN_HEAD_DIM))
    return out.reshape(T, W)


def _moba_kernel(q_ref, k_ref, v_ref, o_ref, km_ref):
    BS, HD = MOBA_BLOCK, MOBA_HEAD_DIM
    T = k_ref.shape[0]
    NB = T // BS
    cur = pl.program_id(1)
    nt = (((1,), (1,)), ((), ()))

    @pl.when(cur == 0)
    def _():
        kf = k_ref[...].astype(F32).reshape(NB, BS, HD)
        km_ref[...] = jnp.zeros_like(km_ref)
        km_ref[0:NB, :] = jnp.sum(kf, axis=1) * (1.0 / BS)

    q = q_ref[...]
    lane = lax.broadcasted_iota(I32, (BS, V7X_LANES), 1)
    gate = lax.dot_general(q, km_ref[...].astype(BF16), nt, preferred_element_type=F32)
    valid = lane < cur
    g = jnp.where(valid, gate, NEG_INF)
    sel = jnp.zeros((BS, V7X_LANES), jnp.bool_)
    for _ in range(MOBA_TOPK):
        mx = jnp.max(g, axis=1, keepdims=True)
        idx = jnp.min(jnp.where(g == mx, lane, V7X_LANES), axis=1, keepdims=True)
        pick = (lane == idx) & (mx > 0.5 * NEG_INF)
        sel = sel | pick
        g = jnp.where(pick, NEG_INF, g)
    pen = jnp.where(sel, 0.0, NEG_INF).astype(BF16)
    qc = jnp.concatenate([q, pen], axis=1)
    ones = jnp.ones((BS, HD), BF16)

    r0 = pl.multiple_of(cur * BS, BS)
    k_own = k_ref[pl.ds(r0, BS), :]
    v_own = v_ref[pl.ds(r0, BS), :]
    s = lax.dot_general(q, k_own, nt, preferred_element_type=F32)
    row = lax.broadcasted_iota(I32, (BS, BS), 0)
    col = lax.broadcasted_iota(I32, (BS, BS), 1)
    s = jnp.where(col <= row, s, NEG_INF)
    m0 = jnp.max(s, axis=1, keepdims=True)
    p = jnp.exp(s - m0).astype(BF16)
    acc0 = jnp.dot(p, jnp.concatenate([v_own, ones], axis=1), preferred_element_type=F32)

    def body(n, carry):
        m, acc = carry
        rn = pl.multiple_of(n * BS, BS)
        kb = k_ref[pl.ds(rn, BS), :]
        vb = v_ref[pl.ds(rn, BS), :]
        onehot = jnp.where(lane == n, 1.0, 0.0).astype(BF16)
        sn = lax.dot_general(qc, jnp.concatenate([kb, onehot], axis=1), nt, preferred_element_type=F32)
        m_new = jnp.maximum(m, jnp.max(sn, axis=1, keepdims=True))
        alpha = jnp.exp(m - m_new)
        pn = jnp.exp(sn - m_new).astype(BF16)
        acc = alpha * acc + jnp.dot(pn, jnp.concatenate([vb, ones], axis=1), preferred_element_type=F32)
        return m_new, acc

    _, acc = lax.fori_loop(0, cur, body, (m0, acc0))
    o_ref[...] = (acc[:, :HD] / acc[:, HD:]).astype(o_ref.dtype)


def _moba(mq, mk, mv):
    T = mq.shape[0]
    BS, HD = MOBA_BLOCK, MOBA_HEAD_DIM
    vmem = 4 * _nbytes((T, HD), BF16) + 4 * _nbytes((BS, HD), BF16) + 24 * _nbytes((BS, BS), F32)
    return pl.pallas_call(
        _moba_kernel,
        out_shape=jax.ShapeDtypeStruct((T, MOBA_WIDTH), BF16),
        grid=(MOBA_HEADS, T // BS),
        in_specs=[
            pl.BlockSpec((BS, HD), lambda h, i: (i, h)),
            pl.BlockSpec((T, HD), lambda h, i: (0, h)),
            pl.BlockSpec((T, HD), lambda h, i: (0, h)),
        ],
        out_specs=pl.BlockSpec((BS, HD), lambda h, i: (i, h)),
        scratch_shapes=[pltpu.VMEM((V7X_LANES, HD), F32)],
        compiler_params=_cparams(2, vmem),
        name="moba",
    )(mq, mk, mv)


def _merge_kernel(oh_ref, om_ref, ga_ref, gb_ref, wh_ref, wm_ref, o_ref):
    a = jnp.dot(oh_ref[...].astype(BF16), wh_ref[...], preferred_element_type=F32)
    b = jnp.dot(om_ref[...], wm_ref[...], preferred_element_type=F32)
    o_ref[...] = (ga_ref[...].astype(F32) * a + gb_ref[...].astype(F32) * b).astype(o_ref.dtype)


def _merge(o_hgrn, o_moba, gates, w_up_hgrn, w_up_moba, tm=512):
    T = o_hgrn.shape[0]
    D = D_MODEL
    vmem = 2 * (_nbytes((tm, HGRN_WIDTH), F32) + _nbytes((tm, MOBA_WIDTH), BF16) + 3 * _nbytes((tm, D), BF16)
                + 2 * _nbytes((HGRN_WIDTH, D), BF16)) + 3 * _nbytes((tm, D), F32)
    return pl.pallas_call(
        _merge_kernel,
        out_shape=jax.ShapeDtypeStruct((T, D), BF16),
        grid=(T // tm,),
        in_specs=[
            pl.BlockSpec((tm, HGRN_WIDTH), lambda i: (i, 0)),
            pl.BlockSpec((tm, MOBA_WIDTH), lambda i: (i, 0)),
            pl.BlockSpec((tm, D), lambda i: (i, 0)),
            pl.BlockSpec((tm, D), lambda i: (i, 1)),
            pl.BlockSpec((HGRN_WIDTH, D), lambda i: (0, 0)),
            pl.BlockSpec((MOBA_WIDTH, D), lambda i: (0, 0)),
        ],
        out_specs=pl.BlockSpec((tm, D), lambda i: (i, 0)),
        compiler_params=_cparams(1, vmem),
        name="merge",
    )(o_hgrn, o_moba, gates, gates, w_up_hgrn, w_up_moba)


def _outproj_kernel(m_ref, x_ref, w_ref, g_ref, x1_ref, h2_ref):
    x1 = x_ref[...] + jnp.dot(m_ref[...], w_ref[...], preferred_element_type=F32)
    x1_ref[...] = x1
    h2_ref[...] = _rms(x1, g_ref[...])


def _outproj(merged, x, w_out, g_ffn, tm=256):
    T, D = x.shape
    vmem = 2 * (_nbytes((tm, D), BF16) + 3 * _nbytes((tm, D), F32) + _nbytes((D, D), BF16)) + 2 * _nbytes((tm, D), F32)
    return pl.pallas_call(
        _outproj_kernel,
        out_shape=(jax.ShapeDtypeStruct((T, D), F32), jax.ShapeDtypeStruct((T, D), F32)),
        grid=(T // tm,),
        in_specs=[
            pl.BlockSpec((tm, D), lambda i: (i, 0)),
            pl.BlockSpec((tm, D), lambda i: (i, 0)),
            pl.BlockSpec((D, D), lambda i: (0, 0)),
            pl.BlockSpec((1, D), lambda i: (0, 0)),
        ],
        out_specs=(pl.BlockSpec((tm, D), lambda i: (i, 0)), pl.BlockSpec((tm, D), lambda i: (i, 0))),
        compiler_params=_cparams(1, vmem),
        name="outproj",
    )(merged, x, w_out, g_ffn.reshape(1, D))


def _router_kernel(h_ref, w_ref, info_ref, cnt_ref, carry_ref):
    tm = h_ref.shape[0]

    @pl.when(pl.program_id(0) == 0)
    def _():
        carry_ref[...] = jnp.zeros_like(carry_ref)

    logits = jnp.dot(h_ref[...].astype(BF16), w_ref[...], preferred_element_type=F32)
    lane = lax.broadcasted_iota(I32, (tm, V7X_LANES), 1)
    is_g = lane < N_GROUPS
    gl = jnp.where(is_g, logits, NEG_INF)
    gmax = jnp.max(gl, axis=1, keepdims=True)
    g_sel = jnp.min(jnp.where(gl == gmax, lane, V7X_LANES), axis=1, keepdims=True)
    gsum = jnp.sum(jnp.where(is_g, jnp.exp(gl - gmax), 0.0), axis=1, keepdims=True)
    p_group = 1.0 / gsum
    lo = N_GROUPS + EXPERTS_PER_GROUP * g_sel
    emask = (lane >= lo) & (lane < lo + EXPERTS_PER_GROUP)
    el = jnp.where(emask, logits, NEG_INF)
    e1 = jnp.max(el, axis=1, keepdims=True)
    i1 = jnp.min(jnp.where((el == e1) & emask, lane, V7X_LANES), axis=1, keepdims=True)
    emask2 = emask & (lane != i1)
    el2 = jnp.where(emask2, logits, NEG_INF)
    e2 = jnp.max(el2, axis=1, keepdims=True)
    i2 = jnp.min(jnp.where((el2 == e2) & emask2, lane, V7X_LANES), axis=1, keepdims=True)
    r = jnp.exp(e2 - e1)
    w1 = p_group / (1.0 + r)
    w2 = p_group * r / (1.0 + r)
    eid1 = i1 - N_GROUPS
    eid2 = i2 - N_GROUPS
    oh1 = jnp.where(lane == eid1, 1.0, 0.0)
    oh2 = jnp.where(lane == eid2, 1.0, 0.0)
    cnt = oh1 + oh2
    tri = jnp.where(lax.broadcasted_iota(I32, (tm, tm), 0) > lax.broadcasted_iota(I32, (tm, tm), 1), 1.0, 0.0)
    before = jnp.dot(tri.astype(BF16), cnt.astype(BF16), preferred_element_type=F32) + carry_ref[...]
    rank1 = jnp.sum(oh1 * before, axis=1, keepdims=True)
    rank2 = jnp.sum(oh2 * before, axis=1, keepdims=True)
    carry_ref[...] = carry_ref[...] + jnp.sum(cnt, axis=0, keepdims=True)
    info = jnp.zeros((tm, V7X_LANES), F32)
    for k, val in enumerate((eid1.astype(F32), eid2.astype(F32), w1, w2, rank1, rank2)):
        info = jnp.where(lane == k, val, info)
    info_ref[...] = info
    cnt_ref[...] = carry_ref[...]


def _router(h2, w_router, tm=256):
    T, D = h2.shape
    vmem = 2 * (_nbytes((tm, D), F32) + _nbytes((D, V7X_LANES), BF16)) + 16 * _nbytes((tm, V7X_LANES), F32) + (1 << 22)
    return pl.pallas_call(
        _router_kernel,
        out_shape=(jax.ShapeDtypeStruct((T, V7X_LANES), F32), jax.ShapeDtypeStruct((1, V7X_LANES), F32)),
        grid=(T // tm,),
        in_specs=[pl.BlockSpec((tm, D), lambda i: (i, 0)), pl.BlockSpec((D, V7X_LANES), lambda i: (0, 0))],
        out_specs=(pl.BlockSpec((tm, V7X_LANES), lambda i: (i, 0)), pl.BlockSpec((1, V7X_LANES), lambda i: (0, 0))),
        scratch_shapes=[pltpu.VMEM((1, V7X_LANES), F32)],
        compiler_params=_cparams(1, vmem),
        name="router",
    )(h2, w_router)


def _row_copy(src_ref, src_row, dst_ref, dst_row, sem):
    return pltpu.make_async_copy(src_ref.at[pl.ds(src_row, 1), :], dst_ref.at[pl.ds(dst_row, 1), :], sem)


def _dispatch_kernel(dest_ref, h_ref, xs_in_ref, xs_ref, sem):
    del xs_in_ref
    tm = h_ref.shape[0]
    base = pl.program_id(0) * tm

    def issue(r, c):
        for k in range(TOPK_IN_GROUP):
            _row_copy(h_ref, r, xs_ref, dest_ref[(base + r) * TOPK_IN_GROUP + k], sem).start()
        return c

    def drain(r, c):
        for k in range(TOPK_IN_GROUP):
            _row_copy(h_ref, 0, xs_ref, 0, sem).wait()
        return c

    lax.fori_loop(0, tm, issue, 0)
    lax.fori_loop(0, tm, drain, 0)


def _dispatch(dest_flat, h2, m_pad, tm=256):
    T, D = h2.shape
    xs0 = jnp.zeros((m_pad, D), F32)
    return pl.pallas_call(
        _dispatch_kernel,
        out_shape=jax.ShapeDtypeStruct((m_pad, D), F32),
        grid_spec=pltpu.PrefetchScalarGridSpec(
            num_scalar_prefetch=1,
            grid=(T // tm,),
            in_specs=[pl.BlockSpec((tm, D), lambda i, d: (i, 0)), pl.BlockSpec(memory_space=pl.ANY)],
            out_specs=pl.BlockSpec(memory_space=pl.ANY),
            scratch_shapes=[pltpu.SemaphoreType.DMA(())],
        ),
        input_output_aliases={2: 0},
        compiler_params=_cparams(1, 4 * _nbytes((tm, D), F32)),
        name="dispatch",
    )(dest_flat, h2, xs0)


def _expert_kernel(be_ref, nu_ref, xs_ref, wg_ref, wu_ref, wd_ref, y_ref):
    del be_ref
    b = pl.program_id(0)

    @pl.when(b < nu_ref[0])
    def _():
        x = xs_ref[...].astype(BF16)
        a = jnp.dot(x, wg_ref[0], preferred_element_type=F32)
        u = jnp.dot(x, wu_ref[0], preferred_element_type=F32)
        hm = (a * jax.nn.sigmoid(a) * u).astype(BF16)
        y_ref[...] = jnp.dot(hm, wd_ref[0], preferred_element_type=F32)

    @pl.when(b >= nu_ref[0])
    def _():
        y_ref[...] = jnp.zeros_like(y_ref)


def _experts(block_expert, n_used, xs, w_gate, w_up, w_down):
    m_pad, D = xs.shape
    R, Fd = MOE_ROWS, D_EXPERT
    vmem = 2 * (2 * _nbytes((R, D), F32) + 3 * _nbytes((D, Fd), BF16)) + 4 * _nbytes((R, D), F32)
    return pl.pallas_call(
        _expert_kernel,
        out_shape=jax.ShapeDtypeStruct((m_pad, D), F32),
        grid_spec=pltpu.PrefetchScalarGridSpec(
            num_scalar_prefetch=2,
            grid=(m_pad // R,),
            in_specs=[
                pl.BlockSpec((R, D), lambda b, be, nu: (jnp.minimum(b, jnp.maximum(nu[0] - 1, 0)), 0)),
                pl.BlockSpec((1, D, Fd), lambda b, be, nu: (be[b], 0, 0)),
                pl.BlockSpec((1, D, Fd), lambda b, be, nu: (be[b], 0, 0)),
                pl.BlockSpec((1, Fd, D), lambda b, be, nu: (be[b], 0, 0)),
            ],
            out_specs=pl.BlockSpec((R, D), lambda b, be, nu: (b, 0)),
        ),
        compiler_params=_cparams(1, vmem),
        name="experts",
    )(block_expert, n_used, xs, w_gate, w_up, w_down)


def _combine_kernel(dest_ref, x1_ref, info_ref, yb_ref, p_ref, gp_ref, wpg_ref, wpp_ref, gf_ref, o_ref, ybuf, sem):
    tm = x1_ref.shape[0]
    base = pl.program_id(0) * tm

    def issue(r, c):
        for k in range(TOPK_IN_GROUP):
            _row_copy(yb_ref, dest_ref[(base + r) * TOPK_IN_GROUP + k], ybuf.at[k], r, sem).start()
        return c

    def drain(r, c):
        for k in range(TOPK_IN_GROUP):
            _row_copy(yb_ref, 0, ybuf.at[k], 0, sem).wait()
        return c

    lax.fori_loop(0, tm, issue, 0)
    lax.fori_loop(0, tm, drain, 0)
    info = info_ref[...]
    x2 = x1_ref[...] + info[:, 2:3] * ybuf[0] + info[:, 3:4] * ybuf[1]
    hp = _rms(x2, gp_ref[...]).astype(BF16)
    z = jnp.dot(hp, wpg_ref[...], preferred_element_type=F32)
    pp = jnp.dot(p_ref[...].astype(BF16), wpp_ref[...], preferred_element_type=F32)
    x3 = x2 + jax.nn.sigmoid(z) * pp
    o_ref[...] = _rms(x3, gf_ref[...])


def _combine(dest_flat, x1, info, yb, p, g_ple, w_ple_gate, w_ple_proj, g_final, tm=256):
    T, D = x1.shape
    vmem = 2 * (2 * _nbytes((tm, D), F32) + _nbytes((D, D), BF16) + _nbytes((PLE_DIM, D), BF16)
                + _nbytes((tm, PLE_DIM), F32)) + 8 * _nbytes((tm, D), F32)
    return pl.pallas_call(
        _combine_kernel,
        out_shape=jax.ShapeDtypeStruct((T, D), F32),
        grid_spec=pltpu.PrefetchScalarGridSpec(
            num_scalar_prefetch=1,
            grid=(T // tm,),
            in_specs=[
                pl.BlockSpec((tm, D), lambda i, d: (i, 0)),
                pl.BlockSpec((tm, V7X_LANES), lambda i, d: (i, 0)),
                pl.BlockSpec(memory_space=pl.ANY),
                pl.BlockSpec((tm, PLE_DIM), lambda i, d: (i, 0)),
                pl.BlockSpec((1, D), lambda i, d: (0, 0)),
                pl.BlockSpec((D, D), lambda i, d: (0, 0)),
                pl.BlockSpec((PLE_DIM, D), lambda i, d: (0, 0)),
                pl.BlockSpec((1, D), lambda i, d: (0, 0)),
            ],
            out_specs=pl.BlockSpec((tm, D), lambda i, d: (i, 0)),
            scratch_shapes=[pltpu.VMEM((TOPK_IN_GROUP, tm, D), F32), pltpu.SemaphoreType.DMA(())],
        ),
        compiler_params=_cparams(1, vmem),
        name="combine",
    )(dest_flat, x1, info, yb, p, g_ple.reshape(1, D), w_ple_gate, w_ple_proj, g_final.reshape(1, D))


def _rope_tables(T):
    half = MOBA_HEAD_DIM // 2
    inv_freq = ROPE_THETA ** (-jnp.arange(half, dtype=F32) / half)
    ang = jnp.arange(T, dtype=F32)[:, None] * inv_freq[None, :]
    cos, sin = jnp.cos(ang), jnp.sin(ang)
    return jnp.concatenate([cos, cos], axis=1), jnp.concatenate([-sin, sin], axis=1)


def _mixers(x2d, g_mix, w_in, lb, hgrn_norm_g):
    T = x2d.shape[0]
    W = HGRN_WIDTH
    h = _rmsnorm(x2d, g_mix, BF16)
    w = w_in.astype(BF16)
    cos, sin = _rope_tables(T)
    log_lb = jnp.log(lb).reshape(1, W)
    log_1m = jnp.log1p(-lb).reshape(1, W)
    hq = _proj(h, w, 0 * W, W, _ep_silu, F32)
    logf = _proj(h, w, 1 * W, W, _ep_logf, F32, col_extras=(log_lb, log_1m))
    hi = _proj(h, w, 2 * W, W, _ep_identity, F32)
    hog = _proj(h, w, 3 * W, W, _ep_silu, F32)
    scale = MOBA_HEAD_DIM ** -0.5
    mq = _proj(h, w, 4 * W, W, functools.partial(_ep_rope, scale=scale), BF16, row_extras=(cos, sin))
    mk = _proj(h, w, 5 * W, W, functools.partial(_ep_rope, scale=1.0), BF16, row_extras=(cos, sin))
    mv = _proj(h, w, 6 * W, W, _ep_identity, BF16)
    gates = _proj(h, w, 7 * W, 2 * D_MODEL, _ep_sigmoid, BF16)
    o_hgrn = _hgrn(hq, logf, hi, hog, hgrn_norm_g)
    o_moba = _moba(mq, mk, mv)
    return o_hgrn, o_moba, gates


def _moe_plan(info, cnt, T):
    R = MOE_ROWS
    eid = info[:, 0:TOPK_IN_GROUP].astype(I32)
    rank = info[:, 4:4 + TOPK_IN_GROUP].astype(I32)
    counts = cnt[0, :N_EXPERTS].astype(I32)
    padded = (counts + R - 1) // R * R
    pad_end = jnp.cumsum(padded)
    pad_start = pad_end - padded
    dest = (pad_start[eid] + rank).reshape(-1)
    n_blocks = (T * TOPK_IN_GROUP) // R + N_EXPERTS
    block_expert = jnp.minimum(
        jnp.searchsorted(pad_end, jnp.arange(n_blocks, dtype=I32) * R, side="right"), N_EXPERTS - 1).astype(I32)
    n_used = (pad_end[-1:] // R).astype(I32)
    return dest, block_expert, n_used, n_blocks * R


def kernel(x, p, norm_mix_g, w_in, hgrn_lb_raw, hgrn_norm_g, w_up_hgrn, w_up_moba, w_out, norm_ffn_g,
           w_router_group, w_router_expert, w_exp_gate, w_exp_up, w_exp_down, norm_ple_g, w_ple_gate,
           w_ple_proj, norm_final_g):
    B, T, D = x.shape
    assert B == 1 and D == D_MODEL and w_in.shape[0] == 1 and T % (4 * MOBA_BLOCK) == 0
    lower_bounds = jnp.cumsum(jax.nn.softmax(hgrn_lb_raw.astype(F32), axis=0), axis=0)
    x2d = x.reshape(T, D)
    o_hgrn, o_moba, gates = _mixers(x2d, norm_mix_g[0], w_in[0], lower_bounds[0], hgrn_norm_g[0])
    merged = _merge(o_hgrn, o_moba, gates, w_up_hgrn[0].astype(BF16), w_up_moba[0].astype(BF16))
    x1, h2 = _outproj(merged, x2d, w_out[0].astype(BF16), norm_ffn_g[0])
    w_router = jnp.pad(jnp.concatenate([w_router_group[0], w_router_expert[0]], axis=1),
                       ((0, 0), (0, V7X_LANES - N_GROUPS - N_EXPERTS))).astype(BF16)
    info, cnt = _router(h2, w_router)
    dest, block_expert, n_used, m_pad = _moe_plan(info, cnt, T)
    xs = _dispatch(dest, h2, m_pad)
    yb = _experts(block_expert, n_used, xs, w_exp_gate[0].astype(BF16), w_exp_up[0].astype(BF16),
                  w_exp_down[0].astype(BF16))
    out = _combine(dest, x1, info, yb, p[0].reshape(T, PLE_DIM), norm_ple_g[0], w_ple_gate[0].astype(BF16),
                   w_ple_proj[0].astype(BF16), norm_final_g)
    return out.reshape(B, T, D)
```

```python
import functools

import jax
import jax.numpy as jnp
from jax import lax
from jax.experimental import pallas as pl
from jax.experimental.pallas import tpu as pltpu

F32 = jnp.float32
BF16 = jnp.bfloat16
I32 = jnp.int32

D_MODEL = 2048
PLE_DIM = 256
HGRN_HEADS = 8
HGRN_HEAD_DIM = 128
HGRN_WIDTH = HGRN_HEADS * HGRN_HEAD_DIM
MOBA_HEADS = 8
MOBA_HEAD_DIM = 128
MOBA_WIDTH = MOBA_HEADS * MOBA_HEAD_DIM
MOBA_BLOCK = 256
MOBA_TOPK = 3
ROPE_THETA = 10000.0
N_GROUPS = 4
EXPERTS_PER_GROUP = 8
N_EXPERTS = N_GROUPS * EXPERTS_PER_GROUP
TOPK_IN_GROUP = 2
D_EXPERT = 512
EPS = 1e-6
NEG_INF = -1e30

V7X_LANES = 128
V7X_SUBLANES = 8
V7X_VMEM_BUDGET_BYTES = 56 * 1024 * 1024

HGRN_FOLD = 16
HGRN_CHUNK = V7X_SUBLANES * HGRN_FOLD
MOE_ROWS = 256


def _cparams(n_grid, vmem_bytes):
    return pltpu.CompilerParams(
        dimension_semantics=("arbitrary",) * n_grid,
        vmem_limit_bytes=int(min(max(vmem_bytes, 16 * 1024 * 1024), V7X_VMEM_BUDGET_BYTES)),
    )


def _nbytes(shape, dtype):
    n = 1
    for s in shape:
        n *= s
    return n * jnp.dtype(dtype).itemsize


def _rms(x, g):
    ms = jnp.mean(x * x, axis=-1, keepdims=True)
    return x * lax.rsqrt(ms + EPS) * g


def _rmsnorm_kernel(x_ref, g_ref, o_ref):
    o_ref[...] = _rms(x_ref[...], g_ref[...]).astype(o_ref.dtype)


def _rmsnorm(x, g, out_dtype, tm=512):
    T, D = x.shape
    return pl.pallas_call(
        _rmsnorm_kernel,
        out_shape=jax.ShapeDtypeStruct((T, D), out_dtype),
        grid=(T // tm,),
        in_specs=[pl.BlockSpec((tm, D), lambda i: (i, 0)), pl.BlockSpec((1, D), lambda i: (0, 0))],
        out_specs=pl.BlockSpec((tm, D), lambda i: (i, 0)),
        compiler_params=_cparams(1, 4 * _nbytes((tm, D), F32)),
        name="rmsnorm",
    )(x, g.reshape(1, D))


def _ep_identity(acc):
    return acc


def _ep_silu(acc):
    return acc * jax.nn.sigmoid(acc)


def _ep_sigmoid(acc):
    return jax.nn.sigmoid(acc)


def _ep_logf(acc, la_ref, lc_ref):
    ls = jnp.minimum(acc, 0.0) - jnp.log(1.0 + jnp.exp(-jnp.abs(acc)))
    u = la_ref[...]
    v = lc_ref[...] + ls
    return jnp.maximum(u, v) + jnp.log(1.0 + jnp.exp(-jnp.abs(u - v)))


def _ep_rope(acc, cos_ref, sin_ref, *, scale):
    cos = cos_ref[...]
    sin = sin_ref[...]
    outs = []
    for hh in range(acc.shape[1] // MOBA_HEAD_DIM):
        a = acc[:, hh * MOBA_HEAD_DIM:(hh + 1) * MOBA_HEAD_DIM]
        r = pltpu.roll(a, MOBA_HEAD_DIM // 2, axis=1)
        outs.append((a * cos + r * sin) * scale)
    return jnp.concatenate(outs, axis=1)


def _ep_rope_aug(acc, cos_ref, sin_ref, oh_ref):
    cos = cos_ref[...]
    sin = sin_ref[...]
    oh = oh_ref[...]
    outs = []
    for hh in range(acc.shape[1] // MOBA_HEAD_DIM):
        a = acc[:, hh * MOBA_HEAD_DIM:(hh + 1) * MOBA_HEAD_DIM]
        outs.append(a * cos + pltpu.roll(a, MOBA_HEAD_DIM // 2, axis=1) * sin)
        outs.append(oh)
    return jnp.concatenate(outs, axis=1)


def _proj_kernel(h_ref, w_ref, *refs, epilogue):
    *extra, o_ref = refs
    acc = jnp.dot(h_ref[...], w_ref[...], preferred_element_type=F32)
    o_ref[...] = epilogue(acc, *extra).astype(o_ref.dtype)


def _proj(h, w, col0, ncols, epilogue, out_dtype, row_extras=(), col_extras=(), tm=1024, tn=1024, widen=1):
    T, K = h.shape
    tn = min(tn, ncols)
    tm = min(tm, T)
    cb = col0 // tn
    otn = widen * tn
    in_specs = [
        pl.BlockSpec((tm, K), lambda j, i: (i, 0)),
        pl.BlockSpec((K, tn), lambda j, i: (0, cb + j)),
    ]
    for e in row_extras:
        in_specs.append(pl.BlockSpec((tm, e.shape[1]), lambda j, i: (i, 0)))
    for e in col_extras:
        in_specs.append(pl.BlockSpec((1, tn), lambda j, i: (0, j)))
    vmem = 2 * (_nbytes((tm, K), h.dtype) + _nbytes((K, tn), w.dtype) + _nbytes((tm, otn), out_dtype))
    vmem += 3 * _nbytes((tm, otn), F32)
    return pl.pallas_call(
        functools.partial(_proj_kernel, epilogue=epilogue),
        out_shape=jax.ShapeDtypeStruct((T, widen * ncols), out_dtype),
        grid=(ncols // tn, T // tm),
        in_specs=in_specs,
        out_specs=pl.BlockSpec((tm, otn), lambda j, i: (i, j)),
        compiler_params=_cparams(2, vmem),
        name="proj",
    )(h, w, *row_extras, *col_extras)


def _hgrn_kernel(q_ref, g_ref, v_ref, og_ref, ng_ref, o_ref, st_ref):
    W, J, HD, NH = HGRN_WIDTH, HGRN_FOLD, HGRN_HEAD_DIM, HGRN_HEADS
    C = HGRN_CHUNK

    @pl.when(pl.program_id(0) == 0)
    def _():
        st_ref[...] = jnp.zeros_like(st_ref)

    a_idx = lax.broadcasted_iota(I32, (V7X_SUBLANES, W), 0)

    def cols(ref):
        return [ref[:, j * W:(j + 1) * W] for j in range(J)]

    def bcast_row(x, r):
        return jnp.broadcast_to(x[r:r + 1, :], x.shape)

    g = cols(g_ref)
    q = cols(q_ref)
    v = cols(v_ref)
    og = cols(og_ref)

    P = [g[0]]
    for j in range(1, J):
        P.append(P[-1] + g[j])
    R = P[J - 1]
    Y = R
    for s in (1, 2, 4):
        Y = Y + jnp.where(a_idx >= s, pltpu.roll(Y, s, axis=0), 0.0)
    X = Y - R
    bC = bcast_row(Y, V7X_SUBLANES - 1)

    fk = [1.0 - jnp.exp(g[j]) for j in range(J)]
    qe = [q[j] * jnp.exp(P[j] + X) for j in range(J)]
    ks = [fk[j] * jnp.exp(bC - (P[j] + X)) for j in range(J)]

    levels = [(0, q, fk)]
    for lvl, m in enumerate((1, 2, 4, 8), start=1):
        qr = [None] * J
        kr = [None] * J
        for j in range(J):
            jr = (j // (2 * m)) * (2 * m) + m - 1
            if (j % (2 * m)) >= m:
                qr[j] = q[j] * jnp.exp(P[j] - P[jr])
            else:
                kr[j] = fk[j] * jnp.exp(P[jr] - P[j])
        levels.append((lvl, qr, kr))
    y16 = jnp.where(a_idx % 2 == 0, Y, pltpu.roll(Y, 1, axis=0))
    y32 = jnp.where(a_idx < 4, bcast_row(Y, 1), bcast_row(Y, 5))
    y64 = bcast_row(Y, 3)
    big = ((y16, a_idx % 2 == 1), (y32, a_idx % 4 >= 2), (y64, a_idx >= 4))
    for lvl, (yr, upper) in enumerate(big, start=5):
        z = X - yr
        qr, kr = [], []
        for j in range(J):
            e = jnp.exp(-jnp.abs(P[j] + z))
            qr.append(jnp.where(upper, q[j] * e, 0.0))
            kr.append(jnp.where(upper, 0.0, fk[j] * e))
        levels.append((lvl, qr, kr))

    pr = lax.broadcasted_iota(I32, (C, C), 0)
    pc = lax.broadcasted_iota(I32, (C, C), 1)
    tr = HGRN_FOLD * (pr % V7X_SUBLANES) + pr // V7X_SUBLANES
    tc = HGRN_FOLD * (pc % V7X_SUBLANES) + pc // V7X_SUBLANES
    xr = tr ^ tc
    code = jnp.zeros((C, C), I32)
    for lvl in range(1, 8):
        code = jnp.where(xr >= (1 << (lvl - 1)), lvl, code)
    code = jnp.where(tc > tr, -1, code)

    zero_tile = jnp.zeros((V7X_SUBLANES, HD), F32)

    def mat(rows, h):
        return jnp.concatenate(
            [zero_tile if r is None else r[:, h * HD:(h + 1) * HD] for r in rows], axis=0)

    nt = (((1,), (1,)), ((), ()))
    tn = (((0,), (0,)), ((), ()))
    ebc = jnp.exp(bC)
    ng = ng_ref[...]
    for h in range(NH):
        a_mat = jnp.zeros((C, C), F32)
        for lvl, qr, kr in levels:
            s = lax.dot_general(mat(qr, h).astype(BF16), mat(kr, h).astype(BF16), nt,
                                preferred_element_type=F32)
            a_mat = jnp.where(code == lvl, s, a_mat)
        vh = mat(v, h).astype(BF16)
        st = st_ref[h]
        o = jnp.dot(a_mat.astype(BF16), vh, preferred_element_type=F32)
        o = o + lax.dot_general(mat(qe, h).astype(BF16), st.astype(BF16), nt, preferred_element_type=F32)
        o = _rms(o, ng) * mat(og, h)
        for j in range(J):
            o_ref[:, j * W + h * HD:j * W + (h + 1) * HD] = o[j * V7X_SUBLANES:(j + 1) * V7X_SUBLANES, :]
        st_ref[h] = st * ebc[0:1, h * HD:(h + 1) * HD] + lax.dot_general(
            vh, mat(ks, h).astype(BF16), tn, preferred_element_type=F32)


def _hgrn(q, logf, v, og, norm_g):
    T, W = q.shape
    rows = T // HGRN_FOLD
    fold = lambda a: a.reshape(rows, HGRN_FOLD * W)
    blk = pl.BlockSpec((V7X_SUBLANES, HGRN_FOLD * W), lambda c: (c, 0))
    vmem = 10 * _nbytes((V7X_SUBLANES, HGRN_FOLD * W), F32) + 40 * _nbytes((V7X_SUBLANES, HGRN_FOLD * W), F32)
    out = pl.pallas_call(
        _hgrn_kernel,
        out_shape=jax.ShapeDtypeStruct((rows, HGRN_FOLD * W), F32),
        grid=(T // HGRN_CHUNK,),
        in_specs=[blk, blk, blk, blk, pl.BlockSpec((1, HGRN_HEAD_DIM), lambda c: (0, 0))],
        out_specs=blk,
        scratch_shapes=[pltpu.VMEM((HGRN_HEADS, HGRN_HEAD_DIM, HGRN_HEAD_DIM), F32)],
        compiler_params=_cparams(1, vmem),
        name="hgrn",
    )(fold(q), fold(logf), fold(v), fold(og), norm_g.reshape(1, HGRN_HEAD_DIM))
    return out.reshape(T, W)


def _moba_kernel(q_ref, k_ref, v_ref, o_ref, km_ref):
    BS, HD = MOBA_BLOCK, MOBA_HEAD_DIM
    T = k_ref.shape[0]
    NB = T // BS
    G = q_ref.shape[1] // HD
    cur = pl.program_id(1)
    nt = (((1,), (1,)), ((), ()))

    @pl.when(cur == 0)
    def _():
        km_ref[...] = jnp.zeros_like(km_ref)
        for g in range(G):
            kf = k_ref[:, 2 * g * HD:(2 * g + 1) * HD].astype(F32).reshape(NB, BS, HD)
            km_ref[g, 0:NB, :] = jnp.sum(kf, axis=1) * (1.0 / BS)

    lane = lax.broadcasted_iota(I32, (BS, V7X_LANES), 1)
    row = lax.broadcasted_iota(I32, (BS, BS), 0)
    col = lax.broadcasted_iota(I32, (BS, BS), 1)
    ones = jnp.ones((BS, HD), BF16)
    r0 = pl.multiple_of(cur * BS, BS)
    qcs, ms, accs = [], [], []
    for g in range(G):
        q = q_ref[:, g * HD:(g + 1) * HD]
        gate = lax.dot_general(q, km_ref[g].astype(BF16), nt, preferred_element_type=F32)
        gt = jnp.where(lane < cur, gate, NEG_INF)
        sel = jnp.zeros((BS, V7X_LANES), jnp.bool_)
        for _ in range(MOBA_TOPK):
            mx = jnp.max(gt, axis=1, keepdims=True)
            idx = jnp.min(jnp.where(gt == mx, lane, V7X_LANES), axis=1, keepdims=True)
            pick = (lane == idx) & (mx > 0.5 * NEG_INF)
            sel = sel | pick
            gt = jnp.where(pick, NEG_INF, gt)
        pen = jnp.where(sel, 0.0, NEG_INF).astype(BF16)
        qcs.append(jnp.concatenate([q, pen], axis=1))
        k_own = k_ref[pl.ds(r0, BS), 2 * g * HD:(2 * g + 1) * HD]
        v_own = v_ref[pl.ds(r0, BS), g * HD:(g + 1) * HD]
        s = lax.dot_general(q, k_own, nt, preferred_element_type=F32)
        s = jnp.where(col <= row, s, NEG_INF)
        m0 = jnp.max(s, axis=1, keepdims=True)
        p = jnp.exp(s - m0).astype(BF16)
        ms.append(m0)
        accs.append(jnp.dot(p, jnp.concatenate([v_own, ones], axis=1), preferred_element_type=F32))

    def body(n, carry):
        ms, accs = carry
        rn = pl.multiple_of(n * BS, BS)
        new_ms, new_accs = [], []
        for g in range(G):
            kc = k_ref[pl.ds(rn, BS), 2 * g * HD:(2 * g + 2) * HD]
            vb = v_ref[pl.ds(rn, BS), g * HD:(g + 1) * HD]
            sn = lax.dot_general(qcs[g], kc, nt, preferred_element_type=F32)
            m_new = jnp.maximum(ms[g], jnp.max(sn, axis=1, keepdims=True))
            alpha = jnp.exp(ms[g] - m_new)
            pn = jnp.exp(sn - m_new).astype(BF16)
            new_ms.append(m_new)
            new_accs.append(alpha * accs[g] + jnp.dot(pn, jnp.concatenate([vb, ones], axis=1),
                                                      preferred_element_type=F32))
        return tuple(new_ms), tuple(new_accs)

    _, accs = lax.fori_loop(0, cur, body, (tuple(ms), tuple(accs)))
    for g in range(G):
        o_ref[:, g * HD:(g + 1) * HD] = (accs[g][:, :HD] / accs[g][:, HD:]).astype(o_ref.dtype)


def _moba(mq, mk_aug, mv, heads_per_step=4):
    T = mq.shape[0]
    BS, HD, G = MOBA_BLOCK, MOBA_HEAD_DIM, heads_per_step
    vmem = _nbytes((T, 3 * G * HD), BF16) + 8 * _nbytes((BS, G * HD), BF16) + 16 * G * _nbytes((BS, BS), F32)
    resident = pl.Buffered(1)
    return pl.pallas_call(
        _moba_kernel,
        out_shape=jax.ShapeDtypeStruct((T, MOBA_WIDTH), BF16),
        grid=(MOBA_HEADS // G, T // BS),
        in_specs=[
            pl.BlockSpec((BS, G * HD), lambda h, i: (i, h)),
            pl.BlockSpec((T, 2 * G * HD), lambda h, i: (0, h), pipeline_mode=resident),
            pl.BlockSpec((T, G * HD), lambda h, i: (0, h), pipeline_mode=resident),
        ],
        out_specs=pl.BlockSpec((BS, G * HD), lambda h, i: (i, h)),
        scratch_shapes=[pltpu.VMEM((G, V7X_LANES, HD), F32)],
        compiler_params=_cparams(2, vmem),
        name="moba",
    )(mq, mk_aug, mv)


def _merge_kernel(oh_ref, om_ref, ga_ref, gb_ref, wh_ref, wm_ref, o_ref):
    a = jnp.dot(oh_ref[...].astype(BF16), wh_ref[...], preferred_element_type=F32)
    b = jnp.dot(om_ref[...], wm_ref[...], preferred_element_type=F32)
    o_ref[...] = (ga_ref[...].astype(F32) * a + gb_ref[...].astype(F32) * b).astype(o_ref.dtype)


def _merge(o_hgrn, o_moba, gates, w_up_hgrn, w_up_moba, tm=512):
    T = o_hgrn.shape[0]
    D = D_MODEL
    vmem = 2 * (_nbytes((tm, HGRN_WIDTH), F32) + _nbytes((tm, MOBA_WIDTH), BF16) + 3 * _nbytes((tm, D), BF16)
                + 2 * _nbytes((HGRN_WIDTH, D), BF16)) + 3 * _nbytes((tm, D), F32)
    return pl.pallas_call(
        _merge_kernel,
        out_shape=jax.ShapeDtypeStruct((T, D), BF16),
        grid=(T // tm,),
        in_specs=[
            pl.BlockSpec((tm, HGRN_WIDTH), lambda i: (i, 0)),
            pl.BlockSpec((tm, MOBA_WIDTH), lambda i: (i, 0)),
            pl.BlockSpec((tm, D), lambda i: (i, 0)),
            pl.BlockSpec((tm, D), lambda i: (i, 1)),
            pl.BlockSpec((HGRN_WIDTH, D), lambda i: (0, 0)),
            pl.BlockSpec((MOBA_WIDTH, D), lambda i: (0, 0)),
        ],
        out_specs=pl.BlockSpec((tm, D), lambda i: (i, 0)),
        compiler_params=_cparams(1, vmem),
        name="merge",
    )(o_hgrn, o_moba, gates, gates, w_up_hgrn, w_up_moba)


def _outproj_kernel(m_ref, x_ref, w_ref, g_ref, x1_ref, h2_ref):
    x1 = x_ref[...] + jnp.dot(m_ref[...], w_ref[...], preferred_element_type=F32)
    x1_ref[...] = x1
    h2_ref[...] = _rms(x1, g_ref[...])


def _outproj(merged, x, w_out, g_ffn, tm=256):
    T, D = x.shape
    vmem = 2 * (_nbytes((tm, D), BF16) + 3 * _nbytes((tm, D), F32) + _nbytes((D, D), BF16)) + 2 * _nbytes((tm, D), F32)
    return pl.pallas_call(
        _outproj_kernel,
        out_shape=(jax.ShapeDtypeStruct((T, D), F32), jax.ShapeDtypeStruct((T, D), F32)),
        grid=(T // tm,),
        in_specs=[
            pl.BlockSpec((tm, D), lambda i: (i, 0)),
            pl.BlockSpec((tm, D), lambda i: (i, 0)),
            pl.BlockSpec((D, D), lambda i: (0, 0)),
            pl.BlockSpec((1, D), lambda i: (0, 0)),
        ],
        out_specs=(pl.BlockSpec((tm, D), lambda i: (i, 0)), pl.BlockSpec((tm, D), lambda i: (i, 0))),
        compiler_params=_cparams(1, vmem),
        name="outproj",
    )(merged, x, w_out, g_ffn.reshape(1, D))


def _router_kernel(h_ref, w_ref, info_ref, cnt_ref, carry_ref):
    tm = h_ref.shape[0]

    @pl.when(pl.program_id(0) == 0)
    def _():
        carry_ref[...] = jnp.zeros_like(carry_ref)

    logits = jnp.dot(h_ref[...].astype(BF16), w_ref[...], preferred_element_type=F32)
    lane = lax.broadcasted_iota(I32, (tm, V7X_LANES), 1)
    is_g = lane < N_GROUPS
    gl = jnp.where(is_g, logits, NEG_INF)
    gmax = jnp.max(gl, axis=1, keepdims=True)
    g_sel = jnp.min(jnp.where(gl == gmax, lane, V7X_LANES), axis=1, keepdims=True)
    gsum = jnp.sum(jnp.where(is_g, jnp.exp(gl - gmax), 0.0), axis=1, keepdims=True)
    p_group = 1.0 / gsum
    lo = N_GROUPS + EXPERTS_PER_GROUP * g_sel
    emask = (lane >= lo) & (lane < lo + EXPERTS_PER_GROUP)
    el = jnp.where(emask, logits, NEG_INF)
    e1 = jnp.max(el, axis=1, keepdims=True)
    i1 = jnp.min(jnp.where((el == e1) & emask, lane, V7X_LANES), axis=1, keepdims=True)
    emask2 = emask & (lane != i1)
    el2 = jnp.where(emask2, logits, NEG_INF)
    e2 = jnp.max(el2, axis=1, keepdims=True)
    i2 = jnp.min(jnp.where((el2 == e2) & emask2, lane, V7X_LANES), axis=1, keepdims=True)
    r = jnp.exp(e2 - e1)
    w1 = p_group / (1.0 + r)
    w2 = p_group * r / (1.0 + r)
    eid1 = i1 - N_GROUPS
    eid2 = i2 - N_GROUPS
    oh1 = jnp.where(lane == eid1, 1.0, 0.0)
    oh2 = jnp.where(lane == eid2, 1.0, 0.0)
    cnt = oh1 + oh2
    tri = jnp.where(lax.broadcasted_iota(I32, (tm, tm), 0) > lax.broadcasted_iota(I32, (tm, tm), 1), 1.0, 0.0)
    before = jnp.dot(tri.astype(BF16), cnt.astype(BF16), preferred_element_type=F32) + carry_ref[...]
    rank1 = jnp.sum(oh1 * before, axis=1, keepdims=True)
    rank2 = jnp.sum(oh2 * before, axis=1, keepdims=True)
    carry_ref[...] = carry_ref[...] + jnp.sum(cnt, axis=0, keepdims=True)
    info = jnp.zeros((tm, V7X_LANES), F32)
    for k, val in enumerate((eid1.astype(F32), eid2.astype(F32), w1, w2, rank1, rank2)):
        info = jnp.where(lane == k, val, info)
    info_ref[...] = info
    cnt_ref[...] = carry_ref[...]


def _router(h2, w_router, tm=256):
    T, D = h2.shape
    vmem = 2 * (_nbytes((tm, D), F32) + _nbytes((D, V7X_LANES), BF16)) + 16 * _nbytes((tm, V7X_LANES), F32) + (1 << 22)
    return pl.pallas_call(
        _router_kernel,
        out_shape=(jax.ShapeDtypeStruct((T, V7X_LANES), F32), jax.ShapeDtypeStruct((1, V7X_LANES), F32)),
        grid=(T // tm,),
        in_specs=[pl.BlockSpec((tm, D), lambda i: (i, 0)), pl.BlockSpec((D, V7X_LANES), lambda i: (0, 0))],
        out_specs=(pl.BlockSpec((tm, V7X_LANES), lambda i: (i, 0)), pl.BlockSpec((1, V7X_LANES), lambda i: (0, 0))),
        scratch_shapes=[pltpu.VMEM((1, V7X_LANES), F32)],
        compiler_params=_cparams(1, vmem),
        name="router",
    )(h2, w_router)


def _row_copy(src_ref, src_row, dst_ref, dst_row, sem):
    return pltpu.make_async_copy(src_ref.at[pl.ds(src_row, 1), :], dst_ref.at[pl.ds(dst_row, 1), :], sem)


def _dispatch_kernel(dest_ref, h_ref, xs_in_ref, xs_ref, sem):
    del xs_in_ref
    tm = h_ref.shape[0]
    base = pl.program_id(0) * tm

    def issue(r, c):
        for k in range(TOPK_IN_GROUP):
            _row_copy(h_ref, r, xs_ref, dest_ref[(base + r) * TOPK_IN_GROUP + k], sem).start()
        return c

    def drain(r, c):
        for k in range(TOPK_IN_GROUP):
            _row_copy(h_ref, 0, xs_ref, 0, sem).wait()
        return c

    lax.fori_loop(0, tm, issue, 0)
    lax.fori_loop(0, tm, drain, 0)


def _dispatch(dest_flat, h2, m_pad, tm=256):
    T, D = h2.shape
    xs0 = jnp.zeros((m_pad, D), F32)
    return pl.pallas_call(
        _dispatch_kernel,
        out_shape=jax.ShapeDtypeStruct((m_pad, D), F32),
        grid_spec=pltpu.PrefetchScalarGridSpec(
            num_scalar_prefetch=1,
            grid=(T // tm,),
            in_specs=[pl.BlockSpec((tm, D), lambda i, d: (i, 0)), pl.BlockSpec(memory_space=pl.ANY)],
            out_specs=pl.BlockSpec(memory_space=pl.ANY),
            scratch_shapes=[pltpu.SemaphoreType.DMA(())],
        ),
        input_output_aliases={2: 0},
        compiler_params=_cparams(1, 4 * _nbytes((tm, D), F32)),
        name="dispatch",
    )(dest_flat, h2, xs0)


def _expert_kernel(be_ref, nu_ref, xs_ref, wg_ref, wu_ref, wd_ref, y_ref):
    del be_ref
    b = pl.program_id(0)

    @pl.when(b < nu_ref[0])
    def _():
        x = xs_ref[...].astype(BF16)
        a = jnp.dot(x, wg_ref[0], preferred_element_type=F32)
        u = jnp.dot(x, wu_ref[0], preferred_element_type=F32)
        hm = (a * jax.nn.sigmoid(a) * u).astype(BF16)
        y_ref[...] = jnp.dot(hm, wd_ref[0], preferred_element_type=F32)

    @pl.when(b >= nu_ref[0])
    def _():
        y_ref[...] = jnp.zeros_like(y_ref)


def _experts(block_expert, n_used, xs, w_gate, w_up, w_down):
    m_pad, D = xs.shape
    R, Fd = MOE_ROWS, D_EXPERT
    vmem = 2 * (2 * _nbytes((R, D), F32) + 3 * _nbytes((D, Fd), BF16)) + 4 * _nbytes((R, D), F32)
    return pl.pallas_call(
        _expert_kernel,
        out_shape=jax.ShapeDtypeStruct((m_pad, D), F32),
        grid_spec=pltpu.PrefetchScalarGridSpec(
            num_scalar_prefetch=2,
            grid=(m_pad // R,),
            in_specs=[
                pl.BlockSpec((R, D), lambda b, be, nu: (jnp.minimum(b, jnp.maximum(nu[0] - 1, 0)), 0)),
                pl.BlockSpec((1, D, Fd), lambda b, be, nu: (be[b], 0, 0)),
                pl.BlockSpec((1, D, Fd), lambda b, be, nu: (be[b], 0, 0)),
                pl.BlockSpec((1, Fd, D), lambda b, be, nu: (be[b], 0, 0)),
            ],
            out_specs=pl.BlockSpec((R, D), lambda b, be, nu: (b, 0)),
        ),
        compiler_params=_cparams(1, vmem),
        name="experts",
    )(block_expert, n_used, xs, w_gate, w_up, w_down)


def _combine_kernel(dest_ref, x1_ref, info_ref, yb_ref, p_ref, gp_ref, wpg_ref, wpp_ref, gf_ref, o_ref, ybuf, sem):
    tm = x1_ref.shape[0]
    base = pl.program_id(0) * tm

    def issue(r, c):
        for k in range(TOPK_IN_GROUP):
            _row_copy(yb_ref, dest_ref[(base + r) * TOPK_IN_GROUP + k], ybuf.at[k], r, sem).start()
        return c

    def drain(r, c):
        for k in range(TOPK_IN_GROUP):
            _row_copy(yb_ref, 0, ybuf.at[k], 0, sem).wait()
        return c

    lax.fori_loop(0, tm, issue, 0)
    lax.fori_loop(0, tm, drain, 0)
    info = info_ref[...]
    x2 = x1_ref[...] + info[:, 2:3] * ybuf[0] + info[:, 3:4] * ybuf[1]
    hp = _rms(x2, gp_ref[...]).astype(BF16)
    z = jnp.dot(hp, wpg_ref[...], preferred_element_type=F32)
    pp = jnp.dot(p_ref[...].astype(BF16), wpp_ref[...], preferred_element_type=F32)
    x3 = x2 + jax.nn.sigmoid(z) * pp
    o_ref[...] = _rms(x3, gf_ref[...])


def _combine(dest_flat, x1, info, yb, p, g_ple, w_ple_gate, w_ple_proj, g_final, tm=256):
    T, D = x1.shape
    vmem = 2 * (2 * _nbytes((tm, D), F32) + _nbytes((D, D), BF16) + _nbytes((PLE_DIM, D), BF16)
                + _nbytes((tm, PLE_DIM), F32)) + 8 * _nbytes((tm, D), F32)
    return pl.pallas_call(
        _combine_kernel,
        out_shape=jax.ShapeDtypeStruct((T, D), F32),
        grid_spec=pltpu.PrefetchScalarGridSpec(
            num_scalar_prefetch=1,
            grid=(T // tm,),
            in_specs=[
                pl.BlockSpec((tm, D), lambda i, d: (i, 0)),
                pl.BlockSpec((tm, V7X_LANES), lambda i, d: (i, 0)),
                pl.BlockSpec(memory_space=pl.ANY),
                pl.BlockSpec((tm, PLE_DIM), lambda i, d: (i, 0)),
                pl.BlockSpec((1, D), lambda i, d: (0, 0)),
                pl.BlockSpec((D, D), lambda i, d: (0, 0)),
                pl.BlockSpec((PLE_DIM, D), lambda i, d: (0, 0)),
                pl.BlockSpec((1, D), lambda i, d: (0, 0)),
            ],
            out_specs=pl.BlockSpec((tm, D), lambda i, d: (i, 0)),
            scratch_shapes=[pltpu.VMEM((TOPK_IN_GROUP, tm, D), F32), pltpu.SemaphoreType.DMA(())],
        ),
        compiler_params=_cparams(1, vmem),
        name="combine",
    )(dest_flat, x1, info, yb, p, g_ple.reshape(1, D), w_ple_gate, w_ple_proj, g_final.reshape(1, D))


def _rope_tables(T):
    half = MOBA_HEAD_DIM // 2
    inv_freq = ROPE_THETA ** (-jnp.arange(half, dtype=F32) / half)
    ang = jnp.arange(T, dtype=F32)[:, None] * inv_freq[None, :]
    cos, sin = jnp.cos(ang), jnp.sin(ang)
    return jnp.concatenate([cos, cos], axis=1), jnp.concatenate([-sin, sin], axis=1)


def _mixers(x2d, g_mix, w_in, lb, hgrn_norm_g):
    T = x2d.shape[0]
    W = HGRN_WIDTH
    h = _rmsnorm(x2d, g_mix, BF16)
    w = w_in.astype(BF16)
    cos, sin = _rope_tables(T)
    log_lb = jnp.log(lb).reshape(1, W)
    log_1m = jnp.log1p(-lb).reshape(1, W)
    hq = _proj(h, w, 0 * W, W, _ep_silu, F32)
    logf = _proj(h, w, 1 * W, W, _ep_logf, F32, col_extras=(log_lb, log_1m))
    hi = _proj(h, w, 2 * W, W, _ep_identity, F32)
    hog = _proj(h, w, 3 * W, W, _ep_silu, F32)
    scale = MOBA_HEAD_DIM ** -0.5
    mq = _proj(h, w, 4 * W, W, functools.partial(_ep_rope, scale=scale), BF16, row_extras=(cos, sin))
    blk = jnp.arange(T, dtype=I32)[:, None] // MOBA_BLOCK
    blk_onehot = (blk == jnp.arange(V7X_LANES, dtype=I32)[None, :]).astype(F32)
    mk = _proj(h, w, 5 * W, W, _ep_rope_aug, BF16, row_extras=(cos, sin, blk_onehot), widen=2)
    mv = _proj(h, w, 6 * W, W, _ep_identity, BF16)
    gates = _proj(h, w, 7 * W, 2 * D_MODEL, _ep_sigmoid, BF16)
    o_hgrn = _hgrn(hq, logf, hi, hog, hgrn_norm_g)
    o_moba = _moba(mq, mk, mv)
    return o_hgrn, o_moba, gates


def _moe_plan(info, cnt, T):
    R = MOE_ROWS
    eid = info[:, 0:TOPK_IN_GROUP].astype(I32)
    rank = info[:, 4:4 + TOPK_IN_GROUP].astype(I32)
    counts = cnt[0, :N_EXPERTS].astype(I32)
    padded = (counts + R - 1) // R * R
    pad_end = jnp.cumsum(padded)
    pad_start = pad_end - padded
    dest = (pad_start[eid] + rank).reshape(-1)
    n_blocks = (T * TOPK_IN_GROUP) // R + N_EXPERTS
    block_expert = jnp.minimum(
        jnp.searchsorted(pad_end, jnp.arange(n_blocks, dtype=I32) * R, side="right"), N_EXPERTS - 1).astype(I32)
    n_used = (pad_end[-1:] // R).astype(I32)
    return dest, block_expert, n_used, n_blocks * R


def kernel(x, p, norm_mix_g, w_in, hgrn_lb_raw, hgrn_norm_g, w_up_hgrn, w_up_moba, w_out, norm_ffn_g,
           w_router_group, w_router_expert, w_exp_gate, w_exp_up, w_exp_down, norm_ple_g, w_ple_gate,
           w_ple_proj, norm_final_g):
    B, T, D = x.shape
    assert B == 1 and D == D_MODEL and w_in.shape[0] == 1 and T % (4 * MOBA_BLOCK) == 0
    lower_bounds = jnp.cumsum(jax.nn.softmax(hgrn_lb_raw.astype(F32), axis=0), axis=0)
    x2d = x.reshape(T, D)
    o_hgrn, o_moba, gates = _mixers(x2d, norm_mix_g[0], w_in[0], lower_bounds[0], hgrn_norm_g[0])
    merged = _merge(o_hgrn, o_moba, gates, w_up_hgrn[0].astype(BF16), w_up_moba[0].astype(BF16))
    x1, h2 = _outproj(merged, x2d, w_out[0].astype(BF16), norm_ffn_g[0])
    w_router = jnp.pad(jnp.concatenate([w_router_group[0], w_router_expert[0]], axis=1),
                       ((0, 0), (0, V7X_LANES - N_GROUPS - N_EXPERTS))).astype(BF16)
    info, cnt = _router(h2, w_router)
    dest, block_expert, n_used, m_pad = _moe_plan(info, cnt, T)
    xs = _dispatch(dest, h2, m_pad)
    yb = _experts(block_expert, n_used, xs, w_exp_gate[0].astype(BF16), w_exp_up[0].astype(BF16),
                  w_exp_down[0].astype(BF16))
    out = _combine(dest, x1, info, yb, p[0].reshape(T, PLE_DIM), norm_ple_g[0], w_ple_gate[0].astype(BF16),
                   w_ple_proj[0].astype(BF16), norm_final_g)
    return out.reshape(B, T, D)
```

```python
import functools

import jax
import jax.numpy as jnp
from jax import lax
from jax.experimental import pallas as pl
from jax.experimental.pallas import tpu as pltpu

F32 = jnp.float32
BF16 = jnp.bfloat16
I32 = jnp.int32

D_MODEL = 2048
PLE_DIM = 256
HGRN_HEADS = 8
HGRN_HEAD_DIM = 128
HGRN_WIDTH = HGRN_HEADS * HGRN_HEAD_DIM
MOBA_HEADS = 8
MOBA_HEAD_DIM = 128
MOBA_WIDTH = MOBA_HEADS * MOBA_HEAD_DIM
MOBA_BLOCK = 256
MOBA_TOPK = 3
ROPE_THETA = 10000.0
N_GROUPS = 4
EXPERTS_PER_GROUP = 8
N_EXPERTS = N_GROUPS * EXPERTS_PER_GROUP
TOPK_IN_GROUP = 2
D_EXPERT = 512
EPS = 1e-6
NEG_INF = -1e30

V7X_LANES = 128
V7X_SUBLANES = 8
V7X_VMEM_BUDGET_BYTES = 56 * 1024 * 1024

HGRN_FOLD = 16
HGRN_CHUNK = V7X_SUBLANES * HGRN_FOLD
MOE_ROWS = 256


def _cparams(n_grid, vmem_bytes):
    return pltpu.CompilerParams(
        dimension_semantics=("arbitrary",) * n_grid,
        vmem_limit_bytes=int(min(max(vmem_bytes, 16 * 1024 * 1024), V7X_VMEM_BUDGET_BYTES)),
    )


def _nbytes(shape, dtype):
    n = 1
    for s in shape:
        n *= s
    return n * jnp.dtype(dtype).itemsize


def _rms(x, g):
    ms = jnp.mean(x * x, axis=-1, keepdims=True)
    return x * lax.rsqrt(ms + EPS) * g


def _rmsnorm_kernel(x_ref, g_ref, o_ref):
    o_ref[...] = _rms(x_ref[...], g_ref[...]).astype(o_ref.dtype)


def _rmsnorm(x, g, out_dtype, tm=512):
    T, D = x.shape
    return pl.pallas_call(
        _rmsnorm_kernel,
        out_shape=jax.ShapeDtypeStruct((T, D), out_dtype),
        grid=(T // tm,),
        in_specs=[pl.BlockSpec((tm, D), lambda i: (i, 0)), pl.BlockSpec((1, D), lambda i: (0, 0))],
        out_specs=pl.BlockSpec((tm, D), lambda i: (i, 0)),
        compiler_params=_cparams(1, 4 * _nbytes((tm, D), F32)),
        name="rmsnorm",
    )(x, g.reshape(1, D))


def _ep_identity(acc):
    return acc


def _ep_silu(acc):
    return acc * jax.nn.sigmoid(acc)


def _ep_sigmoid(acc):
    return jax.nn.sigmoid(acc)


def _ep_logf(acc, la_ref, lc_ref):
    ls = jnp.minimum(acc, 0.0) - jnp.log(1.0 + jnp.exp(-jnp.abs(acc)))
    u = la_ref[...]
    v = lc_ref[...] + ls
    return jnp.maximum(u, v) + jnp.log(1.0 + jnp.exp(-jnp.abs(u - v)))


def _ep_rope(acc, cos_ref, sin_ref, *, scale):
    cos = cos_ref[...]
    sin = sin_ref[...]
    outs = []
    for hh in range(acc.shape[1] // MOBA_HEAD_DIM):
        a = acc[:, hh * MOBA_HEAD_DIM:(hh + 1) * MOBA_HEAD_DIM]
        r = pltpu.roll(a, MOBA_HEAD_DIM // 2, axis=1)
        outs.append((a * cos + r * sin) * scale)
    return jnp.concatenate(outs, axis=1)


def _ep_rope_aug(acc, cos_ref, sin_ref, oh_ref):
    cos = cos_ref[...]
    sin = sin_ref[...]
    oh = oh_ref[...]
    outs = []
    for hh in range(acc.shape[1] // MOBA_HEAD_DIM):
        a = acc[:, hh * MOBA_HEAD_DIM:(hh + 1) * MOBA_HEAD_DIM]
        outs.append(a * cos + pltpu.roll(a, MOBA_HEAD_DIM // 2, axis=1) * sin)
        outs.append(oh)
    return jnp.concatenate(outs, axis=1)


def _proj_kernel(h_ref, w_ref, *refs, epilogue):
    *extra, o_ref = refs
    acc = jnp.dot(h_ref[...], w_ref[...], preferred_element_type=F32)
    o_ref[...] = epilogue(acc, *extra).astype(o_ref.dtype)


def _proj(h, w, col0, ncols, epilogue, out_dtype, row_extras=(), col_extras=(), tm=1024, tn=1024, widen=1):
    T, K = h.shape
    tn = min(tn, ncols)
    tm = min(tm, T)
    cb = col0 // tn
    otn = widen * tn
    in_specs = [
        pl.BlockSpec((tm, K), lambda j, i: (i, 0)),
        pl.BlockSpec((K, tn), lambda j, i: (0, cb + j)),
    ]
    for e in row_extras:
        in_specs.append(pl.BlockSpec((tm, e.shape[1]), lambda j, i: (i, 0)))
    for e in col_extras:
        in_specs.append(pl.BlockSpec((1, tn), lambda j, i: (0, j)))
    vmem = 2 * (_nbytes((tm, K), h.dtype) + _nbytes((K, tn), w.dtype) + _nbytes((tm, otn), out_dtype))
    vmem += 3 * _nbytes((tm, otn), F32)
    return pl.pallas_call(
        functools.partial(_proj_kernel, epilogue=epilogue),
        out_shape=jax.ShapeDtypeStruct((T, widen * ncols), out_dtype),
        grid=(ncols // tn, T // tm),
        in_specs=in_specs,
        out_specs=pl.BlockSpec((tm, otn), lambda j, i: (i, j)),
        compiler_params=_cparams(2, vmem),
        name="proj",
    )(h, w, *row_extras, *col_extras)


MOBA_VT_ROWS = MOBA_HEAD_DIM + 16


def _proj_t_kernel(h_ref, w_ref, *refs, epilogue, ones_rows):
    *extra, o_ref = refs
    BS, HD = MOBA_BLOCK, MOBA_HEAD_DIM
    acc = epilogue(jnp.dot(h_ref[...], w_ref[...], preferred_element_type=F32), *extra)
    ones = jnp.ones((ones_rows, BS), F32) if ones_rows else None
    for b in range(acc.shape[0] // BS):
        parts = []
        for hh in range(acc.shape[1] // HD):
            parts.append(acc[b * BS:(b + 1) * BS, hh * HD:(hh + 1) * HD].T)
            if ones_rows:
                parts.append(ones)
        o_ref[b] = jnp.concatenate(parts, axis=0).astype(o_ref.dtype)


def _proj_t(h, w, col0, epilogue, row_extras=(), ones_rows=0, tm=1024):
    T, K = h.shape
    tn = MOBA_WIDTH
    tm = min(tm, T)
    cb = col0 // tn
    rows = MOBA_HEADS * (MOBA_HEAD_DIM + ones_rows)
    in_specs = [pl.BlockSpec((tm, K), lambda i: (i, 0)), pl.BlockSpec((K, tn), lambda i: (0, cb))]
    for e in row_extras:
        in_specs.append(pl.BlockSpec((tm, e.shape[1]), lambda i: (i, 0)))
    vmem = 2 * (_nbytes((tm, K), h.dtype) + _nbytes((K, tn), w.dtype) + _nbytes((tm, 2 * tn), BF16))
    vmem += 4 * _nbytes((tm, tn), F32)
    return pl.pallas_call(
        functools.partial(_proj_t_kernel, epilogue=epilogue, ones_rows=ones_rows),
        out_shape=jax.ShapeDtypeStruct((T // MOBA_BLOCK, rows, MOBA_BLOCK), BF16),
        grid=(T // tm,),
        in_specs=in_specs,
        out_specs=pl.BlockSpec((tm // MOBA_BLOCK, rows, MOBA_BLOCK), lambda i: (i, 0, 0)),
        compiler_params=_cparams(1, vmem),
        name="proj_t",
    )(h, w, *row_extras)


def _hgrn_kernel(q_ref, g_ref, v_ref, og_ref, ng_ref, o_ref, st_ref):
    W, J, HD, NH = HGRN_WIDTH, HGRN_FOLD, HGRN_HEAD_DIM, HGRN_HEADS
    C = HGRN_CHUNK

    @pl.when(pl.program_id(0) == 0)
    def _():
        st_ref[...] = jnp.zeros_like(st_ref)

    a_idx = lax.broadcasted_iota(I32, (V7X_SUBLANES, W), 0)

    def cols(ref):
        return [ref[:, j * W:(j + 1) * W] for j in range(J)]

    def bcast_row(x, r):
        return jnp.broadcast_to(x[r:r + 1, :], x.shape)

    g = cols(g_ref)
    q = cols(q_ref)
    v = cols(v_ref)
    og = cols(og_ref)

    P = [g[0]]
    for j in range(1, J):
        P.append(P[-1] + g[j])
    R = P[J - 1]
    Y = R
    for s in (1, 2, 4):
        Y = Y + jnp.where(a_idx >= s, pltpu.roll(Y, s, axis=0), 0.0)
    X = Y - R
    bC = bcast_row(Y, V7X_SUBLANES - 1)

    fk = [1.0 - jnp.exp(g[j]) for j in range(J)]
    qe = [q[j] * jnp.exp(P[j] + X) for j in range(J)]
    ks = [fk[j] * jnp.exp(bC - (P[j] + X)) for j in range(J)]

    levels = [(0, q, fk)]
    for lvl, m in enumerate((1, 2, 4, 8), start=1):
        qr = [None] * J
        kr = [None] * J
        for j in range(J):
            jr = (j // (2 * m)) * (2 * m) + m - 1
            if (j % (2 * m)) >= m:
                qr[j] = q[j] * jnp.exp(P[j] - P[jr])
            else:
                kr[j] = fk[j] * jnp.exp(P[jr] - P[j])
        levels.append((lvl, qr, kr))
    y16 = jnp.where(a_idx % 2 == 0, Y, pltpu.roll(Y, 1, axis=0))
    y32 = jnp.where(a_idx < 4, bcast_row(Y, 1), bcast_row(Y, 5))
    y64 = bcast_row(Y, 3)
    big = ((y16, a_idx % 2 == 1), (y32, a_idx % 4 >= 2), (y64, a_idx >= 4))
    for lvl, (yr, upper) in enumerate(big, start=5):
        z = X - yr
        qr, kr = [], []
        for j in range(J):
            e = jnp.exp(-jnp.abs(P[j] + z))
            qr.append(jnp.where(upper, q[j] * e, 0.0))
            kr.append(jnp.where(upper, 0.0, fk[j] * e))
        levels.append((lvl, qr, kr))

    pr = lax.broadcasted_iota(I32, (C, C), 0)
    pc = lax.broadcasted_iota(I32, (C, C), 1)
    tr = HGRN_FOLD * (pr % V7X_SUBLANES) + pr // V7X_SUBLANES
    tc = HGRN_FOLD * (pc % V7X_SUBLANES) + pc // V7X_SUBLANES
    xr = tr ^ tc
    code = jnp.zeros((C, C), I32)
    for lvl in range(1, 8):
        code = jnp.where(xr >= (1 << (lvl - 1)), lvl, code)
    code = jnp.where(tc > tr, -1, code)

    zero_tile = jnp.zeros((V7X_SUBLANES, HD), F32)

    def mat(rows, h):
        return jnp.concatenate(
            [zero_tile if r is None else r[:, h * HD:(h + 1) * HD] for r in rows], axis=0)

    nt = (((1,), (1,)), ((), ()))
    tn = (((0,), (0,)), ((), ()))
    ebc = jnp.exp(bC)
    ng = ng_ref[...]
    for h in range(NH):
        a_mat = jnp.zeros((C, C), F32)
        for lvl, qr, kr in levels:
            s = lax.dot_general(mat(qr, h).astype(BF16), mat(kr, h).astype(BF16), nt,
                                preferred_element_type=F32)
            a_mat = jnp.where(code == lvl, s, a_mat)
        vh = mat(v, h).astype(BF16)
        st = st_ref[h]
        o = jnp.dot(a_mat.astype(BF16), vh, preferred_element_type=F32)
        o = o + lax.dot_general(mat(qe, h).astype(BF16), st.astype(BF16), nt, preferred_element_type=F32)
        o = _rms(o, ng) * mat(og, h)
        for j in range(J):
            o_ref[:, j * W + h * HD:j * W + (h + 1) * HD] = o[j * V7X_SUBLANES:(j + 1) * V7X_SUBLANES, :]
        st_ref[h] = st * ebc[0:1, h * HD:(h + 1) * HD] + lax.dot_general(
            vh, mat(ks, h).astype(BF16), tn, preferred_element_type=F32)


def _hgrn(q, logf, v, og, norm_g):
    T, W = q.shape
    rows = T // HGRN_FOLD
    fold = lambda a: a.reshape(rows, HGRN_FOLD * W)
    blk = pl.BlockSpec((V7X_SUBLANES, HGRN_FOLD * W), lambda c: (c, 0))
    vmem = 10 * _nbytes((V7X_SUBLANES, HGRN_FOLD * W), F32) + 40 * _nbytes((V7X_SUBLANES, HGRN_FOLD * W), F32)
    out = pl.pallas_call(
        _hgrn_kernel,
        out_shape=jax.ShapeDtypeStruct((rows, HGRN_FOLD * W), F32),
        grid=(T // HGRN_CHUNK,),
        in_specs=[blk, blk, blk, blk, pl.BlockSpec((1, HGRN_HEAD_DIM), lambda c: (0, 0))],
        out_specs=blk,
        scratch_shapes=[pltpu.VMEM((HGRN_HEADS, HGRN_HEAD_DIM, HGRN_HEAD_DIM), F32)],
        compiler_params=_cparams(1, vmem),
        name="hgrn",
    )(fold(q), fold(logf), fold(v), fold(og), norm_g.reshape(1, HGRN_HEAD_DIM))
    return out.reshape(T, W)


def _moba_kernel(qt_ref, k_ref, vt_ref, o_ref, km_ref):
    BS, HD, VR = MOBA_BLOCK, MOBA_HEAD_DIM, MOBA_VT_ROWS
    T = k_ref.shape[0]
    NB = T // BS
    G = o_ref.shape[1] // HD
    cur = pl.program_id(1)

    @pl.when(cur == 0)
    def _():
        for g in range(G):
            kf = k_ref[:, 2 * g * HD:(2 * g + 1) * HD].astype(F32).reshape(NB, BS, HD)
            km_ref[g] = jnp.sum(kf, axis=1) * (1.0 / BS)

    blk = lax.broadcasted_iota(I32, (NB, BS), 0)
    key = lax.broadcasted_iota(I32, (BS, BS), 0)
    qry = lax.broadcasted_iota(I32, (BS, BS), 1)
    pad = jnp.zeros((V7X_LANES - NB, BS), F32)
    r0 = pl.multiple_of(cur * BS, BS)
    qts = [qt_ref[0, g * HD:(g + 1) * HD, :] for g in range(G)]
    gts = [jnp.dot(km_ref[g].astype(BF16), qts[g], preferred_element_type=F32) for g in range(G)]
    s_own = [jnp.dot(k_ref[pl.ds(r0, BS), 2 * g * HD:(2 * g + 1) * HD], qts[g], preferred_element_type=F32)
             for g in range(G)]
    qcs = []
    for g in range(G):
        gt = jnp.where(blk < cur, gts[g], NEG_INF)
        sel = jnp.zeros((NB, BS), jnp.bool_)
        for _ in range(MOBA_TOPK):
            mx = jnp.max(gt, axis=0, keepdims=True)
            idx = jnp.min(jnp.where(gt == mx, blk, NB), axis=0, keepdims=True)
            pick = (blk == idx) & (mx > 0.5 * NEG_INF)
            sel = sel | pick
            gt = jnp.where(pick, NEG_INF, gt)
        pen = jnp.concatenate([jnp.where(sel, 0.0, NEG_INF), pad], axis=0).astype(BF16)
        qcs.append(jnp.concatenate([qts[g], pen], axis=0))

    def scores(n):
        rn = pl.multiple_of(n * BS, BS)
        return tuple(jnp.dot(k_ref[pl.ds(rn, BS), 2 * g * HD:(2 * g + 2) * HD], qcs[g], preferred_element_type=F32)
                     for g in range(G))

    ms, ps = [], []
    for g in range(G):
        s = jnp.where(key <= qry, s_own[g], NEG_INF)
        m0 = jnp.max(s, axis=0, keepdims=True)
        ms.append(m0)
        ps.append(jnp.exp(s - m0).astype(BF16))
    accs = [jnp.dot(vt_ref[cur, g * VR:(g + 1) * VR, :], ps[g], preferred_element_type=F32)
            for g in range(G)]

    def body(n, carry):
        ms, accs = carry
        sns = scores(n)
        new_ms, alphas, pns = [], [], []
        for g in range(G):
            m_new = jnp.maximum(ms[g], jnp.max(sns[g], axis=0, keepdims=True))
            alphas.append(jnp.exp(ms[g] - m_new))
            pns.append(jnp.exp(sns[g] - m_new).astype(BF16))
            new_ms.append(m_new)
        pvs = [jnp.dot(vt_ref[n, g * VR:(g + 1) * VR, :], pns[g], preferred_element_type=F32) for g in range(G)]
        return tuple(new_ms), tuple(alphas[g] * accs[g] + pvs[g] for g in range(G))

    _, accs = lax.fori_loop(0, cur, body, (tuple(ms), tuple(accs)))
    for g in range(G):
        ot = accs[g][:HD, :] / accs[g][HD:HD + 1, :]
        o_ref[:, g * HD:(g + 1) * HD] = ot.T.astype(o_ref.dtype)


def _moba(mqt, mk_aug, mvt, heads_per_step=4):
    T = mk_aug.shape[0]
    BS, HD, G, VR = MOBA_BLOCK, MOBA_HEAD_DIM, heads_per_step, MOBA_VT_ROWS
    NB = T // BS
    vmem = _nbytes((T, 2 * G * HD), BF16) + _nbytes((NB, G * VR, BS), BF16) + 8 * _nbytes((BS, G * HD), BF16)
    vmem += 16 * G * _nbytes((BS, BS), F32)
    resident = pl.Buffered(1)
    return pl.pallas_call(
        _moba_kernel,
        out_shape=jax.ShapeDtypeStruct((T, MOBA_WIDTH), BF16),
        grid=(MOBA_HEADS // G, NB),
        in_specs=[
            pl.BlockSpec((1, G * HD, BS), lambda h, i: (i, h, 0)),
            pl.BlockSpec((T, 2 * G * HD), lambda h, i: (0, h), pipeline_mode=resident),
            pl.BlockSpec((NB, G * VR, BS), lambda h, i: (0, h, 0), pipeline_mode=resident),
        ],
        out_specs=pl.BlockSpec((BS, G * HD), lambda h, i: (i, h)),
        scratch_shapes=[pltpu.VMEM((G, NB, HD), F32)],
        compiler_params=_cparams(2, vmem),
        name="moba",
    )(mqt, mk_aug, mvt)


def _merge_kernel(oh_ref, om_ref, ga_ref, gb_ref, wh_ref, wm_ref, o_ref):
    a = jnp.dot(oh_ref[...].astype(BF16), wh_ref[...], preferred_element_type=F32)
    b = jnp.dot(om_ref[...], wm_ref[...], preferred_element_type=F32)
    o_ref[...] = (ga_ref[...].astype(F32) * a + gb_ref[...].astype(F32) * b).astype(o_ref.dtype)


def _merge(o_hgrn, o_moba, gates, w_up_hgrn, w_up_moba, tm=512):
    T = o_hgrn.shape[0]
    D = D_MODEL
    vmem = 2 * (_nbytes((tm, HGRN_WIDTH), F32) + _nbytes((tm, MOBA_WIDTH), BF16) + 3 * _nbytes((tm, D), BF16)
                + 2 * _nbytes((HGRN_WIDTH, D), BF16)) + 3 * _nbytes((tm, D), F32)
    return pl.pallas_call(
        _merge_kernel,
        out_shape=jax.ShapeDtypeStruct((T, D), BF16),
        grid=(T // tm,),
        in_specs=[
            pl.BlockSpec((tm, HGRN_WIDTH), lambda i: (i, 0)),
            pl.BlockSpec((tm, MOBA_WIDTH), lambda i: (i, 0)),
            pl.BlockSpec((tm, D), lambda i: (i, 0)),
            pl.BlockSpec((tm, D), lambda i: (i, 1)),
            pl.BlockSpec((HGRN_WIDTH, D), lambda i: (0, 0)),
            pl.BlockSpec((MOBA_WIDTH, D), lambda i: (0, 0)),
        ],
        out_specs=pl.BlockSpec((tm, D), lambda i: (i, 0)),
        compiler_params=_cparams(1, vmem),
        name="merge",
    )(o_hgrn, o_moba, gates, gates, w_up_hgrn, w_up_moba)


def _outproj_kernel(m_ref, x_ref, w_ref, g_ref, x1_ref, h2_ref):
    x1 = x_ref[...] + jnp.dot(m_ref[...], w_ref[...], preferred_element_type=F32)
    x1_ref[...] = x1
    h2_ref[...] = _rms(x1, g_ref[...])


def _outproj(merged, x, w_out, g_ffn, tm=256):
    T, D = x.shape
    vmem = 2 * (_nbytes((tm, D), BF16) + 3 * _nbytes((tm, D), F32) + _nbytes((D, D), BF16)) + 2 * _nbytes((tm, D), F32)
    return pl.pallas_call(
        _outproj_kernel,
        out_shape=(jax.ShapeDtypeStruct((T, D), F32), jax.ShapeDtypeStruct((T, D), F32)),
        grid=(T // tm,),
        in_specs=[
            pl.BlockSpec((tm, D), lambda i: (i, 0)),
            pl.BlockSpec((tm, D), lambda i: (i, 0)),
            pl.BlockSpec((D, D), lambda i: (0, 0)),
            pl.BlockSpec((1, D), lambda i: (0, 0)),
        ],
        out_specs=(pl.BlockSpec((tm, D), lambda i: (i, 0)), pl.BlockSpec((tm, D), lambda i: (i, 0))),
        compiler_params=_cparams(1, vmem),
        name="outproj",
    )(merged, x, w_out, g_ffn.reshape(1, D))


def _router_kernel(h_ref, w_ref, info_ref, cnt_ref, carry_ref):
    tm = h_ref.shape[0]

    @pl.when(pl.program_id(0) == 0)
    def _():
        carry_ref[...] = jnp.zeros_like(carry_ref)

    logits = jnp.dot(h_ref[...].astype(BF16), w_ref[...], preferred_element_type=F32)
    lane = lax.broadcasted_iota(I32, (tm, V7X_LANES), 1)
    is_g = lane < N_GROUPS
    gl = jnp.where(is_g, logits, NEG_INF)
    gmax = jnp.max(gl, axis=1, keepdims=True)
    g_sel = jnp.min(jnp.where(gl == gmax, lane, V7X_LANES), axis=1, keepdims=True)
    gsum = jnp.sum(jnp.where(is_g, jnp.exp(gl - gmax), 0.0), axis=1, keepdims=True)
    p_group = 1.0 / gsum
    lo = N_GROUPS + EXPERTS_PER_GROUP * g_sel
    emask = (lane >= lo) & (lane < lo + EXPERTS_PER_GROUP)
    el = jnp.where(emask, logits, NEG_INF)
    e1 = jnp.max(el, axis=1, keepdims=True)
    i1 = jnp.min(jnp.where((el == e1) & emask, lane, V7X_LANES), axis=1, keepdims=True)
    emask2 = emask & (lane != i1)
    el2 = jnp.where(emask2, logits, NEG_INF)
    e2 = jnp.max(el2, axis=1, keepdims=True)
    i2 = jnp.min(jnp.where((el2 == e2) & emask2, lane, V7X_LANES), axis=1, keepdims=True)
    r = jnp.exp(e2 - e1)
    w1 = p_group / (1.0 + r)
    w2 = p_group * r / (1.0 + r)
    eid1 = i1 - N_GROUPS
    eid2 = i2 - N_GROUPS
    oh1 = jnp.where(lane == eid1, 1.0, 0.0)
    oh2 = jnp.where(lane == eid2, 1.0, 0.0)
    cnt = oh1 + oh2
    tri = jnp.where(lax.broadcasted_iota(I32, (tm, tm), 0) > lax.broadcasted_iota(I32, (tm, tm), 1), 1.0, 0.0)
    before = jnp.dot(tri.astype(BF16), cnt.astype(BF16), preferred_element_type=F32) + carry_ref[...]
    rank1 = jnp.sum(oh1 * before, axis=1, keepdims=True)
    rank2 = jnp.sum(oh2 * before, axis=1, keepdims=True)
    carry_ref[...] = carry_ref[...] + jnp.sum(cnt, axis=0, keepdims=True)
    info = jnp.zeros((tm, V7X_LANES), F32)
    for k, val in enumerate((eid1.astype(F32), eid2.astype(F32), w1, w2, rank1, rank2)):
        info = jnp.where(lane == k, val, info)
    info_ref[...] = info
    cnt_ref[...] = carry_ref[...]


def _router(h2, w_router, tm=256):
    T, D = h2.shape
    vmem = 2 * (_nbytes((tm, D), F32) + _nbytes((D, V7X_LANES), BF16)) + 16 * _nbytes((tm, V7X_LANES), F32) + (1 << 22)
    return pl.pallas_call(
        _router_kernel,
        out_shape=(jax.ShapeDtypeStruct((T, V7X_LANES), F32), jax.ShapeDtypeStruct((1, V7X_LANES), F32)),
        grid=(T // tm,),
        in_specs=[pl.BlockSpec((tm, D), lambda i: (i, 0)), pl.BlockSpec((D, V7X_LANES), lambda i: (0, 0))],
        out_specs=(pl.BlockSpec((tm, V7X_LANES), lambda i: (i, 0)), pl.BlockSpec((1, V7X_LANES), lambda i: (0, 0))),
        scratch_shapes=[pltpu.VMEM((1, V7X_LANES), F32)],
        compiler_params=_cparams(1, vmem),
        name="router",
    )(h2, w_router)


def _row_copy(src_ref, src_row, dst_ref, dst_row, sem):
    return pltpu.make_async_copy(src_ref.at[pl.ds(src_row, 1), :], dst_ref.at[pl.ds(dst_row, 1), :], sem)


def _dispatch_kernel(dest_ref, h_ref, xs_in_ref, xs_ref, sem):
    del xs_in_ref
    tm = h_ref.shape[0]
    base = pl.program_id(0) * tm

    def issue(r, c):
        for k in range(TOPK_IN_GROUP):
            _row_copy(h_ref, r, xs_ref, dest_ref[(base + r) * TOPK_IN_GROUP + k], sem).start()
        return c

    def drain(r, c):
        for k in range(TOPK_IN_GROUP):
            _row_copy(h_ref, 0, xs_ref, 0, sem).wait()
        return c

    lax.fori_loop(0, tm, issue, 0)
    lax.fori_loop(0, tm, drain, 0)


def _dispatch(dest_flat, h2, m_pad, tm=256):
    T, D = h2.shape
    xs0 = jnp.zeros((m_pad, D), F32)
    return pl.pallas_call(
        _dispatch_kernel,
        out_shape=jax.ShapeDtypeStruct((m_pad, D), F32),
        grid_spec=pltpu.PrefetchScalarGridSpec(
            num_scalar_prefetch=1,
            grid=(T // tm,),
            in_specs=[pl.BlockSpec((tm, D), lambda i, d: (i, 0)), pl.BlockSpec(memory_space=pl.ANY)],
            out_specs=pl.BlockSpec(memory_space=pl.ANY),
            scratch_shapes=[pltpu.SemaphoreType.DMA(())],
        ),
        input_output_aliases={2: 0},
        compiler_params=_cparams(1, 4 * _nbytes((tm, D), F32)),
        name="dispatch",
    )(dest_flat, h2, xs0)


def _expert_kernel(be_ref, nu_ref, xs_ref, wg_ref, wu_ref, wd_ref, y_ref):
    del be_ref
    b = pl.program_id(0)

    @pl.when(b < nu_ref[0])
    def _():
        x = xs_ref[...].astype(BF16)
        a = jnp.dot(x, wg_ref[0], preferred_element_type=F32)
        u = jnp.dot(x, wu_ref[0], preferred_element_type=F32)
        hm = (a * jax.nn.sigmoid(a) * u).astype(BF16)
        y_ref[...] = jnp.dot(hm, wd_ref[0], preferred_element_type=F32)

    @pl.when(b >= nu_ref[0])
    def _():
        y_ref[...] = jnp.zeros_like(y_ref)


def _experts(block_expert, n_used, xs, w_gate, w_up, w_down):
    m_pad, D = xs.shape
    R, Fd = MOE_ROWS, D_EXPERT
    vmem = 2 * (2 * _nbytes((R, D), F32) + 3 * _nbytes((D, Fd), BF16)) + 4 * _nbytes((R, D), F32)
    return pl.pallas_call(
        _expert_kernel,
        out_shape=jax.ShapeDtypeStruct((m_pad, D), F32),
        grid_spec=pltpu.PrefetchScalarGridSpec(
            num_scalar_prefetch=2,
            grid=(m_pad // R,),
            in_specs=[
                pl.BlockSpec((R, D), lambda b, be, nu: (jnp.minimum(b, jnp.maximum(nu[0] - 1, 0)), 0)),
                pl.BlockSpec((1, D, Fd), lambda b, be, nu: (be[b], 0, 0)),
                pl.BlockSpec((1, D, Fd), lambda b, be, nu: (be[b], 0, 0)),
                pl.BlockSpec((1, Fd, D), lambda b, be, nu: (be[b], 0, 0)),
            ],
            out_specs=pl.BlockSpec((R, D), lambda b, be, nu: (b, 0)),
        ),
        compiler_params=_cparams(1, vmem),
        name="experts",
    )(block_expert, n_used, xs, w_gate, w_up, w_down)


def _combine_kernel(dest_ref, x1_ref, info_ref, yb_ref, p_ref, gp_ref, wpg_ref, wpp_ref, gf_ref, o_ref, ybuf, sem):
    tm = x1_ref.shape[0]
    base = pl.program_id(0) * tm

    def issue(r, c):
        for k in range(TOPK_IN_GROUP):
            _row_copy(yb_ref, dest_ref[(base + r) * TOPK_IN_GROUP + k], ybuf.at[k], r, sem).start()
        return c

    def drain(r, c):
        for k in range(TOPK_IN_GROUP):
            _row_copy(yb_ref, 0, ybuf.at[k], 0, sem).wait()
        return c

    lax.fori_loop(0, tm, issue, 0)
    lax.fori_loop(0, tm, drain, 0)
    info = info_ref[...]
    x2 = x1_ref[...] + info[:, 2:3] * ybuf[0] + info[:, 3:4] * ybuf[1]
    hp = _rms(x2, gp_ref[...]).astype(BF16)
    z = jnp.dot(hp, wpg_ref[...], preferred_element_type=F32)
    pp = jnp.dot(p_ref[...].astype(BF16), wpp_ref[...], preferred_element_type=F32)
    x3 = x2 + jax.nn.sigmoid(z) * pp
    o_ref[...] = _rms(x3, gf_ref[...])


def _combine(dest_flat, x1, info, yb, p, g_ple, w_ple_gate, w_ple_proj, g_final, tm=256):
    T, D = x1.shape
    vmem = 2 * (2 * _nbytes((tm, D), F32) + _nbytes((D, D), BF16) + _nbytes((PLE_DIM, D), BF16)
                + _nbytes((tm, PLE_DIM), F32)) + 8 * _nbytes((tm, D), F32)
    return pl.pallas_call(
        _combine_kernel,
        out_shape=jax.ShapeDtypeStruct((T, D), F32),
        grid_spec=pltpu.PrefetchScalarGridSpec(
            num_scalar_prefetch=1,
            grid=(T // tm,),
            in_specs=[
                pl.BlockSpec((tm, D), lambda i, d: (i, 0)),
                pl.BlockSpec((tm, V7X_LANES), lambda i, d: (i, 0)),
                pl.BlockSpec(memory_space=pl.ANY),
                pl.BlockSpec((tm, PLE_DIM), lambda i, d: (i, 0)),
                pl.BlockSpec((1, D), lambda i, d: (0, 0)),
                pl.BlockSpec((D, D), lambda i, d: (0, 0)),
                pl.BlockSpec((PLE_DIM, D), lambda i, d: (0, 0)),
                pl.BlockSpec((1, D), lambda i, d: (0, 0)),
            ],
            out_specs=pl.BlockSpec((tm, D), lambda i, d: (i, 0)),
            scratch_shapes=[pltpu.VMEM((TOPK_IN_GROUP, tm, D), F32), pltpu.SemaphoreType.DMA(())],
        ),
        compiler_params=_cparams(1, vmem),
        name="combine",
    )(dest_flat, x1, info, yb, p, g_ple.reshape(1, D), w_ple_gate, w_ple_proj, g_final.reshape(1, D))


def _rope_tables(T):
    half = MOBA_HEAD_DIM // 2
    inv_freq = ROPE_THETA ** (-jnp.arange(half, dtype=F32) / half)
    ang = jnp.arange(T, dtype=F32)[:, None] * inv_freq[None, :]
    cos, sin = jnp.cos(ang), jnp.sin(ang)
    return jnp.concatenate([cos, cos], axis=1), jnp.concatenate([-sin, sin], axis=1)


def _mixers(x2d, g_mix, w_in, lb, hgrn_norm_g):
    T = x2d.shape[0]
    W = HGRN_WIDTH
    h = _rmsnorm(x2d, g_mix, BF16)
    w = w_in.astype(BF16)
    cos, sin = _rope_tables(T)
    log_lb = jnp.log(lb).reshape(1, W)
    log_1m = jnp.log1p(-lb).reshape(1, W)
    hq = _proj(h, w, 0 * W, W, _ep_silu, F32)
    logf = _proj(h, w, 1 * W, W, _ep_logf, F32, col_extras=(log_lb, log_1m))
    hi = _proj(h, w, 2 * W, W, _ep_identity, F32)
    hog = _proj(h, w, 3 * W, W, _ep_silu, F32)
    scale = MOBA_HEAD_DIM ** -0.5
    mqt = _proj_t(h, w, 4 * W, functools.partial(_ep_rope, scale=scale), row_extras=(cos, sin))
    blk = jnp.arange(T, dtype=I32)[:, None] // MOBA_BLOCK
    blk_onehot = (blk == jnp.arange(V7X_LANES, dtype=I32)[None, :]).astype(F32)
    mk = _proj(h, w, 5 * W, W, _ep_rope_aug, BF16, row_extras=(cos, sin, blk_onehot), widen=2)
    mvt = _proj_t(h, w, 6 * W, _ep_identity, ones_rows=MOBA_VT_ROWS - MOBA_HEAD_DIM)
    gates = _proj(h, w, 7 * W, 2 * D_MODEL, _ep_sigmoid, BF16)
    o_hgrn = _hgrn(hq, logf, hi, hog, hgrn_norm_g)
    o_moba = _moba(mqt, mk, mvt)
    return o_hgrn, o_moba, gates


def _moe_plan(info, cnt, T):
    R = MOE_ROWS
    eid = info[:, 0:TOPK_IN_GROUP].astype(I32)
    rank = info[:, 4:4 + TOPK_IN_GROUP].astype(I32)
    counts = cnt[0, :N_EXPERTS].astype(I32)
    padded = (counts + R - 1) // R * R
    pad_end = jnp.cumsum(padded)
    pad_start = pad_end - padded
    dest = (pad_start[eid] + rank).reshape(-1)
    n_blocks = (T * TOPK_IN_GROUP) // R + N_EXPERTS
    block_expert = jnp.minimum(
        jnp.searchsorted(pad_end, jnp.arange(n_blocks, dtype=I32) * R, side="right"), N_EXPERTS - 1).astype(I32)
    n_used = (pad_end[-1:] // R).astype(I32)
    return dest, block_expert, n_used, n_blocks * R


def kernel(x, p, norm_mix_g, w_in, hgrn_lb_raw, hgrn_norm_g, w_up_hgrn, w_up_moba, w_out, norm_ffn_g,
           w_router_group, w_router_expert, w_exp_gate, w_exp_up, w_exp_down, norm_ple_g, w_ple_gate,
           w_ple_proj, norm_final_g):
    B, T, D = x.shape
    assert B == 1 and D == D_MODEL and w_in.shape[0] == 1 and T % (4 * MOBA_BLOCK) == 0
    lower_bounds = jnp.cumsum(jax.nn.softmax(hgrn_lb_raw.astype(F32), axis=0), axis=0)
    x2d = x.reshape(T, D)
    o_hgrn, o_moba, gates = _mixers(x2d, norm_mix_g[0], w_in[0], lower_bounds[0], hgrn_norm_g[0])
    merged = _merge(o_hgrn, o_moba, gates, w_up_hgrn[0].astype(BF16), w_up_moba[0].astype(BF16))
    x1, h2 = _outproj(merged, x2d, w_out[0].astype(BF16), norm_ffn_g[0])
    w_router = jnp.pad(jnp.concatenate([w_router_group[0], w_router_expert[0]], axis=1),
                       ((0, 0), (0, V7X_LANES - N_GROUPS - N_EXPERTS))).astype(BF16)
    info, cnt = _router(h2, w_router)
    dest, block_expert, n_used, m_pad = _moe_plan(info, cnt, T)
    xs = _dispatch(dest, h2, m_pad)
    yb = _experts(block_expert, n_used, xs, w_exp_gate[0].astype(BF16), w_exp_up[0].astype(BF16),
                  w_exp_down[0].astype(BF16))
    out = _combine(dest, x1, info, yb, p[0].reshape(T, PLE_DIM), norm_ple_g[0], w_ple_gate[0].astype(BF16),
                   w_ple_proj[0].astype(BF16), norm_final_g)
    return out.reshape(B, T, D)
```

```python
import functools

import jax
import jax.numpy as jnp
from jax import lax
from jax.experimental import pallas as pl
from jax.experimental.pallas import tpu as pltpu

F32 = jnp.float32
BF16 = jnp.bfloat16
I32 = jnp.int32

D_MODEL = 2048
PLE_DIM = 256
HGRN_HEADS = 8
HGRN_HEAD_DIM = 128
HGRN_WIDTH = HGRN_HEADS * HGRN_HEAD_DIM
MOBA_HEADS = 8
MOBA_HEAD_DIM = 128
MOBA_WIDTH = MOBA_HEADS * MOBA_HEAD_DIM
MOBA_BLOCK = 256
MOBA_TOPK = 3
ROPE_THETA = 10000.0
N_GROUPS = 4
EXPERTS_PER_GROUP = 8
N_EXPERTS = N_GROUPS * EXPERTS_PER_GROUP
TOPK_IN_GROUP = 2
D_EXPERT = 512
EPS = 1e-6
NEG_INF = -1e30

V7X_LANES = 128
V7X_SUBLANES = 8
V7X_VMEM_BUDGET_BYTES = 56 * 1024 * 1024

HGRN_CHUNK = 128
MOE_ROWS = 256


def _cparams(n_grid, vmem_bytes):
    return pltpu.CompilerParams(
        dimension_semantics=("arbitrary",) * n_grid,
        vmem_limit_bytes=int(min(max(vmem_bytes, 16 * 1024 * 1024), V7X_VMEM_BUDGET_BYTES)),
    )


def _nbytes(shape, dtype):
    n = 1
    for s in shape:
        n *= s
    return n * jnp.dtype(dtype).itemsize


def _rms(x, g):
    ms = jnp.mean(x * x, axis=-1, keepdims=True)
    return x * lax.rsqrt(ms + EPS) * g


def _rmsnorm_kernel(x_ref, g_ref, o_ref):
    o_ref[...] = _rms(x_ref[...], g_ref[...]).astype(o_ref.dtype)


def _rmsnorm(x, g, out_dtype, tm=512):
    T, D = x.shape
    return pl.pallas_call(
        _rmsnorm_kernel,
        out_shape=jax.ShapeDtypeStruct((T, D), out_dtype),
        grid=(T // tm,),
        in_specs=[pl.BlockSpec((tm, D), lambda i: (i, 0)), pl.BlockSpec((1, D), lambda i: (0, 0))],
        out_specs=pl.BlockSpec((tm, D), lambda i: (i, 0)),
        compiler_params=_cparams(1, 4 * _nbytes((tm, D), F32)),
        name="rmsnorm",
    )(x, g.reshape(1, D))


def _ep_identity(acc):
    return acc


def _ep_silu(acc):
    return acc * jax.nn.sigmoid(acc)


def _ep_sigmoid(acc):
    return jax.nn.sigmoid(acc)


def _ep_logf(acc, la_ref, lc_ref):
    ls = jnp.minimum(acc, 0.0) - jnp.log(1.0 + jnp.exp(-jnp.abs(acc)))
    u = la_ref[...]
    v = lc_ref[...] + ls
    return jnp.maximum(u, v) + jnp.log(1.0 + jnp.exp(-jnp.abs(u - v)))


def _ep_rope(acc, cos_ref, sin_ref, *, scale):
    cos = cos_ref[...]
    sin = sin_ref[...]
    outs = []
    for hh in range(acc.shape[1] // MOBA_HEAD_DIM):
        a = acc[:, hh * MOBA_HEAD_DIM:(hh + 1) * MOBA_HEAD_DIM]
        r = pltpu.roll(a, MOBA_HEAD_DIM // 2, axis=1)
        outs.append((a * cos + r * sin) * scale)
    return jnp.concatenate(outs, axis=1)


def _ep_rope_aug(acc, cos_ref, sin_ref, oh_ref):
    cos = cos_ref[...]
    sin = sin_ref[...]
    oh = oh_ref[...]
    outs = []
    for hh in range(acc.shape[1] // MOBA_HEAD_DIM):
        a = acc[:, hh * MOBA_HEAD_DIM:(hh + 1) * MOBA_HEAD_DIM]
        outs.append(a * cos + pltpu.roll(a, MOBA_HEAD_DIM // 2, axis=1) * sin)
        outs.append(oh)
    return jnp.concatenate(outs, axis=1)


def _cast_weight_once(w_ref, wb_ref, row_axis):
    @pl.when(pl.program_id(row_axis) == 0)
    def _():
        wb_ref[...] = w_ref[...].astype(wb_ref.dtype)


def _proj_kernel(h_ref, w_ref, *refs, epilogue):
    *extra, o_ref, wb_ref = refs
    _cast_weight_once(w_ref, wb_ref, 1)
    acc = jnp.dot(h_ref[...], wb_ref[...], preferred_element_type=F32)
    o_ref[...] = epilogue(acc, *extra).astype(o_ref.dtype)


def _proj(h, w, col0, ncols, epilogue, out_dtype, row_extras=(), col_extras=(), tm=1024, tn=1024, widen=1):
    T, K = h.shape
    tn = min(tn, ncols)
    tm = min(tm, T)
    cb = col0 // tn
    otn = widen * tn
    in_specs = [
        pl.BlockSpec((tm, K), lambda j, i: (i, 0)),
        pl.BlockSpec((K, tn), lambda j, i: (0, cb + j)),
    ]
    for e in row_extras:
        in_specs.append(pl.BlockSpec((tm, e.shape[1]), lambda j, i: (i, 0)))
    for e in col_extras:
        in_specs.append(pl.BlockSpec((1, tn), lambda j, i: (0, j)))
    vmem = 2 * (_nbytes((tm, K), h.dtype) + _nbytes((K, tn), w.dtype) + _nbytes((tm, otn), out_dtype))
    vmem += 3 * _nbytes((tm, otn), F32) + _nbytes((K, tn), BF16)
    return pl.pallas_call(
        functools.partial(_proj_kernel, epilogue=epilogue),
        out_shape=jax.ShapeDtypeStruct((T, widen * ncols), out_dtype),
        grid=(ncols // tn, T // tm),
        in_specs=in_specs,
        out_specs=pl.BlockSpec((tm, otn), lambda j, i: (i, j)),
        scratch_shapes=[pltpu.VMEM((K, tn), BF16)],
        compiler_params=_cparams(2, vmem),
        name="proj",
    )(h, w, *row_extras, *col_extras)


MOBA_VT_ROWS = MOBA_HEAD_DIM + 16


def _proj_t_kernel(h_ref, w_ref, *refs, epilogue, ones_rows):
    *extra, o_ref, wb_ref = refs
    BS, HD = MOBA_BLOCK, MOBA_HEAD_DIM
    _cast_weight_once(w_ref, wb_ref, 0)
    acc = epilogue(jnp.dot(h_ref[...], wb_ref[...], preferred_element_type=F32), *extra)
    ones = jnp.ones((ones_rows, BS), F32) if ones_rows else None
    for b in range(acc.shape[0] // BS):
        parts = []
        for hh in range(acc.shape[1] // HD):
            parts.append(acc[b * BS:(b + 1) * BS, hh * HD:(hh + 1) * HD].T)
            if ones_rows:
                parts.append(ones)
        o_ref[b] = jnp.concatenate(parts, axis=0).astype(o_ref.dtype)


def _proj_t(h, w, col0, epilogue, row_extras=(), ones_rows=0, tm=1024):
    T, K = h.shape
    tn = MOBA_WIDTH
    tm = min(tm, T)
    cb = col0 // tn
    rows = MOBA_HEADS * (MOBA_HEAD_DIM + ones_rows)
    in_specs = [pl.BlockSpec((tm, K), lambda i: (i, 0)), pl.BlockSpec((K, tn), lambda i: (0, cb))]
    for e in row_extras:
        in_specs.append(pl.BlockSpec((tm, e.shape[1]), lambda i: (i, 0)))
    vmem = 2 * (_nbytes((tm, K), h.dtype) + _nbytes((K, tn), w.dtype) + _nbytes((tm, 2 * tn), BF16))
    vmem += 4 * _nbytes((tm, tn), F32) + _nbytes((K, tn), BF16)
    return pl.pallas_call(
        functools.partial(_proj_t_kernel, epilogue=epilogue, ones_rows=ones_rows),
        out_shape=jax.ShapeDtypeStruct((T // MOBA_BLOCK, rows, MOBA_BLOCK), BF16),
        grid=(T // tm,),
        in_specs=in_specs,
        out_specs=pl.BlockSpec((tm // MOBA_BLOCK, rows, MOBA_BLOCK), lambda i: (i, 0, 0)),
        scratch_shapes=[pltpu.VMEM((K, tn), BF16)],
        compiler_params=_cparams(1, vmem),
        name="proj_t",
    )(h, w, *row_extras)


def _hgrn_kernel(q_ref, g_ref, v_ref, og_ref, ng_ref, o_ref, st_ref):
    W, HD, NH, C, S = HGRN_WIDTH, HGRN_HEAD_DIM, HGRN_HEADS, HGRN_CHUNK, V7X_SUBLANES
    J = C // S

    @pl.when(pl.program_id(0) == 0)
    def _():
        st_ref[...] = jnp.zeros_like(st_ref)

    def r3(x):
        return x.astype(F32).reshape(J, S, W)

    def sub_bcast(x3, r):
        return jnp.broadcast_to(x3[:, r:r + 1, :], x3.shape)

    g3, q3, v3 = r3(g_ref[...]), r3(q_ref[...]), r3(v_ref[...])
    sub = lax.broadcasted_iota(I32, (1, S, W), 1)

    c3 = g3
    for s in (1, 2, 4):
        c3 = c3 + jnp.where(sub >= s, pltpu.roll(c3, s, axis=1), 0.0)
    run = jnp.zeros((1, 1, W), F32)
    carry = []
    for j in range(J):
        carry.append(run)
        run = run + c3[j:j + 1, S - 1:S, :]
    b3 = c3 + jnp.concatenate(carry, axis=0)
    bC = run

    k3 = 1.0 - jnp.exp(g3)
    qe3 = q3 * jnp.exp(b3)
    ks3 = k3 * jnp.exp(bC - b3)

    levels = [(0, q3, k3)]
    ref1 = jnp.where(sub % 2 == 0, b3, pltpu.roll(b3, 1, axis=1))
    ref2 = jnp.where(sub < 4, sub_bcast(b3, 1), sub_bcast(b3, 5))
    ref4 = sub_bcast(b3, 3)
    for lvl, (ref, upper) in enumerate(((ref1, sub % 2 == 1), (ref2, sub % 4 >= 2), (ref4, sub >= 4)), start=1):
        e = jnp.exp(-jnp.abs(b3 - ref))
        levels.append((lvl, jnp.where(upper, q3 * e, 0.0), jnp.where(upper, 0.0, k3 * e)))
    zero_group = jnp.zeros((1, S, W), F32)
    for lvl, half in enumerate((1, 2, 4, 8), start=4):
        qparts, kparts = [], []
        for j in range(J):
            jr = (j // (2 * half)) * (2 * half) + half - 1
            ref = b3[jr:jr + 1, S - 1:S, :]
            if (j % (2 * half)) >= half:
                qparts.append(q3[j:j + 1] * jnp.exp(b3[j:j + 1] - ref))
                kparts.append(zero_group)
            else:
                qparts.append(zero_group)
                kparts.append(k3[j:j + 1] * jnp.exp(ref - b3[j:j + 1]))
        levels.append((lvl, jnp.concatenate(qparts, axis=0), jnp.concatenate(kparts, axis=0)))

    tr = lax.broadcasted_iota(I32, (C, C), 0)
    tc = lax.broadcasted_iota(I32, (C, C), 1)
    xr = tr ^ tc
    code = jnp.zeros((C, C), I32)
    for lvl in range(1, 8):
        code = jnp.where(xr >= (1 << (lvl - 1)), lvl, code)
    code = jnp.where(tc > tr, -1, code)

    def mat(x3, h):
        return x3.reshape(C, W)[:, h * HD:(h + 1) * HD].astype(BF16)

    nt = (((1,), (1,)), ((), ()))
    tn = (((0,), (0,)), ((), ()))
    ebc = jnp.exp(bC).reshape(1, W)
    ng = ng_ref[...]
    for h in range(NH):
        a_mat = jnp.zeros((C, C), F32)
        for lvl, qr, kr in levels:
            s = lax.dot_general(mat(qr, h), mat(kr, h), nt, preferred_element_type=F32)
            a_mat = jnp.where(code == lvl, s, a_mat)
        vh = mat(v3, h)
        st = st_ref[h]
        o = jnp.dot(a_mat.astype(BF16), vh, preferred_element_type=F32)
        o = o + lax.dot_general(mat(qe3, h), st.astype(BF16), nt, preferred_element_type=F32)
        o = _rms(o, ng) * og_ref[:, h * HD:(h + 1) * HD].astype(F32)
        o_ref[:, h * HD:(h + 1) * HD] = o.astype(o_ref.dtype)
        st_ref[h] = st * ebc[:, h * HD:(h + 1) * HD] + lax.dot_general(
            vh, mat(ks3, h), tn, preferred_element_type=F32)


def _hgrn(q, logf, v, og, norm_g):
    T, W = q.shape
    C = HGRN_CHUNK
    blk = pl.BlockSpec((C, W), lambda c: (c, 0))
    vmem = 64 * _nbytes((C, W), F32)
    return pl.pallas_call(
        _hgrn_kernel,
        out_shape=jax.ShapeDtypeStruct((T, W), BF16),
        grid=(T // C,),
        in_specs=[blk, blk, blk, blk, pl.BlockSpec((1, HGRN_HEAD_DIM), lambda c: (0, 0))],
        out_specs=blk,
        scratch_shapes=[pltpu.VMEM((HGRN_HEADS, HGRN_HEAD_DIM, HGRN_HEAD_DIM), F32)],
        compiler_params=_cparams(1, vmem),
        name="hgrn",
    )(q, logf, v, og, norm_g.reshape(1, HGRN_HEAD_DIM))


def _moba_kernel(qt_ref, k_ref, vt_ref, o_ref, km_ref):
    BS, HD, VR = MOBA_BLOCK, MOBA_HEAD_DIM, MOBA_VT_ROWS
    T = k_ref.shape[0]
    NB = T // BS
    G = o_ref.shape[1] // HD
    cur = pl.program_id(1)

    @pl.when(cur == 0)
    def _():
        for g in range(G):
            kf = k_ref[:, 2 * g * HD:(2 * g + 1) * HD].astype(F32).reshape(NB, BS, HD)
            km_ref[g] = jnp.sum(kf, axis=1) * (1.0 / BS)

    blk = lax.broadcasted_iota(I32, (NB, BS), 0)
    key = lax.broadcasted_iota(I32, (BS, BS), 0)
    qry = lax.broadcasted_iota(I32, (BS, BS), 1)
    pad = jnp.zeros((V7X_LANES - NB, BS), F32)
    r0 = pl.multiple_of(cur * BS, BS)
    qts = [qt_ref[0, g * HD:(g + 1) * HD, :] for g in range(G)]
    gts = [jnp.dot(km_ref[g].astype(BF16), qts[g], preferred_element_type=F32) for g in range(G)]
    s_own = [jnp.dot(k_ref[pl.ds(r0, BS), 2 * g * HD:(2 * g + 1) * HD], qts[g], preferred_element_type=F32)
             for g in range(G)]
    qcs = []
    for g in range(G):
        gt = jnp.where(blk < cur, gts[g], NEG_INF)
        sel = jnp.zeros((NB, BS), jnp.bool_)
        for _ in range(MOBA_TOPK):
            mx = jnp.max(gt, axis=0, keepdims=True)
            idx = jnp.min(jnp.where(gt == mx, blk, NB), axis=0, keepdims=True)
            pick = (blk == idx) & (mx > 0.5 * NEG_INF)
            sel = sel | pick
            gt = jnp.where(pick, NEG_INF, gt)
        pen = jnp.concatenate([jnp.where(sel, 0.0, NEG_INF), pad], axis=0).astype(BF16)
        qcs.append(jnp.concatenate([qts[g], pen], axis=0))

    def scores(n):
        rn = pl.multiple_of(n * BS, BS)
        return tuple(jnp.dot(k_ref[pl.ds(rn, BS), 2 * g * HD:(2 * g + 2) * HD], qcs[g], preferred_element_type=F32)
                     for g in range(G))

    ms, ps = [], []
    for g in range(G):
        s = jnp.where(key <= qry, s_own[g], NEG_INF)
        m0 = jnp.max(s, axis=0, keepdims=True)
        ms.append(m0)
        ps.append(jnp.exp(s - m0).astype(BF16))
    accs = [jnp.dot(vt_ref[cur, g * VR:(g + 1) * VR, :], ps[g], preferred_element_type=F32)
            for g in range(G)]

    def body(n, carry):
        ms, accs = carry
        sns = scores(n)
        new_ms, alphas, pns = [], [], []
        for g in range(G):
            m_new = jnp.maximum(ms[g], jnp.max(sns[g], axis=0, keepdims=True))
            alphas.append(jnp.exp(ms[g] - m_new))
            pns.append(jnp.exp(sns[g] - m_new).astype(BF16))
            new_ms.append(m_new)
        pvs = [jnp.dot(vt_ref[n, g * VR:(g + 1) * VR, :], pns[g], preferred_element_type=F32) for g in range(G)]
        return tuple(new_ms), tuple(alphas[g] * accs[g] + pvs[g] for g in range(G))

    _, accs = lax.fori_loop(0, cur, body, (tuple(ms), tuple(accs)))
    for g in range(G):
        ot = accs[g][:HD, :] / accs[g][HD:HD + 1, :]
        o_ref[:, g * HD:(g + 1) * HD] = ot.T.astype(o_ref.dtype)


def _moba(mqt, mk_aug, mvt, heads_per_step=4):
    T = mk_aug.shape[0]
    BS, HD, G, VR = MOBA_BLOCK, MOBA_HEAD_DIM, heads_per_step, MOBA_VT_ROWS
    NB = T // BS
    vmem = _nbytes((T, 2 * G * HD), BF16) + _nbytes((NB, G * VR, BS), BF16) + 8 * _nbytes((BS, G * HD), BF16)
    vmem += 16 * G * _nbytes((BS, BS), F32)
    resident = pl.Buffered(1)
    return pl.pallas_call(
        _moba_kernel,
        out_shape=jax.ShapeDtypeStruct((T, MOBA_WIDTH), BF16),
        grid=(MOBA_HEADS // G, NB),
        in_specs=[
            pl.BlockSpec((1, G * HD, BS), lambda h, i: (i, h, 0)),
            pl.BlockSpec((T, 2 * G * HD), lambda h, i: (0, h), pipeline_mode=resident),
            pl.BlockSpec((NB, G * VR, BS), lambda h, i: (0, h, 0), pipeline_mode=resident),
        ],
        out_specs=pl.BlockSpec((BS, G * HD), lambda h, i: (i, h)),
        scratch_shapes=[pltpu.VMEM((G, NB, HD), F32)],
        compiler_params=_cparams(2, vmem),
        name="moba",
    )(mqt, mk_aug, mvt)


def _merge_kernel(oh_ref, om_ref, ga_ref, gb_ref, wh_ref, wm_ref, o_ref):
    a = jnp.dot(oh_ref[...], wh_ref[...], preferred_element_type=F32)
    b = jnp.dot(om_ref[...], wm_ref[...], preferred_element_type=F32)
    o_ref[...] = (ga_ref[...].astype(F32) * a + gb_ref[...].astype(F32) * b).astype(o_ref.dtype)


def _merge(o_hgrn, o_moba, gates, w_up_hgrn, w_up_moba, tm=512):
    T = o_hgrn.shape[0]
    D = D_MODEL
    vmem = 2 * (_nbytes((tm, HGRN_WIDTH), F32) + _nbytes((tm, MOBA_WIDTH), BF16) + 3 * _nbytes((tm, D), BF16)
                + 2 * _nbytes((HGRN_WIDTH, D), BF16)) + 3 * _nbytes((tm, D), F32)
    return pl.pallas_call(
        _merge_kernel,
        out_shape=jax.ShapeDtypeStruct((T, D), BF16),
        grid=(T // tm,),
        in_specs=[
            pl.BlockSpec((tm, HGRN_WIDTH), lambda i: (i, 0)),
            pl.BlockSpec((tm, MOBA_WIDTH), lambda i: (i, 0)),
            pl.BlockSpec((tm, D), lambda i: (i, 0)),
            pl.BlockSpec((tm, D), lambda i: (i, 1)),
            pl.BlockSpec((HGRN_WIDTH, D), lambda i: (0, 0)),
            pl.BlockSpec((MOBA_WIDTH, D), lambda i: (0, 0)),
        ],
        out_specs=pl.BlockSpec((tm, D), lambda i: (i, 0)),
        compiler_params=_cparams(1, vmem),
        name="merge",
    )(o_hgrn, o_moba, gates, gates, w_up_hgrn, w_up_moba)


def _outproj_kernel(m_ref, x_ref, w_ref, g_ref, x1_ref, h2_ref):
    x1 = x_ref[...] + jnp.dot(m_ref[...], w_ref[...], preferred_element_type=F32)
    x1_ref[...] = x1
    h2_ref[...] = _rms(x1, g_ref[...])


def _outproj(merged, x, w_out, g_ffn, tm=256):
    T, D = x.shape
    vmem = 2 * (_nbytes((tm, D), BF16) + 3 * _nbytes((tm, D), F32) + _nbytes((D, D), BF16)) + 2 * _nbytes((tm, D), F32)
    return pl.pallas_call(
        _outproj_kernel,
        out_shape=(jax.ShapeDtypeStruct((T, D), F32), jax.ShapeDtypeStruct((T, D), F32)),
        grid=(T // tm,),
        in_specs=[
            pl.BlockSpec((tm, D), lambda i: (i, 0)),
            pl.BlockSpec((tm, D), lambda i: (i, 0)),
            pl.BlockSpec((D, D), lambda i: (0, 0)),
            pl.BlockSpec((1, D), lambda i: (0, 0)),
        ],
        out_specs=(pl.BlockSpec((tm, D), lambda i: (i, 0)), pl.BlockSpec((tm, D), lambda i: (i, 0))),
        compiler_params=_cparams(1, vmem),
        name="outproj",
    )(merged, x, w_out, g_ffn.reshape(1, D))


def _router_kernel(h_ref, w_ref, info_ref, cnt_ref, carry_ref):
    tm = h_ref.shape[0]

    @pl.when(pl.program_id(0) == 0)
    def _():
        carry_ref[...] = jnp.zeros_like(carry_ref)

    logits = jnp.dot(h_ref[...].astype(BF16), w_ref[...], preferred_element_type=F32)
    lane = lax.broadcasted_iota(I32, (tm, V7X_LANES), 1)
    is_g = lane < N_GROUPS
    gl = jnp.where(is_g, logits, NEG_INF)
    gmax = jnp.max(gl, axis=1, keepdims=True)
    g_sel = jnp.min(jnp.where(gl == gmax, lane, V7X_LANES), axis=1, keepdims=True)
    gsum = jnp.sum(jnp.where(is_g, jnp.exp(gl - gmax), 0.0), axis=1, keepdims=True)
    p_group = 1.0 / gsum
    lo = N_GROUPS + EXPERTS_PER_GROUP * g_sel
    emask = (lane >= lo) & (lane < lo + EXPERTS_PER_GROUP)
    el = jnp.where(emask, logits, NEG_INF)
    e1 = jnp.max(el, axis=1, keepdims=True)
    i1 = jnp.min(jnp.where((el == e1) & emask, lane, V7X_LANES), axis=1, keepdims=True)
    emask2 = emask & (lane != i1)
    el2 = jnp.where(emask2, logits, NEG_INF)
    e2 = jnp.max(el2, axis=1, keepdims=True)
    i2 = jnp.min(jnp.where((el2 == e2) & emask2, lane, V7X_LANES), axis=1, keepdims=True)
    r = jnp.exp(e2 - e1)
    w1 = p_group / (1.0 + r)
    w2 = p_group * r / (1.0 + r)
    eid1 = i1 - N_GROUPS
    eid2 = i2 - N_GROUPS
    oh1 = jnp.where(lane == eid1, 1.0, 0.0)
    oh2 = jnp.where(lane == eid2, 1.0, 0.0)
    cnt = oh1 + oh2
    tri = jnp.where(lax.broadcasted_iota(I32, (tm, tm), 0) > lax.broadcasted_iota(I32, (tm, tm), 1), 1.0, 0.0)
    before = jnp.dot(tri.astype(BF16), cnt.astype(BF16), preferred_element_type=F32) + carry_ref[...]
    rank1 = jnp.sum(oh1 * before, axis=1, keepdims=True)
    rank2 = jnp.sum(oh2 * before, axis=1, keepdims=True)
    carry_ref[...] = carry_ref[...] + jnp.sum(cnt, axis=0, keepdims=True)
    info = jnp.zeros((tm, V7X_LANES), F32)
    for k, val in enumerate((eid1.astype(F32), eid2.astype(F32), w1, w2, rank1, rank2)):
        info = jnp.where(lane == k, val, info)
    info_ref[...] = info
    cnt_ref[...] = carry_ref[...]


def _router(h2, w_router, tm=256):
    T, D = h2.shape
    vmem = 2 * (_nbytes((tm, D), F32) + _nbytes((D, V7X_LANES), BF16)) + 16 * _nbytes((tm, V7X_LANES), F32) + (1 << 22)
    return pl.pallas_call(
        _router_kernel,
        out_shape=(jax.ShapeDtypeStruct((T, V7X_LANES), F32), jax.ShapeDtypeStruct((1, V7X_LANES), F32)),
        grid=(T // tm,),
        in_specs=[pl.BlockSpec((tm, D), lambda i: (i, 0)), pl.BlockSpec((D, V7X_LANES), lambda i: (0, 0))],
        out_specs=(pl.BlockSpec((tm, V7X_LANES), lambda i: (i, 0)), pl.BlockSpec((1, V7X_LANES), lambda i: (0, 0))),
        scratch_shapes=[pltpu.VMEM((1, V7X_LANES), F32)],
        compiler_params=_cparams(1, vmem),
        name="router",
    )(h2, w_router)


def _row_copy(src_ref, src_row, dst_ref, dst_row, sem):
    return pltpu.make_async_copy(src_ref.at[pl.ds(src_row, 1), :], dst_ref.at[pl.ds(dst_row, 1), :], sem)


ROW_DMA_UNROLL = 8


def _dispatch_kernel(dest_ref, pend_ref, h_ref, xs_ref, zero_ref, sem):
    tm = h_ref.shape[0]
    step = pl.program_id(0)
    base = step * tm

    @pl.when(step == 0)
    def _():
        zero_ref[...] = jnp.zeros_like(zero_ref)

        def tail(e):
            return pltpu.make_async_copy(
                zero_ref, xs_ref.at[pl.ds(pl.multiple_of(pend_ref[e] - MOE_ROWS, MOE_ROWS), MOE_ROWS), :], sem)

        def nonempty(e):
            return pend_ref[e] > (pend_ref[e - 1] if e else 0)

        def unused(b):
            return pltpu.make_async_copy(
                zero_ref, xs_ref.at[pl.ds(pl.multiple_of(b * MOE_ROWS, MOE_ROWS), MOE_ROWS), :], sem)

        first_unused = pend_ref[N_EXPERTS - 1] // MOE_ROWS
        n_blocks = xs_ref.shape[0] // MOE_ROWS
        for e in range(N_EXPERTS):
            pl.when(nonempty(e))(lambda e=e: tail(e).start())
        lax.fori_loop(first_unused, n_blocks, lambda b, c: (unused(b).start(), c)[1], 0)
        for e in range(N_EXPERTS):
            pl.when(nonempty(e))(lambda e=e: tail(e).wait())
        lax.fori_loop(first_unused, n_blocks, lambda b, c: (unused(b).wait(), c)[1], 0)

    def issue(rb, c):
        for u in range(ROW_DMA_UNROLL):
            r = rb * ROW_DMA_UNROLL + u
            for k in range(TOPK_IN_GROUP):
                _row_copy(h_ref, r, xs_ref, dest_ref[(base + r) * TOPK_IN_GROUP + k], sem).start()
        return c

    lax.fori_loop(0, tm // ROW_DMA_UNROLL, issue, 0)
    for _ in range(tm * TOPK_IN_GROUP):
        _row_copy(h_ref, 0, xs_ref, 0, sem).wait()


def _dispatch(dest_flat, pad_end, h2, m_pad, tm=256):
    T, D = h2.shape
    return pl.pallas_call(
        _dispatch_kernel,
        out_shape=jax.ShapeDtypeStruct((m_pad, D), F32),
        grid_spec=pltpu.PrefetchScalarGridSpec(
            num_scalar_prefetch=2,
            grid=(T // tm,),
            in_specs=[pl.BlockSpec((tm, D), lambda i, d, pe: (i, 0))],
            out_specs=pl.BlockSpec(memory_space=pl.ANY),
            scratch_shapes=[pltpu.VMEM((MOE_ROWS, D), F32), pltpu.SemaphoreType.DMA(())],
        ),
        compiler_params=_cparams(1, 6 * _nbytes((tm, D), F32)),
        name="dispatch",
    )(dest_flat, pad_end, h2)


def _expert_kernel(be_ref, nu_ref, xs_ref, wg_ref, wu_ref, wd_ref, y_ref, wgb_ref, wub_ref, wdb_ref):
    b = pl.program_id(0)
    used = b < nu_ref[0]

    @pl.when(used & ((b == 0) | (be_ref[b] != be_ref[jnp.maximum(b - 1, 0)])))
    def _():
        wgb_ref[...] = wg_ref[0].astype(BF16)
        wub_ref[...] = wu_ref[0].astype(BF16)
        wdb_ref[...] = wd_ref[0].astype(BF16)

    @pl.when(used)
    def _():
        x = xs_ref[...].astype(BF16)
        a = jnp.dot(x, wgb_ref[...], preferred_element_type=F32)
        u = jnp.dot(x, wub_ref[...], preferred_element_type=F32)
        hm = (a * jax.nn.sigmoid(a) * u).astype(BF16)
        y_ref[...] = jnp.dot(hm, wdb_ref[...], preferred_element_type=F32)

    @pl.when(jnp.logical_not(used))
    def _():
        y_ref[...] = jnp.zeros_like(y_ref)


def _experts(block_expert, n_used, xs, w_gate, w_up, w_down):
    m_pad, D = xs.shape
    R, Fd = MOE_ROWS, D_EXPERT
    vmem = 2 * (2 * _nbytes((R, D), F32) + 3 * _nbytes((D, Fd), F32)) + 3 * _nbytes((D, Fd), BF16)
    vmem += 4 * _nbytes((R, D), F32)
    return pl.pallas_call(
        _expert_kernel,
        out_shape=jax.ShapeDtypeStruct((m_pad, D), F32),
        grid_spec=pltpu.PrefetchScalarGridSpec(
            num_scalar_prefetch=2,
            grid=(m_pad // R,),
            in_specs=[
                pl.BlockSpec((R, D), lambda b, be, nu: (jnp.minimum(b, jnp.maximum(nu[0] - 1, 0)), 0)),
                pl.BlockSpec((1, D, Fd), lambda b, be, nu: (be[b], 0, 0)),
                pl.BlockSpec((1, D, Fd), lambda b, be, nu: (be[b], 0, 0)),
                pl.BlockSpec((1, Fd, D), lambda b, be, nu: (be[b], 0, 0)),
            ],
            out_specs=pl.BlockSpec((R, D), lambda b, be, nu: (b, 0)),
            scratch_shapes=[pltpu.VMEM((D, Fd), BF16), pltpu.VMEM((D, Fd), BF16), pltpu.VMEM((Fd, D), BF16)],
        ),
        compiler_params=_cparams(1, vmem),
        name="experts",
    )(block_expert, n_used, xs, w_gate, w_up, w_down)


def _combine_kernel(dest_ref, x1_ref, info_ref, yb_ref, p_ref, gp_ref, wpg_ref, wpp_ref, gf_ref, o_ref, ybuf, sem):
    tm = x1_ref.shape[0]
    base = pl.program_id(0) * tm

    def issue(rb, c):
        for u in range(ROW_DMA_UNROLL):
            r = rb * ROW_DMA_UNROLL + u
            for k in range(TOPK_IN_GROUP):
                _row_copy(yb_ref, dest_ref[(base + r) * TOPK_IN_GROUP + k], ybuf.at[k], r, sem).start()
        return c

    lax.fori_loop(0, tm // ROW_DMA_UNROLL, issue, 0)
    for _ in range(tm * TOPK_IN_GROUP):
        _row_copy(yb_ref, 0, ybuf.at[0], 0, sem).wait()
    info = info_ref[...]
    x2 = x1_ref[...] + info[:, 2:3] * ybuf[0] + info[:, 3:4] * ybuf[1]
    hp = _rms(x2, gp_ref[...]).astype(BF16)
    z = jnp.dot(hp, wpg_ref[...], preferred_element_type=F32)
    pp = jnp.dot(p_ref[...].astype(BF16), wpp_ref[...], preferred_element_type=F32)
    x3 = x2 + jax.nn.sigmoid(z) * pp
    o_ref[...] = _rms(x3, gf_ref[...])


def _combine(dest_flat, x1, info, yb, p, g_ple, w_ple_gate, w_ple_proj, g_final, tm=256):
    T, D = x1.shape
    vmem = 2 * (2 * _nbytes((tm, D), F32) + _nbytes((D, D), BF16) + _nbytes((PLE_DIM, D), BF16)
                + _nbytes((tm, PLE_DIM), F32)) + 8 * _nbytes((tm, D), F32)
    return pl.pallas_call(
        _combine_kernel,
        out_shape=jax.ShapeDtypeStruct((T, D), F32),
        grid_spec=pltpu.PrefetchScalarGridSpec(
            num_scalar_prefetch=1,
            grid=(T // tm,),
            in_specs=[
                pl.BlockSpec((tm, D), lambda i, d: (i, 0)),
                pl.BlockSpec((tm, V7X_LANES), lambda i, d: (i, 0)),
                pl.BlockSpec(memory_space=pl.ANY),
                pl.BlockSpec((tm, PLE_DIM), lambda i, d: (i, 0)),
                pl.BlockSpec((1, D), lambda i, d: (0, 0)),
                pl.BlockSpec((D, D), lambda i, d: (0, 0)),
                pl.BlockSpec((PLE_DIM, D), lambda i, d: (0, 0)),
                pl.BlockSpec((1, D), lambda i, d: (0, 0)),
            ],
            out_specs=pl.BlockSpec((tm, D), lambda i, d: (i, 0)),
            scratch_shapes=[pltpu.VMEM((TOPK_IN_GROUP, tm, D), F32), pltpu.SemaphoreType.DMA(())],
        ),
        compiler_params=_cparams(1, vmem),
        name="combine",
    )(dest_flat, x1, info, yb, p, g_ple.reshape(1, D), w_ple_gate, w_ple_proj, g_final.reshape(1, D))


def _rope_tables(T):
    half = MOBA_HEAD_DIM // 2
    inv_freq = ROPE_THETA ** (-jnp.arange(half, dtype=F32) / half)
    ang = jnp.arange(T, dtype=F32)[:, None] * inv_freq[None, :]
    cos, sin = jnp.cos(ang), jnp.sin(ang)
    return jnp.concatenate([cos, cos], axis=1), jnp.concatenate([-sin, sin], axis=1)


def _mixers(x2d, g_mix, w_in, lb, hgrn_norm_g):
    T = x2d.shape[0]
    W = HGRN_WIDTH
    h = _rmsnorm(x2d, g_mix, BF16)
    w = w_in
    cos, sin = _rope_tables(T)
    log_lb = jnp.log(lb).reshape(1, W)
    log_1m = jnp.log1p(-lb).reshape(1, W)
    hq = _proj(h, w, 0 * W, W, _ep_silu, BF16)
    logf = _proj(h, w, 1 * W, W, _ep_logf, F32, col_extras=(log_lb, log_1m))
    hi = _proj(h, w, 2 * W, W, _ep_identity, BF16)
    hog = _proj(h, w, 3 * W, W, _ep_silu, BF16)
    scale = MOBA_HEAD_DIM ** -0.5
    mqt = _proj_t(h, w, 4 * W, functools.partial(_ep_rope, scale=scale), row_extras=(cos, sin))
    blk = jnp.arange(T, dtype=I32)[:, None] // MOBA_BLOCK
    blk_onehot = (blk == jnp.arange(V7X_LANES, dtype=I32)[None, :]).astype(F32)
    mk = _proj(h, w, 5 * W, W, _ep_rope_aug, BF16, row_extras=(cos, sin, blk_onehot), widen=2)
    mvt = _proj_t(h, w, 6 * W, _ep_identity, ones_rows=MOBA_VT_ROWS - MOBA_HEAD_DIM)
    gates = _proj(h, w, 7 * W, 2 * D_MODEL, _ep_sigmoid, BF16)
    o_hgrn = _hgrn(hq, logf, hi, hog, hgrn_norm_g)
    o_moba = _moba(mqt, mk, mvt)
    return o_hgrn, o_moba, gates


def _moe_plan(info, cnt, T):
    R = MOE_ROWS
    eid = info[:, 0:TOPK_IN_GROUP].astype(I32)
    rank = info[:, 4:4 + TOPK_IN_GROUP].astype(I32)
    counts = cnt[0, :N_EXPERTS].astype(I32)
    padded = (counts + R - 1) // R * R
    pad_end = jnp.cumsum(padded)
    pad_start = pad_end - padded
    dest = (pad_start[eid] + rank).reshape(-1)
    n_blocks = (T * TOPK_IN_GROUP) // R + N_EXPERTS
    block_expert = jnp.minimum(
        jnp.searchsorted(pad_end, jnp.arange(n_blocks, dtype=I32) * R, side="right"), N_EXPERTS - 1).astype(I32)
    n_used = (pad_end[-1:] // R).astype(I32)
    return dest, pad_end.astype(I32), block_expert, n_used, n_blocks * R


def kernel(x, p, norm_mix_g, w_in, hgrn_lb_raw, hgrn_norm_g, w_up_hgrn, w_up_moba, w_out, norm_ffn_g,
           w_router_group, w_router_expert, w_exp_gate, w_exp_up, w_exp_down, norm_ple_g, w_ple_gate,
           w_ple_proj, norm_final_g):
    B, T, D = x.shape
    assert B == 1 and D == D_MODEL and w_in.shape[0] == 1 and T % (4 * MOBA_BLOCK) == 0
    lower_bounds = jnp.cumsum(jax.nn.softmax(hgrn_lb_raw.astype(F32), axis=0), axis=0)
    x2d = x.reshape(T, D)
    o_hgrn, o_moba, gates = _mixers(x2d, norm_mix_g[0], w_in[0], lower_bounds[0], hgrn_norm_g[0])
    merged = _merge(o_hgrn, o_moba, gates, w_up_hgrn[0].astype(BF16), w_up_moba[0].astype(BF16))
    x1, h2 = _outproj(merged, x2d, w_out[0].astype(BF16), norm_ffn_g[0])
    w_router = jnp.pad(jnp.concatenate([w_router_group[0], w_router_expert[0]], axis=1),
                       ((0, 0), (0, V7X_LANES - N_GROUPS - N_EXPERTS))).astype(BF16)
    info, cnt = _router(h2, w_router)
    dest, pad_end, block_expert, n_used, m_pad = _moe_plan(info, cnt, T)
    xs = _dispatch(dest, pad_end, h2, m_pad)
    yb = _experts(block_expert, n_used, xs, w_exp_gate[0], w_exp_up[0], w_exp_down[0])
    out = _combine(dest, x1, info, yb, p[0].reshape(T, PLE_DIM), norm_ple_g[0], w_ple_gate[0].astype(BF16),
                   w_ple_proj[0].astype(BF16), norm_final_g)
    return out.reshape(B, T, D)
```

```python
import functools

import jax
import jax.numpy as jnp
from jax import lax
from jax.experimental import pallas as pl
from jax.experimental.pallas import tpu as pltpu

F32 = jnp.float32
BF16 = jnp.bfloat16
I32 = jnp.int32

D_MODEL = 2048
PLE_DIM = 256
HGRN_HEADS = 8
HGRN_HEAD_DIM = 128
HGRN_WIDTH = HGRN_HEADS * HGRN_HEAD_DIM
MOBA_HEADS = 8
MOBA_HEAD_DIM = 128
MOBA_WIDTH = MOBA_HEADS * MOBA_HEAD_DIM
MOBA_BLOCK = 256
MOBA_TOPK = 3
ROPE_THETA = 10000.0
N_GROUPS = 4
EXPERTS_PER_GROUP = 8
N_EXPERTS = N_GROUPS * EXPERTS_PER_GROUP
TOPK_IN_GROUP = 2
D_EXPERT = 512
EPS = 1e-6
NEG_INF = -1e30

V7X_LANES = 128
V7X_SUBLANES = 8
V7X_VMEM_BUDGET_BYTES = 56 * 1024 * 1024

HGRN_CHUNK = 128
MOE_ROWS = 256


def _cparams(n_grid, vmem_bytes):
    return pltpu.CompilerParams(
        dimension_semantics=("arbitrary",) * n_grid,
        vmem_limit_bytes=int(min(max(vmem_bytes, 16 * 1024 * 1024), V7X_VMEM_BUDGET_BYTES)),
    )


def _nbytes(shape, dtype):
    n = 1
    for s in shape:
        n *= s
    return n * jnp.dtype(dtype).itemsize


def _rms(x, g):
    ms = jnp.mean(x * x, axis=-1, keepdims=True)
    return x * lax.rsqrt(ms + EPS) * g


def _rmsnorm_kernel(x_ref, g_ref, o_ref):
    o_ref[...] = _rms(x_ref[...], g_ref[...]).astype(o_ref.dtype)


def _rmsnorm(x, g, out_dtype, tm=512):
    T, D = x.shape
    return pl.pallas_call(
        _rmsnorm_kernel,
        out_shape=jax.ShapeDtypeStruct((T, D), out_dtype),
        grid=(T // tm,),
        in_specs=[pl.BlockSpec((tm, D), lambda i: (i, 0)), pl.BlockSpec((1, D), lambda i: (0, 0))],
        out_specs=pl.BlockSpec((tm, D), lambda i: (i, 0)),
        compiler_params=_cparams(1, 4 * _nbytes((tm, D), F32)),
        name="rmsnorm",
    )(x, g.reshape(1, D))


def _ep_identity(acc):
    return acc


def _ep_silu(acc):
    return acc * jax.nn.sigmoid(acc)


def _ep_sigmoid(acc):
    return jax.nn.sigmoid(acc)


def _ep_logf(acc, la_ref, lc_ref):
    ls = jnp.minimum(acc, 0.0) - jnp.log(1.0 + jnp.exp(-jnp.abs(acc)))
    u = la_ref[...]
    v = lc_ref[...] + ls
    return jnp.maximum(u, v) + jnp.log(1.0 + jnp.exp(-jnp.abs(u - v)))


def _ep_rope(acc, cos_ref, sin_ref, *, scale):
    cos = cos_ref[...]
    sin = sin_ref[...]
    outs = []
    for hh in range(acc.shape[1] // MOBA_HEAD_DIM):
        a = acc[:, hh * MOBA_HEAD_DIM:(hh + 1) * MOBA_HEAD_DIM]
        r = pltpu.roll(a, MOBA_HEAD_DIM // 2, axis=1)
        outs.append((a * cos + r * sin) * scale)
    return jnp.concatenate(outs, axis=1)


def _ep_rope_aug(acc, cos_ref, sin_ref, oh_ref):
    cos = cos_ref[...]
    sin = sin_ref[...]
    oh = oh_ref[...]
    outs = []
    for hh in range(acc.shape[1] // MOBA_HEAD_DIM):
        a = acc[:, hh * MOBA_HEAD_DIM:(hh + 1) * MOBA_HEAD_DIM]
        outs.append(a * cos + pltpu.roll(a, MOBA_HEAD_DIM // 2, axis=1) * sin)
        outs.append(oh)
    return jnp.concatenate(outs, axis=1)


def _cast_weight_once(w_ref, wb_ref, row_axis):
    @pl.when(pl.program_id(row_axis) == 0)
    def _():
        wb_ref[...] = w_ref[...].astype(wb_ref.dtype)


def _proj_kernel(h_ref, w_ref, *refs, epilogue):
    *extra, o_ref, wb_ref = refs
    _cast_weight_once(w_ref, wb_ref, 1)
    acc = jnp.dot(h_ref[...], wb_ref[...], preferred_element_type=F32)
    o_ref[...] = epilogue(acc, *extra).astype(o_ref.dtype)


def _proj(h, w, col0, ncols, epilogue, out_dtype, row_extras=(), col_extras=(), tm=1024, tn=1024, widen=1):
    T, K = h.shape
    tn = min(tn, ncols)
    tm = min(tm, T)
    cb = col0 // tn
    otn = widen * tn
    in_specs = [
        pl.BlockSpec((tm, K), lambda j, i: (i, 0)),
        pl.BlockSpec((K, tn), lambda j, i: (0, cb + j)),
    ]
    for e in row_extras:
        in_specs.append(pl.BlockSpec((tm, e.shape[1]), lambda j, i: (i, 0)))
    for e in col_extras:
        in_specs.append(pl.BlockSpec((1, tn), lambda j, i: (0, j)))
    vmem = 2 * (_nbytes((tm, K), h.dtype) + _nbytes((K, tn), w.dtype) + _nbytes((tm, otn), out_dtype))
    vmem += 3 * _nbytes((tm, otn), F32) + _nbytes((K, tn), BF16)
    return pl.pallas_call(
        functools.partial(_proj_kernel, epilogue=epilogue),
        out_shape=jax.ShapeDtypeStruct((T, widen * ncols), out_dtype),
        grid=(ncols // tn, T // tm),
        in_specs=in_specs,
        out_specs=pl.BlockSpec((tm, otn), lambda j, i: (i, j)),
        scratch_shapes=[pltpu.VMEM((K, tn), BF16)],
        compiler_params=_cparams(2, vmem),
        name="proj",
    )(h, w, *row_extras, *col_extras)


MOBA_VT_ROWS = MOBA_HEAD_DIM + 16
MOBA_BLOCKS_PER_STEP = 4


def _proj_t_kernel(h_ref, w_ref, *refs, epilogue, ones_rows):
    *extra, o_ref, wb_ref = refs
    BS, HD = MOBA_BLOCK, MOBA_HEAD_DIM
    _cast_weight_once(w_ref, wb_ref, 0)
    acc = epilogue(jnp.dot(h_ref[...], wb_ref[...], preferred_element_type=F32), *extra)
    ones = jnp.ones((ones_rows, BS), F32) if ones_rows else None
    for b in range(acc.shape[0] // BS):
        parts = []
        for hh in range(acc.shape[1] // HD):
            parts.append(acc[b * BS:(b + 1) * BS, hh * HD:(hh + 1) * HD].T)
            if ones_rows:
                parts.append(ones)
        o_ref[b] = jnp.concatenate(parts, axis=0).astype(o_ref.dtype)


def _proj_t(h, w, col0, epilogue, row_extras=(), ones_rows=0, tm=1024):
    T, K = h.shape
    tn = MOBA_WIDTH
    tm = min(tm, T)
    cb = col0 // tn
    rows = MOBA_HEADS * (MOBA_HEAD_DIM + ones_rows)
    in_specs = [pl.BlockSpec((tm, K), lambda i: (i, 0)), pl.BlockSpec((K, tn), lambda i: (0, cb))]
    for e in row_extras:
        in_specs.append(pl.BlockSpec((tm, e.shape[1]), lambda i: (i, 0)))
    vmem = 2 * (_nbytes((tm, K), h.dtype) + _nbytes((K, tn), w.dtype) + _nbytes((tm, 2 * tn), BF16))
    vmem += 4 * _nbytes((tm, tn), F32) + _nbytes((K, tn), BF16)
    return pl.pallas_call(
        functools.partial(_proj_t_kernel, epilogue=epilogue, ones_rows=ones_rows),
        out_shape=jax.ShapeDtypeStruct((T // MOBA_BLOCK, rows, MOBA_BLOCK), BF16),
        grid=(T // tm,),
        in_specs=in_specs,
        out_specs=pl.BlockSpec((tm // MOBA_BLOCK, rows, MOBA_BLOCK), lambda i: (i, 0, 0)),
        scratch_shapes=[pltpu.VMEM((K, tn), BF16)],
        compiler_params=_cparams(1, vmem),
        name="proj_t",
    )(h, w, *row_extras)


def _hgrn_kernel(q_ref, g_ref, v_ref, og_ref, ng_ref, o_ref, st_ref):
    W, HD, NH, C, S = HGRN_WIDTH, HGRN_HEAD_DIM, HGRN_HEADS, HGRN_CHUNK, V7X_SUBLANES
    J = C // S

    @pl.when(pl.program_id(0) == 0)
    def _():
        st_ref[...] = jnp.zeros_like(st_ref)

    def r3(x):
        return x.astype(F32).reshape(J, S, W)

    def sub_bcast(x3, r):
        return jnp.broadcast_to(x3[:, r:r + 1, :], x3.shape)

    g3, q3, v3 = r3(g_ref[...]), r3(q_ref[...]), r3(v_ref[...])
    sub = lax.broadcasted_iota(I32, (1, S, W), 1)

    c3 = g3
    for s in (1, 2, 4):
        c3 = c3 + jnp.where(sub >= s, pltpu.roll(c3, s, axis=1), 0.0)
    run = jnp.zeros((1, 1, W), F32)
    carry = []
    for j in range(J):
        carry.append(run)
        run = run + c3[j:j + 1, S - 1:S, :]
    b3 = c3 + jnp.concatenate(carry, axis=0)
    bC = run

    k3 = 1.0 - jnp.exp(g3)
    qe3 = q3 * jnp.exp(b3)
    ks3 = k3 * jnp.exp(bC - b3)

    levels = [(0, q3, k3)]
    ref1 = jnp.where(sub % 2 == 0, b3, pltpu.roll(b3, 1, axis=1))
    ref2 = jnp.where(sub < 4, sub_bcast(b3, 1), sub_bcast(b3, 5))
    ref4 = sub_bcast(b3, 3)
    for lvl, (ref, upper) in enumerate(((ref1, sub % 2 == 1), (ref2, sub % 4 >= 2), (ref4, sub >= 4)), start=1):
        e = jnp.exp(-jnp.abs(b3 - ref))
        levels.append((lvl, jnp.where(upper, q3 * e, 0.0), jnp.where(upper, 0.0, k3 * e)))
    zero_group = jnp.zeros((1, S, W), F32)
    for lvl, half in enumerate((1, 2, 4, 8), start=4):
        qparts, kparts = [], []
        for j in range(J):
            jr = (j // (2 * half)) * (2 * half) + half - 1
            ref = b3[jr:jr + 1, S - 1:S, :]
            if (j % (2 * half)) >= half:
                qparts.append(q3[j:j + 1] * jnp.exp(b3[j:j + 1] - ref))
                kparts.append(zero_group)
            else:
                qparts.append(zero_group)
                kparts.append(k3[j:j + 1] * jnp.exp(ref - b3[j:j + 1]))
        levels.append((lvl, jnp.concatenate(qparts, axis=0), jnp.concatenate(kparts, axis=0)))

    tr = lax.broadcasted_iota(I32, (C, C), 0)
    tc = lax.broadcasted_iota(I32, (C, C), 1)
    xr = tr ^ tc
    code = jnp.zeros((C, C), I32)
    for lvl in range(1, 8):
        code = jnp.where(xr >= (1 << (lvl - 1)), lvl, code)
    code = jnp.where(tc > tr, -1, code)

    def mat(x3, h):
        return x3.reshape(C, W)[:, h * HD:(h + 1) * HD].astype(BF16)

    nt = (((1,), (1,)), ((), ()))
    tn = (((0,), (0,)), ((), ()))
    ebc = jnp.exp(bC).reshape(1, W)
    ng = ng_ref[...]
    for h in range(NH):
        a_mat = jnp.zeros((C, C), F32)
        for lvl, qr, kr in levels:
            s = lax.dot_general(mat(qr, h), mat(kr, h), nt, preferred_element_type=F32)
            a_mat = jnp.where(code == lvl, s, a_mat)
        vh = mat(v3, h)
        st = st_ref[h]
        o = jnp.dot(a_mat.astype(BF16), vh, preferred_element_type=F32)
        o = o + lax.dot_general(mat(qe3, h), st.astype(BF16), nt, preferred_element_type=F32)
        o = _rms(o, ng) * og_ref[:, h * HD:(h + 1) * HD].astype(F32)
        o_ref[:, h * HD:(h + 1) * HD] = o.astype(o_ref.dtype)
        st_ref[h] = st * ebc[:, h * HD:(h + 1) * HD] + lax.dot_general(
            vh, mat(ks3, h), tn, preferred_element_type=F32)


def _hgrn(q, logf, v, og, norm_g):
    T, W = q.shape
    C = HGRN_CHUNK
    blk = pl.BlockSpec((C, W), lambda c: (c, 0))
    vmem = 64 * _nbytes((C, W), F32)
    return pl.pallas_call(
        _hgrn_kernel,
        out_shape=jax.ShapeDtypeStruct((T, W), BF16),
        grid=(T // C,),
        in_specs=[blk, blk, blk, blk, pl.BlockSpec((1, HGRN_HEAD_DIM), lambda c: (0, 0))],
        out_specs=blk,
        scratch_shapes=[pltpu.VMEM((HGRN_HEADS, HGRN_HEAD_DIM, HGRN_HEAD_DIM), F32)],
        compiler_params=_cparams(1, vmem),
        name="hgrn",
    )(q, logf, v, og, norm_g.reshape(1, HGRN_HEAD_DIM))


def _moba_kernel(qt_ref, k_ref, vt_ref, o_ref, km_ref):
    BS, HD, VR = MOBA_BLOCK, MOBA_HEAD_DIM, MOBA_VT_ROWS
    T = k_ref.shape[0]
    NB = T // BS
    G = o_ref.shape[1] // HD
    cur = pl.program_id(1)

    @pl.when(cur == 0)
    def _():
        for g in range(G):
            kf = k_ref[:, 2 * g * HD:(2 * g + 1) * HD].astype(F32).reshape(NB, BS, HD)
            km_ref[g] = jnp.sum(kf, axis=1) * (1.0 / BS)

    blk = lax.broadcasted_iota(I32, (NB, BS), 0)
    key = lax.broadcasted_iota(I32, (BS, BS), 0)
    qry = lax.broadcasted_iota(I32, (BS, BS), 1)
    pad = jnp.zeros((V7X_LANES - NB, BS), F32)
    r0 = pl.multiple_of(cur * BS, BS)
    qts = [qt_ref[0, g * HD:(g + 1) * HD, :] for g in range(G)]
    gts = [jnp.dot(km_ref[g].astype(BF16), qts[g], preferred_element_type=F32) for g in range(G)]
    s_own = [jnp.dot(k_ref[pl.ds(r0, BS), 2 * g * HD:(2 * g + 1) * HD], qts[g], preferred_element_type=F32)
             for g in range(G)]
    qcs = []
    for g in range(G):
        gt = jnp.where(blk < cur, gts[g], NEG_INF)
        sel = jnp.zeros((NB, BS), jnp.bool_)
        for _ in range(MOBA_TOPK):
            mx = jnp.max(gt, axis=0, keepdims=True)
            idx = jnp.min(jnp.where(gt == mx, blk, NB), axis=0, keepdims=True)
            pick = (blk == idx) & (mx > 0.5 * NEG_INF)
            sel = sel | pick
            gt = jnp.where(pick, NEG_INF, gt)
        pen = jnp.concatenate([jnp.where(sel, 0.0, NEG_INF), pad], axis=0).astype(BF16)
        qcs.append(jnp.concatenate([qts[g], pen], axis=0))

    def scores(n):
        rn = pl.multiple_of(n * BS, BS)
        return tuple(jnp.dot(k_ref[pl.ds(rn, BS), 2 * g * HD:(2 * g + 2) * HD], qcs[g], preferred_element_type=F32)
                     for g in range(G))

    ms, ps = [], []
    for g in range(G):
        s = jnp.where(key <= qry, s_own[g], NEG_INF)
        m0 = jnp.max(s, axis=0, keepdims=True)
        ms.append(m0)
        ps.append(jnp.exp(s - m0).astype(BF16))
    accs = [jnp.dot(vt_ref[cur, g * VR:(g + 1) * VR, :], ps[g], preferred_element_type=F32)
            for g in range(G)]

    KB = MOBA_BLOCKS_PER_STEP

    def body(c, carry):
        ms, accs = carry
        r = pl.multiple_of(c * KB * BS, KB * BS)
        sns = [jnp.dot(k_ref[pl.ds(r, KB * BS), 2 * g * HD:(2 * g + 2) * HD], qcs[g], preferred_element_type=F32)
               for g in range(G)]
        new_ms, alphas, pns = [], [], []
        for g in range(G):
            m_new = jnp.maximum(ms[g], jnp.max(sns[g], axis=0, keepdims=True))
            alphas.append(jnp.exp(ms[g] - m_new))
            pns.append(jnp.exp(sns[g] - m_new).astype(BF16))
            new_ms.append(m_new)
        new_accs = []
        for g in range(G):
            pv = alphas[g] * accs[g]
            for j in range(KB):
                pv = pv + jnp.dot(vt_ref[KB * c + j, g * VR:(g + 1) * VR, :], pns[g][j * BS:(j + 1) * BS],
                                  preferred_element_type=F32)
            new_accs.append(pv)
        return tuple(new_ms), tuple(new_accs)

    _, accs = lax.fori_loop(0, (cur + KB - 1) // KB, body, (tuple(ms), tuple(accs)))
    for g in range(G):
        ot = accs[g][:HD, :] / accs[g][HD:HD + 1, :]
        o_ref[:, g * HD:(g + 1) * HD] = ot.T.astype(o_ref.dtype)


def _moba(mqt, mk_aug, mvt, heads_per_step=4):
    T = mk_aug.shape[0]
    BS, HD, G, VR = MOBA_BLOCK, MOBA_HEAD_DIM, heads_per_step, MOBA_VT_ROWS
    NB = T // BS
    vmem = _nbytes((T, 2 * G * HD), BF16) + _nbytes((NB, G * VR, BS), BF16) + 8 * _nbytes((BS, G * HD), BF16)
    vmem += 16 * G * _nbytes((BS, BS), F32)
    resident = pl.Buffered(1)
    return pl.pallas_call(
        _moba_kernel,
        out_shape=jax.ShapeDtypeStruct((T, MOBA_WIDTH), BF16),
        grid=(MOBA_HEADS // G, NB),
        in_specs=[
            pl.BlockSpec((1, G * HD, BS), lambda h, i: (i, h, 0)),
            pl.BlockSpec((T, 2 * G * HD), lambda h, i: (0, h), pipeline_mode=resident),
            pl.BlockSpec((NB, G * VR, BS), lambda h, i: (0, h, 0), pipeline_mode=resident),
        ],
        out_specs=pl.BlockSpec((BS, G * HD), lambda h, i: (i, h)),
        scratch_shapes=[pltpu.VMEM((G, NB, HD), F32)],
        compiler_params=_cparams(2, vmem),
        name="moba",
    )(mqt, mk_aug, mvt)


def _merge_kernel(oh_ref, om_ref, ga_ref, gb_ref, wh_ref, wm_ref, o_ref):
    a = jnp.dot(oh_ref[...], wh_ref[...], preferred_element_type=F32)
    b = jnp.dot(om_ref[...], wm_ref[...], preferred_element_type=F32)
    o_ref[...] = (ga_ref[...].astype(F32) * a + gb_ref[...].astype(F32) * b).astype(o_ref.dtype)


def _merge(o_hgrn, o_moba, gates, w_up_hgrn, w_up_moba, tm=512):
    T = o_hgrn.shape[0]
    D = D_MODEL
    vmem = 2 * (_nbytes((tm, HGRN_WIDTH), F32) + _nbytes((tm, MOBA_WIDTH), BF16) + 3 * _nbytes((tm, D), BF16)
                + 2 * _nbytes((HGRN_WIDTH, D), BF16)) + 3 * _nbytes((tm, D), F32)
    return pl.pallas_call(
        _merge_kernel,
        out_shape=jax.ShapeDtypeStruct((T, D), BF16),
        grid=(T // tm,),
        in_specs=[
            pl.BlockSpec((tm, HGRN_WIDTH), lambda i: (i, 0)),
            pl.BlockSpec((tm, MOBA_WIDTH), lambda i: (i, 0)),
            pl.BlockSpec((tm, D), lambda i: (i, 0)),
            pl.BlockSpec((tm, D), lambda i: (i, 1)),
            pl.BlockSpec((HGRN_WIDTH, D), lambda i: (0, 0)),
            pl.BlockSpec((MOBA_WIDTH, D), lambda i: (0, 0)),
        ],
        out_specs=pl.BlockSpec((tm, D), lambda i: (i, 0)),
        compiler_params=_cparams(1, vmem),
        name="merge",
    )(o_hgrn, o_moba, gates, gates, w_up_hgrn, w_up_moba)


def _outproj_kernel(m_ref, x_ref, w_ref, g_ref, x1_ref, h2_ref):
    x1 = x_ref[...] + jnp.dot(m_ref[...], w_ref[...], preferred_element_type=F32)
    x1_ref[...] = x1
    h2_ref[...] = _rms(x1, g_ref[...])


def _outproj(merged, x, w_out, g_ffn, tm=256):
    T, D = x.shape
    vmem = 2 * (_nbytes((tm, D), BF16) + 3 * _nbytes((tm, D), F32) + _nbytes((D, D), BF16)) + 2 * _nbytes((tm, D), F32)
    return pl.pallas_call(
        _outproj_kernel,
        out_shape=(jax.ShapeDtypeStruct((T, D), F32), jax.ShapeDtypeStruct((T, D), F32)),
        grid=(T // tm,),
        in_specs=[
            pl.BlockSpec((tm, D), lambda i: (i, 0)),
            pl.BlockSpec((tm, D), lambda i: (i, 0)),
            pl.BlockSpec((D, D), lambda i: (0, 0)),
            pl.BlockSpec((1, D), lambda i: (0, 0)),
        ],
        out_specs=(pl.BlockSpec((tm, D), lambda i: (i, 0)), pl.BlockSpec((tm, D), lambda i: (i, 0))),
        compiler_params=_cparams(1, vmem),
        name="outproj",
    )(merged, x, w_out, g_ffn.reshape(1, D))


def _router_kernel(h_ref, w_ref, info_ref, cnt_ref, carry_ref):
    tm = h_ref.shape[0]

    @pl.when(pl.program_id(0) == 0)
    def _():
        carry_ref[...] = jnp.zeros_like(carry_ref)

    logits = jnp.dot(h_ref[...].astype(BF16), w_ref[...], preferred_element_type=F32)
    lane = lax.broadcasted_iota(I32, (tm, V7X_LANES), 1)
    is_g = lane < N_GROUPS
    gl = jnp.where(is_g, logits, NEG_INF)
    gmax = jnp.max(gl, axis=1, keepdims=True)
    g_sel = jnp.min(jnp.where(gl == gmax, lane, V7X_LANES), axis=1, keepdims=True)
    gsum = jnp.sum(jnp.where(is_g, jnp.exp(gl - gmax), 0.0), axis=1, keepdims=True)
    p_group = 1.0 / gsum
    lo = N_GROUPS + EXPERTS_PER_GROUP * g_sel
    emask = (lane >= lo) & (lane < lo + EXPERTS_PER_GROUP)
    el = jnp.where(emask, logits, NEG_INF)
    e1 = jnp.max(el, axis=1, keepdims=True)
    i1 = jnp.min(jnp.where((el == e1) & emask, lane, V7X_LANES), axis=1, keepdims=True)
    emask2 = emask & (lane != i1)
    el2 = jnp.where(emask2, logits, NEG_INF)
    e2 = jnp.max(el2, axis=1, keepdims=True)
    i2 = jnp.min(jnp.where((el2 == e2) & emask2, lane, V7X_LANES), axis=1, keepdims=True)
    r = jnp.exp(e2 - e1)
    w1 = p_group / (1.0 + r)
    w2 = p_group * r / (1.0 + r)
    eid1 = i1 - N_GROUPS
    eid2 = i2 - N_GROUPS
    oh1 = jnp.where(lane == eid1, 1.0, 0.0)
    oh2 = jnp.where(lane == eid2, 1.0, 0.0)
    cnt = oh1 + oh2
    tri = jnp.where(lax.broadcasted_iota(I32, (tm, tm), 0) > lax.broadcasted_iota(I32, (tm, tm), 1), 1.0, 0.0)
    before = jnp.dot(tri.astype(BF16), cnt.astype(BF16), preferred_element_type=F32) + carry_ref[...]
    rank1 = jnp.sum(oh1 * before, axis=1, keepdims=True)
    rank2 = jnp.sum(oh2 * before, axis=1, keepdims=True)
    carry_ref[...] = carry_ref[...] + jnp.sum(cnt, axis=0, keepdims=True)
    info = jnp.zeros((tm, V7X_LANES), F32)
    for k, val in enumerate((eid1.astype(F32), eid2.astype(F32), w1, w2, rank1, rank2)):
        info = jnp.where(lane == k, val, info)
    info_ref[...] = info
    cnt_ref[...] = carry_ref[...]


def _router(h2, w_router, tm=256):
    T, D = h2.shape
    vmem = 2 * (_nbytes((tm, D), F32) + _nbytes((D, V7X_LANES), BF16)) + 16 * _nbytes((tm, V7X_LANES), F32) + (1 << 22)
    return pl.pallas_call(
        _router_kernel,
        out_shape=(jax.ShapeDtypeStruct((T, V7X_LANES), F32), jax.ShapeDtypeStruct((1, V7X_LANES), F32)),
        grid=(T // tm,),
        in_specs=[pl.BlockSpec((tm, D), lambda i: (i, 0)), pl.BlockSpec((D, V7X_LANES), lambda i: (0, 0))],
        out_specs=(pl.BlockSpec((tm, V7X_LANES), lambda i: (i, 0)), pl.BlockSpec((1, V7X_LANES), lambda i: (0, 0))),
        scratch_shapes=[pltpu.VMEM((1, V7X_LANES), F32)],
        compiler_params=_cparams(1, vmem),
        name="router",
    )(h2, w_router)


def _row_copy(src_ref, src_row, dst_ref, dst_row, sem):
    return pltpu.make_async_copy(src_ref.at[pl.ds(src_row, 1), :], dst_ref.at[pl.ds(dst_row, 1), :], sem)


ROW_DMA_UNROLL = 8


def _dispatch_kernel(dest_ref, pend_ref, h_ref, xs_ref, zero_ref, sem):
    tm = h_ref.shape[0]
    step = pl.program_id(0)
    base = step * tm

    @pl.when(step == 0)
    def _():
        zero_ref[...] = jnp.zeros_like(zero_ref)

        def tail(e):
            return pltpu.make_async_copy(
                zero_ref, xs_ref.at[pl.ds(pl.multiple_of(pend_ref[e] - MOE_ROWS, MOE_ROWS), MOE_ROWS), :], sem)

        def nonempty(e):
            return pend_ref[e] > (pend_ref[e - 1] if e else 0)

        def unused(b):
            return pltpu.make_async_copy(
                zero_ref, xs_ref.at[pl.ds(pl.multiple_of(b * MOE_ROWS, MOE_ROWS), MOE_ROWS), :], sem)

        first_unused = pend_ref[N_EXPERTS - 1] // MOE_ROWS
        n_blocks = xs_ref.shape[0] // MOE_ROWS
        for e in range(N_EXPERTS):
            pl.when(nonempty(e))(lambda e=e: tail(e).start())
        lax.fori_loop(first_unused, n_blocks, lambda b, c: (unused(b).start(), c)[1], 0)
        for e in range(N_EXPERTS):
            pl.when(nonempty(e))(lambda e=e: tail(e).wait())
        lax.fori_loop(first_unused, n_blocks, lambda b, c: (unused(b).wait(), c)[1], 0)

    def issue(rb, c):
        for u in range(ROW_DMA_UNROLL):
            r = rb * ROW_DMA_UNROLL + u
            for k in range(TOPK_IN_GROUP):
                _row_copy(h_ref, r, xs_ref, dest_ref[(base + r) * TOPK_IN_GROUP + k], sem).start()
        return c

    lax.fori_loop(0, tm // ROW_DMA_UNROLL, issue, 0)
    for _ in range(tm * TOPK_IN_GROUP):
        _row_copy(h_ref, 0, xs_ref, 0, sem).wait()


def _dispatch(dest_flat, pad_end, h2, m_pad, tm=256):
    T, D = h2.shape
    return pl.pallas_call(
        _dispatch_kernel,
        out_shape=jax.ShapeDtypeStruct((m_pad, D), F32),
        grid_spec=pltpu.PrefetchScalarGridSpec(
            num_scalar_prefetch=2,
            grid=(T // tm,),
            in_specs=[pl.BlockSpec((tm, D), lambda i, d, pe: (i, 0))],
            out_specs=pl.BlockSpec(memory_space=pl.ANY),
            scratch_shapes=[pltpu.VMEM((MOE_ROWS, D), F32), pltpu.SemaphoreType.DMA(())],
        ),
        compiler_params=_cparams(1, 6 * _nbytes((tm, D), F32)),
        name="dispatch",
    )(dest_flat, pad_end, h2)


def _expert_kernel(be_ref, nu_ref, xs_ref, wg_ref, wu_ref, wd_ref, y_ref, wgb_ref, wub_ref, wdb_ref):
    b = pl.program_id(0)
    used = b < nu_ref[0]

    @pl.when(used & ((b == 0) | (be_ref[b] != be_ref[jnp.maximum(b - 1, 0)])))
    def _():
        wgb_ref[...] = wg_ref[0].astype(BF16)
        wub_ref[...] = wu_ref[0].astype(BF16)
        wdb_ref[...] = wd_ref[0].astype(BF16)

    @pl.when(used)
    def _():
        x = xs_ref[...].astype(BF16)
        a = jnp.dot(x, wgb_ref[...], preferred_element_type=F32)
        u = jnp.dot(x, wub_ref[...], preferred_element_type=F32)
        hm = (a * jax.nn.sigmoid(a) * u).astype(BF16)
        y_ref[...] = jnp.dot(hm, wdb_ref[...], preferred_element_type=F32)

    @pl.when(jnp.logical_not(used))
    def _():
        y_ref[...] = jnp.zeros_like(y_ref)


def _experts(block_expert, n_used, xs, w_gate, w_up, w_down):
    m_pad, D = xs.shape
    R, Fd = MOE_ROWS, D_EXPERT
    vmem = 2 * (2 * _nbytes((R, D), F32) + 3 * _nbytes((D, Fd), F32)) + 3 * _nbytes((D, Fd), BF16)
    vmem += 4 * _nbytes((R, D), F32)
    return pl.pallas_call(
        _expert_kernel,
        out_shape=jax.ShapeDtypeStruct((m_pad, D), F32),
        grid_spec=pltpu.PrefetchScalarGridSpec(
            num_scalar_prefetch=2,
            grid=(m_pad // R,),
            in_specs=[
                pl.BlockSpec((R, D), lambda b, be, nu: (jnp.minimum(b, jnp.maximum(nu[0] - 1, 0)), 0)),
                pl.BlockSpec((1, D, Fd), lambda b, be, nu: (be[b], 0, 0)),
                pl.BlockSpec((1, D, Fd), lambda b, be, nu: (be[b], 0, 0)),
                pl.BlockSpec((1, Fd, D), lambda b, be, nu: (be[b], 0, 0)),
            ],
            out_specs=pl.BlockSpec((R, D), lambda b, be, nu: (b, 0)),
            scratch_shapes=[pltpu.VMEM((D, Fd), BF16), pltpu.VMEM((D, Fd), BF16), pltpu.VMEM((Fd, D), BF16)],
        ),
        compiler_params=_cparams(1, vmem),
        name="experts",
    )(block_expert, n_used, xs, w_gate, w_up, w_down)


def _combine_kernel(dest_ref, x1_ref, info_ref, yb_ref, p_ref, gp_ref, wpg_ref, wpp_ref, gf_ref, o_ref, ybuf, sem):
    tm = x1_ref.shape[0]
    base = pl.program_id(0) * tm

    def issue(rb, c):
        for u in range(ROW_DMA_UNROLL):
            r = rb * ROW_DMA_UNROLL + u
            for k in range(TOPK_IN_GROUP):
                _row_copy(yb_ref, dest_ref[(base + r) * TOPK_IN_GROUP + k], ybuf.at[k], r, sem).start()
        return c

    lax.fori_loop(0, tm // ROW_DMA_UNROLL, issue, 0)
    for _ in range(tm * TOPK_IN_GROUP):
        _row_copy(yb_ref, 0, ybuf.at[0], 0, sem).wait()
    info = info_ref[...]
    x2 = x1_ref[...] + info[:, 2:3] * ybuf[0] + info[:, 3:4] * ybuf[1]
    hp = _rms(x2, gp_ref[...]).astype(BF16)
    z = jnp.dot(hp, wpg_ref[...], preferred_element_type=F32)
    pp = jnp.dot(p_ref[...].astype(BF16), wpp_ref[...], preferred_element_type=F32)
    x3 = x2 + jax.nn.sigmoid(z) * pp
    o_ref[...] = _rms(x3, gf_ref[...])


def _combine(dest_flat, x1, info, yb, p, g_ple, w_ple_gate, w_ple_proj, g_final, tm=256):
    T, D = x1.shape
    vmem = 2 * (2 * _nbytes((tm, D), F32) + _nbytes((D, D), BF16) + _nbytes((PLE_DIM, D), BF16)
                + _nbytes((tm, PLE_DIM), F32)) + 8 * _nbytes((tm, D), F32)
    return pl.pallas_call(
        _combine_kernel,
        out_shape=jax.ShapeDtypeStruct((T, D), F32),
        grid_spec=pltpu.PrefetchScalarGridSpec(
            num_scalar_prefetch=1,
            grid=(T // tm,),
            in_specs=[
                pl.BlockSpec((tm, D), lambda i, d: (i, 0)),
                pl.BlockSpec((tm, V7X_LANES), lambda i, d: (i, 0)),
                pl.BlockSpec(memory_space=pl.ANY),
                pl.BlockSpec((tm, PLE_DIM), lambda i, d: (i, 0)),
                pl.BlockSpec((1, D), lambda i, d: (0, 0)),
                pl.BlockSpec((D, D), lambda i, d: (0, 0)),
                pl.BlockSpec((PLE_DIM, D), lambda i, d: (0, 0)),
                pl.BlockSpec((1, D), lambda i, d: (0, 0)),
            ],
            out_specs=pl.BlockSpec((tm, D), lambda i, d: (i, 0)),
            scratch_shapes=[pltpu.VMEM((TOPK_IN_GROUP, tm, D), F32), pltpu.SemaphoreType.DMA(())],
        ),
        compiler_params=_cparams(1, vmem),
        name="combine",
    )(dest_flat, x1, info, yb, p, g_ple.reshape(1, D), w_ple_gate, w_ple_proj, g_final.reshape(1, D))


def _rope_tables(T):
    half = MOBA_HEAD_DIM // 2
    inv_freq = ROPE_THETA ** (-jnp.arange(half, dtype=F32) / half)
    ang = jnp.arange(T, dtype=F32)[:, None] * inv_freq[None, :]
    cos, sin = jnp.cos(ang), jnp.sin(ang)
    return jnp.concatenate([cos, cos], axis=1), jnp.concatenate([-sin, sin], axis=1)


def _mixers(x2d, g_mix, w_in, lb, hgrn_norm_g):
    T = x2d.shape[0]
    W = HGRN_WIDTH
    h = _rmsnorm(x2d, g_mix, BF16)
    w = w_in
    cos, sin = _rope_tables(T)
    log_lb = jnp.log(lb).reshape(1, W)
    log_1m = jnp.log1p(-lb).reshape(1, W)
    hq = _proj(h, w, 0 * W, W, _ep_silu, BF16)
    logf = _proj(h, w, 1 * W, W, _ep_logf, F32, col_extras=(log_lb, log_1m))
    hi = _proj(h, w, 2 * W, W, _ep_identity, BF16)
    hog = _proj(h, w, 3 * W, W, _ep_silu, BF16)
    scale = MOBA_HEAD_DIM ** -0.5
    mqt = _proj_t(h, w, 4 * W, functools.partial(_ep_rope, scale=scale), row_extras=(cos, sin))
    blk = jnp.arange(T, dtype=I32)[:, None] // MOBA_BLOCK
    blk_onehot = (blk == jnp.arange(V7X_LANES, dtype=I32)[None, :]).astype(F32)
    mk = _proj(h, w, 5 * W, W, _ep_rope_aug, BF16, row_extras=(cos, sin, blk_onehot), widen=2)
    mvt = _proj_t(h, w, 6 * W, _ep_identity, ones_rows=MOBA_VT_ROWS - MOBA_HEAD_DIM)
    gates = _proj(h, w, 7 * W, 2 * D_MODEL, _ep_sigmoid, BF16)
    o_hgrn = _hgrn(hq, logf, hi, hog, hgrn_norm_g)
    o_moba = _moba(mqt, mk, mvt)
    return o_hgrn, o_moba, gates


def _moe_plan(info, cnt, T):
    R = MOE_ROWS
    eid = info[:, 0:TOPK_IN_GROUP].astype(I32)
    rank = info[:, 4:4 + TOPK_IN_GROUP].astype(I32)
    counts = cnt[0, :N_EXPERTS].astype(I32)
    padded = (counts + R - 1) // R * R
    pad_end = jnp.cumsum(padded)
    pad_start = pad_end - padded
    dest = (pad_start[eid] + rank).reshape(-1)
    n_blocks = (T * TOPK_IN_GROUP) // R + N_EXPERTS
    block_expert = jnp.minimum(
        jnp.searchsorted(pad_end, jnp.arange(n_blocks, dtype=I32) * R, side="right"), N_EXPERTS - 1).astype(I32)
    n_used = (pad_end[-1:] // R).astype(I32)
    return dest, pad_end.astype(I32), block_expert, n_used, n_blocks * R


def kernel(x, p, norm_mix_g, w_in, hgrn_lb_raw, hgrn_norm_g, w_up_hgrn, w_up_moba, w_out, norm_ffn_g,
           w_router_group, w_router_expert, w_exp_gate, w_exp_up, w_exp_down, norm_ple_g, w_ple_gate,
           w_ple_proj, norm_final_g):
    B, T, D = x.shape
    assert B == 1 and D == D_MODEL and w_in.shape[0] == 1 and T % (4 * MOBA_BLOCK) == 0
    lower_bounds = jnp.cumsum(jax.nn.softmax(hgrn_lb_raw.astype(F32), axis=0), axis=0)
    x2d = x.reshape(T, D)
    o_hgrn, o_moba, gates = _mixers(x2d, norm_mix_g[0], w_in[0], lower_bounds[0], hgrn_norm_g[0])
    merged = _merge(o_hgrn, o_moba, gates, w_up_hgrn[0].astype(BF16), w_up_moba[0].astype(BF16))
    x1, h2 = _outproj(merged, x2d, w_out[0].astype(BF16), norm_ffn_g[0])
    w_router = jnp.pad(jnp.concatenate([w_router_group[0], w_router_expert[0]], axis=1),
                       ((0, 0), (0, V7X_LANES - N_GROUPS - N_EXPERTS))).astype(BF16)
    info, cnt = _router(h2, w_router)
    dest, pad_end, block_expert, n_used, m_pad = _moe_plan(info, cnt, T)
    xs = _dispatch(dest, pad_end, h2, m_pad)
    yb = _experts(block_expert, n_used, xs, w_exp_gate[0], w_exp_up[0], w_exp_down[0])
    out = _combine(dest, x1, info, yb, p[0].reshape(T, PLE_DIM), norm_ple_g[0], w_ple_gate[0].astype(BF16),
                   w_ple_proj[0].astype(BF16), norm_final_g)
    return out.reshape(B, T, D)
```

```python
import functools

import jax
import jax.numpy as jnp
from jax import lax
from jax.experimental import pallas as pl
from jax.experimental.pallas import tpu as pltpu

F32 = jnp.float32
BF16 = jnp.bfloat16
I32 = jnp.int32

D_MODEL = 2048
PLE_DIM = 256
HGRN_HEADS = 8
HGRN_HEAD_DIM = 128
HGRN_WIDTH = HGRN_HEADS * HGRN_HEAD_DIM
MOBA_HEADS = 8
MOBA_HEAD_DIM = 128
MOBA_WIDTH = MOBA_HEADS * MOBA_HEAD_DIM
MOBA_BLOCK = 256
MOBA_TOPK = 3
ROPE_THETA = 10000.0
N_GROUPS = 4
EXPERTS_PER_GROUP = 8
N_EXPERTS = N_GROUPS * EXPERTS_PER_GROUP
TOPK_IN_GROUP = 2
D_EXPERT = 512
EPS = 1e-6
NEG_INF = -1e30

V7X_LANES = 128
V7X_SUBLANES = 8
V7X_VMEM_BUDGET_BYTES = 56 * 1024 * 1024

HGRN_CHUNK = 128
MOE_ROWS = 256


def _cparams(n_grid, vmem_bytes):
    return pltpu.CompilerParams(
        dimension_semantics=("arbitrary",) * n_grid,
        vmem_limit_bytes=int(min(max(vmem_bytes, 16 * 1024 * 1024), V7X_VMEM_BUDGET_BYTES)),
    )


def _nbytes(shape, dtype):
    n = 1
    for s in shape:
        n *= s
    return n * jnp.dtype(dtype).itemsize


def _rms(x, g):
    ms = jnp.mean(x * x, axis=-1, keepdims=True)
    return x * lax.rsqrt(ms + EPS) * g


def _rmsnorm_kernel(x_ref, g_ref, o_ref):
    o_ref[...] = _rms(x_ref[...], g_ref[...]).astype(o_ref.dtype)


def _rmsnorm(x, g, out_dtype, tm=512):
    T, D = x.shape
    return pl.pallas_call(
        _rmsnorm_kernel,
        out_shape=jax.ShapeDtypeStruct((T, D), out_dtype),
        grid=(T // tm,),
        in_specs=[pl.BlockSpec((tm, D), lambda i: (i, 0)), pl.BlockSpec((1, D), lambda i: (0, 0))],
        out_specs=pl.BlockSpec((tm, D), lambda i: (i, 0)),
        compiler_params=_cparams(1, 4 * _nbytes((tm, D), F32)),
        name="rmsnorm",
    )(x, g.reshape(1, D))


def _ep_identity(acc):
    return acc


def _ep_silu(acc):
    return acc * jax.nn.sigmoid(acc)


def _ep_sigmoid(acc):
    return jax.nn.sigmoid(acc)


def _ep_logf(acc, la_ref, lc_ref):
    ls = jnp.minimum(acc, 0.0) - jnp.log(1.0 + jnp.exp(-jnp.abs(acc)))
    u = la_ref[...]
    v = lc_ref[...] + ls
    return jnp.maximum(u, v) + jnp.log(1.0 + jnp.exp(-jnp.abs(u - v)))


def _ep_rope(acc, cos_ref, sin_ref, *, scale):
    cos = cos_ref[...]
    sin = sin_ref[...]
    outs = []
    for hh in range(acc.shape[1] // MOBA_HEAD_DIM):
        a = acc[:, hh * MOBA_HEAD_DIM:(hh + 1) * MOBA_HEAD_DIM]
        r = pltpu.roll(a, MOBA_HEAD_DIM // 2, axis=1)
        outs.append((a * cos + r * sin) * scale)
    return jnp.concatenate(outs, axis=1)


def _ep_rope_aug(acc, cos_ref, sin_ref, oh_ref):
    cos = cos_ref[...]
    sin = sin_ref[...]
    oh = oh_ref[...]
    outs = []
    for hh in range(acc.shape[1] // MOBA_HEAD_DIM):
        a = acc[:, hh * MOBA_HEAD_DIM:(hh + 1) * MOBA_HEAD_DIM]
        outs.append(a * cos + pltpu.roll(a, MOBA_HEAD_DIM // 2, axis=1) * sin)
        outs.append(oh)
    return jnp.concatenate(outs, axis=1)


def _cast_weight_once(w_ref, wb_ref, row_axis):
    @pl.when(pl.program_id(row_axis) == 0)
    def _():
        wb_ref[...] = w_ref[...].astype(wb_ref.dtype)


def _proj_kernel(h_ref, w_ref, *refs, epilogue):
    *extra, o_ref, wb_ref = refs
    _cast_weight_once(w_ref, wb_ref, 1)
    acc = jnp.dot(h_ref[...], wb_ref[...], preferred_element_type=F32)
    o_ref[...] = epilogue(acc, *extra).astype(o_ref.dtype)


def _proj(h, w, col0, ncols, epilogue, out_dtype, row_extras=(), col_extras=(), tm=1024, tn=1024, widen=1):
    T, K = h.shape
    tn = min(tn, ncols)
    tm = min(tm, T)
    cb = col0 // tn
    otn = widen * tn
    in_specs = [
        pl.BlockSpec((tm, K), lambda j, i: (i, 0)),
        pl.BlockSpec((K, tn), lambda j, i: (0, cb + j)),
    ]
    for e in row_extras:
        in_specs.append(pl.BlockSpec((tm, e.shape[1]), lambda j, i: (i, 0)))
    for e in col_extras:
        in_specs.append(pl.BlockSpec((1, tn), lambda j, i: (0, j)))
    vmem = 2 * (_nbytes((tm, K), h.dtype) + _nbytes((K, tn), w.dtype) + _nbytes((tm, otn), out_dtype))
    vmem += 3 * _nbytes((tm, otn), F32) + _nbytes((K, tn), BF16)
    return pl.pallas_call(
        functools.partial(_proj_kernel, epilogue=epilogue),
        out_shape=jax.ShapeDtypeStruct((T, widen * ncols), out_dtype),
        grid=(ncols // tn, T // tm),
        in_specs=in_specs,
        out_specs=pl.BlockSpec((tm, otn), lambda j, i: (i, j)),
        scratch_shapes=[pltpu.VMEM((K, tn), BF16)],
        compiler_params=_cparams(2, vmem),
        name="proj",
    )(h, w, *row_extras, *col_extras)


MOBA_VT_ROWS = MOBA_HEAD_DIM + 16
MOBA_BLOCKS_PER_STEP = 4


def _proj_t_kernel(h_ref, w_ref, *refs, epilogue, ones_rows):
    *extra, o_ref, wb_ref = refs
    BS, HD = MOBA_BLOCK, MOBA_HEAD_DIM
    _cast_weight_once(w_ref, wb_ref, 0)
    acc = epilogue(jnp.dot(h_ref[...], wb_ref[...], preferred_element_type=F32), *extra)
    ones = jnp.ones((ones_rows, BS), F32) if ones_rows else None
    for b in range(acc.shape[0] // BS):
        parts = []
        for hh in range(acc.shape[1] // HD):
            parts.append(acc[b * BS:(b + 1) * BS, hh * HD:(hh + 1) * HD].T)
            if ones_rows:
                parts.append(ones)
        o_ref[b] = jnp.concatenate(parts, axis=0).astype(o_ref.dtype)


def _proj_t(h, w, col0, epilogue, row_extras=(), ones_rows=0, tm=1024):
    T, K = h.shape
    tn = MOBA_WIDTH
    tm = min(tm, T)
    cb = col0 // tn
    rows = MOBA_HEADS * (MOBA_HEAD_DIM + ones_rows)
    in_specs = [pl.BlockSpec((tm, K), lambda i: (i, 0)), pl.BlockSpec((K, tn), lambda i: (0, cb))]
    for e in row_extras:
        in_specs.append(pl.BlockSpec((tm, e.shape[1]), lambda i: (i, 0)))
    vmem = 2 * (_nbytes((tm, K), h.dtype) + _nbytes((K, tn), w.dtype) + _nbytes((tm, 2 * tn), BF16))
    vmem += 4 * _nbytes((tm, tn), F32) + _nbytes((K, tn), BF16)
    return pl.pallas_call(
        functools.partial(_proj_t_kernel, epilogue=epilogue, ones_rows=ones_rows),
        out_shape=jax.ShapeDtypeStruct((T // MOBA_BLOCK, rows, MOBA_BLOCK), BF16),
        grid=(T // tm,),
        in_specs=in_specs,
        out_specs=pl.BlockSpec((tm // MOBA_BLOCK, rows, MOBA_BLOCK), lambda i: (i, 0, 0)),
        scratch_shapes=[pltpu.VMEM((K, tn), BF16)],
        compiler_params=_cparams(1, vmem),
        name="proj_t",
    )(h, w, *row_extras)


def _hgrn_kernel(q_ref, g_ref, v_ref, og_ref, ng_ref, o_ref, st_ref):
    W, HD, NH, C, S = HGRN_WIDTH, HGRN_HEAD_DIM, HGRN_HEADS, HGRN_CHUNK, V7X_SUBLANES
    J = C // S

    @pl.when(pl.program_id(0) == 0)
    def _():
        st_ref[...] = jnp.zeros_like(st_ref)

    def r3(x):
        return x.astype(F32).reshape(J, S, W)

    def sub_bcast(x3, r):
        return jnp.broadcast_to(x3[:, r:r + 1, :], x3.shape)

    g3, q3, v3 = r3(g_ref[...]), r3(q_ref[...]), r3(v_ref[...])
    sub = lax.broadcasted_iota(I32, (1, S, W), 1)

    c3 = g3
    for s in (1, 2, 4):
        c3 = c3 + jnp.where(sub >= s, pltpu.roll(c3, s, axis=1), 0.0)
    run = jnp.zeros((1, 1, W), F32)
    carry = []
    for j in range(J):
        carry.append(run)
        run = run + c3[j:j + 1, S - 1:S, :]
    b3 = c3 + jnp.concatenate(carry, axis=0)
    bC = run

    k3 = 1.0 - jnp.exp(g3)
    qe3 = q3 * jnp.exp(b3)
    ks3 = k3 * jnp.exp(bC - b3)

    levels = [(0, q3, k3)]
    ref1 = jnp.where(sub % 2 == 0, b3, pltpu.roll(b3, 1, axis=1))
    ref2 = jnp.where(sub < 4, sub_bcast(b3, 1), sub_bcast(b3, 5))
    ref4 = sub_bcast(b3, 3)
    for lvl, (ref, upper) in enumerate(((ref1, sub % 2 == 1), (ref2, sub % 4 >= 2), (ref4, sub >= 4)), start=1):
        e = jnp.exp(-jnp.abs(b3 - ref))
        levels.append((lvl, jnp.where(upper, q3 * e, 0.0), jnp.where(upper, 0.0, k3 * e)))
    zero_group = jnp.zeros((1, S, W), F32)
    for lvl, half in enumerate((1, 2, 4, 8), start=4):
        qparts, kparts = [], []
        for j in range(J):
            jr = (j // (2 * half)) * (2 * half) + half - 1
            ref = b3[jr:jr + 1, S - 1:S, :]
            if (j % (2 * half)) >= half:
                qparts.append(q3[j:j + 1] * jnp.exp(b3[j:j + 1] - ref))
                kparts.append(zero_group)
            else:
                qparts.append(zero_group)
                kparts.append(k3[j:j + 1] * jnp.exp(ref - b3[j:j + 1]))
        levels.append((lvl, jnp.concatenate(qparts, axis=0), jnp.concatenate(kparts, axis=0)))

    tr = lax.broadcasted_iota(I32, (C, C), 0)
    tc = lax.broadcasted_iota(I32, (C, C), 1)
    xr = tr ^ tc
    code = jnp.zeros((C, C), I32)
    for lvl in range(1, 8):
        code = jnp.where(xr >= (1 << (lvl - 1)), lvl, code)
    code = jnp.where(tc > tr, -1, code)

    def mat(x3, h):
        return x3.reshape(C, W)[:, h * HD:(h + 1) * HD].astype(BF16)

    nt = (((1,), (1,)), ((), ()))
    tn = (((0,), (0,)), ((), ()))
    ebc = jnp.exp(bC).reshape(1, W)
    ng = ng_ref[...]
    for h in range(NH):
        a_mat = jnp.zeros((C, C), F32)
        for lvl, qr, kr in levels:
            s = lax.dot_general(mat(qr, h), mat(kr, h), nt, preferred_element_type=F32)
            a_mat = jnp.where(code == lvl, s, a_mat)
        vh = mat(v3, h)
        st = st_ref[h]
        o = jnp.dot(a_mat.astype(BF16), vh, preferred_element_type=F32)
        o = o + lax.dot_general(mat(qe3, h), st.astype(BF16), nt, preferred_element_type=F32)
        o = _rms(o, ng) * og_ref[:, h * HD:(h + 1) * HD].astype(F32)
        o_ref[:, h * HD:(h + 1) * HD] = o.astype(o_ref.dtype)
        st_ref[h] = st * ebc[:, h * HD:(h + 1) * HD] + lax.dot_general(
            vh, mat(ks3, h), tn, preferred_element_type=F32)


def _hgrn(q, logf, v, og, norm_g):
    T, W = q.shape
    C = HGRN_CHUNK
    blk = pl.BlockSpec((C, W), lambda c: (c, 0))
    vmem = 64 * _nbytes((C, W), F32)
    return pl.pallas_call(
        _hgrn_kernel,
        out_shape=jax.ShapeDtypeStruct((T, W), BF16),
        grid=(T // C,),
        in_specs=[blk, blk, blk, blk, pl.BlockSpec((1, HGRN_HEAD_DIM), lambda c: (0, 0))],
        out_specs=blk,
        scratch_shapes=[pltpu.VMEM((HGRN_HEADS, HGRN_HEAD_DIM, HGRN_HEAD_DIM), F32)],
        compiler_params=_cparams(1, vmem),
        name="hgrn",
    )(q, logf, v, og, norm_g.reshape(1, HGRN_HEAD_DIM))


def _moba_kernel(qt_ref, k_ref, vt_ref, o_ref, km_ref):
    BS, HD, VR = MOBA_BLOCK, MOBA_HEAD_DIM, MOBA_VT_ROWS
    T = k_ref.shape[0]
    NB = T // BS
    G = o_ref.shape[1] // HD
    cur = pl.program_id(1)

    @pl.when(cur == 0)
    def _():
        for g in range(G):
            kf = k_ref[:, 2 * g * HD:(2 * g + 1) * HD].astype(F32).reshape(NB, BS, HD)
            km_ref[g] = jnp.sum(kf, axis=1) * (1.0 / BS)

    blk = lax.broadcasted_iota(I32, (NB, BS), 0)
    key = lax.broadcasted_iota(I32, (BS, BS), 0)
    qry = lax.broadcasted_iota(I32, (BS, BS), 1)
    pad = jnp.zeros((V7X_LANES - NB, BS), F32)
    r0 = pl.multiple_of(cur * BS, BS)
    qts = [qt_ref[0, g * HD:(g + 1) * HD, :] for g in range(G)]
    gts = [jnp.dot(km_ref[g].astype(BF16), qts[g], preferred_element_type=F32) for g in range(G)]
    s_own = [jnp.dot(k_ref[pl.ds(r0, BS), 2 * g * HD:(2 * g + 1) * HD], qts[g], preferred_element_type=F32)
             for g in range(G)]
    qcs = []
    for g in range(G):
        gt = jnp.where(blk < cur, gts[g], NEG_INF)
        sel = jnp.zeros((NB, BS), jnp.bool_)
        for _ in range(MOBA_TOPK):
            mx = jnp.max(gt, axis=0, keepdims=True)
            idx = jnp.min(jnp.where(gt == mx, blk, NB), axis=0, keepdims=True)
            pick = (blk == idx) & (mx > 0.5 * NEG_INF)
            sel = sel | pick
            gt = jnp.where(pick, NEG_INF, gt)
        pen = jnp.concatenate([jnp.where(sel, 0.0, NEG_INF), pad], axis=0).astype(BF16)
        qcs.append(jnp.concatenate([qts[g], pen], axis=0))

    def scores(n):
        rn = pl.multiple_of(n * BS, BS)
        return tuple(jnp.dot(k_ref[pl.ds(rn, BS), 2 * g * HD:(2 * g + 2) * HD], qcs[g], preferred_element_type=F32)
                     for g in range(G))

    ms, ps = [], []
    for g in range(G):
        s = jnp.where(key <= qry, s_own[g], NEG_INF)
        m0 = jnp.max(s, axis=0, keepdims=True)
        ms.append(m0)
        ps.append(jnp.exp(s - m0).astype(BF16))
    accs = [jnp.dot(vt_ref[cur, g * VR:(g + 1) * VR, :], ps[g], preferred_element_type=F32)
            for g in range(G)]

    KB = MOBA_BLOCKS_PER_STEP

    def body(c, carry):
        ms, accs = carry
        r = pl.multiple_of(c * KB * BS, KB * BS)
        sns = [jnp.dot(k_ref[pl.ds(r, KB * BS), 2 * g * HD:(2 * g + 2) * HD], qcs[g], preferred_element_type=F32)
               for g in range(G)]
        new_ms, alphas, pns = [], [], []
        for g in range(G):
            m_new = jnp.maximum(ms[g], jnp.max(sns[g], axis=0, keepdims=True))
            alphas.append(jnp.exp(ms[g] - m_new))
            pns.append(jnp.exp(sns[g] - m_new).astype(BF16))
            new_ms.append(m_new)
        new_accs = []
        for g in range(G):
            pv = alphas[g] * accs[g]
            for j in range(KB):
                pv = pv + jnp.dot(vt_ref[KB * c + j, g * VR:(g + 1) * VR, :], pns[g][j * BS:(j + 1) * BS],
                                  preferred_element_type=F32)
            new_accs.append(pv)
        return tuple(new_ms), tuple(new_accs)

    _, accs = lax.fori_loop(0, (cur + KB - 1) // KB, body, (tuple(ms), tuple(accs)))
    for g in range(G):
        ot = accs[g][:HD, :] / accs[g][HD:HD + 1, :]
        o_ref[:, g * HD:(g + 1) * HD] = ot.T.astype(o_ref.dtype)


def _moba(mqt, mk_aug, mvt, heads_per_step=4):
    T = mk_aug.shape[0]
    BS, HD, G, VR = MOBA_BLOCK, MOBA_HEAD_DIM, heads_per_step, MOBA_VT_ROWS
    NB = T // BS
    vmem = _nbytes((T, 2 * G * HD), BF16) + _nbytes((NB, G * VR, BS), BF16) + 8 * _nbytes((BS, G * HD), BF16)
    vmem += 16 * G * _nbytes((BS, BS), F32)
    resident = pl.Buffered(1)
    return pl.pallas_call(
        _moba_kernel,
        out_shape=jax.ShapeDtypeStruct((T, MOBA_WIDTH), BF16),
        grid=(MOBA_HEADS // G, NB),
        in_specs=[
            pl.BlockSpec((1, G * HD, BS), lambda h, i: (i, h, 0)),
            pl.BlockSpec((T, 2 * G * HD), lambda h, i: (0, h), pipeline_mode=resident),
            pl.BlockSpec((NB, G * VR, BS), lambda h, i: (0, h, 0), pipeline_mode=resident),
        ],
        out_specs=pl.BlockSpec((BS, G * HD), lambda h, i: (i, h)),
        scratch_shapes=[pltpu.VMEM((G, NB, HD), F32)],
        compiler_params=_cparams(2, vmem),
        name="moba",
    )(mqt, mk_aug, mvt)


def _merge_kernel(oh_ref, om_ref, ga_ref, gb_ref, wh_ref, wm_ref, o_ref):
    a = jnp.dot(oh_ref[...], wh_ref[...], preferred_element_type=F32)
    b = jnp.dot(om_ref[...], wm_ref[...], preferred_element_type=F32)
    o_ref[...] = (ga_ref[...].astype(F32) * a + gb_ref[...].astype(F32) * b).astype(o_ref.dtype)


def _merge(o_hgrn, o_moba, gates, w_up_hgrn, w_up_moba, tm=512):
    T = o_hgrn.shape[0]
    D = D_MODEL
    vmem = 2 * (_nbytes((tm, HGRN_WIDTH), F32) + _nbytes((tm, MOBA_WIDTH), BF16) + 3 * _nbytes((tm, D), BF16)
                + 2 * _nbytes((HGRN_WIDTH, D), BF16)) + 3 * _nbytes((tm, D), F32)
    return pl.pallas_call(
        _merge_kernel,
        out_shape=jax.ShapeDtypeStruct((T, D), BF16),
        grid=(T // tm,),
        in_specs=[
            pl.BlockSpec((tm, HGRN_WIDTH), lambda i: (i, 0)),
            pl.BlockSpec((tm, MOBA_WIDTH), lambda i: (i, 0)),
            pl.BlockSpec((tm, D), lambda i: (i, 0)),
            pl.BlockSpec((tm, D), lambda i: (i, 1)),
            pl.BlockSpec((HGRN_WIDTH, D), lambda i: (0, 0)),
            pl.BlockSpec((MOBA_WIDTH, D), lambda i: (0, 0)),
        ],
        out_specs=pl.BlockSpec((tm, D), lambda i: (i, 0)),
        compiler_params=_cparams(1, vmem),
        name="merge",
    )(o_hgrn, o_moba, gates, gates, w_up_hgrn, w_up_moba)


def _outproj_kernel(m_ref, x_ref, w_ref, g_ref, x1_ref, h2_ref):
    x1 = x_ref[...] + jnp.dot(m_ref[...], w_ref[...], preferred_element_type=F32)
    x1_ref[...] = x1
    h2_ref[...] = _rms(x1, g_ref[...])


def _outproj(merged, x, w_out, g_ffn, tm=256):
    T, D = x.shape
    vmem = 2 * (_nbytes((tm, D), BF16) + 3 * _nbytes((tm, D), F32) + _nbytes((D, D), BF16)) + 2 * _nbytes((tm, D), F32)
    return pl.pallas_call(
        _outproj_kernel,
        out_shape=(jax.ShapeDtypeStruct((T, D), F32), jax.ShapeDtypeStruct((T, D), F32)),
        grid=(T // tm,),
        in_specs=[
            pl.BlockSpec((tm, D), lambda i: (i, 0)),
            pl.BlockSpec((tm, D), lambda i: (i, 0)),
            pl.BlockSpec((D, D), lambda i: (0, 0)),
            pl.BlockSpec((1, D), lambda i: (0, 0)),
        ],
        out_specs=(pl.BlockSpec((tm, D), lambda i: (i, 0)), pl.BlockSpec((tm, D), lambda i: (i, 0))),
        compiler_params=_cparams(1, vmem),
        name="outproj",
    )(merged, x, w_out, g_ffn.reshape(1, D))


def _router_kernel(h_ref, w_ref, info_ref, cnt_ref, carry_ref):
    tm = h_ref.shape[0]

    @pl.when(pl.program_id(0) == 0)
    def _():
        carry_ref[...] = jnp.zeros_like(carry_ref)

    logits = jnp.dot(h_ref[...].astype(BF16), w_ref[...], preferred_element_type=F32)
    lane = lax.broadcasted_iota(I32, (tm, V7X_LANES), 1)
    is_g = lane < N_GROUPS
    gl = jnp.where(is_g, logits, NEG_INF)
    gmax = jnp.max(gl, axis=1, keepdims=True)
    g_sel = jnp.min(jnp.where(gl == gmax, lane, V7X_LANES), axis=1, keepdims=True)
    gsum = jnp.sum(jnp.where(is_g, jnp.exp(gl - gmax), 0.0), axis=1, keepdims=True)
    p_group = 1.0 / gsum
    lo = N_GROUPS + EXPERTS_PER_GROUP * g_sel
    emask = (lane >= lo) & (lane < lo + EXPERTS_PER_GROUP)
    el = jnp.where(emask, logits, NEG_INF)
    e1 = jnp.max(el, axis=1, keepdims=True)
    i1 = jnp.min(jnp.where((el == e1) & emask, lane, V7X_LANES), axis=1, keepdims=True)
    emask2 = emask & (lane != i1)
    el2 = jnp.where(emask2, logits, NEG_INF)
    e2 = jnp.max(el2, axis=1, keepdims=True)
    i2 = jnp.min(jnp.where((el2 == e2) & emask2, lane, V7X_LANES), axis=1, keepdims=True)
    r = jnp.exp(e2 - e1)
    w1 = p_group / (1.0 + r)
    w2 = p_group * r / (1.0 + r)
    eid1 = i1 - N_GROUPS
    eid2 = i2 - N_GROUPS
    oh1 = jnp.where(lane == eid1, 1.0, 0.0)
    oh2 = jnp.where(lane == eid2, 1.0, 0.0)
    cnt = oh1 + oh2
    tri = jnp.where(lax.broadcasted_iota(I32, (tm, tm), 0) > lax.broadcasted_iota(I32, (tm, tm), 1), 1.0, 0.0)
    before = jnp.dot(tri.astype(BF16), cnt.astype(BF16), preferred_element_type=F32) + carry_ref[...]
    rank1 = jnp.sum(oh1 * before, axis=1, keepdims=True)
    rank2 = jnp.sum(oh2 * before, axis=1, keepdims=True)
    carry_ref[...] = carry_ref[...] + jnp.sum(cnt, axis=0, keepdims=True)
    info = jnp.zeros((tm, V7X_LANES), F32)
    for k, val in enumerate((eid1.astype(F32), eid2.astype(F32), w1, w2, rank1, rank2)):
        info = jnp.where(lane == k, val, info)
    info_ref[...] = info
    cnt_ref[...] = carry_ref[...]


def _router(h2, w_router, tm=256):
    T, D = h2.shape
    vmem = 2 * (_nbytes((tm, D), F32) + _nbytes((D, V7X_LANES), BF16)) + 16 * _nbytes((tm, V7X_LANES), F32) + (1 << 22)
    return pl.pallas_call(
        _router_kernel,
        out_shape=(jax.ShapeDtypeStruct((T, V7X_LANES), F32), jax.ShapeDtypeStruct((1, V7X_LANES), F32)),
        grid=(T // tm,),
        in_specs=[pl.BlockSpec((tm, D), lambda i: (i, 0)), pl.BlockSpec((D, V7X_LANES), lambda i: (0, 0))],
        out_specs=(pl.BlockSpec((tm, V7X_LANES), lambda i: (i, 0)), pl.BlockSpec((1, V7X_LANES), lambda i: (0, 0))),
        scratch_shapes=[pltpu.VMEM((1, V7X_LANES), F32)],
        compiler_params=_cparams(1, vmem),
        name="router",
    )(h2, w_router)


def _row_copy(src_ref, src_row, dst_ref, dst_row, sem):
    return pltpu.make_async_copy(src_ref.at[pl.ds(src_row, 1), :], dst_ref.at[pl.ds(dst_row, 1), :], sem)


def _expert_kernel(be_ref, nu_ref, tok_ref, h_ref, wg_ref, wu_ref, wd_ref, y_ref, xbuf, wgb_ref, wub_ref, wdb_ref,
                   sem):
    R = MOE_ROWS
    step = pl.program_id(0)
    b = step - 1
    n_used = nu_ref[0]
    used = (b >= 0) & (b < n_used)

    def fetch(blk):
        slot = blk % 2
        for r in range(R):
            _row_copy(h_ref, tok_ref[blk * R + r], xbuf.at[slot], r, sem.at[slot]).start()

    def drain(blk):
        slot = blk % 2
        for _ in range(R):
            _row_copy(h_ref, 0, xbuf.at[0], 0, sem.at[slot]).wait()

    @pl.when(step == 0)
    def _():
        fetch(step)

    @pl.when((b >= 0) & (b <= n_used))
    def _():
        drain(b)

    @pl.when(used & ((b == 0) | (be_ref[jnp.maximum(b, 0)] != be_ref[jnp.maximum(b - 1, 0)])))
    def _():
        wgb_ref[...] = wg_ref[0].astype(BF16)
        wub_ref[...] = wu_ref[0].astype(BF16)
        wdb_ref[...] = wd_ref[0].astype(BF16)

    @pl.when(used)
    def _():
        fetch(step)
        x = xbuf[b % 2].astype(BF16)
        a = jnp.dot(x, wgb_ref[...], preferred_element_type=F32)
        u = jnp.dot(x, wub_ref[...], preferred_element_type=F32)
        hm = (a * jax.nn.sigmoid(a) * u).astype(BF16)
        y_ref[...] = jnp.dot(hm, wdb_ref[...], preferred_element_type=F32)

    @pl.when((b >= n_used))
    def _():
        y_ref[...] = jnp.zeros_like(y_ref)


def _experts(block_expert, n_used, slot_token, h2, w_gate, w_up, w_down):
    T, D = h2.shape
    R, Fd = MOE_ROWS, D_EXPERT
    m_pad = slot_token.shape[0] - R
    n_blocks = m_pad // R
    vmem = 2 * (_nbytes((R, D), F32) + 3 * _nbytes((D, Fd), F32)) + 3 * _nbytes((D, Fd), BF16)
    vmem += 6 * _nbytes((R, D), F32)
    blk = lambda i, be, nu, tok: jnp.clip(i - 1, 0, n_blocks - 1)
    return pl.pallas_call(
        _expert_kernel,
        out_shape=jax.ShapeDtypeStruct((m_pad, D), F32),
        grid_spec=pltpu.PrefetchScalarGridSpec(
            num_scalar_prefetch=3,
            grid=(n_blocks + 1,),
            in_specs=[
                pl.BlockSpec(memory_space=pl.ANY),
                pl.BlockSpec((1, D, Fd), lambda i, be, nu, tok: (be[blk(i, be, nu, tok)], 0, 0)),
                pl.BlockSpec((1, D, Fd), lambda i, be, nu, tok: (be[blk(i, be, nu, tok)], 0, 0)),
                pl.BlockSpec((1, Fd, D), lambda i, be, nu, tok: (be[blk(i, be, nu, tok)], 0, 0)),
            ],
            out_specs=pl.BlockSpec((R, D), lambda i, be, nu, tok: (blk(i, be, nu, tok), 0)),
            scratch_shapes=[pltpu.VMEM((2, R, D), F32), pltpu.VMEM((D, Fd), BF16), pltpu.VMEM((D, Fd), BF16),
                            pltpu.VMEM((Fd, D), BF16), pltpu.SemaphoreType.DMA((2,))],
        ),
        compiler_params=_cparams(1, vmem),
        name="experts",
    )(block_expert, n_used, slot_token, h2, w_gate, w_up, w_down)


def _combine_kernel(dest_ref, x1_ref, info_ref, yb_ref, p_ref, gp_ref, wpg_ref, wpp_ref, gf_ref, o_ref, ybuf, sem):
    tm = x1_ref.shape[0]
    step = pl.program_id(0)
    n_tiles = pl.num_programs(0) - 2
    K = TOPK_IN_GROUP

    def fetch(tile):
        slot = tile % 2
        for r in range(tm):
            for k in range(K):
                _row_copy(yb_ref, dest_ref[(tile * tm + r) * K + k], ybuf.at[slot * K + k], r, sem.at[slot]).start()

    def drain(tile):
        slot = tile % 2
        for _ in range(tm * K):
            _row_copy(yb_ref, 0, ybuf.at[0], 0, sem.at[slot]).wait()

    @pl.when(step == 0)
    def _():
        fetch(step)

    @pl.when(step > 0)
    def _():
        drain(step - 1)

    @pl.when((step > 0) & (step <= n_tiles))
    def _():
        tile = step - 1
        fetch(step)
        slot = tile % 2
        info = info_ref[...]
        x2 = x1_ref[...] + info[:, 2:3] * ybuf[slot * K] + info[:, 3:4] * ybuf[slot * K + 1]
        hp = _rms(x2, gp_ref[...]).astype(BF16)
        z = jnp.dot(hp, wpg_ref[...], preferred_element_type=F32)
        pp = jnp.dot(p_ref[...].astype(BF16), wpp_ref[...], preferred_element_type=F32)
        x3 = x2 + jax.nn.sigmoid(z) * pp
        o_ref[...] = _rms(x3, gf_ref[...])


def _combine(dest_flat, x1, info, yb, p, g_ple, w_ple_gate, w_ple_proj, g_final, tm=256):
    T, D = x1.shape
    n_tiles = T // tm
    dest_padded = jnp.concatenate([dest_flat, jnp.zeros((tm * TOPK_IN_GROUP,), I32)])
    vmem = 2 * (2 * _nbytes((tm, D), F32) + _nbytes((D, D), BF16) + _nbytes((PLE_DIM, D), BF16)
                + _nbytes((tm, PLE_DIM), F32)) + 10 * _nbytes((tm, D), F32)
    tile = lambda i, d: (jnp.clip(i - 1, 0, n_tiles - 1), 0)
    return pl.pallas_call(
        _combine_kernel,
        out_shape=jax.ShapeDtypeStruct((T, D), F32),
        grid_spec=pltpu.PrefetchScalarGridSpec(
            num_scalar_prefetch=1,
            grid=(n_tiles + 2,),
            in_specs=[
                pl.BlockSpec((tm, D), tile),
                pl.BlockSpec((tm, V7X_LANES), tile),
                pl.BlockSpec(memory_space=pl.ANY),
                pl.BlockSpec((tm, PLE_DIM), tile),
                pl.BlockSpec((1, D), lambda i, d: (0, 0)),
                pl.BlockSpec((D, D), lambda i, d: (0, 0)),
                pl.BlockSpec((PLE_DIM, D), lambda i, d: (0, 0)),
                pl.BlockSpec((1, D), lambda i, d: (0, 0)),
            ],
            out_specs=pl.BlockSpec((tm, D), tile),
            scratch_shapes=[pltpu.VMEM((2 * TOPK_IN_GROUP, tm, D), F32), pltpu.SemaphoreType.DMA((2,))],
        ),
        compiler_params=_cparams(1, vmem),
        name="combine",
    )(dest_padded, x1, info, yb, p, g_ple.reshape(1, D), w_ple_gate, w_ple_proj, g_final.reshape(1, D))


def _rope_tables(T):
    half = MOBA_HEAD_DIM // 2
    inv_freq = ROPE_THETA ** (-jnp.arange(half, dtype=F32) / half)
    ang = jnp.arange(T, dtype=F32)[:, None] * inv_freq[None, :]
    cos, sin = jnp.cos(ang), jnp.sin(ang)
    return jnp.concatenate([cos, cos], axis=1), jnp.concatenate([-sin, sin], axis=1)


def _mixers(x2d, g_mix, w_in, lb, hgrn_norm_g):
    T = x2d.shape[0]
    W = HGRN_WIDTH
    h = _rmsnorm(x2d, g_mix, BF16)
    w = w_in
    cos, sin = _rope_tables(T)
    log_lb = jnp.log(lb).reshape(1, W)
    log_1m = jnp.log1p(-lb).reshape(1, W)
    hq = _proj(h, w, 0 * W, W, _ep_silu, BF16)
    logf = _proj(h, w, 1 * W, W, _ep_logf, F32, col_extras=(log_lb, log_1m))
    hi = _proj(h, w, 2 * W, W, _ep_identity, BF16)
    hog = _proj(h, w, 3 * W, W, _ep_silu, BF16)
    scale = MOBA_HEAD_DIM ** -0.5
    mqt = _proj_t(h, w, 4 * W, functools.partial(_ep_rope, scale=scale), row_extras=(cos, sin))
    blk = jnp.arange(T, dtype=I32)[:, None] // MOBA_BLOCK
    blk_onehot = (blk == jnp.arange(V7X_LANES, dtype=I32)[None, :]).astype(F32)
    mk = _proj(h, w, 5 * W, W, _ep_rope_aug, BF16, row_extras=(cos, sin, blk_onehot), widen=2)
    mvt = _proj_t(h, w, 6 * W, _ep_identity, ones_rows=MOBA_VT_ROWS - MOBA_HEAD_DIM)
    gates = _proj(h, w, 7 * W, 2 * D_MODEL, _ep_sigmoid, BF16)
    o_hgrn = _hgrn(hq, logf, hi, hog, hgrn_norm_g)
    o_moba = _moba(mqt, mk, mvt)
    return o_hgrn, o_moba, gates


def _moe_plan(info, cnt, T):
    R = MOE_ROWS
    eid = info[:, 0:TOPK_IN_GROUP].astype(I32)
    rank = info[:, 4:4 + TOPK_IN_GROUP].astype(I32)
    counts = cnt[0, :N_EXPERTS].astype(I32)
    padded = (counts + R - 1) // R * R
    pad_end = jnp.cumsum(padded)
    pad_start = pad_end - padded
    dest = (pad_start[eid] + rank).reshape(-1)
    n_blocks = (T * TOPK_IN_GROUP) // R + N_EXPERTS
    block_expert = jnp.minimum(
        jnp.searchsorted(pad_end, jnp.arange(n_blocks, dtype=I32) * R, side="right"), N_EXPERTS - 1).astype(I32)
    n_used = (pad_end[-1:] // R).astype(I32)
    slot_token = jnp.zeros(((n_blocks + 1) * R,), I32).at[dest].set(
        jnp.arange(T * TOPK_IN_GROUP, dtype=I32) // TOPK_IN_GROUP)
    return dest, slot_token, block_expert, n_used


def kernel(x, p, norm_mix_g, w_in, hgrn_lb_raw, hgrn_norm_g, w_up_hgrn, w_up_moba, w_out, norm_ffn_g,
           w_router_group, w_router_expert, w_exp_gate, w_exp_up, w_exp_down, norm_ple_g, w_ple_gate,
           w_ple_proj, norm_final_g):
    B, T, D = x.shape
    assert B == 1 and D == D_MODEL and w_in.shape[0] == 1 and T % (4 * MOBA_BLOCK) == 0
    lower_bounds = jnp.cumsum(jax.nn.softmax(hgrn_lb_raw.astype(F32), axis=0), axis=0)
    x2d = x.reshape(T, D)
    o_hgrn, o_moba, gates = _mixers(x2d, norm_mix_g[0], w_in[0], lower_bounds[0], hgrn_norm_g[0])
    merged = _merge(o_hgrn, o_moba, gates, w_up_hgrn[0].astype(BF16), w_up_moba[0].astype(BF16))
    x1, h2 = _outproj(merged, x2d, w_out[0].astype(BF16), norm_ffn_g[0])
    w_router = jnp.pad(jnp.concatenate([w_router_group[0], w_router_expert[0]], axis=1),
                       ((0, 0), (0, V7X_LANES - N_GROUPS - N_EXPERTS))).astype(BF16)
    info, cnt = _router(h2, w_router)
    dest, slot_token, block_expert, n_used = _moe_plan(info, cnt, T)
    yb = _experts(block_expert, n_used, slot_token, h2, w_exp_gate[0], w_exp_up[0], w_exp_down[0])
    out = _combine(dest, x1, info, yb, p[0].reshape(T, PLE_DIM), norm_ple_g[0], w_ple_gate[0].astype(BF16),
                   w_ple_proj[0].astype(BF16), norm_final_g)
    return out.reshape(B, T, D)
```

```python
import functools

import jax
import jax.numpy as jnp
from jax import lax
from jax.experimental import pallas as pl
from jax.experimental.pallas import tpu as pltpu

F32 = jnp.float32
BF16 = jnp.bfloat16
I32 = jnp.int32

D_MODEL = 2048
PLE_DIM = 256
HGRN_HEADS = 8
HGRN_HEAD_DIM = 128
HGRN_WIDTH = HGRN_HEADS * HGRN_HEAD_DIM
MOBA_HEADS = 8
MOBA_HEAD_DIM = 128
MOBA_WIDTH = MOBA_HEADS * MOBA_HEAD_DIM
MOBA_BLOCK = 256
MOBA_TOPK = 3
ROPE_THETA = 10000.0
N_GROUPS = 4
EXPERTS_PER_GROUP = 8
N_EXPERTS = N_GROUPS * EXPERTS_PER_GROUP
TOPK_IN_GROUP = 2
D_EXPERT = 512
EPS = 1e-6
NEG_INF = -1e30

V7X_LANES = 128
V7X_SUBLANES = 8
V7X_VMEM_BUDGET_BYTES = 56 * 1024 * 1024

HGRN_CHUNK = 128
MOE_ROWS = 256


def _cparams(n_grid, vmem_bytes):
    return pltpu.CompilerParams(
        dimension_semantics=("arbitrary",) * n_grid,
        vmem_limit_bytes=int(min(max(vmem_bytes, 16 * 1024 * 1024), V7X_VMEM_BUDGET_BYTES)),
    )


def _nbytes(shape, dtype):
    n = 1
    for s in shape:
        n *= s
    return n * jnp.dtype(dtype).itemsize


def _rms(x, g):
    ms = jnp.mean(x * x, axis=-1, keepdims=True)
    return x * lax.rsqrt(ms + EPS) * g


def _rmsnorm_kernel(x_ref, g_ref, o_ref):
    o_ref[...] = _rms(x_ref[...], g_ref[...]).astype(o_ref.dtype)


def _rmsnorm(x, g, out_dtype, tm=512):
    T, D = x.shape
    return pl.pallas_call(
        _rmsnorm_kernel,
        out_shape=jax.ShapeDtypeStruct((T, D), out_dtype),
        grid=(T // tm,),
        in_specs=[pl.BlockSpec((tm, D), lambda i: (i, 0)), pl.BlockSpec((1, D), lambda i: (0, 0))],
        out_specs=pl.BlockSpec((tm, D), lambda i: (i, 0)),
        compiler_params=_cparams(1, 4 * _nbytes((tm, D), F32)),
        name="rmsnorm",
    )(x, g.reshape(1, D))


def _ep_identity(acc):
    return acc


def _ep_silu(acc):
    return acc * jax.nn.sigmoid(acc)


def _ep_sigmoid(acc):
    return jax.nn.sigmoid(acc)


def _ep_logf(acc, la_ref, lc_ref):
    ls = jnp.minimum(acc, 0.0) - jnp.log(1.0 + jnp.exp(-jnp.abs(acc)))
    u = la_ref[...]
    v = lc_ref[...] + ls
    return jnp.maximum(u, v) + jnp.log(1.0 + jnp.exp(-jnp.abs(u - v)))


def _ep_rope(acc, cos_ref, sin_ref, *, scale):
    cos = cos_ref[...]
    sin = sin_ref[...]
    outs = []
    for hh in range(acc.shape[1] // MOBA_HEAD_DIM):
        a = acc[:, hh * MOBA_HEAD_DIM:(hh + 1) * MOBA_HEAD_DIM]
        r = pltpu.roll(a, MOBA_HEAD_DIM // 2, axis=1)
        outs.append((a * cos + r * sin) * scale)
    return jnp.concatenate(outs, axis=1)


def _ep_rope_aug(acc, cos_ref, sin_ref, oh_ref):
    cos = cos_ref[...]
    sin = sin_ref[...]
    oh = oh_ref[...]
    outs = []
    for hh in range(acc.shape[1] // MOBA_HEAD_DIM):
        a = acc[:, hh * MOBA_HEAD_DIM:(hh + 1) * MOBA_HEAD_DIM]
        outs.append(a * cos + pltpu.roll(a, MOBA_HEAD_DIM // 2, axis=1) * sin)
        outs.append(oh)
    return jnp.concatenate(outs, axis=1)


def _cast_weight_once(w_ref, wb_ref, row_axis):
    @pl.when(pl.program_id(row_axis) == 0)
    def _():
        wb_ref[...] = w_ref[...].astype(wb_ref.dtype)


def _proj_kernel(h_ref, w_ref, *refs, epilogue):
    *extra, o_ref, wb_ref = refs
    _cast_weight_once(w_ref, wb_ref, 1)
    acc = jnp.dot(h_ref[...], wb_ref[...], preferred_element_type=F32)
    o_ref[...] = epilogue(acc, *extra).astype(o_ref.dtype)


def _proj(h, w, col0, ncols, epilogue, out_dtype, row_extras=(), col_extras=(), tm=1024, tn=1024, widen=1):
    T, K = h.shape
    tn = min(tn, ncols)
    tm = min(tm, T)
    cb = col0 // tn
    otn = widen * tn
    in_specs = [
        pl.BlockSpec((tm, K), lambda j, i: (i, 0)),
        pl.BlockSpec((K, tn), lambda j, i: (0, cb + j)),
    ]
    for e in row_extras:
        in_specs.append(pl.BlockSpec((tm, e.shape[1]), lambda j, i: (i, 0)))
    for e in col_extras:
        in_specs.append(pl.BlockSpec((1, tn), lambda j, i: (0, j)))
    vmem = 2 * (_nbytes((tm, K), h.dtype) + _nbytes((K, tn), w.dtype) + _nbytes((tm, otn), out_dtype))
    vmem += 3 * _nbytes((tm, otn), F32) + _nbytes((K, tn), BF16)
    return pl.pallas_call(
        functools.partial(_proj_kernel, epilogue=epilogue),
        out_shape=jax.ShapeDtypeStruct((T, widen * ncols), out_dtype),
        grid=(ncols // tn, T // tm),
        in_specs=in_specs,
        out_specs=pl.BlockSpec((tm, otn), lambda j, i: (i, j)),
        scratch_shapes=[pltpu.VMEM((K, tn), BF16)],
        compiler_params=_cparams(2, vmem),
        name="proj",
    )(h, w, *row_extras, *col_extras)


MOBA_VT_ROWS = MOBA_HEAD_DIM + 16
MOBA_BLOCKS_PER_STEP = 4


def _proj_t_kernel(h_ref, w_ref, *refs, epilogue, ones_rows):
    *extra, o_ref, wb_ref = refs
    BS, HD = MOBA_BLOCK, MOBA_HEAD_DIM
    _cast_weight_once(w_ref, wb_ref, 0)
    acc = epilogue(jnp.dot(h_ref[...], wb_ref[...], preferred_element_type=F32), *extra)
    ones = jnp.ones((ones_rows, BS), F32) if ones_rows else None
    for b in range(acc.shape[0] // BS):
        parts = []
        for hh in range(acc.shape[1] // HD):
            parts.append(acc[b * BS:(b + 1) * BS, hh * HD:(hh + 1) * HD].T)
            if ones_rows:
                parts.append(ones)
        o_ref[b] = jnp.concatenate(parts, axis=0).astype(o_ref.dtype)


def _proj_t(h, w, col0, epilogue, row_extras=(), ones_rows=0, tm=1024):
    T, K = h.shape
    tn = MOBA_WIDTH
    tm = min(tm, T)
    cb = col0 // tn
    rows = MOBA_HEADS * (MOBA_HEAD_DIM + ones_rows)
    in_specs = [pl.BlockSpec((tm, K), lambda i: (i, 0)), pl.BlockSpec((K, tn), lambda i: (0, cb))]
    for e in row_extras:
        in_specs.append(pl.BlockSpec((tm, e.shape[1]), lambda i: (i, 0)))
    vmem = 2 * (_nbytes((tm, K), h.dtype) + _nbytes((K, tn), w.dtype) + _nbytes((tm, 2 * tn), BF16))
    vmem += 4 * _nbytes((tm, tn), F32) + _nbytes((K, tn), BF16)
    return pl.pallas_call(
        functools.partial(_proj_t_kernel, epilogue=epilogue, ones_rows=ones_rows),
        out_shape=jax.ShapeDtypeStruct((T // MOBA_BLOCK, rows, MOBA_BLOCK), BF16),
        grid=(T // tm,),
        in_specs=in_specs,
        out_specs=pl.BlockSpec((tm // MOBA_BLOCK, rows, MOBA_BLOCK), lambda i: (i, 0, 0)),
        scratch_shapes=[pltpu.VMEM((K, tn), BF16)],
        compiler_params=_cparams(1, vmem),
        name="proj_t",
    )(h, w, *row_extras)


def _hgrn_kernel(q_ref, g_ref, v_ref, og_ref, ng_ref, o_ref, st_ref):
    W, HD, NH, C, S = HGRN_WIDTH, HGRN_HEAD_DIM, HGRN_HEADS, HGRN_CHUNK, V7X_SUBLANES
    J = C // S

    @pl.when(pl.program_id(0) == 0)
    def _():
        st_ref[...] = jnp.zeros_like(st_ref)

    def r3(x):
        return x.astype(F32).reshape(J, S, W)

    def sub_bcast(x3, r):
        return jnp.broadcast_to(x3[:, r:r + 1, :], x3.shape)

    g3, q3, v3 = r3(g_ref[...]), r3(q_ref[...]), r3(v_ref[...])
    sub = lax.broadcasted_iota(I32, (1, S, W), 1)

    c3 = g3
    for s in (1, 2, 4):
        c3 = c3 + jnp.where(sub >= s, pltpu.roll(c3, s, axis=1), 0.0)
    run = jnp.zeros((1, 1, W), F32)
    carry = []
    for j in range(J):
        carry.append(run)
        run = run + c3[j:j + 1, S - 1:S, :]
    b3 = c3 + jnp.concatenate(carry, axis=0)
    bC = run

    k3 = 1.0 - jnp.exp(g3)
    qe3 = q3 * jnp.exp(b3)
    ks3 = k3 * jnp.exp(bC - b3)

    levels = [(0, q3, k3)]
    ref1 = jnp.where(sub % 2 == 0, b3, pltpu.roll(b3, 1, axis=1))
    ref2 = jnp.where(sub < 4, sub_bcast(b3, 1), sub_bcast(b3, 5))
    ref4 = sub_bcast(b3, 3)
    for lvl, (ref, upper) in enumerate(((ref1, sub % 2 == 1), (ref2, sub % 4 >= 2), (ref4, sub >= 4)), start=1):
        e = jnp.exp(-jnp.abs(b3 - ref))
        levels.append((lvl, jnp.where(upper, q3 * e, 0.0), jnp.where(upper, 0.0, k3 * e)))
    zero_group = jnp.zeros((1, S, W), F32)
    for lvl, half in enumerate((1, 2, 4, 8), start=4):
        qparts, kparts = [], []
        for j in range(J):
            jr = (j // (2 * half)) * (2 * half) + half - 1
            ref = b3[jr:jr + 1, S - 1:S, :]
            if (j % (2 * half)) >= half:
                qparts.append(q3[j:j + 1] * jnp.exp(b3[j:j + 1] - ref))
                kparts.append(zero_group)
            else:
                qparts.append(zero_group)
                kparts.append(k3[j:j + 1] * jnp.exp(ref - b3[j:j + 1]))
        levels.append((lvl, jnp.concatenate(qparts, axis=0), jnp.concatenate(kparts, axis=0)))

    tr = lax.broadcasted_iota(I32, (C, C), 0)
    tc = lax.broadcasted_iota(I32, (C, C), 1)
    xr = tr ^ tc
    code = jnp.zeros((C, C), I32)
    for lvl in range(1, 8):
        code = jnp.where(xr >= (1 << (lvl - 1)), lvl, code)
    code = jnp.where(tc > tr, -1, code)

    def mat(x3, h):
        return x3.reshape(C, W)[:, h * HD:(h + 1) * HD].astype(BF16)

    nt = (((1,), (1,)), ((), ()))
    tn = (((0,), (0,)), ((), ()))
    ebc = jnp.exp(bC).reshape(1, W)
    ng = ng_ref[...]
    for h in range(NH):
        a_mat = jnp.zeros((C, C), F32)
        for lvl, qr, kr in levels:
            s = lax.dot_general(mat(qr, h), mat(kr, h), nt, preferred_element_type=F32)
            a_mat = jnp.where(code == lvl, s, a_mat)
        vh = mat(v3, h)
        st = st_ref[h]
        o = jnp.dot(a_mat.astype(BF16), vh, preferred_element_type=F32)
        o = o + lax.dot_general(mat(qe3, h), st.astype(BF16), nt, preferred_element_type=F32)
        o = _rms(o, ng) * og_ref[:, h * HD:(h + 1) * HD].astype(F32)
        o_ref[:, h * HD:(h + 1) * HD] = o.astype(o_ref.dtype)
        st_ref[h] = st * ebc[:, h * HD:(h + 1) * HD] + lax.dot_general(
            vh, mat(ks3, h), tn, preferred_element_type=F32)


def _hgrn(q, logf, v, og, norm_g):
    T, W = q.shape
    C = HGRN_CHUNK
    blk = pl.BlockSpec((C, W), lambda c: (c, 0))
    vmem = 64 * _nbytes((C, W), F32)
    return pl.pallas_call(
        _hgrn_kernel,
        out_shape=jax.ShapeDtypeStruct((T, W), BF16),
        grid=(T // C,),
        in_specs=[blk, blk, blk, blk, pl.BlockSpec((1, HGRN_HEAD_DIM), lambda c: (0, 0))],
        out_specs=blk,
        scratch_shapes=[pltpu.VMEM((HGRN_HEADS, HGRN_HEAD_DIM, HGRN_HEAD_DIM), F32)],
        compiler_params=_cparams(1, vmem),
        name="hgrn",
    )(q, logf, v, og, norm_g.reshape(1, HGRN_HEAD_DIM))


def _moba_kernel(qt_ref, k_ref, vt_ref, o_ref, km_ref):
    BS, HD, VR = MOBA_BLOCK, MOBA_HEAD_DIM, MOBA_VT_ROWS
    T = k_ref.shape[0]
    NB = T // BS
    G = o_ref.shape[1] // HD
    cur = pl.program_id(1)

    @pl.when(cur == 0)
    def _():
        for g in range(G):
            kf = k_ref[:, 2 * g * HD:(2 * g + 1) * HD].astype(F32).reshape(NB, BS, HD)
            km_ref[g] = jnp.sum(kf, axis=1) * (1.0 / BS)

    blk = lax.broadcasted_iota(I32, (NB, BS), 0)
    key = lax.broadcasted_iota(I32, (BS, BS), 0)
    qry = lax.broadcasted_iota(I32, (BS, BS), 1)
    pad = jnp.zeros((V7X_LANES - NB, BS), F32)
    r0 = pl.multiple_of(cur * BS, BS)
    qts = [qt_ref[0, g * HD:(g + 1) * HD, :] for g in range(G)]
    gts = [jnp.dot(km_ref[g].astype(BF16), qts[g], preferred_element_type=F32) for g in range(G)]
    s_own = [jnp.dot(k_ref[pl.ds(r0, BS), 2 * g * HD:(2 * g + 1) * HD], qts[g], preferred_element_type=F32)
             for g in range(G)]
    qcs = []
    for g in range(G):
        gt = jnp.where(blk < cur, gts[g], NEG_INF)
        sel = jnp.zeros((NB, BS), jnp.bool_)
        for _ in range(MOBA_TOPK):
            mx = jnp.max(gt, axis=0, keepdims=True)
            idx = jnp.min(jnp.where(gt == mx, blk, NB), axis=0, keepdims=True)
            pick = (blk == idx) & (mx > 0.5 * NEG_INF)
            sel = sel | pick
            gt = jnp.where(pick, NEG_INF, gt)
        pen = jnp.concatenate([jnp.where(sel, 0.0, NEG_INF), pad], axis=0).astype(BF16)
        qcs.append(jnp.concatenate([qts[g], pen], axis=0))

    def scores(n):
        rn = pl.multiple_of(n * BS, BS)
        return tuple(jnp.dot(k_ref[pl.ds(rn, BS), 2 * g * HD:(2 * g + 2) * HD], qcs[g], preferred_element_type=F32)
                     for g in range(G))

    ms, ps = [], []
    for g in range(G):
        s = jnp.where(key <= qry, s_own[g], NEG_INF)
        m0 = jnp.max(s, axis=0, keepdims=True)
        ms.append(m0)
        ps.append(jnp.exp(s - m0).astype(BF16))
    accs = [jnp.dot(vt_ref[cur, g * VR:(g + 1) * VR, :], ps[g], preferred_element_type=F32)
            for g in range(G)]

    KB = MOBA_BLOCKS_PER_STEP

    def body(c, carry):
        ms, accs = carry
        r = pl.multiple_of(c * KB * BS, KB * BS)
        sns = [jnp.dot(k_ref[pl.ds(r, KB * BS), 2 * g * HD:(2 * g + 2) * HD], qcs[g], preferred_element_type=F32)
               for g in range(G)]
        new_ms, alphas, pns = [], [], []
        for g in range(G):
            m_new = jnp.maximum(ms[g], jnp.max(sns[g], axis=0, keepdims=True))
            alphas.append(jnp.exp(ms[g] - m_new))
            pns.append(jnp.exp(sns[g] - m_new).astype(BF16))
            new_ms.append(m_new)
        new_accs = []
        for g in range(G):
            pv = alphas[g] * accs[g]
            for j in range(KB):
                pv = pv + jnp.dot(vt_ref[KB * c + j, g * VR:(g + 1) * VR, :], pns[g][j * BS:(j + 1) * BS],
                                  preferred_element_type=F32)
            new_accs.append(pv)
        return tuple(new_ms), tuple(new_accs)

    _, accs = lax.fori_loop(0, (cur + KB - 1) // KB, body, (tuple(ms), tuple(accs)))
    for g in range(G):
        ot = accs[g][:HD, :] / accs[g][HD:HD + 1, :]
        o_ref[:, g * HD:(g + 1) * HD] = ot.T.astype(o_ref.dtype)


def _moba(mqt, mk_aug, mvt, heads_per_step=4):
    T = mk_aug.shape[0]
    BS, HD, G, VR = MOBA_BLOCK, MOBA_HEAD_DIM, heads_per_step, MOBA_VT_ROWS
    NB = T // BS
    vmem = _nbytes((T, 2 * G * HD), BF16) + _nbytes((NB, G * VR, BS), BF16) + 8 * _nbytes((BS, G * HD), BF16)
    vmem += 16 * G * _nbytes((BS, BS), F32)
    resident = pl.Buffered(1)
    return pl.pallas_call(
        _moba_kernel,
        out_shape=jax.ShapeDtypeStruct((T, MOBA_WIDTH), BF16),
        grid=(MOBA_HEADS // G, NB),
        in_specs=[
            pl.BlockSpec((1, G * HD, BS), lambda h, i: (i, h, 0)),
            pl.BlockSpec((T, 2 * G * HD), lambda h, i: (0, h), pipeline_mode=resident),
            pl.BlockSpec((NB, G * VR, BS), lambda h, i: (0, h, 0), pipeline_mode=resident),
        ],
        out_specs=pl.BlockSpec((BS, G * HD), lambda h, i: (i, h)),
        scratch_shapes=[pltpu.VMEM((G, NB, HD), F32)],
        compiler_params=_cparams(2, vmem),
        name="moba",
    )(mqt, mk_aug, mvt)


def _merge_kernel(oh_ref, om_ref, ga_ref, gb_ref, wh_ref, wm_ref, o_ref):
    a = jnp.dot(oh_ref[...], wh_ref[...], preferred_element_type=F32)
    b = jnp.dot(om_ref[...], wm_ref[...], preferred_element_type=F32)
    o_ref[...] = (ga_ref[...].astype(F32) * a + gb_ref[...].astype(F32) * b).astype(o_ref.dtype)


def _merge(o_hgrn, o_moba, gates, w_up_hgrn, w_up_moba, tm=512):
    T = o_hgrn.shape[0]
    D = D_MODEL
    vmem = 2 * (_nbytes((tm, HGRN_WIDTH), F32) + _nbytes((tm, MOBA_WIDTH), BF16) + 3 * _nbytes((tm, D), BF16)
                + 2 * _nbytes((HGRN_WIDTH, D), BF16)) + 3 * _nbytes((tm, D), F32)
    return pl.pallas_call(
        _merge_kernel,
        out_shape=jax.ShapeDtypeStruct((T, D), BF16),
        grid=(T // tm,),
        in_specs=[
            pl.BlockSpec((tm, HGRN_WIDTH), lambda i: (i, 0)),
            pl.BlockSpec((tm, MOBA_WIDTH), lambda i: (i, 0)),
            pl.BlockSpec((tm, D), lambda i: (i, 0)),
            pl.BlockSpec((tm, D), lambda i: (i, 1)),
            pl.BlockSpec((HGRN_WIDTH, D), lambda i: (0, 0)),
            pl.BlockSpec((MOBA_WIDTH, D), lambda i: (0, 0)),
        ],
        out_specs=pl.BlockSpec((tm, D), lambda i: (i, 0)),
        compiler_params=_cparams(1, vmem),
        name="merge",
    )(o_hgrn, o_moba, gates, gates, w_up_hgrn, w_up_moba)


def _outproj_kernel(m_ref, x_ref, w_ref, g_ref, x1_ref, h2_ref):
    x1 = x_ref[...] + jnp.dot(m_ref[...], w_ref[...], preferred_element_type=F32)
    x1_ref[...] = x1
    h2_ref[...] = _rms(x1, g_ref[...])


def _outproj(merged, x, w_out, g_ffn, tm=256):
    T, D = x.shape
    vmem = 2 * (_nbytes((tm, D), BF16) + 3 * _nbytes((tm, D), F32) + _nbytes((D, D), BF16)) + 2 * _nbytes((tm, D), F32)
    return pl.pallas_call(
        _outproj_kernel,
        out_shape=(jax.ShapeDtypeStruct((T, D), F32), jax.ShapeDtypeStruct((T, D), F32)),
        grid=(T // tm,),
        in_specs=[
            pl.BlockSpec((tm, D), lambda i: (i, 0)),
            pl.BlockSpec((tm, D), lambda i: (i, 0)),
            pl.BlockSpec((D, D), lambda i: (0, 0)),
            pl.BlockSpec((1, D), lambda i: (0, 0)),
        ],
        out_specs=(pl.BlockSpec((tm, D), lambda i: (i, 0)), pl.BlockSpec((tm, D), lambda i: (i, 0))),
        compiler_params=_cparams(1, vmem),
        name="outproj",
    )(merged, x, w_out, g_ffn.reshape(1, D))


def _router_kernel(h_ref, w_ref, info_ref, cnt_ref, carry_ref):
    tm = h_ref.shape[0]

    @pl.when(pl.program_id(0) == 0)
    def _():
        carry_ref[...] = jnp.zeros_like(carry_ref)

    logits = jnp.dot(h_ref[...].astype(BF16), w_ref[...], preferred_element_type=F32)
    lane = lax.broadcasted_iota(I32, (tm, V7X_LANES), 1)
    is_g = lane < N_GROUPS
    gl = jnp.where(is_g, logits, NEG_INF)
    gmax = jnp.max(gl, axis=1, keepdims=True)
    g_sel = jnp.min(jnp.where(gl == gmax, lane, V7X_LANES), axis=1, keepdims=True)
    gsum = jnp.sum(jnp.where(is_g, jnp.exp(gl - gmax), 0.0), axis=1, keepdims=True)
    p_group = 1.0 / gsum
    lo = N_GROUPS + EXPERTS_PER_GROUP * g_sel
    emask = (lane >= lo) & (lane < lo + EXPERTS_PER_GROUP)
    el = jnp.where(emask, logits, NEG_INF)
    e1 = jnp.max(el, axis=1, keepdims=True)
    i1 = jnp.min(jnp.where((el == e1) & emask, lane, V7X_LANES), axis=1, keepdims=True)
    emask2 = emask & (lane != i1)
    el2 = jnp.where(emask2, logits, NEG_INF)
    e2 = jnp.max(el2, axis=1, keepdims=True)
    i2 = jnp.min(jnp.where((el2 == e2) & emask2, lane, V7X_LANES), axis=1, keepdims=True)
    r = jnp.exp(e2 - e1)
    w1 = p_group / (1.0 + r)
    w2 = p_group * r / (1.0 + r)
    eid1 = i1 - N_GROUPS
    eid2 = i2 - N_GROUPS
    oh1 = jnp.where(lane == eid1, 1.0, 0.0)
    oh2 = jnp.where(lane == eid2, 1.0, 0.0)
    cnt = oh1 + oh2
    tri = jnp.where(lax.broadcasted_iota(I32, (tm, tm), 0) > lax.broadcasted_iota(I32, (tm, tm), 1), 1.0, 0.0)
    before = jnp.dot(tri.astype(BF16), cnt.astype(BF16), preferred_element_type=F32) + carry_ref[...]
    rank1 = jnp.sum(oh1 * before, axis=1, keepdims=True)
    rank2 = jnp.sum(oh2 * before, axis=1, keepdims=True)
    carry_ref[...] = carry_ref[...] + jnp.sum(cnt, axis=0, keepdims=True)
    info = jnp.zeros((tm, V7X_LANES), F32)
    for k, val in enumerate((eid1.astype(F32), eid2.astype(F32), w1, w2, rank1, rank2)):
        info = jnp.where(lane == k, val, info)
    info_ref[...] = info
    cnt_ref[...] = carry_ref[...]


def _router(h2, w_router, tm=256):
    T, D = h2.shape
    vmem = 2 * (_nbytes((tm, D), F32) + _nbytes((D, V7X_LANES), BF16)) + 16 * _nbytes((tm, V7X_LANES), F32) + (1 << 22)
    return pl.pallas_call(
        _router_kernel,
        out_shape=(jax.ShapeDtypeStruct((T, V7X_LANES), F32), jax.ShapeDtypeStruct((1, V7X_LANES), F32)),
        grid=(T // tm,),
        in_specs=[pl.BlockSpec((tm, D), lambda i: (i, 0)), pl.BlockSpec((D, V7X_LANES), lambda i: (0, 0))],
        out_specs=(pl.BlockSpec((tm, V7X_LANES), lambda i: (i, 0)), pl.BlockSpec((1, V7X_LANES), lambda i: (0, 0))),
        scratch_shapes=[pltpu.VMEM((1, V7X_LANES), F32)],
        compiler_params=_cparams(1, vmem),
        name="router",
    )(h2, w_router)


def _row_copy(src_ref, src_row, dst_ref, dst_row, sem):
    return pltpu.make_async_copy(src_ref.at[pl.ds(src_row, 1), :], dst_ref.at[pl.ds(dst_row, 1), :], sem)


ROW_DMA_PRIORITY = 1


def _slot_token_kernel(dest_ref, o_ref):
    def clear(i, c):
        o_ref[i] = 0
        return c

    def put(t, c):
        for k in range(TOPK_IN_GROUP):
            o_ref[dest_ref[t * TOPK_IN_GROUP + k]] = t
        return c

    lax.fori_loop(0, o_ref.shape[0], clear, 0, unroll=8)
    lax.fori_loop(0, dest_ref.shape[0] // TOPK_IN_GROUP, put, 0, unroll=8)


def _slot_token(dest_flat, n_slots):
    return pl.pallas_call(
        _slot_token_kernel,
        out_shape=jax.ShapeDtypeStruct((n_slots,), I32),
        in_specs=[pl.BlockSpec(memory_space=pltpu.SMEM)],
        out_specs=pl.BlockSpec(memory_space=pltpu.SMEM),
        name="slot_token",
    )(dest_flat)


def _expert_kernel(be_ref, nu_ref, tok_ref, h_ref, wg_ref, wu_ref, wd_ref, y_ref, xbuf, wgb_ref, wub_ref, wdb_ref,
                   sem):
    R = MOE_ROWS
    step = pl.program_id(0)
    b = step - 1
    n_used = nu_ref[0]
    used = (b >= 0) & (b < n_used)

    def fetch(blk):
        slot = blk % 2
        for r in range(R):
            _row_copy(h_ref, tok_ref[blk * R + r], xbuf.at[slot], r, sem.at[slot]).start(priority=ROW_DMA_PRIORITY)

    def drain(blk):
        slot = blk % 2
        for _ in range(R):
            _row_copy(h_ref, 0, xbuf.at[0], 0, sem.at[slot]).wait()

    @pl.when(step == 0)
    def _():
        fetch(step)

    @pl.when((b >= 0) & (b <= n_used))
    def _():
        drain(b)

    @pl.when(used & ((b == 0) | (be_ref[jnp.maximum(b, 0)] != be_ref[jnp.maximum(b - 1, 0)])))
    def _():
        wgb_ref[...] = wg_ref[0].astype(BF16)
        wub_ref[...] = wu_ref[0].astype(BF16)
        wdb_ref[...] = wd_ref[0].astype(BF16)

    @pl.when(used)
    def _():
        fetch(step)
        x = xbuf[b % 2].astype(BF16)
        a = jnp.dot(x, wgb_ref[...], preferred_element_type=F32)
        u = jnp.dot(x, wub_ref[...], preferred_element_type=F32)
        hm = (a * jax.nn.sigmoid(a) * u).astype(BF16)
        y_ref[...] = jnp.dot(hm, wdb_ref[...], preferred_element_type=F32)

    @pl.when((b >= n_used))
    def _():
        y_ref[...] = jnp.zeros_like(y_ref)


def _experts(block_expert, n_used, slot_token, h2, w_gate, w_up, w_down):
    T, D = h2.shape
    R, Fd = MOE_ROWS, D_EXPERT
    m_pad = slot_token.shape[0] - R
    n_blocks = m_pad // R
    vmem = 2 * (_nbytes((R, D), F32) + 3 * _nbytes((D, Fd), F32)) + 3 * _nbytes((D, Fd), BF16)
    vmem += 6 * _nbytes((R, D), F32)
    blk = lambda i, be, nu, tok: jnp.clip(i - 1, 0, n_blocks - 1)
    return pl.pallas_call(
        _expert_kernel,
        out_shape=jax.ShapeDtypeStruct((m_pad, D), F32),
        grid_spec=pltpu.PrefetchScalarGridSpec(
            num_scalar_prefetch=3,
            grid=(n_blocks + 1,),
            in_specs=[
                pl.BlockSpec(memory_space=pl.ANY),
                pl.BlockSpec((1, D, Fd), lambda i, be, nu, tok: (be[blk(i, be, nu, tok)], 0, 0)),
                pl.BlockSpec((1, D, Fd), lambda i, be, nu, tok: (be[blk(i, be, nu, tok)], 0, 0)),
                pl.BlockSpec((1, Fd, D), lambda i, be, nu, tok: (be[blk(i, be, nu, tok)], 0, 0)),
            ],
            out_specs=pl.BlockSpec((R, D), lambda i, be, nu, tok: (blk(i, be, nu, tok), 0)),
            scratch_shapes=[pltpu.VMEM((2, R, D), F32), pltpu.VMEM((D, Fd), BF16), pltpu.VMEM((D, Fd), BF16),
                            pltpu.VMEM((Fd, D), BF16), pltpu.SemaphoreType.DMA((2,))],
        ),
        compiler_params=_cparams(1, vmem),
        name="experts",
    )(block_expert, n_used, slot_token, h2, w_gate, w_up, w_down)


def _combine_kernel(dest_ref, x1_ref, info_ref, yb_ref, p_ref, gp_ref, wpg_ref, wpp_ref, gf_ref, o_ref, ybuf, sem):
    tm = x1_ref.shape[0]
    step = pl.program_id(0)
    n_tiles = pl.num_programs(0) - 2
    K = TOPK_IN_GROUP

    def fetch(tile):
        slot = tile % 2
        for r in range(tm):
            for k in range(K):
                _row_copy(yb_ref, dest_ref[(tile * tm + r) * K + k], ybuf.at[slot * K + k], r,
                          sem.at[slot]).start(priority=ROW_DMA_PRIORITY)

    def drain(tile):
        slot = tile % 2
        for _ in range(tm * K):
            _row_copy(yb_ref, 0, ybuf.at[0], 0, sem.at[slot]).wait()

    @pl.when(step == 0)
    def _():
        fetch(step)

    @pl.when(step > 0)
    def _():
        drain(step - 1)

    @pl.when((step > 0) & (step <= n_tiles))
    def _():
        tile = step - 1
        fetch(step)
        slot = tile % 2
        info = info_ref[...]
        x2 = x1_ref[...] + info[:, 2:3] * ybuf[slot * K] + info[:, 3:4] * ybuf[slot * K + 1]
        hp = _rms(x2, gp_ref[...]).astype(BF16)
        z = jnp.dot(hp, wpg_ref[...], preferred_element_type=F32)
        pp = jnp.dot(p_ref[...].astype(BF16), wpp_ref[...], preferred_element_type=F32)
        x3 = x2 + jax.nn.sigmoid(z) * pp
        o_ref[...] = _rms(x3, gf_ref[...])


def _combine(dest_flat, x1, info, yb, p, g_ple, w_ple_gate, w_ple_proj, g_final, tm=256):
    T, D = x1.shape
    n_tiles = T // tm
    dest_padded = jnp.concatenate([dest_flat, jnp.zeros((tm * TOPK_IN_GROUP,), I32)])
    vmem = 2 * (2 * _nbytes((tm, D), F32) + _nbytes((D, D), BF16) + _nbytes((PLE_DIM, D), BF16)
                + _nbytes((tm, PLE_DIM), F32)) + 10 * _nbytes((tm, D), F32)
    tile = lambda i, d: (jnp.clip(i - 1, 0, n_tiles - 1), 0)
    return pl.pallas_call(
        _combine_kernel,
        out_shape=jax.ShapeDtypeStruct((T, D), F32),
        grid_spec=pltpu.PrefetchScalarGridSpec(
            num_scalar_prefetch=1,
            grid=(n_tiles + 2,),
            in_specs=[
                pl.BlockSpec((tm, D), tile),
                pl.BlockSpec((tm, V7X_LANES), tile),
                pl.BlockSpec(memory_space=pl.ANY),
                pl.BlockSpec((tm, PLE_DIM), tile),
                pl.BlockSpec((1, D), lambda i, d: (0, 0)),
                pl.BlockSpec((D, D), lambda i, d: (0, 0)),
                pl.BlockSpec((PLE_DIM, D), lambda i, d: (0, 0)),
                pl.BlockSpec((1, D), lambda i, d: (0, 0)),
            ],
            out_specs=pl.BlockSpec((tm, D), tile),
            scratch_shapes=[pltpu.VMEM((2 * TOPK_IN_GROUP, tm, D), F32), pltpu.SemaphoreType.DMA((2,))],
        ),
        compiler_params=_cparams(1, vmem),
        name="combine",
    )(dest_padded, x1, info, yb, p, g_ple.reshape(1, D), w_ple_gate, w_ple_proj, g_final.reshape(1, D))


def _rope_tables(T):
    half = MOBA_HEAD_DIM // 2
    inv_freq = ROPE_THETA ** (-jnp.arange(half, dtype=F32) / half)
    ang = jnp.arange(T, dtype=F32)[:, None] * inv_freq[None, :]
    cos, sin = jnp.cos(ang), jnp.sin(ang)
    return jnp.concatenate([cos, cos], axis=1), jnp.concatenate([-sin, sin], axis=1)


def _mixers(x2d, g_mix, w_in, lb, hgrn_norm_g):
    T = x2d.shape[0]
    W = HGRN_WIDTH
    h = _rmsnorm(x2d, g_mix, BF16)
    w = w_in
    cos, sin = _rope_tables(T)
    log_lb = jnp.log(lb).reshape(1, W)
    log_1m = jnp.log1p(-lb).reshape(1, W)
    hq = _proj(h, w, 0 * W, W, _ep_silu, BF16)
    logf = _proj(h, w, 1 * W, W, _ep_logf, F32, col_extras=(log_lb, log_1m))
    hi = _proj(h, w, 2 * W, W, _ep_identity, BF16)
    hog = _proj(h, w, 3 * W, W, _ep_silu, BF16)
    scale = MOBA_HEAD_DIM ** -0.5
    mqt = _proj_t(h, w, 4 * W, functools.partial(_ep_rope, scale=scale), row_extras=(cos, sin))
    blk = jnp.arange(T, dtype=I32)[:, None] // MOBA_BLOCK
    blk_onehot = (blk == jnp.arange(V7X_LANES, dtype=I32)[None, :]).astype(F32)
    mk = _proj(h, w, 5 * W, W, _ep_rope_aug, BF16, row_extras=(cos, sin, blk_onehot), widen=2)
    mvt = _proj_t(h, w, 6 * W, _ep_identity, ones_rows=MOBA_VT_ROWS - MOBA_HEAD_DIM)
    gates = _proj(h, w, 7 * W, 2 * D_MODEL, _ep_sigmoid, BF16)
    o_hgrn = _hgrn(hq, logf, hi, hog, hgrn_norm_g)
    o_moba = _moba(mqt, mk, mvt)
    return o_hgrn, o_moba, gates


def _moe_plan(info, cnt, T):
    R = MOE_ROWS
    eid = info[:, 0:TOPK_IN_GROUP].astype(I32)
    rank = info[:, 4:4 + TOPK_IN_GROUP].astype(I32)
    counts = cnt[0, :N_EXPERTS].astype(I32)
    padded = (counts + R - 1) // R * R
    pad_end = jnp.cumsum(padded)
    pad_start = pad_end - padded
    dest = (pad_start[eid] + rank).reshape(-1)
    n_blocks = (T * TOPK_IN_GROUP) // R + N_EXPERTS
    block_expert = jnp.minimum(
        jnp.searchsorted(pad_end, jnp.arange(n_blocks, dtype=I32) * R, side="right"), N_EXPERTS - 1).astype(I32)
    n_used = (pad_end[-1:] // R).astype(I32)
    slot_token = _slot_token(dest, (n_blocks + 1) * R)
    return dest, slot_token, block_expert, n_used


def kernel(x, p, norm_mix_g, w_in, hgrn_lb_raw, hgrn_norm_g, w_up_hgrn, w_up_moba, w_out, norm_ffn_g,
           w_router_group, w_router_expert, w_exp_gate, w_exp_up, w_exp_down, norm_ple_g, w_ple_gate,
           w_ple_proj, norm_final_g):
    B, T, D = x.shape
    assert B == 1 and D == D_MODEL and w_in.shape[0] == 1 and T % (4 * MOBA_BLOCK) == 0
    lower_bounds = jnp.cumsum(jax.nn.softmax(hgrn_lb_raw.astype(F32), axis=0), axis=0)
    x2d = x.reshape(T, D)
    o_hgrn, o_moba, gates = _mixers(x2d, norm_mix_g[0], w_in[0], lower_bounds[0], hgrn_norm_g[0])
    merged = _merge(o_hgrn, o_moba, gates, w_up_hgrn[0].astype(BF16), w_up_moba[0].astype(BF16))
    x1, h2 = _outproj(merged, x2d, w_out[0].astype(BF16), norm_ffn_g[0])
    w_router = jnp.pad(jnp.concatenate([w_router_group[0], w_router_expert[0]], axis=1),
                       ((0, 0), (0, V7X_LANES - N_GROUPS - N_EXPERTS))).astype(BF16)
    info, cnt = _router(h2, w_router)
    dest, slot_token, block_expert, n_used = _moe_plan(info, cnt, T)
    yb = _experts(block_expert, n_used, slot_token, h2, w_exp_gate[0], w_exp_up[0], w_exp_down[0])
    out = _combine(dest, x1, info, yb, p[0].reshape(T, PLE_DIM), norm_ple_g[0], w_ple_gate[0].astype(BF16),
                   w_ple_proj[0].astype(BF16), norm_final_g)
    return out.reshape(B, T, D)
```

```python
import functools

import jax
import jax.numpy as jnp
from jax import lax
from jax.experimental import pallas as pl
from jax.experimental.pallas import tpu as pltpu

F32 = jnp.float32
BF16 = jnp.bfloat16
I32 = jnp.int32

D_MODEL = 2048
PLE_DIM = 256
HGRN_HEADS = 8
HGRN_HEAD_DIM = 128
HGRN_WIDTH = HGRN_HEADS * HGRN_HEAD_DIM
MOBA_HEADS = 8
MOBA_HEAD_DIM = 128
MOBA_WIDTH = MOBA_HEADS * MOBA_HEAD_DIM
MOBA_BLOCK = 256
MOBA_TOPK = 3
ROPE_THETA = 10000.0
N_GROUPS = 4
EXPERTS_PER_GROUP = 8
N_EXPERTS = N_GROUPS * EXPERTS_PER_GROUP
TOPK_IN_GROUP = 2
D_EXPERT = 512
EPS = 1e-6
NEG_INF = -1e30

V7X_LANES = 128
V7X_SUBLANES = 8
V7X_VMEM_BUDGET_BYTES = 56 * 1024 * 1024

HGRN_CHUNK = 128
MOE_ROWS = 256


def _cparams(n_grid, vmem_bytes):
    return pltpu.CompilerParams(
        dimension_semantics=("arbitrary",) * n_grid,
        vmem_limit_bytes=int(min(max(vmem_bytes, 16 * 1024 * 1024), V7X_VMEM_BUDGET_BYTES)),
    )


def _nbytes(shape, dtype):
    n = 1
    for s in shape:
        n *= s
    return n * jnp.dtype(dtype).itemsize


def _rms(x, g):
    ms = jnp.mean(x * x, axis=-1, keepdims=True)
    return x * lax.rsqrt(ms + EPS) * g


def _rmsnorm_kernel(x_ref, g_ref, o_ref):
    o_ref[...] = _rms(x_ref[...], g_ref[...]).astype(o_ref.dtype)


def _rmsnorm(x, g, out_dtype, tm=512):
    T, D = x.shape
    return pl.pallas_call(
        _rmsnorm_kernel,
        out_shape=jax.ShapeDtypeStruct((T, D), out_dtype),
        grid=(T // tm,),
        in_specs=[pl.BlockSpec((tm, D), lambda i: (i, 0)), pl.BlockSpec((1, D), lambda i: (0, 0))],
        out_specs=pl.BlockSpec((tm, D), lambda i: (i, 0)),
        compiler_params=_cparams(1, 4 * _nbytes((tm, D), F32)),
        name="rmsnorm",
    )(x, g.reshape(1, D))


def _ep_identity(acc):
    return acc


def _ep_silu(acc):
    return acc * jax.nn.sigmoid(acc)


def _ep_sigmoid(acc):
    return jax.nn.sigmoid(acc)


def _ep_logf(acc, la_ref, lc_ref):
    ls = jnp.minimum(acc, 0.0) - jnp.log(1.0 + jnp.exp(-jnp.abs(acc)))
    u = la_ref[...]
    v = lc_ref[...] + ls
    return jnp.maximum(u, v) + jnp.log(1.0 + jnp.exp(-jnp.abs(u - v)))


def _ep_rope(acc, cos_ref, sin_ref, *, scale):
    cos = cos_ref[...]
    sin = sin_ref[...]
    outs = []
    for hh in range(acc.shape[1] // MOBA_HEAD_DIM):
        a = acc[:, hh * MOBA_HEAD_DIM:(hh + 1) * MOBA_HEAD_DIM]
        r = pltpu.roll(a, MOBA_HEAD_DIM // 2, axis=1)
        outs.append((a * cos + r * sin) * scale)
    return jnp.concatenate(outs, axis=1)


def _ep_rope_aug(acc, cos_ref, sin_ref, oh_ref):
    cos = cos_ref[...]
    sin = sin_ref[...]
    oh = oh_ref[...]
    outs = []
    for hh in range(acc.shape[1] // MOBA_HEAD_DIM):
        a = acc[:, hh * MOBA_HEAD_DIM:(hh + 1) * MOBA_HEAD_DIM]
        outs.append(a * cos + pltpu.roll(a, MOBA_HEAD_DIM // 2, axis=1) * sin)
        outs.append(oh)
    return jnp.concatenate(outs, axis=1)


def _cast_weight_once(w_ref, wb_ref, row_axis):
    @pl.when(pl.program_id(row_axis) == 0)
    def _():
        wb_ref[...] = w_ref[...].astype(wb_ref.dtype)


def _proj_kernel(h_ref, w_ref, *refs, epilogue):
    *extra, o_ref, wb_ref = refs
    _cast_weight_once(w_ref, wb_ref, 1)
    acc = jnp.dot(h_ref[...], wb_ref[...], preferred_element_type=F32)
    o_ref[...] = epilogue(acc, *extra).astype(o_ref.dtype)


def _proj(h, w, col0, ncols, epilogue, out_dtype, row_extras=(), col_extras=(), tm=1024, tn=1024, widen=1):
    T, K = h.shape
    tn = min(tn, ncols)
    tm = min(tm, T)
    cb = col0 // tn
    otn = widen * tn
    in_specs = [
        pl.BlockSpec((tm, K), lambda j, i: (i, 0)),
        pl.BlockSpec((K, tn), lambda j, i: (0, cb + j)),
    ]
    for e in row_extras:
        in_specs.append(pl.BlockSpec((tm, e.shape[1]), lambda j, i: (i, 0)))
    for e in col_extras:
        in_specs.append(pl.BlockSpec((1, tn), lambda j, i: (0, j)))
    vmem = 2 * (_nbytes((tm, K), h.dtype) + _nbytes((K, tn), w.dtype) + _nbytes((tm, otn), out_dtype))
    vmem += 3 * _nbytes((tm, otn), F32) + _nbytes((K, tn), BF16)
    return pl.pallas_call(
        functools.partial(_proj_kernel, epilogue=epilogue),
        out_shape=jax.ShapeDtypeStruct((T, widen * ncols), out_dtype),
        grid=(ncols // tn, T // tm),
        in_specs=in_specs,
        out_specs=pl.BlockSpec((tm, otn), lambda j, i: (i, j)),
        scratch_shapes=[pltpu.VMEM((K, tn), BF16)],
        compiler_params=_cparams(2, vmem),
        name="proj",
    )(h, w, *row_extras, *col_extras)


MOBA_VT_ROWS = MOBA_HEAD_DIM + 16
MOBA_BLOCKS_PER_STEP = 4


def _proj_t_kernel(h_ref, w_ref, *refs, epilogue, ones_rows):
    *extra, o_ref, wb_ref = refs
    BS, HD = MOBA_BLOCK, MOBA_HEAD_DIM
    _cast_weight_once(w_ref, wb_ref, 0)
    acc = epilogue(jnp.dot(h_ref[...], wb_ref[...], preferred_element_type=F32), *extra)
    ones = jnp.ones((ones_rows, BS), F32) if ones_rows else None
    for b in range(acc.shape[0] // BS):
        parts = []
        for hh in range(acc.shape[1] // HD):
            parts.append(acc[b * BS:(b + 1) * BS, hh * HD:(hh + 1) * HD].T)
            if ones_rows:
                parts.append(ones)
        o_ref[b] = jnp.concatenate(parts, axis=0).astype(o_ref.dtype)


def _proj_t(h, w, col0, epilogue, row_extras=(), ones_rows=0, tm=1024):
    T, K = h.shape
    tn = MOBA_WIDTH
    tm = min(tm, T)
    cb = col0 // tn
    rows = MOBA_HEADS * (MOBA_HEAD_DIM + ones_rows)
    in_specs = [pl.BlockSpec((tm, K), lambda i: (i, 0)), pl.BlockSpec((K, tn), lambda i: (0, cb))]
    for e in row_extras:
        in_specs.append(pl.BlockSpec((tm, e.shape[1]), lambda i: (i, 0)))
    vmem = 2 * (_nbytes((tm, K), h.dtype) + _nbytes((K, tn), w.dtype) + _nbytes((tm, 2 * tn), BF16))
    vmem += 4 * _nbytes((tm, tn), F32) + _nbytes((K, tn), BF16)
    return pl.pallas_call(
        functools.partial(_proj_t_kernel, epilogue=epilogue, ones_rows=ones_rows),
        out_shape=jax.ShapeDtypeStruct((T // MOBA_BLOCK, rows, MOBA_BLOCK), BF16),
        grid=(T // tm,),
        in_specs=in_specs,
        out_specs=pl.BlockSpec((tm // MOBA_BLOCK, rows, MOBA_BLOCK), lambda i: (i, 0, 0)),
        scratch_shapes=[pltpu.VMEM((K, tn), BF16)],
        compiler_params=_cparams(1, vmem),
        name="proj_t",
    )(h, w, *row_extras)


def _hgrn_kernel(q_ref, g_ref, v_ref, og_ref, ng_ref, o_ref, st_ref):
    W, HD, NH, C, S = HGRN_WIDTH, HGRN_HEAD_DIM, HGRN_HEADS, HGRN_CHUNK, V7X_SUBLANES
    J = C // S

    @pl.when(pl.program_id(0) == 0)
    def _():
        st_ref[...] = jnp.zeros_like(st_ref)

    def r3(x):
        return x.astype(F32).reshape(J, S, W)

    def sub_bcast(x3, r):
        return jnp.broadcast_to(x3[:, r:r + 1, :], x3.shape)

    g3, q3, v3 = r3(g_ref[...]), r3(q_ref[...]), r3(v_ref[...])
    sub = lax.broadcasted_iota(I32, (1, S, W), 1)

    c3 = g3
    for s in (1, 2, 4):
        c3 = c3 + jnp.where(sub >= s, pltpu.roll(c3, s, axis=1), 0.0)
    run = jnp.zeros((1, 1, W), F32)
    carry = []
    for j in range(J):
        carry.append(run)
        run = run + c3[j:j + 1, S - 1:S, :]
    b3 = c3 + jnp.concatenate(carry, axis=0)
    bC = run

    k3 = 1.0 - jnp.exp(g3)
    qe3 = q3 * jnp.exp(b3)
    ks3 = k3 * jnp.exp(bC - b3)

    levels = [(0, q3, k3)]
    ref1 = jnp.where(sub % 2 == 0, b3, pltpu.roll(b3, 1, axis=1))
    ref2 = jnp.where(sub < 4, sub_bcast(b3, 1), sub_bcast(b3, 5))
    ref4 = sub_bcast(b3, 3)
    for lvl, (ref, upper) in enumerate(((ref1, sub % 2 == 1), (ref2, sub % 4 >= 2), (ref4, sub >= 4)), start=1):
        e = jnp.exp(-jnp.abs(b3 - ref))
        levels.append((lvl, jnp.where(upper, q3 * e, 0.0), jnp.where(upper, 0.0, k3 * e)))
    zero_group = jnp.zeros((1, S, W), F32)
    for lvl, half in enumerate((1, 2, 4, 8), start=4):
        qparts, kparts = [], []
        for j in range(J):
            jr = (j // (2 * half)) * (2 * half) + half - 1
            ref = b3[jr:jr + 1, S - 1:S, :]
            if (j % (2 * half)) >= half:
                qparts.append(q3[j:j + 1] * jnp.exp(b3[j:j + 1] - ref))
                kparts.append(zero_group)
            else:
                qparts.append(zero_group)
                kparts.append(k3[j:j + 1] * jnp.exp(ref - b3[j:j + 1]))
        levels.append((lvl, jnp.concatenate(qparts, axis=0), jnp.concatenate(kparts, axis=0)))

    tr = lax.broadcasted_iota(I32, (C, C), 0)
    tc = lax.broadcasted_iota(I32, (C, C), 1)
    xr = tr ^ tc
    code = jnp.zeros((C, C), I32)
    for lvl in range(1, 8):
        code = jnp.where(xr >= (1 << (lvl - 1)), lvl, code)
    code = jnp.where(tc > tr, -1, code)

    def mat(x3, h):
        return x3.reshape(C, W)[:, h * HD:(h + 1) * HD].astype(BF16)

    nt = (((1,), (1,)), ((), ()))
    tn = (((0,), (0,)), ((), ()))
    ebc = jnp.exp(bC).reshape(1, W)
    ng = ng_ref[...]
    for h in range(NH):
        a_mat = jnp.zeros((C, C), F32)
        for lvl, qr, kr in levels:
            s = lax.dot_general(mat(qr, h), mat(kr, h), nt, preferred_element_type=F32)
            a_mat = jnp.where(code == lvl, s, a_mat)
        vh = mat(v3, h)
        st = st_ref[h]
        o = jnp.dot(a_mat.astype(BF16), vh, preferred_element_type=F32)
        o = o + lax.dot_general(mat(qe3, h), st.astype(BF16), nt, preferred_element_type=F32)
        o = _rms(o, ng) * og_ref[:, h * HD:(h + 1) * HD].astype(F32)
        o_ref[:, h * HD:(h + 1) * HD] = o.astype(o_ref.dtype)
        st_ref[h] = st * ebc[:, h * HD:(h + 1) * HD] + lax.dot_general(
            vh, mat(ks3, h), tn, preferred_element_type=F32)


def _hgrn(q, logf, v, og, norm_g):
    T, W = q.shape
    C = HGRN_CHUNK
    blk = pl.BlockSpec((C, W), lambda c: (c, 0))
    vmem = 64 * _nbytes((C, W), F32)
    return pl.pallas_call(
        _hgrn_kernel,
        out_shape=jax.ShapeDtypeStruct((T, W), BF16),
        grid=(T // C,),
        in_specs=[blk, blk, blk, blk, pl.BlockSpec((1, HGRN_HEAD_DIM), lambda c: (0, 0))],
        out_specs=blk,
        scratch_shapes=[pltpu.VMEM((HGRN_HEADS, HGRN_HEAD_DIM, HGRN_HEAD_DIM), F32)],
        compiler_params=_cparams(1, vmem),
        name="hgrn",
    )(q, logf, v, og, norm_g.reshape(1, HGRN_HEAD_DIM))


def _moba_kernel(qt_ref, k_ref, vt_ref, o_ref, km_ref):
    BS, HD, VR = MOBA_BLOCK, MOBA_HEAD_DIM, MOBA_VT_ROWS
    T = k_ref.shape[0]
    NB = T // BS
    G = o_ref.shape[1] // HD
    cur = pl.program_id(1)

    @pl.when(cur == 0)
    def _():
        for g in range(G):
            kf = k_ref[:, 2 * g * HD:(2 * g + 1) * HD].astype(F32).reshape(NB, BS, HD)
            km_ref[g] = jnp.sum(kf, axis=1) * (1.0 / BS)

    blk = lax.broadcasted_iota(I32, (NB, BS), 0)
    key = lax.broadcasted_iota(I32, (BS, BS), 0)
    qry = lax.broadcasted_iota(I32, (BS, BS), 1)
    pad = jnp.zeros((V7X_LANES - NB, BS), F32)
    r0 = pl.multiple_of(cur * BS, BS)
    qts = [qt_ref[0, g * HD:(g + 1) * HD, :] for g in range(G)]
    gts = [jnp.dot(km_ref[g].astype(BF16), qts[g], preferred_element_type=F32) for g in range(G)]
    s_own = [jnp.dot(k_ref[pl.ds(r0, BS), 2 * g * HD:(2 * g + 1) * HD], qts[g], preferred_element_type=F32)
             for g in range(G)]
    qcs = []
    for g in range(G):
        gt = jnp.where(blk < cur, gts[g], NEG_INF)
        sel = jnp.zeros((NB, BS), jnp.bool_)
        for _ in range(MOBA_TOPK):
            mx = jnp.max(gt, axis=0, keepdims=True)
            idx = jnp.min(jnp.where(gt == mx, blk, NB), axis=0, keepdims=True)
            pick = (blk == idx) & (mx > 0.5 * NEG_INF)
            sel = sel | pick
            gt = jnp.where(pick, NEG_INF, gt)
        pen = jnp.concatenate([jnp.where(sel, 0.0, NEG_INF), pad], axis=0).astype(BF16)
        qcs.append(jnp.concatenate([qts[g], pen], axis=0))

    def scores(n):
        rn = pl.multiple_of(n * BS, BS)
        return tuple(jnp.dot(k_ref[pl.ds(rn, BS), 2 * g * HD:(2 * g + 2) * HD], qcs[g], preferred_element_type=F32)
                     for g in range(G))

    ms, ps = [], []
    for g in range(G):
        s = jnp.where(key <= qry, s_own[g], NEG_INF)
        m0 = jnp.max(s, axis=0, keepdims=True)
        ms.append(m0)
        ps.append(jnp.exp(s - m0).astype(BF16))
    accs = [jnp.dot(vt_ref[cur, g * VR:(g + 1) * VR, :], ps[g], preferred_element_type=F32)
            for g in range(G)]

    KB = MOBA_BLOCKS_PER_STEP

    def body(c, carry):
        ms, accs = carry
        r = pl.multiple_of(c * KB * BS, KB * BS)
        sns = [jnp.dot(k_ref[pl.ds(r, KB * BS), 2 * g * HD:(2 * g + 2) * HD], qcs[g], preferred_element_type=F32)
               for g in range(G)]
        new_ms, alphas, pns = [], [], []
        for g in range(G):
            m_new = jnp.maximum(ms[g], jnp.max(sns[g], axis=0, keepdims=True))
            alphas.append(jnp.exp(ms[g] - m_new))
            pns.append(jnp.exp(sns[g] - m_new).astype(BF16))
            new_ms.append(m_new)
        new_accs = []
        for g in range(G):
            pv = alphas[g] * accs[g]
            for j in range(KB):
                pv = pv + jnp.dot(vt_ref[KB * c + j, g * VR:(g + 1) * VR, :], pns[g][j * BS:(j + 1) * BS],
                                  preferred_element_type=F32)
            new_accs.append(pv)
        return tuple(new_ms), tuple(new_accs)

    _, accs = lax.fori_loop(0, (cur + KB - 1) // KB, body, (tuple(ms), tuple(accs)))
    for g in range(G):
        ot = accs[g][:HD, :] / accs[g][HD:HD + 1, :]
        o_ref[:, g * HD:(g + 1) * HD] = ot.T.astype(o_ref.dtype)


def _moba(mqt, mk_aug, mvt, heads_per_step=4):
    T = mk_aug.shape[0]
    BS, HD, G, VR = MOBA_BLOCK, MOBA_HEAD_DIM, heads_per_step, MOBA_VT_ROWS
    NB = T // BS
    vmem = _nbytes((T, 2 * G * HD), BF16) + _nbytes((NB, G * VR, BS), BF16) + 8 * _nbytes((BS, G * HD), BF16)
    vmem += 16 * G * _nbytes((BS, BS), F32)
    resident = pl.Buffered(1)
    return pl.pallas_call(
        _moba_kernel,
        out_shape=jax.ShapeDtypeStruct((T, MOBA_WIDTH), BF16),
        grid=(MOBA_HEADS // G, NB),
        in_specs=[
            pl.BlockSpec((1, G * HD, BS), lambda h, i: (i, h, 0)),
            pl.BlockSpec((T, 2 * G * HD), lambda h, i: (0, h), pipeline_mode=resident),
            pl.BlockSpec((NB, G * VR, BS), lambda h, i: (0, h, 0), pipeline_mode=resident),
        ],
        out_specs=pl.BlockSpec((BS, G * HD), lambda h, i: (i, h)),
        scratch_shapes=[pltpu.VMEM((G, NB, HD), F32)],
        compiler_params=_cparams(2, vmem),
        name="moba",
    )(mqt, mk_aug, mvt)


def _merge_kernel(oh_ref, om_ref, ga_ref, gb_ref, wh_ref, wm_ref, o_ref):
    a = jnp.dot(oh_ref[...], wh_ref[...], preferred_element_type=F32)
    b = jnp.dot(om_ref[...], wm_ref[...], preferred_element_type=F32)
    o_ref[...] = (ga_ref[...].astype(F32) * a + gb_ref[...].astype(F32) * b).astype(o_ref.dtype)


def _merge(o_hgrn, o_moba, gates, w_up_hgrn, w_up_moba, tm=512):
    T = o_hgrn.shape[0]
    D = D_MODEL
    vmem = 2 * (_nbytes((tm, HGRN_WIDTH), F32) + _nbytes((tm, MOBA_WIDTH), BF16) + 3 * _nbytes((tm, D), BF16)
                + 2 * _nbytes((HGRN_WIDTH, D), BF16)) + 3 * _nbytes((tm, D), F32)
    return pl.pallas_call(
        _merge_kernel,
        out_shape=jax.ShapeDtypeStruct((T, D), BF16),
        grid=(T // tm,),
        in_specs=[
            pl.BlockSpec((tm, HGRN_WIDTH), lambda i: (i, 0)),
            pl.BlockSpec((tm, MOBA_WIDTH), lambda i: (i, 0)),
            pl.BlockSpec((tm, D), lambda i: (i, 0)),
            pl.BlockSpec((tm, D), lambda i: (i, 1)),
            pl.BlockSpec((HGRN_WIDTH, D), lambda i: (0, 0)),
            pl.BlockSpec((MOBA_WIDTH, D), lambda i: (0, 0)),
        ],
        out_specs=pl.BlockSpec((tm, D), lambda i: (i, 0)),
        compiler_params=_cparams(1, vmem),
        name="merge",
    )(o_hgrn, o_moba, gates, gates, w_up_hgrn, w_up_moba)


def _outproj_kernel(m_ref, x_ref, w_ref, g_ref, x1_ref, h2_ref):
    x1 = x_ref[...] + jnp.dot(m_ref[...], w_ref[...], preferred_element_type=F32)
    x1_ref[...] = x1
    h2_ref[...] = _rms(x1, g_ref[...])


def _outproj(merged, x, w_out, g_ffn, tm=256):
    T, D = x.shape
    vmem = 2 * (_nbytes((tm, D), BF16) + 3 * _nbytes((tm, D), F32) + _nbytes((D, D), BF16)) + 2 * _nbytes((tm, D), F32)
    return pl.pallas_call(
        _outproj_kernel,
        out_shape=(jax.ShapeDtypeStruct((T, D), F32), jax.ShapeDtypeStruct((T, D), F32)),
        grid=(T // tm,),
        in_specs=[
            pl.BlockSpec((tm, D), lambda i: (i, 0)),
            pl.BlockSpec((tm, D), lambda i: (i, 0)),
            pl.BlockSpec((D, D), lambda i: (0, 0)),
            pl.BlockSpec((1, D), lambda i: (0, 0)),
        ],
        out_specs=(pl.BlockSpec((tm, D), lambda i: (i, 0)), pl.BlockSpec((tm, D), lambda i: (i, 0))),
        compiler_params=_cparams(1, vmem),
        name="outproj",
    )(merged, x, w_out, g_ffn.reshape(1, D))


def _router_kernel(h_ref, w_ref, info_ref, cnt_ref, carry_ref):
    tm = h_ref.shape[0]

    @pl.when(pl.program_id(0) == 0)
    def _():
        carry_ref[...] = jnp.zeros_like(carry_ref)

    logits = jnp.dot(h_ref[...].astype(BF16), w_ref[...], preferred_element_type=F32)
    lane = lax.broadcasted_iota(I32, (tm, V7X_LANES), 1)
    is_g = lane < N_GROUPS
    gl = jnp.where(is_g, logits, NEG_INF)
    gmax = jnp.max(gl, axis=1, keepdims=True)
    g_sel = jnp.min(jnp.where(gl == gmax, lane, V7X_LANES), axis=1, keepdims=True)
    gsum = jnp.sum(jnp.where(is_g, jnp.exp(gl - gmax), 0.0), axis=1, keepdims=True)
    p_group = 1.0 / gsum
    lo = N_GROUPS + EXPERTS_PER_GROUP * g_sel
    emask = (lane >= lo) & (lane < lo + EXPERTS_PER_GROUP)
    el = jnp.where(emask, logits, NEG_INF)
    e1 = jnp.max(el, axis=1, keepdims=True)
    i1 = jnp.min(jnp.where((el == e1) & emask, lane, V7X_LANES), axis=1, keepdims=True)
    emask2 = emask & (lane != i1)
    el2 = jnp.where(emask2, logits, NEG_INF)
    e2 = jnp.max(el2, axis=1, keepdims=True)
    i2 = jnp.min(jnp.where((el2 == e2) & emask2, lane, V7X_LANES), axis=1, keepdims=True)
    r = jnp.exp(e2 - e1)
    w1 = p_group / (1.0 + r)
    w2 = p_group * r / (1.0 + r)
    eid1 = i1 - N_GROUPS
    eid2 = i2 - N_GROUPS
    oh1 = jnp.where(lane == eid1, 1.0, 0.0)
    oh2 = jnp.where(lane == eid2, 1.0, 0.0)
    cnt = oh1 + oh2
    tri = jnp.where(lax.broadcasted_iota(I32, (tm, tm), 0) > lax.broadcasted_iota(I32, (tm, tm), 1), 1.0, 0.0)
    before = jnp.dot(tri.astype(BF16), cnt.astype(BF16), preferred_element_type=F32) + carry_ref[...]
    rank1 = jnp.sum(oh1 * before, axis=1, keepdims=True)
    rank2 = jnp.sum(oh2 * before, axis=1, keepdims=True)
    carry_ref[...] = carry_ref[...] + jnp.sum(cnt, axis=0, keepdims=True)
    info = jnp.zeros((tm, V7X_LANES), F32)
    for k, val in enumerate((eid1.astype(F32), eid2.astype(F32), w1, w2, rank1, rank2)):
        info = jnp.where(lane == k, val, info)
    info_ref[...] = info
    cnt_ref[...] = carry_ref[...]


def _router(h2, w_router, tm=256):
    T, D = h2.shape
    vmem = 2 * (_nbytes((tm, D), F32) + _nbytes((D, V7X_LANES), BF16)) + 16 * _nbytes((tm, V7X_LANES), F32) + (1 << 22)
    return pl.pallas_call(
        _router_kernel,
        out_shape=(jax.ShapeDtypeStruct((T, V7X_LANES), F32), jax.ShapeDtypeStruct((1, V7X_LANES), F32)),
        grid=(T // tm,),
        in_specs=[pl.BlockSpec((tm, D), lambda i: (i, 0)), pl.BlockSpec((D, V7X_LANES), lambda i: (0, 0))],
        out_specs=(pl.BlockSpec((tm, V7X_LANES), lambda i: (i, 0)), pl.BlockSpec((1, V7X_LANES), lambda i: (0, 0))),
        scratch_shapes=[pltpu.VMEM((1, V7X_LANES), F32)],
        compiler_params=_cparams(1, vmem),
        name="router",
    )(h2, w_router)


def _row_copy(src_ref, src_row, dst_ref, dst_row, sem):
    return pltpu.make_async_copy(src_ref.at[pl.ds(src_row, 1), :], dst_ref.at[pl.ds(dst_row, 1), :], sem)


ROW_DMA_PRIORITY = 1


def _slot_token_kernel(dest_ref, o_ref):
    def clear(i, c):
        o_ref[i] = 0
        return c

    def put(t, c):
        for k in range(TOPK_IN_GROUP):
            o_ref[dest_ref[t * TOPK_IN_GROUP + k]] = t
        return c

    lax.fori_loop(0, o_ref.shape[0], clear, 0, unroll=8)
    lax.fori_loop(0, dest_ref.shape[0] // TOPK_IN_GROUP, put, 0, unroll=8)


def _slot_token(dest_flat, n_slots):
    return pl.pallas_call(
        _slot_token_kernel,
        out_shape=jax.ShapeDtypeStruct((n_slots,), I32),
        in_specs=[pl.BlockSpec(memory_space=pltpu.SMEM)],
        out_specs=pl.BlockSpec(memory_space=pltpu.SMEM),
        name="slot_token",
    )(dest_flat)


def _expert_kernel(be_ref, nu_ref, tok_ref, h_ref, wg_ref, wu_ref, wd_ref, y_ref, xbuf, wgb_ref, wub_ref, wdb_ref,
                   sem):
    R = MOE_ROWS
    step = pl.program_id(0)
    b = step - 1
    n_used = nu_ref[0]
    used = (b >= 0) & (b < n_used)

    PARTS, UNROLL = 4, 8

    def fetch_part(blk, part):
        slot = blk % 2
        rows = R // PARTS

        def issue(j, c):
            for u in range(UNROLL):
                r = part * rows + j * UNROLL + u
                _row_copy(h_ref, tok_ref[blk * R + r], xbuf.at[slot], r, sem.at[slot]).start(
                    priority=ROW_DMA_PRIORITY)
            return c

        lax.fori_loop(0, rows // UNROLL, issue, 0)

    def fetch(blk):
        for part in range(PARTS):
            fetch_part(blk, part)

    def drain(blk):
        slot = blk % 2
        for _ in range(R):
            _row_copy(h_ref, 0, xbuf.at[0], 0, sem.at[slot]).wait()

    @pl.when(step == 0)
    def _():
        fetch(step)

    @pl.when((b >= 0) & (b <= n_used))
    def _():
        drain(b)

    @pl.when(used & ((b == 0) | (be_ref[jnp.maximum(b, 0)] != be_ref[jnp.maximum(b - 1, 0)])))
    def _():
        wgb_ref[...] = wg_ref[0].astype(BF16)
        wub_ref[...] = wu_ref[0].astype(BF16)
        wdb_ref[...] = wd_ref[0].astype(BF16)

    @pl.when(used)
    def _():
        half = y_ref.shape[1] // 2
        x = xbuf[b % 2].astype(BF16)
        fetch_part(step, 0)
        a = jnp.dot(x, wgb_ref[...], preferred_element_type=F32)
        fetch_part(step, 1)
        u = jnp.dot(x, wub_ref[...], preferred_element_type=F32)
        hm = (a * jax.nn.sigmoid(a) * u).astype(BF16)
        fetch_part(step, 2)
        y_ref[:, :half] = jnp.dot(hm, wdb_ref[:, :half], preferred_element_type=F32)
        fetch_part(step, 3)
        y_ref[:, half:] = jnp.dot(hm, wdb_ref[:, half:], preferred_element_type=F32)

    @pl.when((b >= n_used))
    def _():
        y_ref[...] = jnp.zeros_like(y_ref)


def _experts(block_expert, n_used, slot_token, h2, w_gate, w_up, w_down):
    T, D = h2.shape
    R, Fd = MOE_ROWS, D_EXPERT
    m_pad = slot_token.shape[0] - R
    n_blocks = m_pad // R
    vmem = 2 * (_nbytes((R, D), F32) + 3 * _nbytes((D, Fd), F32)) + 3 * _nbytes((D, Fd), BF16)
    vmem += 6 * _nbytes((R, D), F32)
    blk = lambda i, be, nu, tok: jnp.clip(i - 1, 0, n_blocks - 1)
    return pl.pallas_call(
        _expert_kernel,
        out_shape=jax.ShapeDtypeStruct((m_pad, D), F32),
        grid_spec=pltpu.PrefetchScalarGridSpec(
            num_scalar_prefetch=3,
            grid=(n_blocks + 1,),
            in_specs=[
                pl.BlockSpec(memory_space=pl.ANY),
                pl.BlockSpec((1, D, Fd), lambda i, be, nu, tok: (be[blk(i, be, nu, tok)], 0, 0)),
                pl.BlockSpec((1, D, Fd), lambda i, be, nu, tok: (be[blk(i, be, nu, tok)], 0, 0)),
                pl.BlockSpec((1, Fd, D), lambda i, be, nu, tok: (be[blk(i, be, nu, tok)], 0, 0)),
            ],
            out_specs=pl.BlockSpec((R, D), lambda i, be, nu, tok: (blk(i, be, nu, tok), 0)),
            scratch_shapes=[pltpu.VMEM((2, R, D), F32), pltpu.VMEM((D, Fd), BF16), pltpu.VMEM((D, Fd), BF16),
                            pltpu.VMEM((Fd, D), BF16), pltpu.SemaphoreType.DMA((2,))],
        ),
        compiler_params=_cparams(1, vmem),
        name="experts",
    )(block_expert, n_used, slot_token, h2, w_gate, w_up, w_down)


def _combine_kernel(dest_ref, x1_ref, info_ref, yb_ref, p_ref, gp_ref, wpg_ref, wpp_ref, gf_ref, o_ref, ybuf, sem):
    tm = x1_ref.shape[0]
    step = pl.program_id(0)
    n_tiles = pl.num_programs(0) - 2
    K = TOPK_IN_GROUP

    def fetch(tile):
        slot = tile % 2
        for r in range(tm):
            for k in range(K):
                _row_copy(yb_ref, dest_ref[(tile * tm + r) * K + k], ybuf.at[slot * K + k], r,
                          sem.at[slot]).start(priority=ROW_DMA_PRIORITY)

    def drain(tile):
        slot = tile % 2
        for _ in range(tm * K):
            _row_copy(yb_ref, 0, ybuf.at[0], 0, sem.at[slot]).wait()

    @pl.when(step == 0)
    def _():
        fetch(step)

    @pl.when(step > 0)
    def _():
        drain(step - 1)

    @pl.when((step > 0) & (step <= n_tiles))
    def _():
        tile = step - 1
        fetch(step)
        slot = tile % 2
        info = info_ref[...]
        x2 = x1_ref[...] + info[:, 2:3] * ybuf[slot * K] + info[:, 3:4] * ybuf[slot * K + 1]
        hp = _rms(x2, gp_ref[...]).astype(BF16)
        z = jnp.dot(hp, wpg_ref[...], preferred_element_type=F32)
        pp = jnp.dot(p_ref[...].astype(BF16), wpp_ref[...], preferred_element_type=F32)
        x3 = x2 + jax.nn.sigmoid(z) * pp
        o_ref[...] = _rms(x3, gf_ref[...])


def _combine(dest_flat, x1, info, yb, p, g_ple, w_ple_gate, w_ple_proj, g_final, tm=256):
    T, D = x1.shape
    n_tiles = T // tm
    dest_padded = jnp.concatenate([dest_flat, jnp.zeros((tm * TOPK_IN_GROUP,), I32)])
    vmem = 2 * (2 * _nbytes((tm, D), F32) + _nbytes((D, D), BF16) + _nbytes((PLE_DIM, D), BF16)
                + _nbytes((tm, PLE_DIM), F32)) + 10 * _nbytes((tm, D), F32)
    tile = lambda i, d: (jnp.clip(i - 1, 0, n_tiles - 1), 0)
    return pl.pallas_call(
        _combine_kernel,
        out_shape=jax.ShapeDtypeStruct((T, D), F32),
        grid_spec=pltpu.PrefetchScalarGridSpec(
            num_scalar_prefetch=1,
            grid=(n_tiles + 2,),
            in_specs=[
                pl.BlockSpec((tm, D), tile),
                pl.BlockSpec((tm, V7X_LANES), tile),
                pl.BlockSpec(memory_space=pl.ANY),
                pl.BlockSpec((tm, PLE_DIM), tile),
                pl.BlockSpec((1, D), lambda i, d: (0, 0)),
                pl.BlockSpec((D, D), lambda i, d: (0, 0)),
                pl.BlockSpec((PLE_DIM, D), lambda i, d: (0, 0)),
                pl.BlockSpec((1, D), lambda i, d: (0, 0)),
            ],
            out_specs=pl.BlockSpec((tm, D), tile),
            scratch_shapes=[pltpu.VMEM((2 * TOPK_IN_GROUP, tm, D), F32), pltpu.SemaphoreType.DMA((2,))],
        ),
        compiler_params=_cparams(1, vmem),
        name="combine",
    )(dest_padded, x1, info, yb, p, g_ple.reshape(1, D), w_ple_gate, w_ple_proj, g_final.reshape(1, D))


def _rope_tables(T):
    half = MOBA_HEAD_DIM // 2
    inv_freq = ROPE_THETA ** (-jnp.arange(half, dtype=F32) / half)
    ang = jnp.arange(T, dtype=F32)[:, None] * inv_freq[None, :]
    cos, sin = jnp.cos(ang), jnp.sin(ang)
    return jnp.concatenate([cos, cos], axis=1), jnp.concatenate([-sin, sin], axis=1)


def _mixers(x2d, g_mix, w_in, lb, hgrn_norm_g):
    T = x2d.shape[0]
    W = HGRN_WIDTH
    h = _rmsnorm(x2d, g_mix, BF16)
    w = w_in
    cos, sin = _rope_tables(T)
    log_lb = jnp.log(lb).reshape(1, W)
    log_1m = jnp.log1p(-lb).reshape(1, W)
    hq = _proj(h, w, 0 * W, W, _ep_silu, BF16)
    logf = _proj(h, w, 1 * W, W, _ep_logf, F32, col_extras=(log_lb, log_1m))
    hi = _proj(h, w, 2 * W, W, _ep_identity, BF16)
    hog = _proj(h, w, 3 * W, W, _ep_silu, BF16)
    scale = MOBA_HEAD_DIM ** -0.5
    mqt = _proj_t(h, w, 4 * W, functools.partial(_ep_rope, scale=scale), row_extras=(cos, sin))
    blk = jnp.arange(T, dtype=I32)[:, None] // MOBA_BLOCK
    blk_onehot = (blk == jnp.arange(V7X_LANES, dtype=I32)[None, :]).astype(F32)
    mk = _proj(h, w, 5 * W, W, _ep_rope_aug, BF16, row_extras=(cos, sin, blk_onehot), widen=2)
    mvt = _proj_t(h, w, 6 * W, _ep_identity, ones_rows=MOBA_VT_ROWS - MOBA_HEAD_DIM)
    gates = _proj(h, w, 7 * W, 2 * D_MODEL, _ep_sigmoid, BF16)
    o_hgrn = _hgrn(hq, logf, hi, hog, hgrn_norm_g)
    o_moba = _moba(mqt, mk, mvt)
    return o_hgrn, o_moba, gates


def _moe_plan(info, cnt, T):
    R = MOE_ROWS
    eid = info[:, 0:TOPK_IN_GROUP].astype(I32)
    rank = info[:, 4:4 + TOPK_IN_GROUP].astype(I32)
    counts = cnt[0, :N_EXPERTS].astype(I32)
    padded = (counts + R - 1) // R * R
    pad_end = jnp.cumsum(padded)
    pad_start = pad_end - padded
    dest = (pad_start[eid] + rank).reshape(-1)
    n_blocks = (T * TOPK_IN_GROUP) // R + N_EXPERTS
    block_expert = jnp.minimum(
        jnp.searchsorted(pad_end, jnp.arange(n_blocks, dtype=I32) * R, side="right"), N_EXPERTS - 1).astype(I32)
    n_used = (pad_end[-1:] // R).astype(I32)
    slot_token = _slot_token(dest, (n_blocks + 1) * R)
    return dest, slot_token, block_expert, n_used


def kernel(x, p, norm_mix_g, w_in, hgrn_lb_raw, hgrn_norm_g, w_up_hgrn, w_up_moba, w_out, norm_ffn_g,
           w_router_group, w_router_expert, w_exp_gate, w_exp_up, w_exp_down, norm_ple_g, w_ple_gate,
           w_ple_proj, norm_final_g):
    B, T, D = x.shape
    assert B == 1 and D == D_MODEL and w_in.shape[0] == 1 and T % (4 * MOBA_BLOCK) == 0
    lower_bounds = jnp.cumsum(jax.nn.softmax(hgrn_lb_raw.astype(F32), axis=0), axis=0)
    x2d = x.reshape(T, D)
    o_hgrn, o_moba, gates = _mixers(x2d, norm_mix_g[0], w_in[0], lower_bounds[0], hgrn_norm_g[0])
    merged = _merge(o_hgrn, o_moba, gates, w_up_hgrn[0].astype(BF16), w_up_moba[0].astype(BF16))
    x1, h2 = _outproj(merged, x2d, w_out[0].astype(BF16), norm_ffn_g[0])
    w_router = jnp.pad(jnp.concatenate([w_router_group[0], w_router_expert[0]], axis=1),
                       ((0, 0), (0, V7X_LANES - N_GROUPS - N_EXPERTS))).astype(BF16)
    info, cnt = _router(h2, w_router)
    dest, slot_token, block_expert, n_used = _moe_plan(info, cnt, T)
    yb = _experts(block_expert, n_used, slot_token, h2, w_exp_gate[0], w_exp_up[0], w_exp_down[0])
    out = _combine(dest, x1, info, yb, p[0].reshape(T, PLE_DIM), norm_ple_g[0], w_ple_gate[0].astype(BF16),
                   w_ple_proj[0].astype(BF16), norm_final_g)
    return out.reshape(B, T, D)
```

```python
import functools

import jax
import jax.numpy as jnp
from jax import lax
from jax.experimental import pallas as pl
from jax.experimental.pallas import tpu as pltpu

F32 = jnp.float32
BF16 = jnp.bfloat16
I32 = jnp.int32

D_MODEL = 2048
PLE_DIM = 256
HGRN_HEADS = 8
HGRN_HEAD_DIM = 128
HGRN_WIDTH = HGRN_HEADS * HGRN_HEAD_DIM
MOBA_HEADS = 8
MOBA_HEAD_DIM = 128
MOBA_WIDTH = MOBA_HEADS * MOBA_HEAD_DIM
MOBA_BLOCK = 256
MOBA_TOPK = 3
ROPE_THETA = 10000.0
N_GROUPS = 4
EXPERTS_PER_GROUP = 8
N_EXPERTS = N_GROUPS * EXPERTS_PER_GROUP
TOPK_IN_GROUP = 2
D_EXPERT = 512
EPS = 1e-6
NEG_INF = -1e30

V7X_LANES = 128
V7X_SUBLANES = 8
V7X_VMEM_BUDGET_BYTES = 56 * 1024 * 1024

HGRN_CHUNK = 128
MOE_ROWS = 256


def _cparams(n_grid, vmem_bytes):
    return pltpu.CompilerParams(
        dimension_semantics=("arbitrary",) * n_grid,
        vmem_limit_bytes=int(min(max(vmem_bytes, 16 * 1024 * 1024), V7X_VMEM_BUDGET_BYTES)),
    )


def _nbytes(shape, dtype):
    n = 1
    for s in shape:
        n *= s
    return n * jnp.dtype(dtype).itemsize


def _rms(x, g):
    ms = jnp.mean(x * x, axis=-1, keepdims=True)
    return x * lax.rsqrt(ms + EPS) * g


def _rmsnorm_kernel(x_ref, g_ref, o_ref):
    o_ref[...] = _rms(x_ref[...], g_ref[...]).astype(o_ref.dtype)


def _rmsnorm(x, g, out_dtype, tm=512):
    T, D = x.shape
    return pl.pallas_call(
        _rmsnorm_kernel,
        out_shape=jax.ShapeDtypeStruct((T, D), out_dtype),
        grid=(T // tm,),
        in_specs=[pl.BlockSpec((tm, D), lambda i: (i, 0)), pl.BlockSpec((1, D), lambda i: (0, 0))],
        out_specs=pl.BlockSpec((tm, D), lambda i: (i, 0)),
        compiler_params=_cparams(1, 4 * _nbytes((tm, D), F32)),
        name="rmsnorm",
    )(x, g.reshape(1, D))


def _ep_identity(acc):
    return acc


def _ep_silu(acc):
    return acc * jax.nn.sigmoid(acc)


def _ep_sigmoid(acc):
    return jax.nn.sigmoid(acc)


def _ep_logf(acc, la_ref, lc_ref):
    ls = jnp.minimum(acc, 0.0) - jnp.log(1.0 + jnp.exp(-jnp.abs(acc)))
    u = la_ref[...]
    v = lc_ref[...] + ls
    return jnp.maximum(u, v) + jnp.log(1.0 + jnp.exp(-jnp.abs(u - v)))


def _ep_rope(acc, cos_ref, sin_ref, *, scale):
    cos = cos_ref[...]
    sin = sin_ref[...]
    outs = []
    for hh in range(acc.shape[1] // MOBA_HEAD_DIM):
        a = acc[:, hh * MOBA_HEAD_DIM:(hh + 1) * MOBA_HEAD_DIM]
        r = pltpu.roll(a, MOBA_HEAD_DIM // 2, axis=1)
        outs.append((a * cos + r * sin) * scale)
    return jnp.concatenate(outs, axis=1)


def _ep_rope_aug(acc, cos_ref, sin_ref, oh_ref):
    cos = cos_ref[...]
    sin = sin_ref[...]
    oh = oh_ref[...]
    outs = []
    for hh in range(acc.shape[1] // MOBA_HEAD_DIM):
        a = acc[:, hh * MOBA_HEAD_DIM:(hh + 1) * MOBA_HEAD_DIM]
        outs.append(a * cos + pltpu.roll(a, MOBA_HEAD_DIM // 2, axis=1) * sin)
        outs.append(oh)
    return jnp.concatenate(outs, axis=1)


def _cast_weight_once(w_ref, wb_ref, row_axis):
    @pl.when(pl.program_id(row_axis) == 0)
    def _():
        wb_ref[...] = w_ref[...].astype(wb_ref.dtype)


def _proj_kernel(h_ref, w_ref, *refs, epilogue):
    *extra, o_ref, wb_ref = refs
    _cast_weight_once(w_ref, wb_ref, 1)
    acc = jnp.dot(h_ref[...], wb_ref[...], preferred_element_type=F32)
    o_ref[...] = epilogue(acc, *extra).astype(o_ref.dtype)


def _proj(h, w, col0, ncols, epilogue, out_dtype, row_extras=(), col_extras=(), tm=1024, tn=1024, widen=1):
    T, K = h.shape
    tn = min(tn, ncols)
    tm = min(tm, T)
    cb = col0 // tn
    otn = widen * tn
    in_specs = [
        pl.BlockSpec((tm, K), lambda j, i: (i, 0)),
        pl.BlockSpec((K, tn), lambda j, i: (0, cb + j)),
    ]
    for e in row_extras:
        in_specs.append(pl.BlockSpec((tm, e.shape[1]), lambda j, i: (i, 0)))
    for e in col_extras:
        in_specs.append(pl.BlockSpec((1, tn), lambda j, i: (0, j)))
    vmem = 2 * (_nbytes((tm, K), h.dtype) + _nbytes((K, tn), w.dtype) + _nbytes((tm, otn), out_dtype))
    vmem += 3 * _nbytes((tm, otn), F32) + _nbytes((K, tn), BF16)
    return pl.pallas_call(
        functools.partial(_proj_kernel, epilogue=epilogue),
        out_shape=jax.ShapeDtypeStruct((T, widen * ncols), out_dtype),
        grid=(ncols // tn, T // tm),
        in_specs=in_specs,
        out_specs=pl.BlockSpec((tm, otn), lambda j, i: (i, j)),
        scratch_shapes=[pltpu.VMEM((K, tn), BF16)],
        compiler_params=_cparams(2, vmem),
        name="proj",
    )(h, w, *row_extras, *col_extras)


MOBA_VT_ROWS = MOBA_HEAD_DIM + 16
MOBA_BLOCKS_PER_STEP = 4


def _proj_t_kernel(h_ref, w_ref, *refs, epilogue, ones_rows):
    *extra, o_ref, wb_ref = refs
    BS, HD = MOBA_BLOCK, MOBA_HEAD_DIM
    _cast_weight_once(w_ref, wb_ref, 0)
    acc = epilogue(jnp.dot(h_ref[...], wb_ref[...], preferred_element_type=F32), *extra)
    ones = jnp.ones((ones_rows, BS), F32) if ones_rows else None
    for b in range(acc.shape[0] // BS):
        parts = []
        for hh in range(acc.shape[1] // HD):
            parts.append(acc[b * BS:(b + 1) * BS, hh * HD:(hh + 1) * HD].T)
            if ones_rows:
                parts.append(ones)
        o_ref[b] = jnp.concatenate(parts, axis=0).astype(o_ref.dtype)


def _proj_t(h, w, col0, epilogue, row_extras=(), ones_rows=0, tm=1024):
    T, K = h.shape
    tn = MOBA_WIDTH
    tm = min(tm, T)
    cb = col0 // tn
    rows = MOBA_HEADS * (MOBA_HEAD_DIM + ones_rows)
    in_specs = [pl.BlockSpec((tm, K), lambda i: (i, 0)), pl.BlockSpec((K, tn), lambda i: (0, cb))]
    for e in row_extras:
        in_specs.append(pl.BlockSpec((tm, e.shape[1]), lambda i: (i, 0)))
    vmem = 2 * (_nbytes((tm, K), h.dtype) + _nbytes((K, tn), w.dtype) + _nbytes((tm, 2 * tn), BF16))
    vmem += 4 * _nbytes((tm, tn), F32) + _nbytes((K, tn), BF16)
    return pl.pallas_call(
        functools.partial(_proj_t_kernel, epilogue=epilogue, ones_rows=ones_rows),
        out_shape=jax.ShapeDtypeStruct((T // MOBA_BLOCK, rows, MOBA_BLOCK), BF16),
        grid=(T // tm,),
        in_specs=in_specs,
        out_specs=pl.BlockSpec((tm // MOBA_BLOCK, rows, MOBA_BLOCK), lambda i: (i, 0, 0)),
        scratch_shapes=[pltpu.VMEM((K, tn), BF16)],
        compiler_params=_cparams(1, vmem),
        name="proj_t",
    )(h, w, *row_extras)


def _hgrn_kernel(q_ref, g_ref, v_ref, og_ref, ng_ref, o_ref, st_ref):
    W, HD, NH, C, S = HGRN_WIDTH, HGRN_HEAD_DIM, HGRN_HEADS, HGRN_CHUNK, V7X_SUBLANES
    J = C // S

    @pl.when(pl.program_id(0) == 0)
    def _():
        st_ref[...] = jnp.zeros_like(st_ref)

    def r3(x):
        return x.astype(F32).reshape(J, S, W)

    def sub_bcast(x3, r):
        return jnp.broadcast_to(x3[:, r:r + 1, :], x3.shape)

    g3, q3, v3 = r3(g_ref[...]), r3(q_ref[...]), r3(v_ref[...])
    sub = lax.broadcasted_iota(I32, (1, S, W), 1)

    c3 = g3
    for s in (1, 2, 4):
        c3 = c3 + jnp.where(sub >= s, pltpu.roll(c3, s, axis=1), 0.0)
    run = jnp.zeros((1, 1, W), F32)
    carry = []
    for j in range(J):
        carry.append(run)
        run = run + c3[j:j + 1, S - 1:S, :]
    b3 = c3 + jnp.concatenate(carry, axis=0)
    bC = run

    k3 = 1.0 - jnp.exp(g3)
    qe3 = q3 * jnp.exp(b3)
    ks3 = k3 * jnp.exp(bC - b3)

    levels = [(0, q3, k3)]
    ref1 = jnp.where(sub % 2 == 0, b3, pltpu.roll(b3, 1, axis=1))
    ref2 = jnp.where(sub < 4, sub_bcast(b3, 1), sub_bcast(b3, 5))
    ref4 = sub_bcast(b3, 3)
    for lvl, (ref, upper) in enumerate(((ref1, sub % 2 == 1), (ref2, sub % 4 >= 2), (ref4, sub >= 4)), start=1):
        e = jnp.exp(-jnp.abs(b3 - ref))
        levels.append((lvl, jnp.where(upper, q3 * e, 0.0), jnp.where(upper, 0.0, k3 * e)))
    zero_group = jnp.zeros((1, S, W), F32)
    for lvl, half in enumerate((1, 2, 4, 8), start=4):
        qparts, kparts = [], []
        for j in range(J):
            jr = (j // (2 * half)) * (2 * half) + half - 1
            ref = b3[jr:jr + 1, S - 1:S, :]
            if (j % (2 * half)) >= half:
                qparts.append(q3[j:j + 1] * jnp.exp(b3[j:j + 1] - ref))
                kparts.append(zero_group)
            else:
                qparts.append(zero_group)
                kparts.append(k3[j:j + 1] * jnp.exp(ref - b3[j:j + 1]))
        levels.append((lvl, jnp.concatenate(qparts, axis=0), jnp.concatenate(kparts, axis=0)))

    tr = lax.broadcasted_iota(I32, (C, C), 0)
    tc = lax.broadcasted_iota(I32, (C, C), 1)
    xr = tr ^ tc
    code = jnp.zeros((C, C), I32)
    for lvl in range(1, 8):
        code = jnp.where(xr >= (1 << (lvl - 1)), lvl, code)
    code = jnp.where(tc > tr, -1, code)

    def mat(x3, h):
        return x3.reshape(C, W)[:, h * HD:(h + 1) * HD].astype(BF16)

    nt = (((1,), (1,)), ((), ()))
    tn = (((0,), (0,)), ((), ()))
    ebc = jnp.exp(bC).reshape(1, W)
    ng = ng_ref[...]
    for h in range(NH):
        a_mat = jnp.zeros((C, C), F32)
        for lvl, qr, kr in levels:
            s = lax.dot_general(mat(qr, h), mat(kr, h), nt, preferred_element_type=F32)
            a_mat = jnp.where(code == lvl, s, a_mat)
        vh = mat(v3, h)
        st = st_ref[h]
        o = jnp.dot(a_mat.astype(BF16), vh, preferred_element_type=F32)
        o = o + lax.dot_general(mat(qe3, h), st.astype(BF16), nt, preferred_element_type=F32)
        o = _rms(o, ng) * og_ref[:, h * HD:(h + 1) * HD].astype(F32)
        o_ref[:, h * HD:(h + 1) * HD] = o.astype(o_ref.dtype)
        st_ref[h] = st * ebc[:, h * HD:(h + 1) * HD] + lax.dot_general(
            vh, mat(ks3, h), tn, preferred_element_type=F32)


def _hgrn(q, logf, v, og, norm_g):
    T, W = q.shape
    C = HGRN_CHUNK
    blk = pl.BlockSpec((C, W), lambda c: (c, 0))
    vmem = 64 * _nbytes((C, W), F32)
    return pl.pallas_call(
        _hgrn_kernel,
        out_shape=jax.ShapeDtypeStruct((T, W), BF16),
        grid=(T // C,),
        in_specs=[blk, blk, blk, blk, pl.BlockSpec((1, HGRN_HEAD_DIM), lambda c: (0, 0))],
        out_specs=blk,
        scratch_shapes=[pltpu.VMEM((HGRN_HEADS, HGRN_HEAD_DIM, HGRN_HEAD_DIM), F32)],
        compiler_params=_cparams(1, vmem),
        name="hgrn",
    )(q, logf, v, og, norm_g.reshape(1, HGRN_HEAD_DIM))


def _moba_kernel(qt_ref, k_ref, vt_ref, o_ref, km_ref):
    BS, HD, VR = MOBA_BLOCK, MOBA_HEAD_DIM, MOBA_VT_ROWS
    T = k_ref.shape[0]
    NB = T // BS
    G = o_ref.shape[1] // HD
    cur = pl.program_id(1)

    @pl.when(cur == 0)
    def _():
        for g in range(G):
            kf = k_ref[:, 2 * g * HD:(2 * g + 1) * HD].astype(F32).reshape(NB, BS, HD)
            km_ref[g] = jnp.sum(kf, axis=1) * (1.0 / BS)

    blk = lax.broadcasted_iota(I32, (NB, BS), 0)
    key = lax.broadcasted_iota(I32, (BS, BS), 0)
    qry = lax.broadcasted_iota(I32, (BS, BS), 1)
    pad = jnp.zeros((V7X_LANES - NB, BS), F32)
    r0 = pl.multiple_of(cur * BS, BS)
    qts = [qt_ref[0, g * HD:(g + 1) * HD, :] for g in range(G)]
    gts = [jnp.dot(km_ref[g].astype(BF16), qts[g], preferred_element_type=F32) for g in range(G)]
    s_own = [jnp.dot(k_ref[pl.ds(r0, BS), 2 * g * HD:(2 * g + 1) * HD], qts[g], preferred_element_type=F32)
             for g in range(G)]
    qcs = []
    for g in range(G):
        gt = jnp.where(blk < cur, gts[g], NEG_INF)
        sel = jnp.zeros((NB, BS), jnp.bool_)
        for _ in range(MOBA_TOPK):
            mx = jnp.max(gt, axis=0, keepdims=True)
            idx = jnp.min(jnp.where(gt == mx, blk, NB), axis=0, keepdims=True)
            pick = (blk == idx) & (mx > 0.5 * NEG_INF)
            sel = sel | pick
            gt = jnp.where(pick, NEG_INF, gt)
        pen = jnp.concatenate([jnp.where(sel, 0.0, NEG_INF), pad], axis=0).astype(BF16)
        qcs.append(jnp.concatenate([qts[g], pen], axis=0))

    def scores(n):
        rn = pl.multiple_of(n * BS, BS)
        return tuple(jnp.dot(k_ref[pl.ds(rn, BS), 2 * g * HD:(2 * g + 2) * HD], qcs[g], preferred_element_type=F32)
                     for g in range(G))

    ms, ps = [], []
    for g in range(G):
        s = jnp.where(key <= qry, s_own[g], NEG_INF)
        m0 = jnp.max(s, axis=0, keepdims=True)
        ms.append(m0)
        ps.append(jnp.exp(s - m0).astype(BF16))
    accs = [jnp.dot(vt_ref[cur, g * VR:(g + 1) * VR, :], ps[g], preferred_element_type=F32)
            for g in range(G)]

    KB = MOBA_BLOCKS_PER_STEP

    def body(c, carry):
        ms, accs = carry
        r = pl.multiple_of(c * KB * BS, KB * BS)
        sns = [jnp.dot(k_ref[pl.ds(r, KB * BS), 2 * g * HD:(2 * g + 2) * HD], qcs[g], preferred_element_type=F32)
               for g in range(G)]
        new_ms, alphas, pns = [], [], []
        for g in range(G):
            m_new = jnp.maximum(ms[g], jnp.max(sns[g], axis=0, keepdims=True))
            alphas.append(jnp.exp(ms[g] - m_new))
            pns.append(jnp.exp(sns[g] - m_new).astype(BF16))
            new_ms.append(m_new)
        new_accs = []
        for g in range(G):
            pv = alphas[g] * accs[g]
            for j in range(KB):
                pv = pv + jnp.dot(vt_ref[KB * c + j, g * VR:(g + 1) * VR, :], pns[g][j * BS:(j + 1) * BS],
                                  preferred_element_type=F32)
            new_accs.append(pv)
        return tuple(new_ms), tuple(new_accs)

    _, accs = lax.fori_loop(0, (cur + KB - 1) // KB, body, (tuple(ms), tuple(accs)))
    for g in range(G):
        ot = accs[g][:HD, :] / accs[g][HD:HD + 1, :]
        o_ref[:, g * HD:(g + 1) * HD] = ot.T.astype(o_ref.dtype)


def _moba(mqt, mk_aug, mvt, heads_per_step=4):
    T = mk_aug.shape[0]
    BS, HD, G, VR = MOBA_BLOCK, MOBA_HEAD_DIM, heads_per_step, MOBA_VT_ROWS
    NB = T // BS
    vmem = _nbytes((T, 2 * G * HD), BF16) + _nbytes((NB, G * VR, BS), BF16) + 8 * _nbytes((BS, G * HD), BF16)
    vmem += 16 * G * _nbytes((BS, BS), F32)
    resident = pl.Buffered(1)
    return pl.pallas_call(
        _moba_kernel,
        out_shape=jax.ShapeDtypeStruct((T, MOBA_WIDTH), BF16),
        grid=(MOBA_HEADS // G, NB),
        in_specs=[
            pl.BlockSpec((1, G * HD, BS), lambda h, i: (i, h, 0)),
            pl.BlockSpec((T, 2 * G * HD), lambda h, i: (0, h), pipeline_mode=resident),
            pl.BlockSpec((NB, G * VR, BS), lambda h, i: (0, h, 0), pipeline_mode=resident),
        ],
        out_specs=pl.BlockSpec((BS, G * HD), lambda h, i: (i, h)),
        scratch_shapes=[pltpu.VMEM((G, NB, HD), F32)],
        compiler_params=_cparams(2, vmem),
        name="moba",
    )(mqt, mk_aug, mvt)


def _merge_kernel(oh_ref, om_ref, ga_ref, gb_ref, wh_ref, wm_ref, o_ref):
    a = jnp.dot(oh_ref[...], wh_ref[...], preferred_element_type=F32)
    b = jnp.dot(om_ref[...], wm_ref[...], preferred_element_type=F32)
    o_ref[...] = (ga_ref[...].astype(F32) * a + gb_ref[...].astype(F32) * b).astype(o_ref.dtype)


def _merge(o_hgrn, o_moba, gates, w_up_hgrn, w_up_moba, tm=512):
    T = o_hgrn.shape[0]
    D = D_MODEL
    vmem = 2 * (_nbytes((tm, HGRN_WIDTH), F32) + _nbytes((tm, MOBA_WIDTH), BF16) + 3 * _nbytes((tm, D), BF16)
                + 2 * _nbytes((HGRN_WIDTH, D), BF16)) + 3 * _nbytes((tm, D), F32)
    return pl.pallas_call(
        _merge_kernel,
        out_shape=jax.ShapeDtypeStruct((T, D), BF16),
        grid=(T // tm,),
        in_specs=[
            pl.BlockSpec((tm, HGRN_WIDTH), lambda i: (i, 0)),
            pl.BlockSpec((tm, MOBA_WIDTH), lambda i: (i, 0)),
            pl.BlockSpec((tm, D), lambda i: (i, 0)),
            pl.BlockSpec((tm, D), lambda i: (i, 1)),
            pl.BlockSpec((HGRN_WIDTH, D), lambda i: (0, 0)),
            pl.BlockSpec((MOBA_WIDTH, D), lambda i: (0, 0)),
        ],
        out_specs=pl.BlockSpec((tm, D), lambda i: (i, 0)),
        compiler_params=_cparams(1, vmem),
        name="merge",
    )(o_hgrn, o_moba, gates, gates, w_up_hgrn, w_up_moba)


def _outproj_kernel(m_ref, x_ref, w_ref, g_ref, x1_ref, h2_ref):
    x1 = x_ref[...] + jnp.dot(m_ref[...], w_ref[...], preferred_element_type=F32)
    x1_ref[...] = x1
    h2_ref[...] = _rms(x1, g_ref[...])


def _outproj(merged, x, w_out, g_ffn, tm=256):
    T, D = x.shape
    vmem = 2 * (_nbytes((tm, D), BF16) + 3 * _nbytes((tm, D), F32) + _nbytes((D, D), BF16)) + 2 * _nbytes((tm, D), F32)
    return pl.pallas_call(
        _outproj_kernel,
        out_shape=(jax.ShapeDtypeStruct((T, D), F32), jax.ShapeDtypeStruct((T, D), F32)),
        grid=(T // tm,),
        in_specs=[
            pl.BlockSpec((tm, D), lambda i: (i, 0)),
            pl.BlockSpec((tm, D), lambda i: (i, 0)),
            pl.BlockSpec((D, D), lambda i: (0, 0)),
            pl.BlockSpec((1, D), lambda i: (0, 0)),
        ],
        out_specs=(pl.BlockSpec((tm, D), lambda i: (i, 0)), pl.BlockSpec((tm, D), lambda i: (i, 0))),
        compiler_params=_cparams(1, vmem),
        name="outproj",
    )(merged, x, w_out, g_ffn.reshape(1, D))


def _router_kernel(h_ref, w_ref, info_ref, cnt_ref, carry_ref):
    tm = h_ref.shape[0]

    @pl.when(pl.program_id(0) == 0)
    def _():
        carry_ref[...] = jnp.zeros_like(carry_ref)

    logits = jnp.dot(h_ref[...].astype(BF16), w_ref[...], preferred_element_type=F32)
    lane = lax.broadcasted_iota(I32, (tm, V7X_LANES), 1)
    is_g = lane < N_GROUPS
    gl = jnp.where(is_g, logits, NEG_INF)
    gmax = jnp.max(gl, axis=1, keepdims=True)
    g_sel = jnp.min(jnp.where(gl == gmax, lane, V7X_LANES), axis=1, keepdims=True)
    gsum = jnp.sum(jnp.where(is_g, jnp.exp(gl - gmax), 0.0), axis=1, keepdims=True)
    p_group = 1.0 / gsum
    lo = N_GROUPS + EXPERTS_PER_GROUP * g_sel
    emask = (lane >= lo) & (lane < lo + EXPERTS_PER_GROUP)
    el = jnp.where(emask, logits, NEG_INF)
    e1 = jnp.max(el, axis=1, keepdims=True)
    i1 = jnp.min(jnp.where((el == e1) & emask, lane, V7X_LANES), axis=1, keepdims=True)
    emask2 = emask & (lane != i1)
    el2 = jnp.where(emask2, logits, NEG_INF)
    e2 = jnp.max(el2, axis=1, keepdims=True)
    i2 = jnp.min(jnp.where((el2 == e2) & emask2, lane, V7X_LANES), axis=1, keepdims=True)
    r = jnp.exp(e2 - e1)
    w1 = p_group / (1.0 + r)
    w2 = p_group * r / (1.0 + r)
    eid1 = i1 - N_GROUPS
    eid2 = i2 - N_GROUPS
    oh1 = jnp.where(lane == eid1, 1.0, 0.0)
    oh2 = jnp.where(lane == eid2, 1.0, 0.0)
    cnt = oh1 + oh2
    tri = jnp.where(lax.broadcasted_iota(I32, (tm, tm), 0) > lax.broadcasted_iota(I32, (tm, tm), 1), 1.0, 0.0)
    before = jnp.dot(tri.astype(BF16), cnt.astype(BF16), preferred_element_type=F32) + carry_ref[...]
    rank1 = jnp.sum(oh1 * before, axis=1, keepdims=True)
    rank2 = jnp.sum(oh2 * before, axis=1, keepdims=True)
    carry_ref[...] = carry_ref[...] + jnp.sum(cnt, axis=0, keepdims=True)
    info = jnp.zeros((tm, V7X_LANES), F32)
    for k, val in enumerate((eid1.astype(F32), eid2.astype(F32), w1, w2, rank1, rank2)):
        info = jnp.where(lane == k, val, info)
    info_ref[...] = info
    cnt_ref[...] = carry_ref[...]


def _router(h2, w_router, tm=256):
    T, D = h2.shape
    vmem = 2 * (_nbytes((tm, D), F32) + _nbytes((D, V7X_LANES), BF16)) + 16 * _nbytes((tm, V7X_LANES), F32) + (1 << 22)
    return pl.pallas_call(
        _router_kernel,
        out_shape=(jax.ShapeDtypeStruct((T, V7X_LANES), F32), jax.ShapeDtypeStruct((1, V7X_LANES), F32)),
        grid=(T // tm,),
        in_specs=[pl.BlockSpec((tm, D), lambda i: (i, 0)), pl.BlockSpec((D, V7X_LANES), lambda i: (0, 0))],
        out_specs=(pl.BlockSpec((tm, V7X_LANES), lambda i: (i, 0)), pl.BlockSpec((1, V7X_LANES), lambda i: (0, 0))),
        scratch_shapes=[pltpu.VMEM((1, V7X_LANES), F32)],
        compiler_params=_cparams(1, vmem),
        name="router",
    )(h2, w_router)


def _row_copy(src_ref, src_row, dst_ref, dst_row, sem):
    return pltpu.make_async_copy(src_ref.at[pl.ds(src_row, 1), :], dst_ref.at[pl.ds(dst_row, 1), :], sem)


ROW_DMA_PRIORITY = 1


def _slot_token_kernel(dest_ref, o_ref):
    def clear(i, c):
        o_ref[i] = 0
        return c

    def put(t, c):
        for k in range(TOPK_IN_GROUP):
            o_ref[dest_ref[t * TOPK_IN_GROUP + k]] = t
        return c

    lax.fori_loop(0, o_ref.shape[0], clear, 0, unroll=8)
    lax.fori_loop(0, dest_ref.shape[0] // TOPK_IN_GROUP, put, 0, unroll=8)


def _slot_token(dest_flat, n_slots):
    return pl.pallas_call(
        _slot_token_kernel,
        out_shape=jax.ShapeDtypeStruct((n_slots,), I32),
        in_specs=[pl.BlockSpec(memory_space=pltpu.SMEM)],
        out_specs=pl.BlockSpec(memory_space=pltpu.SMEM),
        name="slot_token",
    )(dest_flat)


def _expert_kernel(be_ref, nu_ref, tok_ref, h_ref, wg_ref, wu_ref, wd_ref, y_ref, xbuf, wgb_ref, wub_ref, wdb_ref,
                   sem):
    R = MOE_ROWS
    step = pl.program_id(0)
    b = step - 1
    n_used = nu_ref[0]
    used = (b >= 0) & (b < n_used)

    PARTS, UNROLL = 4, 8

    def fetch_part(blk, part):
        slot = blk % 2
        rows = R // PARTS

        def issue(j, c):
            for u in range(UNROLL):
                r = part * rows + j * UNROLL + u
                _row_copy(h_ref, tok_ref[0, 0, r], xbuf.at[slot], r, sem.at[slot]).start(priority=ROW_DMA_PRIORITY)
            return c

        lax.fori_loop(0, rows // UNROLL, issue, 0)

    def fetch(blk):
        for part in range(PARTS):
            fetch_part(blk, part)

    def drain(blk):
        slot = blk % 2
        for _ in range(R):
            _row_copy(h_ref, 0, xbuf.at[0], 0, sem.at[slot]).wait()

    @pl.when(step == 0)
    def _():
        fetch(step)

    @pl.when((b >= 0) & (b <= n_used))
    def _():
        drain(b)

    @pl.when(used & ((b == 0) | (be_ref[jnp.maximum(b, 0)] != be_ref[jnp.maximum(b - 1, 0)])))
    def _():
        wgb_ref[...] = wg_ref[0].astype(BF16)
        wub_ref[...] = wu_ref[0].astype(BF16)
        wdb_ref[...] = wd_ref[0].astype(BF16)

    @pl.when(used)
    def _():
        half = y_ref.shape[1] // 2
        x = xbuf[b % 2].astype(BF16)
        fetch_part(step, 0)
        a = jnp.dot(x, wgb_ref[...], preferred_element_type=F32)
        fetch_part(step, 1)
        u = jnp.dot(x, wub_ref[...], preferred_element_type=F32)
        hm = (a * jax.nn.sigmoid(a) * u).astype(BF16)
        fetch_part(step, 2)
        y_ref[:, :half] = jnp.dot(hm, wdb_ref[:, :half], preferred_element_type=F32)
        fetch_part(step, 3)
        y_ref[:, half:] = jnp.dot(hm, wdb_ref[:, half:], preferred_element_type=F32)

    @pl.when((b >= n_used))
    def _():
        y_ref[...] = jnp.zeros_like(y_ref)


def _experts(block_expert, n_used, slot_token, h2, w_gate, w_up, w_down):
    T, D = h2.shape
    R, Fd = MOE_ROWS, D_EXPERT
    m_pad = slot_token.shape[0] - R
    n_blocks = m_pad // R
    vmem = 2 * (_nbytes((R, D), F32) + 3 * _nbytes((D, Fd), F32)) + 3 * _nbytes((D, Fd), BF16)
    vmem += 6 * _nbytes((R, D), F32)
    blk = lambda i: jnp.clip(i - 1, 0, n_blocks - 1)
    return pl.pallas_call(
        _expert_kernel,
        out_shape=jax.ShapeDtypeStruct((m_pad, D), F32),
        grid_spec=pltpu.PrefetchScalarGridSpec(
            num_scalar_prefetch=2,
            grid=(n_blocks + 1,),
            in_specs=[
                pl.BlockSpec((1, 1, R), lambda i, be, nu: (i, 0, 0), memory_space=pltpu.SMEM),
                pl.BlockSpec(memory_space=pl.ANY),
                pl.BlockSpec((1, D, Fd), lambda i, be, nu: (be[blk(i)], 0, 0)),
                pl.BlockSpec((1, D, Fd), lambda i, be, nu: (be[blk(i)], 0, 0)),
                pl.BlockSpec((1, Fd, D), lambda i, be, nu: (be[blk(i)], 0, 0)),
            ],
            out_specs=pl.BlockSpec((R, D), lambda i, be, nu: (blk(i), 0)),
            scratch_shapes=[pltpu.VMEM((2, R, D), F32), pltpu.VMEM((D, Fd), BF16), pltpu.VMEM((D, Fd), BF16),
                            pltpu.VMEM((Fd, D), BF16), pltpu.SemaphoreType.DMA((2,))],
        ),
        compiler_params=_cparams(1, vmem),
        name="experts",
    )(block_expert, n_used, slot_token.reshape(n_blocks + 1, 1, R), h2, w_gate, w_up, w_down)


def _combine_kernel(dest_ref, x1_ref, info_ref, yb_ref, p_ref, gp_ref, wpg_ref, wpp_ref, gf_ref, o_ref, ybuf, sem):
    tm = x1_ref.shape[0]
    step = pl.program_id(0)
    n_tiles = pl.num_programs(0) - 2
    K = TOPK_IN_GROUP

    def fetch(tile):
        slot = tile % 2
        for r in range(tm):
            for k in range(K):
                _row_copy(yb_ref, dest_ref[(tile * tm + r) * K + k], ybuf.at[slot * K + k], r,
                          sem.at[slot]).start(priority=ROW_DMA_PRIORITY)

    def drain(tile):
        slot = tile % 2
        for _ in range(tm * K):
            _row_copy(yb_ref, 0, ybuf.at[0], 0, sem.at[slot]).wait()

    @pl.when(step == 0)
    def _():
        fetch(step)

    @pl.when(step > 0)
    def _():
        drain(step - 1)

    @pl.when((step > 0) & (step <= n_tiles))
    def _():
        tile = step - 1
        fetch(step)
        slot = tile % 2
        info = info_ref[...]
        x2 = x1_ref[...] + info[:, 2:3] * ybuf[slot * K] + info[:, 3:4] * ybuf[slot * K + 1]
        hp = _rms(x2, gp_ref[...]).astype(BF16)
        z = jnp.dot(hp, wpg_ref[...], preferred_element_type=F32)
        pp = jnp.dot(p_ref[...].astype(BF16), wpp_ref[...], preferred_element_type=F32)
        x3 = x2 + jax.nn.sigmoid(z) * pp
        o_ref[...] = _rms(x3, gf_ref[...])


def _combine(dest_flat, x1, info, yb, p, g_ple, w_ple_gate, w_ple_proj, g_final, tm=256):
    T, D = x1.shape
    n_tiles = T // tm
    dest_padded = jnp.concatenate([dest_flat, jnp.zeros((tm * TOPK_IN_GROUP,), I32)])
    vmem = 2 * (2 * _nbytes((tm, D), F32) + _nbytes((D, D), BF16) + _nbytes((PLE_DIM, D), BF16)
                + _nbytes((tm, PLE_DIM), F32)) + 10 * _nbytes((tm, D), F32)
    tile = lambda i, d: (jnp.clip(i - 1, 0, n_tiles - 1), 0)
    return pl.pallas_call(
        _combine_kernel,
        out_shape=jax.ShapeDtypeStruct((T, D), F32),
        grid_spec=pltpu.PrefetchScalarGridSpec(
            num_scalar_prefetch=1,
            grid=(n_tiles + 2,),
            in_specs=[
                pl.BlockSpec((tm, D), tile),
                pl.BlockSpec((tm, V7X_LANES), tile),
                pl.BlockSpec(memory_space=pl.ANY),
                pl.BlockSpec((tm, PLE_DIM), tile),
                pl.BlockSpec((1, D), lambda i, d: (0, 0)),
                pl.BlockSpec((D, D), lambda i, d: (0, 0)),
                pl.BlockSpec((PLE_DIM, D), lambda i, d: (0, 0)),
                pl.BlockSpec((1, D), lambda i, d: (0, 0)),
            ],
            out_specs=pl.BlockSpec((tm, D), tile),
            scratch_shapes=[pltpu.VMEM((2 * TOPK_IN_GROUP, tm, D), F32), pltpu.SemaphoreType.DMA((2,))],
        ),
        compiler_params=_cparams(1, vmem),
        name="combine",
    )(dest_padded, x1, info, yb, p, g_ple.reshape(1, D), w_ple_gate, w_ple_proj, g_final.reshape(1, D))


def _rope_tables(T):
    half = MOBA_HEAD_DIM // 2
    inv_freq = ROPE_THETA ** (-jnp.arange(half, dtype=F32) / half)
    ang = jnp.arange(T, dtype=F32)[:, None] * inv_freq[None, :]
    cos, sin = jnp.cos(ang), jnp.sin(ang)
    return jnp.concatenate([cos, cos], axis=1), jnp.concatenate([-sin, sin], axis=1)


def _mixers(x2d, g_mix, w_in, lb, hgrn_norm_g):
    T = x2d.shape[0]
    W = HGRN_WIDTH
    h = _rmsnorm(x2d, g_mix, BF16)
    w = w_in
    cos, sin = _rope_tables(T)
    log_lb = jnp.log(lb).reshape(1, W)
    log_1m = jnp.log1p(-lb).reshape(1, W)
    hq = _proj(h, w, 0 * W, W, _ep_silu, BF16)
    logf = _proj(h, w, 1 * W, W, _ep_logf, F32, col_extras=(log_lb, log_1m))
    hi = _proj(h, w, 2 * W, W, _ep_identity, BF16)
    hog = _proj(h, w, 3 * W, W, _ep_silu, BF16)
    scale = MOBA_HEAD_DIM ** -0.5
    mqt = _proj_t(h, w, 4 * W, functools.partial(_ep_rope, scale=scale), row_extras=(cos, sin))
    blk = jnp.arange(T, dtype=I32)[:, None] // MOBA_BLOCK
    blk_onehot = (blk == jnp.arange(V7X_LANES, dtype=I32)[None, :]).astype(F32)
    mk = _proj(h, w, 5 * W, W, _ep_rope_aug, BF16, row_extras=(cos, sin, blk_onehot), widen=2)
    mvt = _proj_t(h, w, 6 * W, _ep_identity, ones_rows=MOBA_VT_ROWS - MOBA_HEAD_DIM)
    gates = _proj(h, w, 7 * W, 2 * D_MODEL, _ep_sigmoid, BF16)
    o_hgrn = _hgrn(hq, logf, hi, hog, hgrn_norm_g)
    o_moba = _moba(mqt, mk, mvt)
    return o_hgrn, o_moba, gates


def _moe_plan(info, cnt, T):
    R = MOE_ROWS
    eid = info[:, 0:TOPK_IN_GROUP].astype(I32)
    rank = info[:, 4:4 + TOPK_IN_GROUP].astype(I32)
    counts = cnt[0, :N_EXPERTS].astype(I32)
    padded = (counts + R - 1) // R * R
    pad_end = jnp.cumsum(padded)
    pad_start = pad_end - padded
    dest = (pad_start[eid] + rank).reshape(-1)
    n_blocks = (T * TOPK_IN_GROUP) // R + N_EXPERTS
    block_expert = jnp.minimum(
        jnp.searchsorted(pad_end, jnp.arange(n_blocks, dtype=I32) * R, side="right"), N_EXPERTS - 1).astype(I32)
    n_used = (pad_end[-1:] // R).astype(I32)
    slot_token = _slot_token(dest, (n_blocks + 1) * R)
    return dest, slot_token, block_expert, n_used


def kernel(x, p, norm_mix_g, w_in, hgrn_lb_raw, hgrn_norm_g, w_up_hgrn, w_up_moba, w_out, norm_ffn_g,
           w_router_group, w_router_expert, w_exp_gate, w_exp_up, w_exp_down, norm_ple_g, w_ple_gate,
           w_ple_proj, norm_final_g):
    B, T, D = x.shape
    assert B == 1 and D == D_MODEL and w_in.shape[0] == 1 and T % (4 * MOBA_BLOCK) == 0
    lower_bounds = jnp.cumsum(jax.nn.softmax(hgrn_lb_raw.astype(F32), axis=0), axis=0)
    x2d = x.reshape(T, D)
    o_hgrn, o_moba, gates = _mixers(x2d, norm_mix_g[0], w_in[0], lower_bounds[0], hgrn_norm_g[0])
    merged = _merge(o_hgrn, o_moba, gates, w_up_hgrn[0].astype(BF16), w_up_moba[0].astype(BF16))
    x1, h2 = _outproj(merged, x2d, w_out[0].astype(BF16), norm_ffn_g[0])
    w_router = jnp.pad(jnp.concatenate([w_router_group[0], w_router_expert[0]], axis=1),
                       ((0, 0), (0, V7X_LANES - N_GROUPS - N_EXPERTS))).astype(BF16)
    info, cnt = _router(h2, w_router)
    dest, slot_token, block_expert, n_used = _moe_plan(info, cnt, T)
    yb = _experts(block_expert, n_used, slot_token, h2, w_exp_gate[0], w_exp_up[0], w_exp_down[0])
    out = _combine(dest, x1, info, yb, p[0].reshape(T, PLE_DIM), norm_ple_g[0], w_ple_gate[0].astype(BF16),
                   w_ple_proj[0].astype(BF16), norm_final_g)
    return out.reshape(B, T, D)
```

```python
import functools

import jax
import jax.numpy as jnp
from jax import lax
from jax.experimental import pallas as pl
from jax.experimental.pallas import tpu as pltpu

F32 = jnp.float32
BF16 = jnp.bfloat16
I32 = jnp.int32

D_MODEL = 2048
PLE_DIM = 256
HGRN_HEADS = 8
HGRN_HEAD_DIM = 128
HGRN_WIDTH = HGRN_HEADS * HGRN_HEAD_DIM
MOBA_HEADS = 8
MOBA_HEAD_DIM = 128
MOBA_WIDTH = MOBA_HEADS * MOBA_HEAD_DIM
MOBA_BLOCK = 256
MOBA_TOPK = 3
ROPE_THETA = 10000.0
N_GROUPS = 4
EXPERTS_PER_GROUP = 8
N_EXPERTS = N_GROUPS * EXPERTS_PER_GROUP
TOPK_IN_GROUP = 2
D_EXPERT = 512
EPS = 1e-6
NEG_INF = -1e30

V7X_LANES = 128
V7X_SUBLANES = 8
V7X_VMEM_BUDGET_BYTES = 56 * 1024 * 1024

HGRN_CHUNK = 128
MOE_ROWS = 256


def _cparams(n_grid, vmem_bytes):
    return pltpu.CompilerParams(
        dimension_semantics=("arbitrary",) * n_grid,
        vmem_limit_bytes=int(min(max(vmem_bytes, 16 * 1024 * 1024), V7X_VMEM_BUDGET_BYTES)),
    )


def _nbytes(shape, dtype):
    n = 1
    for s in shape:
        n *= s
    return n * jnp.dtype(dtype).itemsize


def _rms(x, g):
    ms = jnp.mean(x * x, axis=-1, keepdims=True)
    return x * lax.rsqrt(ms + EPS) * g


def _rmsnorm_kernel(x_ref, g_ref, o_ref):
    o_ref[...] = _rms(x_ref[...], g_ref[...]).astype(o_ref.dtype)


def _rmsnorm(x, g, out_dtype, tm=512):
    T, D = x.shape
    return pl.pallas_call(
        _rmsnorm_kernel,
        out_shape=jax.ShapeDtypeStruct((T, D), out_dtype),
        grid=(T // tm,),
        in_specs=[pl.BlockSpec((tm, D), lambda i: (i, 0)), pl.BlockSpec((1, D), lambda i: (0, 0))],
        out_specs=pl.BlockSpec((tm, D), lambda i: (i, 0)),
        compiler_params=_cparams(1, 4 * _nbytes((tm, D), F32)),
        name="rmsnorm",
    )(x, g.reshape(1, D))


def _ep_identity(acc):
    return acc


def _ep_silu(acc):
    return acc * jax.nn.sigmoid(acc)


def _ep_sigmoid(acc):
    return jax.nn.sigmoid(acc)


def _ep_logf(acc, la_ref, lc_ref):
    ls = jnp.minimum(acc, 0.0) - jnp.log(1.0 + jnp.exp(-jnp.abs(acc)))
    u = la_ref[...]
    v = lc_ref[...] + ls
    return jnp.maximum(u, v) + jnp.log(1.0 + jnp.exp(-jnp.abs(u - v)))


def _ep_rope(acc, cos_ref, sin_ref, *, scale):
    cos = cos_ref[...]
    sin = sin_ref[...]
    outs = []
    for hh in range(acc.shape[1] // MOBA_HEAD_DIM):
        a = acc[:, hh * MOBA_HEAD_DIM:(hh + 1) * MOBA_HEAD_DIM]
        r = pltpu.roll(a, MOBA_HEAD_DIM // 2, axis=1)
        outs.append((a * cos + r * sin) * scale)
    return jnp.concatenate(outs, axis=1)


def _ep_rope_aug(acc, cos_ref, sin_ref, oh_ref):
    cos = cos_ref[...]
    sin = sin_ref[...]
    oh = oh_ref[...]
    outs = []
    for hh in range(acc.shape[1] // MOBA_HEAD_DIM):
        a = acc[:, hh * MOBA_HEAD_DIM:(hh + 1) * MOBA_HEAD_DIM]
        outs.append(a * cos + pltpu.roll(a, MOBA_HEAD_DIM // 2, axis=1) * sin)
        outs.append(oh)
    return jnp.concatenate(outs, axis=1)


def _cast_weight_once(w_ref, wb_ref, row_axis):
    @pl.when(pl.program_id(row_axis) == 0)
    def _():
        wb_ref[...] = w_ref[...].astype(wb_ref.dtype)


def _proj_kernel(h_ref, w_ref, *refs, epilogue):
    *extra, o_ref, wb_ref = refs
    _cast_weight_once(w_ref, wb_ref, 1)
    acc = jnp.dot(h_ref[...], wb_ref[...], preferred_element_type=F32)
    o_ref[...] = epilogue(acc, *extra).astype(o_ref.dtype)


def _proj(h, w, col0, ncols, epilogue, out_dtype, row_extras=(), col_extras=(), tm=1024, tn=1024, widen=1):
    T, K = h.shape
    tn = min(tn, ncols)
    tm = min(tm, T)
    cb = col0 // tn
    otn = widen * tn
    in_specs = [
        pl.BlockSpec((tm, K), lambda j, i: (i, 0)),
        pl.BlockSpec((K, tn), lambda j, i: (0, cb + j)),
    ]
    for e in row_extras:
        in_specs.append(pl.BlockSpec((tm, e.shape[1]), lambda j, i: (i, 0)))
    for e in col_extras:
        in_specs.append(pl.BlockSpec((1, tn), lambda j, i: (0, j)))
    vmem = 2 * (_nbytes((tm, K), h.dtype) + _nbytes((K, tn), w.dtype) + _nbytes((tm, otn), out_dtype))
    vmem += 3 * _nbytes((tm, otn), F32) + _nbytes((K, tn), BF16)
    return pl.pallas_call(
        functools.partial(_proj_kernel, epilogue=epilogue),
        out_shape=jax.ShapeDtypeStruct((T, widen * ncols), out_dtype),
        grid=(ncols // tn, T // tm),
        in_specs=in_specs,
        out_specs=pl.BlockSpec((tm, otn), lambda j, i: (i, j)),
        scratch_shapes=[pltpu.VMEM((K, tn), BF16)],
        compiler_params=_cparams(2, vmem),
        name="proj",
    )(h, w, *row_extras, *col_extras)


MOBA_VT_ROWS = MOBA_HEAD_DIM + 16
MOBA_BLOCKS_PER_STEP = 4


def _proj_t_kernel(h_ref, w_ref, *refs, epilogue, ones_rows):
    *extra, o_ref, wb_ref = refs
    BS, HD = MOBA_BLOCK, MOBA_HEAD_DIM
    _cast_weight_once(w_ref, wb_ref, 0)
    acc = epilogue(jnp.dot(h_ref[...], wb_ref[...], preferred_element_type=F32), *extra)
    ones = jnp.ones((ones_rows, BS), F32) if ones_rows else None
    for b in range(acc.shape[0] // BS):
        parts = []
        for hh in range(acc.shape[1] // HD):
            parts.append(acc[b * BS:(b + 1) * BS, hh * HD:(hh + 1) * HD].T)
            if ones_rows:
                parts.append(ones)
        o_ref[b] = jnp.concatenate(parts, axis=0).astype(o_ref.dtype)


def _proj_t(h, w, col0, epilogue, row_extras=(), ones_rows=0, tm=1024):
    T, K = h.shape
    tn = MOBA_WIDTH
    tm = min(tm, T)
    cb = col0 // tn
    rows = MOBA_HEADS * (MOBA_HEAD_DIM + ones_rows)
    in_specs = [pl.BlockSpec((tm, K), lambda i: (i, 0)), pl.BlockSpec((K, tn), lambda i: (0, cb))]
    for e in row_extras:
        in_specs.append(pl.BlockSpec((tm, e.shape[1]), lambda i: (i, 0)))
    vmem = 2 * (_nbytes((tm, K), h.dtype) + _nbytes((K, tn), w.dtype) + _nbytes((tm, 2 * tn), BF16))
    vmem += 4 * _nbytes((tm, tn), F32) + _nbytes((K, tn), BF16)
    return pl.pallas_call(
        functools.partial(_proj_t_kernel, epilogue=epilogue, ones_rows=ones_rows),
        out_shape=jax.ShapeDtypeStruct((T // MOBA_BLOCK, rows, MOBA_BLOCK), BF16),
        grid=(T // tm,),
        in_specs=in_specs,
        out_specs=pl.BlockSpec((tm // MOBA_BLOCK, rows, MOBA_BLOCK), lambda i: (i, 0, 0)),
        scratch_shapes=[pltpu.VMEM((K, tn), BF16)],
        compiler_params=_cparams(1, vmem),
        name="proj_t",
    )(h, w, *row_extras)


def _hgrn_kernel(q_ref, g_ref, v_ref, og_ref, ng_ref, o_ref, st_ref):
    W, HD, NH, C, S = HGRN_WIDTH, HGRN_HEAD_DIM, HGRN_HEADS, HGRN_CHUNK, V7X_SUBLANES
    J = C // S

    @pl.when(pl.program_id(0) == 0)
    def _():
        st_ref[...] = jnp.zeros_like(st_ref)

    def r3(x):
        return x.astype(F32).reshape(J, S, W)

    def sub_bcast(x3, r):
        return jnp.broadcast_to(x3[:, r:r + 1, :], x3.shape)

    g3, q3, v3 = r3(g_ref[...]), r3(q_ref[...]), r3(v_ref[...])
    sub = lax.broadcasted_iota(I32, (1, S, W), 1)

    c3 = g3
    for s in (1, 2, 4):
        c3 = c3 + jnp.where(sub >= s, pltpu.roll(c3, s, axis=1), 0.0)
    run = jnp.zeros((1, 1, W), F32)
    carry = []
    for j in range(J):
        carry.append(run)
        run = run + c3[j:j + 1, S - 1:S, :]
    b3 = c3 + jnp.concatenate(carry, axis=0)
    bC = run

    k3 = 1.0 - jnp.exp(g3)
    qe3 = q3 * jnp.exp(b3)
    ks3 = k3 * jnp.exp(bC - b3)

    levels = [(0, q3, k3)]
    ref1 = jnp.where(sub % 2 == 0, b3, pltpu.roll(b3, 1, axis=1))
    ref2 = jnp.where(sub < 4, sub_bcast(b3, 1), sub_bcast(b3, 5))
    ref4 = sub_bcast(b3, 3)
    for lvl, (ref, upper) in enumerate(((ref1, sub % 2 == 1), (ref2, sub % 4 >= 2), (ref4, sub >= 4)), start=1):
        e = jnp.exp(-jnp.abs(b3 - ref))
        levels.append((lvl, jnp.where(upper, q3 * e, 0.0), jnp.where(upper, 0.0, k3 * e)))
    zero_group = jnp.zeros((1, S, W), F32)
    for lvl, half in enumerate((1, 2, 4, 8), start=4):
        qparts, kparts = [], []
        for j in range(J):
            jr = (j // (2 * half)) * (2 * half) + half - 1
            ref = b3[jr:jr + 1, S - 1:S, :]
            if (j % (2 * half)) >= half:
                qparts.append(q3[j:j + 1] * jnp.exp(b3[j:j + 1] - ref))
                kparts.append(zero_group)
            else:
                qparts.append(zero_group)
                kparts.append(k3[j:j + 1] * jnp.exp(ref - b3[j:j + 1]))
        levels.append((lvl, jnp.concatenate(qparts, axis=0), jnp.concatenate(kparts, axis=0)))

    tr = lax.broadcasted_iota(I32, (C, C), 0)
    tc = lax.broadcasted_iota(I32, (C, C), 1)
    xr = tr ^ tc
    code = jnp.zeros((C, C), I32)
    for lvl in range(1, 8):
        code = jnp.where(xr >= (1 << (lvl - 1)), lvl, code)
    code = jnp.where(tc > tr, -1, code)

    def mat(x3, h):
        return x3.reshape(C, W)[:, h * HD:(h + 1) * HD].astype(BF16)

    nt = (((1,), (1,)), ((), ()))
    tn = (((0,), (0,)), ((), ()))
    ebc = jnp.exp(bC).reshape(1, W)
    ng = ng_ref[...]
    for h in range(NH):
        a_mat = jnp.zeros((C, C), F32)
        for lvl, qr, kr in levels:
            s = lax.dot_general(mat(qr, h), mat(kr, h), nt, preferred_element_type=F32)
            a_mat = jnp.where(code == lvl, s, a_mat)
        vh = mat(v3, h)
        st = st_ref[h]
        o = jnp.dot(a_mat.astype(BF16), vh, preferred_element_type=F32)
        o = o + lax.dot_general(mat(qe3, h), st.astype(BF16), nt, preferred_element_type=F32)
        o = _rms(o, ng) * og_ref[:, h * HD:(h + 1) * HD].astype(F32)
        o_ref[:, h * HD:(h + 1) * HD] = o.astype(o_ref.dtype)
        st_ref[h] = st * ebc[:, h * HD:(h + 1) * HD] + lax.dot_general(
            vh, mat(ks3, h), tn, preferred_element_type=F32)


def _hgrn(q, logf, v, og, norm_g):
    T, W = q.shape
    C = HGRN_CHUNK
    blk = pl.BlockSpec((C, W), lambda c: (c, 0))
    vmem = 64 * _nbytes((C, W), F32)
    return pl.pallas_call(
        _hgrn_kernel,
        out_shape=jax.ShapeDtypeStruct((T, W), BF16),
        grid=(T // C,),
        in_specs=[blk, blk, blk, blk, pl.BlockSpec((1, HGRN_HEAD_DIM), lambda c: (0, 0))],
        out_specs=blk,
        scratch_shapes=[pltpu.VMEM((HGRN_HEADS, HGRN_HEAD_DIM, HGRN_HEAD_DIM), F32)],
        compiler_params=_cparams(1, vmem),
        name="hgrn",
    )(q, logf, v, og, norm_g.reshape(1, HGRN_HEAD_DIM))


def _moba_kernel(qt_ref, k_ref, vt_ref, o_ref, km_ref):
    BS, HD, VR = MOBA_BLOCK, MOBA_HEAD_DIM, MOBA_VT_ROWS
    T = k_ref.shape[0]
    NB = T // BS
    G = o_ref.shape[1] // HD
    cur = pl.program_id(1)

    @pl.when(cur == 0)
    def _():
        for g in range(G):
            kf = k_ref[:, 2 * g * HD:(2 * g + 1) * HD].astype(F32).reshape(NB, BS, HD)
            km_ref[g] = jnp.sum(kf, axis=1) * (1.0 / BS)

    blk = lax.broadcasted_iota(I32, (NB, BS), 0)
    key = lax.broadcasted_iota(I32, (BS, BS), 0)
    qry = lax.broadcasted_iota(I32, (BS, BS), 1)
    pad = jnp.zeros((V7X_LANES - NB, BS), F32)
    r0 = pl.multiple_of(cur * BS, BS)
    qts = [qt_ref[0, g * HD:(g + 1) * HD, :] for g in range(G)]
    gts = [jnp.dot(km_ref[g].astype(BF16), qts[g], preferred_element_type=F32) for g in range(G)]
    s_own = [jnp.dot(k_ref[pl.ds(r0, BS), 2 * g * HD:(2 * g + 1) * HD], qts[g], preferred_element_type=F32)
             for g in range(G)]
    qcs = []
    for g in range(G):
        gt = jnp.where(blk < cur, gts[g], NEG_INF)
        sel = jnp.zeros((NB, BS), jnp.bool_)
        for _ in range(MOBA_TOPK):
            mx = jnp.max(gt, axis=0, keepdims=True)
            idx = jnp.min(jnp.where(gt == mx, blk, NB), axis=0, keepdims=True)
            pick = (blk == idx) & (mx > 0.5 * NEG_INF)
            sel = sel | pick
            gt = jnp.where(pick, NEG_INF, gt)
        pen = jnp.concatenate([jnp.where(sel, 0.0, NEG_INF), pad], axis=0).astype(BF16)
        qcs.append(jnp.concatenate([qts[g], pen], axis=0))

    def scores(n):
        rn = pl.multiple_of(n * BS, BS)
        return tuple(jnp.dot(k_ref[pl.ds(rn, BS), 2 * g * HD:(2 * g + 2) * HD], qcs[g], preferred_element_type=F32)
                     for g in range(G))

    ms, ps = [], []
    for g in range(G):
        s = jnp.where(key <= qry, s_own[g], NEG_INF)
        m0 = jnp.max(s, axis=0, keepdims=True)
        ms.append(m0)
        ps.append(jnp.exp(s - m0).astype(BF16))
    accs = [jnp.dot(vt_ref[cur, g * VR:(g + 1) * VR, :], ps[g], preferred_element_type=F32)
            for g in range(G)]

    KB = MOBA_BLOCKS_PER_STEP

    def body(c, carry):
        ms, accs = carry
        r = pl.multiple_of(c * KB * BS, KB * BS)
        sns = [jnp.dot(k_ref[pl.ds(r, KB * BS), 2 * g * HD:(2 * g + 2) * HD], qcs[g], preferred_element_type=F32)
               for g in range(G)]
        new_ms, alphas, pns = [], [], []
        for g in range(G):
            m_new = jnp.maximum(ms[g], jnp.max(sns[g], axis=0, keepdims=True))
            alphas.append(jnp.exp(ms[g] - m_new))
            pns.append(jnp.exp(sns[g] - m_new).astype(BF16))
            new_ms.append(m_new)
        new_accs = []
        for g in range(G):
            pv = alphas[g] * accs[g]
            for j in range(KB):
                pv = pv + jnp.dot(vt_ref[KB * c + j, g * VR:(g + 1) * VR, :], pns[g][j * BS:(j + 1) * BS],
                                  preferred_element_type=F32)
            new_accs.append(pv)
        return tuple(new_ms), tuple(new_accs)

    _, accs = lax.fori_loop(0, (cur + KB - 1) // KB, body, (tuple(ms), tuple(accs)))
    for g in range(G):
        ot = accs[g][:HD, :] / accs[g][HD:HD + 1, :]
        o_ref[:, g * HD:(g + 1) * HD] = ot.T.astype(o_ref.dtype)


def _moba(mqt, mk_aug, mvt, heads_per_step=4):
    T = mk_aug.shape[0]
    BS, HD, G, VR = MOBA_BLOCK, MOBA_HEAD_DIM, heads_per_step, MOBA_VT_ROWS
    NB = T // BS
    vmem = _nbytes((T, 2 * G * HD), BF16) + _nbytes((NB, G * VR, BS), BF16) + 8 * _nbytes((BS, G * HD), BF16)
    vmem += 16 * G * _nbytes((BS, BS), F32)
    resident = pl.Buffered(1)
    return pl.pallas_call(
        _moba_kernel,
        out_shape=jax.ShapeDtypeStruct((T, MOBA_WIDTH), BF16),
        grid=(MOBA_HEADS // G, NB),
        in_specs=[
            pl.BlockSpec((1, G * HD, BS), lambda h, i: (i, h, 0)),
            pl.BlockSpec((T, 2 * G * HD), lambda h, i: (0, h), pipeline_mode=resident),
            pl.BlockSpec((NB, G * VR, BS), lambda h, i: (0, h, 0), pipeline_mode=resident),
        ],
        out_specs=pl.BlockSpec((BS, G * HD), lambda h, i: (i, h)),
        scratch_shapes=[pltpu.VMEM((G, NB, HD), F32)],
        compiler_params=_cparams(2, vmem),
        name="moba",
    )(mqt, mk_aug, mvt)


def _merge_kernel(oh_ref, om_ref, ga_ref, gb_ref, wh_ref, wm_ref, o_ref):
    a = jnp.dot(oh_ref[...], wh_ref[...], preferred_element_type=F32)
    b = jnp.dot(om_ref[...], wm_ref[...], preferred_element_type=F32)
    o_ref[...] = (ga_ref[...].astype(F32) * a + gb_ref[...].astype(F32) * b).astype(o_ref.dtype)


def _merge(o_hgrn, o_moba, gates, w_up_hgrn, w_up_moba, tm=512):
    T = o_hgrn.shape[0]
    D = D_MODEL
    vmem = 2 * (_nbytes((tm, HGRN_WIDTH), F32) + _nbytes((tm, MOBA_WIDTH), BF16) + 3 * _nbytes((tm, D), BF16)
                + 2 * _nbytes((HGRN_WIDTH, D), BF16)) + 3 * _nbytes((tm, D), F32)
    return pl.pallas_call(
        _merge_kernel,
        out_shape=jax.ShapeDtypeStruct((T, D), BF16),
        grid=(T // tm,),
        in_specs=[
            pl.BlockSpec((tm, HGRN_WIDTH), lambda i: (i, 0)),
            pl.BlockSpec((tm, MOBA_WIDTH), lambda i: (i, 0)),
            pl.BlockSpec((tm, D), lambda i: (i, 0)),
            pl.BlockSpec((tm, D), lambda i: (i, 1)),
            pl.BlockSpec((HGRN_WIDTH, D), lambda i: (0, 0)),
            pl.BlockSpec((MOBA_WIDTH, D), lambda i: (0, 0)),
        ],
        out_specs=pl.BlockSpec((tm, D), lambda i: (i, 0)),
        compiler_params=_cparams(1, vmem),
        name="merge",
    )(o_hgrn, o_moba, gates, gates, w_up_hgrn, w_up_moba)


def _outproj_kernel(m_ref, x_ref, w_ref, g_ref, x1_ref, h2_ref):
    x1 = x_ref[...] + jnp.dot(m_ref[...], w_ref[...], preferred_element_type=F32)
    x1_ref[...] = x1
    h2_ref[...] = _rms(x1, g_ref[...])


def _outproj(merged, x, w_out, g_ffn, tm=256):
    T, D = x.shape
    vmem = 2 * (_nbytes((tm, D), BF16) + 3 * _nbytes((tm, D), F32) + _nbytes((D, D), BF16)) + 2 * _nbytes((tm, D), F32)
    return pl.pallas_call(
        _outproj_kernel,
        out_shape=(jax.ShapeDtypeStruct((T, D), F32), jax.ShapeDtypeStruct((T, D), F32)),
        grid=(T // tm,),
        in_specs=[
            pl.BlockSpec((tm, D), lambda i: (i, 0)),
            pl.BlockSpec((tm, D), lambda i: (i, 0)),
            pl.BlockSpec((D, D), lambda i: (0, 0)),
            pl.BlockSpec((1, D), lambda i: (0, 0)),
        ],
        out_specs=(pl.BlockSpec((tm, D), lambda i: (i, 0)), pl.BlockSpec((tm, D), lambda i: (i, 0))),
        compiler_params=_cparams(1, vmem),
        name="outproj",
    )(merged, x, w_out, g_ffn.reshape(1, D))


def _router_kernel(h_ref, w_ref, info_ref, cnt_ref, carry_ref):
    tm = h_ref.shape[0]

    @pl.when(pl.program_id(0) == 0)
    def _():
        carry_ref[...] = jnp.zeros_like(carry_ref)

    logits = jnp.dot(h_ref[...].astype(BF16), w_ref[...], preferred_element_type=F32)
    lane = lax.broadcasted_iota(I32, (tm, V7X_LANES), 1)
    is_g = lane < N_GROUPS
    gl = jnp.where(is_g, logits, NEG_INF)
    gmax = jnp.max(gl, axis=1, keepdims=True)
    g_sel = jnp.min(jnp.where(gl == gmax, lane, V7X_LANES), axis=1, keepdims=True)
    gsum = jnp.sum(jnp.where(is_g, jnp.exp(gl - gmax), 0.0), axis=1, keepdims=True)
    p_group = 1.0 / gsum
    lo = N_GROUPS + EXPERTS_PER_GROUP * g_sel
    emask = (lane >= lo) & (lane < lo + EXPERTS_PER_GROUP)
    el = jnp.where(emask, logits, NEG_INF)
    e1 = jnp.max(el, axis=1, keepdims=True)
    i1 = jnp.min(jnp.where((el == e1) & emask, lane, V7X_LANES), axis=1, keepdims=True)
    emask2 = emask & (lane != i1)
    el2 = jnp.where(emask2, logits, NEG_INF)
    e2 = jnp.max(el2, axis=1, keepdims=True)
    i2 = jnp.min(jnp.where((el2 == e2) & emask2, lane, V7X_LANES), axis=1, keepdims=True)
    r = jnp.exp(e2 - e1)
    w1 = p_group / (1.0 + r)
    w2 = p_group * r / (1.0 + r)
    eid1 = i1 - N_GROUPS
    eid2 = i2 - N_GROUPS
    oh1 = jnp.where(lane == eid1, 1.0, 0.0)
    oh2 = jnp.where(lane == eid2, 1.0, 0.0)
    cnt = oh1 + oh2
    tri = jnp.where(lax.broadcasted_iota(I32, (tm, tm), 0) > lax.broadcasted_iota(I32, (tm, tm), 1), 1.0, 0.0)
    before = jnp.dot(tri.astype(BF16), cnt.astype(BF16), preferred_element_type=F32) + carry_ref[...]
    rank1 = jnp.sum(oh1 * before, axis=1, keepdims=True)
    rank2 = jnp.sum(oh2 * before, axis=1, keepdims=True)
    carry_ref[...] = carry_ref[...] + jnp.sum(cnt, axis=0, keepdims=True)
    info = jnp.zeros((tm, V7X_LANES), F32)
    for k, val in enumerate((eid1.astype(F32), eid2.astype(F32), w1, w2, rank1, rank2)):
        info = jnp.where(lane == k, val, info)
    info_ref[...] = info
    cnt_ref[...] = carry_ref[...]


def _router(h2, w_router, tm=256):
    T, D = h2.shape
    vmem = 2 * (_nbytes((tm, D), F32) + _nbytes((D, V7X_LANES), BF16)) + 16 * _nbytes((tm, V7X_LANES), F32) + (1 << 22)
    return pl.pallas_call(
        _router_kernel,
        out_shape=(jax.ShapeDtypeStruct((T, V7X_LANES), F32), jax.ShapeDtypeStruct((1, V7X_LANES), F32)),
        grid=(T // tm,),
        in_specs=[pl.BlockSpec((tm, D), lambda i: (i, 0)), pl.BlockSpec((D, V7X_LANES), lambda i: (0, 0))],
        out_specs=(pl.BlockSpec((tm, V7X_LANES), lambda i: (i, 0)), pl.BlockSpec((1, V7X_LANES), lambda i: (0, 0))),
        scratch_shapes=[pltpu.VMEM((1, V7X_LANES), F32)],
        compiler_params=_cparams(1, vmem),
        name="router",
    )(h2, w_router)


def _row_copy(src_ref, src_row, dst_ref, dst_row, sem):
    return pltpu.make_async_copy(src_ref.at[pl.ds(src_row, 1), :], dst_ref.at[pl.ds(dst_row, 1), :], sem)


ROW_DMA_PRIORITY = 1


ROW_DMA_UNROLL = 8


def _dispatch_kernel(dest_ref, pend_ref, h_ref, xs_ref, zero_ref, sem):
    tm = h_ref.shape[0]
    step = pl.program_id(0)
    base = step * tm

    @pl.when(step == 0)
    def _():
        zero_ref[...] = jnp.zeros_like(zero_ref)

        def tail(e):
            return pltpu.make_async_copy(
                zero_ref, xs_ref.at[pl.ds(pl.multiple_of(pend_ref[e] - MOE_ROWS, MOE_ROWS), MOE_ROWS), :], sem)

        def nonempty(e):
            return pend_ref[e] > (pend_ref[e - 1] if e else 0)

        def unused(b):
            return pltpu.make_async_copy(
                zero_ref, xs_ref.at[pl.ds(pl.multiple_of(b * MOE_ROWS, MOE_ROWS), MOE_ROWS), :], sem)

        first_unused = pend_ref[N_EXPERTS - 1] // MOE_ROWS
        n_blocks = xs_ref.shape[0] // MOE_ROWS
        for e in range(N_EXPERTS):
            pl.when(nonempty(e))(lambda e=e: tail(e).start())
        lax.fori_loop(first_unused, n_blocks, lambda b, c: (unused(b).start(), c)[1], 0)
        for e in range(N_EXPERTS):
            pl.when(nonempty(e))(lambda e=e: tail(e).wait())
        lax.fori_loop(first_unused, n_blocks, lambda b, c: (unused(b).wait(), c)[1], 0)

    def issue(rb, c):
        for u in range(ROW_DMA_UNROLL):
            r = rb * ROW_DMA_UNROLL + u
            for k in range(TOPK_IN_GROUP):
                _row_copy(h_ref, r, xs_ref, dest_ref[(base + r) * TOPK_IN_GROUP + k], sem).start()
        return c

    lax.fori_loop(0, tm // ROW_DMA_UNROLL, issue, 0)
    for _ in range(tm * TOPK_IN_GROUP):
        _row_copy(h_ref, 0, xs_ref, 0, sem).wait()


def _dispatch(dest_flat, pad_end, h2, m_pad, tm=256):
    T, D = h2.shape
    return pl.pallas_call(
        _dispatch_kernel,
        out_shape=jax.ShapeDtypeStruct((m_pad, D), F32),
        grid_spec=pltpu.PrefetchScalarGridSpec(
            num_scalar_prefetch=2,
            grid=(T // tm,),
            in_specs=[pl.BlockSpec((tm, D), lambda i, d, pe: (i, 0))],
            out_specs=pl.BlockSpec(memory_space=pl.ANY),
            scratch_shapes=[pltpu.VMEM((MOE_ROWS, D), F32), pltpu.SemaphoreType.DMA(())],
        ),
        compiler_params=_cparams(1, 6 * _nbytes((tm, D), F32)),
        name="dispatch",
    )(dest_flat, pad_end, h2)


def _expert_kernel(be_ref, nu_ref, xs_ref, wg_ref, wu_ref, wd_ref, y_ref, wgb_ref, wub_ref, wdb_ref):
    b = pl.program_id(0)
    used = b < nu_ref[0]

    @pl.when(used & ((b == 0) | (be_ref[b] != be_ref[jnp.maximum(b - 1, 0)])))
    def _():
        wgb_ref[...] = wg_ref[0].astype(BF16)
        wub_ref[...] = wu_ref[0].astype(BF16)
        wdb_ref[...] = wd_ref[0].astype(BF16)

    @pl.when(used)
    def _():
        x = xs_ref[...].astype(BF16)
        a = jnp.dot(x, wgb_ref[...], preferred_element_type=F32)
        u = jnp.dot(x, wub_ref[...], preferred_element_type=F32)
        hm = (a * jax.nn.sigmoid(a) * u).astype(BF16)
        y_ref[...] = jnp.dot(hm, wdb_ref[...], preferred_element_type=F32)

    @pl.when(jnp.logical_not(used))
    def _():
        y_ref[...] = jnp.zeros_like(y_ref)


def _experts(block_expert, n_used, xs, w_gate, w_up, w_down):
    m_pad, D = xs.shape
    R, Fd = MOE_ROWS, D_EXPERT
    vmem = 2 * (2 * _nbytes((R, D), F32) + 3 * _nbytes((D, Fd), F32)) + 3 * _nbytes((D, Fd), BF16)
    vmem += 4 * _nbytes((R, D), F32)
    return pl.pallas_call(
        _expert_kernel,
        out_shape=jax.ShapeDtypeStruct((m_pad, D), F32),
        grid_spec=pltpu.PrefetchScalarGridSpec(
            num_scalar_prefetch=2,
            grid=(m_pad // R,),
            in_specs=[
                pl.BlockSpec((R, D), lambda b, be, nu: (jnp.minimum(b, jnp.maximum(nu[0] - 1, 0)), 0)),
                pl.BlockSpec((1, D, Fd), lambda b, be, nu: (be[b], 0, 0)),
                pl.BlockSpec((1, D, Fd), lambda b, be, nu: (be[b], 0, 0)),
                pl.BlockSpec((1, Fd, D), lambda b, be, nu: (be[b], 0, 0)),
            ],
            out_specs=pl.BlockSpec((R, D), lambda b, be, nu: (b, 0)),
            scratch_shapes=[pltpu.VMEM((D, Fd), BF16), pltpu.VMEM((D, Fd), BF16), pltpu.VMEM((Fd, D), BF16)],
        ),
        compiler_params=_cparams(1, vmem),
        name="experts",
    )(block_expert, n_used, xs, w_gate, w_up, w_down)


def _combine_kernel(dest_ref, x1_ref, info_ref, yb_ref, p_ref, gp_ref, wpg_ref, wpp_ref, gf_ref, o_ref, ybuf, sem):
    tm = x1_ref.shape[0]
    step = pl.program_id(0)
    n_tiles = pl.num_programs(0) - 2
    K = TOPK_IN_GROUP

    def fetch(tile):
        slot = tile % 2
        for r in range(tm):
            for k in range(K):
                _row_copy(yb_ref, dest_ref[(tile * tm + r) * K + k], ybuf.at[slot * K + k], r,
                          sem.at[slot]).start(priority=ROW_DMA_PRIORITY)

    def drain(tile):
        slot = tile % 2
        for _ in range(tm * K):
            _row_copy(yb_ref, 0, ybuf.at[0], 0, sem.at[slot]).wait()

    @pl.when(step == 0)
    def _():
        fetch(step)

    @pl.when(step > 0)
    def _():
        drain(step - 1)

    @pl.when((step > 0) & (step <= n_tiles))
    def _():
        tile = step - 1
        fetch(step)
        slot = tile % 2
        info = info_ref[...]
        x2 = x1_ref[...] + info[:, 2:3] * ybuf[slot * K] + info[:, 3:4] * ybuf[slot * K + 1]
        hp = _rms(x2, gp_ref[...]).astype(BF16)
        z = jnp.dot(hp, wpg_ref[...], preferred_element_type=F32)
        pp = jnp.dot(p_ref[...].astype(BF16), wpp_ref[...], preferred_element_type=F32)
        x3 = x2 + jax.nn.sigmoid(z) * pp
        o_ref[...] = _rms(x3, gf_ref[...])


def _combine(dest_flat, x1, info, yb, p, g_ple, w_ple_gate, w_ple_proj, g_final, tm=256):
    T, D = x1.shape
    n_tiles = T // tm
    dest_padded = jnp.concatenate([dest_flat, jnp.zeros((tm * TOPK_IN_GROUP,), I32)])
    vmem = 2 * (2 * _nbytes((tm, D), F32) + _nbytes((D, D), BF16) + _nbytes((PLE_DIM, D), BF16)
                + _nbytes((tm, PLE_DIM), F32)) + 10 * _nbytes((tm, D), F32)
    tile = lambda i, d: (jnp.clip(i - 1, 0, n_tiles - 1), 0)
    return pl.pallas_call(
        _combine_kernel,
        out_shape=jax.ShapeDtypeStruct((T, D), F32),
        grid_spec=pltpu.PrefetchScalarGridSpec(
            num_scalar_prefetch=1,
            grid=(n_tiles + 2,),
            in_specs=[
                pl.BlockSpec((tm, D), tile),
                pl.BlockSpec((tm, V7X_LANES), tile),
                pl.BlockSpec(memory_space=pl.ANY),
                pl.BlockSpec((tm, PLE_DIM), tile),
                pl.BlockSpec((1, D), lambda i, d: (0, 0)),
                pl.BlockSpec((D, D), lambda i, d: (0, 0)),
                pl.BlockSpec((PLE_DIM, D), lambda i, d: (0, 0)),
                pl.BlockSpec((1, D), lambda i, d: (0, 0)),
            ],
            out_specs=pl.BlockSpec((tm, D), tile),
            scratch_shapes=[pltpu.VMEM((2 * TOPK_IN_GROUP, tm, D), F32), pltpu.SemaphoreType.DMA((2,))],
        ),
        compiler_params=_cparams(1, vmem),
        name="combine",
    )(dest_padded, x1, info, yb, p, g_ple.reshape(1, D), w_ple_gate, w_ple_proj, g_final.reshape(1, D))


def _rope_tables(T):
    half = MOBA_HEAD_DIM // 2
    inv_freq = ROPE_THETA ** (-jnp.arange(half, dtype=F32) / half)
    ang = jnp.arange(T, dtype=F32)[:, None] * inv_freq[None, :]
    cos, sin = jnp.cos(ang), jnp.sin(ang)
    return jnp.concatenate([cos, cos], axis=1), jnp.concatenate([-sin, sin], axis=1)


def _mixers(x2d, g_mix, w_in, lb, hgrn_norm_g):
    T = x2d.shape[0]
    W = HGRN_WIDTH
    h = _rmsnorm(x2d, g_mix, BF16)
    w = w_in
    cos, sin = _rope_tables(T)
    log_lb = jnp.log(lb).reshape(1, W)
    log_1m = jnp.log1p(-lb).reshape(1, W)
    hq = _proj(h, w, 0 * W, W, _ep_silu, BF16)
    logf = _proj(h, w, 1 * W, W, _ep_logf, F32, col_extras=(log_lb, log_1m))
    hi = _proj(h, w, 2 * W, W, _ep_identity, BF16)
    hog = _proj(h, w, 3 * W, W, _ep_silu, BF16)
    scale = MOBA_HEAD_DIM ** -0.5
    mqt = _proj_t(h, w, 4 * W, functools.partial(_ep_rope, scale=scale), row_extras=(cos, sin))
    blk = jnp.arange(T, dtype=I32)[:, None] // MOBA_BLOCK
    blk_onehot = (blk == jnp.arange(V7X_LANES, dtype=I32)[None, :]).astype(F32)
    mk = _proj(h, w, 5 * W, W, _ep_rope_aug, BF16, row_extras=(cos, sin, blk_onehot), widen=2)
    mvt = _proj_t(h, w, 6 * W, _ep_identity, ones_rows=MOBA_VT_ROWS - MOBA_HEAD_DIM)
    gates = _proj(h, w, 7 * W, 2 * D_MODEL, _ep_sigmoid, BF16)
    o_hgrn = _hgrn(hq, logf, hi, hog, hgrn_norm_g)
    o_moba = _moba(mqt, mk, mvt)
    return o_hgrn, o_moba, gates


def _moe_plan(info, cnt, T):
    R = MOE_ROWS
    eid = info[:, 0:TOPK_IN_GROUP].astype(I32)
    rank = info[:, 4:4 + TOPK_IN_GROUP].astype(I32)
    counts = cnt[0, :N_EXPERTS].astype(I32)
    padded = (counts + R - 1) // R * R
    pad_end = jnp.cumsum(padded)
    pad_start = pad_end - padded
    dest = (pad_start[eid] + rank).reshape(-1)
    n_blocks = (T * TOPK_IN_GROUP) // R + N_EXPERTS
    block_expert = jnp.minimum(
        jnp.searchsorted(pad_end, jnp.arange(n_blocks, dtype=I32) * R, side="right"), N_EXPERTS - 1).astype(I32)
    n_used = (pad_end[-1:] // R).astype(I32)
    return dest, pad_end.astype(I32), block_expert, n_used, n_blocks * R


def kernel(x, p, norm_mix_g, w_in, hgrn_lb_raw, hgrn_norm_g, w_up_hgrn, w_up_moba, w_out, norm_ffn_g,
           w_router_group, w_router_expert, w_exp_gate, w_exp_up, w_exp_down, norm_ple_g, w_ple_gate,
           w_ple_proj, norm_final_g):
    B, T, D = x.shape
    assert B == 1 and D == D_MODEL and w_in.shape[0] == 1 and T % (4 * MOBA_BLOCK) == 0
    lower_bounds = jnp.cumsum(jax.nn.softmax(hgrn_lb_raw.astype(F32), axis=0), axis=0)
    x2d = x.reshape(T, D)
    o_hgrn, o_moba, gates = _mixers(x2d, norm_mix_g[0], w_in[0], lower_bounds[0], hgrn_norm_g[0])
    merged = _merge(o_hgrn, o_moba, gates, w_up_hgrn[0].astype(BF16), w_up_moba[0].astype(BF16))
    x1, h2 = _outproj(merged, x2d, w_out[0].astype(BF16), norm_ffn_g[0])
    w_router = jnp.pad(jnp.concatenate([w_router_group[0], w_router_expert[0]], axis=1),
                       ((0, 0), (0, V7X_LANES - N_GROUPS - N_EXPERTS))).astype(BF16)
    info, cnt = _router(h2, w_router)
    dest, pad_end, block_expert, n_used, m_pad = _moe_plan(info, cnt, T)
    xs = _dispatch(dest, pad_end, h2, m_pad)
    yb = _experts(block_expert, n_used, xs, w_exp_gate[0], w_exp_up[0], w_exp_down[0])
    out = _combine(dest, x1, info, yb, p[0].reshape(T, PLE_DIM), norm_ple_g[0], w_ple_gate[0].astype(BF16),
                   w_ple_proj[0].astype(BF16), norm_final_g)
    return out.reshape(B, T, D)
```

```python
import functools

import jax
import jax.numpy as jnp
from jax import lax
from jax.experimental import pallas as pl
from jax.experimental.pallas import tpu as pltpu

F32 = jnp.float32
BF16 = jnp.bfloat16
I32 = jnp.int32

D_MODEL = 2048
PLE_DIM = 256
HGRN_HEADS = 8
HGRN_HEAD_DIM = 128
HGRN_WIDTH = HGRN_HEADS * HGRN_HEAD_DIM
MOBA_HEADS = 8
MOBA_HEAD_DIM = 128
MOBA_WIDTH = MOBA_HEADS * MOBA_HEAD_DIM
MOBA_BLOCK = 256
MOBA_TOPK = 3
ROPE_THETA = 10000.0
N_GROUPS = 4
EXPERTS_PER_GROUP = 8
N_EXPERTS = N_GROUPS * EXPERTS_PER_GROUP
TOPK_IN_GROUP = 2
D_EXPERT = 512
EPS = 1e-6
NEG_INF = -1e30

V7X_LANES = 128
V7X_SUBLANES = 8
V7X_VMEM_BUDGET_BYTES = 56 * 1024 * 1024

HGRN_CHUNK = 128
MOE_ROWS = 256


def _cparams(n_grid, vmem_bytes):
    return pltpu.CompilerParams(
        dimension_semantics=("arbitrary",) * n_grid,
        vmem_limit_bytes=int(min(max(vmem_bytes, 16 * 1024 * 1024), V7X_VMEM_BUDGET_BYTES)),
    )


def _nbytes(shape, dtype):
    n = 1
    for s in shape:
        n *= s
    return n * jnp.dtype(dtype).itemsize


def _rms(x, g):
    ms = jnp.mean(x * x, axis=-1, keepdims=True)
    return x * lax.rsqrt(ms + EPS) * g


def _rmsnorm_kernel(x_ref, g_ref, o_ref):
    o_ref[...] = _rms(x_ref[...], g_ref[...]).astype(o_ref.dtype)


def _rmsnorm(x, g, out_dtype, tm=512):
    T, D = x.shape
    return pl.pallas_call(
        _rmsnorm_kernel,
        out_shape=jax.ShapeDtypeStruct((T, D), out_dtype),
        grid=(T // tm,),
        in_specs=[pl.BlockSpec((tm, D), lambda i: (i, 0)), pl.BlockSpec((1, D), lambda i: (0, 0))],
        out_specs=pl.BlockSpec((tm, D), lambda i: (i, 0)),
        compiler_params=_cparams(1, 4 * _nbytes((tm, D), F32)),
        name="rmsnorm",
    )(x, g.reshape(1, D))


def _ep_identity(acc):
    return acc


def _ep_silu(acc):
    return acc * jax.nn.sigmoid(acc)


def _ep_sigmoid(acc):
    return jax.nn.sigmoid(acc)


def _ep_logf(acc, la_ref, lc_ref):
    ls = jnp.minimum(acc, 0.0) - jnp.log(1.0 + jnp.exp(-jnp.abs(acc)))
    u = la_ref[...]
    v = lc_ref[...] + ls
    return jnp.maximum(u, v) + jnp.log(1.0 + jnp.exp(-jnp.abs(u - v)))


def _ep_rope(acc, cos_ref, sin_ref, *, scale):
    cos = cos_ref[...]
    sin = sin_ref[...]
    outs = []
    for hh in range(acc.shape[1] // MOBA_HEAD_DIM):
        a = acc[:, hh * MOBA_HEAD_DIM:(hh + 1) * MOBA_HEAD_DIM]
        r = pltpu.roll(a, MOBA_HEAD_DIM // 2, axis=1)
        outs.append((a * cos + r * sin) * scale)
    return jnp.concatenate(outs, axis=1)


def _ep_rope_aug(acc, cos_ref, sin_ref, oh_ref):
    cos = cos_ref[...]
    sin = sin_ref[...]
    oh = oh_ref[...]
    outs = []
    for hh in range(acc.shape[1] // MOBA_HEAD_DIM):
        a = acc[:, hh * MOBA_HEAD_DIM:(hh + 1) * MOBA_HEAD_DIM]
        outs.append(a * cos + pltpu.roll(a, MOBA_HEAD_DIM // 2, axis=1) * sin)
        outs.append(oh)
    return jnp.concatenate(outs, axis=1)


def _cast_weight_once(w_ref, wb_ref, row_axis):
    @pl.when(pl.program_id(row_axis) == 0)
    def _():
        wb_ref[...] = w_ref[...].astype(wb_ref.dtype)


def _proj_kernel(h_ref, w_ref, *refs, epilogue):
    *extra, o_ref, wb_ref = refs
    _cast_weight_once(w_ref, wb_ref, 1)
    acc = jnp.dot(h_ref[...], wb_ref[...], preferred_element_type=F32)
    o_ref[...] = epilogue(acc, *extra).astype(o_ref.dtype)


def _proj(h, w, col0, ncols, epilogue, out_dtype, row_extras=(), col_extras=(), tm=1024, tn=1024, widen=1):
    T, K = h.shape
    tn = min(tn, ncols)
    tm = min(tm, T)
    cb = col0 // tn
    otn = widen * tn
    in_specs = [
        pl.BlockSpec((tm, K), lambda j, i: (i, 0)),
        pl.BlockSpec((K, tn), lambda j, i: (0, cb + j)),
    ]
    for e in row_extras:
        in_specs.append(pl.BlockSpec((tm, e.shape[1]), lambda j, i: (i, 0)))
    for e in col_extras:
        in_specs.append(pl.BlockSpec((1, tn), lambda j, i: (0, j)))
    vmem = 2 * (_nbytes((tm, K), h.dtype) + _nbytes((K, tn), w.dtype) + _nbytes((tm, otn), out_dtype))
    vmem += 3 * _nbytes((tm, otn), F32) + _nbytes((K, tn), BF16)
    return pl.pallas_call(
        functools.partial(_proj_kernel, epilogue=epilogue),
        out_shape=jax.ShapeDtypeStruct((T, widen * ncols), out_dtype),
        grid=(ncols // tn, T // tm),
        in_specs=in_specs,
        out_specs=pl.BlockSpec((tm, otn), lambda j, i: (i, j)),
        scratch_shapes=[pltpu.VMEM((K, tn), BF16)],
        compiler_params=_cparams(2, vmem),
        name="proj",
    )(h, w, *row_extras, *col_extras)


MOBA_VT_ROWS = MOBA_HEAD_DIM + 16
MOBA_BLOCKS_PER_STEP = 4


def _proj_t_kernel(h_ref, w_ref, *refs, epilogue, ones_rows):
    *extra, o_ref, wb_ref = refs
    BS, HD = MOBA_BLOCK, MOBA_HEAD_DIM
    _cast_weight_once(w_ref, wb_ref, 0)
    acc = epilogue(jnp.dot(h_ref[...], wb_ref[...], preferred_element_type=F32), *extra)
    ones = jnp.ones((ones_rows, BS), F32) if ones_rows else None
    for b in range(acc.shape[0] // BS):
        parts = []
        for hh in range(acc.shape[1] // HD):
            parts.append(acc[b * BS:(b + 1) * BS, hh * HD:(hh + 1) * HD].T)
            if ones_rows:
                parts.append(ones)
        o_ref[b] = jnp.concatenate(parts, axis=0).astype(o_ref.dtype)


def _proj_t(h, w, col0, epilogue, row_extras=(), ones_rows=0, tm=1024):
    T, K = h.shape
    tn = MOBA_WIDTH
    tm = min(tm, T)
    cb = col0 // tn
    rows = MOBA_HEADS * (MOBA_HEAD_DIM + ones_rows)
    in_specs = [pl.BlockSpec((tm, K), lambda i: (i, 0)), pl.BlockSpec((K, tn), lambda i: (0, cb))]
    for e in row_extras:
        in_specs.append(pl.BlockSpec((tm, e.shape[1]), lambda i: (i, 0)))
    vmem = 2 * (_nbytes((tm, K), h.dtype) + _nbytes((K, tn), w.dtype) + _nbytes((tm, 2 * tn), BF16))
    vmem += 4 * _nbytes((tm, tn), F32) + _nbytes((K, tn), BF16)
    return pl.pallas_call(
        functools.partial(_proj_t_kernel, epilogue=epilogue, ones_rows=ones_rows),
        out_shape=jax.ShapeDtypeStruct((T // MOBA_BLOCK, rows, MOBA_BLOCK), BF16),
        grid=(T // tm,),
        in_specs=in_specs,
        out_specs=pl.BlockSpec((tm // MOBA_BLOCK, rows, MOBA_BLOCK), lambda i: (i, 0, 0)),
        scratch_shapes=[pltpu.VMEM((K, tn), BF16)],
        compiler_params=_cparams(1, vmem),
        name="proj_t",
    )(h, w, *row_extras)


def _hgrn_kernel(q_ref, g_ref, v_ref, og_ref, ng_ref, o_ref, st_ref):
    W, HD, NH, C, S = HGRN_WIDTH, HGRN_HEAD_DIM, HGRN_HEADS, HGRN_CHUNK, V7X_SUBLANES
    J = C // S

    @pl.when(pl.program_id(0) == 0)
    def _():
        st_ref[...] = jnp.zeros_like(st_ref)

    def r3(x):
        return x.astype(F32).reshape(J, S, W)

    def sub_bcast(x3, r):
        return jnp.broadcast_to(x3[:, r:r + 1, :], x3.shape)

    g3, q3, v3 = r3(g_ref[...]), r3(q_ref[...]), r3(v_ref[...])
    sub = lax.broadcasted_iota(I32, (1, S, W), 1)

    c3 = g3
    for s in (1, 2, 4):
        c3 = c3 + jnp.where(sub >= s, pltpu.roll(c3, s, axis=1), 0.0)
    run = jnp.zeros((1, 1, W), F32)
    carry = []
    for j in range(J):
        carry.append(run)
        run = run + c3[j:j + 1, S - 1:S, :]
    b3 = c3 + jnp.concatenate(carry, axis=0)
    bC = run

    k3 = 1.0 - jnp.exp(g3)
    qe3 = q3 * jnp.exp(b3)
    ks3 = k3 * jnp.exp(bC - b3)

    levels = [(0, q3, k3)]
    ref1 = jnp.where(sub % 2 == 0, b3, pltpu.roll(b3, 1, axis=1))
    ref2 = jnp.where(sub < 4, sub_bcast(b3, 1), sub_bcast(b3, 5))
    ref4 = sub_bcast(b3, 3)
    for lvl, (ref, upper) in enumerate(((ref1, sub % 2 == 1), (ref2, sub % 4 >= 2), (ref4, sub >= 4)), start=1):
        e = jnp.exp(-jnp.abs(b3 - ref))
        levels.append((lvl, jnp.where(upper, q3 * e, 0.0), jnp.where(upper, 0.0, k3 * e)))
    zero_group = jnp.zeros((1, S, W), F32)
    for lvl, half in enumerate((1, 2, 4, 8), start=4):
        qparts, kparts = [], []
        for j in range(J):
            jr = (j // (2 * half)) * (2 * half) + half - 1
            ref = b3[jr:jr + 1, S - 1:S, :]
            if (j % (2 * half)) >= half:
                qparts.append(q3[j:j + 1] * jnp.exp(b3[j:j + 1] - ref))
                kparts.append(zero_group)
            else:
                qparts.append(zero_group)
                kparts.append(k3[j:j + 1] * jnp.exp(ref - b3[j:j + 1]))
        levels.append((lvl, jnp.concatenate(qparts, axis=0), jnp.concatenate(kparts, axis=0)))

    tr = lax.broadcasted_iota(I32, (C, C), 0)
    tc = lax.broadcasted_iota(I32, (C, C), 1)
    xr = tr ^ tc
    code = jnp.zeros((C, C), I32)
    for lvl in range(1, 8):
        code = jnp.where(xr >= (1 << (lvl - 1)), lvl, code)
    code = jnp.where(tc > tr, -1, code)

    def mat(x3, h):
        return x3.reshape(C, W)[:, h * HD:(h + 1) * HD].astype(BF16)

    nt = (((1,), (1,)), ((), ()))
    tn = (((0,), (0,)), ((), ()))
    ebc = jnp.exp(bC).reshape(1, W)
    ng = ng_ref[...]
    for h in range(NH):
        a_mat = jnp.zeros((C, C), F32)
        for lvl, qr, kr in levels:
            s = lax.dot_general(mat(qr, h), mat(kr, h), nt, preferred_element_type=F32)
            a_mat = jnp.where(code == lvl, s, a_mat)
        vh = mat(v3, h)
        st = st_ref[h]
        o = jnp.dot(a_mat.astype(BF16), vh, preferred_element_type=F32)
        o = o + lax.dot_general(mat(qe3, h), st.astype(BF16), nt, preferred_element_type=F32)
        o = _rms(o, ng) * og_ref[:, h * HD:(h + 1) * HD].astype(F32)
        o_ref[:, h * HD:(h + 1) * HD] = o.astype(o_ref.dtype)
        st_ref[h] = st * ebc[:, h * HD:(h + 1) * HD] + lax.dot_general(
            vh, mat(ks3, h), tn, preferred_element_type=F32)


def _hgrn(q, logf, v, og, norm_g):
    T, W = q.shape
    C = HGRN_CHUNK
    blk = pl.BlockSpec((C, W), lambda c: (c, 0))
    vmem = 64 * _nbytes((C, W), F32)
    return pl.pallas_call(
        _hgrn_kernel,
        out_shape=jax.ShapeDtypeStruct((T, W), BF16),
        grid=(T // C,),
        in_specs=[blk, blk, blk, blk, pl.BlockSpec((1, HGRN_HEAD_DIM), lambda c: (0, 0))],
        out_specs=blk,
        scratch_shapes=[pltpu.VMEM((HGRN_HEADS, HGRN_HEAD_DIM, HGRN_HEAD_DIM), F32)],
        compiler_params=_cparams(1, vmem),
        name="hgrn",
    )(q, logf, v, og, norm_g.reshape(1, HGRN_HEAD_DIM))


def _moba_kernel(qt_ref, k_ref, vt_ref, o_ref, km_ref):
    BS, HD, VR = MOBA_BLOCK, MOBA_HEAD_DIM, MOBA_VT_ROWS
    T = k_ref.shape[0]
    NB = T // BS
    G = o_ref.shape[1] // HD
    cur = pl.program_id(1)

    @pl.when(cur == 0)
    def _():
        for g in range(G):
            kf = k_ref[:, 2 * g * HD:(2 * g + 1) * HD].astype(F32).reshape(NB, BS, HD)
            km_ref[g] = jnp.sum(kf, axis=1) * (1.0 / BS)

    blk = lax.broadcasted_iota(I32, (NB, BS), 0)
    key = lax.broadcasted_iota(I32, (BS, BS), 0)
    qry = lax.broadcasted_iota(I32, (BS, BS), 1)
    pad = jnp.zeros((V7X_LANES - NB, BS), F32)
    r0 = pl.multiple_of(cur * BS, BS)
    qts = [qt_ref[0, g * HD:(g + 1) * HD, :] for g in range(G)]
    gts = [jnp.dot(km_ref[g].astype(BF16), qts[g], preferred_element_type=F32) for g in range(G)]
    s_own = [jnp.dot(k_ref[pl.ds(r0, BS), 2 * g * HD:(2 * g + 1) * HD], qts[g], preferred_element_type=F32)
             for g in range(G)]
    qcs = []
    for g in range(G):
        gt = jnp.where(blk < cur, gts[g], NEG_INF)
        sel = jnp.zeros((NB, BS), jnp.bool_)
        for _ in range(MOBA_TOPK):
            mx = jnp.max(gt, axis=0, keepdims=True)
            idx = jnp.min(jnp.where(gt == mx, blk, NB), axis=0, keepdims=True)
            pick = (blk == idx) & (mx > 0.5 * NEG_INF)
            sel = sel | pick
            gt = jnp.where(pick, NEG_INF, gt)
        pen = jnp.concatenate([jnp.where(sel, 0.0, NEG_INF), pad], axis=0).astype(BF16)
        qcs.append(jnp.concatenate([qts[g], pen], axis=0))

    def scores(n):
        rn = pl.multiple_of(n * BS, BS)
        return tuple(jnp.dot(k_ref[pl.ds(rn, BS), 2 * g * HD:(2 * g + 2) * HD], qcs[g], preferred_element_type=F32)
                     for g in range(G))

    ms, ps = [], []
    for g in range(G):
        s = jnp.where(key <= qry, s_own[g], NEG_INF)
        m0 = jnp.max(s, axis=0, keepdims=True)
        ms.append(m0)
        ps.append(jnp.exp(s - m0).astype(BF16))
    accs = [jnp.dot(vt_ref[cur, g * VR:(g + 1) * VR, :], ps[g], preferred_element_type=F32)
            for g in range(G)]

    KB = MOBA_BLOCKS_PER_STEP

    def body(c, carry):
        ms, accs = carry
        r = pl.multiple_of(c * KB * BS, KB * BS)
        sns = [jnp.dot(k_ref[pl.ds(r, KB * BS), 2 * g * HD:(2 * g + 2) * HD], qcs[g], preferred_element_type=F32)
               for g in range(G)]
        new_ms, alphas, pns = [], [], []
        for g in range(G):
            m_new = jnp.maximum(ms[g], jnp.max(sns[g], axis=0, keepdims=True))
            alphas.append(jnp.exp(ms[g] - m_new))
            pns.append(jnp.exp(sns[g] - m_new).astype(BF16))
            new_ms.append(m_new)
        new_accs = []
        for g in range(G):
            pv = alphas[g] * accs[g]
            for j in range(KB):
                pv = pv + jnp.dot(vt_ref[KB * c + j, g * VR:(g + 1) * VR, :], pns[g][j * BS:(j + 1) * BS],
                                  preferred_element_type=F32)
            new_accs.append(pv)
        return tuple(new_ms), tuple(new_accs)

    _, accs = lax.fori_loop(0, (cur + KB - 1) // KB, body, (tuple(ms), tuple(accs)))
    for g in range(G):
        ot = accs[g][:HD, :] / accs[g][HD:HD + 1, :]
        o_ref[:, g * HD:(g + 1) * HD] = ot.T.astype(o_ref.dtype)


def _moba(mqt, mk_aug, mvt, heads_per_step=4):
    T = mk_aug.shape[0]
    BS, HD, G, VR = MOBA_BLOCK, MOBA_HEAD_DIM, heads_per_step, MOBA_VT_ROWS
    NB = T // BS
    vmem = _nbytes((T, 2 * G * HD), BF16) + _nbytes((NB, G * VR, BS), BF16) + 8 * _nbytes((BS, G * HD), BF16)
    vmem += 16 * G * _nbytes((BS, BS), F32)
    resident = pl.Buffered(1)
    return pl.pallas_call(
        _moba_kernel,
        out_shape=jax.ShapeDtypeStruct((T, MOBA_WIDTH), BF16),
        grid=(MOBA_HEADS // G, NB),
        in_specs=[
            pl.BlockSpec((1, G * HD, BS), lambda h, i: (i, h, 0)),
            pl.BlockSpec((T, 2 * G * HD), lambda h, i: (0, h), pipeline_mode=resident),
            pl.BlockSpec((NB, G * VR, BS), lambda h, i: (0, h, 0), pipeline_mode=resident),
        ],
        out_specs=pl.BlockSpec((BS, G * HD), lambda h, i: (i, h)),
        scratch_shapes=[pltpu.VMEM((G, NB, HD), F32)],
        compiler_params=_cparams(2, vmem),
        name="moba",
    )(mqt, mk_aug, mvt)


def _merge_kernel(oh_ref, om_ref, ga_ref, gb_ref, wh_ref, wm_ref, o_ref):
    a = jnp.dot(oh_ref[...], wh_ref[...], preferred_element_type=F32)
    b = jnp.dot(om_ref[...], wm_ref[...], preferred_element_type=F32)
    o_ref[...] = (ga_ref[...].astype(F32) * a + gb_ref[...].astype(F32) * b).astype(o_ref.dtype)


def _merge(o_hgrn, o_moba, gates, w_up_hgrn, w_up_moba, tm=512):
    T = o_hgrn.shape[0]
    D = D_MODEL
    vmem = 2 * (_nbytes((tm, HGRN_WIDTH), F32) + _nbytes((tm, MOBA_WIDTH), BF16) + 3 * _nbytes((tm, D), BF16)
                + 2 * _nbytes((HGRN_WIDTH, D), BF16)) + 3 * _nbytes((tm, D), F32)
    return pl.pallas_call(
        _merge_kernel,
        out_shape=jax.ShapeDtypeStruct((T, D), BF16),
        grid=(T // tm,),
        in_specs=[
            pl.BlockSpec((tm, HGRN_WIDTH), lambda i: (i, 0)),
            pl.BlockSpec((tm, MOBA_WIDTH), lambda i: (i, 0)),
            pl.BlockSpec((tm, D), lambda i: (i, 0)),
            pl.BlockSpec((tm, D), lambda i: (i, 1)),
            pl.BlockSpec((HGRN_WIDTH, D), lambda i: (0, 0)),
            pl.BlockSpec((MOBA_WIDTH, D), lambda i: (0, 0)),
        ],
        out_specs=pl.BlockSpec((tm, D), lambda i: (i, 0)),
        compiler_params=_cparams(1, vmem),
        name="merge",
    )(o_hgrn, o_moba, gates, gates, w_up_hgrn, w_up_moba)


def _pack_bf16_pairs(x):
    C = x.shape[1] // 2
    b = lax.bitcast_convert_type(x, jnp.uint32)
    r = (b + jnp.uint32(0x7FFF) + ((b >> 16) & jnp.uint32(1))) >> 16
    return r[:, :C] | (r[:, C:] << 16)


def _unpack_bf16_pairs(p):
    lo = lax.bitcast_convert_type(p << 16, F32)
    hi = lax.bitcast_convert_type(p & jnp.uint32(0xFFFF0000), F32)
    return jnp.concatenate([lo, hi], axis=1)


def _outproj_kernel(m_ref, x_ref, w_ref, g_ref, wr_ref, x1_ref, h2p_ref, info_ref, cnt_ref, carry_ref):
    @pl.when(pl.program_id(0) == 0)
    def _():
        carry_ref[...] = jnp.zeros_like(carry_ref)

    x1 = x_ref[...] + jnp.dot(m_ref[...], w_ref[...], preferred_element_type=F32)
    x1_ref[...] = x1
    h2 = _rms(x1, g_ref[...])
    h2p_ref[...] = _pack_bf16_pairs(h2)
    info_ref[...] = _route(h2, wr_ref, carry_ref)
    cnt_ref[...] = carry_ref[...]


def _outproj(merged, x, w_out, g_ffn, w_router, tm=256):
    T, D = x.shape
    vmem = 2 * (_nbytes((tm, D), BF16) + 3 * _nbytes((tm, D), F32) + _nbytes((D, D), BF16)) + 6 * _nbytes((tm, D), F32)
    row = lambda i: (i, 0)
    fixed = lambda i: (0, 0)
    return pl.pallas_call(
        _outproj_kernel,
        out_shape=(jax.ShapeDtypeStruct((T, D), F32), jax.ShapeDtypeStruct((T, D // 2), jnp.uint32),
                   jax.ShapeDtypeStruct((T, V7X_LANES), F32), jax.ShapeDtypeStruct((1, V7X_LANES), F32)),
        grid=(T // tm,),
        in_specs=[
            pl.BlockSpec((tm, D), row),
            pl.BlockSpec((tm, D), row),
            pl.BlockSpec((D, D), fixed),
            pl.BlockSpec((1, D), fixed),
            pl.BlockSpec((D, V7X_LANES), fixed),
        ],
        out_specs=(pl.BlockSpec((tm, D), row), pl.BlockSpec((tm, D // 2), row), pl.BlockSpec((tm, V7X_LANES), row),
                   pl.BlockSpec((1, V7X_LANES), fixed)),
        scratch_shapes=[pltpu.VMEM((1, V7X_LANES), F32)],
        compiler_params=_cparams(1, vmem),
        name="outproj",
    )(merged, x, w_out, g_ffn.reshape(1, D), w_router)


def _route(h2, w_ref, carry_ref):
    tm = h2.shape[0]
    logits = jnp.dot(h2.astype(BF16), w_ref[...], preferred_element_type=F32)
    lane = lax.broadcasted_iota(I32, (tm, V7X_LANES), 1)
    is_g = lane < N_GROUPS
    gl = jnp.where(is_g, logits, NEG_INF)
    gmax = jnp.max(gl, axis=1, keepdims=True)
    g_sel = jnp.min(jnp.where(gl == gmax, lane, V7X_LANES), axis=1, keepdims=True)
    gsum = jnp.sum(jnp.where(is_g, jnp.exp(gl - gmax), 0.0), axis=1, keepdims=True)
    p_group = 1.0 / gsum
    lo = N_GROUPS + EXPERTS_PER_GROUP * g_sel
    emask = (lane >= lo) & (lane < lo + EXPERTS_PER_GROUP)
    el = jnp.where(emask, logits, NEG_INF)
    e1 = jnp.max(el, axis=1, keepdims=True)
    i1 = jnp.min(jnp.where((el == e1) & emask, lane, V7X_LANES), axis=1, keepdims=True)
    emask2 = emask & (lane != i1)
    el2 = jnp.where(emask2, logits, NEG_INF)
    e2 = jnp.max(el2, axis=1, keepdims=True)
    i2 = jnp.min(jnp.where((el2 == e2) & emask2, lane, V7X_LANES), axis=1, keepdims=True)
    r = jnp.exp(e2 - e1)
    w1 = p_group / (1.0 + r)
    w2 = p_group * r / (1.0 + r)
    eid1 = i1 - N_GROUPS
    eid2 = i2 - N_GROUPS
    oh1 = jnp.where(lane == eid1, 1.0, 0.0)
    oh2 = jnp.where(lane == eid2, 1.0, 0.0)
    cnt = oh1 + oh2
    tri = jnp.where(lax.broadcasted_iota(I32, (tm, tm), 0) > lax.broadcasted_iota(I32, (tm, tm), 1), 1.0, 0.0)
    before = jnp.dot(tri.astype(BF16), cnt.astype(BF16), preferred_element_type=F32) + carry_ref[...]
    rank1 = jnp.sum(oh1 * before, axis=1, keepdims=True)
    rank2 = jnp.sum(oh2 * before, axis=1, keepdims=True)
    carry_ref[...] = carry_ref[...] + jnp.sum(cnt, axis=0, keepdims=True)
    info = jnp.zeros((tm, V7X_LANES), F32)
    for k, val in enumerate((eid1.astype(F32), eid2.astype(F32), w1, w2, rank1, rank2)):
        info = jnp.where(lane == k, val, info)
    return info


def _row_copy(src_ref, src_row, dst_ref, dst_row, sem):
    return pltpu.make_async_copy(src_ref.at[pl.ds(src_row, 1), :], dst_ref.at[pl.ds(dst_row, 1), :], sem)


ROW_DMA_PRIORITY = 1


ROW_DMA_UNROLL = 8


def _dispatch_kernel(dest_ref, pend_ref, h_ref, xs_ref, zero_ref, sem):
    tm = h_ref.shape[0]
    step = pl.program_id(0)
    base = step * tm

    @pl.when(step == 0)
    def _():
        zero_ref[...] = jnp.zeros_like(zero_ref)

        def tail(e):
            return pltpu.make_async_copy(
                zero_ref, xs_ref.at[pl.ds(pl.multiple_of(pend_ref[e] - MOE_ROWS, MOE_ROWS), MOE_ROWS), :], sem)

        def nonempty(e):
            return pend_ref[e] > (pend_ref[e - 1] if e else 0)

        def unused(b):
            return pltpu.make_async_copy(
                zero_ref, xs_ref.at[pl.ds(pl.multiple_of(b * MOE_ROWS, MOE_ROWS), MOE_ROWS), :], sem)

        first_unused = pend_ref[N_EXPERTS - 1] // MOE_ROWS
        n_blocks = xs_ref.shape[0] // MOE_ROWS
        for e in range(N_EXPERTS):
            pl.when(nonempty(e))(lambda e=e: tail(e).start())
        lax.fori_loop(first_unused, n_blocks, lambda b, c: (unused(b).start(), c)[1], 0)
        for e in range(N_EXPERTS):
            pl.when(nonempty(e))(lambda e=e: tail(e).wait())
        lax.fori_loop(first_unused, n_blocks, lambda b, c: (unused(b).wait(), c)[1], 0)

    def issue(rb, c):
        for u in range(ROW_DMA_UNROLL):
            r = rb * ROW_DMA_UNROLL + u
            for k in range(TOPK_IN_GROUP):
                _row_copy(h_ref, r, xs_ref, dest_ref[(base + r) * TOPK_IN_GROUP + k], sem).start()
        return c

    lax.fori_loop(0, tm // ROW_DMA_UNROLL, issue, 0)
    for _ in range(tm * TOPK_IN_GROUP):
        _row_copy(h_ref, 0, xs_ref, 0, sem).wait()


def _dispatch(dest_flat, pad_end, h2p, m_pad, tm=256):
    T, C = h2p.shape
    return pl.pallas_call(
        _dispatch_kernel,
        out_shape=jax.ShapeDtypeStruct((m_pad, C), h2p.dtype),
        grid_spec=pltpu.PrefetchScalarGridSpec(
            num_scalar_prefetch=2,
            grid=(T // tm,),
            in_specs=[pl.BlockSpec((tm, C), lambda i, d, pe: (i, 0))],
            out_specs=pl.BlockSpec(memory_space=pl.ANY),
            scratch_shapes=[pltpu.VMEM((MOE_ROWS, C), h2p.dtype), pltpu.SemaphoreType.DMA(())],
        ),
        compiler_params=_cparams(1, 6 * _nbytes((tm, C), h2p.dtype)),
        name="dispatch",
    )(dest_flat, pad_end, h2p)


def _expert_kernel(be_ref, nu_ref, xs_ref, wg_ref, wu_ref, wd_ref, y_ref, wgb_ref, wub_ref, wdb_ref):
    b = pl.program_id(0)
    used = b < nu_ref[0]

    @pl.when(used & ((b == 0) | (be_ref[b] != be_ref[jnp.maximum(b - 1, 0)])))
    def _():
        wgb_ref[...] = wg_ref[0].astype(BF16)
        wub_ref[...] = wu_ref[0].astype(BF16)
        wdb_ref[...] = wd_ref[0].astype(BF16)

    @pl.when(used)
    def _():
        x = _unpack_bf16_pairs(xs_ref[...]).astype(BF16)
        a = jnp.dot(x, wgb_ref[...], preferred_element_type=F32)
        u = jnp.dot(x, wub_ref[...], preferred_element_type=F32)
        hm = (a * jax.nn.sigmoid(a) * u).astype(BF16)
        y_ref[...] = _pack_bf16_pairs(jnp.dot(hm, wdb_ref[...], preferred_element_type=F32))

    @pl.when(jnp.logical_not(used))
    def _():
        y_ref[...] = jnp.zeros_like(y_ref)


def _experts(block_expert, n_used, xs, w_gate, w_up, w_down):
    m_pad, C = xs.shape
    D = 2 * C
    R, Fd = MOE_ROWS, D_EXPERT
    vmem = 2 * (2 * _nbytes((R, C), xs.dtype) + 3 * _nbytes((D, Fd), F32)) + 3 * _nbytes((D, Fd), BF16)
    vmem += 6 * _nbytes((R, D), F32)
    return pl.pallas_call(
        _expert_kernel,
        out_shape=jax.ShapeDtypeStruct((m_pad, C), xs.dtype),
        grid_spec=pltpu.PrefetchScalarGridSpec(
            num_scalar_prefetch=2,
            grid=(m_pad // R,),
            in_specs=[
                pl.BlockSpec((R, C), lambda b, be, nu: (jnp.minimum(b, jnp.maximum(nu[0] - 1, 0)), 0)),
                pl.BlockSpec((1, D, Fd), lambda b, be, nu: (be[b], 0, 0)),
                pl.BlockSpec((1, D, Fd), lambda b, be, nu: (be[b], 0, 0)),
                pl.BlockSpec((1, Fd, D), lambda b, be, nu: (be[b], 0, 0)),
            ],
            out_specs=pl.BlockSpec((R, C), lambda b, be, nu: (b, 0)),
            scratch_shapes=[pltpu.VMEM((D, Fd), BF16), pltpu.VMEM((D, Fd), BF16), pltpu.VMEM((Fd, D), BF16)],
        ),
        compiler_params=_cparams(1, vmem),
        name="experts",
    )(block_expert, n_used, xs, w_gate, w_up, w_down)


def _combine_kernel(dest_ref, x1_ref, info_ref, yb_ref, p_ref, gp_ref, wpg_ref, wpp_ref, gf_ref, o_ref, ybuf, sem):
    tm = x1_ref.shape[0]
    step = pl.program_id(0)
    n_tiles = pl.num_programs(0) - 2
    K = TOPK_IN_GROUP

    def fetch(tile):
        slot = tile % 2
        for r in range(tm):
            for k in range(K):
                _row_copy(yb_ref, dest_ref[(tile * tm + r) * K + k], ybuf.at[slot * K + k], r,
                          sem.at[slot]).start(priority=ROW_DMA_PRIORITY)

    def drain(tile):
        slot = tile % 2
        for _ in range(tm * K):
            _row_copy(yb_ref, 0, ybuf.at[0], 0, sem.at[slot]).wait()

    @pl.when(step == 0)
    def _():
        fetch(step)

    @pl.when(step > 0)
    def _():
        drain(step - 1)

    @pl.when((step > 0) & (step <= n_tiles))
    def _():
        tile = step - 1
        fetch(step)
        slot = tile % 2
        info = info_ref[...]
        x2 = (x1_ref[...] + info[:, 2:3] * _unpack_bf16_pairs(ybuf[slot * K])
              + info[:, 3:4] * _unpack_bf16_pairs(ybuf[slot * K + 1]))
        hp = _rms(x2, gp_ref[...]).astype(BF16)
        z = jnp.dot(hp, wpg_ref[...], preferred_element_type=F32)
        pp = jnp.dot(p_ref[...].astype(BF16), wpp_ref[...], preferred_element_type=F32)
        x3 = x2 + jax.nn.sigmoid(z) * pp
        o_ref[...] = _rms(x3, gf_ref[...])


def _combine(dest_flat, x1, info, yb, p, g_ple, w_ple_gate, w_ple_proj, g_final, tm=256):
    T, D = x1.shape
    n_tiles = T // tm
    dest_padded = jnp.concatenate([dest_flat, jnp.zeros((tm * TOPK_IN_GROUP,), I32)])
    vmem = 2 * (2 * _nbytes((tm, D), F32) + _nbytes((D, D), BF16) + _nbytes((PLE_DIM, D), BF16)
                + _nbytes((tm, PLE_DIM), F32)) + 10 * _nbytes((tm, D), F32)
    tile = lambda i, d: (jnp.clip(i - 1, 0, n_tiles - 1), 0)
    return pl.pallas_call(
        _combine_kernel,
        out_shape=jax.ShapeDtypeStruct((T, D), F32),
        grid_spec=pltpu.PrefetchScalarGridSpec(
            num_scalar_prefetch=1,
            grid=(n_tiles + 2,),
            in_specs=[
                pl.BlockSpec((tm, D), tile),
                pl.BlockSpec((tm, V7X_LANES), tile),
                pl.BlockSpec(memory_space=pl.ANY),
                pl.BlockSpec((tm, PLE_DIM), tile),
                pl.BlockSpec((1, D), lambda i, d: (0, 0)),
                pl.BlockSpec((D, D), lambda i, d: (0, 0)),
                pl.BlockSpec((PLE_DIM, D), lambda i, d: (0, 0)),
                pl.BlockSpec((1, D), lambda i, d: (0, 0)),
            ],
            out_specs=pl.BlockSpec((tm, D), tile),
            scratch_shapes=[pltpu.VMEM((2 * TOPK_IN_GROUP, tm, yb.shape[1]), yb.dtype),
                            pltpu.SemaphoreType.DMA((2,))],
        ),
        compiler_params=_cparams(1, vmem),
        name="combine",
    )(dest_padded, x1, info, yb, p, g_ple.reshape(1, D), w_ple_gate, w_ple_proj, g_final.reshape(1, D))


def _rope_tables(T):
    half, BS = MOBA_HEAD_DIM // 2, MOBA_BLOCK
    inv_freq = ROPE_THETA ** (-jnp.arange(half, dtype=F32) / half)
    ang_a = (jnp.arange(T // BS, dtype=F32) * BS)[:, None, None] * inv_freq
    ang_b = jnp.arange(BS, dtype=F32)[None, :, None] * inv_freq
    ca, sa, cb, sb = jnp.cos(ang_a), jnp.sin(ang_a), jnp.cos(ang_b), jnp.sin(ang_b)
    cos = (ca * cb - sa * sb).reshape(T, half)
    sin = (sa * cb + ca * sb).reshape(T, half)
    return jnp.concatenate([cos, cos], axis=1), jnp.concatenate([-sin, sin], axis=1)


def _mixers(x2d, g_mix, w_in, lb, hgrn_norm_g):
    T = x2d.shape[0]
    W = HGRN_WIDTH
    h = _rmsnorm(x2d, g_mix, BF16)
    w = w_in
    cos, sin = _rope_tables(T)
    log_lb = jnp.log(lb).reshape(1, W)
    log_1m = jnp.log1p(-lb).reshape(1, W)
    hq = _proj(h, w, 0 * W, W, _ep_silu, BF16)
    logf = _proj(h, w, 1 * W, W, _ep_logf, F32, col_extras=(log_lb, log_1m))
    hi = _proj(h, w, 2 * W, W, _ep_identity, BF16)
    hog = _proj(h, w, 3 * W, W, _ep_silu, BF16)
    scale = MOBA_HEAD_DIM ** -0.5
    mqt = _proj_t(h, w, 4 * W, functools.partial(_ep_rope, scale=scale), row_extras=(cos, sin))
    blk = jnp.arange(T, dtype=I32)[:, None] // MOBA_BLOCK
    blk_onehot = (blk == jnp.arange(V7X_LANES, dtype=I32)[None, :]).astype(F32)
    mk = _proj(h, w, 5 * W, W, _ep_rope_aug, BF16, row_extras=(cos, sin, blk_onehot), widen=2)
    mvt = _proj_t(h, w, 6 * W, _ep_identity, ones_rows=MOBA_VT_ROWS - MOBA_HEAD_DIM)
    gates = _proj(h, w, 7 * W, 2 * D_MODEL, _ep_sigmoid, BF16)
    o_hgrn = _hgrn(hq, logf, hi, hog, hgrn_norm_g)
    o_moba = _moba(mqt, mk, mvt)
    return o_hgrn, o_moba, gates


def _moe_plan(info, cnt, T):
    R = MOE_ROWS
    eid = info[:, 0:TOPK_IN_GROUP].astype(I32)
    rank = info[:, 4:4 + TOPK_IN_GROUP].astype(I32)
    counts = cnt[0, :N_EXPERTS].astype(I32)
    padded = (counts + R - 1) // R * R
    pad_end = jnp.cumsum(padded)
    pad_start = pad_end - padded
    dest = (pad_start[eid] + rank).reshape(-1)
    n_blocks = (T * TOPK_IN_GROUP) // R + N_EXPERTS
    block_expert = jnp.minimum(
        jnp.searchsorted(pad_end, jnp.arange(n_blocks, dtype=I32) * R, side="right"), N_EXPERTS - 1).astype(I32)
    n_used = (pad_end[-1:] // R).astype(I32)
    return dest, pad_end.astype(I32), block_expert, n_used, n_blocks * R


def kernel(x, p, norm_mix_g, w_in, hgrn_lb_raw, hgrn_norm_g, w_up_hgrn, w_up_moba, w_out, norm_ffn_g,
           w_router_group, w_router_expert, w_exp_gate, w_exp_up, w_exp_down, norm_ple_g, w_ple_gate,
           w_ple_proj, norm_final_g):
    B, T, D = x.shape
    assert B == 1 and D == D_MODEL and w_in.shape[0] == 1 and T % (4 * MOBA_BLOCK) == 0
    lower_bounds = jnp.cumsum(jax.nn.softmax(hgrn_lb_raw.astype(F32), axis=0), axis=0)
    x2d = x.reshape(T, D)
    o_hgrn, o_moba, gates = _mixers(x2d, norm_mix_g[0], w_in[0], lower_bounds[0], hgrn_norm_g[0])
    merged = _merge(o_hgrn, o_moba, gates, w_up_hgrn[0].astype(BF16), w_up_moba[0].astype(BF16))
    w_router = jnp.pad(jnp.concatenate([w_router_group[0], w_router_expert[0]], axis=1),
                       ((0, 0), (0, V7X_LANES - N_GROUPS - N_EXPERTS))).astype(BF16)
    x1, h2p, info, cnt = _outproj(merged, x2d, w_out[0].astype(BF16), norm_ffn_g[0], w_router)
    dest, pad_end, block_expert, n_used, m_pad = _moe_plan(info, cnt, T)
    xs = _dispatch(dest, pad_end, h2p, m_pad)
    yb = _experts(block_expert, n_used, xs, w_exp_gate[0], w_exp_up[0], w_exp_down[0])
    out = _combine(dest, x1, info, yb, p[0].reshape(T, PLE_DIM), norm_ple_g[0], w_ple_gate[0].astype(BF16),
                   w_ple_proj[0].astype(BF16), norm_final_g)
    return out.reshape(B, T, D)
```

```python
import functools

import jax
import jax.numpy as jnp
from jax import lax
from jax.experimental import pallas as pl
from jax.experimental.pallas import tpu as pltpu

F32 = jnp.float32
BF16 = jnp.bfloat16
I32 = jnp.int32

D_MODEL = 2048
PLE_DIM = 256
HGRN_HEADS = 8
HGRN_HEAD_DIM = 128
HGRN_WIDTH = HGRN_HEADS * HGRN_HEAD_DIM
MOBA_HEADS = 8
MOBA_HEAD_DIM = 128
MOBA_WIDTH = MOBA_HEADS * MOBA_HEAD_DIM
MOBA_BLOCK = 256
MOBA_TOPK = 3
ROPE_THETA = 10000.0
N_GROUPS = 4
EXPERTS_PER_GROUP = 8
N_EXPERTS = N_GROUPS * EXPERTS_PER_GROUP
TOPK_IN_GROUP = 2
D_EXPERT = 512
EPS = 1e-6
NEG_INF = -1e30

V7X_LANES = 128
V7X_SUBLANES = 8
V7X_VMEM_BUDGET_BYTES = 56 * 1024 * 1024

HGRN_CHUNK = 128
MOE_ROWS = 256


def _cparams(n_grid, vmem_bytes):
    return pltpu.CompilerParams(
        dimension_semantics=("arbitrary",) * n_grid,
        vmem_limit_bytes=int(min(max(vmem_bytes, 16 * 1024 * 1024), V7X_VMEM_BUDGET_BYTES)),
    )


def _nbytes(shape, dtype):
    n = 1
    for s in shape:
        n *= s
    return n * jnp.dtype(dtype).itemsize


def _rms(x, g):
    ms = jnp.mean(x * x, axis=-1, keepdims=True)
    return x * lax.rsqrt(ms + EPS) * g


def _rmsnorm_kernel(x_ref, g_ref, o_ref):
    o_ref[...] = _rms(x_ref[...], g_ref[...]).astype(o_ref.dtype)


def _rmsnorm(x, g, out_dtype, tm=512):
    T, D = x.shape
    return pl.pallas_call(
        _rmsnorm_kernel,
        out_shape=jax.ShapeDtypeStruct((T, D), out_dtype),
        grid=(T // tm,),
        in_specs=[pl.BlockSpec((tm, D), lambda i: (i, 0)), pl.BlockSpec((1, D), lambda i: (0, 0))],
        out_specs=pl.BlockSpec((tm, D), lambda i: (i, 0)),
        compiler_params=_cparams(1, 4 * _nbytes((tm, D), F32)),
        name="rmsnorm",
    )(x, g.reshape(1, D))


def _ep_identity(acc):
    return acc


def _ep_silu(acc):
    return acc * jax.nn.sigmoid(acc)


def _ep_sigmoid(acc):
    return jax.nn.sigmoid(acc)


def _ep_logf(acc, la_ref, lc_ref):
    ls = jnp.minimum(acc, 0.0) - jnp.log(1.0 + jnp.exp(-jnp.abs(acc)))
    u = la_ref[...]
    v = lc_ref[...] + ls
    return jnp.maximum(u, v) + jnp.log(1.0 + jnp.exp(-jnp.abs(u - v)))


def _ep_rope(acc, cos_ref, sin_ref, *, scale):
    cos = cos_ref[...]
    sin = sin_ref[...]
    outs = []
    for hh in range(acc.shape[1] // MOBA_HEAD_DIM):
        a = acc[:, hh * MOBA_HEAD_DIM:(hh + 1) * MOBA_HEAD_DIM]
        r = pltpu.roll(a, MOBA_HEAD_DIM // 2, axis=1)
        outs.append((a * cos + r * sin) * scale)
    return jnp.concatenate(outs, axis=1)


def _ep_rope_aug(acc, cos_ref, sin_ref, oh_ref):
    cos = cos_ref[...]
    sin = sin_ref[...]
    oh = oh_ref[...]
    outs = []
    for hh in range(acc.shape[1] // MOBA_HEAD_DIM):
        a = acc[:, hh * MOBA_HEAD_DIM:(hh + 1) * MOBA_HEAD_DIM]
        outs.append(a * cos + pltpu.roll(a, MOBA_HEAD_DIM // 2, axis=1) * sin)
        outs.append(oh)
    return jnp.concatenate(outs, axis=1)


def _cast_weight_once(w_ref, wb_ref, row_axis):
    @pl.when(pl.program_id(row_axis) == 0)
    def _():
        wb_ref[...] = w_ref[...].astype(wb_ref.dtype)


def _proj_kernel(h_ref, w_ref, *refs, epilogue):
    *extra, o_ref, wb_ref = refs
    _cast_weight_once(w_ref, wb_ref, 1)
    acc = jnp.dot(h_ref[...], wb_ref[...], preferred_element_type=F32)
    o_ref[...] = epilogue(acc, *extra).astype(o_ref.dtype)


def _proj(h, w, col0, ncols, epilogue, out_dtype, row_extras=(), col_extras=(), tm=1024, tn=1024, widen=1):
    T, K = h.shape
    tn = min(tn, ncols)
    tm = min(tm, T)
    cb = col0 // tn
    otn = widen * tn
    in_specs = [
        pl.BlockSpec((tm, K), lambda j, i: (i, 0)),
        pl.BlockSpec((K, tn), lambda j, i: (0, cb + j)),
    ]
    for e in row_extras:
        in_specs.append(pl.BlockSpec((tm, e.shape[1]), lambda j, i: (i, 0)))
    for e in col_extras:
        in_specs.append(pl.BlockSpec((1, tn), lambda j, i: (0, j)))
    vmem = 2 * (_nbytes((tm, K), h.dtype) + _nbytes((K, tn), w.dtype) + _nbytes((tm, otn), out_dtype))
    vmem += 3 * _nbytes((tm, otn), F32) + _nbytes((K, tn), BF16)
    return pl.pallas_call(
        functools.partial(_proj_kernel, epilogue=epilogue),
        out_shape=jax.ShapeDtypeStruct((T, widen * ncols), out_dtype),
        grid=(ncols // tn, T // tm),
        in_specs=in_specs,
        out_specs=pl.BlockSpec((tm, otn), lambda j, i: (i, j)),
        scratch_shapes=[pltpu.VMEM((K, tn), BF16)],
        compiler_params=_cparams(2, vmem),
        name="proj",
    )(h, w, *row_extras, *col_extras)


MOBA_VT_ROWS = MOBA_HEAD_DIM + 16
MOBA_BLOCKS_PER_STEP = 4


def _proj_t_kernel(h_ref, w_ref, *refs, epilogue, ones_rows):
    *extra, o_ref, wb_ref = refs
    BS, HD = MOBA_BLOCK, MOBA_HEAD_DIM
    _cast_weight_once(w_ref, wb_ref, 0)
    acc = epilogue(jnp.dot(h_ref[...], wb_ref[...], preferred_element_type=F32), *extra)
    ones = jnp.ones((ones_rows, BS), F32) if ones_rows else None
    for b in range(acc.shape[0] // BS):
        parts = []
        for hh in range(acc.shape[1] // HD):
            parts.append(acc[b * BS:(b + 1) * BS, hh * HD:(hh + 1) * HD].T)
            if ones_rows:
                parts.append(ones)
        o_ref[b] = jnp.concatenate(parts, axis=0).astype(o_ref.dtype)


def _proj_t(h, w, col0, epilogue, row_extras=(), ones_rows=0, tm=1024):
    T, K = h.shape
    tn = MOBA_WIDTH
    tm = min(tm, T)
    cb = col0 // tn
    rows = MOBA_HEADS * (MOBA_HEAD_DIM + ones_rows)
    in_specs = [pl.BlockSpec((tm, K), lambda i: (i, 0)), pl.BlockSpec((K, tn), lambda i: (0, cb))]
    for e in row_extras:
        in_specs.append(pl.BlockSpec((tm, e.shape[1]), lambda i: (i, 0)))
    vmem = 2 * (_nbytes((tm, K), h.dtype) + _nbytes((K, tn), w.dtype) + _nbytes((tm, 2 * tn), BF16))
    vmem += 4 * _nbytes((tm, tn), F32) + _nbytes((K, tn), BF16)
    return pl.pallas_call(
        functools.partial(_proj_t_kernel, epilogue=epilogue, ones_rows=ones_rows),
        out_shape=jax.ShapeDtypeStruct((T // MOBA_BLOCK, rows, MOBA_BLOCK), BF16),
        grid=(T // tm,),
        in_specs=in_specs,
        out_specs=pl.BlockSpec((tm // MOBA_BLOCK, rows, MOBA_BLOCK), lambda i: (i, 0, 0)),
        scratch_shapes=[pltpu.VMEM((K, tn), BF16)],
        compiler_params=_cparams(1, vmem),
        name="proj_t",
    )(h, w, *row_extras)


def _hgrn_kernel(q_ref, g_ref, v_ref, og_ref, ng_ref, o_ref, st_ref):
    W, HD, NH, C, S = HGRN_WIDTH, HGRN_HEAD_DIM, HGRN_HEADS, HGRN_CHUNK, V7X_SUBLANES
    J = C // S

    @pl.when(pl.program_id(0) == 0)
    def _():
        st_ref[...] = jnp.zeros_like(st_ref)

    def r3(x):
        return x.astype(F32).reshape(J, S, W)

    def sub_bcast(x3, r):
        return jnp.broadcast_to(x3[:, r:r + 1, :], x3.shape)

    g3, q3, v3 = r3(g_ref[...]), r3(q_ref[...]), r3(v_ref[...])
    sub = lax.broadcasted_iota(I32, (1, S, W), 1)

    c3 = g3
    for s in (1, 2, 4):
        c3 = c3 + jnp.where(sub >= s, pltpu.roll(c3, s, axis=1), 0.0)
    run = jnp.zeros((1, 1, W), F32)
    carry = []
    for j in range(J):
        carry.append(run)
        run = run + c3[j:j + 1, S - 1:S, :]
    b3 = c3 + jnp.concatenate(carry, axis=0)
    bC = run

    k3 = 1.0 - jnp.exp(g3)
    qe3 = q3 * jnp.exp(b3)
    ks3 = k3 * jnp.exp(bC - b3)

    levels = [(0, q3, k3)]
    ref1 = jnp.where(sub % 2 == 0, b3, pltpu.roll(b3, 1, axis=1))
    ref2 = jnp.where(sub < 4, sub_bcast(b3, 1), sub_bcast(b3, 5))
    ref4 = sub_bcast(b3, 3)
    for lvl, (ref, upper) in enumerate(((ref1, sub % 2 == 1), (ref2, sub % 4 >= 2), (ref4, sub >= 4)), start=1):
        e = jnp.exp(-jnp.abs(b3 - ref))
        levels.append((lvl, jnp.where(upper, q3 * e, 0.0), jnp.where(upper, 0.0, k3 * e)))
    zero_group = jnp.zeros((1, S, W), F32)
    for lvl, half in enumerate((1, 2, 4, 8), start=4):
        qparts, kparts = [], []
        for j in range(J):
            jr = (j // (2 * half)) * (2 * half) + half - 1
            ref = b3[jr:jr + 1, S - 1:S, :]
            if (j % (2 * half)) >= half:
                qparts.append(q3[j:j + 1] * jnp.exp(b3[j:j + 1] - ref))
                kparts.append(zero_group)
            else:
                qparts.append(zero_group)
                kparts.append(k3[j:j + 1] * jnp.exp(ref - b3[j:j + 1]))
        levels.append((lvl, jnp.concatenate(qparts, axis=0), jnp.concatenate(kparts, axis=0)))

    tr = lax.broadcasted_iota(I32, (C, C), 0)
    tc = lax.broadcasted_iota(I32, (C, C), 1)
    xr = tr ^ tc
    code = jnp.zeros((C, C), I32)
    for lvl in range(1, 8):
        code = jnp.where(xr >= (1 << (lvl - 1)), lvl, code)
    code = jnp.where(tc > tr, -1, code)

    def mat(x3, h):
        return x3.reshape(C, W)[:, h * HD:(h + 1) * HD].astype(BF16)

    nt = (((1,), (1,)), ((), ()))
    tn = (((0,), (0,)), ((), ()))
    ebc = jnp.exp(bC).reshape(1, W)
    ng = ng_ref[...]
    for h in range(NH):
        a_mat = jnp.zeros((C, C), F32)
        for lvl, qr, kr in levels:
            s = lax.dot_general(mat(qr, h), mat(kr, h), nt, preferred_element_type=F32)
            a_mat = jnp.where(code == lvl, s, a_mat)
        vh = mat(v3, h)
        st = st_ref[h]
        o = jnp.dot(a_mat.astype(BF16), vh, preferred_element_type=F32)
        o = o + lax.dot_general(mat(qe3, h), st.astype(BF16), nt, preferred_element_type=F32)
        o = _rms(o, ng) * og_ref[:, h * HD:(h + 1) * HD].astype(F32)
        o_ref[:, h * HD:(h + 1) * HD] = o.astype(o_ref.dtype)
        st_ref[h] = st * ebc[:, h * HD:(h + 1) * HD] + lax.dot_general(
            vh, mat(ks3, h), tn, preferred_element_type=F32)


def _hgrn(q, logf, v, og, norm_g):
    T, W = q.shape
    C = HGRN_CHUNK
    blk = pl.BlockSpec((C, W), lambda c: (c, 0))
    vmem = 64 * _nbytes((C, W), F32)
    return pl.pallas_call(
        _hgrn_kernel,
        out_shape=jax.ShapeDtypeStruct((T, W), BF16),
        grid=(T // C,),
        in_specs=[blk, blk, blk, blk, pl.BlockSpec((1, HGRN_HEAD_DIM), lambda c: (0, 0))],
        out_specs=blk,
        scratch_shapes=[pltpu.VMEM((HGRN_HEADS, HGRN_HEAD_DIM, HGRN_HEAD_DIM), F32)],
        compiler_params=_cparams(1, vmem),
        name="hgrn",
    )(q, logf, v, og, norm_g.reshape(1, HGRN_HEAD_DIM))


def _moba_kernel(qt_ref, k_ref, vt_ref, o_ref, km_ref):
    BS, HD, VR = MOBA_BLOCK, MOBA_HEAD_DIM, MOBA_VT_ROWS
    T = k_ref.shape[0]
    NB = T // BS
    G = o_ref.shape[1] // HD
    cur = pl.program_id(1)

    @pl.when(cur == 0)
    def _():
        for g in range(G):
            kf = k_ref[:, 2 * g * HD:(2 * g + 1) * HD].astype(F32).reshape(NB, BS, HD)
            km_ref[g] = jnp.sum(kf, axis=1) * (1.0 / BS)

    blk = lax.broadcasted_iota(I32, (NB, BS), 0)
    key = lax.broadcasted_iota(I32, (BS, BS), 0)
    qry = lax.broadcasted_iota(I32, (BS, BS), 1)
    pad = jnp.zeros((V7X_LANES - NB, BS), F32)
    r0 = pl.multiple_of(cur * BS, BS)
    qts = [qt_ref[0, g * HD:(g + 1) * HD, :] for g in range(G)]
    gts = [jnp.dot(km_ref[g].astype(BF16), qts[g], preferred_element_type=F32) for g in range(G)]
    s_own = [jnp.dot(k_ref[pl.ds(r0, BS), 2 * g * HD:(2 * g + 1) * HD], qts[g], preferred_element_type=F32)
             for g in range(G)]
    qcs = []
    for g in range(G):
        gt = jnp.where(blk < cur, gts[g], NEG_INF)
        sel = jnp.zeros((NB, BS), jnp.bool_)
        for _ in range(MOBA_TOPK):
            mx = jnp.max(gt, axis=0, keepdims=True)
            idx = jnp.min(jnp.where(gt == mx, blk, NB), axis=0, keepdims=True)
            pick = (blk == idx) & (mx > 0.5 * NEG_INF)
            sel = sel | pick
            gt = jnp.where(pick, NEG_INF, gt)
        pen = jnp.concatenate([jnp.where(sel, 0.0, NEG_INF), pad], axis=0).astype(BF16)
        qcs.append(jnp.concatenate([qts[g], pen], axis=0))

    def scores(n):
        rn = pl.multiple_of(n * BS, BS)
        return tuple(jnp.dot(k_ref[pl.ds(rn, BS), 2 * g * HD:(2 * g + 2) * HD], qcs[g], preferred_element_type=F32)
                     for g in range(G))

    ms, ps = [], []
    for g in range(G):
        s = jnp.where(key <= qry, s_own[g], NEG_INF)
        m0 = jnp.max(s, axis=0, keepdims=True)
        ms.append(m0)
        ps.append(jnp.exp(s - m0).astype(BF16))
    accs = [jnp.dot(vt_ref[cur, g * VR:(g + 1) * VR, :], ps[g], preferred_element_type=F32)
            for g in range(G)]

    KB = MOBA_BLOCKS_PER_STEP

    def body(c, carry):
        ms, accs = carry
        r = pl.multiple_of(c * KB * BS, KB * BS)
        sns = [jnp.dot(k_ref[pl.ds(r, KB * BS), 2 * g * HD:(2 * g + 2) * HD], qcs[g], preferred_element_type=F32)
               for g in range(G)]
        new_ms, alphas, pns = [], [], []
        for g in range(G):
            m_new = jnp.maximum(ms[g], jnp.max(sns[g], axis=0, keepdims=True))
            alphas.append(jnp.exp(ms[g] - m_new))
            pns.append(jnp.exp(sns[g] - m_new).astype(BF16))
            new_ms.append(m_new)
        new_accs = []
        for g in range(G):
            pv = alphas[g] * accs[g]
            for j in range(KB):
                pv = pv + jnp.dot(vt_ref[KB * c + j, g * VR:(g + 1) * VR, :], pns[g][j * BS:(j + 1) * BS],
                                  preferred_element_type=F32)
            new_accs.append(pv)
        return tuple(new_ms), tuple(new_accs)

    _, accs = lax.fori_loop(0, (cur + KB - 1) // KB, body, (tuple(ms), tuple(accs)))
    for g in range(G):
        ot = accs[g][:HD, :] / accs[g][HD:HD + 1, :]
        o_ref[:, g * HD:(g + 1) * HD] = ot.T.astype(o_ref.dtype)


def _moba(mqt, mk_aug, mvt, heads_per_step=4):
    T = mk_aug.shape[0]
    BS, HD, G, VR = MOBA_BLOCK, MOBA_HEAD_DIM, heads_per_step, MOBA_VT_ROWS
    NB = T // BS
    vmem = _nbytes((T, 2 * G * HD), BF16) + _nbytes((NB, G * VR, BS), BF16) + 8 * _nbytes((BS, G * HD), BF16)
    vmem += 16 * G * _nbytes((BS, BS), F32)
    resident = pl.Buffered(1)
    return pl.pallas_call(
        _moba_kernel,
        out_shape=jax.ShapeDtypeStruct((T, MOBA_WIDTH), BF16),
        grid=(MOBA_HEADS // G, NB),
        in_specs=[
            pl.BlockSpec((1, G * HD, BS), lambda h, i: (i, h, 0)),
            pl.BlockSpec((T, 2 * G * HD), lambda h, i: (0, h), pipeline_mode=resident),
            pl.BlockSpec((NB, G * VR, BS), lambda h, i: (0, h, 0), pipeline_mode=resident),
        ],
        out_specs=pl.BlockSpec((BS, G * HD), lambda h, i: (i, h)),
        scratch_shapes=[pltpu.VMEM((G, NB, HD), F32)],
        compiler_params=_cparams(2, vmem),
        name="moba",
    )(mqt, mk_aug, mvt)


def _merge_kernel(oh_ref, om_ref, ga_ref, gb_ref, wh_ref, wm_ref, o_ref):
    a = jnp.dot(oh_ref[...], wh_ref[...], preferred_element_type=F32)
    b = jnp.dot(om_ref[...], wm_ref[...], preferred_element_type=F32)
    o_ref[...] = (ga_ref[...].astype(F32) * a + gb_ref[...].astype(F32) * b).astype(o_ref.dtype)


def _merge(o_hgrn, o_moba, gates, w_up_hgrn, w_up_moba, tm=512):
    T = o_hgrn.shape[0]
    D = D_MODEL
    vmem = 2 * (_nbytes((tm, HGRN_WIDTH), F32) + _nbytes((tm, MOBA_WIDTH), BF16) + 3 * _nbytes((tm, D), BF16)
                + 2 * _nbytes((HGRN_WIDTH, D), BF16)) + 3 * _nbytes((tm, D), F32)
    return pl.pallas_call(
        _merge_kernel,
        out_shape=jax.ShapeDtypeStruct((T, D), BF16),
        grid=(T // tm,),
        in_specs=[
            pl.BlockSpec((tm, HGRN_WIDTH), lambda i: (i, 0)),
            pl.BlockSpec((tm, MOBA_WIDTH), lambda i: (i, 0)),
            pl.BlockSpec((tm, D), lambda i: (i, 0)),
            pl.BlockSpec((tm, D), lambda i: (i, 1)),
            pl.BlockSpec((HGRN_WIDTH, D), lambda i: (0, 0)),
            pl.BlockSpec((MOBA_WIDTH, D), lambda i: (0, 0)),
        ],
        out_specs=pl.BlockSpec((tm, D), lambda i: (i, 0)),
        compiler_params=_cparams(1, vmem),
        name="merge",
    )(o_hgrn, o_moba, gates, gates, w_up_hgrn, w_up_moba)


def _pack_bf16_pairs(x):
    C = x.shape[1] // 2
    b = lax.bitcast_convert_type(x, jnp.uint32)
    r = (b + jnp.uint32(0x7FFF) + ((b >> 16) & jnp.uint32(1))) >> 16
    return r[:, :C] | (r[:, C:] << 16)


def _unpack_bf16_pairs(p):
    lo = lax.bitcast_convert_type(p << 16, F32)
    hi = lax.bitcast_convert_type(p & jnp.uint32(0xFFFF0000), F32)
    return jnp.concatenate([lo, hi], axis=1)


def _outproj_kernel(m_ref, x_ref, w_ref, g_ref, wr_ref, x1_ref, h2p_ref, info_ref, cnt_ref, carry_ref):
    @pl.when(pl.program_id(0) == 0)
    def _():
        carry_ref[...] = jnp.zeros_like(carry_ref)

    x1 = x_ref[...] + jnp.dot(m_ref[...], w_ref[...], preferred_element_type=F32)
    x1_ref[...] = x1
    h2 = _rms(x1, g_ref[...])
    h2p_ref[...] = _pack_bf16_pairs(h2)
    info_ref[...] = _route(h2, wr_ref, carry_ref)
    cnt_ref[...] = carry_ref[...]


def _outproj(merged, x, w_out, g_ffn, w_router, tm=512):
    T, D = x.shape
    vmem = 2 * (_nbytes((tm, D), BF16) + 3 * _nbytes((tm, D), F32) + _nbytes((D, D), BF16)) + 6 * _nbytes((tm, D), F32)
    row = lambda i: (i, 0)
    fixed = lambda i: (0, 0)
    return pl.pallas_call(
        _outproj_kernel,
        out_shape=(jax.ShapeDtypeStruct((T, D), F32), jax.ShapeDtypeStruct((T, D // 2), jnp.uint32),
                   jax.ShapeDtypeStruct((T, V7X_LANES), F32), jax.ShapeDtypeStruct((1, V7X_LANES), F32)),
        grid=(T // tm,),
        in_specs=[
            pl.BlockSpec((tm, D), row),
            pl.BlockSpec((tm, D), row),
            pl.BlockSpec((D, D), fixed),
            pl.BlockSpec((1, D), fixed),
            pl.BlockSpec((D, V7X_LANES), fixed),
        ],
        out_specs=(pl.BlockSpec((tm, D), row), pl.BlockSpec((tm, D // 2), row), pl.BlockSpec((tm, V7X_LANES), row),
                   pl.BlockSpec((1, V7X_LANES), fixed)),
        scratch_shapes=[pltpu.VMEM((1, V7X_LANES), F32)],
        compiler_params=_cparams(1, vmem),
        name="outproj",
    )(merged, x, w_out, g_ffn.reshape(1, D), w_router)


def _route(h2, w_ref, carry_ref):
    tm = h2.shape[0]
    logits = jnp.dot(h2.astype(BF16), w_ref[...], preferred_element_type=F32)
    lane = lax.broadcasted_iota(I32, (tm, V7X_LANES), 1)
    is_g = lane < N_GROUPS
    gl = jnp.where(is_g, logits, NEG_INF)
    gmax = jnp.max(gl, axis=1, keepdims=True)
    g_sel = jnp.min(jnp.where(gl == gmax, lane, V7X_LANES), axis=1, keepdims=True)
    gsum = jnp.sum(jnp.where(is_g, jnp.exp(gl - gmax), 0.0), axis=1, keepdims=True)
    p_group = 1.0 / gsum
    lo = N_GROUPS + EXPERTS_PER_GROUP * g_sel
    emask = (lane >= lo) & (lane < lo + EXPERTS_PER_GROUP)
    el = jnp.where(emask, logits, NEG_INF)
    e1 = jnp.max(el, axis=1, keepdims=True)
    i1 = jnp.min(jnp.where((el == e1) & emask, lane, V7X_LANES), axis=1, keepdims=True)
    emask2 = emask & (lane != i1)
    el2 = jnp.where(emask2, logits, NEG_INF)
    e2 = jnp.max(el2, axis=1, keepdims=True)
    i2 = jnp.min(jnp.where((el2 == e2) & emask2, lane, V7X_LANES), axis=1, keepdims=True)
    r = jnp.exp(e2 - e1)
    w1 = p_group / (1.0 + r)
    w2 = p_group * r / (1.0 + r)
    eid1 = i1 - N_GROUPS
    eid2 = i2 - N_GROUPS
    oh1 = jnp.where(lane == eid1, 1.0, 0.0)
    oh2 = jnp.where(lane == eid2, 1.0, 0.0)
    cnt = oh1 + oh2
    tri = jnp.where(lax.broadcasted_iota(I32, (tm, tm), 0) > lax.broadcasted_iota(I32, (tm, tm), 1), 1.0, 0.0)
    before = jnp.dot(tri.astype(BF16), cnt.astype(BF16), preferred_element_type=F32) + carry_ref[...]
    rank1 = jnp.sum(oh1 * before, axis=1, keepdims=True)
    rank2 = jnp.sum(oh2 * before, axis=1, keepdims=True)
    carry_ref[...] = carry_ref[...] + jnp.sum(cnt, axis=0, keepdims=True)
    info = jnp.zeros((tm, V7X_LANES), F32)
    for k, val in enumerate((eid1.astype(F32), eid2.astype(F32), w1, w2, rank1, rank2)):
        info = jnp.where(lane == k, val, info)
    return info


def _row_copy(src_ref, src_row, dst_ref, dst_row, sem):
    return pltpu.make_async_copy(src_ref.at[pl.ds(src_row, 1), :], dst_ref.at[pl.ds(dst_row, 1), :], sem)


ROW_DMA_PRIORITY = 1


ROW_DMA_UNROLL = 8


def _dispatch_kernel(dest_ref, pend_ref, h_ref, xs_ref, zero_ref, sem):
    tm = h_ref.shape[0]
    step = pl.program_id(0)
    base = step * tm

    @pl.when(step == 0)
    def _():
        zero_ref[...] = jnp.zeros_like(zero_ref)

        def tail(e):
            return pltpu.make_async_copy(
                zero_ref, xs_ref.at[pl.ds(pl.multiple_of(pend_ref[e] - MOE_ROWS, MOE_ROWS), MOE_ROWS), :], sem)

        def nonempty(e):
            return pend_ref[e] > (pend_ref[e - 1] if e else 0)

        def unused(b):
            return pltpu.make_async_copy(
                zero_ref, xs_ref.at[pl.ds(pl.multiple_of(b * MOE_ROWS, MOE_ROWS), MOE_ROWS), :], sem)

        first_unused = pend_ref[N_EXPERTS - 1] // MOE_ROWS
        n_blocks = xs_ref.shape[0] // MOE_ROWS
        for e in range(N_EXPERTS):
            pl.when(nonempty(e))(lambda e=e: tail(e).start())
        lax.fori_loop(first_unused, n_blocks, lambda b, c: (unused(b).start(), c)[1], 0)
        for e in range(N_EXPERTS):
            pl.when(nonempty(e))(lambda e=e: tail(e).wait())
        lax.fori_loop(first_unused, n_blocks, lambda b, c: (unused(b).wait(), c)[1], 0)

    def issue(rb, c):
        for u in range(ROW_DMA_UNROLL):
            r = rb * ROW_DMA_UNROLL + u
            for k in range(TOPK_IN_GROUP):
                _row_copy(h_ref, r, xs_ref, dest_ref[(base + r) * TOPK_IN_GROUP + k], sem).start()
        return c

    lax.fori_loop(0, tm // ROW_DMA_UNROLL, issue, 0)
    for _ in range(tm * TOPK_IN_GROUP):
        _row_copy(h_ref, 0, xs_ref, 0, sem).wait()


def _dispatch(dest_flat, pad_end, h2p, m_pad, tm=256):
    T, C = h2p.shape
    return pl.pallas_call(
        _dispatch_kernel,
        out_shape=jax.ShapeDtypeStruct((m_pad, C), h2p.dtype),
        grid_spec=pltpu.PrefetchScalarGridSpec(
            num_scalar_prefetch=2,
            grid=(T // tm,),
            in_specs=[pl.BlockSpec((tm, C), lambda i, d, pe: (i, 0))],
            out_specs=pl.BlockSpec(memory_space=pl.ANY),
            scratch_shapes=[pltpu.VMEM((MOE_ROWS, C), h2p.dtype), pltpu.SemaphoreType.DMA(())],
        ),
        compiler_params=_cparams(1, 6 * _nbytes((tm, C), h2p.dtype)),
        name="dispatch",
    )(dest_flat, pad_end, h2p)


def _expert_kernel(be_ref, nu_ref, xs_ref, wg_ref, wu_ref, wd_ref, y_ref, wgb_ref, wub_ref, wdb_ref):
    b = pl.program_id(0)
    used = b < nu_ref[0]

    @pl.when(used & ((b == 0) | (be_ref[b] != be_ref[jnp.maximum(b - 1, 0)])))
    def _():
        wgb_ref[...] = wg_ref[0].astype(BF16)
        wub_ref[...] = wu_ref[0].astype(BF16)
        wdb_ref[...] = wd_ref[0].astype(BF16)

    @pl.when(used)
    def _():
        x = _unpack_bf16_pairs(xs_ref[...]).astype(BF16)
        a = jnp.dot(x, wgb_ref[...], preferred_element_type=F32)
        u = jnp.dot(x, wub_ref[...], preferred_element_type=F32)
        hm = (a * jax.nn.sigmoid(a) * u).astype(BF16)
        y_ref[...] = _pack_bf16_pairs(jnp.dot(hm, wdb_ref[...], preferred_element_type=F32))

    @pl.when(jnp.logical_not(used))
    def _():
        y_ref[...] = jnp.zeros_like(y_ref)


def _experts(block_expert, n_used, xs, w_gate, w_up, w_down):
    m_pad, C = xs.shape
    D = 2 * C
    R, Fd = MOE_ROWS, D_EXPERT
    vmem = 2 * (2 * _nbytes((R, C), xs.dtype) + 3 * _nbytes((D, Fd), F32)) + 3 * _nbytes((D, Fd), BF16)
    vmem += 6 * _nbytes((R, D), F32)
    return pl.pallas_call(
        _expert_kernel,
        out_shape=jax.ShapeDtypeStruct((m_pad, C), xs.dtype),
        grid_spec=pltpu.PrefetchScalarGridSpec(
            num_scalar_prefetch=2,
            grid=(m_pad // R,),
            in_specs=[
                pl.BlockSpec((R, C), lambda b, be, nu: (jnp.minimum(b, jnp.maximum(nu[0] - 1, 0)), 0)),
                pl.BlockSpec((1, D, Fd), lambda b, be, nu: (be[b], 0, 0)),
                pl.BlockSpec((1, D, Fd), lambda b, be, nu: (be[b], 0, 0)),
                pl.BlockSpec((1, Fd, D), lambda b, be, nu: (be[b], 0, 0)),
            ],
            out_specs=pl.BlockSpec((R, C), lambda b, be, nu: (b, 0)),
            scratch_shapes=[pltpu.VMEM((D, Fd), BF16), pltpu.VMEM((D, Fd), BF16), pltpu.VMEM((Fd, D), BF16)],
        ),
        compiler_params=_cparams(1, vmem),
        name="experts",
    )(block_expert, n_used, xs, w_gate, w_up, w_down)


def _combine_kernel(dest_ref, x1_ref, info_ref, yb_ref, p_ref, gp_ref, wpg_ref, wpp_ref, gf_ref, o_ref, ybuf, sem):
    tm, D = x1_ref.shape
    step = pl.program_id(0)
    n_tiles = pl.num_programs(0) - 2
    K = TOPK_IN_GROUP
    GROUPS = 8

    def fetch(tile, group=None):
        slot = tile % 2
        rows = range(tm) if group is None else range(group * tm // GROUPS, (group + 1) * tm // GROUPS)
        for r in rows:
            for k in range(K):
                _row_copy(yb_ref, dest_ref[(tile * tm + r) * K + k], ybuf.at[slot * K + k], r,
                          sem.at[slot]).start(priority=ROW_DMA_PRIORITY)

    def drain(tile):
        slot = tile % 2
        for _ in range(tm * K):
            _row_copy(yb_ref, 0, ybuf.at[0], 0, sem.at[slot]).wait()

    @pl.when(step == 0)
    def _():
        fetch(step)

    @pl.when(step > 0)
    def _():
        drain(step - 1)

    @pl.when((step > 0) & (step <= n_tiles))
    def _():
        tile = step - 1
        slot = tile % 2
        info = info_ref[...]
        x2 = (x1_ref[...] + info[:, 2:3] * _unpack_bf16_pairs(ybuf[slot * K])
              + info[:, 3:4] * _unpack_bf16_pairs(ybuf[slot * K + 1]))
        hp = _rms(x2, gp_ref[...]).astype(BF16)
        pp = jnp.dot(p_ref[...].astype(BF16), wpp_ref[...], preferred_element_type=F32)
        cw = D // GROUPS
        zs = []
        for c in range(GROUPS):
            fetch(step, c)
            zs.append(jnp.dot(hp, wpg_ref[:, c * cw:(c + 1) * cw], preferred_element_type=F32))
        x3 = x2 + jax.nn.sigmoid(jnp.concatenate(zs, axis=1)) * pp
        o_ref[...] = _rms(x3, gf_ref[...])


def _combine(dest_flat, x1, info, yb, p, g_ple, w_ple_gate, w_ple_proj, g_final, tm=256):
    T, D = x1.shape
    n_tiles = T // tm
    dest_padded = jnp.concatenate([dest_flat, jnp.zeros((tm * TOPK_IN_GROUP,), I32)])
    vmem = 4 * _nbytes((tm, D), F32) + _nbytes((D, D), BF16) + _nbytes((PLE_DIM, D), BF16)
    vmem += 2 * _nbytes((tm, PLE_DIM), F32) + 2 * TOPK_IN_GROUP * _nbytes((tm, D // 2), yb.dtype)
    vmem += 6 * _nbytes((tm, D), F32)
    tile = lambda i, d: (jnp.clip(i - 1, 0, n_tiles - 1), 0)
    resident = pl.Buffered(1)
    return pl.pallas_call(
        _combine_kernel,
        out_shape=jax.ShapeDtypeStruct((T, D), F32),
        grid_spec=pltpu.PrefetchScalarGridSpec(
            num_scalar_prefetch=1,
            grid=(n_tiles + 2,),
            in_specs=[
                pl.BlockSpec((tm, D), tile),
                pl.BlockSpec((tm, V7X_LANES), tile),
                pl.BlockSpec(memory_space=pl.ANY),
                pl.BlockSpec((tm, PLE_DIM), tile),
                pl.BlockSpec((1, D), lambda i, d: (0, 0)),
                pl.BlockSpec((D, D), lambda i, d: (0, 0), pipeline_mode=resident),
                pl.BlockSpec((PLE_DIM, D), lambda i, d: (0, 0), pipeline_mode=resident),
                pl.BlockSpec((1, D), lambda i, d: (0, 0)),
            ],
            out_specs=pl.BlockSpec((tm, D), tile),
            scratch_shapes=[pltpu.VMEM((2 * TOPK_IN_GROUP, tm, yb.shape[1]), yb.dtype),
                            pltpu.SemaphoreType.DMA((2,))],
        ),
        compiler_params=_cparams(1, vmem),
        name="combine",
    )(dest_padded, x1, info, yb, p, g_ple.reshape(1, D), w_ple_gate, w_ple_proj, g_final.reshape(1, D))


def _rope_tables(T):
    half, BS = MOBA_HEAD_DIM // 2, MOBA_BLOCK
    inv_freq = ROPE_THETA ** (-jnp.arange(half, dtype=F32) / half)
    ang_a = (jnp.arange(T // BS, dtype=F32) * BS)[:, None, None] * inv_freq
    ang_b = jnp.arange(BS, dtype=F32)[None, :, None] * inv_freq
    ca, sa, cb, sb = jnp.cos(ang_a), jnp.sin(ang_a), jnp.cos(ang_b), jnp.sin(ang_b)
    cos = (ca * cb - sa * sb).reshape(T, half)
    sin = (sa * cb + ca * sb).reshape(T, half)
    return jnp.concatenate([cos, cos], axis=1), jnp.concatenate([-sin, sin], axis=1)


def _mixers(x2d, g_mix, w_in, lb, hgrn_norm_g):
    T = x2d.shape[0]
    W = HGRN_WIDTH
    h = _rmsnorm(x2d, g_mix, BF16)
    w = w_in
    cos, sin = _rope_tables(T)
    log_lb = jnp.log(lb).reshape(1, W)
    log_1m = jnp.log1p(-lb).reshape(1, W)
    hq = _proj(h, w, 0 * W, W, _ep_silu, BF16)
    logf = _proj(h, w, 1 * W, W, _ep_logf, F32, col_extras=(log_lb, log_1m))
    hi = _proj(h, w, 2 * W, W, _ep_identity, BF16)
    hog = _proj(h, w, 3 * W, W, _ep_silu, BF16)
    scale = MOBA_HEAD_DIM ** -0.5
    mqt = _proj_t(h, w, 4 * W, functools.partial(_ep_rope, scale=scale), row_extras=(cos, sin))
    blk = jnp.arange(T, dtype=I32)[:, None] // MOBA_BLOCK
    blk_onehot = (blk == jnp.arange(V7X_LANES, dtype=I32)[None, :]).astype(F32)
    mk = _proj(h, w, 5 * W, W, _ep_rope_aug, BF16, row_extras=(cos, sin, blk_onehot), widen=2)
    mvt = _proj_t(h, w, 6 * W, _ep_identity, ones_rows=MOBA_VT_ROWS - MOBA_HEAD_DIM)
    gates = _proj(h, w, 7 * W, 2 * D_MODEL, _ep_sigmoid, BF16)
    o_hgrn = _hgrn(hq, logf, hi, hog, hgrn_norm_g)
    o_moba = _moba(mqt, mk, mvt)
    return o_hgrn, o_moba, gates


def _moe_plan(info, cnt, T):
    R = MOE_ROWS
    eid = info[:, 0:TOPK_IN_GROUP].astype(I32)
    rank = info[:, 4:4 + TOPK_IN_GROUP].astype(I32)
    counts = cnt[0, :N_EXPERTS].astype(I32)
    padded = (counts + R - 1) // R * R
    pad_end = jnp.cumsum(padded)
    pad_start = pad_end - padded
    dest = (pad_start[eid] + rank).reshape(-1)
    n_blocks = (T * TOPK_IN_GROUP) // R + N_EXPERTS
    block_expert = jnp.minimum(
        jnp.searchsorted(pad_end, jnp.arange(n_blocks, dtype=I32) * R, side="right"), N_EXPERTS - 1).astype(I32)
    n_used = (pad_end[-1:] // R).astype(I32)
    return dest, pad_end.astype(I32), block_expert, n_used, n_blocks * R


def kernel(x, p, norm_mix_g, w_in, hgrn_lb_raw, hgrn_norm_g, w_up_hgrn, w_up_moba, w_out, norm_ffn_g,
           w_router_group, w_router_expert, w_exp_gate, w_exp_up, w_exp_down, norm_ple_g, w_ple_gate,
           w_ple_proj, norm_final_g):
    B, T, D = x.shape
    assert B == 1 and D == D_MODEL and w_in.shape[0] == 1 and T % (4 * MOBA_BLOCK) == 0
    lower_bounds = jnp.cumsum(jax.nn.softmax(hgrn_lb_raw.astype(F32), axis=0), axis=0)
    x2d = x.reshape(T, D)
    o_hgrn, o_moba, gates = _mixers(x2d, norm_mix_g[0], w_in[0], lower_bounds[0], hgrn_norm_g[0])
    merged = _merge(o_hgrn, o_moba, gates, w_up_hgrn[0].astype(BF16), w_up_moba[0].astype(BF16))
    w_router = jnp.pad(jnp.concatenate([w_router_group[0], w_router_expert[0]], axis=1),
                       ((0, 0), (0, V7X_LANES - N_GROUPS - N_EXPERTS))).astype(BF16)
    x1, h2p, info, cnt = _outproj(merged, x2d, w_out[0].astype(BF16), norm_ffn_g[0], w_router)
    dest, pad_end, block_expert, n_used, m_pad = _moe_plan(info, cnt, T)
    xs = _dispatch(dest, pad_end, h2p, m_pad)
    yb = _experts(block_expert, n_used, xs, w_exp_gate[0], w_exp_up[0], w_exp_down[0])
    out = _combine(dest, x1, info, yb, p[0].reshape(T, PLE_DIM), norm_ple_g[0], w_ple_gate[0].astype(BF16),
                   w_ple_proj[0].astype(BF16), norm_final_g)
    return out.reshape(B, T, D)
```

```python
import functools

import jax
import jax.numpy as jnp
from jax import lax
from jax.experimental import pallas as pl
from jax.experimental.pallas import tpu as pltpu

F32 = jnp.float32
BF16 = jnp.bfloat16
I32 = jnp.int32

D_MODEL = 2048
PLE_DIM = 256
HGRN_HEADS = 8
HGRN_HEAD_DIM = 128
HGRN_WIDTH = HGRN_HEADS * HGRN_HEAD_DIM
MOBA_HEADS = 8
MOBA_HEAD_DIM = 128
MOBA_WIDTH = MOBA_HEADS * MOBA_HEAD_DIM
MOBA_BLOCK = 256
MOBA_TOPK = 3
ROPE_THETA = 10000.0
N_GROUPS = 4
EXPERTS_PER_GROUP = 8
N_EXPERTS = N_GROUPS * EXPERTS_PER_GROUP
TOPK_IN_GROUP = 2
D_EXPERT = 512
EPS = 1e-6
NEG_INF = -1e30

V7X_LANES = 128
V7X_SUBLANES = 8
V7X_VMEM_BUDGET_BYTES = 56 * 1024 * 1024

HGRN_CHUNK = 128
MOE_ROWS = 256


def _cparams(n_grid, vmem_bytes):
    return pltpu.CompilerParams(
        dimension_semantics=("arbitrary",) * n_grid,
        vmem_limit_bytes=int(min(max(vmem_bytes, 16 * 1024 * 1024), V7X_VMEM_BUDGET_BYTES)),
    )


def _nbytes(shape, dtype):
    n = 1
    for s in shape:
        n *= s
    return n * jnp.dtype(dtype).itemsize


def _rms(x, g):
    ms = jnp.mean(x * x, axis=-1, keepdims=True)
    return x * lax.rsqrt(ms + EPS) * g


def _rmsnorm_kernel(x_ref, g_ref, o_ref):
    o_ref[...] = _rms(x_ref[...], g_ref[...]).astype(o_ref.dtype)


def _rmsnorm(x, g, out_dtype, tm=512):
    T, D = x.shape
    return pl.pallas_call(
        _rmsnorm_kernel,
        out_shape=jax.ShapeDtypeStruct((T, D), out_dtype),
        grid=(T // tm,),
        in_specs=[pl.BlockSpec((tm, D), lambda i: (i, 0)), pl.BlockSpec((1, D), lambda i: (0, 0))],
        out_specs=pl.BlockSpec((tm, D), lambda i: (i, 0)),
        compiler_params=_cparams(1, 4 * _nbytes((tm, D), F32)),
        name="rmsnorm",
    )(x, g.reshape(1, D))


def _ep_identity(acc):
    return acc


def _ep_silu(acc):
    return acc * jax.nn.sigmoid(acc)


def _ep_sigmoid(acc):
    return jax.nn.sigmoid(acc)


def _ep_logf(acc, la_ref, lc_ref):
    ls = jnp.minimum(acc, 0.0) - jnp.log(1.0 + jnp.exp(-jnp.abs(acc)))
    u = la_ref[...]
    v = lc_ref[...] + ls
    return jnp.maximum(u, v) + jnp.log(1.0 + jnp.exp(-jnp.abs(u - v)))


def _ep_rope(acc, cos_ref, sin_ref, *, scale):
    cos = cos_ref[...]
    sin = sin_ref[...]
    outs = []
    for hh in range(acc.shape[1] // MOBA_HEAD_DIM):
        a = acc[:, hh * MOBA_HEAD_DIM:(hh + 1) * MOBA_HEAD_DIM]
        r = pltpu.roll(a, MOBA_HEAD_DIM // 2, axis=1)
        outs.append((a * cos + r * sin) * scale)
    return jnp.concatenate(outs, axis=1)


def _ep_rope_aug(acc, cos_ref, sin_ref, oh_ref):
    cos = cos_ref[...]
    sin = sin_ref[...]
    oh = oh_ref[...]
    outs = []
    for hh in range(acc.shape[1] // MOBA_HEAD_DIM):
        a = acc[:, hh * MOBA_HEAD_DIM:(hh + 1) * MOBA_HEAD_DIM]
        outs.append(a * cos + pltpu.roll(a, MOBA_HEAD_DIM // 2, axis=1) * sin)
        outs.append(oh)
    return jnp.concatenate(outs, axis=1)


def _cast_weight_once(w_ref, wb_ref, row_axis):
    @pl.when(pl.program_id(row_axis) == 0)
    def _():
        wb_ref[...] = w_ref[...].astype(wb_ref.dtype)


def _proj_kernel(h_ref, w_ref, *refs, epilogue):
    *extra, o_ref, wb_ref = refs
    _cast_weight_once(w_ref, wb_ref, 1)
    acc = jnp.dot(h_ref[...], wb_ref[...], preferred_element_type=F32)
    o_ref[...] = epilogue(acc, *extra).astype(o_ref.dtype)


def _proj(h, w, col0, ncols, epilogue, out_dtype, row_extras=(), col_extras=(), tm=1024, tn=1024, widen=1):
    T, K = h.shape
    tn = min(tn, ncols)
    tm = min(tm, T)
    cb = col0 // tn
    otn = widen * tn
    in_specs = [
        pl.BlockSpec((tm, K), lambda j, i: (i, 0)),
        pl.BlockSpec((K, tn), lambda j, i: (0, cb + j)),
    ]
    for e in row_extras:
        in_specs.append(pl.BlockSpec((tm, e.shape[1]), lambda j, i: (i, 0)))
    for e in col_extras:
        in_specs.append(pl.BlockSpec((1, tn), lambda j, i: (0, j)))
    vmem = 2 * (_nbytes((tm, K), h.dtype) + _nbytes((K, tn), w.dtype) + _nbytes((tm, otn), out_dtype))
    vmem += 3 * _nbytes((tm, otn), F32) + _nbytes((K, tn), BF16)
    return pl.pallas_call(
        functools.partial(_proj_kernel, epilogue=epilogue),
        out_shape=jax.ShapeDtypeStruct((T, widen * ncols), out_dtype),
        grid=(ncols // tn, T // tm),
        in_specs=in_specs,
        out_specs=pl.BlockSpec((tm, otn), lambda j, i: (i, j)),
        scratch_shapes=[pltpu.VMEM((K, tn), BF16)],
        compiler_params=_cparams(2, vmem),
        name="proj",
    )(h, w, *row_extras, *col_extras)


MOBA_VT_ROWS = MOBA_HEAD_DIM + 16
MOBA_BLOCKS_PER_STEP = 4


def _proj_t_kernel(h_ref, w_ref, *refs, epilogue, ones_rows):
    *extra, o_ref, wb_ref = refs
    BS, HD = MOBA_BLOCK, MOBA_HEAD_DIM
    _cast_weight_once(w_ref, wb_ref, 0)
    acc = epilogue(jnp.dot(h_ref[...], wb_ref[...], preferred_element_type=F32), *extra)
    ones = jnp.ones((ones_rows, BS), F32) if ones_rows else None
    for b in range(acc.shape[0] // BS):
        parts = []
        for hh in range(acc.shape[1] // HD):
            parts.append(acc[b * BS:(b + 1) * BS, hh * HD:(hh + 1) * HD].T)
            if ones_rows:
                parts.append(ones)
        o_ref[b] = jnp.concatenate(parts, axis=0).astype(o_ref.dtype)


def _proj_t(h, w, col0, epilogue, row_extras=(), ones_rows=0, tm=1024):
    T, K = h.shape
    tn = MOBA_WIDTH
    tm = min(tm, T)
    cb = col0 // tn
    rows = MOBA_HEADS * (MOBA_HEAD_DIM + ones_rows)
    in_specs = [pl.BlockSpec((tm, K), lambda i: (i, 0)), pl.BlockSpec((K, tn), lambda i: (0, cb))]
    for e in row_extras:
        in_specs.append(pl.BlockSpec((tm, e.shape[1]), lambda i: (i, 0)))
    vmem = 2 * (_nbytes((tm, K), h.dtype) + _nbytes((K, tn), w.dtype) + _nbytes((tm, 2 * tn), BF16))
    vmem += 4 * _nbytes((tm, tn), F32) + _nbytes((K, tn), BF16)
    return pl.pallas_call(
        functools.partial(_proj_t_kernel, epilogue=epilogue, ones_rows=ones_rows),
        out_shape=jax.ShapeDtypeStruct((T // MOBA_BLOCK, rows, MOBA_BLOCK), BF16),
        grid=(T // tm,),
        in_specs=in_specs,
        out_specs=pl.BlockSpec((tm // MOBA_BLOCK, rows, MOBA_BLOCK), lambda i: (i, 0, 0)),
        scratch_shapes=[pltpu.VMEM((K, tn), BF16)],
        compiler_params=_cparams(1, vmem),
        name="proj_t",
    )(h, w, *row_extras)


def _hgrn_kernel(q_ref, g_ref, v_ref, og_ref, ng_ref, o_ref, st_ref):
    W, HD, NH, C, S = HGRN_WIDTH, HGRN_HEAD_DIM, HGRN_HEADS, HGRN_CHUNK, V7X_SUBLANES
    J = C // S

    @pl.when(pl.program_id(0) == 0)
    def _():
        st_ref[...] = jnp.zeros_like(st_ref)

    def r3(x):
        return x.astype(F32).reshape(J, S, W)

    def sub_bcast(x3, r):
        return jnp.broadcast_to(x3[:, r:r + 1, :], x3.shape)

    g3, q3, v3 = r3(g_ref[...]), r3(q_ref[...]), r3(v_ref[...])
    sub = lax.broadcasted_iota(I32, (1, S, W), 1)

    c3 = g3
    for s in (1, 2, 4):
        c3 = c3 + jnp.where(sub >= s, pltpu.roll(c3, s, axis=1), 0.0)
    run = jnp.zeros((1, 1, W), F32)
    carry = []
    for j in range(J):
        carry.append(run)
        run = run + c3[j:j + 1, S - 1:S, :]
    b3 = c3 + jnp.concatenate(carry, axis=0)
    bC = run

    k3 = 1.0 - jnp.exp(g3)
    qe3 = q3 * jnp.exp(b3)
    ks3 = k3 * jnp.exp(bC - b3)

    levels = [(0, q3, k3)]
    ref1 = jnp.where(sub % 2 == 0, b3, pltpu.roll(b3, 1, axis=1))
    ref2 = jnp.where(sub < 4, sub_bcast(b3, 1), sub_bcast(b3, 5))
    ref4 = sub_bcast(b3, 3)
    for lvl, (ref, upper) in enumerate(((ref1, sub % 2 == 1), (ref2, sub % 4 >= 2), (ref4, sub >= 4)), start=1):
        e = jnp.exp(-jnp.abs(b3 - ref))
        levels.append((lvl, jnp.where(upper, q3 * e, 0.0), jnp.where(upper, 0.0, k3 * e)))
    zero_group = jnp.zeros((1, S, W), F32)
    for lvl, half in enumerate((1, 2, 4, 8), start=4):
        qparts, kparts = [], []
        for j in range(J):
            jr = (j // (2 * half)) * (2 * half) + half - 1
            ref = b3[jr:jr + 1, S - 1:S, :]
            if (j % (2 * half)) >= half:
                qparts.append(q3[j:j + 1] * jnp.exp(b3[j:j + 1] - ref))
                kparts.append(zero_group)
            else:
                qparts.append(zero_group)
                kparts.append(k3[j:j + 1] * jnp.exp(ref - b3[j:j + 1]))
        levels.append((lvl, jnp.concatenate(qparts, axis=0), jnp.concatenate(kparts, axis=0)))

    tr = lax.broadcasted_iota(I32, (C, C), 0)
    tc = lax.broadcasted_iota(I32, (C, C), 1)
    xr = tr ^ tc
    code = jnp.zeros((C, C), I32)
    for lvl in range(1, 8):
        code = jnp.where(xr >= (1 << (lvl - 1)), lvl, code)
    code = jnp.where(tc > tr, -1, code)

    def mat(x3, h):
        return x3.reshape(C, W)[:, h * HD:(h + 1) * HD].astype(BF16)

    nt = (((1,), (1,)), ((), ()))
    tn = (((0,), (0,)), ((), ()))
    ebc = jnp.exp(bC).reshape(1, W)
    ng = ng_ref[...]
    for h in range(NH):
        a_mat = jnp.zeros((C, C), F32)
        for lvl, qr, kr in levels:
            s = lax.dot_general(mat(qr, h), mat(kr, h), nt, preferred_element_type=F32)
            a_mat = jnp.where(code == lvl, s, a_mat)
        vh = mat(v3, h)
        st = st_ref[h]
        o = jnp.dot(a_mat.astype(BF16), vh, preferred_element_type=F32)
        o = o + lax.dot_general(mat(qe3, h), st.astype(BF16), nt, preferred_element_type=F32)
        o = _rms(o, ng) * og_ref[:, h * HD:(h + 1) * HD].astype(F32)
        o_ref[:, h * HD:(h + 1) * HD] = o.astype(o_ref.dtype)
        st_ref[h] = st * ebc[:, h * HD:(h + 1) * HD] + lax.dot_general(
            vh, mat(ks3, h), tn, preferred_element_type=F32)


def _hgrn(q, logf, v, og, norm_g):
    T, W = q.shape
    C = HGRN_CHUNK
    blk = pl.BlockSpec((C, W), lambda c: (c, 0))
    vmem = 64 * _nbytes((C, W), F32)
    return pl.pallas_call(
        _hgrn_kernel,
        out_shape=jax.ShapeDtypeStruct((T, W), BF16),
        grid=(T // C,),
        in_specs=[blk, blk, blk, blk, pl.BlockSpec((1, HGRN_HEAD_DIM), lambda c: (0, 0))],
        out_specs=blk,
        scratch_shapes=[pltpu.VMEM((HGRN_HEADS, HGRN_HEAD_DIM, HGRN_HEAD_DIM), F32)],
        compiler_params=_cparams(1, vmem),
        name="hgrn",
    )(q, logf, v, og, norm_g.reshape(1, HGRN_HEAD_DIM))


def _moba_kernel(qt_ref, k_ref, vt_ref, o_ref, km_ref):
    BS, HD, VR = MOBA_BLOCK, MOBA_HEAD_DIM, MOBA_VT_ROWS
    T = k_ref.shape[0]
    NB = T // BS
    G = o_ref.shape[1] // HD
    cur = pl.program_id(1)

    @pl.when(cur == 0)
    def _():
        for g in range(G):
            kf = k_ref[:, 2 * g * HD:(2 * g + 1) * HD].astype(F32).reshape(NB, BS, HD)
            km_ref[g] = jnp.sum(kf, axis=1) * (1.0 / BS)

    blk = lax.broadcasted_iota(I32, (NB, BS), 0)
    key = lax.broadcasted_iota(I32, (BS, BS), 0)
    qry = lax.broadcasted_iota(I32, (BS, BS), 1)
    pad = jnp.zeros((V7X_LANES - NB, BS), F32)
    r0 = pl.multiple_of(cur * BS, BS)
    qts = [qt_ref[0, g * HD:(g + 1) * HD, :] for g in range(G)]
    gts = [jnp.dot(km_ref[g].astype(BF16), qts[g], preferred_element_type=F32) for g in range(G)]
    s_own = [jnp.dot(k_ref[pl.ds(r0, BS), 2 * g * HD:(2 * g + 1) * HD], qts[g], preferred_element_type=F32)
             for g in range(G)]
    qcs = []
    for g in range(G):
        gt = jnp.where(blk < cur, gts[g], NEG_INF)
        sel = jnp.zeros((NB, BS), jnp.bool_)
        for _ in range(MOBA_TOPK):
            mx = jnp.max(gt, axis=0, keepdims=True)
            idx = jnp.min(jnp.where(gt == mx, blk, NB), axis=0, keepdims=True)
            pick = (blk == idx) & (mx > 0.5 * NEG_INF)
            sel = sel | pick
            gt = jnp.where(pick, NEG_INF, gt)
        pen = jnp.concatenate([jnp.where(sel, 0.0, NEG_INF), pad], axis=0).astype(BF16)
        qcs.append(jnp.concatenate([qts[g], pen], axis=0))

    def scores(n):
        rn = pl.multiple_of(n * BS, BS)
        return tuple(jnp.dot(k_ref[pl.ds(rn, BS), 2 * g * HD:(2 * g + 2) * HD], qcs[g], preferred_element_type=F32)
                     for g in range(G))

    ms, ps = [], []
    for g in range(G):
        s = jnp.where(key <= qry, s_own[g], NEG_INF)
        m0 = jnp.max(s, axis=0, keepdims=True)
        ms.append(m0)
        ps.append(jnp.exp(s - m0).astype(BF16))
    accs = [jnp.dot(vt_ref[cur, g * VR:(g + 1) * VR, :], ps[g], preferred_element_type=F32)
            for g in range(G)]

    KB = MOBA_BLOCKS_PER_STEP

    def body(c, carry):
        ms, accs = carry
        r = pl.multiple_of(c * KB * BS, KB * BS)
        sns = [jnp.dot(k_ref[pl.ds(r, KB * BS), 2 * g * HD:(2 * g + 2) * HD], qcs[g], preferred_element_type=F32)
               for g in range(G)]
        new_ms, alphas, pns = [], [], []
        for g in range(G):
            m_new = jnp.maximum(ms[g], jnp.max(sns[g], axis=0, keepdims=True))
            alphas.append(jnp.exp(ms[g] - m_new))
            pns.append(jnp.exp(sns[g] - m_new).astype(BF16))
            new_ms.append(m_new)
        new_accs = []
        for g in range(G):
            pv = alphas[g] * accs[g]
            for j in range(KB):
                pv = pv + jnp.dot(vt_ref[KB * c + j, g * VR:(g + 1) * VR, :], pns[g][j * BS:(j + 1) * BS],
                                  preferred_element_type=F32)
            new_accs.append(pv)
        return tuple(new_ms), tuple(new_accs)

    _, accs = lax.fori_loop(0, (cur + KB - 1) // KB, body, (tuple(ms), tuple(accs)))
    for g in range(G):
        ot = accs[g][:HD, :] / accs[g][HD:HD + 1, :]
        o_ref[:, g * HD:(g + 1) * HD] = ot.T.astype(o_ref.dtype)


def _moba(mqt, mk_aug, mvt, heads_per_step=4):
    T = mk_aug.shape[0]
    BS, HD, G, VR = MOBA_BLOCK, MOBA_HEAD_DIM, heads_per_step, MOBA_VT_ROWS
    NB = T // BS
    vmem = _nbytes((T, 2 * G * HD), BF16) + _nbytes((NB, G * VR, BS), BF16) + 8 * _nbytes((BS, G * HD), BF16)
    vmem += 16 * G * _nbytes((BS, BS), F32)
    resident = pl.Buffered(1)
    return pl.pallas_call(
        _moba_kernel,
        out_shape=jax.ShapeDtypeStruct((T, MOBA_WIDTH), BF16),
        grid=(MOBA_HEADS // G, NB),
        in_specs=[
            pl.BlockSpec((1, G * HD, BS), lambda h, i: (i, h, 0)),
            pl.BlockSpec((T, 2 * G * HD), lambda h, i: (0, h), pipeline_mode=resident),
            pl.BlockSpec((NB, G * VR, BS), lambda h, i: (0, h, 0), pipeline_mode=resident),
        ],
        out_specs=pl.BlockSpec((BS, G * HD), lambda h, i: (i, h)),
        scratch_shapes=[pltpu.VMEM((G, NB, HD), F32)],
        compiler_params=_cparams(2, vmem),
        name="moba",
    )(mqt, mk_aug, mvt)


def _merge_kernel(oh_ref, om_ref, ga_ref, gb_ref, wh_ref, wm_ref, o_ref):
    a = jnp.dot(oh_ref[...], wh_ref[...], preferred_element_type=F32)
    b = jnp.dot(om_ref[...], wm_ref[...], preferred_element_type=F32)
    o_ref[...] = (ga_ref[...].astype(F32) * a + gb_ref[...].astype(F32) * b).astype(o_ref.dtype)


def _merge(o_hgrn, o_moba, gates, w_up_hgrn, w_up_moba, tm=512):
    T = o_hgrn.shape[0]
    D = D_MODEL
    vmem = 2 * (_nbytes((tm, HGRN_WIDTH), F32) + _nbytes((tm, MOBA_WIDTH), BF16) + 3 * _nbytes((tm, D), BF16)
                + 2 * _nbytes((HGRN_WIDTH, D), BF16)) + 3 * _nbytes((tm, D), F32)
    return pl.pallas_call(
        _merge_kernel,
        out_shape=jax.ShapeDtypeStruct((T, D), BF16),
        grid=(T // tm,),
        in_specs=[
            pl.BlockSpec((tm, HGRN_WIDTH), lambda i: (i, 0)),
            pl.BlockSpec((tm, MOBA_WIDTH), lambda i: (i, 0)),
            pl.BlockSpec((tm, D), lambda i: (i, 0)),
            pl.BlockSpec((tm, D), lambda i: (i, 1)),
            pl.BlockSpec((HGRN_WIDTH, D), lambda i: (0, 0)),
            pl.BlockSpec((MOBA_WIDTH, D), lambda i: (0, 0)),
        ],
        out_specs=pl.BlockSpec((tm, D), lambda i: (i, 0)),
        compiler_params=_cparams(1, vmem),
        name="merge",
    )(o_hgrn, o_moba, gates, gates, w_up_hgrn, w_up_moba)


def _pack_bf16_pairs(x):
    C = x.shape[1] // 2
    b = lax.bitcast_convert_type(x, jnp.uint32)
    r = (b + jnp.uint32(0x7FFF) + ((b >> 16) & jnp.uint32(1))) >> 16
    return r[:, :C] | (r[:, C:] << 16)


def _unpack_bf16_pairs(p):
    lo = lax.bitcast_convert_type(p << 16, F32)
    hi = lax.bitcast_convert_type(p & jnp.uint32(0xFFFF0000), F32)
    return jnp.concatenate([lo, hi], axis=1)


def _outproj_kernel(m_ref, x_ref, w_ref, g_ref, wr_ref, x1_ref, h2p_ref, info_ref, cnt_ref, carry_ref):
    @pl.when(pl.program_id(0) == 0)
    def _():
        carry_ref[...] = jnp.zeros_like(carry_ref)

    x1 = x_ref[...] + jnp.dot(m_ref[...], w_ref[...], preferred_element_type=F32)
    x1_ref[...] = x1
    h2 = _rms(x1, g_ref[...])
    h2p_ref[...] = _pack_bf16_pairs(h2)
    info_ref[...] = _route(h2, wr_ref, carry_ref)
    cnt_ref[...] = carry_ref[...]


def _outproj(merged, x, w_out, g_ffn, w_router, tm=512):
    T, D = x.shape
    vmem = 2 * (_nbytes((tm, D), BF16) + 3 * _nbytes((tm, D), F32) + _nbytes((D, D), BF16)) + 6 * _nbytes((tm, D), F32)
    row = lambda i: (i, 0)
    fixed = lambda i: (0, 0)
    return pl.pallas_call(
        _outproj_kernel,
        out_shape=(jax.ShapeDtypeStruct((T, D), F32), jax.ShapeDtypeStruct((T, D // 2), jnp.uint32),
                   jax.ShapeDtypeStruct((T, V7X_LANES), F32), jax.ShapeDtypeStruct((1, V7X_LANES), F32)),
        grid=(T // tm,),
        in_specs=[
            pl.BlockSpec((tm, D), row),
            pl.BlockSpec((tm, D), row),
            pl.BlockSpec((D, D), fixed),
            pl.BlockSpec((1, D), fixed),
            pl.BlockSpec((D, V7X_LANES), fixed),
        ],
        out_specs=(pl.BlockSpec((tm, D), row), pl.BlockSpec((tm, D // 2), row), pl.BlockSpec((tm, V7X_LANES), row),
                   pl.BlockSpec((1, V7X_LANES), fixed)),
        scratch_shapes=[pltpu.VMEM((1, V7X_LANES), F32)],
        compiler_params=_cparams(1, vmem),
        name="outproj",
    )(merged, x, w_out, g_ffn.reshape(1, D), w_router)


def _route(h2, w_ref, carry_ref):
    tm = h2.shape[0]
    logits = jnp.dot(h2.astype(BF16), w_ref[...], preferred_element_type=F32)
    lane = lax.broadcasted_iota(I32, (tm, V7X_LANES), 1)
    is_g = lane < N_GROUPS
    gl = jnp.where(is_g, logits, NEG_INF)
    gmax = jnp.max(gl, axis=1, keepdims=True)
    g_sel = jnp.min(jnp.where(gl == gmax, lane, V7X_LANES), axis=1, keepdims=True)
    gsum = jnp.sum(jnp.where(is_g, jnp.exp(gl - gmax), 0.0), axis=1, keepdims=True)
    p_group = 1.0 / gsum
    lo = N_GROUPS + EXPERTS_PER_GROUP * g_sel
    emask = (lane >= lo) & (lane < lo + EXPERTS_PER_GROUP)
    el = jnp.where(emask, logits, NEG_INF)
    e1 = jnp.max(el, axis=1, keepdims=True)
    i1 = jnp.min(jnp.where((el == e1) & emask, lane, V7X_LANES), axis=1, keepdims=True)
    emask2 = emask & (lane != i1)
    el2 = jnp.where(emask2, logits, NEG_INF)
    e2 = jnp.max(el2, axis=1, keepdims=True)
    i2 = jnp.min(jnp.where((el2 == e2) & emask2, lane, V7X_LANES), axis=1, keepdims=True)
    r = jnp.exp(e2 - e1)
    w1 = p_group / (1.0 + r)
    w2 = p_group * r / (1.0 + r)
    eid1 = i1 - N_GROUPS
    eid2 = i2 - N_GROUPS
    oh1 = jnp.where(lane == eid1, 1.0, 0.0)
    oh2 = jnp.where(lane == eid2, 1.0, 0.0)
    cnt = oh1 + oh2
    tri = jnp.where(lax.broadcasted_iota(I32, (tm, tm), 0) > lax.broadcasted_iota(I32, (tm, tm), 1), 1.0, 0.0)
    before = jnp.dot(tri.astype(BF16), cnt.astype(BF16), preferred_element_type=F32) + carry_ref[...]
    rank1 = jnp.sum(oh1 * before, axis=1, keepdims=True)
    rank2 = jnp.sum(oh2 * before, axis=1, keepdims=True)
    carry_ref[...] = carry_ref[...] + jnp.sum(cnt, axis=0, keepdims=True)
    info = jnp.zeros((tm, V7X_LANES), F32)
    for k, val in enumerate((eid1.astype(F32), eid2.astype(F32), w1, w2, rank1, rank2)):
        info = jnp.where(lane == k, val, info)
    return info


def _row_copy(src_ref, src_row, dst_ref, dst_row, sem):
    return pltpu.make_async_copy(src_ref.at[pl.ds(src_row, 1), :], dst_ref.at[pl.ds(dst_row, 1), :], sem)


ROW_DMA_UNROLL = 8


def _dispatch_kernel(dest_ref, pend_ref, h_ref, xs_ref, zero_ref, sem):
    tm = h_ref.shape[0]
    step = pl.program_id(0)
    base = step * tm

    @pl.when(step == 0)
    def _():
        zero_ref[...] = jnp.zeros_like(zero_ref)

        def tail(e):
            return pltpu.make_async_copy(
                zero_ref, xs_ref.at[pl.ds(pl.multiple_of(pend_ref[e] - MOE_ROWS, MOE_ROWS), MOE_ROWS), :], sem)

        def nonempty(e):
            return pend_ref[e] > (pend_ref[e - 1] if e else 0)

        def unused(b):
            return pltpu.make_async_copy(
                zero_ref, xs_ref.at[pl.ds(pl.multiple_of(b * MOE_ROWS, MOE_ROWS), MOE_ROWS), :], sem)

        first_unused = pend_ref[N_EXPERTS - 1] // MOE_ROWS
        n_blocks = xs_ref.shape[0] // MOE_ROWS
        for e in range(N_EXPERTS):
            pl.when(nonempty(e))(lambda e=e: tail(e).start())
        lax.fori_loop(first_unused, n_blocks, lambda b, c: (unused(b).start(), c)[1], 0)
        for e in range(N_EXPERTS):
            pl.when(nonempty(e))(lambda e=e: tail(e).wait())
        lax.fori_loop(first_unused, n_blocks, lambda b, c: (unused(b).wait(), c)[1], 0)

    def issue(rb, c):
        for u in range(ROW_DMA_UNROLL):
            r = rb * ROW_DMA_UNROLL + u
            for k in range(TOPK_IN_GROUP):
                _row_copy(h_ref, r, xs_ref, dest_ref[(base + r) * TOPK_IN_GROUP + k], sem).start(priority=k % 2)
        return c

    lax.fori_loop(0, tm // ROW_DMA_UNROLL, issue, 0)
    for _ in range(tm * TOPK_IN_GROUP):
        _row_copy(h_ref, 0, xs_ref, 0, sem).wait()


def _dispatch(dest_flat, pad_end, h2p, m_pad, tm=256):
    T, C = h2p.shape
    return pl.pallas_call(
        _dispatch_kernel,
        out_shape=jax.ShapeDtypeStruct((m_pad, C), h2p.dtype),
        grid_spec=pltpu.PrefetchScalarGridSpec(
            num_scalar_prefetch=2,
            grid=(T // tm,),
            in_specs=[pl.BlockSpec((tm, C), lambda i, d, pe: (i, 0))],
            out_specs=pl.BlockSpec(memory_space=pl.ANY),
            scratch_shapes=[pltpu.VMEM((MOE_ROWS, C), h2p.dtype), pltpu.SemaphoreType.DMA(())],
        ),
        compiler_params=_cparams(1, 6 * _nbytes((tm, C), h2p.dtype)),
        name="dispatch",
    )(dest_flat, pad_end, h2p)


def _expert_kernel(be_ref, nu_ref, xs_ref, wg_ref, wu_ref, wd_ref, y_ref, wgb_ref, wub_ref, wdb_ref):
    b = pl.program_id(0)
    used = b < nu_ref[0]

    @pl.when(used & ((b == 0) | (be_ref[b] != be_ref[jnp.maximum(b - 1, 0)])))
    def _():
        wgb_ref[...] = wg_ref[0].astype(BF16)
        wub_ref[...] = wu_ref[0].astype(BF16)
        wdb_ref[...] = wd_ref[0].astype(BF16)

    @pl.when(used)
    def _():
        x = _unpack_bf16_pairs(xs_ref[...]).astype(BF16)
        a = jnp.dot(x, wgb_ref[...], preferred_element_type=F32)
        u = jnp.dot(x, wub_ref[...], preferred_element_type=F32)
        hm = (a * jax.nn.sigmoid(a) * u).astype(BF16)
        y_ref[...] = _pack_bf16_pairs(jnp.dot(hm, wdb_ref[...], preferred_element_type=F32))

    @pl.when(jnp.logical_not(used))
    def _():
        y_ref[...] = jnp.zeros_like(y_ref)


def _experts(block_expert, n_used, xs, w_gate, w_up, w_down):
    m_pad, C = xs.shape
    D = 2 * C
    R, Fd = MOE_ROWS, D_EXPERT
    vmem = 2 * (2 * _nbytes((R, C), xs.dtype) + 3 * _nbytes((D, Fd), F32)) + 3 * _nbytes((D, Fd), BF16)
    vmem += 6 * _nbytes((R, D), F32)
    return pl.pallas_call(
        _expert_kernel,
        out_shape=jax.ShapeDtypeStruct((m_pad, C), xs.dtype),
        grid_spec=pltpu.PrefetchScalarGridSpec(
            num_scalar_prefetch=2,
            grid=(m_pad // R,),
            in_specs=[
                pl.BlockSpec((R, C), lambda b, be, nu: (jnp.minimum(b, jnp.maximum(nu[0] - 1, 0)), 0)),
                pl.BlockSpec((1, D, Fd), lambda b, be, nu: (be[b], 0, 0)),
                pl.BlockSpec((1, D, Fd), lambda b, be, nu: (be[b], 0, 0)),
                pl.BlockSpec((1, Fd, D), lambda b, be, nu: (be[b], 0, 0)),
            ],
            out_specs=pl.BlockSpec((R, C), lambda b, be, nu: (b, 0)),
            scratch_shapes=[pltpu.VMEM((D, Fd), BF16), pltpu.VMEM((D, Fd), BF16), pltpu.VMEM((Fd, D), BF16)],
        ),
        compiler_params=_cparams(1, vmem),
        name="experts",
    )(block_expert, n_used, xs, w_gate, w_up, w_down)


def _combine_kernel(dest_ref, x1_ref, info_ref, yb_ref, p_ref, gp_ref, wpg_ref, wpp_ref, gf_ref, o_ref, ybuf, sem):
    tm, D = x1_ref.shape
    step = pl.program_id(0)
    n_tiles = pl.num_programs(0) - 2
    K = TOPK_IN_GROUP
    GROUPS = 8

    def fetch(tile, group=None):
        slot = tile % 2
        rows = range(tm) if group is None else range(group * tm // GROUPS, (group + 1) * tm // GROUPS)
        for r in rows:
            for k in range(K):
                _row_copy(yb_ref, dest_ref[(tile * tm + r) * K + k], ybuf.at[slot * K + k], r,
                          sem.at[slot]).start(priority=k % 2)

    def drain(tile):
        slot = tile % 2
        for _ in range(tm * K):
            _row_copy(yb_ref, 0, ybuf.at[0], 0, sem.at[slot]).wait()

    @pl.when(step == 0)
    def _():
        fetch(step)

    @pl.when(step > 0)
    def _():
        drain(step - 1)

    @pl.when((step > 0) & (step <= n_tiles))
    def _():
        tile = step - 1
        slot = tile % 2
        info = info_ref[...]
        x2 = (x1_ref[...] + info[:, 2:3] * _unpack_bf16_pairs(ybuf[slot * K])
              + info[:, 3:4] * _unpack_bf16_pairs(ybuf[slot * K + 1]))
        hp = _rms(x2, gp_ref[...]).astype(BF16)
        pp = jnp.dot(p_ref[...].astype(BF16), wpp_ref[...], preferred_element_type=F32)
        cw = D // GROUPS
        zs = []
        for c in range(GROUPS):
            fetch(step, c)
            zs.append(jnp.dot(hp, wpg_ref[:, c * cw:(c + 1) * cw], preferred_element_type=F32))
        x3 = x2 + jax.nn.sigmoid(jnp.concatenate(zs, axis=1)) * pp
        o_ref[...] = _rms(x3, gf_ref[...])


def _combine(dest_flat, x1, info, yb, p, g_ple, w_ple_gate, w_ple_proj, g_final, tm=256):
    T, D = x1.shape
    n_tiles = T // tm
    dest_padded = jnp.concatenate([dest_flat, jnp.zeros((tm * TOPK_IN_GROUP,), I32)])
    vmem = 4 * _nbytes((tm, D), F32) + _nbytes((D, D), BF16) + _nbytes((PLE_DIM, D), BF16)
    vmem += 2 * _nbytes((tm, PLE_DIM), F32) + 2 * TOPK_IN_GROUP * _nbytes((tm, D // 2), yb.dtype)
    vmem += 6 * _nbytes((tm, D), F32)
    tile = lambda i, d: (jnp.clip(i - 1, 0, n_tiles - 1), 0)
    resident = pl.Buffered(1)
    return pl.pallas_call(
        _combine_kernel,
        out_shape=jax.ShapeDtypeStruct((T, D), F32),
        grid_spec=pltpu.PrefetchScalarGridSpec(
            num_scalar_prefetch=1,
            grid=(n_tiles + 2,),
            in_specs=[
                pl.BlockSpec((tm, D), tile),
                pl.BlockSpec((tm, V7X_LANES), tile),
                pl.BlockSpec(memory_space=pl.ANY),
                pl.BlockSpec((tm, PLE_DIM), tile),
                pl.BlockSpec((1, D), lambda i, d: (0, 0)),
                pl.BlockSpec((D, D), lambda i, d: (0, 0), pipeline_mode=resident),
                pl.BlockSpec((PLE_DIM, D), lambda i, d: (0, 0), pipeline_mode=resident),
                pl.BlockSpec((1, D), lambda i, d: (0, 0)),
            ],
            out_specs=pl.BlockSpec((tm, D), tile),
            scratch_shapes=[pltpu.VMEM((2 * TOPK_IN_GROUP, tm, yb.shape[1]), yb.dtype),
                            pltpu.SemaphoreType.DMA((2,))],
        ),
        compiler_params=_cparams(1, vmem),
        name="combine",
    )(dest_padded, x1, info, yb, p, g_ple.reshape(1, D), w_ple_gate, w_ple_proj, g_final.reshape(1, D))


def _rope_tables(T):
    half, BS = MOBA_HEAD_DIM // 2, MOBA_BLOCK
    inv_freq = ROPE_THETA ** (-jnp.arange(half, dtype=F32) / half)
    ang_a = (jnp.arange(T // BS, dtype=F32) * BS)[:, None, None] * inv_freq
    ang_b = jnp.arange(BS, dtype=F32)[None, :, None] * inv_freq
    ca, sa, cb, sb = jnp.cos(ang_a), jnp.sin(ang_a), jnp.cos(ang_b), jnp.sin(ang_b)
    cos = (ca * cb - sa * sb).reshape(T, half)
    sin = (sa * cb + ca * sb).reshape(T, half)
    return jnp.concatenate([cos, cos], axis=1), jnp.concatenate([-sin, sin], axis=1)


def _mixers(x2d, g_mix, w_in, lb, hgrn_norm_g):
    T = x2d.shape[0]
    W = HGRN_WIDTH
    h = _rmsnorm(x2d, g_mix, BF16)
    w = w_in
    cos, sin = _rope_tables(T)
    log_lb = jnp.log(lb).reshape(1, W)
    log_1m = jnp.log1p(-lb).reshape(1, W)
    hq = _proj(h, w, 0 * W, W, _ep_silu, BF16)
    logf = _proj(h, w, 1 * W, W, _ep_logf, F32, col_extras=(log_lb, log_1m))
    hi = _proj(h, w, 2 * W, W, _ep_identity, BF16)
    hog = _proj(h, w, 3 * W, W, _ep_silu, BF16)
    scale = MOBA_HEAD_DIM ** -0.5
    mqt = _proj_t(h, w, 4 * W, functools.partial(_ep_rope, scale=scale), row_extras=(cos, sin))
    blk = jnp.arange(T, dtype=I32)[:, None] // MOBA_BLOCK
    blk_onehot = (blk == jnp.arange(V7X_LANES, dtype=I32)[None, :]).astype(F32)
    mk = _proj(h, w, 5 * W, W, _ep_rope_aug, BF16, row_extras=(cos, sin, blk_onehot), widen=2)
    mvt = _proj_t(h, w, 6 * W, _ep_identity, ones_rows=MOBA_VT_ROWS - MOBA_HEAD_DIM)
    gates = _proj(h, w, 7 * W, 2 * D_MODEL, _ep_sigmoid, BF16)
    o_hgrn = _hgrn(hq, logf, hi, hog, hgrn_norm_g)
    o_moba = _moba(mqt, mk, mvt)
    return o_hgrn, o_moba, gates


def _moe_plan(info, cnt, T):
    R = MOE_ROWS
    eid = info[:, 0:TOPK_IN_GROUP].astype(I32)
    rank = info[:, 4:4 + TOPK_IN_GROUP].astype(I32)
    counts = cnt[0, :N_EXPERTS].astype(I32)
    padded = (counts + R - 1) // R * R
    pad_end = jnp.cumsum(padded)
    pad_start = pad_end - padded
    dest = (pad_start[eid] + rank).reshape(-1)
    n_blocks = (T * TOPK_IN_GROUP) // R + N_EXPERTS
    block_expert = jnp.minimum(
        jnp.searchsorted(pad_end, jnp.arange(n_blocks, dtype=I32) * R, side="right"), N_EXPERTS - 1).astype(I32)
    n_used = (pad_end[-1:] // R).astype(I32)
    return dest, pad_end.astype(I32), block_expert, n_used, n_blocks * R


def kernel(x, p, norm_mix_g, w_in, hgrn_lb_raw, hgrn_norm_g, w_up_hgrn, w_up_moba, w_out, norm_ffn_g,
           w_router_group, w_router_expert, w_exp_gate, w_exp_up, w_exp_down, norm_ple_g, w_ple_gate,
           w_ple_proj, norm_final_g):
    B, T, D = x.shape
    assert B == 1 and D == D_MODEL and w_in.shape[0] == 1 and T % (4 * MOBA_BLOCK) == 0
    lower_bounds = jnp.cumsum(jax.nn.softmax(hgrn_lb_raw.astype(F32), axis=0), axis=0)
    x2d = x.reshape(T, D)
    o_hgrn, o_moba, gates = _mixers(x2d, norm_mix_g[0], w_in[0], lower_bounds[0], hgrn_norm_g[0])
    merged = _merge(o_hgrn, o_moba, gates, w_up_hgrn[0].astype(BF16), w_up_moba[0].astype(BF16))
    w_router = jnp.pad(jnp.concatenate([w_router_group[0], w_router_expert[0]], axis=1),
                       ((0, 0), (0, V7X_LANES - N_GROUPS - N_EXPERTS))).astype(BF16)
    x1, h2p, info, cnt = _outproj(merged, x2d, w_out[0].astype(BF16), norm_ffn_g[0], w_router)
    dest, pad_end, block_expert, n_used, m_pad = _moe_plan(info, cnt, T)
    xs = _dispatch(dest, pad_end, h2p, m_pad)
    yb = _experts(block_expert, n_used, xs, w_exp_gate[0], w_exp_up[0], w_exp_down[0])
    out = _combine(dest, x1, info, yb, p[0].reshape(T, PLE_DIM), norm_ple_g[0], w_ple_gate[0].astype(BF16),
                   w_ple_proj[0].astype(BF16), norm_final_g)
    return out.reshape(B, T, D)
```

```python
import functools

import jax
import jax.numpy as jnp
from jax import lax
from jax.experimental import pallas as pl
from jax.experimental.pallas import tpu as pltpu

F32 = jnp.float32
BF16 = jnp.bfloat16
I32 = jnp.int32

D_MODEL = 2048
PLE_DIM = 256
HGRN_HEADS = 8
HGRN_HEAD_DIM = 128
HGRN_WIDTH = HGRN_HEADS * HGRN_HEAD_DIM
MOBA_HEADS = 8
MOBA_HEAD_DIM = 128
MOBA_WIDTH = MOBA_HEADS * MOBA_HEAD_DIM
MOBA_BLOCK = 256
MOBA_TOPK = 3
ROPE_THETA = 10000.0
N_GROUPS = 4
EXPERTS_PER_GROUP = 8
N_EXPERTS = N_GROUPS * EXPERTS_PER_GROUP
TOPK_IN_GROUP = 2
D_EXPERT = 512
EPS = 1e-6
NEG_INF = -1e30

V7X_LANES = 128
V7X_SUBLANES = 8
V7X_VMEM_BUDGET_BYTES = 56 * 1024 * 1024

HGRN_CHUNK = 128
MOE_ROWS = 256


def _cparams(n_grid, vmem_bytes):
    return pltpu.CompilerParams(
        dimension_semantics=("arbitrary",) * n_grid,
        vmem_limit_bytes=int(min(max(vmem_bytes, 16 * 1024 * 1024), V7X_VMEM_BUDGET_BYTES)),
    )


def _nbytes(shape, dtype):
    n = 1
    for s in shape:
        n *= s
    return n * jnp.dtype(dtype).itemsize


def _rms(x, g):
    ms = jnp.mean(x * x, axis=-1, keepdims=True)
    return x * lax.rsqrt(ms + EPS) * g


def _rmsnorm_kernel(x_ref, g_ref, o_ref):
    o_ref[...] = _rms(x_ref[...], g_ref[...]).astype(o_ref.dtype)


def _rmsnorm(x, g, out_dtype, tm=512):
    T, D = x.shape
    return pl.pallas_call(
        _rmsnorm_kernel,
        out_shape=jax.ShapeDtypeStruct((T, D), out_dtype),
        grid=(T // tm,),
        in_specs=[pl.BlockSpec((tm, D), lambda i: (i, 0)), pl.BlockSpec((1, D), lambda i: (0, 0))],
        out_specs=pl.BlockSpec((tm, D), lambda i: (i, 0)),
        compiler_params=_cparams(1, 4 * _nbytes((tm, D), F32)),
        name="rmsnorm",
    )(x, g.reshape(1, D))


def _ep_identity(acc):
    return acc


def _ep_silu(acc):
    return acc * jax.nn.sigmoid(acc)


def _ep_sigmoid(acc):
    return jax.nn.sigmoid(acc)


def _ep_logf(acc, la_ref, lc_ref):
    ls = jnp.minimum(acc, 0.0) - jnp.log(1.0 + jnp.exp(-jnp.abs(acc)))
    u = la_ref[...]
    v = lc_ref[...] + ls
    return jnp.maximum(u, v) + jnp.log(1.0 + jnp.exp(-jnp.abs(u - v)))


def _ep_rope(acc, cos_ref, sin_ref, *, scale):
    cos = cos_ref[...]
    sin = sin_ref[...]
    outs = []
    for hh in range(acc.shape[1] // MOBA_HEAD_DIM):
        a = acc[:, hh * MOBA_HEAD_DIM:(hh + 1) * MOBA_HEAD_DIM]
        r = pltpu.roll(a, MOBA_HEAD_DIM // 2, axis=1)
        outs.append((a * cos + r * sin) * scale)
    return jnp.concatenate(outs, axis=1)


def _ep_rope_aug(acc, cos_ref, sin_ref, oh_ref):
    cos = cos_ref[...]
    sin = sin_ref[...]
    oh = oh_ref[...]
    outs = []
    for hh in range(acc.shape[1] // MOBA_HEAD_DIM):
        a = acc[:, hh * MOBA_HEAD_DIM:(hh + 1) * MOBA_HEAD_DIM]
        outs.append(a * cos + pltpu.roll(a, MOBA_HEAD_DIM // 2, axis=1) * sin)
        outs.append(oh)
    return jnp.concatenate(outs, axis=1)


def _cast_weight_once(w_ref, wb_ref, row_axis):
    @pl.when(pl.program_id(row_axis) == 0)
    def _():
        wb_ref[...] = w_ref[...].astype(wb_ref.dtype)


def _proj_kernel(h_ref, w_ref, *refs, epilogue):
    *extra, o_ref, wb_ref = refs
    _cast_weight_once(w_ref, wb_ref, 1)
    acc = jnp.dot(h_ref[...], wb_ref[...], preferred_element_type=F32)
    o_ref[...] = epilogue(acc, *extra).astype(o_ref.dtype)


def _proj(h, w, col0, ncols, epilogue, out_dtype, row_extras=(), col_extras=(), tm=1024, tn=1024, widen=1):
    T, K = h.shape
    tn = min(tn, ncols)
    tm = min(tm, T)
    cb = col0 // tn
    otn = widen * tn
    in_specs = [
        pl.BlockSpec((tm, K), lambda j, i: (i, 0)),
        pl.BlockSpec((K, tn), lambda j, i: (0, cb + j)),
    ]
    for e in row_extras:
        in_specs.append(pl.BlockSpec((tm, e.shape[1]), lambda j, i: (i, 0)))
    for e in col_extras:
        in_specs.append(pl.BlockSpec((1, tn), lambda j, i: (0, j)))
    vmem = 2 * (_nbytes((tm, K), h.dtype) + _nbytes((K, tn), w.dtype) + _nbytes((tm, otn), out_dtype))
    vmem += 3 * _nbytes((tm, otn), F32) + _nbytes((K, tn), BF16)
    return pl.pallas_call(
        functools.partial(_proj_kernel, epilogue=epilogue),
        out_shape=jax.ShapeDtypeStruct((T, widen * ncols), out_dtype),
        grid=(ncols // tn, T // tm),
        in_specs=in_specs,
        out_specs=pl.BlockSpec((tm, otn), lambda j, i: (i, j)),
        scratch_shapes=[pltpu.VMEM((K, tn), BF16)],
        compiler_params=_cparams(2, vmem),
        name="proj",
    )(h, w, *row_extras, *col_extras)


MOBA_VT_ROWS = MOBA_HEAD_DIM + 16
MOBA_BLOCKS_PER_STEP = 4


def _proj_t_kernel(h_ref, w_ref, *refs, epilogue, ones_rows):
    *extra, o_ref, wb_ref = refs
    BS, HD = MOBA_BLOCK, MOBA_HEAD_DIM
    _cast_weight_once(w_ref, wb_ref, 0)
    acc = epilogue(jnp.dot(h_ref[...], wb_ref[...], preferred_element_type=F32), *extra)
    ones = jnp.ones((ones_rows, BS), F32) if ones_rows else None
    for b in range(acc.shape[0] // BS):
        parts = []
        for hh in range(acc.shape[1] // HD):
            parts.append(acc[b * BS:(b + 1) * BS, hh * HD:(hh + 1) * HD].T)
            if ones_rows:
                parts.append(ones)
        o_ref[b] = jnp.concatenate(parts, axis=0).astype(o_ref.dtype)


def _proj_t(h, w, col0, epilogue, row_extras=(), ones_rows=0, tm=1024):
    T, K = h.shape
    tn = MOBA_WIDTH
    tm = min(tm, T)
    cb = col0 // tn
    rows = MOBA_HEADS * (MOBA_HEAD_DIM + ones_rows)
    in_specs = [pl.BlockSpec((tm, K), lambda i: (i, 0)), pl.BlockSpec((K, tn), lambda i: (0, cb))]
    for e in row_extras:
        in_specs.append(pl.BlockSpec((tm, e.shape[1]), lambda i: (i, 0)))
    vmem = 2 * (_nbytes((tm, K), h.dtype) + _nbytes((K, tn), w.dtype) + _nbytes((tm, 2 * tn), BF16))
    vmem += 4 * _nbytes((tm, tn), F32) + _nbytes((K, tn), BF16)
    return pl.pallas_call(
        functools.partial(_proj_t_kernel, epilogue=epilogue, ones_rows=ones_rows),
        out_shape=jax.ShapeDtypeStruct((T // MOBA_BLOCK, rows, MOBA_BLOCK), BF16),
        grid=(T // tm,),
        in_specs=in_specs,
        out_specs=pl.BlockSpec((tm // MOBA_BLOCK, rows, MOBA_BLOCK), lambda i: (i, 0, 0)),
        scratch_shapes=[pltpu.VMEM((K, tn), BF16)],
        compiler_params=_cparams(1, vmem),
        name="proj_t",
    )(h, w, *row_extras)


def _hgrn_kernel(q_ref, g_ref, v_ref, og_ref, ng_ref, o_ref, st_ref):
    W, HD, NH, C, S = HGRN_WIDTH, HGRN_HEAD_DIM, HGRN_HEADS, HGRN_CHUNK, V7X_SUBLANES
    J = C // S

    @pl.when(pl.program_id(0) == 0)
    def _():
        st_ref[...] = jnp.zeros_like(st_ref)

    def r3(x):
        return x.astype(F32).reshape(J, S, W)

    def sub_bcast(x3, r):
        return jnp.broadcast_to(x3[:, r:r + 1, :], x3.shape)

    g3, q3, v3 = r3(g_ref[...]), r3(q_ref[...]), r3(v_ref[...])
    sub = lax.broadcasted_iota(I32, (1, S, W), 1)

    c3 = g3
    for s in (1, 2, 4):
        c3 = c3 + jnp.where(sub >= s, pltpu.roll(c3, s, axis=1), 0.0)
    run = jnp.zeros((1, 1, W), F32)
    carry = []
    for j in range(J):
        carry.append(run)
        run = run + c3[j:j + 1, S - 1:S, :]
    b3 = c3 + jnp.concatenate(carry, axis=0)
    bC = run

    k3 = 1.0 - jnp.exp(g3)
    qe3 = q3 * jnp.exp(b3)
    ks3 = k3 * jnp.exp(bC - b3)

    levels = [(0, q3, k3)]
    ref1 = jnp.where(sub % 2 == 0, b3, pltpu.roll(b3, 1, axis=1))
    ref2 = jnp.where(sub < 4, sub_bcast(b3, 1), sub_bcast(b3, 5))
    ref4 = sub_bcast(b3, 3)
    for lvl, (ref, upper) in enumerate(((ref1, sub % 2 == 1), (ref2, sub % 4 >= 2), (ref4, sub >= 4)), start=1):
        e = jnp.exp(-jnp.abs(b3 - ref))
        levels.append((lvl, jnp.where(upper, q3 * e, 0.0), jnp.where(upper, 0.0, k3 * e)))
    zero_group = jnp.zeros((1, S, W), F32)
    for lvl, half in enumerate((1, 2, 4, 8), start=4):
        qparts, kparts = [], []
        for j in range(J):
            jr = (j // (2 * half)) * (2 * half) + half - 1
            ref = b3[jr:jr + 1, S - 1:S, :]
            if (j % (2 * half)) >= half:
                qparts.append(q3[j:j + 1] * jnp.exp(b3[j:j + 1] - ref))
                kparts.append(zero_group)
            else:
                qparts.append(zero_group)
                kparts.append(k3[j:j + 1] * jnp.exp(ref - b3[j:j + 1]))
        levels.append((lvl, jnp.concatenate(qparts, axis=0), jnp.concatenate(kparts, axis=0)))

    tr = lax.broadcasted_iota(I32, (C, C), 0)
    tc = lax.broadcasted_iota(I32, (C, C), 1)
    xr = tr ^ tc
    code = jnp.zeros((C, C), I32)
    for lvl in range(1, 8):
        code = jnp.where(xr >= (1 << (lvl - 1)), lvl, code)
    code = jnp.where(tc > tr, -1, code)

    def mat(x3, h):
        return x3.reshape(C, W)[:, h * HD:(h + 1) * HD].astype(BF16)

    nt = (((1,), (1,)), ((), ()))
    tn = (((0,), (0,)), ((), ()))
    ebc = jnp.exp(bC).reshape(1, W)
    ng = ng_ref[...]
    for h in range(NH):
        a_mat = jnp.zeros((C, C), F32)
        for lvl, qr, kr in levels:
            s = lax.dot_general(mat(qr, h), mat(kr, h), nt, preferred_element_type=F32)
            a_mat = jnp.where(code == lvl, s, a_mat)
        vh = mat(v3, h)
        st = st_ref[h]
        o = jnp.dot(a_mat.astype(BF16), vh, preferred_element_type=F32)
        o = o + lax.dot_general(mat(qe3, h), st.astype(BF16), nt, preferred_element_type=F32)
        o = _rms(o, ng) * og_ref[:, h * HD:(h + 1) * HD].astype(F32)
        o_ref[:, h * HD:(h + 1) * HD] = o.astype(o_ref.dtype)
        st_ref[h] = st * ebc[:, h * HD:(h + 1) * HD] + lax.dot_general(
            vh, mat(ks3, h), tn, preferred_element_type=F32)


def _hgrn(q, logf, v, og, norm_g):
    T, W = q.shape
    C = HGRN_CHUNK
    blk = pl.BlockSpec((C, W), lambda c: (c, 0))
    vmem = 64 * _nbytes((C, W), F32)
    return pl.pallas_call(
        _hgrn_kernel,
        out_shape=jax.ShapeDtypeStruct((T, W), BF16),
        grid=(T // C,),
        in_specs=[blk, blk, blk, blk, pl.BlockSpec((1, HGRN_HEAD_DIM), lambda c: (0, 0))],
        out_specs=blk,
        scratch_shapes=[pltpu.VMEM((HGRN_HEADS, HGRN_HEAD_DIM, HGRN_HEAD_DIM), F32)],
        compiler_params=_cparams(1, vmem),
        name="hgrn",
    )(q, logf, v, og, norm_g.reshape(1, HGRN_HEAD_DIM))


def _moba_kernel(qt_ref, k_ref, vt_ref, o_ref, km_ref):
    BS, HD, VR = MOBA_BLOCK, MOBA_HEAD_DIM, MOBA_VT_ROWS
    T = k_ref.shape[0]
    NB = T // BS
    G = o_ref.shape[1] // HD
    cur = pl.program_id(1)

    @pl.when(cur == 0)
    def _():
        for g in range(G):
            kf = k_ref[:, 2 * g * HD:(2 * g + 1) * HD].astype(F32).reshape(NB, BS, HD)
            km_ref[g] = jnp.sum(kf, axis=1) * (1.0 / BS)

    blk = lax.broadcasted_iota(I32, (NB, BS), 0)
    key = lax.broadcasted_iota(I32, (BS, BS), 0)
    qry = lax.broadcasted_iota(I32, (BS, BS), 1)
    pad = jnp.zeros((V7X_LANES - NB, BS), F32)
    r0 = pl.multiple_of(cur * BS, BS)
    qts = [qt_ref[0, g * HD:(g + 1) * HD, :] for g in range(G)]
    gts = [jnp.dot(km_ref[g].astype(BF16), qts[g], preferred_element_type=F32) for g in range(G)]
    s_own = [jnp.dot(k_ref[pl.ds(r0, BS), 2 * g * HD:(2 * g + 1) * HD], qts[g], preferred_element_type=F32)
             for g in range(G)]
    qcs = []
    for g in range(G):
        gt = jnp.where(blk < cur, gts[g], NEG_INF)
        sel = jnp.zeros((NB, BS), jnp.bool_)
        for _ in range(MOBA_TOPK):
            mx = jnp.max(gt, axis=0, keepdims=True)
            idx = jnp.min(jnp.where(gt == mx, blk, NB), axis=0, keepdims=True)
            pick = (blk == idx) & (mx > 0.5 * NEG_INF)
            sel = sel | pick
            gt = jnp.where(pick, NEG_INF, gt)
        pen = jnp.concatenate([jnp.where(sel, 0.0, NEG_INF), pad], axis=0).astype(BF16)
        qcs.append(jnp.concatenate([qts[g], pen], axis=0))

    def scores(n):
        rn = pl.multiple_of(n * BS, BS)
        return tuple(jnp.dot(k_ref[pl.ds(rn, BS), 2 * g * HD:(2 * g + 2) * HD], qcs[g], preferred_element_type=F32)
                     for g in range(G))

    ms, ps = [], []
    for g in range(G):
        s = jnp.where(key <= qry, s_own[g], NEG_INF)
        m0 = jnp.max(s, axis=0, keepdims=True)
        ms.append(m0)
        ps.append(jnp.exp(s - m0).astype(BF16))
    accs = [jnp.dot(vt_ref[cur, g * VR:(g + 1) * VR, :], ps[g], preferred_element_type=F32)
            for g in range(G)]

    KB = MOBA_BLOCKS_PER_STEP

    def body(c, carry):
        ms, accs = carry
        r = pl.multiple_of(c * KB * BS, KB * BS)
        sns = [jnp.dot(k_ref[pl.ds(r, KB * BS), 2 * g * HD:(2 * g + 2) * HD], qcs[g], preferred_element_type=F32)
               for g in range(G)]
        new_ms, alphas, pns = [], [], []
        for g in range(G):
            m_new = jnp.maximum(ms[g], jnp.max(sns[g], axis=0, keepdims=True))
            alphas.append(jnp.exp(ms[g] - m_new))
            pns.append(jnp.exp(sns[g] - m_new).astype(BF16))
            new_ms.append(m_new)
        new_accs = []
        for g in range(G):
            pv = alphas[g] * accs[g]
            for j in range(KB):
                pv = pv + jnp.dot(vt_ref[KB * c + j, g * VR:(g + 1) * VR, :], pns[g][j * BS:(j + 1) * BS],
                                  preferred_element_type=F32)
            new_accs.append(pv)
        return tuple(new_ms), tuple(new_accs)

    _, accs = lax.fori_loop(0, (cur + KB - 1) // KB, body, (tuple(ms), tuple(accs)))
    for g in range(G):
        ot = accs[g][:HD, :] / accs[g][HD:HD + 1, :]
        o_ref[:, g * HD:(g + 1) * HD] = ot.T.astype(o_ref.dtype)


def _moba(mqt, mk_aug, mvt, heads_per_step=4):
    T = mk_aug.shape[0]
    BS, HD, G, VR = MOBA_BLOCK, MOBA_HEAD_DIM, heads_per_step, MOBA_VT_ROWS
    NB = T // BS
    vmem = _nbytes((T, 2 * G * HD), BF16) + _nbytes((NB, G * VR, BS), BF16) + 8 * _nbytes((BS, G * HD), BF16)
    vmem += 16 * G * _nbytes((BS, BS), F32)
    resident = pl.Buffered(1)
    return pl.pallas_call(
        _moba_kernel,
        out_shape=jax.ShapeDtypeStruct((T, MOBA_WIDTH), BF16),
        grid=(MOBA_HEADS // G, NB),
        in_specs=[
            pl.BlockSpec((1, G * HD, BS), lambda h, i: (i, h, 0)),
            pl.BlockSpec((T, 2 * G * HD), lambda h, i: (0, h), pipeline_mode=resident),
            pl.BlockSpec((NB, G * VR, BS), lambda h, i: (0, h, 0), pipeline_mode=resident),
        ],
        out_specs=pl.BlockSpec((BS, G * HD), lambda h, i: (i, h)),
        scratch_shapes=[pltpu.VMEM((G, NB, HD), F32)],
        compiler_params=_cparams(2, vmem),
        name="moba",
    )(mqt, mk_aug, mvt)


def _merge_kernel(oh_ref, om_ref, ga_ref, gb_ref, wh_ref, wm_ref, o_ref):
    a = jnp.dot(oh_ref[...], wh_ref[...], preferred_element_type=F32)
    b = jnp.dot(om_ref[...], wm_ref[...], preferred_element_type=F32)
    o_ref[...] = (ga_ref[...].astype(F32) * a + gb_ref[...].astype(F32) * b).astype(o_ref.dtype)


def _merge(o_hgrn, o_moba, gates, w_up_hgrn, w_up_moba, tm=512):
    T = o_hgrn.shape[0]
    D = D_MODEL
    vmem = 2 * (_nbytes((tm, HGRN_WIDTH), F32) + _nbytes((tm, MOBA_WIDTH), BF16) + 3 * _nbytes((tm, D), BF16)
                + 2 * _nbytes((HGRN_WIDTH, D), BF16)) + 3 * _nbytes((tm, D), F32)
    return pl.pallas_call(
        _merge_kernel,
        out_shape=jax.ShapeDtypeStruct((T, D), BF16),
        grid=(T // tm,),
        in_specs=[
            pl.BlockSpec((tm, HGRN_WIDTH), lambda i: (i, 0)),
            pl.BlockSpec((tm, MOBA_WIDTH), lambda i: (i, 0)),
            pl.BlockSpec((tm, D), lambda i: (i, 0)),
            pl.BlockSpec((tm, D), lambda i: (i, 1)),
            pl.BlockSpec((HGRN_WIDTH, D), lambda i: (0, 0)),
            pl.BlockSpec((MOBA_WIDTH, D), lambda i: (0, 0)),
        ],
        out_specs=pl.BlockSpec((tm, D), lambda i: (i, 0)),
        compiler_params=_cparams(1, vmem),
        name="merge",
    )(o_hgrn, o_moba, gates, gates, w_up_hgrn, w_up_moba)


def _pack_bf16_pairs(x):
    C = x.shape[1] // 2
    b = lax.bitcast_convert_type(x, jnp.uint32)
    r = (b + jnp.uint32(0x7FFF) + ((b >> 16) & jnp.uint32(1))) >> 16
    return r[:, :C] | (r[:, C:] << 16)


def _unpack_bf16_pairs(p):
    lo = lax.bitcast_convert_type(p << 16, F32)
    hi = lax.bitcast_convert_type(p & jnp.uint32(0xFFFF0000), F32)
    return jnp.concatenate([lo, hi], axis=1)


def _outproj_kernel(m_ref, x_ref, w_ref, g_ref, wr_ref, x1_ref, h2p_ref, info_ref, cnt_ref, prev_ref, carry_ref):
    step = pl.program_id(0)
    n_tiles = pl.num_programs(0) - 1

    @pl.when(step == 0)
    def _():
        carry_ref[...] = jnp.zeros_like(carry_ref)

    def finish_previous():
        h2 = _rms(prev_ref[...], g_ref[...])
        h2p_ref[...] = _pack_bf16_pairs(h2)
        info_ref[...] = _route(h2, wr_ref, carry_ref)
        cnt_ref[...] = carry_ref[...]

    def project():
        return x_ref[...] + jnp.dot(m_ref[...], w_ref[...], preferred_element_type=F32)

    @pl.when(step == 0)
    def _():
        x1 = project()
        x1_ref[...] = x1
        prev_ref[...] = x1

    @pl.when((step > 0) & (step < n_tiles))
    def _():
        x1 = project()
        finish_previous()
        x1_ref[...] = x1
        prev_ref[...] = x1

    @pl.when(step == n_tiles)
    def _():
        finish_previous()


def _outproj(merged, x, w_out, g_ffn, w_router, tm=512):
    T, D = x.shape
    n_tiles = T // tm
    vmem = 2 * (_nbytes((tm, D), BF16) + 3 * _nbytes((tm, D), F32) + _nbytes((D, D), BF16)) + 7 * _nbytes((tm, D), F32)
    cur = lambda i: (jnp.minimum(i, n_tiles - 1), 0)
    prev = lambda i: (jnp.maximum(i - 1, 0), 0)
    fixed = lambda i: (0, 0)
    return pl.pallas_call(
        _outproj_kernel,
        out_shape=(jax.ShapeDtypeStruct((T, D), F32), jax.ShapeDtypeStruct((T, D // 2), jnp.uint32),
                   jax.ShapeDtypeStruct((T, V7X_LANES), F32), jax.ShapeDtypeStruct((1, V7X_LANES), F32)),
        grid=(n_tiles + 1,),
        in_specs=[
            pl.BlockSpec((tm, D), cur),
            pl.BlockSpec((tm, D), cur),
            pl.BlockSpec((D, D), fixed, pipeline_mode=pl.Buffered(1)),
            pl.BlockSpec((1, D), fixed),
            pl.BlockSpec((D, V7X_LANES), fixed),
        ],
        out_specs=(pl.BlockSpec((tm, D), cur), pl.BlockSpec((tm, D // 2), prev), pl.BlockSpec((tm, V7X_LANES), prev),
                   pl.BlockSpec((1, V7X_LANES), fixed)),
        scratch_shapes=[pltpu.VMEM((tm, D), F32), pltpu.VMEM((1, V7X_LANES), F32)],
        compiler_params=_cparams(1, vmem),
        name="outproj",
    )(merged, x, w_out, g_ffn.reshape(1, D), w_router)


def _route(h2, w_ref, carry_ref):
    tm = h2.shape[0]
    logits = jnp.dot(h2.astype(BF16), w_ref[...], preferred_element_type=F32)
    lane = lax.broadcasted_iota(I32, (tm, V7X_LANES), 1)
    is_g = lane < N_GROUPS
    gl = jnp.where(is_g, logits, NEG_INF)
    gmax = jnp.max(gl, axis=1, keepdims=True)
    g_sel = jnp.min(jnp.where(gl == gmax, lane, V7X_LANES), axis=1, keepdims=True)
    gsum = jnp.sum(jnp.where(is_g, jnp.exp(gl - gmax), 0.0), axis=1, keepdims=True)
    p_group = 1.0 / gsum
    lo = N_GROUPS + EXPERTS_PER_GROUP * g_sel
    emask = (lane >= lo) & (lane < lo + EXPERTS_PER_GROUP)
    el = jnp.where(emask, logits, NEG_INF)
    e1 = jnp.max(el, axis=1, keepdims=True)
    i1 = jnp.min(jnp.where((el == e1) & emask, lane, V7X_LANES), axis=1, keepdims=True)
    emask2 = emask & (lane != i1)
    el2 = jnp.where(emask2, logits, NEG_INF)
    e2 = jnp.max(el2, axis=1, keepdims=True)
    i2 = jnp.min(jnp.where((el2 == e2) & emask2, lane, V7X_LANES), axis=1, keepdims=True)
    r = jnp.exp(e2 - e1)
    w1 = p_group / (1.0 + r)
    w2 = p_group * r / (1.0 + r)
    eid1 = i1 - N_GROUPS
    eid2 = i2 - N_GROUPS
    oh1 = jnp.where(lane == eid1, 1.0, 0.0)
    oh2 = jnp.where(lane == eid2, 1.0, 0.0)
    cnt = oh1 + oh2
    tri = jnp.where(lax.broadcasted_iota(I32, (tm, tm), 0) > lax.broadcasted_iota(I32, (tm, tm), 1), 1.0, 0.0)
    before = jnp.dot(tri.astype(BF16), cnt.astype(BF16), preferred_element_type=F32) + carry_ref[...]
    rank1 = jnp.sum(oh1 * before, axis=1, keepdims=True)
    rank2 = jnp.sum(oh2 * before, axis=1, keepdims=True)
    carry_ref[...] = carry_ref[...] + jnp.sum(cnt, axis=0, keepdims=True)
    info = jnp.zeros((tm, V7X_LANES), F32)
    for k, val in enumerate((eid1.astype(F32), eid2.astype(F32), w1, w2, rank1, rank2)):
        info = jnp.where(lane == k, val, info)
    return info


def _row_copy(src_ref, src_row, dst_ref, dst_row, sem):
    return pltpu.make_async_copy(src_ref.at[pl.ds(src_row, 1), :], dst_ref.at[pl.ds(dst_row, 1), :], sem)


ROW_DMA_UNROLL = 8
RANK_BITS = 16


def _slot(code, pstart_ref):
    return pstart_ref[lax.shift_right_logical(code, RANK_BITS)] + (code & ((1 << RANK_BITS) - 1))


def _dispatch_kernel(code_ref, pstart_ref, pend_ref, h_ref, xs_ref, zero_ref, sem):
    tm = h_ref.shape[0]
    step = pl.program_id(0)
    base = step * tm

    @pl.when(step == 0)
    def _():
        zero_ref[...] = jnp.zeros_like(zero_ref)

        def tail(e):
            return pltpu.make_async_copy(
                zero_ref, xs_ref.at[pl.ds(pl.multiple_of(pend_ref[e] - MOE_ROWS, MOE_ROWS), MOE_ROWS), :], sem)

        def nonempty(e):
            return pend_ref[e] > (pend_ref[e - 1] if e else 0)

        def unused(b):
            return pltpu.make_async_copy(
                zero_ref, xs_ref.at[pl.ds(pl.multiple_of(b * MOE_ROWS, MOE_ROWS), MOE_ROWS), :], sem)

        first_unused = pend_ref[N_EXPERTS - 1] // MOE_ROWS
        n_blocks = xs_ref.shape[0] // MOE_ROWS
        for e in range(N_EXPERTS):
            pl.when(nonempty(e))(lambda e=e: tail(e).start())
        lax.fori_loop(first_unused, n_blocks, lambda b, c: (unused(b).start(), c)[1], 0)
        for e in range(N_EXPERTS):
            pl.when(nonempty(e))(lambda e=e: tail(e).wait())
        lax.fori_loop(first_unused, n_blocks, lambda b, c: (unused(b).wait(), c)[1], 0)

    def issue(rb, c):
        for u in range(ROW_DMA_UNROLL):
            r = rb * ROW_DMA_UNROLL + u
            for k in range(TOPK_IN_GROUP):
                slot = _slot(code_ref[(base + r) * TOPK_IN_GROUP + k], pstart_ref)
                _row_copy(h_ref, r, xs_ref, slot, sem).start(priority=k % 2)
        return c

    lax.fori_loop(0, tm // ROW_DMA_UNROLL, issue, 0)
    for _ in range(tm * TOPK_IN_GROUP):
        _row_copy(h_ref, 0, xs_ref, 0, sem).wait()


def _dispatch(codes, pad_start, pad_end, h2p, m_pad, tm=256):
    T, C = h2p.shape
    return pl.pallas_call(
        _dispatch_kernel,
        out_shape=jax.ShapeDtypeStruct((m_pad, C), h2p.dtype),
        grid_spec=pltpu.PrefetchScalarGridSpec(
            num_scalar_prefetch=3,
            grid=(T // tm,),
            in_specs=[pl.BlockSpec((tm, C), lambda i, c, ps, pe: (i, 0))],
            out_specs=pl.BlockSpec(memory_space=pl.ANY),
            scratch_shapes=[pltpu.VMEM((MOE_ROWS, C), h2p.dtype), pltpu.SemaphoreType.DMA(())],
        ),
        compiler_params=_cparams(1, 6 * _nbytes((tm, C), h2p.dtype)),
        name="dispatch",
    )(codes, pad_start, pad_end, h2p)


def _expert_kernel(be_ref, nu_ref, xs_ref, wg_ref, wu_ref, wd_ref, y_ref, wgb_ref, wub_ref, wdb_ref):
    b = pl.program_id(0)
    used = b < nu_ref[0]

    @pl.when(used & ((b == 0) | (be_ref[b] != be_ref[jnp.maximum(b - 1, 0)])))
    def _():
        wgb_ref[...] = wg_ref[0].astype(BF16)
        wub_ref[...] = wu_ref[0].astype(BF16)
        wdb_ref[...] = wd_ref[0].astype(BF16)

    @pl.when(used)
    def _():
        x = _unpack_bf16_pairs(xs_ref[...]).astype(BF16)
        a = jnp.dot(x, wgb_ref[...], preferred_element_type=F32)
        u = jnp.dot(x, wub_ref[...], preferred_element_type=F32)
        hm = (a * jax.nn.sigmoid(a) * u).astype(BF16)
        y_ref[...] = _pack_bf16_pairs(jnp.dot(hm, wdb_ref[...], preferred_element_type=F32))

    @pl.when(jnp.logical_not(used))
    def _():
        y_ref[...] = jnp.zeros_like(y_ref)


def _experts(block_expert, n_used, xs, w_gate, w_up, w_down):
    m_pad, C = xs.shape
    D = 2 * C
    R, Fd = MOE_ROWS, D_EXPERT
    vmem = 2 * (2 * _nbytes((R, C), xs.dtype) + 3 * _nbytes((D, Fd), F32)) + 3 * _nbytes((D, Fd), BF16)
    vmem += 6 * _nbytes((R, D), F32)
    return pl.pallas_call(
        _expert_kernel,
        out_shape=jax.ShapeDtypeStruct((m_pad, C), xs.dtype),
        grid_spec=pltpu.PrefetchScalarGridSpec(
            num_scalar_prefetch=2,
            grid=(m_pad // R,),
            in_specs=[
                pl.BlockSpec((R, C), lambda b, be, nu: (jnp.minimum(b, jnp.maximum(nu[0] - 1, 0)), 0)),
                pl.BlockSpec((1, D, Fd), lambda b, be, nu: (be[b], 0, 0)),
                pl.BlockSpec((1, D, Fd), lambda b, be, nu: (be[b], 0, 0)),
                pl.BlockSpec((1, Fd, D), lambda b, be, nu: (be[b], 0, 0)),
            ],
            out_specs=pl.BlockSpec((R, C), lambda b, be, nu: (b, 0)),
            scratch_shapes=[pltpu.VMEM((D, Fd), BF16), pltpu.VMEM((D, Fd), BF16), pltpu.VMEM((Fd, D), BF16)],
        ),
        compiler_params=_cparams(1, vmem),
        name="experts",
    )(block_expert, n_used, xs, w_gate, w_up, w_down)


def _combine_kernel(code_ref, pstart_ref, x1_ref, info_ref, yb_ref, p_ref, gp_ref, wpg_ref, wpp_ref, gf_ref, o_ref,
                    ybuf, sem):
    tm, D = x1_ref.shape
    step = pl.program_id(0)
    n_tiles = pl.num_programs(0) - 2
    K = TOPK_IN_GROUP
    GROUPS = 8

    def fetch(tile, group=None):
        slot = tile % 2
        rows = range(tm) if group is None else range(group * tm // GROUPS, (group + 1) * tm // GROUPS)
        for r in rows:
            for k in range(K):
                src = _slot(code_ref[(tile * tm + r) * K + k], pstart_ref)
                _row_copy(yb_ref, src, ybuf.at[slot * K + k], r, sem.at[slot]).start(priority=k % 2)

    def drain(tile):
        slot = tile % 2
        for _ in range(tm * K):
            _row_copy(yb_ref, 0, ybuf.at[0], 0, sem.at[slot]).wait()

    @pl.when(step == 0)
    def _():
        fetch(step)

    @pl.when(step > 0)
    def _():
        drain(step - 1)

    @pl.when((step > 0) & (step <= n_tiles))
    def _():
        tile = step - 1
        slot = tile % 2
        info = info_ref[...]
        x2 = (x1_ref[...] + info[:, 2:3] * _unpack_bf16_pairs(ybuf[slot * K])
              + info[:, 3:4] * _unpack_bf16_pairs(ybuf[slot * K + 1]))
        hp = _rms(x2, gp_ref[...]).astype(BF16)
        pp = jnp.dot(p_ref[...].astype(BF16), wpp_ref[...], preferred_element_type=F32)
        cw = D // GROUPS
        zs = []
        for c in range(GROUPS):
            fetch(step, c)
            zs.append(jnp.dot(hp, wpg_ref[:, c * cw:(c + 1) * cw], preferred_element_type=F32))
        x3 = x2 + jax.nn.sigmoid(jnp.concatenate(zs, axis=1)) * pp
        o_ref[...] = _rms(x3, gf_ref[...])


def _combine(codes, pad_start, x1, info, yb, p, g_ple, w_ple_gate, w_ple_proj, g_final, tm=256):
    T, D = x1.shape
    n_tiles = T // tm
    codes_padded = jnp.concatenate([codes, jnp.zeros((tm * TOPK_IN_GROUP,), I32)])
    vmem = 4 * _nbytes((tm, D), F32) + _nbytes((D, D), BF16) + _nbytes((PLE_DIM, D), BF16)
    vmem += 2 * _nbytes((tm, PLE_DIM), F32) + 2 * TOPK_IN_GROUP * _nbytes((tm, D // 2), yb.dtype)
    vmem += 6 * _nbytes((tm, D), F32)
    tile = lambda i, c, ps: (jnp.clip(i - 1, 0, n_tiles - 1), 0)
    fixed = lambda i, c, ps: (0, 0)
    resident = pl.Buffered(1)
    return pl.pallas_call(
        _combine_kernel,
        out_shape=jax.ShapeDtypeStruct((T, D), F32),
        grid_spec=pltpu.PrefetchScalarGridSpec(
            num_scalar_prefetch=2,
            grid=(n_tiles + 2,),
            in_specs=[
                pl.BlockSpec((tm, D), tile),
                pl.BlockSpec((tm, V7X_LANES), tile),
                pl.BlockSpec(memory_space=pl.ANY),
                pl.BlockSpec((tm, PLE_DIM), tile),
                pl.BlockSpec((1, D), fixed),
                pl.BlockSpec((D, D), fixed, pipeline_mode=resident),
                pl.BlockSpec((PLE_DIM, D), fixed, pipeline_mode=resident),
                pl.BlockSpec((1, D), fixed),
            ],
            out_specs=pl.BlockSpec((tm, D), tile),
            scratch_shapes=[pltpu.VMEM((2 * TOPK_IN_GROUP, tm, yb.shape[1]), yb.dtype),
                            pltpu.SemaphoreType.DMA((2,))],
        ),
        compiler_params=_cparams(1, vmem),
        name="combine",
    )(codes_padded, pad_start, x1, info, yb, p, g_ple.reshape(1, D), w_ple_gate, w_ple_proj, g_final.reshape(1, D))


def _rope_tables(T):
    half, BS = MOBA_HEAD_DIM // 2, MOBA_BLOCK
    inv_freq = ROPE_THETA ** (-jnp.arange(half, dtype=F32) / half)
    ang_a = (jnp.arange(T // BS, dtype=F32) * BS)[:, None, None] * inv_freq
    ang_b = jnp.arange(BS, dtype=F32)[None, :, None] * inv_freq
    ca, sa, cb, sb = jnp.cos(ang_a), jnp.sin(ang_a), jnp.cos(ang_b), jnp.sin(ang_b)
    cos = (ca * cb - sa * sb).reshape(T, half)
    sin = (sa * cb + ca * sb).reshape(T, half)
    return jnp.concatenate([cos, cos], axis=1), jnp.concatenate([-sin, sin], axis=1)


def _mixers(x2d, g_mix, w_in, lb, hgrn_norm_g):
    T = x2d.shape[0]
    W = HGRN_WIDTH
    h = _rmsnorm(x2d, g_mix, BF16)
    w = w_in
    cos, sin = _rope_tables(T)
    log_lb = jnp.log(lb).reshape(1, W)
    log_1m = jnp.log1p(-lb).reshape(1, W)
    hq = _proj(h, w, 0 * W, W, _ep_silu, BF16)
    logf = _proj(h, w, 1 * W, W, _ep_logf, F32, col_extras=(log_lb, log_1m))
    hi = _proj(h, w, 2 * W, W, _ep_identity, BF16)
    hog = _proj(h, w, 3 * W, W, _ep_silu, BF16)
    scale = MOBA_HEAD_DIM ** -0.5
    mqt = _proj_t(h, w, 4 * W, functools.partial(_ep_rope, scale=scale), row_extras=(cos, sin))
    blk = jnp.arange(T, dtype=I32)[:, None] // MOBA_BLOCK
    blk_onehot = (blk == jnp.arange(V7X_LANES, dtype=I32)[None, :]).astype(F32)
    mk = _proj(h, w, 5 * W, W, _ep_rope_aug, BF16, row_extras=(cos, sin, blk_onehot), widen=2)
    mvt = _proj_t(h, w, 6 * W, _ep_identity, ones_rows=MOBA_VT_ROWS - MOBA_HEAD_DIM)
    gates = _proj(h, w, 7 * W, 2 * D_MODEL, _ep_sigmoid, BF16)
    o_hgrn = _hgrn(hq, logf, hi, hog, hgrn_norm_g)
    o_moba = _moba(mqt, mk, mvt)
    return o_hgrn, o_moba, gates


def _moe_plan(info, cnt, T):
    R = MOE_ROWS
    eid = info[:, 0:TOPK_IN_GROUP].astype(I32)
    rank = info[:, 4:4 + TOPK_IN_GROUP].astype(I32)
    counts = cnt[0, :N_EXPERTS].astype(I32)
    padded = (counts + R - 1) // R * R
    pad_end = jnp.cumsum(padded)
    pad_start = pad_end - padded
    codes = (eid * (1 << RANK_BITS) + rank).reshape(-1)
    n_blocks = (T * TOPK_IN_GROUP) // R + N_EXPERTS
    block_expert = jnp.minimum(
        jnp.searchsorted(pad_end, jnp.arange(n_blocks, dtype=I32) * R, side="right"), N_EXPERTS - 1).astype(I32)
    n_used = (pad_end[-1:] // R).astype(I32)
    return codes, pad_start.astype(I32), pad_end.astype(I32), block_expert, n_used, n_blocks * R


def kernel(x, p, norm_mix_g, w_in, hgrn_lb_raw, hgrn_norm_g, w_up_hgrn, w_up_moba, w_out, norm_ffn_g,
           w_router_group, w_router_expert, w_exp_gate, w_exp_up, w_exp_down, norm_ple_g, w_ple_gate,
           w_ple_proj, norm_final_g):
    B, T, D = x.shape
    assert B == 1 and D == D_MODEL and w_in.shape[0] == 1 and T % (4 * MOBA_BLOCK) == 0
    lower_bounds = jnp.cumsum(jax.nn.softmax(hgrn_lb_raw.astype(F32), axis=0), axis=0)
    x2d = x.reshape(T, D)
    o_hgrn, o_moba, gates = _mixers(x2d, norm_mix_g[0], w_in[0], lower_bounds[0], hgrn_norm_g[0])
    merged = _merge(o_hgrn, o_moba, gates, w_up_hgrn[0].astype(BF16), w_up_moba[0].astype(BF16))
    w_router = jnp.pad(jnp.concatenate([w_router_group[0], w_router_expert[0]], axis=1),
                       ((0, 0), (0, V7X_LANES - N_GROUPS - N_EXPERTS))).astype(BF16)
    x1, h2p, info, cnt = _outproj(merged, x2d, w_out[0].astype(BF16), norm_ffn_g[0], w_router)
    codes, pad_start, pad_end, block_expert, n_used, m_pad = _moe_plan(info, cnt, T)
    xs = _dispatch(codes, pad_start, pad_end, h2p, m_pad)
    yb = _experts(block_expert, n_used, xs, w_exp_gate[0], w_exp_up[0], w_exp_down[0])
    out = _combine(codes, pad_start, x1, info, yb, p[0].reshape(T, PLE_DIM), norm_ple_g[0],
                   w_ple_gate[0].astype(BF16), w_ple_proj[0].astype(BF16), norm_final_g)
    return out.reshape(B, T, D)
```

```python
import functools

import jax
import jax.numpy as jnp
from jax import lax
from jax.experimental import pallas as pl
from jax.experimental.pallas import tpu as pltpu

F32 = jnp.float32
BF16 = jnp.bfloat16
I32 = jnp.int32

D_MODEL = 2048
PLE_DIM = 256
HGRN_HEADS = 8
HGRN_HEAD_DIM = 128
HGRN_WIDTH = HGRN_HEADS * HGRN_HEAD_DIM
MOBA_HEADS = 8
MOBA_HEAD_DIM = 128
MOBA_WIDTH = MOBA_HEADS * MOBA_HEAD_DIM
MOBA_BLOCK = 256
MOBA_TOPK = 3
ROPE_THETA = 10000.0
N_GROUPS = 4
EXPERTS_PER_GROUP = 8
N_EXPERTS = N_GROUPS * EXPERTS_PER_GROUP
TOPK_IN_GROUP = 2
D_EXPERT = 512
EPS = 1e-6
NEG_INF = -1e30

V7X_LANES = 128
V7X_SUBLANES = 8
V7X_VMEM_BUDGET_BYTES = 56 * 1024 * 1024

HGRN_CHUNK = 128
MOE_ROWS = 256


def _cparams(n_grid, vmem_bytes):
    return pltpu.CompilerParams(
        dimension_semantics=("arbitrary",) * n_grid,
        vmem_limit_bytes=int(min(max(vmem_bytes, 16 * 1024 * 1024), V7X_VMEM_BUDGET_BYTES)),
    )


def _nbytes(shape, dtype):
    n = 1
    for s in shape:
        n *= s
    return n * jnp.dtype(dtype).itemsize


def _rms(x, g):
    ms = jnp.mean(x * x, axis=-1, keepdims=True)
    return x * lax.rsqrt(ms + EPS) * g


def _ep_identity(acc):
    return acc


def _ep_silu(acc):
    return acc * jax.nn.sigmoid(acc)


def _ep_sigmoid(acc):
    return jax.nn.sigmoid(acc)


def _ep_logf(acc, la_ref, lc_ref):
    ls = jnp.minimum(acc, 0.0) - jnp.log(1.0 + jnp.exp(-jnp.abs(acc)))
    u = la_ref[...]
    v = lc_ref[...] + ls
    return jnp.maximum(u, v) + jnp.log(1.0 + jnp.exp(-jnp.abs(u - v)))


def _ep_rope(acc, cos_ref, sin_ref, *, scale):
    cos = cos_ref[...]
    sin = sin_ref[...]
    outs = []
    for hh in range(acc.shape[1] // MOBA_HEAD_DIM):
        a = acc[:, hh * MOBA_HEAD_DIM:(hh + 1) * MOBA_HEAD_DIM]
        r = pltpu.roll(a, MOBA_HEAD_DIM // 2, axis=1)
        outs.append((a * cos + r * sin) * scale)
    return jnp.concatenate(outs, axis=1)


def _ep_rope_aug(acc, cos_ref, sin_ref, oh_ref):
    cos = cos_ref[...]
    sin = sin_ref[...]
    oh = oh_ref[...]
    outs = []
    for hh in range(acc.shape[1] // MOBA_HEAD_DIM):
        a = acc[:, hh * MOBA_HEAD_DIM:(hh + 1) * MOBA_HEAD_DIM]
        outs.append(a * cos + pltpu.roll(a, MOBA_HEAD_DIM // 2, axis=1) * sin)
        outs.append(oh)
    return jnp.concatenate(outs, axis=1)


def _cast_weight_once(w_ref, wb_ref, row_axis):
    @pl.when(pl.program_id(row_axis) == 0)
    def _():
        wb_ref[...] = w_ref[...].astype(wb_ref.dtype)


def _proj_kernel(h_ref, w_ref, *refs, epilogue):
    *extra, o_ref, wb_ref = refs
    _cast_weight_once(w_ref, wb_ref, 1)
    acc = jnp.dot(h_ref[...], wb_ref[...], preferred_element_type=F32)
    o_ref[...] = epilogue(acc, *extra).astype(o_ref.dtype)


def _proj(h, w, col0, ncols, epilogue, out_dtype, row_extras=(), col_extras=(), tm=1024, tn=1024, widen=1):
    T, K = h.shape
    tn = min(tn, ncols)
    tm = min(tm, T)
    cb = col0 // tn
    otn = widen * tn
    in_specs = [
        pl.BlockSpec((tm, K), lambda j, i: (i, 0)),
        pl.BlockSpec((K, tn), lambda j, i: (0, cb + j)),
    ]
    for e in row_extras:
        in_specs.append(pl.BlockSpec((tm, e.shape[1]), lambda j, i: (i, 0)))
    for e in col_extras:
        in_specs.append(pl.BlockSpec((1, tn), lambda j, i: (0, j)))
    vmem = 2 * (_nbytes((tm, K), h.dtype) + _nbytes((K, tn), w.dtype) + _nbytes((tm, otn), out_dtype))
    vmem += 3 * _nbytes((tm, otn), F32) + _nbytes((K, tn), BF16)
    return pl.pallas_call(
        functools.partial(_proj_kernel, epilogue=epilogue),
        out_shape=jax.ShapeDtypeStruct((T, widen * ncols), out_dtype),
        grid=(ncols // tn, T // tm),
        in_specs=in_specs,
        out_specs=pl.BlockSpec((tm, otn), lambda j, i: (i, j)),
        scratch_shapes=[pltpu.VMEM((K, tn), BF16)],
        compiler_params=_cparams(2, vmem),
        name="proj",
    )(h, w, *row_extras, *col_extras)


def _norm_proj_kernel(x_ref, g_ref, w_ref, o_ref, h_ref, wb_ref, *, epilogue):
    _cast_weight_once(w_ref, wb_ref, 0)
    h = _rms(x_ref[...], g_ref[...]).astype(h_ref.dtype)
    h_ref[...] = h
    o_ref[...] = epilogue(jnp.dot(h, wb_ref[...], preferred_element_type=F32)).astype(o_ref.dtype)


def _norm_proj(x, g, w, col0, ncols, epilogue, out_dtype, tm=512):
    T, K = x.shape
    cb = col0 // ncols
    vmem = 2 * (_nbytes((tm, K), F32) + _nbytes((tm, ncols), out_dtype) + _nbytes((tm, K), BF16))
    vmem += _nbytes((K, ncols), F32) + _nbytes((K, ncols), BF16) + 2 * _nbytes((tm, K), F32) + 3 * _nbytes((tm, ncols), F32)
    return pl.pallas_call(
        functools.partial(_norm_proj_kernel, epilogue=epilogue),
        out_shape=(jax.ShapeDtypeStruct((T, ncols), out_dtype), jax.ShapeDtypeStruct((T, K), BF16)),
        grid=(T // tm,),
        in_specs=[
            pl.BlockSpec((tm, K), lambda i: (i, 0)),
            pl.BlockSpec((1, K), lambda i: (0, 0)),
            pl.BlockSpec((K, ncols), lambda i: (0, cb), pipeline_mode=pl.Buffered(1)),
        ],
        out_specs=(pl.BlockSpec((tm, ncols), lambda i: (i, 0)), pl.BlockSpec((tm, K), lambda i: (i, 0))),
        scratch_shapes=[pltpu.VMEM((K, ncols), BF16)],
        compiler_params=_cparams(1, vmem),
        name="norm_proj",
    )(x, g.reshape(1, K), w)


MOBA_VT_ROWS = MOBA_HEAD_DIM + 16
MOBA_BLOCKS_PER_STEP = 4


def _proj_t_kernel(h_ref, w_ref, *refs, epilogue, ones_rows):
    *extra, o_ref, wb_ref = refs
    BS, HD = MOBA_BLOCK, MOBA_HEAD_DIM
    _cast_weight_once(w_ref, wb_ref, 0)
    acc = epilogue(jnp.dot(h_ref[...], wb_ref[...], preferred_element_type=F32), *extra)
    ones = jnp.ones((ones_rows, BS), F32) if ones_rows else None
    for b in range(acc.shape[0] // BS):
        parts = []
        for hh in range(acc.shape[1] // HD):
            parts.append(acc[b * BS:(b + 1) * BS, hh * HD:(hh + 1) * HD].T)
            if ones_rows:
                parts.append(ones)
        o_ref[b] = jnp.concatenate(parts, axis=0).astype(o_ref.dtype)


def _proj_t(h, w, col0, epilogue, row_extras=(), ones_rows=0, tm=1024):
    T, K = h.shape
    tn = MOBA_WIDTH
    tm = min(tm, T)
    cb = col0 // tn
    rows = MOBA_HEADS * (MOBA_HEAD_DIM + ones_rows)
    in_specs = [pl.BlockSpec((tm, K), lambda i: (i, 0)), pl.BlockSpec((K, tn), lambda i: (0, cb))]
    for e in row_extras:
        in_specs.append(pl.BlockSpec((tm, e.shape[1]), lambda i: (i, 0)))
    vmem = 2 * (_nbytes((tm, K), h.dtype) + _nbytes((K, tn), w.dtype) + _nbytes((tm, 2 * tn), BF16))
    vmem += 4 * _nbytes((tm, tn), F32) + _nbytes((K, tn), BF16)
    return pl.pallas_call(
        functools.partial(_proj_t_kernel, epilogue=epilogue, ones_rows=ones_rows),
        out_shape=jax.ShapeDtypeStruct((T // MOBA_BLOCK, rows, MOBA_BLOCK), BF16),
        grid=(T // tm,),
        in_specs=in_specs,
        out_specs=pl.BlockSpec((tm // MOBA_BLOCK, rows, MOBA_BLOCK), lambda i: (i, 0, 0)),
        scratch_shapes=[pltpu.VMEM((K, tn), BF16)],
        compiler_params=_cparams(1, vmem),
        name="proj_t",
    )(h, w, *row_extras)


def _hgrn_kernel(q_ref, g_ref, v_ref, og_ref, ng_ref, o_ref, st_ref, code_ref):
    W, HD, NH, C, S = HGRN_WIDTH, HGRN_HEAD_DIM, HGRN_HEADS, HGRN_CHUNK, V7X_SUBLANES
    J = C // S

    @pl.when(pl.program_id(0) == 0)
    def _():
        st_ref[...] = jnp.zeros_like(st_ref)
        tr = lax.broadcasted_iota(I32, (C, C), 0)
        tc = lax.broadcasted_iota(I32, (C, C), 1)
        xr = tr ^ tc
        code = jnp.zeros((C, C), I32)
        for lvl in range(1, 8):
            code = jnp.where(xr >= (1 << (lvl - 1)), lvl, code)
        code_ref[...] = jnp.where(tc > tr, -1, code)

    def r3(x):
        return x.astype(F32).reshape(J, S, W)

    def sub_bcast(x3, r):
        return jnp.broadcast_to(x3[:, r:r + 1, :], x3.shape)

    g3, q3, v3 = r3(g_ref[...]), r3(q_ref[...]), r3(v_ref[...])
    sub = lax.broadcasted_iota(I32, (1, S, W), 1)

    c3 = g3
    for s in (1, 2, 4):
        c3 = c3 + jnp.where(sub >= s, pltpu.roll(c3, s, axis=1), 0.0)
    run = jnp.zeros((1, 1, W), F32)
    carry = []
    for j in range(J):
        carry.append(run)
        run = run + c3[j:j + 1, S - 1:S, :]
    b3 = c3 + jnp.concatenate(carry, axis=0)
    bC = run

    k3 = 1.0 - jnp.exp(g3)
    qe3 = q3 * jnp.exp(b3)
    ks3 = k3 * jnp.exp(bC - b3)

    levels = [(0, q3, k3)]
    ref1 = jnp.where(sub % 2 == 0, b3, pltpu.roll(b3, 1, axis=1))
    ref2 = jnp.where(sub < 4, sub_bcast(b3, 1), sub_bcast(b3, 5))
    ref4 = sub_bcast(b3, 3)
    for lvl, (ref, upper) in enumerate(((ref1, sub % 2 == 1), (ref2, sub % 4 >= 2), (ref4, sub >= 4)), start=1):
        e = jnp.exp(-jnp.abs(b3 - ref))
        levels.append((lvl, jnp.where(upper, q3 * e, 0.0), jnp.where(upper, 0.0, k3 * e)))
    zero_group = jnp.zeros((1, S, W), F32)
    for lvl, half in enumerate((1, 2, 4, 8), start=4):
        qparts, kparts = [], []
        for j in range(J):
            jr = (j // (2 * half)) * (2 * half) + half - 1
            ref = b3[jr:jr + 1, S - 1:S, :]
            if (j % (2 * half)) >= half:
                qparts.append(q3[j:j + 1] * jnp.exp(b3[j:j + 1] - ref))
                kparts.append(zero_group)
            else:
                qparts.append(zero_group)
                kparts.append(k3[j:j + 1] * jnp.exp(ref - b3[j:j + 1]))
        levels.append((lvl, jnp.concatenate(qparts, axis=0), jnp.concatenate(kparts, axis=0)))

    code = code_ref[...]

    def mat(x3, h):
        return x3.reshape(C, W)[:, h * HD:(h + 1) * HD].astype(BF16)

    nt = (((1,), (1,)), ((), ()))
    tn = (((0,), (0,)), ((), ()))
    ebc = jnp.exp(bC).reshape(1, W)
    ng = ng_ref[...]
    for h in range(NH):
        a_mat = jnp.zeros((C, C), F32)
        for lvl, qr, kr in levels:
            s = lax.dot_general(mat(qr, h), mat(kr, h), nt, preferred_element_type=F32)
            a_mat = jnp.where(code == lvl, s, a_mat)
        vh = mat(v3, h)
        st = st_ref[h]
        o = jnp.dot(a_mat.astype(BF16), vh, preferred_element_type=F32)
        o = o + lax.dot_general(mat(qe3, h), st.astype(BF16), nt, preferred_element_type=F32)
        o = _rms(o, ng) * og_ref[:, h * HD:(h + 1) * HD].astype(F32)
        o_ref[:, h * HD:(h + 1) * HD] = o.astype(o_ref.dtype)
        st_ref[h] = st * ebc[:, h * HD:(h + 1) * HD] + lax.dot_general(
            vh, mat(ks3, h), tn, preferred_element_type=F32)


def _hgrn(q, logf, v, og, norm_g):
    T, W = q.shape
    C = HGRN_CHUNK
    blk = pl.BlockSpec((C, W), lambda c: (c, 0))
    vmem = 64 * _nbytes((C, W), F32)
    return pl.pallas_call(
        _hgrn_kernel,
        out_shape=jax.ShapeDtypeStruct((T, W), BF16),
        grid=(T // C,),
        in_specs=[blk, blk, blk, blk, pl.BlockSpec((1, HGRN_HEAD_DIM), lambda c: (0, 0))],
        out_specs=blk,
        scratch_shapes=[pltpu.VMEM((HGRN_HEADS, HGRN_HEAD_DIM, HGRN_HEAD_DIM), F32), pltpu.VMEM((C, C), I32)],
        compiler_params=_cparams(1, vmem),
        name="hgrn",
    )(q, logf, v, og, norm_g.reshape(1, HGRN_HEAD_DIM))


def _moba_kernel(qt_ref, k_ref, vt_ref, o_ref, km_ref):
    BS, HD, VR = MOBA_BLOCK, MOBA_HEAD_DIM, MOBA_VT_ROWS
    T = k_ref.shape[0]
    NB = T // BS
    G = o_ref.shape[1] // HD
    cur = pl.program_id(1)

    @pl.when(cur == 0)
    def _():
        for g in range(G):
            kf = k_ref[:, 2 * g * HD:(2 * g + 1) * HD].astype(F32).reshape(NB, BS, HD)
            km_ref[g] = jnp.sum(kf, axis=1) * (1.0 / BS)

    blk = lax.broadcasted_iota(I32, (NB, BS), 0)
    key = lax.broadcasted_iota(I32, (BS, BS), 0)
    qry = lax.broadcasted_iota(I32, (BS, BS), 1)
    pad = jnp.zeros((V7X_LANES - NB, BS), F32)
    r0 = pl.multiple_of(cur * BS, BS)
    qts = [qt_ref[0, g * HD:(g + 1) * HD, :] for g in range(G)]
    gts = [jnp.dot(km_ref[g].astype(BF16), qts[g], preferred_element_type=F32) for g in range(G)]
    s_own = [jnp.dot(k_ref[pl.ds(r0, BS), 2 * g * HD:(2 * g + 1) * HD], qts[g], preferred_element_type=F32)
             for g in range(G)]
    qcs = []
    for g in range(G):
        gt = jnp.where(blk < cur, gts[g], NEG_INF)
        sel = jnp.zeros((NB, BS), jnp.bool_)
        for _ in range(MOBA_TOPK):
            mx = jnp.max(gt, axis=0, keepdims=True)
            idx = jnp.min(jnp.where(gt == mx, blk, NB), axis=0, keepdims=True)
            pick = (blk == idx) & (mx > 0.5 * NEG_INF)
            sel = sel | pick
            gt = jnp.where(pick, NEG_INF, gt)
        pen = jnp.concatenate([jnp.where(sel, 0.0, NEG_INF), pad], axis=0).astype(BF16)
        qcs.append(jnp.concatenate([qts[g], pen], axis=0))

    def scores(n):
        rn = pl.multiple_of(n * BS, BS)
        return tuple(jnp.dot(k_ref[pl.ds(rn, BS), 2 * g * HD:(2 * g + 2) * HD], qcs[g], preferred_element_type=F32)
                     for g in range(G))

    ms, ps = [], []
    for g in range(G):
        s = jnp.where(key <= qry, s_own[g], NEG_INF)
        m0 = jnp.max(s, axis=0, keepdims=True)
        ms.append(m0)
        ps.append(jnp.exp(s - m0).astype(BF16))
    accs = [jnp.dot(vt_ref[cur, g * VR:(g + 1) * VR, :], ps[g], preferred_element_type=F32)
            for g in range(G)]

    KB = MOBA_BLOCKS_PER_STEP

    def body(c, carry):
        ms, accs = carry
        r = pl.multiple_of(c * KB * BS, KB * BS)
        sns = [jnp.dot(k_ref[pl.ds(r, KB * BS), 2 * g * HD:(2 * g + 2) * HD], qcs[g], preferred_element_type=F32)
               for g in range(G)]
        new_ms, alphas, pns = [], [], []
        for g in range(G):
            m_new = jnp.maximum(ms[g], jnp.max(sns[g], axis=0, keepdims=True))
            alphas.append(jnp.exp(ms[g] - m_new))
            pns.append(jnp.exp(sns[g] - m_new).astype(BF16))
            new_ms.append(m_new)
        new_accs = []
        for g in range(G):
            pv = alphas[g] * accs[g]
            for j in range(KB):
                pv = pv + jnp.dot(vt_ref[KB * c + j, g * VR:(g + 1) * VR, :], pns[g][j * BS:(j + 1) * BS],
                                  preferred_element_type=F32)
            new_accs.append(pv)
        return tuple(new_ms), tuple(new_accs)

    _, accs = lax.fori_loop(0, (cur + KB - 1) // KB, body, (tuple(ms), tuple(accs)))
    for g in range(G):
        ot = accs[g][:HD, :] / accs[g][HD:HD + 1, :]
        o_ref[:, g * HD:(g + 1) * HD] = ot.T.astype(o_ref.dtype)


def _moba(mqt, mk_aug, mvt, heads_per_step=4):
    T = mk_aug.shape[0]
    BS, HD, G, VR = MOBA_BLOCK, MOBA_HEAD_DIM, heads_per_step, MOBA_VT_ROWS
    NB = T // BS
    vmem = _nbytes((T, 2 * G * HD), BF16) + _nbytes((NB, G * VR, BS), BF16) + 8 * _nbytes((BS, G * HD), BF16)
    vmem += 16 * G * _nbytes((BS, BS), F32)
    resident = pl.Buffered(1)
    return pl.pallas_call(
        _moba_kernel,
        out_shape=jax.ShapeDtypeStruct((T, MOBA_WIDTH), BF16),
        grid=(MOBA_HEADS // G, NB),
        in_specs=[
            pl.BlockSpec((1, G * HD, BS), lambda h, i: (i, h, 0)),
            pl.BlockSpec((T, 2 * G * HD), lambda h, i: (0, h), pipeline_mode=resident),
            pl.BlockSpec((NB, G * VR, BS), lambda h, i: (0, h, 0), pipeline_mode=resident),
        ],
        out_specs=pl.BlockSpec((BS, G * HD), lambda h, i: (i, h)),
        scratch_shapes=[pltpu.VMEM((G, NB, HD), F32)],
        compiler_params=_cparams(2, vmem),
        name="moba",
    )(mqt, mk_aug, mvt)


def _merge_kernel(oh_ref, om_ref, ga_ref, gb_ref, wh_ref, wm_ref, o_ref):
    a = jnp.dot(oh_ref[...], wh_ref[...], preferred_element_type=F32)
    b = jnp.dot(om_ref[...], wm_ref[...], preferred_element_type=F32)
    o_ref[...] = (ga_ref[...].astype(F32) * a + gb_ref[...].astype(F32) * b).astype(o_ref.dtype)


def _merge(o_hgrn, o_moba, gates, w_up_hgrn, w_up_moba, tm=512):
    T = o_hgrn.shape[0]
    D = D_MODEL
    vmem = 2 * (_nbytes((tm, HGRN_WIDTH), F32) + _nbytes((tm, MOBA_WIDTH), BF16) + 3 * _nbytes((tm, D), BF16)
                + 2 * _nbytes((HGRN_WIDTH, D), BF16)) + 3 * _nbytes((tm, D), F32)
    return pl.pallas_call(
        _merge_kernel,
        out_shape=jax.ShapeDtypeStruct((T, D), BF16),
        grid=(T // tm,),
        in_specs=[
            pl.BlockSpec((tm, HGRN_WIDTH), lambda i: (i, 0)),
            pl.BlockSpec((tm, MOBA_WIDTH), lambda i: (i, 0)),
            pl.BlockSpec((tm, D), lambda i: (i, 0)),
            pl.BlockSpec((tm, D), lambda i: (i, 1)),
            pl.BlockSpec((HGRN_WIDTH, D), lambda i: (0, 0)),
            pl.BlockSpec((MOBA_WIDTH, D), lambda i: (0, 0)),
        ],
        out_specs=pl.BlockSpec((tm, D), lambda i: (i, 0)),
        compiler_params=_cparams(1, vmem),
        name="merge",
    )(o_hgrn, o_moba, gates, gates, w_up_hgrn, w_up_moba)


def _pack_bf16_pairs(x):
    C = x.shape[1] // 2
    b = lax.bitcast_convert_type(x, jnp.uint32)
    r = (b + jnp.uint32(0x7FFF) + ((b >> 16) & jnp.uint32(1))) >> 16
    return r[:, :C] | (r[:, C:] << 16)


def _unpack_bf16_pairs(p):
    lo = lax.bitcast_convert_type(p << 16, F32)
    hi = lax.bitcast_convert_type(p & jnp.uint32(0xFFFF0000), F32)
    return jnp.concatenate([lo, hi], axis=1)


def _outproj_kernel(m_ref, x_ref, w_ref, g_ref, wr_ref, x1_ref, h2p_ref, info_ref, cnt_ref, prev_ref, carry_ref):
    step = pl.program_id(0)
    n_tiles = pl.num_programs(0) - 1

    @pl.when(step == 0)
    def _():
        carry_ref[...] = jnp.zeros_like(carry_ref)

    def finish_previous():
        h2 = _rms(prev_ref[...], g_ref[...])
        h2p_ref[...] = _pack_bf16_pairs(h2)
        info_ref[...] = _route(h2, wr_ref, carry_ref)
        cnt_ref[...] = carry_ref[...]

    def project():
        return x_ref[...] + jnp.dot(m_ref[...], w_ref[...], preferred_element_type=F32)

    @pl.when(step == 0)
    def _():
        x1 = project()
        x1_ref[...] = x1
        prev_ref[...] = x1

    @pl.when((step > 0) & (step < n_tiles))
    def _():
        x1 = project()
        finish_previous()
        x1_ref[...] = x1
        prev_ref[...] = x1

    @pl.when(step == n_tiles)
    def _():
        finish_previous()


def _outproj(merged, x, w_out, g_ffn, w_router, tm=512):
    T, D = x.shape
    n_tiles = T // tm
    vmem = 2 * (_nbytes((tm, D), BF16) + 3 * _nbytes((tm, D), F32) + _nbytes((D, D), BF16)) + 7 * _nbytes((tm, D), F32)
    cur = lambda i: (jnp.minimum(i, n_tiles - 1), 0)
    prev = lambda i: (jnp.maximum(i - 1, 0), 0)
    fixed = lambda i: (0, 0)
    return pl.pallas_call(
        _outproj_kernel,
        out_shape=(jax.ShapeDtypeStruct((T, D), F32), jax.ShapeDtypeStruct((T, D // 2), jnp.uint32),
                   jax.ShapeDtypeStruct((T, V7X_LANES), F32), jax.ShapeDtypeStruct((1, V7X_LANES), F32)),
        grid=(n_tiles + 1,),
        in_specs=[
            pl.BlockSpec((tm, D), cur),
            pl.BlockSpec((tm, D), cur),
            pl.BlockSpec((D, D), fixed, pipeline_mode=pl.Buffered(1)),
            pl.BlockSpec((1, D), fixed),
            pl.BlockSpec((D, V7X_LANES), fixed),
        ],
        out_specs=(pl.BlockSpec((tm, D), cur), pl.BlockSpec((tm, D // 2), prev), pl.BlockSpec((tm, V7X_LANES), prev),
                   pl.BlockSpec((1, V7X_LANES), fixed)),
        scratch_shapes=[pltpu.VMEM((tm, D), F32), pltpu.VMEM((1, V7X_LANES), F32)],
        compiler_params=_cparams(1, vmem),
        name="outproj",
    )(merged, x, w_out, g_ffn.reshape(1, D), w_router)


def _route(h2, w_ref, carry_ref):
    tm = h2.shape[0]
    logits = jnp.dot(h2.astype(BF16), w_ref[...], preferred_element_type=F32)
    lane = lax.broadcasted_iota(I32, (tm, V7X_LANES), 1)
    is_g = lane < N_GROUPS
    gl = jnp.where(is_g, logits, NEG_INF)
    gmax = jnp.max(gl, axis=1, keepdims=True)
    g_sel = jnp.min(jnp.where(gl == gmax, lane, V7X_LANES), axis=1, keepdims=True)
    gsum = jnp.sum(jnp.where(is_g, jnp.exp(gl - gmax), 0.0), axis=1, keepdims=True)
    p_group = 1.0 / gsum
    lo = N_GROUPS + EXPERTS_PER_GROUP * g_sel
    emask = (lane >= lo) & (lane < lo + EXPERTS_PER_GROUP)
    el = jnp.where(emask, logits, NEG_INF)
    e1 = jnp.max(el, axis=1, keepdims=True)
    i1 = jnp.min(jnp.where((el == e1) & emask, lane, V7X_LANES), axis=1, keepdims=True)
    emask2 = emask & (lane != i1)
    el2 = jnp.where(emask2, logits, NEG_INF)
    e2 = jnp.max(el2, axis=1, keepdims=True)
    i2 = jnp.min(jnp.where((el2 == e2) & emask2, lane, V7X_LANES), axis=1, keepdims=True)
    r = jnp.exp(e2 - e1)
    w1 = p_group / (1.0 + r)
    w2 = p_group * r / (1.0 + r)
    eid1 = i1 - N_GROUPS
    eid2 = i2 - N_GROUPS
    oh1 = jnp.where(lane == eid1, 1.0, 0.0)
    oh2 = jnp.where(lane == eid2, 1.0, 0.0)
    cnt = oh1 + oh2
    tri = jnp.where(lax.broadcasted_iota(I32, (tm, tm), 0) > lax.broadcasted_iota(I32, (tm, tm), 1), 1.0, 0.0)
    before = jnp.dot(tri.astype(BF16), cnt.astype(BF16), preferred_element_type=F32) + carry_ref[...]
    rank1 = jnp.sum(oh1 * before, axis=1, keepdims=True)
    rank2 = jnp.sum(oh2 * before, axis=1, keepdims=True)
    carry_ref[...] = carry_ref[...] + jnp.sum(cnt, axis=0, keepdims=True)
    info = jnp.zeros((tm, V7X_LANES), F32)
    for k, val in enumerate((eid1.astype(F32), eid2.astype(F32), w1, w2, rank1, rank2)):
        info = jnp.where(lane == k, val, info)
    return info


def _row_copy(src_ref, src_row, dst_ref, dst_row, sem):
    return pltpu.make_async_copy(src_ref.at[pl.ds(src_row, 1), :], dst_ref.at[pl.ds(dst_row, 1), :], sem)


ROW_DMA_UNROLL = 8
RANK_BITS = 16


def _slot(code, pstart_ref):
    return pstart_ref[lax.shift_right_logical(code, RANK_BITS)] + (code & ((1 << RANK_BITS) - 1))


def _dispatch_kernel(code_ref, pstart_ref, pend_ref, h_ref, xs_ref, zero_ref, sem):
    tm = h_ref.shape[0]
    step = pl.program_id(0)
    base = step * tm

    @pl.when(step == 0)
    def _():
        zero_ref[...] = jnp.zeros_like(zero_ref)

        def tail(e):
            return pltpu.make_async_copy(
                zero_ref, xs_ref.at[pl.ds(pl.multiple_of(pend_ref[e] - MOE_ROWS, MOE_ROWS), MOE_ROWS), :], sem)

        def nonempty(e):
            return pend_ref[e] > (pend_ref[e - 1] if e else 0)

        def unused(b):
            return pltpu.make_async_copy(
                zero_ref, xs_ref.at[pl.ds(pl.multiple_of(b * MOE_ROWS, MOE_ROWS), MOE_ROWS), :], sem)

        first_unused = pend_ref[N_EXPERTS - 1] // MOE_ROWS
        n_blocks = xs_ref.shape[0] // MOE_ROWS
        for e in range(N_EXPERTS):
            pl.when(nonempty(e))(lambda e=e: tail(e).start())
        lax.fori_loop(first_unused, n_blocks, lambda b, c: (unused(b).start(), c)[1], 0)
        for e in range(N_EXPERTS):
            pl.when(nonempty(e))(lambda e=e: tail(e).wait())
        lax.fori_loop(first_unused, n_blocks, lambda b, c: (unused(b).wait(), c)[1], 0)

    def issue(rb, c):
        for u in range(ROW_DMA_UNROLL):
            r = rb * ROW_DMA_UNROLL + u
            for k in range(TOPK_IN_GROUP):
                slot = _slot(code_ref[(base + r) * TOPK_IN_GROUP + k], pstart_ref)
                _row_copy(h_ref, r, xs_ref, slot, sem).start(priority=k % 2)
        return c

    lax.fori_loop(0, tm // ROW_DMA_UNROLL, issue, 0)
    for _ in range(tm * TOPK_IN_GROUP):
        _row_copy(h_ref, 0, xs_ref, 0, sem).wait()


def _dispatch(codes, pad_start, pad_end, h2p, m_pad, tm=256):
    T, C = h2p.shape
    return pl.pallas_call(
        _dispatch_kernel,
        out_shape=jax.ShapeDtypeStruct((m_pad, C), h2p.dtype),
        grid_spec=pltpu.PrefetchScalarGridSpec(
            num_scalar_prefetch=3,
            grid=(T // tm,),
            in_specs=[pl.BlockSpec((tm, C), lambda i, c, ps, pe: (i, 0))],
            out_specs=pl.BlockSpec(memory_space=pl.ANY),
            scratch_shapes=[pltpu.VMEM((MOE_ROWS, C), h2p.dtype), pltpu.SemaphoreType.DMA(())],
        ),
        compiler_params=_cparams(1, 6 * _nbytes((tm, C), h2p.dtype)),
        name="dispatch",
    )(codes, pad_start, pad_end, h2p)


def _expert_kernel(be_ref, nu_ref, xs_ref, wg_ref, wu_ref, wd_ref, y_ref, wgb_ref, wub_ref, wdb_ref):
    b = pl.program_id(0)
    used = b < nu_ref[0]

    @pl.when(used & ((b == 0) | (be_ref[b] != be_ref[jnp.maximum(b - 1, 0)])))
    def _():
        wgb_ref[...] = wg_ref[0].astype(BF16)
        wub_ref[...] = wu_ref[0].astype(BF16)
        wdb_ref[...] = wd_ref[0].astype(BF16)

    @pl.when(used)
    def _():
        x = _unpack_bf16_pairs(xs_ref[...]).astype(BF16)
        a = jnp.dot(x, wgb_ref[...], preferred_element_type=F32)
        u = jnp.dot(x, wub_ref[...], preferred_element_type=F32)
        hm = (a * jax.nn.sigmoid(a) * u).astype(BF16)
        y_ref[...] = _pack_bf16_pairs(jnp.dot(hm, wdb_ref[...], preferred_element_type=F32))

    @pl.when(jnp.logical_not(used))
    def _():
        y_ref[...] = jnp.zeros_like(y_ref)


def _experts(block_expert, n_used, xs, w_gate, w_up, w_down):
    m_pad, C = xs.shape
    D = 2 * C
    R, Fd = MOE_ROWS, D_EXPERT
    vmem = 2 * (2 * _nbytes((R, C), xs.dtype) + 3 * _nbytes((D, Fd), F32)) + 3 * _nbytes((D, Fd), BF16)
    vmem += 6 * _nbytes((R, D), F32)
    return pl.pallas_call(
        _expert_kernel,
        out_shape=jax.ShapeDtypeStruct((m_pad, C), xs.dtype),
        grid_spec=pltpu.PrefetchScalarGridSpec(
            num_scalar_prefetch=2,
            grid=(m_pad // R,),
            in_specs=[
                pl.BlockSpec((R, C), lambda b, be, nu: (jnp.minimum(b, jnp.maximum(nu[0] - 1, 0)), 0)),
                pl.BlockSpec((1, D, Fd), lambda b, be, nu: (be[b], 0, 0)),
                pl.BlockSpec((1, D, Fd), lambda b, be, nu: (be[b], 0, 0)),
                pl.BlockSpec((1, Fd, D), lambda b, be, nu: (be[b], 0, 0)),
            ],
            out_specs=pl.BlockSpec((R, C), lambda b, be, nu: (b, 0)),
            scratch_shapes=[pltpu.VMEM((D, Fd), BF16), pltpu.VMEM((D, Fd), BF16), pltpu.VMEM((Fd, D), BF16)],
        ),
        compiler_params=_cparams(1, vmem),
        name="experts",
    )(block_expert, n_used, xs, w_gate, w_up, w_down)


def _combine_kernel(code_ref, pstart_ref, x1_ref, info_ref, yb_ref, p_ref, gp_ref, wpg_ref, wpp_ref, gf_ref, o_ref,
                    ybuf, sem):
    tm, D = x1_ref.shape
    step = pl.program_id(0)
    n_tiles = pl.num_programs(0) - 2
    K = TOPK_IN_GROUP
    GROUPS = 8

    def fetch(tile, group=None):
        slot = tile % 2
        rows = range(tm) if group is None else range(group * tm // GROUPS, (group + 1) * tm // GROUPS)
        for r in rows:
            for k in range(K):
                src = _slot(code_ref[(tile * tm + r) * K + k], pstart_ref)
                _row_copy(yb_ref, src, ybuf.at[slot * K + k], r, sem.at[slot]).start(priority=k % 2)

    def drain(tile):
        slot = tile % 2
        for _ in range(tm * K):
            _row_copy(yb_ref, 0, ybuf.at[0], 0, sem.at[slot]).wait()

    @pl.when(step == 0)
    def _():
        fetch(step)

    @pl.when(step > 0)
    def _():
        drain(step - 1)

    @pl.when((step > 0) & (step <= n_tiles))
    def _():
        tile = step - 1
        slot = tile % 2
        info = info_ref[...]
        x2 = (x1_ref[...] + info[:, 2:3] * _unpack_bf16_pairs(ybuf[slot * K])
              + info[:, 3:4] * _unpack_bf16_pairs(ybuf[slot * K + 1]))
        hp = _rms(x2, gp_ref[...]).astype(BF16)
        pp = jnp.dot(p_ref[...].astype(BF16), wpp_ref[...], preferred_element_type=F32)
        cw = D // GROUPS
        zs = []
        for c in range(GROUPS):
            fetch(step, c)
            zs.append(jnp.dot(hp, wpg_ref[:, c * cw:(c + 1) * cw], preferred_element_type=F32))
        x3 = x2 + jax.nn.sigmoid(jnp.concatenate(zs, axis=1)) * pp
        o_ref[...] = _rms(x3, gf_ref[...])


def _combine(codes, pad_start, x1, info, yb, p, g_ple, w_ple_gate, w_ple_proj, g_final, tm=256):
    T, D = x1.shape
    n_tiles = T // tm
    codes_padded = jnp.concatenate([codes, jnp.zeros((tm * TOPK_IN_GROUP,), I32)])
    vmem = 4 * _nbytes((tm, D), F32) + _nbytes((D, D), BF16) + _nbytes((PLE_DIM, D), BF16)
    vmem += 2 * _nbytes((tm, PLE_DIM), F32) + 2 * TOPK_IN_GROUP * _nbytes((tm, D // 2), yb.dtype)
    vmem += 6 * _nbytes((tm, D), F32)
    tile = lambda i, c, ps: (jnp.clip(i - 1, 0, n_tiles - 1), 0)
    fixed = lambda i, c, ps: (0, 0)
    resident = pl.Buffered(1)
    return pl.pallas_call(
        _combine_kernel,
        out_shape=jax.ShapeDtypeStruct((T, D), F32),
        grid_spec=pltpu.PrefetchScalarGridSpec(
            num_scalar_prefetch=2,
            grid=(n_tiles + 2,),
            in_specs=[
                pl.BlockSpec((tm, D), tile),
                pl.BlockSpec((tm, V7X_LANES), tile),
                pl.BlockSpec(memory_space=pl.ANY),
                pl.BlockSpec((tm, PLE_DIM), tile),
                pl.BlockSpec((1, D), fixed),
                pl.BlockSpec((D, D), fixed, pipeline_mode=resident),
                pl.BlockSpec((PLE_DIM, D), fixed, pipeline_mode=resident),
                pl.BlockSpec((1, D), fixed),
            ],
            out_specs=pl.BlockSpec((tm, D), tile),
            scratch_shapes=[pltpu.VMEM((2 * TOPK_IN_GROUP, tm, yb.shape[1]), yb.dtype),
                            pltpu.SemaphoreType.DMA((2,))],
        ),
        compiler_params=_cparams(1, vmem),
        name="combine",
    )(codes_padded, pad_start, x1, info, yb, p, g_ple.reshape(1, D), w_ple_gate, w_ple_proj, g_final.reshape(1, D))


def _rope_tables(T):
    half, BS = MOBA_HEAD_DIM // 2, MOBA_BLOCK
    inv_freq = ROPE_THETA ** (-jnp.arange(half, dtype=F32) / half)
    ang_a = (jnp.arange(T // BS, dtype=F32) * BS)[:, None, None] * inv_freq
    ang_b = jnp.arange(BS, dtype=F32)[None, :, None] * inv_freq
    ca, sa, cb, sb = jnp.cos(ang_a), jnp.sin(ang_a), jnp.cos(ang_b), jnp.sin(ang_b)
    cos = (ca * cb - sa * sb).reshape(T, half)
    sin = (sa * cb + ca * sb).reshape(T, half)
    return jnp.concatenate([cos, cos], axis=1), jnp.concatenate([-sin, sin], axis=1)


def _mixers(x2d, g_mix, w_in, lb, hgrn_norm_g):
    T = x2d.shape[0]
    W = HGRN_WIDTH
    w = w_in
    cos, sin = _rope_tables(T)
    log_lb = jnp.log(lb).reshape(1, W)
    log_1m = jnp.log1p(-lb).reshape(1, W)
    hq, h = _norm_proj(x2d, g_mix, w, 0 * W, W, _ep_silu, BF16)
    logf = _proj(h, w, 1 * W, W, _ep_logf, F32, col_extras=(log_lb, log_1m))
    hi = _proj(h, w, 2 * W, W, _ep_identity, BF16)
    hog = _proj(h, w, 3 * W, W, _ep_silu, BF16)
    scale = MOBA_HEAD_DIM ** -0.5
    mqt = _proj_t(h, w, 4 * W, functools.partial(_ep_rope, scale=scale), row_extras=(cos, sin))
    blk = jnp.arange(T, dtype=I32)[:, None] // MOBA_BLOCK
    blk_onehot = (blk == jnp.arange(V7X_LANES, dtype=I32)[None, :]).astype(F32)
    mk = _proj(h, w, 5 * W, W, _ep_rope_aug, BF16, row_extras=(cos, sin, blk_onehot), widen=2)
    mvt = _proj_t(h, w, 6 * W, _ep_identity, ones_rows=MOBA_VT_ROWS - MOBA_HEAD_DIM)
    gates = _proj(h, w, 7 * W, 2 * D_MODEL, _ep_sigmoid, BF16)
    o_hgrn = _hgrn(hq, logf, hi, hog, hgrn_norm_g)
    o_moba = _moba(mqt, mk, mvt)
    return o_hgrn, o_moba, gates


def _moe_plan(info, cnt, T):
    R = MOE_ROWS
    eid = info[:, 0:TOPK_IN_GROUP].astype(I32)
    rank = info[:, 4:4 + TOPK_IN_GROUP].astype(I32)
    counts = cnt[0, :N_EXPERTS].astype(I32)
    padded = (counts + R - 1) // R * R
    pad_end = jnp.cumsum(padded)
    pad_start = pad_end - padded
    codes = (eid * (1 << RANK_BITS) + rank).reshape(-1)
    n_blocks = (T * TOPK_IN_GROUP) // R + N_EXPERTS
    block_expert = jnp.minimum(
        jnp.searchsorted(pad_end, jnp.arange(n_blocks, dtype=I32) * R, side="right"), N_EXPERTS - 1).astype(I32)
    n_used = (pad_end[-1:] // R).astype(I32)
    return codes, pad_start.astype(I32), pad_end.astype(I32), block_expert, n_used, n_blocks * R


def kernel(x, p, norm_mix_g, w_in, hgrn_lb_raw, hgrn_norm_g, w_up_hgrn, w_up_moba, w_out, norm_ffn_g,
           w_router_group, w_router_expert, w_exp_gate, w_exp_up, w_exp_down, norm_ple_g, w_ple_gate,
           w_ple_proj, norm_final_g):
    B, T, D = x.shape
    assert B == 1 and D == D_MODEL and w_in.shape[0] == 1 and T % (4 * MOBA_BLOCK) == 0
    lower_bounds = jnp.cumsum(jax.nn.softmax(hgrn_lb_raw.astype(F32), axis=0), axis=0)
    x2d = x.reshape(T, D)
    o_hgrn, o_moba, gates = _mixers(x2d, norm_mix_g[0], w_in[0], lower_bounds[0], hgrn_norm_g[0])
    merged = _merge(o_hgrn, o_moba, gates, w_up_hgrn[0].astype(BF16), w_up_moba[0].astype(BF16))
    w_router = jnp.pad(jnp.concatenate([w_router_group[0], w_router_expert[0]], axis=1),
                       ((0, 0), (0, V7X_LANES - N_GROUPS - N_EXPERTS))).astype(BF16)
    x1, h2p, info, cnt = _outproj(merged, x2d, w_out[0].astype(BF16), norm_ffn_g[0], w_router)
    codes, pad_start, pad_end, block_expert, n_used, m_pad = _moe_plan(info, cnt, T)
    xs = _dispatch(codes, pad_start, pad_end, h2p, m_pad)
    yb = _experts(block_expert, n_used, xs, w_exp_gate[0], w_exp_up[0], w_exp_down[0])
    out = _combine(codes, pad_start, x1, info, yb, p[0].reshape(T, PLE_DIM), norm_ple_g[0],
                   w_ple_gate[0].astype(BF16), w_ple_proj[0].astype(BF16), norm_final_g)
    return out.reshape(B, T, D)
```

```python
import functools

import jax
import jax.numpy as jnp
from jax import lax
from jax.experimental import pallas as pl
from jax.experimental.pallas import tpu as pltpu

F32 = jnp.float32
BF16 = jnp.bfloat16
I32 = jnp.int32

D_MODEL = 2048
PLE_DIM = 256
HGRN_HEADS = 8
HGRN_HEAD_DIM = 128
HGRN_WIDTH = HGRN_HEADS * HGRN_HEAD_DIM
MOBA_HEADS = 8
MOBA_HEAD_DIM = 128
MOBA_WIDTH = MOBA_HEADS * MOBA_HEAD_DIM
MOBA_BLOCK = 256
MOBA_TOPK = 3
ROPE_THETA = 10000.0
N_GROUPS = 4
EXPERTS_PER_GROUP = 8
N_EXPERTS = N_GROUPS * EXPERTS_PER_GROUP
TOPK_IN_GROUP = 2
D_EXPERT = 512
EPS = 1e-6
NEG_INF = -1e30

V7X_LANES = 128
V7X_SUBLANES = 8
V7X_VMEM_BUDGET_BYTES = 56 * 1024 * 1024

HGRN_CHUNK = 128
MOE_ROWS = 256


def _cparams(n_grid, vmem_bytes):
    return pltpu.CompilerParams(
        dimension_semantics=("arbitrary",) * n_grid,
        vmem_limit_bytes=int(min(max(vmem_bytes, 16 * 1024 * 1024), V7X_VMEM_BUDGET_BYTES)),
    )


def _nbytes(shape, dtype):
    n = 1
    for s in shape:
        n *= s
    return n * jnp.dtype(dtype).itemsize


def _rms(x, g):
    ms = jnp.mean(x * x, axis=-1, keepdims=True)
    return x * lax.rsqrt(ms + EPS) * g


def _ep_identity(acc):
    return acc


def _ep_silu(acc):
    return acc * jax.nn.sigmoid(acc)


def _ep_sigmoid(acc):
    return jax.nn.sigmoid(acc)


def _ep_logf(acc, la_ref, lc_ref):
    ls = jnp.minimum(acc, 0.0) - jnp.log(1.0 + jnp.exp(-jnp.abs(acc)))
    u = la_ref[...]
    v = lc_ref[...] + ls
    return jnp.maximum(u, v) + jnp.log(1.0 + jnp.exp(-jnp.abs(u - v)))


def _ep_rope(acc, cos_ref, sin_ref, *, scale):
    cos = cos_ref[...]
    sin = sin_ref[...]
    outs = []
    for hh in range(acc.shape[1] // MOBA_HEAD_DIM):
        a = acc[:, hh * MOBA_HEAD_DIM:(hh + 1) * MOBA_HEAD_DIM]
        r = pltpu.roll(a, MOBA_HEAD_DIM // 2, axis=1)
        outs.append((a * cos + r * sin) * scale)
    return jnp.concatenate(outs, axis=1)


def _ep_rope_aug(acc, cos_ref, sin_ref, oh_ref):
    cos = cos_ref[...]
    sin = sin_ref[...]
    oh = oh_ref[...]
    outs = []
    for hh in range(acc.shape[1] // MOBA_HEAD_DIM):
        a = acc[:, hh * MOBA_HEAD_DIM:(hh + 1) * MOBA_HEAD_DIM]
        outs.append(a * cos + pltpu.roll(a, MOBA_HEAD_DIM // 2, axis=1) * sin)
        outs.append(oh)
    return jnp.concatenate(outs, axis=1)


def _cast_weight_once(w_ref, wb_ref, row_axis):
    @pl.when(pl.program_id(row_axis) == 0)
    def _():
        wb_ref[...] = w_ref[...].astype(wb_ref.dtype)


def _proj_kernel(h_ref, w_ref, *refs, epilogue):
    *extra, o_ref, wb_ref = refs
    _cast_weight_once(w_ref, wb_ref, 1)
    acc = jnp.dot(h_ref[...], wb_ref[...], preferred_element_type=F32)
    o_ref[...] = epilogue(acc, *extra).astype(o_ref.dtype)


def _proj(h, w, col0, ncols, epilogue, out_dtype, row_extras=(), col_extras=(), tm=1024, tn=1024, widen=1):
    T, K = h.shape
    tn = min(tn, ncols)
    tm = min(tm, T)
    cb = col0 // tn
    otn = widen * tn
    in_specs = [
        pl.BlockSpec((tm, K), lambda j, i: (i, 0)),
        pl.BlockSpec((K, tn), lambda j, i: (0, cb + j)),
    ]
    for e in row_extras:
        in_specs.append(pl.BlockSpec((tm, e.shape[1]), lambda j, i: (i, 0)))
    for e in col_extras:
        in_specs.append(pl.BlockSpec((1, tn), lambda j, i: (0, j)))
    vmem = 2 * (_nbytes((tm, K), h.dtype) + _nbytes((K, tn), w.dtype) + _nbytes((tm, otn), out_dtype))
    vmem += 3 * _nbytes((tm, otn), F32) + _nbytes((K, tn), BF16)
    return pl.pallas_call(
        functools.partial(_proj_kernel, epilogue=epilogue),
        out_shape=jax.ShapeDtypeStruct((T, widen * ncols), out_dtype),
        grid=(ncols // tn, T // tm),
        in_specs=in_specs,
        out_specs=pl.BlockSpec((tm, otn), lambda j, i: (i, j)),
        scratch_shapes=[pltpu.VMEM((K, tn), BF16)],
        compiler_params=_cparams(2, vmem),
        name="proj",
    )(h, w, *row_extras, *col_extras)


def _norm_proj_kernel(x_ref, g_ref, w_ref, o_ref, h_ref, wb_ref, *, epilogue):
    _cast_weight_once(w_ref, wb_ref, 0)
    h = _rms(x_ref[...], g_ref[...]).astype(h_ref.dtype)
    h_ref[...] = h
    o_ref[...] = epilogue(jnp.dot(h, wb_ref[...], preferred_element_type=F32)).astype(o_ref.dtype)


def _norm_proj(x, g, w, col0, ncols, epilogue, out_dtype, tm=512):
    T, K = x.shape
    cb = col0 // ncols
    vmem = 2 * (_nbytes((tm, K), F32) + _nbytes((tm, ncols), out_dtype) + _nbytes((tm, K), BF16))
    vmem += _nbytes((K, ncols), F32) + _nbytes((K, ncols), BF16) + 2 * _nbytes((tm, K), F32) + 3 * _nbytes((tm, ncols), F32)
    return pl.pallas_call(
        functools.partial(_norm_proj_kernel, epilogue=epilogue),
        out_shape=(jax.ShapeDtypeStruct((T, ncols), out_dtype), jax.ShapeDtypeStruct((T, K), BF16)),
        grid=(T // tm,),
        in_specs=[
            pl.BlockSpec((tm, K), lambda i: (i, 0)),
            pl.BlockSpec((1, K), lambda i: (0, 0)),
            pl.BlockSpec((K, ncols), lambda i: (0, cb), pipeline_mode=pl.Buffered(1)),
        ],
        out_specs=(pl.BlockSpec((tm, ncols), lambda i: (i, 0)), pl.BlockSpec((tm, K), lambda i: (i, 0))),
        scratch_shapes=[pltpu.VMEM((K, ncols), BF16)],
        compiler_params=_cparams(1, vmem),
        name="norm_proj",
    )(x, g.reshape(1, K), w)


MOBA_VT_ROWS = MOBA_HEAD_DIM + 16
MOBA_BLOCKS_PER_STEP = 4


def _proj_t_kernel(h_ref, w_ref, *refs, epilogue, ones_rows):
    *extra, o_ref, wb_ref = refs
    BS, HD = MOBA_BLOCK, MOBA_HEAD_DIM
    _cast_weight_once(w_ref, wb_ref, 0)
    acc = epilogue(jnp.dot(h_ref[...], wb_ref[...], preferred_element_type=F32), *extra)
    ones = jnp.ones((ones_rows, BS), F32) if ones_rows else None
    for b in range(acc.shape[0] // BS):
        parts = []
        for hh in range(acc.shape[1] // HD):
            parts.append(acc[b * BS:(b + 1) * BS, hh * HD:(hh + 1) * HD].T)
            if ones_rows:
                parts.append(ones)
        o_ref[b] = jnp.concatenate(parts, axis=0).astype(o_ref.dtype)


def _proj_t(h, w, col0, epilogue, row_extras=(), ones_rows=0, tm=1024):
    T, K = h.shape
    tn = MOBA_WIDTH
    tm = min(tm, T)
    cb = col0 // tn
    rows = MOBA_HEADS * (MOBA_HEAD_DIM + ones_rows)
    in_specs = [pl.BlockSpec((tm, K), lambda i: (i, 0)), pl.BlockSpec((K, tn), lambda i: (0, cb))]
    for e in row_extras:
        in_specs.append(pl.BlockSpec((tm, e.shape[1]), lambda i: (i, 0)))
    vmem = 2 * (_nbytes((tm, K), h.dtype) + _nbytes((K, tn), w.dtype) + _nbytes((tm, 2 * tn), BF16))
    vmem += 4 * _nbytes((tm, tn), F32) + _nbytes((K, tn), BF16)
    return pl.pallas_call(
        functools.partial(_proj_t_kernel, epilogue=epilogue, ones_rows=ones_rows),
        out_shape=jax.ShapeDtypeStruct((T // MOBA_BLOCK, rows, MOBA_BLOCK), BF16),
        grid=(T // tm,),
        in_specs=in_specs,
        out_specs=pl.BlockSpec((tm // MOBA_BLOCK, rows, MOBA_BLOCK), lambda i: (i, 0, 0)),
        scratch_shapes=[pltpu.VMEM((K, tn), BF16)],
        compiler_params=_cparams(1, vmem),
        name="proj_t",
    )(h, w, *row_extras)


def _hgrn_kernel(q_ref, g_ref, v_ref, og_ref, ng_ref, o_ref, st_ref, code_ref):
    W, HD, NH, C, S = HGRN_WIDTH, HGRN_HEAD_DIM, HGRN_HEADS, HGRN_CHUNK, V7X_SUBLANES
    J = C // S

    @pl.when(pl.program_id(0) == 0)
    def _():
        st_ref[...] = jnp.zeros_like(st_ref)
        tr = lax.broadcasted_iota(I32, (C, C), 0)
        tc = lax.broadcasted_iota(I32, (C, C), 1)
        xr = tr ^ tc
        code = jnp.zeros((C, C), I32)
        for lvl in range(1, 8):
            code = jnp.where(xr >= (1 << (lvl - 1)), lvl, code)
        code_ref[...] = jnp.where(tc > tr, -1, code)

    def r3(x):
        return x.astype(F32).reshape(J, S, W)

    def sub_bcast(x3, r):
        return jnp.broadcast_to(x3[:, r:r + 1, :], x3.shape)

    g3, q3, v3 = r3(g_ref[...]), r3(q_ref[...]), r3(v_ref[...])
    sub = lax.broadcasted_iota(I32, (1, S, W), 1)

    c3 = g3
    for s in (1, 2, 4):
        c3 = c3 + jnp.where(sub >= s, pltpu.roll(c3, s, axis=1), 0.0)
    run = jnp.zeros((1, 1, W), F32)
    carry = []
    for j in range(J):
        carry.append(run)
        run = run + c3[j:j + 1, S - 1:S, :]
    b3 = c3 + jnp.concatenate(carry, axis=0)
    bC = run

    k3 = 1.0 - jnp.exp(g3)
    qe3 = q3 * jnp.exp(b3)
    ks3 = k3 * jnp.exp(bC - b3)

    levels = [(0, q3, k3)]
    ref1 = jnp.where(sub % 2 == 0, b3, pltpu.roll(b3, 1, axis=1))
    ref2 = jnp.where(sub < 4, sub_bcast(b3, 1), sub_bcast(b3, 5))
    ref4 = sub_bcast(b3, 3)
    for lvl, (ref, upper) in enumerate(((ref1, sub % 2 == 1), (ref2, sub % 4 >= 2), (ref4, sub >= 4)), start=1):
        e = jnp.exp(-jnp.abs(b3 - ref))
        levels.append((lvl, jnp.where(upper, q3 * e, 0.0), jnp.where(upper, 0.0, k3 * e)))
    zero_group = jnp.zeros((1, S, W), F32)
    for lvl, half in enumerate((1, 2, 4, 8), start=4):
        qparts, kparts = [], []
        for j in range(J):
            jr = (j // (2 * half)) * (2 * half) + half - 1
            ref = b3[jr:jr + 1, S - 1:S, :]
            if (j % (2 * half)) >= half:
                qparts.append(q3[j:j + 1] * jnp.exp(b3[j:j + 1] - ref))
                kparts.append(zero_group)
            else:
                qparts.append(zero_group)
                kparts.append(k3[j:j + 1] * jnp.exp(ref - b3[j:j + 1]))
        levels.append((lvl, jnp.concatenate(qparts, axis=0), jnp.concatenate(kparts, axis=0)))

    code = code_ref[...]

    def mat(x3, h):
        return x3.reshape(C, W)[:, h * HD:(h + 1) * HD].astype(BF16)

    nt = (((1,), (1,)), ((), ()))
    tn = (((0,), (0,)), ((), ()))
    ebc = jnp.exp(bC).reshape(1, W)
    ng = ng_ref[...]
    for h in range(NH):
        a_mat = jnp.zeros((C, C), F32)
        for lvl, qr, kr in levels:
            s = lax.dot_general(mat(qr, h), mat(kr, h), nt, preferred_element_type=F32)
            a_mat = jnp.where(code == lvl, s, a_mat)
        vh = mat(v3, h)
        st = st_ref[h]
        o = jnp.dot(a_mat.astype(BF16), vh, preferred_element_type=F32)
        o = o + lax.dot_general(mat(qe3, h), st.astype(BF16), nt, preferred_element_type=F32)
        o = _rms(o, ng) * og_ref[:, h * HD:(h + 1) * HD].astype(F32)
        o_ref[:, h * HD:(h + 1) * HD] = o.astype(o_ref.dtype)
        st_ref[h] = st * ebc[:, h * HD:(h + 1) * HD] + lax.dot_general(
            vh, mat(ks3, h), tn, preferred_element_type=F32)


def _hgrn(q, logf, v, og, norm_g):
    T, W = q.shape
    C = HGRN_CHUNK
    blk = pl.BlockSpec((C, W), lambda c: (c, 0))
    vmem = 64 * _nbytes((C, W), F32)
    return pl.pallas_call(
        _hgrn_kernel,
        out_shape=jax.ShapeDtypeStruct((T, W), BF16),
        grid=(T // C,),
        in_specs=[blk, blk, blk, blk, pl.BlockSpec((1, HGRN_HEAD_DIM), lambda c: (0, 0))],
        out_specs=blk,
        scratch_shapes=[pltpu.VMEM((HGRN_HEADS, HGRN_HEAD_DIM, HGRN_HEAD_DIM), F32), pltpu.VMEM((C, C), I32)],
        compiler_params=_cparams(1, vmem),
        name="hgrn",
    )(q, logf, v, og, norm_g.reshape(1, HGRN_HEAD_DIM))


def _moba_kernel(qt_ref, k_ref, vt_ref, o_ref, km_ref):
    BS, HD, VR = MOBA_BLOCK, MOBA_HEAD_DIM, MOBA_VT_ROWS
    T = k_ref.shape[0]
    NB = T // BS
    G = o_ref.shape[1] // HD
    cur = pl.program_id(1)

    @pl.when(cur == 0)
    def _():
        for g in range(G):
            kf = k_ref[:, 2 * g * HD:(2 * g + 1) * HD].astype(F32).reshape(NB, BS, HD)
            km_ref[g] = jnp.sum(kf, axis=1) * (1.0 / BS)

    blk = lax.broadcasted_iota(I32, (NB, BS), 0)
    key = lax.broadcasted_iota(I32, (BS, BS), 0)
    qry = lax.broadcasted_iota(I32, (BS, BS), 1)
    pad = jnp.zeros((V7X_LANES - NB, BS), F32)
    r0 = pl.multiple_of(cur * BS, BS)
    qts = [qt_ref[0, g * HD:(g + 1) * HD, :] for g in range(G)]
    gts = [jnp.dot(km_ref[g].astype(BF16), qts[g], preferred_element_type=F32) for g in range(G)]
    s_own = [jnp.dot(k_ref[pl.ds(r0, BS), 2 * g * HD:(2 * g + 1) * HD], qts[g], preferred_element_type=F32)
             for g in range(G)]
    qcs = []
    for g in range(G):
        gt = jnp.where(blk < cur, gts[g], NEG_INF)
        sel = jnp.zeros((NB, BS), jnp.bool_)
        for _ in range(MOBA_TOPK):
            mx = jnp.max(gt, axis=0, keepdims=True)
            idx = jnp.min(jnp.where(gt == mx, blk, NB), axis=0, keepdims=True)
            pick = (blk == idx) & (mx > 0.5 * NEG_INF)
            sel = sel | pick
            gt = jnp.where(pick, NEG_INF, gt)
        pen = jnp.concatenate([jnp.where(sel, 0.0, NEG_INF), pad], axis=0).astype(BF16)
        qcs.append(jnp.concatenate([qts[g], pen], axis=0))

    def scores(n):
        rn = pl.multiple_of(n * BS, BS)
        return tuple(jnp.dot(k_ref[pl.ds(rn, BS), 2 * g * HD:(2 * g + 2) * HD], qcs[g], preferred_element_type=F32)
                     for g in range(G))

    ms, ps = [], []
    for g in range(G):
        s = jnp.where(key <= qry, s_own[g], NEG_INF)
        m0 = jnp.max(s, axis=0, keepdims=True)
        ms.append(m0)
        ps.append(jnp.exp(s - m0).astype(BF16))
    accs = [jnp.dot(vt_ref[cur, g * VR:(g + 1) * VR, :], ps[g], preferred_element_type=F32)
            for g in range(G)]

    KB = MOBA_BLOCKS_PER_STEP

    def body(c, carry):
        ms, accs = carry
        r = pl.multiple_of(c * KB * BS, KB * BS)
        sns = [jnp.dot(k_ref[pl.ds(r, KB * BS), 2 * g * HD:(2 * g + 2) * HD], qcs[g], preferred_element_type=F32)
               for g in range(G)]
        new_ms, alphas, pns = [], [], []
        for g in range(G):
            m_new = jnp.maximum(ms[g], jnp.max(sns[g], axis=0, keepdims=True))
            alphas.append(jnp.exp(ms[g] - m_new))
            pns.append(jnp.exp(sns[g] - m_new).astype(BF16))
            new_ms.append(m_new)
        new_accs = []
        for g in range(G):
            pv = alphas[g] * accs[g]
            for j in range(KB):
                pv = pv + jnp.dot(vt_ref[KB * c + j, g * VR:(g + 1) * VR, :], pns[g][j * BS:(j + 1) * BS],
                                  preferred_element_type=F32)
            new_accs.append(pv)
        return tuple(new_ms), tuple(new_accs)

    _, accs = lax.fori_loop(0, (cur + KB - 1) // KB, body, (tuple(ms), tuple(accs)))
    for g in range(G):
        ot = accs[g][:HD, :] / accs[g][HD:HD + 1, :]
        o_ref[:, g * HD:(g + 1) * HD] = ot.T.astype(o_ref.dtype)


def _moba(mqt, mk_aug, mvt, heads_per_step=4):
    T = mk_aug.shape[0]
    BS, HD, G, VR = MOBA_BLOCK, MOBA_HEAD_DIM, heads_per_step, MOBA_VT_ROWS
    NB = T // BS
    vmem = _nbytes((T, 2 * G * HD), BF16) + _nbytes((NB, G * VR, BS), BF16) + 8 * _nbytes((BS, G * HD), BF16)
    vmem += 16 * G * _nbytes((BS, BS), F32)
    resident = pl.Buffered(1)
    return pl.pallas_call(
        _moba_kernel,
        out_shape=jax.ShapeDtypeStruct((T, MOBA_WIDTH), BF16),
        grid=(MOBA_HEADS // G, NB),
        in_specs=[
            pl.BlockSpec((1, G * HD, BS), lambda h, i: (i, h, 0)),
            pl.BlockSpec((T, 2 * G * HD), lambda h, i: (0, h), pipeline_mode=resident),
            pl.BlockSpec((NB, G * VR, BS), lambda h, i: (0, h, 0), pipeline_mode=resident),
        ],
        out_specs=pl.BlockSpec((BS, G * HD), lambda h, i: (i, h)),
        scratch_shapes=[pltpu.VMEM((G, NB, HD), F32)],
        compiler_params=_cparams(2, vmem),
        name="moba",
    )(mqt, mk_aug, mvt)


def _merge_kernel(oh_ref, om_ref, ga_ref, gb_ref, wh_ref, wm_ref, o_ref):
    a = jnp.dot(oh_ref[...], wh_ref[...], preferred_element_type=F32)
    b = jnp.dot(om_ref[...], wm_ref[...], preferred_element_type=F32)
    o_ref[...] = (ga_ref[...].astype(F32) * a + gb_ref[...].astype(F32) * b).astype(o_ref.dtype)


def _merge(o_hgrn, o_moba, gates, w_up_hgrn, w_up_moba, tm=512):
    T = o_hgrn.shape[0]
    D = D_MODEL
    vmem = 2 * (_nbytes((tm, HGRN_WIDTH), F32) + _nbytes((tm, MOBA_WIDTH), BF16) + 3 * _nbytes((tm, D), BF16)
                + 2 * _nbytes((HGRN_WIDTH, D), BF16)) + 3 * _nbytes((tm, D), F32)
    return pl.pallas_call(
        _merge_kernel,
        out_shape=jax.ShapeDtypeStruct((T, D), BF16),
        grid=(T // tm,),
        in_specs=[
            pl.BlockSpec((tm, HGRN_WIDTH), lambda i: (i, 0)),
            pl.BlockSpec((tm, MOBA_WIDTH), lambda i: (i, 0)),
            pl.BlockSpec((tm, D), lambda i: (i, 0)),
            pl.BlockSpec((tm, D), lambda i: (i, 1)),
            pl.BlockSpec((HGRN_WIDTH, D), lambda i: (0, 0)),
            pl.BlockSpec((MOBA_WIDTH, D), lambda i: (0, 0)),
        ],
        out_specs=pl.BlockSpec((tm, D), lambda i: (i, 0)),
        compiler_params=_cparams(1, vmem),
        name="merge",
    )(o_hgrn, o_moba, gates, gates, w_up_hgrn, w_up_moba)


def _pack_bf16_pairs(x):
    C = x.shape[1] // 2
    b = lax.bitcast_convert_type(x, jnp.uint32)
    r = (b + jnp.uint32(0x7FFF) + ((b >> 16) & jnp.uint32(1))) >> 16
    return r[:, :C] | (r[:, C:] << 16)


def _unpack_bf16_pairs(p):
    lo = lax.bitcast_convert_type(p << 16, F32)
    hi = lax.bitcast_convert_type(p & jnp.uint32(0xFFFF0000), F32)
    return jnp.concatenate([lo, hi], axis=1)


def _outproj_kernel(m_ref, x_ref, w_ref, g_ref, wr_ref, x1_ref, h2p_ref, info_ref, cnt_ref, prev_ref, carry_ref):
    step = pl.program_id(0)
    n_tiles = pl.num_programs(0) - 1

    @pl.when(step == 0)
    def _():
        carry_ref[...] = jnp.zeros_like(carry_ref)

    def finish_previous():
        h2 = _rms(prev_ref[...], g_ref[...])
        h2p_ref[...] = _pack_bf16_pairs(h2)
        info_ref[...] = _route(h2, wr_ref, carry_ref)
        cnt_ref[...] = carry_ref[...]

    def project():
        return x_ref[...] + jnp.dot(m_ref[...], w_ref[...], preferred_element_type=F32)

    @pl.when(step == 0)
    def _():
        x1 = project()
        x1_ref[...] = x1
        prev_ref[...] = x1

    @pl.when((step > 0) & (step < n_tiles))
    def _():
        x1 = project()
        finish_previous()
        x1_ref[...] = x1
        prev_ref[...] = x1

    @pl.when(step == n_tiles)
    def _():
        finish_previous()


def _outproj(merged, x, w_out, g_ffn, w_router, tm=512):
    T, D = x.shape
    n_tiles = T // tm
    vmem = 2 * (_nbytes((tm, D), BF16) + 3 * _nbytes((tm, D), F32) + _nbytes((D, D), BF16)) + 7 * _nbytes((tm, D), F32)
    cur = lambda i: (jnp.minimum(i, n_tiles - 1), 0)
    prev = lambda i: (jnp.maximum(i - 1, 0), 0)
    fixed = lambda i: (0, 0)
    return pl.pallas_call(
        _outproj_kernel,
        out_shape=(jax.ShapeDtypeStruct((T, D), F32), jax.ShapeDtypeStruct((T, D // 2), jnp.uint32),
                   jax.ShapeDtypeStruct((T, V7X_LANES), F32), jax.ShapeDtypeStruct((1, V7X_LANES), F32)),
        grid=(n_tiles + 1,),
        in_specs=[
            pl.BlockSpec((tm, D), cur),
            pl.BlockSpec((tm, D), cur),
            pl.BlockSpec((D, D), fixed, pipeline_mode=pl.Buffered(1)),
            pl.BlockSpec((1, D), fixed),
            pl.BlockSpec((D, V7X_LANES), fixed),
        ],
        out_specs=(pl.BlockSpec((tm, D), cur), pl.BlockSpec((tm, D // 2), prev), pl.BlockSpec((tm, V7X_LANES), prev),
                   pl.BlockSpec((1, V7X_LANES), fixed)),
        scratch_shapes=[pltpu.VMEM((tm, D), F32), pltpu.VMEM((1, V7X_LANES), F32)],
        compiler_params=_cparams(1, vmem),
        name="outproj",
    )(merged, x, w_out, g_ffn.reshape(1, D), w_router)


def _route(h2, w_ref, carry_ref):
    tm = h2.shape[0]
    logits = jnp.dot(h2.astype(BF16), w_ref[...], preferred_element_type=F32)
    lane = lax.broadcasted_iota(I32, (tm, V7X_LANES), 1)
    is_g = lane < N_GROUPS
    gl = jnp.where(is_g, logits, NEG_INF)
    gmax = jnp.max(gl, axis=1, keepdims=True)
    g_sel = jnp.min(jnp.where(gl == gmax, lane, V7X_LANES), axis=1, keepdims=True)
    gsum = jnp.sum(jnp.where(is_g, jnp.exp(gl - gmax), 0.0), axis=1, keepdims=True)
    p_group = 1.0 / gsum
    lo = N_GROUPS + EXPERTS_PER_GROUP * g_sel
    emask = (lane >= lo) & (lane < lo + EXPERTS_PER_GROUP)
    el = jnp.where(emask, logits, NEG_INF)
    e1 = jnp.max(el, axis=1, keepdims=True)
    i1 = jnp.min(jnp.where((el == e1) & emask, lane, V7X_LANES), axis=1, keepdims=True)
    emask2 = emask & (lane != i1)
    el2 = jnp.where(emask2, logits, NEG_INF)
    e2 = jnp.max(el2, axis=1, keepdims=True)
    i2 = jnp.min(jnp.where((el2 == e2) & emask2, lane, V7X_LANES), axis=1, keepdims=True)
    r = jnp.exp(e2 - e1)
    w1 = p_group / (1.0 + r)
    w2 = p_group * r / (1.0 + r)
    eid1 = i1 - N_GROUPS
    eid2 = i2 - N_GROUPS
    oh1 = jnp.where(lane == eid1, 1.0, 0.0)
    oh2 = jnp.where(lane == eid2, 1.0, 0.0)
    cnt = oh1 + oh2
    tri = jnp.where(lax.broadcasted_iota(I32, (tm, tm), 0) > lax.broadcasted_iota(I32, (tm, tm), 1), 1.0, 0.0)
    before = jnp.dot(tri.astype(BF16), cnt.astype(BF16), preferred_element_type=F32) + carry_ref[...]
    rank1 = jnp.sum(oh1 * before, axis=1, keepdims=True)
    rank2 = jnp.sum(oh2 * before, axis=1, keepdims=True)
    carry_ref[...] = carry_ref[...] + jnp.sum(cnt, axis=0, keepdims=True)
    info = jnp.zeros((tm, V7X_LANES), F32)
    for k, val in enumerate((eid1.astype(F32), eid2.astype(F32), w1, w2, rank1, rank2)):
        info = jnp.where(lane == k, val, info)
    return info


def _row_copy(src_ref, src_row, dst_ref, dst_row, sem):
    return pltpu.make_async_copy(src_ref.at[pl.ds(src_row, 1), :], dst_ref.at[pl.ds(dst_row, 1), :], sem)


ROW_DMA_UNROLL = 8
RANK_BITS = 16


def _slot(code, pstart_ref):
    return pstart_ref[lax.shift_right_logical(code, RANK_BITS)] + (code & ((1 << RANK_BITS) - 1))


def _dispatch_kernel(code_ref, pstart_ref, pend_ref, h_ref, xs_ref, zero_ref, sem):
    tm = h_ref.shape[0]
    step = pl.program_id(0)
    base = step * tm

    @pl.when(step == 0)
    def _():
        zero_ref[...] = jnp.zeros_like(zero_ref)

        def tail(e):
            return pltpu.make_async_copy(
                zero_ref, xs_ref.at[pl.ds(pl.multiple_of(pend_ref[e] - MOE_ROWS, MOE_ROWS), MOE_ROWS), :], sem)

        def nonempty(e):
            return pend_ref[e] > (pend_ref[e - 1] if e else 0)

        def unused(b):
            return pltpu.make_async_copy(
                zero_ref, xs_ref.at[pl.ds(pl.multiple_of(b * MOE_ROWS, MOE_ROWS), MOE_ROWS), :], sem)

        first_unused = pend_ref[N_EXPERTS - 1] // MOE_ROWS
        n_blocks = xs_ref.shape[0] // MOE_ROWS
        for e in range(N_EXPERTS):
            pl.when(nonempty(e))(lambda e=e: tail(e).start())
        lax.fori_loop(first_unused, n_blocks, lambda b, c: (unused(b).start(), c)[1], 0)
        for e in range(N_EXPERTS):
            pl.when(nonempty(e))(lambda e=e: tail(e).wait())
        lax.fori_loop(first_unused, n_blocks, lambda b, c: (unused(b).wait(), c)[1], 0)

    def issue(rb, c):
        for u in range(ROW_DMA_UNROLL):
            r = rb * ROW_DMA_UNROLL + u
            for k in range(TOPK_IN_GROUP):
                slot = _slot(code_ref[(base + r) * TOPK_IN_GROUP + k], pstart_ref)
                _row_copy(h_ref, r, xs_ref, slot, sem).start(priority=k % 2)
        return c

    lax.fori_loop(0, tm // ROW_DMA_UNROLL, issue, 0)
    for _ in range(tm * TOPK_IN_GROUP):
        _row_copy(h_ref, 0, xs_ref, 0, sem).wait()


def _dispatch(codes, pad_start, pad_end, h2p, m_pad, tm=256):
    T, C = h2p.shape
    return pl.pallas_call(
        _dispatch_kernel,
        out_shape=jax.ShapeDtypeStruct((m_pad, C), h2p.dtype),
        grid_spec=pltpu.PrefetchScalarGridSpec(
            num_scalar_prefetch=3,
            grid=(T // tm,),
            in_specs=[pl.BlockSpec((tm, C), lambda i, c, ps, pe: (i, 0))],
            out_specs=pl.BlockSpec(memory_space=pl.ANY),
            scratch_shapes=[pltpu.VMEM((MOE_ROWS, C), h2p.dtype), pltpu.SemaphoreType.DMA(())],
        ),
        compiler_params=_cparams(1, 6 * _nbytes((tm, C), h2p.dtype)),
        name="dispatch",
    )(codes, pad_start, pad_end, h2p)


def _expert_kernel(seg_ref, sexp_ref, nu_ref, xs_ref, wg_ref, wu_ref, wd_ref, y_ref, wgf_ref, wuf_ref, wdf_ref,
                   wgb_ref, wub_ref, wdb_ref, sem):
    b = pl.program_id(0)
    used = b < nu_ref[0]
    seg = seg_ref[b]
    first = used & ((b == 0) | (seg != seg_ref[jnp.maximum(b - 1, 0)]))

    def weights_copy(s, part):
        slot = s % 2
        src, dst = ((wg_ref, wgf_ref), (wu_ref, wuf_ref), (wd_ref, wdf_ref))[part]
        return pltpu.make_async_copy(src.at[sexp_ref[s]], dst.at[slot], sem.at[slot])

    @pl.when(b == 0)
    def _():
        for part in range(3):
            weights_copy(0, part).start()

    @pl.when(first)
    def _():
        for part in range(3):
            weights_copy(seg, part).wait()
        slot = seg % 2
        wgb_ref[...] = wgf_ref[slot].astype(BF16)
        wub_ref[...] = wuf_ref[slot].astype(BF16)
        wdb_ref[...] = wdf_ref[slot].astype(BF16)

    @pl.when(first & (seg + 1 < nu_ref[1]))
    def _():
        for part in range(3):
            weights_copy(seg + 1, part).start()

    @pl.when(used)
    def _():
        x = _unpack_bf16_pairs(xs_ref[...]).astype(BF16)
        a = jnp.dot(x, wgb_ref[...], preferred_element_type=F32)
        u = jnp.dot(x, wub_ref[...], preferred_element_type=F32)
        hm = (a * jax.nn.sigmoid(a) * u).astype(BF16)
        y_ref[...] = _pack_bf16_pairs(jnp.dot(hm, wdb_ref[...], preferred_element_type=F32))

    @pl.when(jnp.logical_not(used))
    def _():
        y_ref[...] = jnp.zeros_like(y_ref)


def _experts(block_segment, segment_expert, n_used, xs, w_gate, w_up, w_down):
    m_pad, C = xs.shape
    D = 2 * C
    R, Fd = MOE_ROWS, D_EXPERT
    vmem = 4 * _nbytes((R, C), xs.dtype) + 6 * _nbytes((D, Fd), F32) + 3 * _nbytes((D, Fd), BF16)
    vmem += 6 * _nbytes((R, D), F32)
    hbm = pl.BlockSpec(memory_space=pl.ANY)
    return pl.pallas_call(
        _expert_kernel,
        out_shape=jax.ShapeDtypeStruct((m_pad, C), xs.dtype),
        grid_spec=pltpu.PrefetchScalarGridSpec(
            num_scalar_prefetch=3,
            grid=(m_pad // R,),
            in_specs=[
                pl.BlockSpec((R, C), lambda b, sg, se, nu: (jnp.minimum(b, jnp.maximum(nu[0] - 1, 0)), 0)),
                hbm, hbm, hbm,
            ],
            out_specs=pl.BlockSpec((R, C), lambda b, sg, se, nu: (b, 0)),
            scratch_shapes=[pltpu.VMEM((2, D, Fd), F32), pltpu.VMEM((2, D, Fd), F32), pltpu.VMEM((2, Fd, D), F32),
                            pltpu.VMEM((D, Fd), BF16), pltpu.VMEM((D, Fd), BF16), pltpu.VMEM((Fd, D), BF16),
                            pltpu.SemaphoreType.DMA((2,))],
        ),
        compiler_params=_cparams(1, vmem),
        name="experts",
    )(block_segment, segment_expert, n_used, xs, w_gate, w_up, w_down)


def _combine_kernel(code_ref, pstart_ref, x1_ref, info_ref, yb_ref, p_ref, gp_ref, wpg_ref, wpp_ref, gf_ref, o_ref,
                    ybuf, sem):
    tm, D = x1_ref.shape
    step = pl.program_id(0)
    n_tiles = pl.num_programs(0) - 2
    K = TOPK_IN_GROUP
    GROUPS = 8

    def fetch(tile, group=None):
        slot = tile % 2
        rows = range(tm) if group is None else range(group * tm // GROUPS, (group + 1) * tm // GROUPS)
        for r in rows:
            for k in range(K):
                src = _slot(code_ref[(tile * tm + r) * K + k], pstart_ref)
                _row_copy(yb_ref, src, ybuf.at[slot * K + k], r, sem.at[slot]).start(priority=k % 2)

    def drain(tile):
        slot = tile % 2
        for _ in range(tm * K):
            _row_copy(yb_ref, 0, ybuf.at[0], 0, sem.at[slot]).wait()

    @pl.when(step == 0)
    def _():
        fetch(step)

    @pl.when(step > 0)
    def _():
        drain(step - 1)

    @pl.when((step > 0) & (step <= n_tiles))
    def _():
        tile = step - 1
        slot = tile % 2
        info = info_ref[...]
        x2 = (x1_ref[...] + info[:, 2:3] * _unpack_bf16_pairs(ybuf[slot * K])
              + info[:, 3:4] * _unpack_bf16_pairs(ybuf[slot * K + 1]))
        hp = _rms(x2, gp_ref[...]).astype(BF16)
        pp = jnp.dot(p_ref[...].astype(BF16), wpp_ref[...], preferred_element_type=F32)
        cw = D // GROUPS
        zs = []
        for c in range(GROUPS):
            fetch(step, c)
            zs.append(jnp.dot(hp, wpg_ref[:, c * cw:(c + 1) * cw], preferred_element_type=F32))
        x3 = x2 + jax.nn.sigmoid(jnp.concatenate(zs, axis=1)) * pp
        o_ref[...] = _rms(x3, gf_ref[...])


def _combine(codes, pad_start, x1, info, yb, p, g_ple, w_ple_gate, w_ple_proj, g_final, tm=256):
    T, D = x1.shape
    n_tiles = T // tm
    codes_padded = jnp.concatenate([codes, jnp.zeros((tm * TOPK_IN_GROUP,), I32)])
    vmem = 4 * _nbytes((tm, D), F32) + _nbytes((D, D), BF16) + _nbytes((PLE_DIM, D), BF16)
    vmem += 2 * _nbytes((tm, PLE_DIM), F32) + 2 * TOPK_IN_GROUP * _nbytes((tm, D // 2), yb.dtype)
    vmem += 6 * _nbytes((tm, D), F32)
    tile = lambda i, c, ps: (jnp.clip(i - 1, 0, n_tiles - 1), 0)
    fixed = lambda i, c, ps: (0, 0)
    resident = pl.Buffered(1)
    return pl.pallas_call(
        _combine_kernel,
        out_shape=jax.ShapeDtypeStruct((T, D), F32),
        grid_spec=pltpu.PrefetchScalarGridSpec(
            num_scalar_prefetch=2,
            grid=(n_tiles + 2,),
            in_specs=[
                pl.BlockSpec((tm, D), tile),
                pl.BlockSpec((tm, V7X_LANES), tile),
                pl.BlockSpec(memory_space=pl.ANY),
                pl.BlockSpec((tm, PLE_DIM), tile),
                pl.BlockSpec((1, D), fixed),
                pl.BlockSpec((D, D), fixed, pipeline_mode=resident),
                pl.BlockSpec((PLE_DIM, D), fixed, pipeline_mode=resident),
                pl.BlockSpec((1, D), fixed),
            ],
            out_specs=pl.BlockSpec((tm, D), tile),
            scratch_shapes=[pltpu.VMEM((2 * TOPK_IN_GROUP, tm, yb.shape[1]), yb.dtype),
                            pltpu.SemaphoreType.DMA((2,))],
        ),
        compiler_params=_cparams(1, vmem),
        name="combine",
    )(codes_padded, pad_start, x1, info, yb, p, g_ple.reshape(1, D), w_ple_gate, w_ple_proj, g_final.reshape(1, D))


def _rope_tables(T):
    half, BS = MOBA_HEAD_DIM // 2, MOBA_BLOCK
    inv_freq = ROPE_THETA ** (-jnp.arange(half, dtype=F32) / half)
    ang_a = (jnp.arange(T // BS, dtype=F32) * BS)[:, None, None] * inv_freq
    ang_b = jnp.arange(BS, dtype=F32)[None, :, None] * inv_freq
    ca, sa, cb, sb = jnp.cos(ang_a), jnp.sin(ang_a), jnp.cos(ang_b), jnp.sin(ang_b)
    cos = (ca * cb - sa * sb).reshape(T, half)
    sin = (sa * cb + ca * sb).reshape(T, half)
    return jnp.concatenate([cos, cos], axis=1), jnp.concatenate([-sin, sin], axis=1)


def _mixers(x2d, g_mix, w_in, lb, hgrn_norm_g):
    T = x2d.shape[0]
    W = HGRN_WIDTH
    w = w_in
    cos, sin = _rope_tables(T)
    log_lb = jnp.log(lb).reshape(1, W)
    log_1m = jnp.log1p(-lb).reshape(1, W)
    hq, h = _norm_proj(x2d, g_mix, w, 0 * W, W, _ep_silu, BF16)
    logf = _proj(h, w, 1 * W, W, _ep_logf, F32, col_extras=(log_lb, log_1m))
    hi = _proj(h, w, 2 * W, W, _ep_identity, BF16)
    hog = _proj(h, w, 3 * W, W, _ep_silu, BF16)
    scale = MOBA_HEAD_DIM ** -0.5
    mqt = _proj_t(h, w, 4 * W, functools.partial(_ep_rope, scale=scale), row_extras=(cos, sin))
    blk = jnp.arange(T, dtype=I32)[:, None] // MOBA_BLOCK
    blk_onehot = (blk == jnp.arange(V7X_LANES, dtype=I32)[None, :]).astype(F32)
    mk = _proj(h, w, 5 * W, W, _ep_rope_aug, BF16, row_extras=(cos, sin, blk_onehot), widen=2)
    mvt = _proj_t(h, w, 6 * W, _ep_identity, ones_rows=MOBA_VT_ROWS - MOBA_HEAD_DIM)
    gates = _proj(h, w, 7 * W, 2 * D_MODEL, _ep_sigmoid, BF16)
    o_hgrn = _hgrn(hq, logf, hi, hog, hgrn_norm_g)
    o_moba = _moba(mqt, mk, mvt)
    return o_hgrn, o_moba, gates


def _moe_plan(info, cnt, T):
    R = MOE_ROWS
    eid = info[:, 0:TOPK_IN_GROUP].astype(I32)
    rank = info[:, 4:4 + TOPK_IN_GROUP].astype(I32)
    counts = cnt[0, :N_EXPERTS].astype(I32)
    padded = (counts + R - 1) // R * R
    pad_end = jnp.cumsum(padded)
    pad_start = pad_end - padded
    codes = (eid * (1 << RANK_BITS) + rank).reshape(-1)
    n_blocks = (T * TOPK_IN_GROUP) // R + N_EXPERTS
    block_expert = jnp.minimum(
        jnp.searchsorted(pad_end, jnp.arange(n_blocks, dtype=I32) * R, side="right"), N_EXPERTS - 1).astype(I32)
    has = counts > 0
    segment_expert = jnp.argsort(jnp.logical_not(has), stable=True).astype(I32)
    block_segment = (jnp.cumsum(has.astype(I32))[block_expert] - 1).astype(I32)
    n_used = jnp.stack([pad_end[-1] // R, jnp.sum(has.astype(I32))]).astype(I32)
    return codes, pad_start.astype(I32), pad_end.astype(I32), block_segment, segment_expert, n_used, n_blocks * R


def kernel(x, p, norm_mix_g, w_in, hgrn_lb_raw, hgrn_norm_g, w_up_hgrn, w_up_moba, w_out, norm_ffn_g,
           w_router_group, w_router_expert, w_exp_gate, w_exp_up, w_exp_down, norm_ple_g, w_ple_gate,
           w_ple_proj, norm_final_g):
    B, T, D = x.shape
    assert B == 1 and D == D_MODEL and w_in.shape[0] == 1 and T % (4 * MOBA_BLOCK) == 0
    lower_bounds = jnp.cumsum(jax.nn.softmax(hgrn_lb_raw.astype(F32), axis=0), axis=0)
    x2d = x.reshape(T, D)
    o_hgrn, o_moba, gates = _mixers(x2d, norm_mix_g[0], w_in[0], lower_bounds[0], hgrn_norm_g[0])
    merged = _merge(o_hgrn, o_moba, gates, w_up_hgrn[0].astype(BF16), w_up_moba[0].astype(BF16))
    w_router = jnp.pad(jnp.concatenate([w_router_group[0], w_router_expert[0]], axis=1),
                       ((0, 0), (0, V7X_LANES - N_GROUPS - N_EXPERTS))).astype(BF16)
    x1, h2p, info, cnt = _outproj(merged, x2d, w_out[0].astype(BF16), norm_ffn_g[0], w_router)
    codes, pad_start, pad_end, block_segment, segment_expert, n_used, m_pad = _moe_plan(info, cnt, T)
    xs = _dispatch(codes, pad_start, pad_end, h2p, m_pad)
    yb = _experts(block_segment, segment_expert, n_used, xs, w_exp_gate[0], w_exp_up[0], w_exp_down[0])
    out = _combine(codes, pad_start, x1, info, yb, p[0].reshape(T, PLE_DIM), norm_ple_g[0],
                   w_ple_gate[0].astype(BF16), w_ple_proj[0].astype(BF16), norm_final_g)
    return out.reshape(B, T, D)
```

```python
import functools

import jax
import jax.numpy as jnp
from jax import lax
from jax.experimental import pallas as pl
from jax.experimental.pallas import tpu as pltpu

F32 = jnp.float32
BF16 = jnp.bfloat16
I32 = jnp.int32

D_MODEL = 2048
PLE_DIM = 256
HGRN_HEADS = 8
HGRN_HEAD_DIM = 128
HGRN_WIDTH = HGRN_HEADS * HGRN_HEAD_DIM
MOBA_HEADS = 8
MOBA_HEAD_DIM = 128
MOBA_WIDTH = MOBA_HEADS * MOBA_HEAD_DIM
MOBA_BLOCK = 256
MOBA_TOPK = 3
ROPE_THETA = 10000.0
N_GROUPS = 4
EXPERTS_PER_GROUP = 8
N_EXPERTS = N_GROUPS * EXPERTS_PER_GROUP
TOPK_IN_GROUP = 2
D_EXPERT = 512
EPS = 1e-6
NEG_INF = -1e30

V7X_LANES = 128
V7X_SUBLANES = 8
V7X_VMEM_BUDGET_BYTES = 56 * 1024 * 1024

HGRN_CHUNK = 128
MOE_ROWS = 256


def _cparams(n_grid, vmem_bytes):
    return pltpu.CompilerParams(
        dimension_semantics=("arbitrary",) * n_grid,
        vmem_limit_bytes=int(min(max(vmem_bytes, 16 * 1024 * 1024), V7X_VMEM_BUDGET_BYTES)),
    )


def _nbytes(shape, dtype):
    n = 1
    for s in shape:
        n *= s
    return n * jnp.dtype(dtype).itemsize


def _rms(x, g):
    ms = jnp.mean(x * x, axis=-1, keepdims=True)
    return x * lax.rsqrt(ms + EPS) * g


def _ep_identity(acc):
    return acc


def _ep_silu(acc):
    return acc * jax.nn.sigmoid(acc)


def _ep_sigmoid(acc):
    return jax.nn.sigmoid(acc)


def _ep_logf(acc, la_ref, lc_ref):
    ls = jnp.minimum(acc, 0.0) - jnp.log(1.0 + jnp.exp(-jnp.abs(acc)))
    u = la_ref[...]
    v = lc_ref[...] + ls
    return jnp.maximum(u, v) + jnp.log(1.0 + jnp.exp(-jnp.abs(u - v)))


def _ep_rope(acc, cos_ref, sin_ref, *, scale):
    cos = cos_ref[...]
    sin = sin_ref[...]
    outs = []
    for hh in range(acc.shape[1] // MOBA_HEAD_DIM):
        a = acc[:, hh * MOBA_HEAD_DIM:(hh + 1) * MOBA_HEAD_DIM]
        r = pltpu.roll(a, MOBA_HEAD_DIM // 2, axis=1)
        outs.append((a * cos + r * sin) * scale)
    return jnp.concatenate(outs, axis=1)


def _ep_rope_aug(acc, cos_ref, sin_ref, oh_ref):
    cos = cos_ref[...]
    sin = sin_ref[...]
    oh = oh_ref[...]
    outs = []
    for hh in range(acc.shape[1] // MOBA_HEAD_DIM):
        a = acc[:, hh * MOBA_HEAD_DIM:(hh + 1) * MOBA_HEAD_DIM]
        outs.append(a * cos + pltpu.roll(a, MOBA_HEAD_DIM // 2, axis=1) * sin)
        outs.append(oh)
    return jnp.concatenate(outs, axis=1)


def _cast_weight_once(w_ref, wb_ref, row_axis):
    @pl.when(pl.program_id(row_axis) == 0)
    def _():
        wb_ref[...] = w_ref[...].astype(wb_ref.dtype)


def _proj_kernel(h_ref, w_ref, *refs, epilogue):
    *extra, o_ref, wb_ref = refs
    _cast_weight_once(w_ref, wb_ref, 1)
    acc = jnp.dot(h_ref[...], wb_ref[...], preferred_element_type=F32)
    o_ref[...] = epilogue(acc, *extra).astype(o_ref.dtype)


def _proj(h, w, col0, ncols, epilogue, out_dtype, row_extras=(), col_extras=(), tm=1024, tn=1024, widen=1):
    T, K = h.shape
    tn = min(tn, ncols)
    tm = min(tm, T)
    cb = col0 // tn
    otn = widen * tn
    in_specs = [
        pl.BlockSpec((tm, K), lambda j, i: (i, 0)),
        pl.BlockSpec((K, tn), lambda j, i: (0, cb + j)),
    ]
    for e in row_extras:
        in_specs.append(pl.BlockSpec((tm, e.shape[1]), lambda j, i: (i, 0)))
    for e in col_extras:
        in_specs.append(pl.BlockSpec((1, tn), lambda j, i: (0, j)))
    vmem = 2 * (_nbytes((tm, K), h.dtype) + _nbytes((K, tn), w.dtype) + _nbytes((tm, otn), out_dtype))
    vmem += 3 * _nbytes((tm, otn), F32) + _nbytes((K, tn), BF16)
    return pl.pallas_call(
        functools.partial(_proj_kernel, epilogue=epilogue),
        out_shape=jax.ShapeDtypeStruct((T, widen * ncols), out_dtype),
        grid=(ncols // tn, T // tm),
        in_specs=in_specs,
        out_specs=pl.BlockSpec((tm, otn), lambda j, i: (i, j)),
        scratch_shapes=[pltpu.VMEM((K, tn), BF16)],
        compiler_params=_cparams(2, vmem),
        name="proj",
    )(h, w, *row_extras, *col_extras)


def _norm_proj_kernel(x_ref, g_ref, w_ref, o_ref, h_ref, wb_ref, *, epilogue):
    _cast_weight_once(w_ref, wb_ref, 0)
    h = _rms(x_ref[...], g_ref[...]).astype(h_ref.dtype)
    h_ref[...] = h
    o_ref[...] = epilogue(jnp.dot(h, wb_ref[...], preferred_element_type=F32)).astype(o_ref.dtype)


def _norm_proj(x, g, w, col0, ncols, epilogue, out_dtype, tm=512):
    T, K = x.shape
    cb = col0 // ncols
    vmem = 2 * (_nbytes((tm, K), F32) + _nbytes((tm, ncols), out_dtype) + _nbytes((tm, K), BF16))
    vmem += _nbytes((K, ncols), F32) + _nbytes((K, ncols), BF16) + 2 * _nbytes((tm, K), F32) + 3 * _nbytes((tm, ncols), F32)
    return pl.pallas_call(
        functools.partial(_norm_proj_kernel, epilogue=epilogue),
        out_shape=(jax.ShapeDtypeStruct((T, ncols), out_dtype), jax.ShapeDtypeStruct((T, K), BF16)),
        grid=(T // tm,),
        in_specs=[
            pl.BlockSpec((tm, K), lambda i: (i, 0)),
            pl.BlockSpec((1, K), lambda i: (0, 0)),
            pl.BlockSpec((K, ncols), lambda i: (0, cb), pipeline_mode=pl.Buffered(1)),
        ],
        out_specs=(pl.BlockSpec((tm, ncols), lambda i: (i, 0)), pl.BlockSpec((tm, K), lambda i: (i, 0))),
        scratch_shapes=[pltpu.VMEM((K, ncols), BF16)],
        compiler_params=_cparams(1, vmem),
        name="norm_proj",
    )(x, g.reshape(1, K), w)


MOBA_VT_ROWS = MOBA_HEAD_DIM + 16
MOBA_BLOCKS_PER_STEP = 4


def _proj_t_kernel(h_ref, w_ref, *refs, epilogue, ones_rows):
    *extra, o_ref, wb_ref = refs
    BS, HD = MOBA_BLOCK, MOBA_HEAD_DIM
    _cast_weight_once(w_ref, wb_ref, 0)
    acc = epilogue(jnp.dot(h_ref[...], wb_ref[...], preferred_element_type=F32), *extra)
    ones = jnp.ones((ones_rows, BS), F32) if ones_rows else None
    for b in range(acc.shape[0] // BS):
        parts = []
        for hh in range(acc.shape[1] // HD):
            parts.append(acc[b * BS:(b + 1) * BS, hh * HD:(hh + 1) * HD].T)
            if ones_rows:
                parts.append(ones)
        o_ref[b] = jnp.concatenate(parts, axis=0).astype(o_ref.dtype)


def _proj_t(h, w, col0, epilogue, row_extras=(), ones_rows=0, tm=1024):
    T, K = h.shape
    tn = MOBA_WIDTH
    tm = min(tm, T)
    cb = col0 // tn
    rows = MOBA_HEADS * (MOBA_HEAD_DIM + ones_rows)
    in_specs = [pl.BlockSpec((tm, K), lambda i: (i, 0)), pl.BlockSpec((K, tn), lambda i: (0, cb))]
    for e in row_extras:
        in_specs.append(pl.BlockSpec((tm, e.shape[1]), lambda i: (i, 0)))
    vmem = 2 * (_nbytes((tm, K), h.dtype) + _nbytes((K, tn), w.dtype) + _nbytes((tm, 2 * tn), BF16))
    vmem += 4 * _nbytes((tm, tn), F32) + _nbytes((K, tn), BF16)
    return pl.pallas_call(
        functools.partial(_proj_t_kernel, epilogue=epilogue, ones_rows=ones_rows),
        out_shape=jax.ShapeDtypeStruct((T // MOBA_BLOCK, rows, MOBA_BLOCK), BF16),
        grid=(T // tm,),
        in_specs=in_specs,
        out_specs=pl.BlockSpec((tm // MOBA_BLOCK, rows, MOBA_BLOCK), lambda i: (i, 0, 0)),
        scratch_shapes=[pltpu.VMEM((K, tn), BF16)],
        compiler_params=_cparams(1, vmem),
        name="proj_t",
    )(h, w, *row_extras)


def _hgrn_kernel(q_ref, g_ref, v_ref, og_ref, ng_ref, o_ref, st_ref, code_ref):
    W, HD, NH, C, S = HGRN_WIDTH, HGRN_HEAD_DIM, HGRN_HEADS, HGRN_CHUNK, V7X_SUBLANES
    J = C // S

    @pl.when(pl.program_id(0) == 0)
    def _():
        st_ref[...] = jnp.zeros_like(st_ref)
        tr = lax.broadcasted_iota(I32, (C, C), 0)
        tc = lax.broadcasted_iota(I32, (C, C), 1)
        xr = tr ^ tc
        code = jnp.zeros((C, C), I32)
        for lvl in range(1, 8):
            code = jnp.where(xr >= (1 << (lvl - 1)), lvl, code)
        code_ref[...] = jnp.where(tc > tr, -1, code)

    def r3(x):
        return x.astype(F32).reshape(J, S, W)

    def sub_bcast(x3, r):
        return jnp.broadcast_to(x3[:, r:r + 1, :], x3.shape)

    g3, q3, v3 = r3(g_ref[...]), r3(q_ref[...]), r3(v_ref[...])
    sub = lax.broadcasted_iota(I32, (1, S, W), 1)

    c3 = g3
    for s in (1, 2, 4):
        c3 = c3 + jnp.where(sub >= s, pltpu.roll(c3, s, axis=1), 0.0)
    run = jnp.zeros((1, 1, W), F32)
    carry = []
    for j in range(J):
        carry.append(run)
        run = run + c3[j:j + 1, S - 1:S, :]
    b3 = c3 + jnp.concatenate(carry, axis=0)
    bC = run

    k3 = 1.0 - jnp.exp(g3)
    qe3 = q3 * jnp.exp(b3)
    ks3 = k3 * jnp.exp(bC - b3)

    levels = [(0, q3, k3)]
    ref1 = jnp.where(sub % 2 == 0, b3, pltpu.roll(b3, 1, axis=1))
    ref2 = jnp.where(sub < 4, sub_bcast(b3, 1), sub_bcast(b3, 5))
    ref4 = sub_bcast(b3, 3)
    for lvl, (ref, upper) in enumerate(((ref1, sub % 2 == 1), (ref2, sub % 4 >= 2), (ref4, sub >= 4)), start=1):
        e = jnp.exp(-jnp.abs(b3 - ref))
        levels.append((lvl, jnp.where(upper, q3 * e, 0.0), jnp.where(upper, 0.0, k3 * e)))
    zero_group = jnp.zeros((1, S, W), F32)
    for lvl, half in enumerate((1, 2, 4, 8), start=4):
        qparts, kparts = [], []
        for j in range(J):
            jr = (j // (2 * half)) * (2 * half) + half - 1
            ref = b3[jr:jr + 1, S - 1:S, :]
            if (j % (2 * half)) >= half:
                qparts.append(q3[j:j + 1] * jnp.exp(b3[j:j + 1] - ref))
                kparts.append(zero_group)
            else:
                qparts.append(zero_group)
                kparts.append(k3[j:j + 1] * jnp.exp(ref - b3[j:j + 1]))
        levels.append((lvl, jnp.concatenate(qparts, axis=0), jnp.concatenate(kparts, axis=0)))

    code = code_ref[...]

    def mat(x3, h):
        return x3.reshape(C, W)[:, h * HD:(h + 1) * HD].astype(BF16)

    nt = (((1,), (1,)), ((), ()))
    tn = (((0,), (0,)), ((), ()))
    ebc = jnp.exp(bC).reshape(1, W)
    ng = ng_ref[...]
    for h in range(NH):
        a_mat = jnp.zeros((C, C), F32)
        for lvl, qr, kr in levels:
            s = lax.dot_general(mat(qr, h), mat(kr, h), nt, preferred_element_type=F32)
            a_mat = jnp.where(code == lvl, s, a_mat)
        vh = mat(v3, h)
        st = st_ref[h]
        o = jnp.dot(a_mat.astype(BF16), vh, preferred_element_type=F32)
        o = o + lax.dot_general(mat(qe3, h), st.astype(BF16), nt, preferred_element_type=F32)
        o = _rms(o, ng) * og_ref[:, h * HD:(h + 1) * HD].astype(F32)
        o_ref[:, h * HD:(h + 1) * HD] = o.astype(o_ref.dtype)
        st_ref[h] = st * ebc[:, h * HD:(h + 1) * HD] + lax.dot_general(
            vh, mat(ks3, h), tn, preferred_element_type=F32)


def _hgrn(q, logf, v, og, norm_g):
    T, W = q.shape
    C = HGRN_CHUNK
    blk = pl.BlockSpec((C, W), lambda c: (c, 0))
    vmem = 64 * _nbytes((C, W), F32)
    return pl.pallas_call(
        _hgrn_kernel,
        out_shape=jax.ShapeDtypeStruct((T, W), BF16),
        grid=(T // C,),
        in_specs=[blk, blk, blk, blk, pl.BlockSpec((1, HGRN_HEAD_DIM), lambda c: (0, 0))],
        out_specs=blk,
        scratch_shapes=[pltpu.VMEM((HGRN_HEADS, HGRN_HEAD_DIM, HGRN_HEAD_DIM), F32), pltpu.VMEM((C, C), I32)],
        compiler_params=_cparams(1, vmem),
        name="hgrn",
    )(q, logf, v, og, norm_g.reshape(1, HGRN_HEAD_DIM))


def _moba_kernel(qt_ref, k_ref, vt_ref, o_ref, km_ref):
    BS, HD, VR = MOBA_BLOCK, MOBA_HEAD_DIM, MOBA_VT_ROWS
    T = k_ref.shape[0]
    NB = T // BS
    G = o_ref.shape[1] // HD
    cur = pl.program_id(1)

    @pl.when(cur == 0)
    def _():
        for g in range(G):
            kf = k_ref[:, 2 * g * HD:(2 * g + 1) * HD].astype(F32).reshape(NB, BS, HD)
            km_ref[g] = jnp.sum(kf, axis=1) * (1.0 / BS)

    blk = lax.broadcasted_iota(I32, (NB, BS), 0)
    pad = jnp.zeros((V7X_LANES - NB, BS), F32)
    qts = [qt_ref[0, g * HD:(g + 1) * HD, :] for g in range(G)]
    gts = [jnp.dot(km_ref[g].astype(BF16), qts[g], preferred_element_type=F32) for g in range(G)]
    qcs = []
    for g in range(G):
        gt = jnp.where(blk < cur, gts[g], NEG_INF)
        sel = blk == cur
        for _ in range(MOBA_TOPK):
            mx = jnp.max(gt, axis=0, keepdims=True)
            idx = jnp.min(jnp.where(gt == mx, blk, NB), axis=0, keepdims=True)
            pick = (blk == idx) & (mx > 0.5 * NEG_INF)
            sel = sel | pick
            gt = jnp.where(pick, NEG_INF, gt)
        pen = jnp.concatenate([jnp.where(sel, 0.0, NEG_INF), pad], axis=0).astype(BF16)
        qcs.append(jnp.concatenate([qts[g], pen], axis=0))

    KB = MOBA_BLOCKS_PER_STEP

    def body(c, carry, own_step=False):
        ms, accs = carry
        r = pl.multiple_of(c * KB * BS, KB * BS)
        sns = [jnp.dot(k_ref[pl.ds(r, KB * BS), 2 * g * HD:(2 * g + 2) * HD], qcs[g], preferred_element_type=F32)
               for g in range(G)]
        if own_step:
            krow = lax.broadcasted_iota(I32, (KB * BS, BS), 0)
            qcol = lax.broadcasted_iota(I32, (KB * BS, BS), 1)
            keep = (c * KB + krow // BS != cur) | (krow % BS <= qcol)
            sns = [jnp.where(keep, s, NEG_INF) for s in sns]
        new_ms, alphas, pns = [], [], []
        for g in range(G):
            m_new = jnp.maximum(ms[g], jnp.max(sns[g], axis=0, keepdims=True))
            alphas.append(jnp.exp(ms[g] - m_new))
            pns.append(jnp.exp(sns[g] - m_new).astype(BF16))
            new_ms.append(m_new)
        new_accs = []
        for g in range(G):
            pv = alphas[g] * accs[g]
            for j in range(KB):
                pv = pv + jnp.dot(vt_ref[KB * c + j, g * VR:(g + 1) * VR, :], pns[g][j * BS:(j + 1) * BS],
                                  preferred_element_type=F32)
            new_accs.append(pv)
        return tuple(new_ms), tuple(new_accs)

    ms = tuple(jnp.full((1, BS), NEG_INF, F32) for _ in range(G))
    accs = tuple(jnp.zeros((VR, BS), F32) for _ in range(G))
    carry = lax.fori_loop(0, cur // KB, body, (ms, accs))
    _, accs = body(cur // KB, carry, own_step=True)
    for g in range(G):
        ot = accs[g][:HD, :] / accs[g][HD:HD + 1, :]
        o_ref[:, g * HD:(g + 1) * HD] = ot.T.astype(o_ref.dtype)


def _moba(mqt, mk_aug, mvt, heads_per_step=4):
    T = mk_aug.shape[0]
    BS, HD, G, VR = MOBA_BLOCK, MOBA_HEAD_DIM, heads_per_step, MOBA_VT_ROWS
    NB = T // BS
    vmem = _nbytes((T, 2 * G * HD), BF16) + _nbytes((NB, G * VR, BS), BF16) + 8 * _nbytes((BS, G * HD), BF16)
    vmem += 16 * G * _nbytes((BS, BS), F32)
    resident = pl.Buffered(1)
    return pl.pallas_call(
        _moba_kernel,
        out_shape=jax.ShapeDtypeStruct((T, MOBA_WIDTH), BF16),
        grid=(MOBA_HEADS // G, NB),
        in_specs=[
            pl.BlockSpec((1, G * HD, BS), lambda h, i: (i, h, 0)),
            pl.BlockSpec((T, 2 * G * HD), lambda h, i: (0, h), pipeline_mode=resident),
            pl.BlockSpec((NB, G * VR, BS), lambda h, i: (0, h, 0), pipeline_mode=resident),
        ],
        out_specs=pl.BlockSpec((BS, G * HD), lambda h, i: (i, h)),
        scratch_shapes=[pltpu.VMEM((G, NB, HD), F32)],
        compiler_params=_cparams(2, vmem),
        name="moba",
    )(mqt, mk_aug, mvt)


def _merge_kernel(oh_ref, om_ref, ga_ref, gb_ref, wh_ref, wm_ref, o_ref):
    a = jnp.dot(oh_ref[...], wh_ref[...], preferred_element_type=F32)
    b = jnp.dot(om_ref[...], wm_ref[...], preferred_element_type=F32)
    o_ref[...] = (ga_ref[...].astype(F32) * a + gb_ref[...].astype(F32) * b).astype(o_ref.dtype)


def _merge(o_hgrn, o_moba, gates, w_up_hgrn, w_up_moba, tm=512):
    T = o_hgrn.shape[0]
    D = D_MODEL
    vmem = 2 * (_nbytes((tm, HGRN_WIDTH), F32) + _nbytes((tm, MOBA_WIDTH), BF16) + 3 * _nbytes((tm, D), BF16)
                + 2 * _nbytes((HGRN_WIDTH, D), BF16)) + 3 * _nbytes((tm, D), F32)
    return pl.pallas_call(
        _merge_kernel,
        out_shape=jax.ShapeDtypeStruct((T, D), BF16),
        grid=(T // tm,),
        in_specs=[
            pl.BlockSpec((tm, HGRN_WIDTH), lambda i: (i, 0)),
            pl.BlockSpec((tm, MOBA_WIDTH), lambda i: (i, 0)),
            pl.BlockSpec((tm, D), lambda i: (i, 0)),
            pl.BlockSpec((tm, D), lambda i: (i, 1)),
            pl.BlockSpec((HGRN_WIDTH, D), lambda i: (0, 0)),
            pl.BlockSpec((MOBA_WIDTH, D), lambda i: (0, 0)),
        ],
        out_specs=pl.BlockSpec((tm, D), lambda i: (i, 0)),
        compiler_params=_cparams(1, vmem),
        name="merge",
    )(o_hgrn, o_moba, gates, gates, w_up_hgrn, w_up_moba)


def _pack_bf16_pairs(x):
    C = x.shape[1] // 2
    b = lax.bitcast_convert_type(x, jnp.uint32)
    r = (b + jnp.uint32(0x7FFF) + ((b >> 16) & jnp.uint32(1))) >> 16
    return r[:, :C] | (r[:, C:] << 16)


def _unpack_bf16_pairs(p):
    lo = lax.bitcast_convert_type(p << 16, F32)
    hi = lax.bitcast_convert_type(p & jnp.uint32(0xFFFF0000), F32)
    return jnp.concatenate([lo, hi], axis=1)


def _outproj_kernel(m_ref, x_ref, w_ref, g_ref, wr_ref, x1_ref, h2p_ref, info_ref, cnt_ref, prev_ref, carry_ref):
    step = pl.program_id(0)
    n_tiles = pl.num_programs(0) - 1

    @pl.when(step == 0)
    def _():
        carry_ref[...] = jnp.zeros_like(carry_ref)

    def finish_previous():
        h2 = _rms(prev_ref[...], g_ref[...])
        h2p_ref[...] = _pack_bf16_pairs(h2)
        info_ref[...] = _route(h2, wr_ref, carry_ref)
        cnt_ref[...] = carry_ref[...]

    def project():
        return x_ref[...] + jnp.dot(m_ref[...], w_ref[...], preferred_element_type=F32)

    @pl.when(step == 0)
    def _():
        x1 = project()
        x1_ref[...] = x1
        prev_ref[...] = x1

    @pl.when((step > 0) & (step < n_tiles))
    def _():
        x1 = project()
        finish_previous()
        x1_ref[...] = x1
        prev_ref[...] = x1

    @pl.when(step == n_tiles)
    def _():
        finish_previous()


def _outproj(merged, x, w_out, g_ffn, w_router, tm=512):
    T, D = x.shape
    n_tiles = T // tm
    vmem = 2 * (_nbytes((tm, D), BF16) + 3 * _nbytes((tm, D), F32) + _nbytes((D, D), BF16)) + 7 * _nbytes((tm, D), F32)
    cur = lambda i: (jnp.minimum(i, n_tiles - 1), 0)
    prev = lambda i: (jnp.maximum(i - 1, 0), 0)
    fixed = lambda i: (0, 0)
    return pl.pallas_call(
        _outproj_kernel,
        out_shape=(jax.ShapeDtypeStruct((T, D), F32), jax.ShapeDtypeStruct((T, D // 2), jnp.uint32),
                   jax.ShapeDtypeStruct((T, V7X_LANES), F32), jax.ShapeDtypeStruct((1, V7X_LANES), F32)),
        grid=(n_tiles + 1,),
        in_specs=[
            pl.BlockSpec((tm, D), cur),
            pl.BlockSpec((tm, D), cur),
            pl.BlockSpec((D, D), fixed, pipeline_mode=pl.Buffered(1)),
            pl.BlockSpec((1, D), fixed),
            pl.BlockSpec((D, V7X_LANES), fixed),
        ],
        out_specs=(pl.BlockSpec((tm, D), cur), pl.BlockSpec((tm, D // 2), prev), pl.BlockSpec((tm, V7X_LANES), prev),
                   pl.BlockSpec((1, V7X_LANES), fixed)),
        scratch_shapes=[pltpu.VMEM((tm, D), F32), pltpu.VMEM((1, V7X_LANES), F32)],
        compiler_params=_cparams(1, vmem),
        name="outproj",
    )(merged, x, w_out, g_ffn.reshape(1, D), w_router)


def _route(h2, w_ref, carry_ref):
    tm = h2.shape[0]
    logits = jnp.dot(h2.astype(BF16), w_ref[...], preferred_element_type=F32)
    lane = lax.broadcasted_iota(I32, (tm, V7X_LANES), 1)
    is_g = lane < N_GROUPS
    gl = jnp.where(is_g, logits, NEG_INF)
    gmax = jnp.max(gl, axis=1, keepdims=True)
    g_sel = jnp.min(jnp.where(gl == gmax, lane, V7X_LANES), axis=1, keepdims=True)
    gsum = jnp.sum(jnp.where(is_g, jnp.exp(gl - gmax), 0.0), axis=1, keepdims=True)
    p_group = 1.0 / gsum
    lo = N_GROUPS + EXPERTS_PER_GROUP * g_sel
    emask = (lane >= lo) & (lane < lo + EXPERTS_PER_GROUP)
    el = jnp.where(emask, logits, NEG_INF)
    e1 = jnp.max(el, axis=1, keepdims=True)
    i1 = jnp.min(jnp.where((el == e1) & emask, lane, V7X_LANES), axis=1, keepdims=True)
    emask2 = emask & (lane != i1)
    el2 = jnp.where(emask2, logits, NEG_INF)
    e2 = jnp.max(el2, axis=1, keepdims=True)
    i2 = jnp.min(jnp.where((el2 == e2) & emask2, lane, V7X_LANES), axis=1, keepdims=True)
    r = jnp.exp(e2 - e1)
    w1 = p_group / (1.0 + r)
    w2 = p_group * r / (1.0 + r)
    eid1 = i1 - N_GROUPS
    eid2 = i2 - N_GROUPS
    oh1 = jnp.where(lane == eid1, 1.0, 0.0)
    oh2 = jnp.where(lane == eid2, 1.0, 0.0)
    cnt = oh1 + oh2
    tri = jnp.where(lax.broadcasted_iota(I32, (tm, tm), 0) > lax.broadcasted_iota(I32, (tm, tm), 1), 1.0, 0.0)
    before = jnp.dot(tri.astype(BF16), cnt.astype(BF16), preferred_element_type=F32) + carry_ref[...]
    rank1 = jnp.sum(oh1 * before, axis=1, keepdims=True)
    rank2 = jnp.sum(oh2 * before, axis=1, keepdims=True)
    carry_ref[...] = carry_ref[...] + jnp.sum(cnt, axis=0, keepdims=True)
    info = jnp.zeros((tm, V7X_LANES), F32)
    for k, val in enumerate((eid1.astype(F32), eid2.astype(F32), w1, w2, rank1, rank2)):
        info = jnp.where(lane == k, val, info)
    return info


def _row_copy(src_ref, src_row, dst_ref, dst_row, sem):
    return pltpu.make_async_copy(src_ref.at[pl.ds(src_row, 1), :], dst_ref.at[pl.ds(dst_row, 1), :], sem)


ROW_DMA_UNROLL = 8
RANK_BITS = 16


def _slot(code, pstart_ref):
    return pstart_ref[lax.shift_right_logical(code, RANK_BITS)] + (code & ((1 << RANK_BITS) - 1))


def _dispatch_kernel(code_ref, pstart_ref, pend_ref, h_ref, xs_ref, zero_ref, sem):
    tm = h_ref.shape[0]
    step = pl.program_id(0)
    base = step * tm

    @pl.when(step == 0)
    def _():
        zero_ref[...] = jnp.zeros_like(zero_ref)

        def tail(e):
            return pltpu.make_async_copy(
                zero_ref, xs_ref.at[pl.ds(pl.multiple_of(pend_ref[e] - MOE_ROWS, MOE_ROWS), MOE_ROWS), :], sem)

        def nonempty(e):
            return pend_ref[e] > (pend_ref[e - 1] if e else 0)

        def unused(b):
            return pltpu.make_async_copy(
                zero_ref, xs_ref.at[pl.ds(pl.multiple_of(b * MOE_ROWS, MOE_ROWS), MOE_ROWS), :], sem)

        first_unused = pend_ref[N_EXPERTS - 1] // MOE_ROWS
        n_blocks = xs_ref.shape[0] // MOE_ROWS
        for e in range(N_EXPERTS):
            pl.when(nonempty(e))(lambda e=e: tail(e).start())
        lax.fori_loop(first_unused, n_blocks, lambda b, c: (unused(b).start(), c)[1], 0)
        for e in range(N_EXPERTS):
            pl.when(nonempty(e))(lambda e=e: tail(e).wait())
        lax.fori_loop(first_unused, n_blocks, lambda b, c: (unused(b).wait(), c)[1], 0)

    def issue(rb, c):
        for u in range(ROW_DMA_UNROLL):
            r = rb * ROW_DMA_UNROLL + u
            for k in range(TOPK_IN_GROUP):
                slot = _slot(code_ref[(base + r) * TOPK_IN_GROUP + k], pstart_ref)
                _row_copy(h_ref, r, xs_ref, slot, sem).start(priority=k % 2)
        return c

    lax.fori_loop(0, tm // ROW_DMA_UNROLL, issue, 0)
    for _ in range(tm * TOPK_IN_GROUP):
        _row_copy(h_ref, 0, xs_ref, 0, sem).wait()


def _dispatch(codes, pad_start, pad_end, h2p, m_pad, tm=256):
    T, C = h2p.shape
    return pl.pallas_call(
        _dispatch_kernel,
        out_shape=jax.ShapeDtypeStruct((m_pad, C), h2p.dtype),
        grid_spec=pltpu.PrefetchScalarGridSpec(
            num_scalar_prefetch=3,
            grid=(T // tm,),
            in_specs=[pl.BlockSpec((tm, C), lambda i, c, ps, pe: (i, 0))],
            out_specs=pl.BlockSpec(memory_space=pl.ANY),
            scratch_shapes=[pltpu.VMEM((MOE_ROWS, C), h2p.dtype), pltpu.SemaphoreType.DMA(())],
        ),
        compiler_params=_cparams(1, 6 * _nbytes((tm, C), h2p.dtype)),
        name="dispatch",
    )(codes, pad_start, pad_end, h2p)


def _expert_kernel(seg_ref, sexp_ref, nu_ref, xs_ref, wg_ref, wu_ref, wd_ref, y_ref, wgf_ref, wuf_ref, wdf_ref,
                   wgb_ref, wub_ref, wdb_ref, sem):
    b = pl.program_id(0)
    used = b < nu_ref[0]
    seg = seg_ref[b]
    first = used & ((b == 0) | (seg != seg_ref[jnp.maximum(b - 1, 0)]))

    def weights_copy(s, part):
        slot = s % 2
        src, dst = ((wg_ref, wgf_ref), (wu_ref, wuf_ref), (wd_ref, wdf_ref))[part]
        return pltpu.make_async_copy(src.at[sexp_ref[s]], dst.at[slot], sem.at[slot])

    @pl.when(b == 0)
    def _():
        for part in range(3):
            weights_copy(0, part).start()

    @pl.when(first)
    def _():
        for part in range(3):
            weights_copy(seg, part).wait()
        slot = seg % 2
        wgb_ref[...] = wgf_ref[slot].astype(BF16)
        wub_ref[...] = wuf_ref[slot].astype(BF16)
        wdb_ref[...] = wdf_ref[slot].astype(BF16)

    @pl.when(first & (seg + 1 < nu_ref[1]))
    def _():
        for part in range(3):
            weights_copy(seg + 1, part).start()

    @pl.when(used)
    def _():
        x = _unpack_bf16_pairs(xs_ref[...]).astype(BF16)
        a = jnp.dot(x, wgb_ref[...], preferred_element_type=F32)
        u = jnp.dot(x, wub_ref[...], preferred_element_type=F32)
        hm = (a * jax.nn.sigmoid(a) * u).astype(BF16)
        y_ref[...] = _pack_bf16_pairs(jnp.dot(hm, wdb_ref[...], preferred_element_type=F32))

    @pl.when(jnp.logical_not(used))
    def _():
        y_ref[...] = jnp.zeros_like(y_ref)


def _experts(block_segment, segment_expert, n_used, xs, w_gate, w_up, w_down):
    m_pad, C = xs.shape
    D = 2 * C
    R, Fd = MOE_ROWS, D_EXPERT
    vmem = 4 * _nbytes((R, C), xs.dtype) + 6 * _nbytes((D, Fd), F32) + 3 * _nbytes((D, Fd), BF16)
    vmem += 6 * _nbytes((R, D), F32)
    hbm = pl.BlockSpec(memory_space=pl.ANY)
    return pl.pallas_call(
        _expert_kernel,
        out_shape=jax.ShapeDtypeStruct((m_pad, C), xs.dtype),
        grid_spec=pltpu.PrefetchScalarGridSpec(
            num_scalar_prefetch=3,
            grid=(m_pad // R,),
            in_specs=[
                pl.BlockSpec((R, C), lambda b, sg, se, nu: (jnp.minimum(b, jnp.maximum(nu[0] - 1, 0)), 0)),
                hbm, hbm, hbm,
            ],
            out_specs=pl.BlockSpec((R, C), lambda b, sg, se, nu: (b, 0)),
            scratch_shapes=[pltpu.VMEM((2, D, Fd), F32), pltpu.VMEM((2, D, Fd), F32), pltpu.VMEM((2, Fd, D), F32),
                            pltpu.VMEM((D, Fd), BF16), pltpu.VMEM((D, Fd), BF16), pltpu.VMEM((Fd, D), BF16),
                            pltpu.SemaphoreType.DMA((2,))],
        ),
        compiler_params=_cparams(1, vmem),
        name="experts",
    )(block_segment, segment_expert, n_used, xs, w_gate, w_up, w_down)


def _combine_kernel(code_ref, pstart_ref, x1_ref, info_ref, yb_ref, p_ref, gp_ref, wpg_ref, wpp_ref, gf_ref, o_ref,
                    ybuf, sem):
    tm, D = x1_ref.shape
    step = pl.program_id(0)
    n_tiles = pl.num_programs(0) - 2
    K = TOPK_IN_GROUP
    GROUPS = 8

    def fetch(tile, group=None):
        slot = tile % 2
        rows = range(tm) if group is None else range(group * tm // GROUPS, (group + 1) * tm // GROUPS)
        for r in rows:
            for k in range(K):
                src = _slot(code_ref[(tile * tm + r) * K + k], pstart_ref)
                _row_copy(yb_ref, src, ybuf.at[slot * K + k], r, sem.at[slot]).start(priority=k % 2)

    def drain(tile):
        slot = tile % 2
        for _ in range(tm * K):
            _row_copy(yb_ref, 0, ybuf.at[0], 0, sem.at[slot]).wait()

    @pl.when(step == 0)
    def _():
        fetch(step)

    @pl.when(step > 0)
    def _():
        drain(step - 1)

    @pl.when((step > 0) & (step <= n_tiles))
    def _():
        tile = step - 1
        slot = tile % 2
        info = info_ref[...]
        x2 = (x1_ref[...] + info[:, 2:3] * _unpack_bf16_pairs(ybuf[slot * K])
              + info[:, 3:4] * _unpack_bf16_pairs(ybuf[slot * K + 1]))
        hp = _rms(x2, gp_ref[...]).astype(BF16)
        pp = jnp.dot(p_ref[...].astype(BF16), wpp_ref[...], preferred_element_type=F32)
        cw = D // GROUPS
        zs = []
        for c in range(GROUPS):
            fetch(step, c)
            zs.append(jnp.dot(hp, wpg_ref[:, c * cw:(c + 1) * cw], preferred_element_type=F32))
        x3 = x2 + jax.nn.sigmoid(jnp.concatenate(zs, axis=1)) * pp
        o_ref[...] = _rms(x3, gf_ref[...])


def _combine(codes, pad_start, x1, info, yb, p, g_ple, w_ple_gate, w_ple_proj, g_final, tm=256):
    T, D = x1.shape
    n_tiles = T // tm
    codes_padded = jnp.concatenate([codes, jnp.zeros((tm * TOPK_IN_GROUP,), I32)])
    vmem = 4 * _nbytes((tm, D), F32) + _nbytes((D, D), BF16) + _nbytes((PLE_DIM, D), BF16)
    vmem += 2 * _nbytes((tm, PLE_DIM), F32) + 2 * TOPK_IN_GROUP * _nbytes((tm, D // 2), yb.dtype)
    vmem += 6 * _nbytes((tm, D), F32)
    tile = lambda i, c, ps: (jnp.clip(i - 1, 0, n_tiles - 1), 0)
    fixed = lambda i, c, ps: (0, 0)
    resident = pl.Buffered(1)
    return pl.pallas_call(
        _combine_kernel,
        out_shape=jax.ShapeDtypeStruct((T, D), F32),
        grid_spec=pltpu.PrefetchScalarGridSpec(
            num_scalar_prefetch=2,
            grid=(n_tiles + 2,),
            in_specs=[
                pl.BlockSpec((tm, D), tile),
                pl.BlockSpec((tm, V7X_LANES), tile),
                pl.BlockSpec(memory_space=pl.ANY),
                pl.BlockSpec((tm, PLE_DIM), tile),
                pl.BlockSpec((1, D), fixed),
                pl.BlockSpec((D, D), fixed, pipeline_mode=resident),
                pl.BlockSpec((PLE_DIM, D), fixed, pipeline_mode=resident),
                pl.BlockSpec((1, D), fixed),
            ],
            out_specs=pl.BlockSpec((tm, D), tile),
            scratch_shapes=[pltpu.VMEM((2 * TOPK_IN_GROUP, tm, yb.shape[1]), yb.dtype),
                            pltpu.SemaphoreType.DMA((2,))],
        ),
        compiler_params=_cparams(1, vmem),
        name="combine",
    )(codes_padded, pad_start, x1, info, yb, p, g_ple.reshape(1, D), w_ple_gate, w_ple_proj, g_final.reshape(1, D))


def _rope_tables(T):
    half, BS = MOBA_HEAD_DIM // 2, MOBA_BLOCK
    inv_freq = ROPE_THETA ** (-jnp.arange(half, dtype=F32) / half)
    ang_a = (jnp.arange(T // BS, dtype=F32) * BS)[:, None, None] * inv_freq
    ang_b = jnp.arange(BS, dtype=F32)[None, :, None] * inv_freq
    ca, sa, cb, sb = jnp.cos(ang_a), jnp.sin(ang_a), jnp.cos(ang_b), jnp.sin(ang_b)
    cos = (ca * cb - sa * sb).reshape(T, half)
    sin = (sa * cb + ca * sb).reshape(T, half)
    return jnp.concatenate([cos, cos], axis=1), jnp.concatenate([-sin, sin], axis=1)


def _mixers(x2d, g_mix, w_in, lb, hgrn_norm_g):
    T = x2d.shape[0]
    W = HGRN_WIDTH
    w = w_in
    cos, sin = _rope_tables(T)
    log_lb = jnp.log(lb).reshape(1, W)
    log_1m = jnp.log1p(-lb).reshape(1, W)
    hq, h = _norm_proj(x2d, g_mix, w, 0 * W, W, _ep_silu, BF16)
    logf = _proj(h, w, 1 * W, W, _ep_logf, F32, col_extras=(log_lb, log_1m))
    hi = _proj(h, w, 2 * W, W, _ep_identity, BF16)
    hog = _proj(h, w, 3 * W, W, _ep_silu, BF16)
    scale = MOBA_HEAD_DIM ** -0.5
    mqt = _proj_t(h, w, 4 * W, functools.partial(_ep_rope, scale=scale), row_extras=(cos, sin))
    blk = jnp.arange(T, dtype=I32)[:, None] // MOBA_BLOCK
    blk_onehot = (blk == jnp.arange(V7X_LANES, dtype=I32)[None, :]).astype(F32)
    mk = _proj(h, w, 5 * W, W, _ep_rope_aug, BF16, row_extras=(cos, sin, blk_onehot), widen=2)
    mvt = _proj_t(h, w, 6 * W, _ep_identity, ones_rows=MOBA_VT_ROWS - MOBA_HEAD_DIM)
    gates = _proj(h, w, 7 * W, 2 * D_MODEL, _ep_sigmoid, BF16)
    o_hgrn = _hgrn(hq, logf, hi, hog, hgrn_norm_g)
    o_moba = _moba(mqt, mk, mvt)
    return o_hgrn, o_moba, gates


def _moe_plan(info, cnt, T):
    R = MOE_ROWS
    eid = info[:, 0:TOPK_IN_GROUP].astype(I32)
    rank = info[:, 4:4 + TOPK_IN_GROUP].astype(I32)
    counts = cnt[0, :N_EXPERTS].astype(I32)
    padded = (counts + R - 1) // R * R
    pad_end = jnp.cumsum(padded)
    pad_start = pad_end - padded
    codes = (eid * (1 << RANK_BITS) + rank).reshape(-1)
    n_blocks = (T * TOPK_IN_GROUP) // R + N_EXPERTS
    block_expert = jnp.minimum(
        jnp.searchsorted(pad_end, jnp.arange(n_blocks, dtype=I32) * R, side="right"), N_EXPERTS - 1).astype(I32)
    has = counts > 0
    segment_expert = jnp.argsort(jnp.logical_not(has), stable=True).astype(I32)
    block_segment = (jnp.cumsum(has.astype(I32))[block_expert] - 1).astype(I32)
    n_used = jnp.stack([pad_end[-1] // R, jnp.sum(has.astype(I32))]).astype(I32)
    return codes, pad_start.astype(I32), pad_end.astype(I32), block_segment, segment_expert, n_used, n_blocks * R


def kernel(x, p, norm_mix_g, w_in, hgrn_lb_raw, hgrn_norm_g, w_up_hgrn, w_up_moba, w_out, norm_ffn_g,
           w_router_group, w_router_expert, w_exp_gate, w_exp_up, w_exp_down, norm_ple_g, w_ple_gate,
           w_ple_proj, norm_final_g):
    B, T, D = x.shape
    assert B == 1 and D == D_MODEL and w_in.shape[0] == 1 and T % (4 * MOBA_BLOCK) == 0
    lower_bounds = jnp.cumsum(jax.nn.softmax(hgrn_lb_raw.astype(F32), axis=0), axis=0)
    x2d = x.reshape(T, D)
    o_hgrn, o_moba, gates = _mixers(x2d, norm_mix_g[0], w_in[0], lower_bounds[0], hgrn_norm_g[0])
    merged = _merge(o_hgrn, o_moba, gates, w_up_hgrn[0].astype(BF16), w_up_moba[0].astype(BF16))
    w_router = jnp.pad(jnp.concatenate([w_router_group[0], w_router_expert[0]], axis=1),
                       ((0, 0), (0, V7X_LANES - N_GROUPS - N_EXPERTS))).astype(BF16)
    x1, h2p, info, cnt = _outproj(merged, x2d, w_out[0].astype(BF16), norm_ffn_g[0], w_router)
    codes, pad_start, pad_end, block_segment, segment_expert, n_used, m_pad = _moe_plan(info, cnt, T)
    xs = _dispatch(codes, pad_start, pad_end, h2p, m_pad)
    yb = _experts(block_segment, segment_expert, n_used, xs, w_exp_gate[0], w_exp_up[0], w_exp_down[0])
    out = _combine(codes, pad_start, x1, info, yb, p[0].reshape(T, PLE_DIM), norm_ple_g[0],
                   w_ple_gate[0].astype(BF16), w_ple_proj[0].astype(BF16), norm_final_g)
    return out.reshape(B, T, D)
```

```python
import functools

import jax
import jax.numpy as jnp
from jax import lax
from jax.experimental import pallas as pl
from jax.experimental.pallas import tpu as pltpu

F32 = jnp.float32
BF16 = jnp.bfloat16
I32 = jnp.int32

D_MODEL = 2048
PLE_DIM = 256
HGRN_HEADS = 8
HGRN_HEAD_DIM = 128
HGRN_WIDTH = HGRN_HEADS * HGRN_HEAD_DIM
MOBA_HEADS = 8
MOBA_HEAD_DIM = 128
MOBA_WIDTH = MOBA_HEADS * MOBA_HEAD_DIM
MOBA_BLOCK = 256
MOBA_TOPK = 3
ROPE_THETA = 10000.0
N_GROUPS = 4
EXPERTS_PER_GROUP = 8
N_EXPERTS = N_GROUPS * EXPERTS_PER_GROUP
TOPK_IN_GROUP = 2
D_EXPERT = 512
EPS = 1e-6
NEG_INF = -1e30

V7X_LANES = 128
V7X_SUBLANES = 8
V7X_VMEM_BUDGET_BYTES = 56 * 1024 * 1024

HGRN_CHUNK = 128
MOE_ROWS = 256


def _cparams(n_grid, vmem_bytes):
    return pltpu.CompilerParams(
        dimension_semantics=("arbitrary",) * n_grid,
        vmem_limit_bytes=int(min(max(vmem_bytes, 16 * 1024 * 1024), V7X_VMEM_BUDGET_BYTES)),
    )


def _nbytes(shape, dtype):
    n = 1
    for s in shape:
        n *= s
    return n * jnp.dtype(dtype).itemsize


def _rms(x, g):
    ms = jnp.mean(x * x, axis=-1, keepdims=True)
    return x * lax.rsqrt(ms + EPS) * g


def _ep_identity(acc):
    return acc


def _ep_silu(acc):
    return acc * jax.nn.sigmoid(acc)


def _ep_sigmoid(acc):
    return jax.nn.sigmoid(acc)


def _ep_logf(acc, la_ref, lc_ref):
    ls = jnp.minimum(acc, 0.0) - jnp.log(1.0 + jnp.exp(-jnp.abs(acc)))
    u = la_ref[...]
    v = lc_ref[...] + ls
    return jnp.maximum(u, v) + jnp.log(1.0 + jnp.exp(-jnp.abs(u - v)))


def _ep_rope(acc, cos_ref, sin_ref, *, scale):
    cos = cos_ref[...]
    sin = sin_ref[...]
    outs = []
    for hh in range(acc.shape[1] // MOBA_HEAD_DIM):
        a = acc[:, hh * MOBA_HEAD_DIM:(hh + 1) * MOBA_HEAD_DIM]
        r = pltpu.roll(a, MOBA_HEAD_DIM // 2, axis=1)
        outs.append((a * cos + r * sin) * scale)
    return jnp.concatenate(outs, axis=1)


def _ep_rope_aug(acc, cos_ref, sin_ref, oh_ref):
    cos = cos_ref[...]
    sin = sin_ref[...]
    oh = oh_ref[...]
    outs = []
    for hh in range(acc.shape[1] // MOBA_HEAD_DIM):
        a = acc[:, hh * MOBA_HEAD_DIM:(hh + 1) * MOBA_HEAD_DIM]
        outs.append(a * cos + pltpu.roll(a, MOBA_HEAD_DIM // 2, axis=1) * sin)
        outs.append(oh)
    return jnp.concatenate(outs, axis=1)


def _cast_weight_once(w_ref, wb_ref, row_axis):
    @pl.when(pl.program_id(row_axis) == 0)
    def _():
        wb_ref[...] = w_ref[...].astype(wb_ref.dtype)


def _proj_kernel(h_ref, w_ref, *refs, epilogue):
    *extra, o_ref, wb_ref = refs
    _cast_weight_once(w_ref, wb_ref, 1)
    acc = jnp.dot(h_ref[...], wb_ref[...], preferred_element_type=F32)
    o_ref[...] = epilogue(acc, *extra).astype(o_ref.dtype)


def _proj(h, w, col0, ncols, epilogue, out_dtype, row_extras=(), col_extras=(), tm=1024, tn=1024, widen=1):
    T, K = h.shape
    tn = min(tn, ncols)
    tm = min(tm, T)
    cb = col0 // tn
    otn = widen * tn
    in_specs = [
        pl.BlockSpec((tm, K), lambda j, i: (i, 0)),
        pl.BlockSpec((K, tn), lambda j, i: (0, cb + j)),
    ]
    for e in row_extras:
        in_specs.append(pl.BlockSpec((tm, e.shape[1]), lambda j, i: (i, 0)))
    for e in col_extras:
        in_specs.append(pl.BlockSpec((1, tn), lambda j, i: (0, j)))
    vmem = 2 * (_nbytes((tm, K), h.dtype) + _nbytes((K, tn), w.dtype) + _nbytes((tm, otn), out_dtype))
    vmem += 3 * _nbytes((tm, otn), F32) + _nbytes((K, tn), BF16)
    return pl.pallas_call(
        functools.partial(_proj_kernel, epilogue=epilogue),
        out_shape=jax.ShapeDtypeStruct((T, widen * ncols), out_dtype),
        grid=(ncols // tn, T // tm),
        in_specs=in_specs,
        out_specs=pl.BlockSpec((tm, otn), lambda j, i: (i, j)),
        scratch_shapes=[pltpu.VMEM((K, tn), BF16)],
        compiler_params=_cparams(2, vmem),
        name="proj",
    )(h, w, *row_extras, *col_extras)


def _norm_proj_kernel(x_ref, g_ref, w_ref, o_ref, h_ref, wb_ref, *, epilogue):
    _cast_weight_once(w_ref, wb_ref, 0)
    h = _rms(x_ref[...], g_ref[...]).astype(h_ref.dtype)
    h_ref[...] = h
    o_ref[...] = epilogue(jnp.dot(h, wb_ref[...], preferred_element_type=F32)).astype(o_ref.dtype)


def _norm_proj(x, g, w, col0, ncols, epilogue, out_dtype, tm=512):
    T, K = x.shape
    cb = col0 // ncols
    vmem = 2 * (_nbytes((tm, K), F32) + _nbytes((tm, ncols), out_dtype) + _nbytes((tm, K), BF16))
    vmem += _nbytes((K, ncols), F32) + _nbytes((K, ncols), BF16) + 2 * _nbytes((tm, K), F32) + 3 * _nbytes((tm, ncols), F32)
    return pl.pallas_call(
        functools.partial(_norm_proj_kernel, epilogue=epilogue),
        out_shape=(jax.ShapeDtypeStruct((T, ncols), out_dtype), jax.ShapeDtypeStruct((T, K), BF16)),
        grid=(T // tm,),
        in_specs=[
            pl.BlockSpec((tm, K), lambda i: (i, 0)),
            pl.BlockSpec((1, K), lambda i: (0, 0)),
            pl.BlockSpec((K, ncols), lambda i: (0, cb), pipeline_mode=pl.Buffered(1)),
        ],
        out_specs=(pl.BlockSpec((tm, ncols), lambda i: (i, 0)), pl.BlockSpec((tm, K), lambda i: (i, 0))),
        scratch_shapes=[pltpu.VMEM((K, ncols), BF16)],
        compiler_params=_cparams(1, vmem),
        name="norm_proj",
    )(x, g.reshape(1, K), w)


MOBA_VT_ROWS = MOBA_HEAD_DIM + 16
MOBA_BLOCKS_PER_STEP = 4


def _proj_t_kernel(h_ref, w_ref, *refs, epilogue, ones_rows):
    *extra, o_ref, wb_ref = refs
    BS, HD = MOBA_BLOCK, MOBA_HEAD_DIM
    _cast_weight_once(w_ref, wb_ref, 0)
    acc = epilogue(jnp.dot(h_ref[...], wb_ref[...], preferred_element_type=F32), *extra)
    ones = jnp.ones((ones_rows, BS), F32) if ones_rows else None
    for b in range(acc.shape[0] // BS):
        parts = []
        for hh in range(acc.shape[1] // HD):
            parts.append(acc[b * BS:(b + 1) * BS, hh * HD:(hh + 1) * HD].T)
            if ones_rows:
                parts.append(ones)
        o_ref[b] = jnp.concatenate(parts, axis=0).astype(o_ref.dtype)


def _proj_t(h, w, col0, epilogue, row_extras=(), ones_rows=0, tm=1024):
    T, K = h.shape
    tn = MOBA_WIDTH
    tm = min(tm, T)
    cb = col0 // tn
    rows = MOBA_HEADS * (MOBA_HEAD_DIM + ones_rows)
    in_specs = [pl.BlockSpec((tm, K), lambda i: (i, 0)), pl.BlockSpec((K, tn), lambda i: (0, cb))]
    for e in row_extras:
        in_specs.append(pl.BlockSpec((tm, e.shape[1]), lambda i: (i, 0)))
    vmem = 2 * (_nbytes((tm, K), h.dtype) + _nbytes((K, tn), w.dtype) + _nbytes((tm, 2 * tn), BF16))
    vmem += 4 * _nbytes((tm, tn), F32) + _nbytes((K, tn), BF16)
    return pl.pallas_call(
        functools.partial(_proj_t_kernel, epilogue=epilogue, ones_rows=ones_rows),
        out_shape=jax.ShapeDtypeStruct((T // MOBA_BLOCK, rows, MOBA_BLOCK), BF16),
        grid=(T // tm,),
        in_specs=in_specs,
        out_specs=pl.BlockSpec((tm // MOBA_BLOCK, rows, MOBA_BLOCK), lambda i: (i, 0, 0)),
        scratch_shapes=[pltpu.VMEM((K, tn), BF16)],
        compiler_params=_cparams(1, vmem),
        name="proj_t",
    )(h, w, *row_extras)


def _hgrn_kernel(q_ref, g_ref, v_ref, og_ref, ng_ref, o_ref, st_ref, code_ref):
    W, HD, NH, C, S = HGRN_WIDTH, HGRN_HEAD_DIM, HGRN_HEADS, HGRN_CHUNK, V7X_SUBLANES
    J = C // S

    @pl.when(pl.program_id(0) == 0)
    def _():
        st_ref[...] = jnp.zeros_like(st_ref)
        tr = lax.broadcasted_iota(I32, (C, C), 0)
        tc = lax.broadcasted_iota(I32, (C, C), 1)
        xr = tr ^ tc
        code = jnp.zeros((C, C), I32)
        for lvl in range(1, 8):
            code = jnp.where(xr >= (1 << (lvl - 1)), lvl, code)
        code_ref[...] = jnp.where(tc > tr, -1, code)

    def r3(x):
        return x.astype(F32).reshape(J, S, W)

    def sub_bcast(x3, r):
        return jnp.broadcast_to(x3[:, r:r + 1, :], x3.shape)

    g3, q3, v3 = r3(g_ref[...]), r3(q_ref[...]), r3(v_ref[...])
    sub = lax.broadcasted_iota(I32, (1, S, W), 1)

    c3 = g3
    for s in (1, 2, 4):
        c3 = c3 + jnp.where(sub >= s, pltpu.roll(c3, s, axis=1), 0.0)
    run = jnp.zeros((1, 1, W), F32)
    carry = []
    for j in range(J):
        carry.append(run)
        run = run + c3[j:j + 1, S - 1:S, :]
    b3 = c3 + jnp.concatenate(carry, axis=0)
    bC = run

    k3 = 1.0 - jnp.exp(g3)
    qe3 = q3 * jnp.exp(b3)
    ks3 = k3 * jnp.exp(bC - b3)

    levels = [(0, q3, k3)]
    ref1 = jnp.where(sub % 2 == 0, b3, pltpu.roll(b3, 1, axis=1))
    ref2 = jnp.where(sub < 4, sub_bcast(b3, 1), sub_bcast(b3, 5))
    ref4 = sub_bcast(b3, 3)
    for lvl, (ref, upper) in enumerate(((ref1, sub % 2 == 1), (ref2, sub % 4 >= 2), (ref4, sub >= 4)), start=1):
        e = jnp.exp(-jnp.abs(b3 - ref))
        levels.append((lvl, jnp.where(upper, q3 * e, 0.0), jnp.where(upper, 0.0, k3 * e)))
    zero_group = jnp.zeros((1, S, W), F32)
    for lvl, half in enumerate((1, 2, 4, 8), start=4):
        qparts, kparts = [], []
        for j in range(J):
            jr = (j // (2 * half)) * (2 * half) + half - 1
            ref = b3[jr:jr + 1, S - 1:S, :]
            if (j % (2 * half)) >= half:
                qparts.append(q3[j:j + 1] * jnp.exp(b3[j:j + 1] - ref))
                kparts.append(zero_group)
            else:
                qparts.append(zero_group)
                kparts.append(k3[j:j + 1] * jnp.exp(ref - b3[j:j + 1]))
        levels.append((lvl, jnp.concatenate(qparts, axis=0), jnp.concatenate(kparts, axis=0)))

    code = code_ref[...]

    def mat(x3, h):
        return x3.reshape(C, W)[:, h * HD:(h + 1) * HD].astype(BF16)

    nt = (((1,), (1,)), ((), ()))
    tn = (((0,), (0,)), ((), ()))
    ebc = jnp.exp(bC).reshape(1, W)
    ng = ng_ref[...]
    for h in range(NH):
        a_mat = jnp.zeros((C, C), F32)
        for lvl, qr, kr in levels:
            s = lax.dot_general(mat(qr, h), mat(kr, h), nt, preferred_element_type=F32)
            a_mat = jnp.where(code == lvl, s, a_mat)
        vh = mat(v3, h)
        st = st_ref[h]
        o = jnp.dot(a_mat.astype(BF16), vh, preferred_element_type=F32)
        o = o + lax.dot_general(mat(qe3, h), st.astype(BF16), nt, preferred_element_type=F32)
        o = _rms(o, ng) * og_ref[:, h * HD:(h + 1) * HD].astype(F32)
        o_ref[:, h * HD:(h + 1) * HD] = o.astype(o_ref.dtype)
        st_ref[h] = st * ebc[:, h * HD:(h + 1) * HD] + lax.dot_general(
            vh, mat(ks3, h), tn, preferred_element_type=F32)


def _hgrn(q, logf, v, og, norm_g):
    T, W = q.shape
    C = HGRN_CHUNK
    blk = pl.BlockSpec((C, W), lambda c: (c, 0))
    vmem = 64 * _nbytes((C, W), F32)
    return pl.pallas_call(
        _hgrn_kernel,
        out_shape=jax.ShapeDtypeStruct((T, W), BF16),
        grid=(T // C,),
        in_specs=[blk, blk, blk, blk, pl.BlockSpec((1, HGRN_HEAD_DIM), lambda c: (0, 0))],
        out_specs=blk,
        scratch_shapes=[pltpu.VMEM((HGRN_HEADS, HGRN_HEAD_DIM, HGRN_HEAD_DIM), F32), pltpu.VMEM((C, C), I32)],
        compiler_params=_cparams(1, vmem),
        name="hgrn",
    )(q, logf, v, og, norm_g.reshape(1, HGRN_HEAD_DIM))


def _moba_kernel(qt_ref, k_ref, vt_ref, o_ref, km_ref):
    BS, HD, VR = MOBA_BLOCK, MOBA_HEAD_DIM, MOBA_VT_ROWS
    T = k_ref.shape[0]
    NB = T // BS
    G = o_ref.shape[1] // HD
    cur = pl.program_id(1)

    @pl.when(cur == 0)
    def _():
        for g in range(G):
            kf = k_ref[:, 2 * g * HD:(2 * g + 1) * HD].astype(F32).reshape(NB, BS, HD)
            km_ref[g] = jnp.sum(kf, axis=1) * (1.0 / BS)

    blk = lax.broadcasted_iota(I32, (NB, BS), 0)
    pad = jnp.zeros((V7X_LANES - NB, BS), F32)
    qts = [qt_ref[0, g * HD:(g + 1) * HD, :] for g in range(G)]
    gts = [jnp.dot(km_ref[g].astype(BF16), qts[g], preferred_element_type=F32) for g in range(G)]
    qcs = []
    for g in range(G):
        gt = jnp.where(blk < cur, gts[g], NEG_INF)
        sel = blk == cur
        for _ in range(MOBA_TOPK):
            mx = jnp.max(gt, axis=0, keepdims=True)
            idx = jnp.min(jnp.where(gt == mx, blk, NB), axis=0, keepdims=True)
            pick = (blk == idx) & (mx > 0.5 * NEG_INF)
            sel = sel | pick
            gt = jnp.where(pick, NEG_INF, gt)
        pen = jnp.concatenate([jnp.where(sel, 0.0, NEG_INF), pad], axis=0).astype(BF16)
        qcs.append(jnp.concatenate([qts[g], pen], axis=0))

    KB = MOBA_BLOCKS_PER_STEP

    def body(c, carry, own_step=False):
        ms, accs = carry
        r = pl.multiple_of(c * KB * BS, KB * BS)
        sns = [jnp.dot(k_ref[pl.ds(r, KB * BS), 2 * g * HD:(2 * g + 2) * HD], qcs[g], preferred_element_type=F32)
               for g in range(G)]
        if own_step:
            krow = lax.broadcasted_iota(I32, (KB * BS, BS), 0)
            qcol = lax.broadcasted_iota(I32, (KB * BS, BS), 1)
            keep = (c * KB + krow // BS != cur) | (krow % BS <= qcol)
            sns = [jnp.where(keep, s, NEG_INF) for s in sns]
        new_ms, alphas, pns = [], [], []
        for g in range(G):
            m_new = jnp.maximum(ms[g], jnp.max(sns[g], axis=0, keepdims=True))
            alphas.append(jnp.exp(ms[g] - m_new))
            pns.append(jnp.exp(sns[g] - m_new).astype(BF16))
            new_ms.append(m_new)
        new_accs = []
        for g in range(G):
            pv = alphas[g] * accs[g]
            for j in range(KB):
                pv = pv + jnp.dot(vt_ref[KB * c + j, g * VR:(g + 1) * VR, :], pns[g][j * BS:(j + 1) * BS],
                                  preferred_element_type=F32)
            new_accs.append(pv)
        return tuple(new_ms), tuple(new_accs)

    ms = tuple(jnp.full((1, BS), NEG_INF, F32) for _ in range(G))
    accs = tuple(jnp.zeros((VR, BS), F32) for _ in range(G))
    carry = lax.fori_loop(0, cur // KB, body, (ms, accs))
    _, accs = body(cur // KB, carry, own_step=True)
    for g in range(G):
        ot = accs[g][:HD, :] / accs[g][HD:HD + 1, :]
        o_ref[:, g * HD:(g + 1) * HD] = ot.T.astype(o_ref.dtype)


def _moba(mqt, mk_aug, mvt, heads_per_step=4):
    T = mk_aug.shape[0]
    BS, HD, G, VR = MOBA_BLOCK, MOBA_HEAD_DIM, heads_per_step, MOBA_VT_ROWS
    NB = T // BS
    vmem = _nbytes((T, 2 * G * HD), BF16) + _nbytes((NB, G * VR, BS), BF16) + 8 * _nbytes((BS, G * HD), BF16)
    vmem += 16 * G * _nbytes((BS, BS), F32)
    resident = pl.Buffered(1)
    return pl.pallas_call(
        _moba_kernel,
        out_shape=jax.ShapeDtypeStruct((T, MOBA_WIDTH), BF16),
        grid=(MOBA_HEADS // G, NB),
        in_specs=[
            pl.BlockSpec((1, G * HD, BS), lambda h, i: (i, h, 0)),
            pl.BlockSpec((T, 2 * G * HD), lambda h, i: (0, h), pipeline_mode=resident),
            pl.BlockSpec((NB, G * VR, BS), lambda h, i: (0, h, 0), pipeline_mode=resident),
        ],
        out_specs=pl.BlockSpec((BS, G * HD), lambda h, i: (i, h)),
        scratch_shapes=[pltpu.VMEM((G, NB, HD), F32)],
        compiler_params=_cparams(2, vmem),
        name="moba",
    )(mqt, mk_aug, mvt)


def _merge_kernel(oh_ref, om_ref, ga_ref, gb_ref, wh_ref, wm_ref, o_ref):
    a = jnp.dot(oh_ref[...], wh_ref[...], preferred_element_type=F32)
    b = jnp.dot(om_ref[...], wm_ref[...], preferred_element_type=F32)
    o_ref[...] = (ga_ref[...].astype(F32) * a + gb_ref[...].astype(F32) * b).astype(o_ref.dtype)


def _merge(o_hgrn, o_moba, gates, w_up_hgrn, w_up_moba, tm=512):
    T = o_hgrn.shape[0]
    D = D_MODEL
    vmem = 2 * (_nbytes((tm, HGRN_WIDTH), F32) + _nbytes((tm, MOBA_WIDTH), BF16) + 3 * _nbytes((tm, D), BF16)
                + 2 * _nbytes((HGRN_WIDTH, D), BF16)) + 3 * _nbytes((tm, D), F32)
    return pl.pallas_call(
        _merge_kernel,
        out_shape=jax.ShapeDtypeStruct((T, D), BF16),
        grid=(T // tm,),
        in_specs=[
            pl.BlockSpec((tm, HGRN_WIDTH), lambda i: (i, 0)),
            pl.BlockSpec((tm, MOBA_WIDTH), lambda i: (i, 0)),
            pl.BlockSpec((tm, D), lambda i: (i, 0)),
            pl.BlockSpec((tm, D), lambda i: (i, 1)),
            pl.BlockSpec((HGRN_WIDTH, D), lambda i: (0, 0)),
            pl.BlockSpec((MOBA_WIDTH, D), lambda i: (0, 0)),
        ],
        out_specs=pl.BlockSpec((tm, D), lambda i: (i, 0)),
        compiler_params=_cparams(1, vmem),
        name="merge",
    )(o_hgrn, o_moba, gates, gates, w_up_hgrn, w_up_moba)


def _pack_bf16_pairs(x):
    C = x.shape[1] // 2
    b = lax.bitcast_convert_type(x, jnp.uint32)
    r = (b + jnp.uint32(0x7FFF) + ((b >> 16) & jnp.uint32(1))) >> 16
    return r[:, :C] | (r[:, C:] << 16)


def _unpack_bf16_pairs(p):
    lo = lax.bitcast_convert_type(p << 16, F32)
    hi = lax.bitcast_convert_type(p & jnp.uint32(0xFFFF0000), F32)
    return jnp.concatenate([lo, hi], axis=1)


def _outproj_kernel(m_ref, x_ref, w_ref, g_ref, wr_ref, x1_ref, h2p_ref, info_ref, cnt_ref, prev_ref, carry_ref):
    step = pl.program_id(0)
    n_tiles = pl.num_programs(0) - 1

    @pl.when(step == 0)
    def _():
        carry_ref[...] = jnp.zeros_like(carry_ref)

    def finish_previous():
        h2 = _rms(prev_ref[...], g_ref[...])
        h2p_ref[...] = _pack_bf16_pairs(h2)
        info_ref[...] = _route(h2, wr_ref, carry_ref)
        cnt_ref[...] = carry_ref[...]

    def project():
        return x_ref[...] + jnp.dot(m_ref[...], w_ref[...], preferred_element_type=F32)

    @pl.when(step == 0)
    def _():
        x1 = project()
        x1_ref[...] = x1
        prev_ref[...] = x1

    @pl.when((step > 0) & (step < n_tiles))
    def _():
        x1 = project()
        finish_previous()
        x1_ref[...] = x1
        prev_ref[...] = x1

    @pl.when(step == n_tiles)
    def _():
        finish_previous()


def _outproj(merged, x, w_out, g_ffn, w_router, tm=512):
    T, D = x.shape
    n_tiles = T // tm
    vmem = 2 * (_nbytes((tm, D), BF16) + 3 * _nbytes((tm, D), F32) + _nbytes((D, D), BF16)) + 7 * _nbytes((tm, D), F32)
    cur = lambda i: (jnp.minimum(i, n_tiles - 1), 0)
    prev = lambda i: (jnp.maximum(i - 1, 0), 0)
    fixed = lambda i: (0, 0)
    return pl.pallas_call(
        _outproj_kernel,
        out_shape=(jax.ShapeDtypeStruct((T, D), F32), jax.ShapeDtypeStruct((T, D // 2), jnp.uint32),
                   jax.ShapeDtypeStruct((T, V7X_LANES), F32), jax.ShapeDtypeStruct((1, V7X_LANES), F32)),
        grid=(n_tiles + 1,),
        in_specs=[
            pl.BlockSpec((tm, D), cur),
            pl.BlockSpec((tm, D), cur),
            pl.BlockSpec((D, D), fixed, pipeline_mode=pl.Buffered(1)),
            pl.BlockSpec((1, D), fixed),
            pl.BlockSpec((D, V7X_LANES), fixed),
        ],
        out_specs=(pl.BlockSpec((tm, D), cur), pl.BlockSpec((tm, D // 2), prev), pl.BlockSpec((tm, V7X_LANES), prev),
                   pl.BlockSpec((1, V7X_LANES), fixed)),
        scratch_shapes=[pltpu.VMEM((tm, D), F32), pltpu.VMEM((1, V7X_LANES), F32)],
        compiler_params=_cparams(1, vmem),
        name="outproj",
    )(merged, x, w_out, g_ffn.reshape(1, D), w_router)


def _route(h2, w_ref, carry_ref):
    tm = h2.shape[0]
    logits = jnp.dot(h2.astype(BF16), w_ref[...], preferred_element_type=F32)
    lane = lax.broadcasted_iota(I32, (tm, V7X_LANES), 1)
    is_g = lane < N_GROUPS
    gl = jnp.where(is_g, logits, NEG_INF)
    gmax = jnp.max(gl, axis=1, keepdims=True)
    g_sel = jnp.min(jnp.where(gl == gmax, lane, V7X_LANES), axis=1, keepdims=True)
    gsum = jnp.sum(jnp.where(is_g, jnp.exp(gl - gmax), 0.0), axis=1, keepdims=True)
    p_group = 1.0 / gsum
    lo = N_GROUPS + EXPERTS_PER_GROUP * g_sel
    emask = (lane >= lo) & (lane < lo + EXPERTS_PER_GROUP)
    el = jnp.where(emask, logits, NEG_INF)
    e1 = jnp.max(el, axis=1, keepdims=True)
    i1 = jnp.min(jnp.where((el == e1) & emask, lane, V7X_LANES), axis=1, keepdims=True)
    emask2 = emask & (lane != i1)
    el2 = jnp.where(emask2, logits, NEG_INF)
    e2 = jnp.max(el2, axis=1, keepdims=True)
    i2 = jnp.min(jnp.where((el2 == e2) & emask2, lane, V7X_LANES), axis=1, keepdims=True)
    r = jnp.exp(e2 - e1)
    w1 = p_group / (1.0 + r)
    w2 = p_group * r / (1.0 + r)
    eid1 = i1 - N_GROUPS
    eid2 = i2 - N_GROUPS
    oh1 = jnp.where(lane == eid1, 1.0, 0.0)
    oh2 = jnp.where(lane == eid2, 1.0, 0.0)
    cnt = oh1 + oh2
    tri = jnp.where(lax.broadcasted_iota(I32, (tm, tm), 0) > lax.broadcasted_iota(I32, (tm, tm), 1), 1.0, 0.0)
    before = jnp.dot(tri.astype(BF16), cnt.astype(BF16), preferred_element_type=F32) + carry_ref[...]
    rank1 = jnp.sum(oh1 * before, axis=1, keepdims=True)
    rank2 = jnp.sum(oh2 * before, axis=1, keepdims=True)
    carry_ref[...] = carry_ref[...] + jnp.sum(cnt, axis=0, keepdims=True)
    info = jnp.zeros((tm, V7X_LANES), F32)
    for k, val in enumerate((eid1.astype(F32), eid2.astype(F32), w1, w2, rank1, rank2)):
        info = jnp.where(lane == k, val, info)
    return info


def _row_copy(src_ref, src_row, dst_ref, dst_row, sem):
    return pltpu.make_async_copy(src_ref.at[pl.ds(src_row, 1), :], dst_ref.at[pl.ds(dst_row, 1), :], sem)


ROW_DMA_UNROLL = 8


def _slots_kernel(info_ref, ps_ref, o_ref):
    info = info_ref[...]
    lane = lax.broadcasted_iota(I32, info.shape, 1)
    lane_f = lane.astype(F32)
    ps = ps_ref[...]
    out = jnp.zeros(info.shape, F32)
    for k in range(TOPK_IN_GROUP):
        start = jnp.sum(jnp.where(lane_f == info[:, k:k + 1], ps, 0.0), axis=1, keepdims=True)
        out = jnp.where(lane == k, start + info[:, 4 + k:5 + k], out)
    o_ref[...] = out.astype(I32)


def _slots(info, pad_start, tm=1024):
    T = info.shape[0]
    ps = jnp.pad(pad_start.astype(F32), (0, V7X_LANES - N_EXPERTS)).reshape(1, V7X_LANES)
    out = pl.pallas_call(
        _slots_kernel,
        out_shape=jax.ShapeDtypeStruct((T, V7X_LANES), I32),
        grid=(T // tm,),
        in_specs=[pl.BlockSpec((tm, V7X_LANES), lambda i: (i, 0)), pl.BlockSpec((1, V7X_LANES), lambda i: (0, 0))],
        out_specs=pl.BlockSpec((tm, V7X_LANES), lambda i: (i, 0)),
        compiler_params=_cparams(1, 16 * _nbytes((tm, V7X_LANES), F32)),
        name="slots",
    )(info, ps)
    return out[:, :TOPK_IN_GROUP].reshape(-1)


def _dispatch_kernel(slot_ref, pend_ref, h_ref, xs_ref, zero_ref, sem):
    tm = h_ref.shape[0]
    step = pl.program_id(0)
    base = step * tm

    @pl.when(step == 0)
    def _():
        zero_ref[...] = jnp.zeros_like(zero_ref)

        def tail(e):
            return pltpu.make_async_copy(
                zero_ref, xs_ref.at[pl.ds(pl.multiple_of(pend_ref[e] - MOE_ROWS, MOE_ROWS), MOE_ROWS), :], sem)

        def nonempty(e):
            return pend_ref[e] > (pend_ref[e - 1] if e else 0)

        def unused(b):
            return pltpu.make_async_copy(
                zero_ref, xs_ref.at[pl.ds(pl.multiple_of(b * MOE_ROWS, MOE_ROWS), MOE_ROWS), :], sem)

        first_unused = pend_ref[N_EXPERTS - 1] // MOE_ROWS
        n_blocks = xs_ref.shape[0] // MOE_ROWS
        for e in range(N_EXPERTS):
            pl.when(nonempty(e))(lambda e=e: tail(e).start())
        lax.fori_loop(first_unused, n_blocks, lambda b, c: (unused(b).start(), c)[1], 0)
        for e in range(N_EXPERTS):
            pl.when(nonempty(e))(lambda e=e: tail(e).wait())
        lax.fori_loop(first_unused, n_blocks, lambda b, c: (unused(b).wait(), c)[1], 0)

    def issue(rb, c):
        for u in range(ROW_DMA_UNROLL):
            r = rb * ROW_DMA_UNROLL + u
            for k in range(TOPK_IN_GROUP):
                _row_copy(h_ref, r, xs_ref, slot_ref[(base + r) * TOPK_IN_GROUP + k], sem).start(priority=k % 2)
        return c

    lax.fori_loop(0, tm // ROW_DMA_UNROLL, issue, 0)
    for _ in range(tm * TOPK_IN_GROUP):
        _row_copy(h_ref, 0, xs_ref, 0, sem).wait()


def _dispatch(slots, pad_end, h2p, m_pad, tm=256):
    T, C = h2p.shape
    return pl.pallas_call(
        _dispatch_kernel,
        out_shape=jax.ShapeDtypeStruct((m_pad, C), h2p.dtype),
        grid_spec=pltpu.PrefetchScalarGridSpec(
            num_scalar_prefetch=2,
            grid=(T // tm,),
            in_specs=[pl.BlockSpec((tm, C), lambda i, sl, pe: (i, 0))],
            out_specs=pl.BlockSpec(memory_space=pl.ANY),
            scratch_shapes=[pltpu.VMEM((MOE_ROWS, C), h2p.dtype), pltpu.SemaphoreType.DMA(())],
        ),
        compiler_params=_cparams(1, 6 * _nbytes((tm, C), h2p.dtype)),
        name="dispatch",
    )(slots, pad_end, h2p)


def _expert_kernel(seg_ref, sexp_ref, nu_ref, xs_ref, wg_ref, wu_ref, wd_ref, y_ref, wgf_ref, wuf_ref, wdf_ref,
                   wgb_ref, wub_ref, wdb_ref, sem):
    b = pl.program_id(0)
    used = b < nu_ref[0]
    seg = seg_ref[b]
    first = used & ((b == 0) | (seg != seg_ref[jnp.maximum(b - 1, 0)]))

    def weights_copy(s, part):
        slot = s % 2
        src, dst = ((wg_ref, wgf_ref), (wu_ref, wuf_ref), (wd_ref, wdf_ref))[part]
        return pltpu.make_async_copy(src.at[sexp_ref[s]], dst.at[slot], sem.at[slot])

    @pl.when(b == 0)
    def _():
        for part in range(3):
            weights_copy(0, part).start()

    @pl.when(first)
    def _():
        for part in range(3):
            weights_copy(seg, part).wait()
        slot = seg % 2
        wgb_ref[...] = wgf_ref[slot].astype(BF16)
        wub_ref[...] = wuf_ref[slot].astype(BF16)
        wdb_ref[...] = wdf_ref[slot].astype(BF16)

    @pl.when(first & (seg + 1 < nu_ref[1]))
    def _():
        for part in range(3):
            weights_copy(seg + 1, part).start()

    @pl.when(used)
    def _():
        x = _unpack_bf16_pairs(xs_ref[...]).astype(BF16)
        a = jnp.dot(x, wgb_ref[...], preferred_element_type=F32)
        u = jnp.dot(x, wub_ref[...], preferred_element_type=F32)
        hm = (a * jax.nn.sigmoid(a) * u).astype(BF16)
        y_ref[...] = _pack_bf16_pairs(jnp.dot(hm, wdb_ref[...], preferred_element_type=F32))

    @pl.when(jnp.logical_not(used))
    def _():
        y_ref[...] = jnp.zeros_like(y_ref)


def _experts(block_segment, segment_expert, n_used, xs, w_gate, w_up, w_down):
    m_pad, C = xs.shape
    D = 2 * C
    R, Fd = MOE_ROWS, D_EXPERT
    vmem = 4 * _nbytes((R, C), xs.dtype) + 6 * _nbytes((D, Fd), F32) + 3 * _nbytes((D, Fd), BF16)
    vmem += 6 * _nbytes((R, D), F32)
    hbm = pl.BlockSpec(memory_space=pl.ANY)
    return pl.pallas_call(
        _expert_kernel,
        out_shape=jax.ShapeDtypeStruct((m_pad, C), xs.dtype),
        grid_spec=pltpu.PrefetchScalarGridSpec(
            num_scalar_prefetch=3,
            grid=(m_pad // R,),
            in_specs=[
                pl.BlockSpec((R, C), lambda b, sg, se, nu: (jnp.minimum(b, jnp.maximum(nu[0] - 1, 0)), 0)),
                hbm, hbm, hbm,
            ],
            out_specs=pl.BlockSpec((R, C), lambda b, sg, se, nu: (b, 0)),
            scratch_shapes=[pltpu.VMEM((2, D, Fd), F32), pltpu.VMEM((2, D, Fd), F32), pltpu.VMEM((2, Fd, D), F32),
                            pltpu.VMEM((D, Fd), BF16), pltpu.VMEM((D, Fd), BF16), pltpu.VMEM((Fd, D), BF16),
                            pltpu.SemaphoreType.DMA((2,))],
        ),
        compiler_params=_cparams(1, vmem),
        name="experts",
    )(block_segment, segment_expert, n_used, xs, w_gate, w_up, w_down)


def _combine_kernel(slot_ref, x1_ref, info_ref, yb_ref, p_ref, gp_ref, wpg_ref, wpp_ref, gf_ref, o_ref, ybuf, sem):
    tm, D = x1_ref.shape
    step = pl.program_id(0)
    n_tiles = pl.num_programs(0) - 2
    K = TOPK_IN_GROUP
    GROUPS = 8

    def fetch(tile, group=None):
        slot = tile % 2
        rows = range(tm) if group is None else range(group * tm // GROUPS, (group + 1) * tm // GROUPS)
        for r in rows:
            for k in range(K):
                _row_copy(yb_ref, slot_ref[(tile * tm + r) * K + k], ybuf.at[slot * K + k], r,
                          sem.at[slot]).start(priority=k % 2)

    def drain(tile):
        slot = tile % 2
        for _ in range(tm * K):
            _row_copy(yb_ref, 0, ybuf.at[0], 0, sem.at[slot]).wait()

    @pl.when(step == 0)
    def _():
        fetch(step)

    @pl.when(step > 0)
    def _():
        drain(step - 1)

    @pl.when((step > 0) & (step <= n_tiles))
    def _():
        tile = step - 1
        slot = tile % 2
        info = info_ref[...]
        x2 = (x1_ref[...] + info[:, 2:3] * _unpack_bf16_pairs(ybuf[slot * K])
              + info[:, 3:4] * _unpack_bf16_pairs(ybuf[slot * K + 1]))
        hp = _rms(x2, gp_ref[...]).astype(BF16)
        pp = jnp.dot(p_ref[...].astype(BF16), wpp_ref[...], preferred_element_type=F32)
        cw = D // GROUPS
        zs = []
        for c in range(GROUPS):
            fetch(step, c)
            zs.append(jnp.dot(hp, wpg_ref[:, c * cw:(c + 1) * cw], preferred_element_type=F32))
        x3 = x2 + jax.nn.sigmoid(jnp.concatenate(zs, axis=1)) * pp
        o_ref[...] = _rms(x3, gf_ref[...])


def _combine(slots, x1, info, yb, p, g_ple, w_ple_gate, w_ple_proj, g_final, tm=256):
    T, D = x1.shape
    n_tiles = T // tm
    slots_padded = jnp.concatenate([slots, jnp.zeros((tm * TOPK_IN_GROUP,), I32)])
    vmem = 4 * _nbytes((tm, D), F32) + _nbytes((D, D), BF16) + _nbytes((PLE_DIM, D), BF16)
    vmem += 2 * _nbytes((tm, PLE_DIM), F32) + 2 * TOPK_IN_GROUP * _nbytes((tm, D // 2), yb.dtype)
    vmem += 6 * _nbytes((tm, D), F32)
    tile = lambda i, sl: (jnp.clip(i - 1, 0, n_tiles - 1), 0)
    fixed = lambda i, sl: (0, 0)
    resident = pl.Buffered(1)
    return pl.pallas_call(
        _combine_kernel,
        out_shape=jax.ShapeDtypeStruct((T, D), F32),
        grid_spec=pltpu.PrefetchScalarGridSpec(
            num_scalar_prefetch=1,
            grid=(n_tiles + 2,),
            in_specs=[
                pl.BlockSpec((tm, D), tile),
                pl.BlockSpec((tm, V7X_LANES), tile),
                pl.BlockSpec(memory_space=pl.ANY),
                pl.BlockSpec((tm, PLE_DIM), tile),
                pl.BlockSpec((1, D), fixed),
                pl.BlockSpec((D, D), fixed, pipeline_mode=resident),
                pl.BlockSpec((PLE_DIM, D), fixed, pipeline_mode=resident),
                pl.BlockSpec((1, D), fixed),
            ],
            out_specs=pl.BlockSpec((tm, D), tile),
            scratch_shapes=[pltpu.VMEM((2 * TOPK_IN_GROUP, tm, yb.shape[1]), yb.dtype),
                            pltpu.SemaphoreType.DMA((2,))],
        ),
        compiler_params=_cparams(1, vmem),
        name="combine",
    )(slots_padded, x1, info, yb, p, g_ple.reshape(1, D), w_ple_gate, w_ple_proj, g_final.reshape(1, D))


def _rope_tables(T):
    half, BS = MOBA_HEAD_DIM // 2, MOBA_BLOCK
    inv_freq = ROPE_THETA ** (-jnp.arange(half, dtype=F32) / half)
    ang_a = (jnp.arange(T // BS, dtype=F32) * BS)[:, None, None] * inv_freq
    ang_b = jnp.arange(BS, dtype=F32)[None, :, None] * inv_freq
    ca, sa, cb, sb = jnp.cos(ang_a), jnp.sin(ang_a), jnp.cos(ang_b), jnp.sin(ang_b)
    cos = (ca * cb - sa * sb).reshape(T, half)
    sin = (sa * cb + ca * sb).reshape(T, half)
    return jnp.concatenate([cos, cos], axis=1), jnp.concatenate([-sin, sin], axis=1)


def _mixers(x2d, g_mix, w_in, lb, hgrn_norm_g):
    T = x2d.shape[0]
    W = HGRN_WIDTH
    w = w_in
    cos, sin = _rope_tables(T)
    log_lb = jnp.log(lb).reshape(1, W)
    log_1m = jnp.log1p(-lb).reshape(1, W)
    hq, h = _norm_proj(x2d, g_mix, w, 0 * W, W, _ep_silu, BF16)
    logf = _proj(h, w, 1 * W, W, _ep_logf, F32, col_extras=(log_lb, log_1m))
    hi = _proj(h, w, 2 * W, W, _ep_identity, BF16)
    hog = _proj(h, w, 3 * W, W, _ep_silu, BF16)
    scale = MOBA_HEAD_DIM ** -0.5
    mqt = _proj_t(h, w, 4 * W, functools.partial(_ep_rope, scale=scale), row_extras=(cos, sin))
    blk = jnp.arange(T, dtype=I32)[:, None] // MOBA_BLOCK
    blk_onehot = (blk == jnp.arange(V7X_LANES, dtype=I32)[None, :]).astype(F32)
    mk = _proj(h, w, 5 * W, W, _ep_rope_aug, BF16, row_extras=(cos, sin, blk_onehot), widen=2)
    mvt = _proj_t(h, w, 6 * W, _ep_identity, ones_rows=MOBA_VT_ROWS - MOBA_HEAD_DIM)
    gates = _proj(h, w, 7 * W, 2 * D_MODEL, _ep_sigmoid, BF16)
    o_hgrn = _hgrn(hq, logf, hi, hog, hgrn_norm_g)
    o_moba = _moba(mqt, mk, mvt)
    return o_hgrn, o_moba, gates


def _moe_plan(info, cnt, T):
    R = MOE_ROWS
    counts = cnt[0, :N_EXPERTS].astype(I32)
    padded = (counts + R - 1) // R * R
    pad_end = jnp.cumsum(padded)
    pad_start = pad_end - padded
    slots = _slots(info, pad_start)
    n_blocks = (T * TOPK_IN_GROUP) // R + N_EXPERTS
    block_expert = jnp.minimum(
        jnp.searchsorted(pad_end, jnp.arange(n_blocks, dtype=I32) * R, side="right"), N_EXPERTS - 1).astype(I32)
    has = counts > 0
    segment_expert = jnp.argsort(jnp.logical_not(has), stable=True).astype(I32)
    block_segment = (jnp.cumsum(has.astype(I32))[block_expert] - 1).astype(I32)
    n_used = jnp.stack([pad_end[-1] // R, jnp.sum(has.astype(I32))]).astype(I32)
    return slots, pad_end.astype(I32), block_segment, segment_expert, n_used, n_blocks * R


def kernel(x, p, norm_mix_g, w_in, hgrn_lb_raw, hgrn_norm_g, w_up_hgrn, w_up_moba, w_out, norm_ffn_g,
           w_router_group, w_router_expert, w_exp_gate, w_exp_up, w_exp_down, norm_ple_g, w_ple_gate,
           w_ple_proj, norm_final_g):
    B, T, D = x.shape
    assert B == 1 and D == D_MODEL and w_in.shape[0] == 1 and T % (4 * MOBA_BLOCK) == 0
    lower_bounds = jnp.cumsum(jax.nn.softmax(hgrn_lb_raw.astype(F32), axis=0), axis=0)
    x2d = x.reshape(T, D)
    o_hgrn, o_moba, gates = _mixers(x2d, norm_mix_g[0], w_in[0], lower_bounds[0], hgrn_norm_g[0])
    merged = _merge(o_hgrn, o_moba, gates, w_up_hgrn[0].astype(BF16), w_up_moba[0].astype(BF16))
    w_router = jnp.pad(jnp.concatenate([w_router_group[0], w_router_expert[0]], axis=1),
                       ((0, 0), (0, V7X_LANES - N_GROUPS - N_EXPERTS))).astype(BF16)
    x1, h2p, info, cnt = _outproj(merged, x2d, w_out[0].astype(BF16), norm_ffn_g[0], w_router)
    slots, pad_end, block_segment, segment_expert, n_used, m_pad = _moe_plan(info, cnt, T)
    xs = _dispatch(slots, pad_end, h2p, m_pad)
    yb = _experts(block_segment, segment_expert, n_used, xs, w_exp_gate[0], w_exp_up[0], w_exp_down[0])
    out = _combine(slots, x1, info, yb, p[0].reshape(T, PLE_DIM), norm_ple_g[0], w_ple_gate[0].astype(BF16),
                   w_ple_proj[0].astype(BF16), norm_final_g)
    return out.reshape(B, T, D)
```

```python
import functools

import jax
import jax.numpy as jnp
from jax import lax
from jax.experimental import pallas as pl
from jax.experimental.pallas import tpu as pltpu

F32 = jnp.float32
BF16 = jnp.bfloat16
I32 = jnp.int32

D_MODEL = 2048
PLE_DIM = 256
HGRN_HEADS = 8
HGRN_HEAD_DIM = 128
HGRN_WIDTH = HGRN_HEADS * HGRN_HEAD_DIM
MOBA_HEADS = 8
MOBA_HEAD_DIM = 128
MOBA_WIDTH = MOBA_HEADS * MOBA_HEAD_DIM
MOBA_BLOCK = 256
MOBA_TOPK = 3
ROPE_THETA = 10000.0
N_GROUPS = 4
EXPERTS_PER_GROUP = 8
N_EXPERTS = N_GROUPS * EXPERTS_PER_GROUP
TOPK_IN_GROUP = 2
D_EXPERT = 512
EPS = 1e-6
NEG_INF = -1e30

V7X_LANES = 128
V7X_SUBLANES = 8
V7X_VMEM_BUDGET_BYTES = 56 * 1024 * 1024

HGRN_CHUNK = 128
MOE_ROWS = 256


def _cparams(n_grid, vmem_bytes):
    return pltpu.CompilerParams(
        dimension_semantics=("arbitrary",) * n_grid,
        vmem_limit_bytes=int(min(max(vmem_bytes, 16 * 1024 * 1024), V7X_VMEM_BUDGET_BYTES)),
    )


def _nbytes(shape, dtype):
    n = 1
    for s in shape:
        n *= s
    return n * jnp.dtype(dtype).itemsize


def _rms(x, g):
    ms = jnp.mean(x * x, axis=-1, keepdims=True)
    return x * lax.rsqrt(ms + EPS) * g


def _ep_identity(acc):
    return acc


def _ep_silu(acc):
    return acc * jax.nn.sigmoid(acc)


def _ep_sigmoid(acc):
    return jax.nn.sigmoid(acc)


def _ep_logf(acc, la_ref, lc_ref):
    ls = jnp.minimum(acc, 0.0) - jnp.log(1.0 + jnp.exp(-jnp.abs(acc)))
    u = la_ref[...]
    v = lc_ref[...] + ls
    return jnp.maximum(u, v) + jnp.log(1.0 + jnp.exp(-jnp.abs(u - v)))


def _ep_rope(acc, cos_ref, sin_ref, *, scale):
    cos = cos_ref[...]
    sin = sin_ref[...]
    outs = []
    for hh in range(acc.shape[1] // MOBA_HEAD_DIM):
        a = acc[:, hh * MOBA_HEAD_DIM:(hh + 1) * MOBA_HEAD_DIM]
        r = pltpu.roll(a, MOBA_HEAD_DIM // 2, axis=1)
        outs.append((a * cos + r * sin) * scale)
    return jnp.concatenate(outs, axis=1)


def _ep_rope_aug(acc, cos_ref, sin_ref, oh_ref):
    cos = cos_ref[...]
    sin = sin_ref[...]
    oh = oh_ref[...]
    outs = []
    for hh in range(acc.shape[1] // MOBA_HEAD_DIM):
        a = acc[:, hh * MOBA_HEAD_DIM:(hh + 1) * MOBA_HEAD_DIM]
        outs.append(a * cos + pltpu.roll(a, MOBA_HEAD_DIM // 2, axis=1) * sin)
        outs.append(oh)
    return jnp.concatenate(outs, axis=1)


def _cast_weight_once(w_ref, wb_ref, row_axis):
    @pl.when(pl.program_id(row_axis) == 0)
    def _():
        wb_ref[...] = w_ref[...].astype(wb_ref.dtype)


def _proj_kernel(h_ref, w_ref, *refs, epilogue):
    *extra, o_ref, wb_ref = refs
    _cast_weight_once(w_ref, wb_ref, 1)
    acc = jnp.dot(h_ref[...], wb_ref[...], preferred_element_type=F32)
    o_ref[...] = epilogue(acc, *extra).astype(o_ref.dtype)


def _proj(h, w, col0, ncols, epilogue, out_dtype, row_extras=(), col_extras=(), tm=1024, tn=1024, widen=1):
    T, K = h.shape
    tn = min(tn, ncols)
    tm = min(tm, T)
    cb = col0 // tn
    otn = widen * tn
    in_specs = [
        pl.BlockSpec((tm, K), lambda j, i: (i, 0)),
        pl.BlockSpec((K, tn), lambda j, i: (0, cb + j)),
    ]
    for e in row_extras:
        in_specs.append(pl.BlockSpec((tm, e.shape[1]), lambda j, i: (i, 0)))
    for e in col_extras:
        in_specs.append(pl.BlockSpec((1, tn), lambda j, i: (0, j)))
    vmem = 2 * (_nbytes((tm, K), h.dtype) + _nbytes((K, tn), w.dtype) + _nbytes((tm, otn), out_dtype))
    vmem += 3 * _nbytes((tm, otn), F32) + _nbytes((K, tn), BF16)
    return pl.pallas_call(
        functools.partial(_proj_kernel, epilogue=epilogue),
        out_shape=jax.ShapeDtypeStruct((T, widen * ncols), out_dtype),
        grid=(ncols // tn, T // tm),
        in_specs=in_specs,
        out_specs=pl.BlockSpec((tm, otn), lambda j, i: (i, j)),
        scratch_shapes=[pltpu.VMEM((K, tn), BF16)],
        compiler_params=_cparams(2, vmem),
        name="proj",
    )(h, w, *row_extras, *col_extras)


def _norm_proj_kernel(x_ref, g_ref, w_ref, o_ref, h_ref, wb_ref, *, epilogue):
    _cast_weight_once(w_ref, wb_ref, 0)
    h = _rms(x_ref[...], g_ref[...]).astype(h_ref.dtype)
    h_ref[...] = h
    o_ref[...] = epilogue(jnp.dot(h, wb_ref[...], preferred_element_type=F32)).astype(o_ref.dtype)


def _norm_proj(x, g, w, col0, ncols, epilogue, out_dtype, tm=512):
    T, K = x.shape
    cb = col0 // ncols
    vmem = 2 * (_nbytes((tm, K), F32) + _nbytes((tm, ncols), out_dtype) + _nbytes((tm, K), BF16))
    vmem += _nbytes((K, ncols), F32) + _nbytes((K, ncols), BF16) + 2 * _nbytes((tm, K), F32) + 3 * _nbytes((tm, ncols), F32)
    return pl.pallas_call(
        functools.partial(_norm_proj_kernel, epilogue=epilogue),
        out_shape=(jax.ShapeDtypeStruct((T, ncols), out_dtype), jax.ShapeDtypeStruct((T, K), BF16)),
        grid=(T // tm,),
        in_specs=[
            pl.BlockSpec((tm, K), lambda i: (i, 0)),
            pl.BlockSpec((1, K), lambda i: (0, 0)),
            pl.BlockSpec((K, ncols), lambda i: (0, cb), pipeline_mode=pl.Buffered(1)),
        ],
        out_specs=(pl.BlockSpec((tm, ncols), lambda i: (i, 0)), pl.BlockSpec((tm, K), lambda i: (i, 0))),
        scratch_shapes=[pltpu.VMEM((K, ncols), BF16)],
        compiler_params=_cparams(1, vmem),
        name="norm_proj",
    )(x, g.reshape(1, K), w)


MOBA_VT_ROWS = MOBA_HEAD_DIM + 16
MOBA_BLOCKS_PER_STEP = 4


def _proj_t_kernel(h_ref, w_ref, *refs, epilogue, ones_rows):
    *extra, o_ref, wb_ref = refs
    BS, HD = MOBA_BLOCK, MOBA_HEAD_DIM
    _cast_weight_once(w_ref, wb_ref, 0)
    acc = epilogue(jnp.dot(h_ref[...], wb_ref[...], preferred_element_type=F32), *extra)
    ones = jnp.ones((ones_rows, BS), F32) if ones_rows else None
    for b in range(acc.shape[0] // BS):
        parts = []
        for hh in range(acc.shape[1] // HD):
            parts.append(acc[b * BS:(b + 1) * BS, hh * HD:(hh + 1) * HD].T)
            if ones_rows:
                parts.append(ones)
        o_ref[b] = jnp.concatenate(parts, axis=0).astype(o_ref.dtype)


def _proj_t(h, w, col0, epilogue, row_extras=(), ones_rows=0, tm=1024):
    T, K = h.shape
    tn = MOBA_WIDTH
    tm = min(tm, T)
    cb = col0 // tn
    rows = MOBA_HEADS * (MOBA_HEAD_DIM + ones_rows)
    in_specs = [pl.BlockSpec((tm, K), lambda i: (i, 0)), pl.BlockSpec((K, tn), lambda i: (0, cb))]
    for e in row_extras:
        in_specs.append(pl.BlockSpec((tm, e.shape[1]), lambda i: (i, 0)))
    vmem = 2 * (_nbytes((tm, K), h.dtype) + _nbytes((K, tn), w.dtype) + _nbytes((tm, 2 * tn), BF16))
    vmem += 4 * _nbytes((tm, tn), F32) + _nbytes((K, tn), BF16)
    return pl.pallas_call(
        functools.partial(_proj_t_kernel, epilogue=epilogue, ones_rows=ones_rows),
        out_shape=jax.ShapeDtypeStruct((T // MOBA_BLOCK, rows, MOBA_BLOCK), BF16),
        grid=(T // tm,),
        in_specs=in_specs,
        out_specs=pl.BlockSpec((tm // MOBA_BLOCK, rows, MOBA_BLOCK), lambda i: (i, 0, 0)),
        scratch_shapes=[pltpu.VMEM((K, tn), BF16)],
        compiler_params=_cparams(1, vmem),
        name="proj_t",
    )(h, w, *row_extras)


def _hgrn_kernel(q_ref, g_ref, v_ref, og_ref, ng_ref, o_ref, st_ref, code_ref):
    W, HD, NH, C, S = HGRN_WIDTH, HGRN_HEAD_DIM, HGRN_HEADS, HGRN_CHUNK, V7X_SUBLANES
    J = C // S

    @pl.when(pl.program_id(0) == 0)
    def _():
        st_ref[...] = jnp.zeros_like(st_ref)
        tr = lax.broadcasted_iota(I32, (C, C), 0)
        tc = lax.broadcasted_iota(I32, (C, C), 1)
        xr = tr ^ tc
        code = jnp.zeros((C, C), I32)
        for lvl in range(1, 8):
            code = jnp.where(xr >= (1 << (lvl - 1)), lvl, code)
        code_ref[...] = jnp.where(tc > tr, -1, code)

    def r3(x):
        return x.astype(F32).reshape(J, S, W)

    def sub_bcast(x3, r):
        return jnp.broadcast_to(x3[:, r:r + 1, :], x3.shape)

    g3, q3, v3 = r3(g_ref[...]), r3(q_ref[...]), r3(v_ref[...])
    sub = lax.broadcasted_iota(I32, (1, S, W), 1)

    c3 = g3
    for s in (1, 2, 4):
        c3 = c3 + jnp.where(sub >= s, pltpu.roll(c3, s, axis=1), 0.0)
    run = jnp.zeros((1, 1, W), F32)
    carry = []
    for j in range(J):
        carry.append(run)
        run = run + c3[j:j + 1, S - 1:S, :]
    b3 = c3 + jnp.concatenate(carry, axis=0)
    bC = run

    k3 = 1.0 - jnp.exp(g3)
    qe3 = q3 * jnp.exp(b3)
    ks3 = k3 * jnp.exp(bC - b3)

    levels = [(0, q3, k3)]
    ref1 = jnp.where(sub % 2 == 0, b3, pltpu.roll(b3, 1, axis=1))
    ref2 = jnp.where(sub < 4, sub_bcast(b3, 1), sub_bcast(b3, 5))
    ref4 = sub_bcast(b3, 3)
    for lvl, (ref, upper) in enumerate(((ref1, sub % 2 == 1), (ref2, sub % 4 >= 2), (ref4, sub >= 4)), start=1):
        e = jnp.exp(-jnp.abs(b3 - ref))
        levels.append((lvl, jnp.where(upper, q3 * e, 0.0), jnp.where(upper, 0.0, k3 * e)))
    zero_group = jnp.zeros((1, S, W), F32)
    for lvl, half in enumerate((1, 2, 4, 8), start=4):
        qparts, kparts = [], []
        for j in range(J):
            jr = (j // (2 * half)) * (2 * half) + half - 1
            ref = b3[jr:jr + 1, S - 1:S, :]
            if (j % (2 * half)) >= half:
                qparts.append(q3[j:j + 1] * jnp.exp(b3[j:j + 1] - ref))
                kparts.append(zero_group)
            else:
                qparts.append(zero_group)
                kparts.append(k3[j:j + 1] * jnp.exp(ref - b3[j:j + 1]))
        levels.append((lvl, jnp.concatenate(qparts, axis=0), jnp.concatenate(kparts, axis=0)))

    code = code_ref[...]

    def mat(x3, h):
        return x3.reshape(C, W)[:, h * HD:(h + 1) * HD].astype(BF16)

    nt = (((1,), (1,)), ((), ()))
    tn = (((0,), (0,)), ((), ()))
    ebc = jnp.exp(bC).reshape(1, W)
    ng = ng_ref[...]
    for h in range(NH):
        a_mat = jnp.zeros((C, C), F32)
        for lvl, qr, kr in levels:
            s = lax.dot_general(mat(qr, h), mat(kr, h), nt, preferred_element_type=F32)
            a_mat = jnp.where(code == lvl, s, a_mat)
        vh = mat(v3, h)
        st = st_ref[h]
        o = jnp.dot(a_mat.astype(BF16), vh, preferred_element_type=F32)
        o = o + lax.dot_general(mat(qe3, h), st.astype(BF16), nt, preferred_element_type=F32)
        o = _rms(o, ng) * og_ref[:, h * HD:(h + 1) * HD].astype(F32)
        o_ref[:, h * HD:(h + 1) * HD] = o.astype(o_ref.dtype)
        st_ref[h] = st * ebc[:, h * HD:(h + 1) * HD] + lax.dot_general(
            vh, mat(ks3, h), tn, preferred_element_type=F32)


def _hgrn(q, logf, v, og, norm_g):
    T, W = q.shape
    C = HGRN_CHUNK
    blk = pl.BlockSpec((C, W), lambda c: (c, 0))
    vmem = 64 * _nbytes((C, W), F32)
    return pl.pallas_call(
        _hgrn_kernel,
        out_shape=jax.ShapeDtypeStruct((T, W), BF16),
        grid=(T // C,),
        in_specs=[blk, blk, blk, blk, pl.BlockSpec((1, HGRN_HEAD_DIM), lambda c: (0, 0))],
        out_specs=blk,
        scratch_shapes=[pltpu.VMEM((HGRN_HEADS, HGRN_HEAD_DIM, HGRN_HEAD_DIM), F32), pltpu.VMEM((C, C), I32)],
        compiler_params=_cparams(1, vmem),
        name="hgrn",
    )(q, logf, v, og, norm_g.reshape(1, HGRN_HEAD_DIM))


def _moba_kernel(qt_ref, k_ref, vt_ref, o_ref, km_ref):
    BS, HD, VR = MOBA_BLOCK, MOBA_HEAD_DIM, MOBA_VT_ROWS
    T = k_ref.shape[0]
    NB = T // BS
    G = o_ref.shape[1] // HD
    cur = pl.program_id(1)

    @pl.when(cur == 0)
    def _():
        for g in range(G):
            kf = k_ref[:, 2 * g * HD:(2 * g + 1) * HD].astype(F32).reshape(NB, BS, HD)
            km_ref[g] = jnp.sum(kf, axis=1) * (1.0 / BS)

    blk = lax.broadcasted_iota(I32, (NB, BS), 0)
    pad = jnp.zeros((V7X_LANES - NB, BS), F32)
    qts = [qt_ref[0, g * HD:(g + 1) * HD, :] for g in range(G)]
    gts = [jnp.dot(km_ref[g].astype(BF16), qts[g], preferred_element_type=F32) for g in range(G)]
    qcs = []
    for g in range(G):
        gt = jnp.where(blk < cur, gts[g], NEG_INF)
        sel = blk == cur
        for _ in range(MOBA_TOPK):
            mx = jnp.max(gt, axis=0, keepdims=True)
            idx = jnp.min(jnp.where(gt == mx, blk, NB), axis=0, keepdims=True)
            pick = (blk == idx) & (mx > 0.5 * NEG_INF)
            sel = sel | pick
            gt = jnp.where(pick, NEG_INF, gt)
        pen = jnp.concatenate([jnp.where(sel, 0.0, NEG_INF), pad], axis=0).astype(BF16)
        qcs.append(jnp.concatenate([qts[g], pen], axis=0))

    KB = MOBA_BLOCKS_PER_STEP

    def body(blk0, carry, own_step=False, nblk=KB):
        ms, accs = carry
        r = pl.multiple_of(blk0 * BS, BS)
        sns = [jnp.dot(k_ref[pl.ds(r, nblk * BS), 2 * g * HD:(2 * g + 2) * HD], qcs[g], preferred_element_type=F32)
               for g in range(G)]
        if own_step:
            krow = lax.broadcasted_iota(I32, (nblk * BS, BS), 0)
            qcol = lax.broadcasted_iota(I32, (nblk * BS, BS), 1)
            keep = (blk0 + krow // BS != cur) | (krow % BS <= qcol)
            sns = [jnp.where(keep, s, NEG_INF) for s in sns]
        new_ms, alphas, pns = [], [], []
        for g in range(G):
            m_new = jnp.maximum(ms[g], jnp.max(sns[g], axis=0, keepdims=True))
            alphas.append(jnp.exp(ms[g] - m_new))
            pns.append(jnp.exp(sns[g] - m_new).astype(BF16))
            new_ms.append(m_new)
        new_accs = []
        for g in range(G):
            pv = alphas[g] * accs[g]
            for j in range(nblk):
                pv = pv + jnp.dot(vt_ref[blk0 + j, g * VR:(g + 1) * VR, :], pns[g][j * BS:(j + 1) * BS],
                                  preferred_element_type=F32)
            new_accs.append(pv)
        return tuple(new_ms), tuple(new_accs)

    ms = tuple(jnp.full((1, BS), NEG_INF, F32) for _ in range(G))
    accs = tuple(jnp.zeros((VR, BS), F32) for _ in range(G))
    carry = lax.fori_loop(0, cur // (2 * KB), lambda c, cr: body(c * 2 * KB, cr, nblk=2 * KB), (ms, accs))
    last = (cur // KB) * KB
    carry = lax.cond(cur % (2 * KB) >= KB, lambda cr: body(last - KB, cr), lambda cr: cr, carry)
    _, accs = lax.cond(
        cur % KB < KB // 2,
        lambda cr: body(last, cr, own_step=True, nblk=KB // 2),
        lambda cr: body(last, cr, own_step=True),
        carry)
    for g in range(G):
        ot = accs[g][:HD, :] / accs[g][HD:HD + 1, :]
        o_ref[:, g * HD:(g + 1) * HD] = ot.T.astype(o_ref.dtype)


def _moba(mqt, mk_aug, mvt, heads_per_step=4):
    T = mk_aug.shape[0]
    BS, HD, G, VR = MOBA_BLOCK, MOBA_HEAD_DIM, heads_per_step, MOBA_VT_ROWS
    NB = T // BS
    vmem = _nbytes((T, 2 * G * HD), BF16) + _nbytes((NB, G * VR, BS), BF16) + 8 * _nbytes((BS, G * HD), BF16)
    vmem += 16 * G * _nbytes((BS, BS), F32)
    resident = pl.Buffered(1)
    return pl.pallas_call(
        _moba_kernel,
        out_shape=jax.ShapeDtypeStruct((T, MOBA_WIDTH), BF16),
        grid=(MOBA_HEADS // G, NB),
        in_specs=[
            pl.BlockSpec((1, G * HD, BS), lambda h, i: (i, h, 0)),
            pl.BlockSpec((T, 2 * G * HD), lambda h, i: (0, h), pipeline_mode=resident),
            pl.BlockSpec((NB, G * VR, BS), lambda h, i: (0, h, 0), pipeline_mode=resident),
        ],
        out_specs=pl.BlockSpec((BS, G * HD), lambda h, i: (i, h)),
        scratch_shapes=[pltpu.VMEM((G, NB, HD), F32)],
        compiler_params=_cparams(2, vmem),
        name="moba",
    )(mqt, mk_aug, mvt)


def _merge_kernel(oh_ref, om_ref, ga_ref, gb_ref, wh_ref, wm_ref, o_ref):
    a = jnp.dot(oh_ref[...], wh_ref[...], preferred_element_type=F32)
    b = jnp.dot(om_ref[...], wm_ref[...], preferred_element_type=F32)
    o_ref[...] = (ga_ref[...].astype(F32) * a + gb_ref[...].astype(F32) * b).astype(o_ref.dtype)


def _merge(o_hgrn, o_moba, gates, w_up_hgrn, w_up_moba, tm=512):
    T = o_hgrn.shape[0]
    D = D_MODEL
    vmem = 2 * (_nbytes((tm, HGRN_WIDTH), F32) + _nbytes((tm, MOBA_WIDTH), BF16) + 3 * _nbytes((tm, D), BF16)
                + 2 * _nbytes((HGRN_WIDTH, D), BF16)) + 3 * _nbytes((tm, D), F32)
    return pl.pallas_call(
        _merge_kernel,
        out_shape=jax.ShapeDtypeStruct((T, D), BF16),
        grid=(T // tm,),
        in_specs=[
            pl.BlockSpec((tm, HGRN_WIDTH), lambda i: (i, 0)),
            pl.BlockSpec((tm, MOBA_WIDTH), lambda i: (i, 0)),
            pl.BlockSpec((tm, D), lambda i: (i, 0)),
            pl.BlockSpec((tm, D), lambda i: (i, 1)),
            pl.BlockSpec((HGRN_WIDTH, D), lambda i: (0, 0)),
            pl.BlockSpec((MOBA_WIDTH, D), lambda i: (0, 0)),
        ],
        out_specs=pl.BlockSpec((tm, D), lambda i: (i, 0)),
        compiler_params=_cparams(1, vmem),
        name="merge",
    )(o_hgrn, o_moba, gates, gates, w_up_hgrn, w_up_moba)


def _pack_bf16_pairs(x):
    C = x.shape[1] // 2
    b = lax.bitcast_convert_type(x, jnp.uint32)
    r = (b + jnp.uint32(0x7FFF) + ((b >> 16) & jnp.uint32(1))) >> 16
    return r[:, :C] | (r[:, C:] << 16)


def _unpack_bf16_pairs(p):
    lo = lax.bitcast_convert_type(p << 16, F32)
    hi = lax.bitcast_convert_type(p & jnp.uint32(0xFFFF0000), F32)
    return jnp.concatenate([lo, hi], axis=1)


def _outproj_kernel(m_ref, x_ref, w_ref, g_ref, wr_ref, x1_ref, h2p_ref, info_ref, cnt_ref, prev_ref, carry_ref):
    step = pl.program_id(0)
    n_tiles = pl.num_programs(0) - 1

    @pl.when(step == 0)
    def _():
        carry_ref[...] = jnp.zeros_like(carry_ref)

    def finish_previous():
        h2 = _rms(prev_ref[...], g_ref[...])
        h2p_ref[...] = _pack_bf16_pairs(h2)
        info_ref[...] = _route(h2, wr_ref, carry_ref)
        cnt_ref[...] = carry_ref[...]

    def project():
        return x_ref[...] + jnp.dot(m_ref[...], w_ref[...], preferred_element_type=F32)

    @pl.when(step == 0)
    def _():
        x1 = project()
        x1_ref[...] = x1
        prev_ref[...] = x1

    @pl.when((step > 0) & (step < n_tiles))
    def _():
        x1 = project()
        finish_previous()
        x1_ref[...] = x1
        prev_ref[...] = x1

    @pl.when(step == n_tiles)
    def _():
        finish_previous()


def _outproj(merged, x, w_out, g_ffn, w_router, tm=512):
    T, D = x.shape
    n_tiles = T // tm
    vmem = 2 * (_nbytes((tm, D), BF16) + 3 * _nbytes((tm, D), F32) + _nbytes((D, D), BF16)) + 7 * _nbytes((tm, D), F32)
    cur = lambda i: (jnp.minimum(i, n_tiles - 1), 0)
    prev = lambda i: (jnp.maximum(i - 1, 0), 0)
    fixed = lambda i: (0, 0)
    return pl.pallas_call(
        _outproj_kernel,
        out_shape=(jax.ShapeDtypeStruct((T, D), F32), jax.ShapeDtypeStruct((T, D // 2), jnp.uint32),
                   jax.ShapeDtypeStruct((T, V7X_LANES), F32), jax.ShapeDtypeStruct((1, V7X_LANES), F32)),
        grid=(n_tiles + 1,),
        in_specs=[
            pl.BlockSpec((tm, D), cur),
            pl.BlockSpec((tm, D), cur),
            pl.BlockSpec((D, D), fixed, pipeline_mode=pl.Buffered(1)),
            pl.BlockSpec((1, D), fixed),
            pl.BlockSpec((D, V7X_LANES), fixed),
        ],
        out_specs=(pl.BlockSpec((tm, D), cur), pl.BlockSpec((tm, D // 2), prev), pl.BlockSpec((tm, V7X_LANES), prev),
                   pl.BlockSpec((1, V7X_LANES), fixed)),
        scratch_shapes=[pltpu.VMEM((tm, D), F32), pltpu.VMEM((1, V7X_LANES), F32)],
        compiler_params=_cparams(1, vmem),
        name="outproj",
    )(merged, x, w_out, g_ffn.reshape(1, D), w_router)


def _route(h2, w_ref, carry_ref):
    tm = h2.shape[0]
    logits = jnp.dot(h2.astype(BF16), w_ref[...], preferred_element_type=F32)
    lane = lax.broadcasted_iota(I32, (tm, V7X_LANES), 1)
    is_g = lane < N_GROUPS
    gl = jnp.where(is_g, logits, NEG_INF)
    gmax = jnp.max(gl, axis=1, keepdims=True)
    g_sel = jnp.min(jnp.where(gl == gmax, lane, V7X_LANES), axis=1, keepdims=True)
    gsum = jnp.sum(jnp.where(is_g, jnp.exp(gl - gmax), 0.0), axis=1, keepdims=True)
    p_group = 1.0 / gsum
    lo = N_GROUPS + EXPERTS_PER_GROUP * g_sel
    emask = (lane >= lo) & (lane < lo + EXPERTS_PER_GROUP)
    el = jnp.where(emask, logits, NEG_INF)
    e1 = jnp.max(el, axis=1, keepdims=True)
    i1 = jnp.min(jnp.where((el == e1) & emask, lane, V7X_LANES), axis=1, keepdims=True)
    emask2 = emask & (lane != i1)
    el2 = jnp.where(emask2, logits, NEG_INF)
    e2 = jnp.max(el2, axis=1, keepdims=True)
    i2 = jnp.min(jnp.where((el2 == e2) & emask2, lane, V7X_LANES), axis=1, keepdims=True)
    r = jnp.exp(e2 - e1)
    w1 = p_group / (1.0 + r)
    w2 = p_group * r / (1.0 + r)
    eid1 = i1 - N_GROUPS
    eid2 = i2 - N_GROUPS
    oh1 = jnp.where(lane == eid1, 1.0, 0.0)
    oh2 = jnp.where(lane == eid2, 1.0, 0.0)
    cnt = oh1 + oh2
    tri = jnp.where(lax.broadcasted_iota(I32, (tm, tm), 0) > lax.broadcasted_iota(I32, (tm, tm), 1), 1.0, 0.0)
    before = jnp.dot(tri.astype(BF16), cnt.astype(BF16), preferred_element_type=F32) + carry_ref[...]
    rank1 = jnp.sum(oh1 * before, axis=1, keepdims=True)
    rank2 = jnp.sum(oh2 * before, axis=1, keepdims=True)
    carry_ref[...] = carry_ref[...] + jnp.sum(cnt, axis=0, keepdims=True)
    info = jnp.zeros((tm, V7X_LANES), F32)
    for k, val in enumerate((eid1.astype(F32), eid2.astype(F32), w1, w2, rank1, rank2)):
        info = jnp.where(lane == k, val, info)
    return info


def _row_copy(src_ref, src_row, dst_ref, dst_row, sem):
    return pltpu.make_async_copy(src_ref.at[pl.ds(src_row, 1), :], dst_ref.at[pl.ds(dst_row, 1), :], sem)


ROW_DMA_UNROLL = 8


def _slots_kernel(info_ref, ps_ref, o_ref):
    info = info_ref[...]
    lane = lax.broadcasted_iota(I32, info.shape, 1)
    lane_f = lane.astype(F32)
    ps = ps_ref[...]
    out = jnp.zeros(info.shape, F32)
    for k in range(TOPK_IN_GROUP):
        start = jnp.sum(jnp.where(lane_f == info[:, k:k + 1], ps, 0.0), axis=1, keepdims=True)
        out = jnp.where(lane == k, start + info[:, 4 + k:5 + k], out)
    o_ref[...] = out.astype(I32)


def _slots(info, pad_start, tm=1024):
    T = info.shape[0]
    ps = jnp.pad(pad_start.astype(F32), (0, V7X_LANES - N_EXPERTS)).reshape(1, V7X_LANES)
    out = pl.pallas_call(
        _slots_kernel,
        out_shape=jax.ShapeDtypeStruct((T, V7X_LANES), I32),
        grid=(T // tm,),
        in_specs=[pl.BlockSpec((tm, V7X_LANES), lambda i: (i, 0)), pl.BlockSpec((1, V7X_LANES), lambda i: (0, 0))],
        out_specs=pl.BlockSpec((tm, V7X_LANES), lambda i: (i, 0)),
        compiler_params=_cparams(1, 16 * _nbytes((tm, V7X_LANES), F32)),
        name="slots",
    )(info, ps)
    return out[:, :TOPK_IN_GROUP].reshape(-1)


def _dispatch_kernel(slot_ref, pend_ref, h_ref, xs_ref, zero_ref, sem):
    tm = h_ref.shape[0]
    step = pl.program_id(0)
    base = step * tm

    @pl.when(step == 0)
    def _():
        zero_ref[...] = jnp.zeros_like(zero_ref)

        def tail(e):
            return pltpu.make_async_copy(
                zero_ref, xs_ref.at[pl.ds(pl.multiple_of(pend_ref[e] - MOE_ROWS, MOE_ROWS), MOE_ROWS), :], sem)

        def nonempty(e):
            return pend_ref[e] > (pend_ref[e - 1] if e else 0)

        def unused(b):
            return pltpu.make_async_copy(
                zero_ref, xs_ref.at[pl.ds(pl.multiple_of(b * MOE_ROWS, MOE_ROWS), MOE_ROWS), :], sem)

        first_unused = pend_ref[N_EXPERTS - 1] // MOE_ROWS
        n_blocks = xs_ref.shape[0] // MOE_ROWS
        for e in range(N_EXPERTS):
            pl.when(nonempty(e))(lambda e=e: tail(e).start())
        lax.fori_loop(first_unused, n_blocks, lambda b, c: (unused(b).start(), c)[1], 0)
        for e in range(N_EXPERTS):
            pl.when(nonempty(e))(lambda e=e: tail(e).wait())
        lax.fori_loop(first_unused, n_blocks, lambda b, c: (unused(b).wait(), c)[1], 0)

    def issue(rb, c):
        for u in range(ROW_DMA_UNROLL):
            r = rb * ROW_DMA_UNROLL + u
            for k in range(TOPK_IN_GROUP):
                _row_copy(h_ref, r, xs_ref, slot_ref[(base + r) * TOPK_IN_GROUP + k], sem).start(priority=k % 2)
        return c

    lax.fori_loop(0, tm // ROW_DMA_UNROLL, issue, 0)
    for _ in range(tm * TOPK_IN_GROUP):
        _row_copy(h_ref, 0, xs_ref, 0, sem).wait()


def _dispatch(slots, pad_end, h2p, m_pad, tm=256):
    T, C = h2p.shape
    return pl.pallas_call(
        _dispatch_kernel,
        out_shape=jax.ShapeDtypeStruct((m_pad, C), h2p.dtype),
        grid_spec=pltpu.PrefetchScalarGridSpec(
            num_scalar_prefetch=2,
            grid=(T // tm,),
            in_specs=[pl.BlockSpec((tm, C), lambda i, sl, pe: (i, 0))],
            out_specs=pl.BlockSpec(memory_space=pl.ANY),
            scratch_shapes=[pltpu.VMEM((MOE_ROWS, C), h2p.dtype), pltpu.SemaphoreType.DMA(())],
        ),
        compiler_params=_cparams(1, 6 * _nbytes((tm, C), h2p.dtype)),
        name="dispatch",
    )(slots, pad_end, h2p)


def _expert_kernel(seg_ref, sexp_ref, nu_ref, xs_ref, wg_ref, wu_ref, wd_ref, y_ref, wgf_ref, wuf_ref, wdf_ref,
                   wgb_ref, wub_ref, wdb_ref, sem):
    b = pl.program_id(0)
    used = b < nu_ref[0]
    seg = seg_ref[b]
    first = used & ((b == 0) | (seg != seg_ref[jnp.maximum(b - 1, 0)]))

    def weights_copy(s, part):
        slot = s % 2
        src, dst = ((wg_ref, wgf_ref), (wu_ref, wuf_ref), (wd_ref, wdf_ref))[part]
        return pltpu.make_async_copy(src.at[sexp_ref[s]], dst.at[slot], sem.at[slot])

    @pl.when(b == 0)
    def _():
        for part in range(3):
            weights_copy(0, part).start()

    @pl.when(first)
    def _():
        for part in range(3):
            weights_copy(seg, part).wait()
        slot = seg % 2
        wgb_ref[...] = wgf_ref[slot].astype(BF16)
        wub_ref[...] = wuf_ref[slot].astype(BF16)
        wdb_ref[...] = wdf_ref[slot].astype(BF16)

    @pl.when(first & (seg + 1 < nu_ref[1]))
    def _():
        for part in range(3):
            weights_copy(seg + 1, part).start()

    @pl.when(used)
    def _():
        x = _unpack_bf16_pairs(xs_ref[...]).astype(BF16)
        a = jnp.dot(x, wgb_ref[...], preferred_element_type=F32)
        u = jnp.dot(x, wub_ref[...], preferred_element_type=F32)
        hm = (a * jax.nn.sigmoid(a) * u).astype(BF16)
        y_ref[...] = _pack_bf16_pairs(jnp.dot(hm, wdb_ref[...], preferred_element_type=F32))

    @pl.when(jnp.logical_not(used))
    def _():
        y_ref[...] = jnp.zeros_like(y_ref)


def _experts(block_segment, segment_expert, n_used, xs, w_gate, w_up, w_down):
    m_pad, C = xs.shape
    D = 2 * C
    R, Fd = MOE_ROWS, D_EXPERT
    vmem = 4 * _nbytes((R, C), xs.dtype) + 6 * _nbytes((D, Fd), F32) + 3 * _nbytes((D, Fd), BF16)
    vmem += 6 * _nbytes((R, D), F32)
    hbm = pl.BlockSpec(memory_space=pl.ANY)
    return pl.pallas_call(
        _expert_kernel,
        out_shape=jax.ShapeDtypeStruct((m_pad, C), xs.dtype),
        grid_spec=pltpu.PrefetchScalarGridSpec(
            num_scalar_prefetch=3,
            grid=(m_pad // R,),
            in_specs=[
                pl.BlockSpec((R, C), lambda b, sg, se, nu: (jnp.minimum(b, jnp.maximum(nu[0] - 1, 0)), 0)),
                hbm, hbm, hbm,
            ],
            out_specs=pl.BlockSpec((R, C), lambda b, sg, se, nu: (b, 0)),
            scratch_shapes=[pltpu.VMEM((2, D, Fd), F32), pltpu.VMEM((2, D, Fd), F32), pltpu.VMEM((2, Fd, D), F32),
                            pltpu.VMEM((D, Fd), BF16), pltpu.VMEM((D, Fd), BF16), pltpu.VMEM((Fd, D), BF16),
                            pltpu.SemaphoreType.DMA((2,))],
        ),
        compiler_params=_cparams(1, vmem),
        name="experts",
    )(block_segment, segment_expert, n_used, xs, w_gate, w_up, w_down)


def _combine_kernel(slot_ref, x1_ref, info_ref, yb_ref, p_ref, gp_ref, wpg_ref, wpp_ref, gf_ref, o_ref, ybuf, sem):
    tm, D = x1_ref.shape
    step = pl.program_id(0)
    n_tiles = pl.num_programs(0) - 2
    K = TOPK_IN_GROUP
    GROUPS = 8

    def fetch(tile, group=None):
        slot = tile % 2
        rows = range(tm) if group is None else range(group * tm // GROUPS, (group + 1) * tm // GROUPS)
        for r in rows:
            for k in range(K):
                _row_copy(yb_ref, slot_ref[(tile * tm + r) * K + k], ybuf.at[slot * K + k], r,
                          sem.at[slot]).start(priority=k % 2)

    def drain(tile):
        slot = tile % 2
        for _ in range(tm * K):
            _row_copy(yb_ref, 0, ybuf.at[0], 0, sem.at[slot]).wait()

    @pl.when(step == 0)
    def _():
        fetch(step)

    @pl.when(step > 0)
    def _():
        drain(step - 1)

    @pl.when((step > 0) & (step <= n_tiles))
    def _():
        tile = step - 1
        slot = tile % 2
        info = info_ref[...]
        x2 = (x1_ref[...] + info[:, 2:3] * _unpack_bf16_pairs(ybuf[slot * K])
              + info[:, 3:4] * _unpack_bf16_pairs(ybuf[slot * K + 1]))
        hp = _rms(x2, gp_ref[...]).astype(BF16)
        pp = jnp.dot(p_ref[...].astype(BF16), wpp_ref[...], preferred_element_type=F32)
        cw = D // GROUPS
        zs = []
        for c in range(GROUPS):
            fetch(step, c)
            zs.append(jnp.dot(hp, wpg_ref[:, c * cw:(c + 1) * cw], preferred_element_type=F32))
        x3 = x2 + jax.nn.sigmoid(jnp.concatenate(zs, axis=1)) * pp
        o_ref[...] = _rms(x3, gf_ref[...])


def _combine(slots, x1, info, yb, p, g_ple, w_ple_gate, w_ple_proj, g_final, tm=256):
    T, D = x1.shape
    n_tiles = T // tm
    slots_padded = jnp.concatenate([slots, jnp.zeros((tm * TOPK_IN_GROUP,), I32)])
    vmem = 4 * _nbytes((tm, D), F32) + _nbytes((D, D), BF16) + _nbytes((PLE_DIM, D), BF16)
    vmem += 2 * _nbytes((tm, PLE_DIM), F32) + 2 * TOPK_IN_GROUP * _nbytes((tm, D // 2), yb.dtype)
    vmem += 6 * _nbytes((tm, D), F32)
    tile = lambda i, sl: (jnp.clip(i - 1, 0, n_tiles - 1), 0)
    fixed = lambda i, sl: (0, 0)
    resident = pl.Buffered(1)
    return pl.pallas_call(
        _combine_kernel,
        out_shape=jax.ShapeDtypeStruct((T, D), F32),
        grid_spec=pltpu.PrefetchScalarGridSpec(
            num_scalar_prefetch=1,
            grid=(n_tiles + 2,),
            in_specs=[
                pl.BlockSpec((tm, D), tile),
                pl.BlockSpec((tm, V7X_LANES), tile),
                pl.BlockSpec(memory_space=pl.ANY),
                pl.BlockSpec((tm, PLE_DIM), tile),
                pl.BlockSpec((1, D), fixed),
                pl.BlockSpec((D, D), fixed, pipeline_mode=resident),
                pl.BlockSpec((PLE_DIM, D), fixed, pipeline_mode=resident),
                pl.BlockSpec((1, D), fixed),
            ],
            out_specs=pl.BlockSpec((tm, D), tile),
            scratch_shapes=[pltpu.VMEM((2 * TOPK_IN_GROUP, tm, yb.shape[1]), yb.dtype),
                            pltpu.SemaphoreType.DMA((2,))],
        ),
        compiler_params=_cparams(1, vmem),
        name="combine",
    )(slots_padded, x1, info, yb, p, g_ple.reshape(1, D), w_ple_gate, w_ple_proj, g_final.reshape(1, D))


def _rope_tables(T):
    half, BS = MOBA_HEAD_DIM // 2, MOBA_BLOCK
    inv_freq = ROPE_THETA ** (-jnp.arange(half, dtype=F32) / half)
    ang_a = (jnp.arange(T // BS, dtype=F32) * BS)[:, None, None] * inv_freq
    ang_b = jnp.arange(BS, dtype=F32)[None, :, None] * inv_freq
    ca, sa, cb, sb = jnp.cos(ang_a), jnp.sin(ang_a), jnp.cos(ang_b), jnp.sin(ang_b)
    cos = (ca * cb - sa * sb).reshape(T, half)
    sin = (sa * cb + ca * sb).reshape(T, half)
    return jnp.concatenate([cos, cos], axis=1), jnp.concatenate([-sin, sin], axis=1)


def _mixers(x2d, g_mix, w_in, lb, hgrn_norm_g):
    T = x2d.shape[0]
    W = HGRN_WIDTH
    w = w_in
    cos, sin = _rope_tables(T)
    log_lb = jnp.log(lb).reshape(1, W)
    log_1m = jnp.log1p(-lb).reshape(1, W)
    hq, h = _norm_proj(x2d, g_mix, w, 0 * W, W, _ep_silu, BF16)
    logf = _proj(h, w, 1 * W, W, _ep_logf, F32, col_extras=(log_lb, log_1m))
    hi = _proj(h, w, 2 * W, W, _ep_identity, BF16)
    hog = _proj(h, w, 3 * W, W, _ep_silu, BF16)
    scale = MOBA_HEAD_DIM ** -0.5
    mqt = _proj_t(h, w, 4 * W, functools.partial(_ep_rope, scale=scale), row_extras=(cos, sin))
    blk = jnp.arange(T, dtype=I32)[:, None] // MOBA_BLOCK
    blk_onehot = (blk == jnp.arange(V7X_LANES, dtype=I32)[None, :]).astype(F32)
    mk = _proj(h, w, 5 * W, W, _ep_rope_aug, BF16, row_extras=(cos, sin, blk_onehot), widen=2)
    mvt = _proj_t(h, w, 6 * W, _ep_identity, ones_rows=MOBA_VT_ROWS - MOBA_HEAD_DIM)
    gates = _proj(h, w, 7 * W, 2 * D_MODEL, _ep_sigmoid, BF16)
    o_hgrn = _hgrn(hq, logf, hi, hog, hgrn_norm_g)
    o_moba = _moba(mqt, mk, mvt)
    return o_hgrn, o_moba, gates


def _moe_plan(info, cnt, T):
    R = MOE_ROWS
    counts = cnt[0, :N_EXPERTS].astype(I32)
    padded = (counts + R - 1) // R * R
    pad_end = jnp.cumsum(padded)
    pad_start = pad_end - padded
    slots = _slots(info, pad_start)
    n_blocks = (T * TOPK_IN_GROUP) // R + N_EXPERTS
    block_expert = jnp.minimum(
        jnp.searchsorted(pad_end, jnp.arange(n_blocks, dtype=I32) * R, side="right"), N_EXPERTS - 1).astype(I32)
    has = counts > 0
    segment_expert = jnp.argsort(jnp.logical_not(has), stable=True).astype(I32)
    block_segment = (jnp.cumsum(has.astype(I32))[block_expert] - 1).astype(I32)
    n_used = jnp.stack([pad_end[-1] // R, jnp.sum(has.astype(I32))]).astype(I32)
    return slots, pad_end.astype(I32), block_segment, segment_expert, n_used, n_blocks * R


def kernel(x, p, norm_mix_g, w_in, hgrn_lb_raw, hgrn_norm_g, w_up_hgrn, w_up_moba, w_out, norm_ffn_g,
           w_router_group, w_router_expert, w_exp_gate, w_exp_up, w_exp_down, norm_ple_g, w_ple_gate,
           w_ple_proj, norm_final_g):
    B, T, D = x.shape
    assert B == 1 and D == D_MODEL and w_in.shape[0] == 1 and T % (4 * MOBA_BLOCK) == 0
    lower_bounds = jnp.cumsum(jax.nn.softmax(hgrn_lb_raw.astype(F32), axis=0), axis=0)
    x2d = x.reshape(T, D)
    o_hgrn, o_moba, gates = _mixers(x2d, norm_mix_g[0], w_in[0], lower_bounds[0], hgrn_norm_g[0])
    merged = _merge(o_hgrn, o_moba, gates, w_up_hgrn[0].astype(BF16), w_up_moba[0].astype(BF16))
    w_router = jnp.pad(jnp.concatenate([w_router_group[0], w_router_expert[0]], axis=1),
                       ((0, 0), (0, V7X_LANES - N_GROUPS - N_EXPERTS))).astype(BF16)
    x1, h2p, info, cnt = _outproj(merged, x2d, w_out[0].astype(BF16), norm_ffn_g[0], w_router)
    slots, pad_end, block_segment, segment_expert, n_used, m_pad = _moe_plan(info, cnt, T)
    xs = _dispatch(slots, pad_end, h2p, m_pad)
    yb = _experts(block_segment, segment_expert, n_used, xs, w_exp_gate[0], w_exp_up[0], w_exp_down[0])
    out = _combine(slots, x1, info, yb, p[0].reshape(T, PLE_DIM), norm_ple_g[0], w_ple_gate[0].astype(BF16),
                   w_ple_proj[0].astype(BF16), norm_final_g)
    return out.reshape(B, T, D)
```

```python
import functools

import jax
import jax.numpy as jnp
from jax import lax
from jax.experimental import pallas as pl
from jax.experimental.pallas import tpu as pltpu

F32 = jnp.float32
BF16 = jnp.bfloat16
I32 = jnp.int32

D_MODEL = 2048
PLE_DIM = 256
HGRN_HEADS = 8
HGRN_HEAD_DIM = 128
HGRN_WIDTH = HGRN_HEADS * HGRN_HEAD_DIM
MOBA_HEADS = 8
MOBA_HEAD_DIM = 128
MOBA_WIDTH = MOBA_HEADS * MOBA_HEAD_DIM
MOBA_BLOCK = 256
MOBA_TOPK = 3
ROPE_THETA = 10000.0
N_GROUPS = 4
EXPERTS_PER_GROUP = 8
N_EXPERTS = N_GROUPS * EXPERTS_PER_GROUP
TOPK_IN_GROUP = 2
D_EXPERT = 512
EPS = 1e-6
NEG_INF = -1e30

V7X_LANES = 128
V7X_SUBLANES = 8
V7X_VMEM_BUDGET_BYTES = 56 * 1024 * 1024

HGRN_CHUNK = 128
MOE_ROWS = 256


def _cparams(n_grid, vmem_bytes):
    return pltpu.CompilerParams(
        dimension_semantics=("arbitrary",) * n_grid,
        vmem_limit_bytes=int(min(max(vmem_bytes, 16 * 1024 * 1024), V7X_VMEM_BUDGET_BYTES)),
    )


def _nbytes(shape, dtype):
    n = 1
    for s in shape:
        n *= s
    return n * jnp.dtype(dtype).itemsize


def _rms(x, g):
    ms = jnp.mean(x * x, axis=-1, keepdims=True)
    return x * lax.rsqrt(ms + EPS) * g


def _ep_identity(acc):
    return acc


def _ep_silu(acc):
    return acc * jax.nn.sigmoid(acc)


def _ep_sigmoid(acc):
    return jax.nn.sigmoid(acc)


def _ep_logf(acc, la_ref, lc_ref):
    ls = jnp.minimum(acc, 0.0) - jnp.log(1.0 + jnp.exp(-jnp.abs(acc)))
    u = la_ref[...]
    v = lc_ref[...] + ls
    return jnp.maximum(u, v) + jnp.log(1.0 + jnp.exp(-jnp.abs(u - v)))


def _ep_rope(acc, cos_ref, sin_ref, *, scale):
    cos = cos_ref[...]
    sin = sin_ref[...]
    outs = []
    for hh in range(acc.shape[1] // MOBA_HEAD_DIM):
        a = acc[:, hh * MOBA_HEAD_DIM:(hh + 1) * MOBA_HEAD_DIM]
        r = pltpu.roll(a, MOBA_HEAD_DIM // 2, axis=1)
        outs.append((a * cos + r * sin) * scale)
    return jnp.concatenate(outs, axis=1)


def _ep_rope_aug(acc, cos_ref, sin_ref, oh_ref):
    cos = cos_ref[...]
    sin = sin_ref[...]
    oh = oh_ref[...]
    outs = []
    for hh in range(acc.shape[1] // MOBA_HEAD_DIM):
        a = acc[:, hh * MOBA_HEAD_DIM:(hh + 1) * MOBA_HEAD_DIM]
        outs.append(a * cos + pltpu.roll(a, MOBA_HEAD_DIM // 2, axis=1) * sin)
        outs.append(oh)
    return jnp.concatenate(outs, axis=1)


def _cast_weight_once(w_ref, wb_ref, row_axis):
    @pl.when(pl.program_id(row_axis) == 0)
    def _():
        wb_ref[...] = w_ref[...].astype(wb_ref.dtype)


def _proj_kernel(h_ref, w_ref, *refs, epilogue):
    *extra, o_ref, wb_ref = refs
    _cast_weight_once(w_ref, wb_ref, 1)
    acc = jnp.dot(h_ref[...], wb_ref[...], preferred_element_type=F32)
    o_ref[...] = epilogue(acc, *extra).astype(o_ref.dtype)


def _proj(h, w, col0, ncols, epilogue, out_dtype, row_extras=(), col_extras=(), tm=1024, tn=1024, widen=1):
    T, K = h.shape
    tn = min(tn, ncols)
    tm = min(tm, T)
    cb = col0 // tn
    otn = widen * tn
    in_specs = [
        pl.BlockSpec((tm, K), lambda j, i: (i, 0)),
        pl.BlockSpec((K, tn), lambda j, i: (0, cb + j)),
    ]
    for e in row_extras:
        in_specs.append(pl.BlockSpec((tm, e.shape[1]), lambda j, i: (i, 0)))
    for e in col_extras:
        in_specs.append(pl.BlockSpec((1, tn), lambda j, i: (0, j)))
    vmem = 2 * (_nbytes((tm, K), h.dtype) + _nbytes((K, tn), w.dtype) + _nbytes((tm, otn), out_dtype))
    vmem += 3 * _nbytes((tm, otn), F32) + _nbytes((K, tn), BF16)
    return pl.pallas_call(
        functools.partial(_proj_kernel, epilogue=epilogue),
        out_shape=jax.ShapeDtypeStruct((T, widen * ncols), out_dtype),
        grid=(ncols // tn, T // tm),
        in_specs=in_specs,
        out_specs=pl.BlockSpec((tm, otn), lambda j, i: (i, j)),
        scratch_shapes=[pltpu.VMEM((K, tn), BF16)],
        compiler_params=_cparams(2, vmem),
        name="proj",
    )(h, w, *row_extras, *col_extras)


def _norm_proj_kernel(x_ref, g_ref, w_ref, o_ref, h_ref, wb_ref, *, epilogue):
    _cast_weight_once(w_ref, wb_ref, 0)
    h = _rms(x_ref[...], g_ref[...]).astype(h_ref.dtype)
    h_ref[...] = h
    o_ref[...] = epilogue(jnp.dot(h, wb_ref[...], preferred_element_type=F32)).astype(o_ref.dtype)


def _norm_proj(x, g, w, col0, ncols, epilogue, out_dtype, tm=512):
    T, K = x.shape
    cb = col0 // ncols
    vmem = 2 * (_nbytes((tm, K), F32) + _nbytes((tm, ncols), out_dtype) + _nbytes((tm, K), BF16))
    vmem += _nbytes((K, ncols), F32) + _nbytes((K, ncols), BF16) + 2 * _nbytes((tm, K), F32) + 3 * _nbytes((tm, ncols), F32)
    return pl.pallas_call(
        functools.partial(_norm_proj_kernel, epilogue=epilogue),
        out_shape=(jax.ShapeDtypeStruct((T, ncols), out_dtype), jax.ShapeDtypeStruct((T, K), BF16)),
        grid=(T // tm,),
        in_specs=[
            pl.BlockSpec((tm, K), lambda i: (i, 0)),
            pl.BlockSpec((1, K), lambda i: (0, 0)),
            pl.BlockSpec((K, ncols), lambda i: (0, cb), pipeline_mode=pl.Buffered(1)),
        ],
        out_specs=(pl.BlockSpec((tm, ncols), lambda i: (i, 0)), pl.BlockSpec((tm, K), lambda i: (i, 0))),
        scratch_shapes=[pltpu.VMEM((K, ncols), BF16)],
        compiler_params=_cparams(1, vmem),
        name="norm_proj",
    )(x, g.reshape(1, K), w)


MOBA_VT_ROWS = MOBA_HEAD_DIM + 16
MOBA_BLOCKS_PER_STEP = 4


def _proj_t_kernel(h_ref, w_ref, *refs, epilogue, ones_rows):
    *extra, o_ref, wb_ref = refs
    BS, HD = MOBA_BLOCK, MOBA_HEAD_DIM
    _cast_weight_once(w_ref, wb_ref, 0)
    acc = epilogue(jnp.dot(h_ref[...], wb_ref[...], preferred_element_type=F32), *extra)
    ones = jnp.ones((ones_rows, BS), F32) if ones_rows else None
    for b in range(acc.shape[0] // BS):
        parts = []
        for hh in range(acc.shape[1] // HD):
            parts.append(acc[b * BS:(b + 1) * BS, hh * HD:(hh + 1) * HD].T)
            if ones_rows:
                parts.append(ones)
        o_ref[b] = jnp.concatenate(parts, axis=0).astype(o_ref.dtype)


def _proj_t(h, w, col0, epilogue, row_extras=(), ones_rows=0, tm=1024):
    T, K = h.shape
    tn = MOBA_WIDTH
    tm = min(tm, T)
    cb = col0 // tn
    rows = MOBA_HEADS * (MOBA_HEAD_DIM + ones_rows)
    in_specs = [pl.BlockSpec((tm, K), lambda i: (i, 0)), pl.BlockSpec((K, tn), lambda i: (0, cb))]
    for e in row_extras:
        in_specs.append(pl.BlockSpec((tm, e.shape[1]), lambda i: (i, 0)))
    vmem = 2 * (_nbytes((tm, K), h.dtype) + _nbytes((K, tn), w.dtype) + _nbytes((tm, 2 * tn), BF16))
    vmem += 4 * _nbytes((tm, tn), F32) + _nbytes((K, tn), BF16)
    return pl.pallas_call(
        functools.partial(_proj_t_kernel, epilogue=epilogue, ones_rows=ones_rows),
        out_shape=jax.ShapeDtypeStruct((T // MOBA_BLOCK, rows, MOBA_BLOCK), BF16),
        grid=(T // tm,),
        in_specs=in_specs,
        out_specs=pl.BlockSpec((tm // MOBA_BLOCK, rows, MOBA_BLOCK), lambda i: (i, 0, 0)),
        scratch_shapes=[pltpu.VMEM((K, tn), BF16)],
        compiler_params=_cparams(1, vmem),
        name="proj_t",
    )(h, w, *row_extras)


def _hgrn_kernel(q_ref, g_ref, v_ref, og_ref, ng_ref, o_ref, st_ref, code_ref):
    W, HD, NH, C, S = HGRN_WIDTH, HGRN_HEAD_DIM, HGRN_HEADS, HGRN_CHUNK, V7X_SUBLANES
    J = C // S

    @pl.when(pl.program_id(0) == 0)
    def _():
        st_ref[...] = jnp.zeros_like(st_ref)
        tr = lax.broadcasted_iota(I32, (C, C), 0)
        tc = lax.broadcasted_iota(I32, (C, C), 1)
        xr = tr ^ tc
        code = jnp.zeros((C, C), I32)
        for lvl in range(1, 8):
            code = jnp.where(xr >= (1 << (lvl - 1)), lvl, code)
        code_ref[...] = jnp.where(tc > tr, -1, code)

    def r3(x):
        return x.astype(F32).reshape(J, S, W)

    def sub_bcast(x3, r):
        return jnp.broadcast_to(x3[:, r:r + 1, :], x3.shape)

    g3, q3, v3 = r3(g_ref[...]), r3(q_ref[...]), r3(v_ref[...])
    sub = lax.broadcasted_iota(I32, (1, S, W), 1)

    c3 = g3
    for s in (1, 2, 4):
        c3 = c3 + jnp.where(sub >= s, pltpu.roll(c3, s, axis=1), 0.0)
    run = jnp.zeros((1, 1, W), F32)
    carry = []
    for j in range(J):
        carry.append(run)
        run = run + c3[j:j + 1, S - 1:S, :]
    b3 = c3 + jnp.concatenate(carry, axis=0)
    bC = run

    k3 = 1.0 - jnp.exp(g3)
    qe3 = q3 * jnp.exp(b3)
    ks3 = k3 * jnp.exp(bC - b3)

    levels = [(0, q3, k3)]
    ref1 = jnp.where(sub % 2 == 0, b3, pltpu.roll(b3, 1, axis=1))
    ref2 = jnp.where(sub < 4, sub_bcast(b3, 1), sub_bcast(b3, 5))
    ref4 = sub_bcast(b3, 3)
    for lvl, (ref, upper) in enumerate(((ref1, sub % 2 == 1), (ref2, sub % 4 >= 2), (ref4, sub >= 4)), start=1):
        e = jnp.exp(-jnp.abs(b3 - ref))
        levels.append((lvl, jnp.where(upper, q3 * e, 0.0), jnp.where(upper, 0.0, k3 * e)))
    zero_group = jnp.zeros((1, S, W), F32)
    for lvl, half in enumerate((1, 2, 4, 8), start=4):
        qparts, kparts = [], []
        for j in range(J):
            jr = (j // (2 * half)) * (2 * half) + half - 1
            ref = b3[jr:jr + 1, S - 1:S, :]
            if (j % (2 * half)) >= half:
                qparts.append(q3[j:j + 1] * jnp.exp(b3[j:j + 1] - ref))
                kparts.append(zero_group)
            else:
                qparts.append(zero_group)
                kparts.append(k3[j:j + 1] * jnp.exp(ref - b3[j:j + 1]))
        levels.append((lvl, jnp.concatenate(qparts, axis=0), jnp.concatenate(kparts, axis=0)))

    code = code_ref[...]

    def mat(x3, h):
        return x3.reshape(C, W)[:, h * HD:(h + 1) * HD].astype(BF16)

    nt = (((1,), (1,)), ((), ()))
    tn = (((0,), (0,)), ((), ()))
    ebc = jnp.exp(bC).reshape(1, W)
    ng = ng_ref[...]
    for h in range(NH):
        a_mat = jnp.zeros((C, C), F32)
        for lvl, qr, kr in levels:
            s = lax.dot_general(mat(qr, h), mat(kr, h), nt, preferred_element_type=F32)
            a_mat = jnp.where(code == lvl, s, a_mat)
        vh = mat(v3, h)
        st = st_ref[h]
        o = jnp.dot(a_mat.astype(BF16), vh, preferred_element_type=F32)
        o = o + lax.dot_general(mat(qe3, h), st.astype(BF16), nt, preferred_element_type=F32)
        o = _rms(o, ng) * og_ref[:, h * HD:(h + 1) * HD].astype(F32)
        o_ref[:, h * HD:(h + 1) * HD] = o.astype(o_ref.dtype)
        st_ref[h] = st * ebc[:, h * HD:(h + 1) * HD] + lax.dot_general(
            vh, mat(ks3, h), tn, preferred_element_type=F32)


def _hgrn(q, logf, v, og, norm_g):
    T, W = q.shape
    C = HGRN_CHUNK
    blk = pl.BlockSpec((C, W), lambda c: (c, 0))
    vmem = 64 * _nbytes((C, W), F32)
    return pl.pallas_call(
        _hgrn_kernel,
        out_shape=jax.ShapeDtypeStruct((T, W), BF16),
        grid=(T // C,),
        in_specs=[blk, blk, blk, blk, pl.BlockSpec((1, HGRN_HEAD_DIM), lambda c: (0, 0))],
        out_specs=blk,
        scratch_shapes=[pltpu.VMEM((HGRN_HEADS, HGRN_HEAD_DIM, HGRN_HEAD_DIM), F32), pltpu.VMEM((C, C), I32)],
        compiler_params=_cparams(1, vmem),
        name="hgrn",
    )(q, logf, v, og, norm_g.reshape(1, HGRN_HEAD_DIM))


def _moba_kernel(qt_ref, k_ref, vt_ref, o_ref, km_ref):
    BS, HD, VR = MOBA_BLOCK, MOBA_HEAD_DIM, MOBA_VT_ROWS
    T = k_ref.shape[0]
    NB = T // BS
    G = o_ref.shape[1] // HD
    cur = pl.program_id(1)

    @pl.when(cur == 0)
    def _():
        for g in range(G):
            kf = k_ref[:, 2 * g * HD:(2 * g + 1) * HD].astype(F32).reshape(NB, BS, HD)
            km_ref[g] = jnp.sum(kf, axis=1) * (1.0 / BS)

    blk = lax.broadcasted_iota(I32, (NB, BS), 0)
    pad = jnp.zeros((V7X_LANES - NB, BS), F32)
    qts = [qt_ref[0, g * HD:(g + 1) * HD, :] for g in range(G)]
    gts = [jnp.dot(km_ref[g].astype(BF16), qts[g], preferred_element_type=F32) for g in range(G)]
    qcs = []
    for g in range(G):
        gt = jnp.where(blk < cur, gts[g], NEG_INF)
        sel = blk == cur
        for _ in range(MOBA_TOPK):
            mx = jnp.max(gt, axis=0, keepdims=True)
            idx = jnp.min(jnp.where(gt == mx, blk, NB), axis=0, keepdims=True)
            pick = (blk == idx) & (mx > 0.5 * NEG_INF)
            sel = sel | pick
            gt = jnp.where(pick, NEG_INF, gt)
        pen = jnp.concatenate([jnp.where(sel, 0.0, NEG_INF), pad], axis=0).astype(BF16)
        qcs.append(jnp.concatenate([qts[g], pen], axis=0))

    KB = MOBA_BLOCKS_PER_STEP

    def body(blk0, carry, own_step=False, nblk=KB):
        ms, accs = carry
        r = pl.multiple_of(blk0 * BS, BS)
        sns = [jnp.dot(k_ref[pl.ds(r, nblk * BS), 2 * g * HD:(2 * g + 2) * HD], qcs[g], preferred_element_type=F32)
               for g in range(G)]
        if own_step:
            krow = lax.broadcasted_iota(I32, (nblk * BS, BS), 0)
            qcol = lax.broadcasted_iota(I32, (nblk * BS, BS), 1)
            keep = (blk0 + krow // BS != cur) | (krow % BS <= qcol)
            sns = [jnp.where(keep, s, NEG_INF) for s in sns]
        new_ms, alphas, pns = [], [], []
        for g in range(G):
            m_new = jnp.maximum(ms[g], jnp.max(sns[g], axis=0, keepdims=True))
            alphas.append(jnp.exp(ms[g] - m_new))
            pns.append(jnp.exp(sns[g] - m_new).astype(BF16))
            new_ms.append(m_new)
        new_accs = []
        for g in range(G):
            pv = alphas[g] * accs[g]
            for j in range(nblk):
                pv = pv + jnp.dot(vt_ref[blk0 + j, g * VR:(g + 1) * VR, :], pns[g][j * BS:(j + 1) * BS],
                                  preferred_element_type=F32)
            new_accs.append(pv)
        return tuple(new_ms), tuple(new_accs)

    ms = tuple(jnp.full((1, BS), NEG_INF, F32) for _ in range(G))
    accs = tuple(jnp.zeros((VR, BS), F32) for _ in range(G))
    carry = lax.fori_loop(0, cur // (2 * KB), lambda c, cr: body(c * 2 * KB, cr, nblk=2 * KB), (ms, accs))
    last = (cur // KB) * KB
    carry = lax.cond(cur % (2 * KB) >= KB, lambda cr: body(last - KB, cr), lambda cr: cr, carry)
    _, accs = lax.cond(
        cur % KB < KB // 2,
        lambda cr: body(last, cr, own_step=True, nblk=KB // 2),
        lambda cr: body(last, cr, own_step=True),
        carry)
    for g in range(G):
        ot = accs[g][:HD, :] / accs[g][HD:HD + 1, :]
        o_ref[:, g * HD:(g + 1) * HD] = ot.T.astype(o_ref.dtype)


def _moba(mqt, mk_aug, mvt, heads_per_step=4):
    T = mk_aug.shape[0]
    BS, HD, G, VR = MOBA_BLOCK, MOBA_HEAD_DIM, heads_per_step, MOBA_VT_ROWS
    NB = T // BS
    vmem = _nbytes((T, 2 * G * HD), BF16) + _nbytes((NB, G * VR, BS), BF16) + 8 * _nbytes((BS, G * HD), BF16)
    vmem += 16 * G * _nbytes((BS, BS), F32)
    resident = pl.Buffered(1)
    return pl.pallas_call(
        _moba_kernel,
        out_shape=jax.ShapeDtypeStruct((T, MOBA_WIDTH), BF16),
        grid=(MOBA_HEADS // G, NB),
        in_specs=[
            pl.BlockSpec((1, G * HD, BS), lambda h, i: (i, h, 0)),
            pl.BlockSpec((T, 2 * G * HD), lambda h, i: (0, h), pipeline_mode=resident),
            pl.BlockSpec((NB, G * VR, BS), lambda h, i: (0, h, 0), pipeline_mode=resident),
        ],
        out_specs=pl.BlockSpec((BS, G * HD), lambda h, i: (i, h)),
        scratch_shapes=[pltpu.VMEM((G, NB, HD), F32)],
        compiler_params=_cparams(2, vmem),
        name="moba",
    )(mqt, mk_aug, mvt)


def _merge_kernel(oh_ref, om_ref, ga_ref, gb_ref, wh_ref, wm_ref, o_ref):
    a = jnp.dot(oh_ref[...], wh_ref[...], preferred_element_type=F32)
    b = jnp.dot(om_ref[...], wm_ref[...], preferred_element_type=F32)
    o_ref[...] = (ga_ref[...].astype(F32) * a + gb_ref[...].astype(F32) * b).astype(o_ref.dtype)


def _merge(o_hgrn, o_moba, gates, w_up_hgrn, w_up_moba, tm=512):
    T = o_hgrn.shape[0]
    D = D_MODEL
    vmem = 2 * (_nbytes((tm, HGRN_WIDTH), F32) + _nbytes((tm, MOBA_WIDTH), BF16) + 3 * _nbytes((tm, D), BF16)
                + 2 * _nbytes((HGRN_WIDTH, D), BF16)) + 3 * _nbytes((tm, D), F32)
    return pl.pallas_call(
        _merge_kernel,
        out_shape=jax.ShapeDtypeStruct((T, D), BF16),
        grid=(T // tm,),
        in_specs=[
            pl.BlockSpec((tm, HGRN_WIDTH), lambda i: (i, 0)),
            pl.BlockSpec((tm, MOBA_WIDTH), lambda i: (i, 0)),
            pl.BlockSpec((tm, D), lambda i: (i, 0)),
            pl.BlockSpec((tm, D), lambda i: (i, 1)),
            pl.BlockSpec((HGRN_WIDTH, D), lambda i: (0, 0)),
            pl.BlockSpec((MOBA_WIDTH, D), lambda i: (0, 0)),
        ],
        out_specs=pl.BlockSpec((tm, D), lambda i: (i, 0)),
        compiler_params=_cparams(1, vmem),
        name="merge",
    )(o_hgrn, o_moba, gates, gates, w_up_hgrn, w_up_moba)


def _pack_bf16_pairs(x):
    C = x.shape[1] // 2
    b = lax.bitcast_convert_type(x, jnp.uint32)
    r = (b + jnp.uint32(0x7FFF) + ((b >> 16) & jnp.uint32(1))) >> 16
    return r[:, :C] | (r[:, C:] << 16)


def _unpack_bf16_pairs(p):
    lo = lax.bitcast_convert_type(p << 16, F32)
    hi = lax.bitcast_convert_type(p & jnp.uint32(0xFFFF0000), F32)
    return jnp.concatenate([lo, hi], axis=1)


def _outproj_kernel(m_ref, x_ref, w_ref, g_ref, wr_ref, x1_ref, h2p_ref, info_ref, cnt_ref, prev_ref, carry_ref):
    step = pl.program_id(0)
    n_tiles = pl.num_programs(0) - 1

    @pl.when(step == 0)
    def _():
        carry_ref[...] = jnp.zeros_like(carry_ref)

    def finish_previous():
        h2 = _rms(prev_ref[...], g_ref[...])
        h2p_ref[...] = _pack_bf16_pairs(h2)
        info_ref[...] = _route(h2, wr_ref, carry_ref)
        cnt_ref[...] = carry_ref[...]

    def project():
        return x_ref[...] + jnp.dot(m_ref[...], w_ref[...], preferred_element_type=F32)

    @pl.when(step == 0)
    def _():
        x1 = project()
        x1_ref[...] = x1
        prev_ref[...] = x1

    @pl.when((step > 0) & (step < n_tiles))
    def _():
        x1 = project()
        finish_previous()
        x1_ref[...] = x1
        prev_ref[...] = x1

    @pl.when(step == n_tiles)
    def _():
        finish_previous()


def _outproj(merged, x, w_out, g_ffn, w_router, tm=512):
    T, D = x.shape
    n_tiles = T // tm
    vmem = 2 * (_nbytes((tm, D), BF16) + 3 * _nbytes((tm, D), F32) + _nbytes((D, D), BF16)) + 7 * _nbytes((tm, D), F32)
    cur = lambda i: (jnp.minimum(i, n_tiles - 1), 0)
    prev = lambda i: (jnp.maximum(i - 1, 0), 0)
    fixed = lambda i: (0, 0)
    return pl.pallas_call(
        _outproj_kernel,
        out_shape=(jax.ShapeDtypeStruct((T, D), F32), jax.ShapeDtypeStruct((T, D // 2), jnp.uint32),
                   jax.ShapeDtypeStruct((T, V7X_LANES), F32), jax.ShapeDtypeStruct((1, V7X_LANES), F32)),
        grid=(n_tiles + 1,),
        in_specs=[
            pl.BlockSpec((tm, D), cur),
            pl.BlockSpec((tm, D), cur),
            pl.BlockSpec((D, D), fixed, pipeline_mode=pl.Buffered(1)),
            pl.BlockSpec((1, D), fixed),
            pl.BlockSpec((D, V7X_LANES), fixed),
        ],
        out_specs=(pl.BlockSpec((tm, D), cur), pl.BlockSpec((tm, D // 2), prev), pl.BlockSpec((tm, V7X_LANES), prev),
                   pl.BlockSpec((1, V7X_LANES), fixed)),
        scratch_shapes=[pltpu.VMEM((tm, D), F32), pltpu.VMEM((1, V7X_LANES), F32)],
        compiler_params=_cparams(1, vmem),
        name="outproj",
    )(merged, x, w_out, g_ffn.reshape(1, D), w_router)


def _route(h2, w_ref, carry_ref):
    tm = h2.shape[0]
    logits = jnp.dot(h2.astype(BF16), w_ref[...], preferred_element_type=F32)
    lane = lax.broadcasted_iota(I32, (tm, V7X_LANES), 1)
    is_g = lane < N_GROUPS
    gl = jnp.where(is_g, logits, NEG_INF)
    gmax = jnp.max(gl, axis=1, keepdims=True)
    g_sel = jnp.min(jnp.where(gl == gmax, lane, V7X_LANES), axis=1, keepdims=True)
    gsum = jnp.sum(jnp.where(is_g, jnp.exp(gl - gmax), 0.0), axis=1, keepdims=True)
    p_group = 1.0 / gsum
    lo = N_GROUPS + EXPERTS_PER_GROUP * g_sel
    emask = (lane >= lo) & (lane < lo + EXPERTS_PER_GROUP)
    el = jnp.where(emask, logits, NEG_INF)
    e1 = jnp.max(el, axis=1, keepdims=True)
    i1 = jnp.min(jnp.where((el == e1) & emask, lane, V7X_LANES), axis=1, keepdims=True)
    emask2 = emask & (lane != i1)
    el2 = jnp.where(emask2, logits, NEG_INF)
    e2 = jnp.max(el2, axis=1, keepdims=True)
    i2 = jnp.min(jnp.where((el2 == e2) & emask2, lane, V7X_LANES), axis=1, keepdims=True)
    r = jnp.exp(e2 - e1)
    w1 = p_group / (1.0 + r)
    w2 = p_group * r / (1.0 + r)
    eid1 = i1 - N_GROUPS
    eid2 = i2 - N_GROUPS
    oh1 = jnp.where(lane == eid1, 1.0, 0.0)
    oh2 = jnp.where(lane == eid2, 1.0, 0.0)
    cnt = oh1 + oh2
    tri = jnp.where(lax.broadcasted_iota(I32, (tm, tm), 0) > lax.broadcasted_iota(I32, (tm, tm), 1), 1.0, 0.0)
    before = jnp.dot(tri.astype(BF16), cnt.astype(BF16), preferred_element_type=F32) + carry_ref[...]
    rank1 = jnp.sum(oh1 * before, axis=1, keepdims=True)
    rank2 = jnp.sum(oh2 * before, axis=1, keepdims=True)
    carry_ref[...] = carry_ref[...] + jnp.sum(cnt, axis=0, keepdims=True)
    info = jnp.zeros((tm, V7X_LANES), F32)
    for k, val in enumerate((eid1.astype(F32), eid2.astype(F32), w1, w2, rank1, rank2)):
        info = jnp.where(lane == k, val, info)
    return info


def _row_copy(src_ref, src_row, dst_ref, dst_row, sem):
    return pltpu.make_async_copy(src_ref.at[pl.ds(src_row, 1), :], dst_ref.at[pl.ds(dst_row, 1), :], sem)


ROW_DMA_UNROLL = 8


def _slots_kernel(info_ref, ps_ref, o_ref):
    info = info_ref[...]
    lane = lax.broadcasted_iota(I32, info.shape, 1)
    lane_f = lane.astype(F32)
    ps = ps_ref[...]
    out = jnp.zeros(info.shape, F32)
    for k in range(TOPK_IN_GROUP):
        start = jnp.sum(jnp.where(lane_f == info[:, k:k + 1], ps, 0.0), axis=1, keepdims=True)
        out = jnp.where(lane == k, start + info[:, 4 + k:5 + k], out)
    o_ref[...] = out.astype(I32)


def _slots(info, pad_start, tm=1024):
    T = info.shape[0]
    ps = jnp.pad(pad_start.astype(F32), (0, V7X_LANES - N_EXPERTS)).reshape(1, V7X_LANES)
    out = pl.pallas_call(
        _slots_kernel,
        out_shape=jax.ShapeDtypeStruct((T, V7X_LANES), I32),
        grid=(T // tm,),
        in_specs=[pl.BlockSpec((tm, V7X_LANES), lambda i: (i, 0)), pl.BlockSpec((1, V7X_LANES), lambda i: (0, 0))],
        out_specs=pl.BlockSpec((tm, V7X_LANES), lambda i: (i, 0)),
        compiler_params=_cparams(1, 16 * _nbytes((tm, V7X_LANES), F32)),
        name="slots",
    )(info, ps)
    return out[:, :TOPK_IN_GROUP].reshape(-1)


def _dispatch_kernel(slot_ref, pend_ref, h_ref, xs_ref, zero_ref, sem):
    tm = h_ref.shape[0]
    step = pl.program_id(0)
    base = step * tm

    @pl.when(step == 0)
    def _():
        zero_ref[...] = jnp.zeros_like(zero_ref)

        def tail(e):
            return pltpu.make_async_copy(
                zero_ref, xs_ref.at[pl.ds(pl.multiple_of(pend_ref[e] - MOE_ROWS, MOE_ROWS), MOE_ROWS), :], sem)

        def nonempty(e):
            return pend_ref[e] > (pend_ref[e - 1] if e else 0)

        def unused(b):
            return pltpu.make_async_copy(
                zero_ref, xs_ref.at[pl.ds(pl.multiple_of(b * MOE_ROWS, MOE_ROWS), MOE_ROWS), :], sem)

        first_unused = pend_ref[N_EXPERTS - 1] // MOE_ROWS
        n_blocks = xs_ref.shape[0] // MOE_ROWS
        for e in range(N_EXPERTS):
            pl.when(nonempty(e))(lambda e=e: tail(e).start())
        lax.fori_loop(first_unused, n_blocks, lambda b, c: (unused(b).start(), c)[1], 0)
        for e in range(N_EXPERTS):
            pl.when(nonempty(e))(lambda e=e: tail(e).wait())
        lax.fori_loop(first_unused, n_blocks, lambda b, c: (unused(b).wait(), c)[1], 0)

    def issue(rb, c):
        for u in range(ROW_DMA_UNROLL):
            r = rb * ROW_DMA_UNROLL + u
            for k in range(TOPK_IN_GROUP):
                _row_copy(h_ref, r, xs_ref, slot_ref[(base + r) * TOPK_IN_GROUP + k], sem).start(priority=k % 2)
        return c

    lax.fori_loop(0, tm // ROW_DMA_UNROLL, issue, 0)
    for _ in range(tm * TOPK_IN_GROUP):
        _row_copy(h_ref, 0, xs_ref, 0, sem).wait()


def _dispatch(slots, pad_end, h2p, m_pad, tm=256):
    T, C = h2p.shape
    return pl.pallas_call(
        _dispatch_kernel,
        out_shape=jax.ShapeDtypeStruct((m_pad, C), h2p.dtype),
        grid_spec=pltpu.PrefetchScalarGridSpec(
            num_scalar_prefetch=2,
            grid=(T // tm,),
            in_specs=[pl.BlockSpec((tm, C), lambda i, sl, pe: (i, 0))],
            out_specs=pl.BlockSpec(memory_space=pl.ANY),
            scratch_shapes=[pltpu.VMEM((MOE_ROWS, C), h2p.dtype), pltpu.SemaphoreType.DMA(())],
        ),
        compiler_params=_cparams(1, 6 * _nbytes((tm, C), h2p.dtype)),
        name="dispatch",
    )(slots, pad_end, h2p)


def _expert_kernel(seg_ref, sexp_ref, nu_ref, xs_ref, wg_ref, wu_ref, wd_ref, y_ref, wgf_ref, wuf_ref, wdf_ref,
                   wgb_ref, wub_ref, wdb_ref, sem):
    b = pl.program_id(0)
    used = b < nu_ref[0]
    seg = seg_ref[b]
    first = used & ((b == 0) | (seg != seg_ref[jnp.maximum(b - 1, 0)]))

    def weights_copy(s, part):
        slot = s % 2
        src, dst = ((wg_ref, wgf_ref), (wu_ref, wuf_ref), (wd_ref, wdf_ref))[part]
        return pltpu.make_async_copy(src.at[sexp_ref[s]], dst.at[slot], sem.at[slot])

    @pl.when(b == 0)
    def _():
        for part in range(3):
            weights_copy(0, part).start()

    @pl.when(first)
    def _():
        for part in range(3):
            weights_copy(seg, part).wait()
        slot = seg % 2
        wgb_ref[...] = wgf_ref[slot].astype(BF16)
        wub_ref[...] = wuf_ref[slot].astype(BF16)
        wdb_ref[...] = wdf_ref[slot].astype(BF16)

    @pl.when(first & (seg + 1 < nu_ref[1]))
    def _():
        for part in range(3):
            weights_copy(seg + 1, part).start()

    @pl.when(used)
    def _():
        x = _unpack_bf16_pairs(xs_ref[...]).astype(BF16)
        a = jnp.dot(x, wgb_ref[...], preferred_element_type=F32)
        u = jnp.dot(x, wub_ref[...], preferred_element_type=F32)
        hm = (a * jax.nn.sigmoid(a) * u).astype(BF16)
        y_ref[...] = _pack_bf16_pairs(jnp.dot(hm, wdb_ref[...], preferred_element_type=F32))

    @pl.when(jnp.logical_not(used))
    def _():
        y_ref[...] = jnp.zeros_like(y_ref)


def _experts(block_segment, segment_expert, n_used, xs, w_gate, w_up, w_down):
    m_pad, C = xs.shape
    D = 2 * C
    R, Fd = MOE_ROWS, D_EXPERT
    vmem = 4 * _nbytes((R, C), xs.dtype) + 6 * _nbytes((D, Fd), F32) + 3 * _nbytes((D, Fd), BF16)
    vmem += 6 * _nbytes((R, D), F32)
    hbm = pl.BlockSpec(memory_space=pl.ANY)
    return pl.pallas_call(
        _expert_kernel,
        out_shape=jax.ShapeDtypeStruct((m_pad, C), xs.dtype),
        grid_spec=pltpu.PrefetchScalarGridSpec(
            num_scalar_prefetch=3,
            grid=(m_pad // R,),
            in_specs=[
                pl.BlockSpec((R, C), lambda b, sg, se, nu: (jnp.minimum(b, jnp.maximum(nu[0] - 1, 0)), 0)),
                hbm, hbm, hbm,
            ],
            out_specs=pl.BlockSpec((R, C), lambda b, sg, se, nu: (b, 0)),
            scratch_shapes=[pltpu.VMEM((2, D, Fd), F32), pltpu.VMEM((2, D, Fd), F32), pltpu.VMEM((2, Fd, D), F32),
                            pltpu.VMEM((D, Fd), BF16), pltpu.VMEM((D, Fd), BF16), pltpu.VMEM((Fd, D), BF16),
                            pltpu.SemaphoreType.DMA((2,))],
        ),
        compiler_params=_cparams(1, vmem),
        name="experts",
    )(block_segment, segment_expert, n_used, xs, w_gate, w_up, w_down)


def _combine_kernel(slot_ref, x1_ref, info_ref, yb_ref, p_ref, gp_ref, wpg_ref, wpp_ref, gf_ref, o_ref, ybuf, sem):
    tm, D = x1_ref.shape
    step = pl.program_id(0)
    n_tiles = pl.num_programs(0) - 2
    K = TOPK_IN_GROUP
    GROUPS = 8

    def fetch(tile, group=None):
        slot = tile % 2
        rows = range(tm) if group is None else range(group * tm // GROUPS, (group + 1) * tm // GROUPS)
        for r in rows:
            for k in range(K):
                _row_copy(yb_ref, slot_ref[(tile * tm + r) * K + k], ybuf.at[slot * K + k], r,
                          sem.at[slot]).start(priority=k % 2)

    def drain(tile):
        slot = tile % 2
        for _ in range(tm * K):
            _row_copy(yb_ref, 0, ybuf.at[0], 0, sem.at[slot]).wait()

    @pl.when(step == 0)
    def _():
        fetch(step)

    @pl.when(step > 0)
    def _():
        drain(step - 1)

    @pl.when((step > 0) & (step <= n_tiles))
    def _():
        tile = step - 1
        slot = tile % 2
        info = info_ref[...]
        x2 = (x1_ref[...] + info[:, 2:3] * _unpack_bf16_pairs(ybuf[slot * K])
              + info[:, 3:4] * _unpack_bf16_pairs(ybuf[slot * K + 1]))
        hp = _rms(x2, gp_ref[...]).astype(BF16)
        pp = jnp.dot(p_ref[...].astype(BF16), wpp_ref[...], preferred_element_type=F32)
        cw = D // GROUPS
        zs = []
        for c in range(GROUPS):
            fetch(step, c)
            zs.append(jnp.dot(hp, wpg_ref[:, c * cw:(c + 1) * cw], preferred_element_type=F32))
        x3 = x2 + jax.nn.sigmoid(jnp.concatenate(zs, axis=1)) * pp
        o_ref[...] = _rms(x3, gf_ref[...])


def _combine(slots, x1, info, yb, p, g_ple, w_ple_gate, w_ple_proj, g_final, tm=256):
    T, D = x1.shape
    n_tiles = T // tm
    slots_padded = jnp.concatenate([slots, jnp.zeros((tm * TOPK_IN_GROUP,), I32)])
    vmem = 4 * _nbytes((tm, D), F32) + _nbytes((D, D), BF16) + _nbytes((PLE_DIM, D), BF16)
    vmem += 2 * _nbytes((tm, PLE_DIM), F32) + 2 * TOPK_IN_GROUP * _nbytes((tm, D // 2), yb.dtype)
    vmem += 6 * _nbytes((tm, D), F32)
    tile = lambda i, sl: (jnp.clip(i - 1, 0, n_tiles - 1), 0)
    fixed = lambda i, sl: (0, 0)
    resident = pl.Buffered(1)
    return pl.pallas_call(
        _combine_kernel,
        out_shape=jax.ShapeDtypeStruct((T, D), F32),
        grid_spec=pltpu.PrefetchScalarGridSpec(
            num_scalar_prefetch=1,
            grid=(n_tiles + 2,),
            in_specs=[
                pl.BlockSpec((tm, D), tile),
                pl.BlockSpec((tm, V7X_LANES), tile),
                pl.BlockSpec(memory_space=pl.ANY),
                pl.BlockSpec((tm, PLE_DIM), tile),
                pl.BlockSpec((1, D), fixed),
                pl.BlockSpec((D, D), fixed, pipeline_mode=resident),
                pl.BlockSpec((PLE_DIM, D), fixed, pipeline_mode=resident),
                pl.BlockSpec((1, D), fixed),
            ],
            out_specs=pl.BlockSpec((tm, D), tile),
            scratch_shapes=[pltpu.VMEM((2 * TOPK_IN_GROUP, tm, yb.shape[1]), yb.dtype),
                            pltpu.SemaphoreType.DMA((2,))],
        ),
        compiler_params=_cparams(1, vmem),
        name="combine",
    )(slots_padded, x1, info, yb, p, g_ple.reshape(1, D), w_ple_gate, w_ple_proj, g_final.reshape(1, D))


def _rope_tables(T):
    half, BS = MOBA_HEAD_DIM // 2, MOBA_BLOCK
    inv_freq = ROPE_THETA ** (-jnp.arange(half, dtype=F32) / half)
    ang_a = (jnp.arange(T // BS, dtype=F32) * BS)[:, None, None] * inv_freq
    ang_b = jnp.arange(BS, dtype=F32)[None, :, None] * inv_freq
    ca, sa, cb, sb = jnp.cos(ang_a), jnp.sin(ang_a), jnp.cos(ang_b), jnp.sin(ang_b)
    cos = (ca * cb - sa * sb).reshape(T, half)
    sin = (sa * cb + ca * sb).reshape(T, half)
    return jnp.concatenate([cos, cos], axis=1), jnp.concatenate([-sin, sin], axis=1)


def _mixers(x2d, g_mix, w_in, lb, hgrn_norm_g):
    T = x2d.shape[0]
    W = HGRN_WIDTH
    w = w_in
    cos, sin = _rope_tables(T)
    log_lb = jnp.log(lb).reshape(1, W)
    log_1m = jnp.log1p(-lb).reshape(1, W)
    hq, h = _norm_proj(x2d, g_mix, w, 0 * W, W, _ep_silu, BF16)
    logf = _proj(h, w, 1 * W, W, _ep_logf, F32, col_extras=(log_lb, log_1m))
    hi = _proj(h, w, 2 * W, W, _ep_identity, BF16)
    hog = _proj(h, w, 3 * W, W, _ep_silu, BF16)
    scale = MOBA_HEAD_DIM ** -0.5
    mqt = _proj_t(h, w, 4 * W, functools.partial(_ep_rope, scale=scale), row_extras=(cos, sin))
    blk = jnp.arange(T, dtype=I32)[:, None] // MOBA_BLOCK
    blk_onehot = (blk == jnp.arange(V7X_LANES, dtype=I32)[None, :]).astype(F32)
    mk = _proj(h, w, 5 * W, W, _ep_rope_aug, BF16, row_extras=(cos, sin, blk_onehot), widen=2)
    mvt = _proj_t(h, w, 6 * W, _ep_identity, ones_rows=MOBA_VT_ROWS - MOBA_HEAD_DIM)
    gates = _proj(h, w, 7 * W, 2 * D_MODEL, _ep_sigmoid, BF16)
    o_hgrn = _hgrn(hq, logf, hi, hog, hgrn_norm_g)
    o_moba = _moba(mqt, mk, mvt)
    return o_hgrn, o_moba, gates


def _moe_plan(info, cnt, T):
    R = MOE_ROWS
    counts = cnt[0, :N_EXPERTS].astype(I32)
    padded = (counts + R - 1) // R * R
    pad_end = jnp.cumsum(padded)
    pad_start = pad_end - padded
    slots = _slots(info, pad_start)
    n_blocks = (T * TOPK_IN_GROUP) // R + N_EXPERTS
    block_row = jnp.arange(n_blocks, dtype=I32) * R
    block_expert = jnp.minimum(jnp.sum(pad_end[None, :] <= block_row[:, None], axis=1), N_EXPERTS - 1).astype(I32)
    has = counts > 0
    segment_expert = jnp.argsort(jnp.logical_not(has), stable=True).astype(I32)
    block_segment = (jnp.cumsum(has.astype(I32))[block_expert] - 1).astype(I32)
    n_used = jnp.stack([pad_end[-1] // R, jnp.sum(has.astype(I32))]).astype(I32)
    return slots, pad_end.astype(I32), block_segment, segment_expert, n_used, n_blocks * R


def kernel(x, p, norm_mix_g, w_in, hgrn_lb_raw, hgrn_norm_g, w_up_hgrn, w_up_moba, w_out, norm_ffn_g,
           w_router_group, w_router_expert, w_exp_gate, w_exp_up, w_exp_down, norm_ple_g, w_ple_gate,
           w_ple_proj, norm_final_g):
    B, T, D = x.shape
    assert B == 1 and D == D_MODEL and w_in.shape[0] == 1 and T % (4 * MOBA_BLOCK) == 0
    lower_bounds = jnp.cumsum(jax.nn.softmax(hgrn_lb_raw.astype(F32), axis=0), axis=0)
    x2d = x.reshape(T, D)
    o_hgrn, o_moba, gates = _mixers(x2d, norm_mix_g[0], w_in[0], lower_bounds[0], hgrn_norm_g[0])
    merged = _merge(o_hgrn, o_moba, gates, w_up_hgrn[0].astype(BF16), w_up_moba[0].astype(BF16))
    w_router = jnp.pad(jnp.concatenate([w_router_group[0], w_router_expert[0]], axis=1),
                       ((0, 0), (0, V7X_LANES - N_GROUPS - N_EXPERTS))).astype(BF16)
    x1, h2p, info, cnt = _outproj(merged, x2d, w_out[0].astype(BF16), norm_ffn_g[0], w_router)
    slots, pad_end, block_segment, segment_expert, n_used, m_pad = _moe_plan(info, cnt, T)
    xs = _dispatch(slots, pad_end, h2p, m_pad)
    yb = _experts(block_segment, segment_expert, n_used, xs, w_exp_gate[0], w_exp_up[0], w_exp_down[0])
    out = _combine(slots, x1, info, yb, p[0].reshape(T, PLE_DIM), norm_ple_g[0], w_ple_gate[0].astype(BF16),
                   w_ple_proj[0].astype(BF16), norm_final_g)
    return out.reshape(B, T, D)
```

```python
import functools

import jax
import jax.numpy as jnp
from jax import lax
from jax.experimental import pallas as pl
from jax.experimental.pallas import tpu as pltpu

F32 = jnp.float32
BF16 = jnp.bfloat16
I32 = jnp.int32

D_MODEL = 2048
PLE_DIM = 256
HGRN_HEADS = 8
HGRN_HEAD_DIM = 128
HGRN_WIDTH = HGRN_HEADS * HGRN_HEAD_DIM
MOBA_HEADS = 8
MOBA_HEAD_DIM = 128
MOBA_WIDTH = MOBA_HEADS * MOBA_HEAD_DIM
MOBA_BLOCK = 256
MOBA_TOPK = 3
ROPE_THETA = 10000.0
N_GROUPS = 4
EXPERTS_PER_GROUP = 8
N_EXPERTS = N_GROUPS * EXPERTS_PER_GROUP
TOPK_IN_GROUP = 2
D_EXPERT = 512
EPS = 1e-6
NEG_INF = -1e30

V7X_LANES = 128
V7X_SUBLANES = 8
V7X_VMEM_BUDGET_BYTES = 56 * 1024 * 1024

HGRN_CHUNK = 128
MOE_ROWS = 256


def _cparams(n_grid, vmem_bytes):
    return pltpu.CompilerParams(
        dimension_semantics=("arbitrary",) * n_grid,
        vmem_limit_bytes=int(min(max(vmem_bytes, 16 * 1024 * 1024), V7X_VMEM_BUDGET_BYTES)),
    )


def _nbytes(shape, dtype):
    n = 1
    for s in shape:
        n *= s
    return n * jnp.dtype(dtype).itemsize


def _rms(x, g):
    ms = jnp.mean(x * x, axis=-1, keepdims=True)
    return x * lax.rsqrt(ms + EPS) * g


def _ep_identity(acc):
    return acc


def _ep_silu(acc):
    return acc * jax.nn.sigmoid(acc)


def _ep_sigmoid(acc):
    return jax.nn.sigmoid(acc)


def _ep_logf(acc, la_ref, lc_ref):
    ls = jnp.minimum(acc, 0.0) - jnp.log(1.0 + jnp.exp(-jnp.abs(acc)))
    u = la_ref[...]
    v = lc_ref[...] + ls
    return jnp.maximum(u, v) + jnp.log(1.0 + jnp.exp(-jnp.abs(u - v)))


def _ep_rope(acc, cos_ref, sin_ref, *, scale):
    cos = cos_ref[...]
    sin = sin_ref[...]
    outs = []
    for hh in range(acc.shape[1] // MOBA_HEAD_DIM):
        a = acc[:, hh * MOBA_HEAD_DIM:(hh + 1) * MOBA_HEAD_DIM]
        r = pltpu.roll(a, MOBA_HEAD_DIM // 2, axis=1)
        outs.append((a * cos + r * sin) * scale)
    return jnp.concatenate(outs, axis=1)


def _cast_weight_once(w_ref, wb_ref, row_axis):
    @pl.when(pl.program_id(row_axis) == 0)
    def _():
        wb_ref[...] = w_ref[...].astype(wb_ref.dtype)


def _proj_kernel(h_ref, w_ref, *refs, epilogue):
    *extra, o_ref, wb_ref = refs
    _cast_weight_once(w_ref, wb_ref, 1)
    acc = jnp.dot(h_ref[...], wb_ref[...], preferred_element_type=F32)
    o_ref[...] = epilogue(acc, *extra).astype(o_ref.dtype)


def _proj(h, w, col0, ncols, epilogue, out_dtype, row_extras=(), col_extras=(), tm=1024, tn=1024, widen=1):
    T, K = h.shape
    tn = min(tn, ncols)
    tm = min(tm, T)
    cb = col0 // tn
    otn = widen * tn
    in_specs = [
        pl.BlockSpec((tm, K), lambda j, i: (i, 0)),
        pl.BlockSpec((K, tn), lambda j, i: (0, cb + j)),
    ]
    for e in row_extras:
        in_specs.append(pl.BlockSpec((tm, e.shape[1]), lambda j, i: (i, 0)))
    for e in col_extras:
        in_specs.append(pl.BlockSpec((1, tn), lambda j, i: (0, j)))
    vmem = 2 * (_nbytes((tm, K), h.dtype) + _nbytes((K, tn), w.dtype) + _nbytes((tm, otn), out_dtype))
    vmem += 3 * _nbytes((tm, otn), F32) + _nbytes((K, tn), BF16)
    return pl.pallas_call(
        functools.partial(_proj_kernel, epilogue=epilogue),
        out_shape=jax.ShapeDtypeStruct((T, widen * ncols), out_dtype),
        grid=(ncols // tn, T // tm),
        in_specs=in_specs,
        out_specs=pl.BlockSpec((tm, otn), lambda j, i: (i, j)),
        scratch_shapes=[pltpu.VMEM((K, tn), BF16)],
        compiler_params=_cparams(2, vmem),
        name="proj",
    )(h, w, *row_extras, *col_extras)


def _norm_proj_kernel(x_ref, g_ref, w_ref, o_ref, h_ref, wb_ref, *, epilogue):
    _cast_weight_once(w_ref, wb_ref, 0)
    h = _rms(x_ref[...], g_ref[...]).astype(h_ref.dtype)
    h_ref[...] = h
    o_ref[...] = epilogue(jnp.dot(h, wb_ref[...], preferred_element_type=F32)).astype(o_ref.dtype)


def _norm_proj(x, g, w, col0, ncols, epilogue, out_dtype, tm=512):
    T, K = x.shape
    cb = col0 // ncols
    vmem = 2 * (_nbytes((tm, K), F32) + _nbytes((tm, ncols), out_dtype) + _nbytes((tm, K), BF16))
    vmem += _nbytes((K, ncols), F32) + _nbytes((K, ncols), BF16) + 2 * _nbytes((tm, K), F32) + 3 * _nbytes((tm, ncols), F32)
    return pl.pallas_call(
        functools.partial(_norm_proj_kernel, epilogue=epilogue),
        out_shape=(jax.ShapeDtypeStruct((T, ncols), out_dtype), jax.ShapeDtypeStruct((T, K), BF16)),
        grid=(T // tm,),
        in_specs=[
            pl.BlockSpec((tm, K), lambda i: (i, 0)),
            pl.BlockSpec((1, K), lambda i: (0, 0)),
            pl.BlockSpec((K, ncols), lambda i: (0, cb), pipeline_mode=pl.Buffered(1)),
        ],
        out_specs=(pl.BlockSpec((tm, ncols), lambda i: (i, 0)), pl.BlockSpec((tm, K), lambda i: (i, 0))),
        scratch_shapes=[pltpu.VMEM((K, ncols), BF16)],
        compiler_params=_cparams(1, vmem),
        name="norm_proj",
    )(x, g.reshape(1, K), w)


MOBA_VT_ROWS = MOBA_HEAD_DIM + 16
MOBA_BLOCKS_PER_STEP = 4


def _proj_t_kernel(h_ref, w_ref, *refs, epilogue, ones_rows):
    *extra, o_ref, wb_ref = refs
    BS, HD = MOBA_BLOCK, MOBA_HEAD_DIM
    _cast_weight_once(w_ref, wb_ref, 0)
    acc = epilogue(jnp.dot(h_ref[...], wb_ref[...], preferred_element_type=F32), *extra)
    ones = jnp.ones((ones_rows, BS), F32) if ones_rows else None
    for b in range(acc.shape[0] // BS):
        parts = []
        for hh in range(acc.shape[1] // HD):
            parts.append(acc[b * BS:(b + 1) * BS, hh * HD:(hh + 1) * HD].T)
            if ones_rows:
                parts.append(ones)
        o_ref[b] = jnp.concatenate(parts, axis=0).astype(o_ref.dtype)


def _proj_t(h, w, col0, epilogue, row_extras=(), ones_rows=0, tm=1024):
    T, K = h.shape
    tn = MOBA_WIDTH
    tm = min(tm, T)
    cb = col0 // tn
    rows = MOBA_HEADS * (MOBA_HEAD_DIM + ones_rows)
    in_specs = [pl.BlockSpec((tm, K), lambda i: (i, 0)), pl.BlockSpec((K, tn), lambda i: (0, cb))]
    for e in row_extras:
        in_specs.append(pl.BlockSpec((tm, e.shape[1]), lambda i: (i, 0)))
    vmem = 2 * (_nbytes((tm, K), h.dtype) + _nbytes((K, tn), w.dtype) + _nbytes((tm, 2 * tn), BF16))
    vmem += 4 * _nbytes((tm, tn), F32) + _nbytes((K, tn), BF16)
    return pl.pallas_call(
        functools.partial(_proj_t_kernel, epilogue=epilogue, ones_rows=ones_rows),
        out_shape=jax.ShapeDtypeStruct((T // MOBA_BLOCK, rows, MOBA_BLOCK), BF16),
        grid=(T // tm,),
        in_specs=in_specs,
        out_specs=pl.BlockSpec((tm // MOBA_BLOCK, rows, MOBA_BLOCK), lambda i: (i, 0, 0)),
        scratch_shapes=[pltpu.VMEM((K, tn), BF16)],
        compiler_params=_cparams(1, vmem),
        name="proj_t",
    )(h, w, *row_extras)


def _hgrn_kernel(q_ref, g_ref, v_ref, og_ref, ng_ref, o_ref, st_ref, code_ref):
    W, HD, NH, C, S = HGRN_WIDTH, HGRN_HEAD_DIM, HGRN_HEADS, HGRN_CHUNK, V7X_SUBLANES
    J = C // S

    @pl.when(pl.program_id(0) == 0)
    def _():
        st_ref[...] = jnp.zeros_like(st_ref)
        tr = lax.broadcasted_iota(I32, (C, C), 0)
        tc = lax.broadcasted_iota(I32, (C, C), 1)
        xr = tr ^ tc
        code = jnp.zeros((C, C), I32)
        for lvl in range(1, 8):
            code = jnp.where(xr >= (1 << (lvl - 1)), lvl, code)
        code_ref[...] = jnp.where(tc > tr, -1, code)

    def r3(x):
        return x.astype(F32).reshape(J, S, W)

    def sub_bcast(x3, r):
        return jnp.broadcast_to(x3[:, r:r + 1, :], x3.shape)

    g3, q3, v3 = r3(g_ref[...]), r3(q_ref[...]), r3(v_ref[...])
    sub = lax.broadcasted_iota(I32, (1, S, W), 1)

    c3 = g3
    for s in (1, 2, 4):
        c3 = c3 + jnp.where(sub >= s, pltpu.roll(c3, s, axis=1), 0.0)
    run = jnp.zeros((1, 1, W), F32)
    carry = []
    for j in range(J):
        carry.append(run)
        run = run + c3[j:j + 1, S - 1:S, :]
    b3 = c3 + jnp.concatenate(carry, axis=0)
    bC = run

    k3 = 1.0 - jnp.exp(g3)
    qe3 = q3 * jnp.exp(b3)
    ks3 = k3 * jnp.exp(bC - b3)

    levels = [(0, q3, k3)]
    ref1 = jnp.where(sub % 2 == 0, b3, pltpu.roll(b3, 1, axis=1))
    ref2 = jnp.where(sub < 4, sub_bcast(b3, 1), sub_bcast(b3, 5))
    ref4 = sub_bcast(b3, 3)
    for lvl, (ref, upper) in enumerate(((ref1, sub % 2 == 1), (ref2, sub % 4 >= 2), (ref4, sub >= 4)), start=1):
        e = jnp.exp(-jnp.abs(b3 - ref))
        levels.append((lvl, jnp.where(upper, q3 * e, 0.0), jnp.where(upper, 0.0, k3 * e)))
    zero_group = jnp.zeros((1, S, W), F32)
    for lvl, half in enumerate((1, 2, 4, 8), start=4):
        qparts, kparts = [], []
        for j in range(J):
            jr = (j // (2 * half)) * (2 * half) + half - 1
            ref = b3[jr:jr + 1, S - 1:S, :]
            if (j % (2 * half)) >= half:
                qparts.append(q3[j:j + 1] * jnp.exp(b3[j:j + 1] - ref))
                kparts.append(zero_group)
            else:
                qparts.append(zero_group)
                kparts.append(k3[j:j + 1] * jnp.exp(ref - b3[j:j + 1]))
        levels.append((lvl, jnp.concatenate(qparts, axis=0), jnp.concatenate(kparts, axis=0)))

    code = code_ref[...]

    def mat(x3, h):
        return x3.reshape(C, W)[:, h * HD:(h + 1) * HD].astype(BF16)

    nt = (((1,), (1,)), ((), ()))
    tn = (((0,), (0,)), ((), ()))
    ebc = jnp.exp(bC).reshape(1, W)
    ng = ng_ref[...]
    for h in range(NH):
        a_mat = jnp.zeros((C, C), F32)
        for lvl, qr, kr in levels:
            s = lax.dot_general(mat(qr, h), mat(kr, h), nt, preferred_element_type=F32)
            a_mat = jnp.where(code == lvl, s, a_mat)
        vh = mat(v3, h)
        st = st_ref[h]
        o = jnp.dot(a_mat.astype(BF16), vh, preferred_element_type=F32)
        o = o + lax.dot_general(mat(qe3, h), st.astype(BF16), nt, preferred_element_type=F32)
        o = _rms(o, ng) * og_ref[:, h * HD:(h + 1) * HD].astype(F32)
        o_ref[:, h * HD:(h + 1) * HD] = o.astype(o_ref.dtype)
        st_ref[h] = st * ebc[:, h * HD:(h + 1) * HD] + lax.dot_general(
            vh, mat(ks3, h), tn, preferred_element_type=F32)


def _hgrn(q, logf, v, og, norm_g):
    T, W = q.shape
    C = HGRN_CHUNK
    blk = pl.BlockSpec((C, W), lambda c: (c, 0))
    vmem = 64 * _nbytes((C, W), F32)
    return pl.pallas_call(
        _hgrn_kernel,
        out_shape=jax.ShapeDtypeStruct((T, W), BF16),
        grid=(T // C,),
        in_specs=[blk, blk, blk, blk, pl.BlockSpec((1, HGRN_HEAD_DIM), lambda c: (0, 0))],
        out_specs=blk,
        scratch_shapes=[pltpu.VMEM((HGRN_HEADS, HGRN_HEAD_DIM, HGRN_HEAD_DIM), F32), pltpu.VMEM((C, C), I32)],
        compiler_params=_cparams(1, vmem),
        name="hgrn",
    )(q, logf, v, og, norm_g.reshape(1, HGRN_HEAD_DIM))


def _moba_kernel(qt_ref, k_ref, oh_ref, vt_ref, o_ref, km_ref):
    BS, HD, VR = MOBA_BLOCK, MOBA_HEAD_DIM, MOBA_VT_ROWS
    T = k_ref.shape[0]
    NB = T // BS
    G = o_ref.shape[1] // HD
    cur = pl.program_id(1)

    @pl.when(cur == 0)
    def _():
        for g in range(G):
            kf = k_ref[:, g * HD:(g + 1) * HD].astype(F32).reshape(NB, BS, HD)
            km_ref[g] = jnp.sum(kf, axis=1) * (1.0 / BS)

    blk = lax.broadcasted_iota(I32, (NB, BS), 0)
    pad = jnp.zeros((V7X_LANES - NB, BS), F32)
    qts = [qt_ref[0, g * HD:(g + 1) * HD, :] for g in range(G)]
    gts = [jnp.dot(km_ref[g].astype(BF16), qts[g], preferred_element_type=F32) for g in range(G)]
    qcs = []
    for g in range(G):
        gt = jnp.where(blk < cur, gts[g], NEG_INF)
        sel = blk == cur
        for _ in range(MOBA_TOPK):
            mx = jnp.max(gt, axis=0, keepdims=True)
            idx = jnp.min(jnp.where(gt == mx, blk, NB), axis=0, keepdims=True)
            pick = (blk == idx) & (mx > 0.5 * NEG_INF)
            sel = sel | pick
            gt = jnp.where(pick, NEG_INF, gt)
        pen = jnp.concatenate([jnp.where(sel, 0.0, NEG_INF), pad], axis=0).astype(BF16)
        qcs.append(jnp.concatenate([qts[g], pen], axis=0))

    KB = MOBA_BLOCKS_PER_STEP

    def body(blk0, carry, own_step=False, nblk=KB):
        ms, accs = carry
        r = pl.multiple_of(blk0 * BS, BS)
        oh = oh_ref[pl.ds(r, nblk * BS), :]
        sns = [jnp.dot(jnp.concatenate([k_ref[pl.ds(r, nblk * BS), g * HD:(g + 1) * HD], oh], axis=1), qcs[g],
                       preferred_element_type=F32) for g in range(G)]
        if own_step:
            krow = lax.broadcasted_iota(I32, (nblk * BS, BS), 0)
            qcol = lax.broadcasted_iota(I32, (nblk * BS, BS), 1)
            keep = (blk0 + krow // BS != cur) | (krow % BS <= qcol)
            sns = [jnp.where(keep, s, NEG_INF) for s in sns]
        new_ms, alphas, pns = [], [], []
        for g in range(G):
            m_new = jnp.maximum(ms[g], jnp.max(sns[g], axis=0, keepdims=True))
            alphas.append(jnp.exp(ms[g] - m_new))
            pns.append(jnp.exp(sns[g] - m_new).astype(BF16))
            new_ms.append(m_new)
        new_accs = []
        for g in range(G):
            pv = alphas[g] * accs[g]
            for j in range(nblk):
                pv = pv + jnp.dot(vt_ref[blk0 + j, g * VR:(g + 1) * VR, :], pns[g][j * BS:(j + 1) * BS],
                                  preferred_element_type=F32)
            new_accs.append(pv)
        return tuple(new_ms), tuple(new_accs)

    ms = tuple(jnp.full((1, BS), NEG_INF, F32) for _ in range(G))
    accs = tuple(jnp.zeros((VR, BS), F32) for _ in range(G))
    carry = lax.fori_loop(0, cur // (2 * KB), lambda c, cr: body(c * 2 * KB, cr, nblk=2 * KB), (ms, accs))
    last = (cur // KB) * KB
    carry = lax.cond(cur % (2 * KB) >= KB, lambda cr: body(last - KB, cr), lambda cr: cr, carry)
    _, accs = lax.cond(
        cur % KB < KB // 2,
        lambda cr: body(last, cr, own_step=True, nblk=KB // 2),
        lambda cr: body(last, cr, own_step=True),
        carry)
    for g in range(G):
        ot = accs[g][:HD, :] / accs[g][HD:HD + 1, :]
        o_ref[:, g * HD:(g + 1) * HD] = ot.T.astype(o_ref.dtype)


def _moba(mqt, mk, block_onehot, mvt, heads_per_step=4):
    T = mk.shape[0]
    BS, HD, G, VR = MOBA_BLOCK, MOBA_HEAD_DIM, heads_per_step, MOBA_VT_ROWS
    NB = T // BS
    vmem = _nbytes((T, (G + 1) * HD), BF16) + _nbytes((NB, G * VR, BS), BF16) + 8 * _nbytes((BS, G * HD), BF16)
    vmem += 40 * G * _nbytes((BS, BS), F32)
    resident = pl.Buffered(1)
    return pl.pallas_call(
        _moba_kernel,
        out_shape=jax.ShapeDtypeStruct((T, MOBA_WIDTH), BF16),
        grid=(MOBA_HEADS // G, NB),
        in_specs=[
            pl.BlockSpec((1, G * HD, BS), lambda h, i: (i, h, 0)),
            pl.BlockSpec((T, G * HD), lambda h, i: (0, h), pipeline_mode=resident),
            pl.BlockSpec((T, V7X_LANES), lambda h, i: (0, 0), pipeline_mode=resident),
            pl.BlockSpec((NB, G * VR, BS), lambda h, i: (0, h, 0), pipeline_mode=resident),
        ],
        out_specs=pl.BlockSpec((BS, G * HD), lambda h, i: (i, h)),
        scratch_shapes=[pltpu.VMEM((G, NB, HD), F32)],
        compiler_params=_cparams(2, vmem),
        name="moba",
    )(mqt, mk, block_onehot, mvt)


def _merge_kernel(oh_ref, om_ref, ga_ref, gb_ref, wh_ref, wm_ref, o_ref):
    a = jnp.dot(oh_ref[...], wh_ref[...], preferred_element_type=F32)
    b = jnp.dot(om_ref[...], wm_ref[...], preferred_element_type=F32)
    o_ref[...] = (ga_ref[...].astype(F32) * a + gb_ref[...].astype(F32) * b).astype(o_ref.dtype)


def _merge(o_hgrn, o_moba, gates, w_up_hgrn, w_up_moba, tm=512):
    T = o_hgrn.shape[0]
    D = D_MODEL
    vmem = 2 * (_nbytes((tm, HGRN_WIDTH), F32) + _nbytes((tm, MOBA_WIDTH), BF16) + 3 * _nbytes((tm, D), BF16)
                + 2 * _nbytes((HGRN_WIDTH, D), BF16)) + 3 * _nbytes((tm, D), F32)
    return pl.pallas_call(
        _merge_kernel,
        out_shape=jax.ShapeDtypeStruct((T, D), BF16),
        grid=(T // tm,),
        in_specs=[
            pl.BlockSpec((tm, HGRN_WIDTH), lambda i: (i, 0)),
            pl.BlockSpec((tm, MOBA_WIDTH), lambda i: (i, 0)),
            pl.BlockSpec((tm, D), lambda i: (i, 0)),
            pl.BlockSpec((tm, D), lambda i: (i, 1)),
            pl.BlockSpec((HGRN_WIDTH, D), lambda i: (0, 0)),
            pl.BlockSpec((MOBA_WIDTH, D), lambda i: (0, 0)),
        ],
        out_specs=pl.BlockSpec((tm, D), lambda i: (i, 0)),
        compiler_params=_cparams(1, vmem),
        name="merge",
    )(o_hgrn, o_moba, gates, gates, w_up_hgrn, w_up_moba)


def _pack_bf16_pairs(x):
    C = x.shape[1] // 2
    b = lax.bitcast_convert_type(x, jnp.uint32)
    r = (b + jnp.uint32(0x7FFF) + ((b >> 16) & jnp.uint32(1))) >> 16
    return r[:, :C] | (r[:, C:] << 16)


def _unpack_bf16_pairs(p):
    lo = lax.bitcast_convert_type(p << 16, F32)
    hi = lax.bitcast_convert_type(p & jnp.uint32(0xFFFF0000), F32)
    return jnp.concatenate([lo, hi], axis=1)


def _outproj_kernel(m_ref, x_ref, w_ref, g_ref, wr_ref, x1_ref, h2p_ref, info_ref, cnt_ref, prev_ref, carry_ref):
    step = pl.program_id(0)
    n_tiles = pl.num_programs(0) - 1

    @pl.when(step == 0)
    def _():
        carry_ref[...] = jnp.zeros_like(carry_ref)

    def finish_previous():
        h2 = _rms(prev_ref[...], g_ref[...])
        h2p_ref[...] = _pack_bf16_pairs(h2)
        info_ref[...] = _route(h2, wr_ref, carry_ref)
        cnt_ref[...] = carry_ref[...]

    def project():
        return x_ref[...] + jnp.dot(m_ref[...], w_ref[...], preferred_element_type=F32)

    @pl.when(step == 0)
    def _():
        x1 = project()
        x1_ref[...] = x1
        prev_ref[...] = x1

    @pl.when((step > 0) & (step < n_tiles))
    def _():
        x1 = project()
        finish_previous()
        x1_ref[...] = x1
        prev_ref[...] = x1

    @pl.when(step == n_tiles)
    def _():
        finish_previous()


def _outproj(merged, x, w_out, g_ffn, w_router, tm=512):
    T, D = x.shape
    n_tiles = T // tm
    vmem = 2 * (_nbytes((tm, D), BF16) + 3 * _nbytes((tm, D), F32) + _nbytes((D, D), BF16)) + 7 * _nbytes((tm, D), F32)
    cur = lambda i: (jnp.minimum(i, n_tiles - 1), 0)
    prev = lambda i: (jnp.maximum(i - 1, 0), 0)
    fixed = lambda i: (0, 0)
    return pl.pallas_call(
        _outproj_kernel,
        out_shape=(jax.ShapeDtypeStruct((T, D), F32), jax.ShapeDtypeStruct((T, D // 2), jnp.uint32),
                   jax.ShapeDtypeStruct((T, V7X_LANES), F32), jax.ShapeDtypeStruct((1, V7X_LANES), F32)),
        grid=(n_tiles + 1,),
        in_specs=[
            pl.BlockSpec((tm, D), cur),
            pl.BlockSpec((tm, D), cur),
            pl.BlockSpec((D, D), fixed, pipeline_mode=pl.Buffered(1)),
            pl.BlockSpec((1, D), fixed),
            pl.BlockSpec((D, V7X_LANES), fixed),
        ],
        out_specs=(pl.BlockSpec((tm, D), cur), pl.BlockSpec((tm, D // 2), prev), pl.BlockSpec((tm, V7X_LANES), prev),
                   pl.BlockSpec((1, V7X_LANES), fixed)),
        scratch_shapes=[pltpu.VMEM((tm, D), F32), pltpu.VMEM((1, V7X_LANES), F32)],
        compiler_params=_cparams(1, vmem),
        name="outproj",
    )(merged, x, w_out, g_ffn.reshape(1, D), w_router)


def _route(h2, w_ref, carry_ref):
    tm = h2.shape[0]
    logits = jnp.dot(h2.astype(BF16), w_ref[...], preferred_element_type=F32)
    lane = lax.broadcasted_iota(I32, (tm, V7X_LANES), 1)
    is_g = lane < N_GROUPS
    gl = jnp.where(is_g, logits, NEG_INF)
    gmax = jnp.max(gl, axis=1, keepdims=True)
    g_sel = jnp.min(jnp.where(gl == gmax, lane, V7X_LANES), axis=1, keepdims=True)
    gsum = jnp.sum(jnp.where(is_g, jnp.exp(gl - gmax), 0.0), axis=1, keepdims=True)
    p_group = 1.0 / gsum
    lo = N_GROUPS + EXPERTS_PER_GROUP * g_sel
    emask = (lane >= lo) & (lane < lo + EXPERTS_PER_GROUP)
    el = jnp.where(emask, logits, NEG_INF)
    e1 = jnp.max(el, axis=1, keepdims=True)
    i1 = jnp.min(jnp.where((el == e1) & emask, lane, V7X_LANES), axis=1, keepdims=True)
    emask2 = emask & (lane != i1)
    el2 = jnp.where(emask2, logits, NEG_INF)
    e2 = jnp.max(el2, axis=1, keepdims=True)
    i2 = jnp.min(jnp.where((el2 == e2) & emask2, lane, V7X_LANES), axis=1, keepdims=True)
    r = jnp.exp(e2 - e1)
    w1 = p_group / (1.0 + r)
    w2 = p_group * r / (1.0 + r)
    eid1 = i1 - N_GROUPS
    eid2 = i2 - N_GROUPS
    oh1 = jnp.where(lane == eid1, 1.0, 0.0)
    oh2 = jnp.where(lane == eid2, 1.0, 0.0)
    cnt = oh1 + oh2
    tri = jnp.where(lax.broadcasted_iota(I32, (tm, tm), 0) > lax.broadcasted_iota(I32, (tm, tm), 1), 1.0, 0.0)
    before = jnp.dot(tri.astype(BF16), cnt.astype(BF16), preferred_element_type=F32) + carry_ref[...]
    rank1 = jnp.sum(oh1 * before, axis=1, keepdims=True)
    rank2 = jnp.sum(oh2 * before, axis=1, keepdims=True)
    carry_ref[...] = carry_ref[...] + jnp.sum(cnt, axis=0, keepdims=True)
    info = jnp.zeros((tm, V7X_LANES), F32)
    for k, val in enumerate((eid1.astype(F32), eid2.astype(F32), w1, w2, rank1, rank2)):
        info = jnp.where(lane == k, val, info)
    return info


def _row_copy(src_ref, src_row, dst_ref, dst_row, sem):
    return pltpu.make_async_copy(src_ref.at[pl.ds(src_row, 1), :], dst_ref.at[pl.ds(dst_row, 1), :], sem)


ROW_DMA_UNROLL = 8


def _slots_kernel(info_ref, ps_ref, o_ref):
    info = info_ref[...]
    lane = lax.broadcasted_iota(I32, info.shape, 1)
    lane_f = lane.astype(F32)
    ps = ps_ref[...]
    out = jnp.zeros(info.shape, F32)
    for k in range(TOPK_IN_GROUP):
        start = jnp.sum(jnp.where(lane_f == info[:, k:k + 1], ps, 0.0), axis=1, keepdims=True)
        out = jnp.where(lane == k, start + info[:, 4 + k:5 + k], out)
    o_ref[...] = out.astype(I32)


def _slots(info, pad_start, tm=1024):
    T = info.shape[0]
    ps = jnp.pad(pad_start.astype(F32), (0, V7X_LANES - N_EXPERTS)).reshape(1, V7X_LANES)
    out = pl.pallas_call(
        _slots_kernel,
        out_shape=jax.ShapeDtypeStruct((T, V7X_LANES), I32),
        grid=(T // tm,),
        in_specs=[pl.BlockSpec((tm, V7X_LANES), lambda i: (i, 0)), pl.BlockSpec((1, V7X_LANES), lambda i: (0, 0))],
        out_specs=pl.BlockSpec((tm, V7X_LANES), lambda i: (i, 0)),
        compiler_params=_cparams(1, 16 * _nbytes((tm, V7X_LANES), F32)),
        name="slots",
    )(info, ps)
    return out[:, :TOPK_IN_GROUP].reshape(-1)


def _dispatch_kernel(slot_ref, pend_ref, h_ref, xs_ref, zero_ref, sem):
    tm = h_ref.shape[0]
    step = pl.program_id(0)
    base = step * tm

    @pl.when(step == 0)
    def _():
        zero_ref[...] = jnp.zeros_like(zero_ref)

        def tail(e):
            return pltpu.make_async_copy(
                zero_ref, xs_ref.at[pl.ds(pl.multiple_of(pend_ref[e] - MOE_ROWS, MOE_ROWS), MOE_ROWS), :], sem)

        def nonempty(e):
            return pend_ref[e] > (pend_ref[e - 1] if e else 0)

        def unused(b):
            return pltpu.make_async_copy(
                zero_ref, xs_ref.at[pl.ds(pl.multiple_of(b * MOE_ROWS, MOE_ROWS), MOE_ROWS), :], sem)

        first_unused = pend_ref[N_EXPERTS - 1] // MOE_ROWS
        n_blocks = xs_ref.shape[0] // MOE_ROWS
        for e in range(N_EXPERTS):
            pl.when(nonempty(e))(lambda e=e: tail(e).start())
        lax.fori_loop(first_unused, n_blocks, lambda b, c: (unused(b).start(), c)[1], 0)
        for e in range(N_EXPERTS):
            pl.when(nonempty(e))(lambda e=e: tail(e).wait())
        lax.fori_loop(first_unused, n_blocks, lambda b, c: (unused(b).wait(), c)[1], 0)

    def issue(rb, c):
        for u in range(ROW_DMA_UNROLL):
            r = rb * ROW_DMA_UNROLL + u
            for k in range(TOPK_IN_GROUP):
                _row_copy(h_ref, r, xs_ref, slot_ref[(base + r) * TOPK_IN_GROUP + k], sem).start(priority=k % 2)
        return c

    lax.fori_loop(0, tm // ROW_DMA_UNROLL, issue, 0)
    for _ in range(tm * TOPK_IN_GROUP):
        _row_copy(h_ref, 0, xs_ref, 0, sem).wait()


def _dispatch(slots, pad_end, h2p, m_pad, tm=256):
    T, C = h2p.shape
    return pl.pallas_call(
        _dispatch_kernel,
        out_shape=jax.ShapeDtypeStruct((m_pad, C), h2p.dtype),
        grid_spec=pltpu.PrefetchScalarGridSpec(
            num_scalar_prefetch=2,
            grid=(T // tm,),
            in_specs=[pl.BlockSpec((tm, C), lambda i, sl, pe: (i, 0))],
            out_specs=pl.BlockSpec(memory_space=pl.ANY),
            scratch_shapes=[pltpu.VMEM((MOE_ROWS, C), h2p.dtype), pltpu.SemaphoreType.DMA(())],
        ),
        compiler_params=_cparams(1, 6 * _nbytes((tm, C), h2p.dtype)),
        name="dispatch",
    )(slots, pad_end, h2p)


def _expert_kernel(seg_ref, sexp_ref, nu_ref, xs_ref, wg_ref, wu_ref, wd_ref, y_ref, wgf_ref, wuf_ref, wdf_ref,
                   wgb_ref, wub_ref, wdb_ref, sem):
    b = pl.program_id(0)
    used = b < nu_ref[0]
    seg = seg_ref[b]
    first = used & ((b == 0) | (seg != seg_ref[jnp.maximum(b - 1, 0)]))

    def weights_copy(s, part):
        slot = s % 2
        src, dst = ((wg_ref, wgf_ref), (wu_ref, wuf_ref), (wd_ref, wdf_ref))[part]
        return pltpu.make_async_copy(src.at[sexp_ref[s]], dst.at[slot], sem.at[slot])

    @pl.when(b == 0)
    def _():
        for part in range(3):
            weights_copy(0, part).start()

    @pl.when(first)
    def _():
        for part in range(3):
            weights_copy(seg, part).wait()
        slot = seg % 2
        wgb_ref[...] = wgf_ref[slot].astype(BF16)
        wub_ref[...] = wuf_ref[slot].astype(BF16)
        wdb_ref[...] = wdf_ref[slot].astype(BF16)

    @pl.when(first & (seg + 1 < nu_ref[1]))
    def _():
        for part in range(3):
            weights_copy(seg + 1, part).start()

    @pl.when(used)
    def _():
        x = _unpack_bf16_pairs(xs_ref[...]).astype(BF16)
        a = jnp.dot(x, wgb_ref[...], preferred_element_type=F32)
        u = jnp.dot(x, wub_ref[...], preferred_element_type=F32)
        hm = (a * jax.nn.sigmoid(a) * u).astype(BF16)
        y_ref[...] = _pack_bf16_pairs(jnp.dot(hm, wdb_ref[...], preferred_element_type=F32))

    @pl.when(jnp.logical_not(used))
    def _():
        y_ref[...] = jnp.zeros_like(y_ref)


def _experts(block_segment, segment_expert, n_used, xs, w_gate, w_up, w_down):
    m_pad, C = xs.shape
    D = 2 * C
    R, Fd = MOE_ROWS, D_EXPERT
    vmem = 4 * _nbytes((R, C), xs.dtype) + 6 * _nbytes((D, Fd), F32) + 3 * _nbytes((D, Fd), BF16)
    vmem += 6 * _nbytes((R, D), F32)
    hbm = pl.BlockSpec(memory_space=pl.ANY)
    return pl.pallas_call(
        _expert_kernel,
        out_shape=jax.ShapeDtypeStruct((m_pad, C), xs.dtype),
        grid_spec=pltpu.PrefetchScalarGridSpec(
            num_scalar_prefetch=3,
            grid=(m_pad // R,),
            in_specs=[
                pl.BlockSpec((R, C), lambda b, sg, se, nu: (jnp.minimum(b, jnp.maximum(nu[0] - 1, 0)), 0)),
                hbm, hbm, hbm,
            ],
            out_specs=pl.BlockSpec((R, C), lambda b, sg, se, nu: (b, 0)),
            scratch_shapes=[pltpu.VMEM((2, D, Fd), F32), pltpu.VMEM((2, D, Fd), F32), pltpu.VMEM((2, Fd, D), F32),
                            pltpu.VMEM((D, Fd), BF16), pltpu.VMEM((D, Fd), BF16), pltpu.VMEM((Fd, D), BF16),
                            pltpu.SemaphoreType.DMA((2,))],
        ),
        compiler_params=_cparams(1, vmem),
        name="experts",
    )(block_segment, segment_expert, n_used, xs, w_gate, w_up, w_down)


def _combine_kernel(slot_ref, x1_ref, info_ref, yb_ref, p_ref, gp_ref, wpg_ref, wpp_ref, gf_ref, o_ref, ybuf, sem):
    tm, D = x1_ref.shape
    step = pl.program_id(0)
    n_tiles = pl.num_programs(0) - 2
    K = TOPK_IN_GROUP
    GROUPS = 8

    def fetch(tile, group=None):
        slot = tile % 2
        rows = range(tm) if group is None else range(group * tm // GROUPS, (group + 1) * tm // GROUPS)
        for r in rows:
            for k in range(K):
                _row_copy(yb_ref, slot_ref[(tile * tm + r) * K + k], ybuf.at[slot * K + k], r,
                          sem.at[slot]).start(priority=k % 2)

    def drain(tile):
        slot = tile % 2
        for _ in range(tm * K):
            _row_copy(yb_ref, 0, ybuf.at[0], 0, sem.at[slot]).wait()

    @pl.when(step == 0)
    def _():
        fetch(step)

    @pl.when(step > 0)
    def _():
        drain(step - 1)

    @pl.when((step > 0) & (step <= n_tiles))
    def _():
        tile = step - 1
        slot = tile % 2
        info = info_ref[...]
        x2 = (x1_ref[...] + info[:, 2:3] * _unpack_bf16_pairs(ybuf[slot * K])
              + info[:, 3:4] * _unpack_bf16_pairs(ybuf[slot * K + 1]))
        hp = _rms(x2, gp_ref[...]).astype(BF16)
        pp = jnp.dot(p_ref[...].astype(BF16), wpp_ref[...], preferred_element_type=F32)
        cw = D // GROUPS
        zs = []
        for c in range(GROUPS):
            fetch(step, c)
            zs.append(jnp.dot(hp, wpg_ref[:, c * cw:(c + 1) * cw], preferred_element_type=F32))
        x3 = x2 + jax.nn.sigmoid(jnp.concatenate(zs, axis=1)) * pp
        o_ref[...] = _rms(x3, gf_ref[...])


def _combine(slots, x1, info, yb, p, g_ple, w_ple_gate, w_ple_proj, g_final, tm=256):
    T, D = x1.shape
    n_tiles = T // tm
    slots_padded = jnp.concatenate([slots, jnp.zeros((tm * TOPK_IN_GROUP,), I32)])
    vmem = 4 * _nbytes((tm, D), F32) + _nbytes((D, D), BF16) + _nbytes((PLE_DIM, D), BF16)
    vmem += 2 * _nbytes((tm, PLE_DIM), F32) + 2 * TOPK_IN_GROUP * _nbytes((tm, D // 2), yb.dtype)
    vmem += 6 * _nbytes((tm, D), F32)
    tile = lambda i, sl: (jnp.clip(i - 1, 0, n_tiles - 1), 0)
    fixed = lambda i, sl: (0, 0)
    resident = pl.Buffered(1)
    return pl.pallas_call(
        _combine_kernel,
        out_shape=jax.ShapeDtypeStruct((T, D), F32),
        grid_spec=pltpu.PrefetchScalarGridSpec(
            num_scalar_prefetch=1,
            grid=(n_tiles + 2,),
            in_specs=[
                pl.BlockSpec((tm, D), tile),
                pl.BlockSpec((tm, V7X_LANES), tile),
                pl.BlockSpec(memory_space=pl.ANY),
                pl.BlockSpec((tm, PLE_DIM), tile),
                pl.BlockSpec((1, D), fixed),
                pl.BlockSpec((D, D), fixed, pipeline_mode=resident),
                pl.BlockSpec((PLE_DIM, D), fixed, pipeline_mode=resident),
                pl.BlockSpec((1, D), fixed),
            ],
            out_specs=pl.BlockSpec((tm, D), tile),
            scratch_shapes=[pltpu.VMEM((2 * TOPK_IN_GROUP, tm, yb.shape[1]), yb.dtype),
                            pltpu.SemaphoreType.DMA((2,))],
        ),
        compiler_params=_cparams(1, vmem),
        name="combine",
    )(slots_padded, x1, info, yb, p, g_ple.reshape(1, D), w_ple_gate, w_ple_proj, g_final.reshape(1, D))


def _rope_tables(T):
    half, BS = MOBA_HEAD_DIM // 2, MOBA_BLOCK
    inv_freq = ROPE_THETA ** (-jnp.arange(half, dtype=F32) / half)
    ang_a = (jnp.arange(T // BS, dtype=F32) * BS)[:, None, None] * inv_freq
    ang_b = jnp.arange(BS, dtype=F32)[None, :, None] * inv_freq
    ca, sa, cb, sb = jnp.cos(ang_a), jnp.sin(ang_a), jnp.cos(ang_b), jnp.sin(ang_b)
    cos = (ca * cb - sa * sb).reshape(T, half)
    sin = (sa * cb + ca * sb).reshape(T, half)
    return jnp.concatenate([cos, cos], axis=1), jnp.concatenate([-sin, sin], axis=1)


def _mixers(x2d, g_mix, w_in, lb, hgrn_norm_g):
    T = x2d.shape[0]
    W = HGRN_WIDTH
    w = w_in
    cos, sin = _rope_tables(T)
    log_lb = jnp.log(lb).reshape(1, W)
    log_1m = jnp.log1p(-lb).reshape(1, W)
    hq, h = _norm_proj(x2d, g_mix, w, 0 * W, W, _ep_silu, BF16)
    logf = _proj(h, w, 1 * W, W, _ep_logf, F32, col_extras=(log_lb, log_1m))
    hi = _proj(h, w, 2 * W, W, _ep_identity, BF16)
    hog = _proj(h, w, 3 * W, W, _ep_silu, BF16)
    scale = MOBA_HEAD_DIM ** -0.5
    mqt = _proj_t(h, w, 4 * W, functools.partial(_ep_rope, scale=scale), row_extras=(cos, sin))
    blk = jnp.arange(T, dtype=I32)[:, None] // MOBA_BLOCK
    blk_onehot = (blk == jnp.arange(V7X_LANES, dtype=I32)[None, :]).astype(BF16)
    mk = _proj(h, w, 5 * W, W, functools.partial(_ep_rope, scale=1.0), BF16, row_extras=(cos, sin))
    mvt = _proj_t(h, w, 6 * W, _ep_identity, ones_rows=MOBA_VT_ROWS - MOBA_HEAD_DIM)
    gates = _proj(h, w, 7 * W, 2 * D_MODEL, _ep_sigmoid, BF16)
    o_hgrn = _hgrn(hq, logf, hi, hog, hgrn_norm_g)
    o_moba = _moba(mqt, mk, blk_onehot, mvt)
    return o_hgrn, o_moba, gates


def _moe_plan(info, cnt, T):
    R = MOE_ROWS
    counts = cnt[0, :N_EXPERTS].astype(I32)
    padded = (counts + R - 1) // R * R
    pad_end = jnp.cumsum(padded)
    pad_start = pad_end - padded
    slots = _slots(info, pad_start)
    n_blocks = (T * TOPK_IN_GROUP) // R + N_EXPERTS
    block_row = jnp.arange(n_blocks, dtype=I32) * R
    block_expert = jnp.minimum(jnp.sum(pad_end[None, :] <= block_row[:, None], axis=1), N_EXPERTS - 1).astype(I32)
    has = counts > 0
    segment_expert = jnp.argsort(jnp.logical_not(has), stable=True).astype(I32)
    block_segment = (jnp.cumsum(has.astype(I32))[block_expert] - 1).astype(I32)
    n_used = jnp.stack([pad_end[-1] // R, jnp.sum(has.astype(I32))]).astype(I32)
    return slots, pad_end.astype(I32), block_segment, segment_expert, n_used, n_blocks * R


def kernel(x, p, norm_mix_g, w_in, hgrn_lb_raw, hgrn_norm_g, w_up_hgrn, w_up_moba, w_out, norm_ffn_g,
           w_router_group, w_router_expert, w_exp_gate, w_exp_up, w_exp_down, norm_ple_g, w_ple_gate,
           w_ple_proj, norm_final_g):
    B, T, D = x.shape
    assert B == 1 and D == D_MODEL and w_in.shape[0] == 1 and T % (4 * MOBA_BLOCK) == 0
    lower_bounds = jnp.cumsum(jax.nn.softmax(hgrn_lb_raw.astype(F32), axis=0), axis=0)
    x2d = x.reshape(T, D)
    o_hgrn, o_moba, gates = _mixers(x2d, norm_mix_g[0], w_in[0], lower_bounds[0], hgrn_norm_g[0])
    merged = _merge(o_hgrn, o_moba, gates, w_up_hgrn[0].astype(BF16), w_up_moba[0].astype(BF16))
    w_router = jnp.pad(jnp.concatenate([w_router_group[0], w_router_expert[0]], axis=1),
                       ((0, 0), (0, V7X_LANES - N_GROUPS - N_EXPERTS))).astype(BF16)
    x1, h2p, info, cnt = _outproj(merged, x2d, w_out[0].astype(BF16), norm_ffn_g[0], w_router)
    slots, pad_end, block_segment, segment_expert, n_used, m_pad = _moe_plan(info, cnt, T)
    xs = _dispatch(slots, pad_end, h2p, m_pad)
    yb = _experts(block_segment, segment_expert, n_used, xs, w_exp_gate[0], w_exp_up[0], w_exp_down[0])
    out = _combine(slots, x1, info, yb, p[0].reshape(T, PLE_DIM), norm_ple_g[0], w_ple_gate[0].astype(BF16),
                   w_ple_proj[0].astype(BF16), norm_final_g)
    return out.reshape(B, T, D)
```

```python
import functools

import jax
import jax.numpy as jnp
from jax import lax
from jax.experimental import pallas as pl
from jax.experimental.pallas import tpu as pltpu

F32 = jnp.float32
BF16 = jnp.bfloat16
I32 = jnp.int32

D_MODEL = 2048
PLE_DIM = 256
HGRN_HEADS = 8
HGRN_HEAD_DIM = 128
HGRN_WIDTH = HGRN_HEADS * HGRN_HEAD_DIM
MOBA_HEADS = 8
MOBA_HEAD_DIM = 128
MOBA_WIDTH = MOBA_HEADS * MOBA_HEAD_DIM
MOBA_BLOCK = 256
MOBA_TOPK = 3
ROPE_THETA = 10000.0
N_GROUPS = 4
EXPERTS_PER_GROUP = 8
N_EXPERTS = N_GROUPS * EXPERTS_PER_GROUP
TOPK_IN_GROUP = 2
D_EXPERT = 512
EPS = 1e-6
NEG_INF = -1e30

V7X_LANES = 128
V7X_SUBLANES = 8
V7X_VMEM_BUDGET_BYTES = 56 * 1024 * 1024

HGRN_CHUNK = 128
MOE_ROWS = 256


def _cparams(n_grid, vmem_bytes):
    return pltpu.CompilerParams(
        dimension_semantics=("arbitrary",) * n_grid,
        vmem_limit_bytes=int(min(max(vmem_bytes, 16 * 1024 * 1024), V7X_VMEM_BUDGET_BYTES)),
    )


def _nbytes(shape, dtype):
    n = 1
    for s in shape:
        n *= s
    return n * jnp.dtype(dtype).itemsize


def _rms(x, g):
    ms = jnp.mean(x * x, axis=-1, keepdims=True)
    return x * lax.rsqrt(ms + EPS) * g


def _ep_identity(acc):
    return acc


def _ep_silu(acc):
    return acc * jax.nn.sigmoid(acc)


def _ep_sigmoid(acc):
    return jax.nn.sigmoid(acc)


def _ep_logf(acc, la_ref, lc_ref):
    ls = jnp.minimum(acc, 0.0) - jnp.log(1.0 + jnp.exp(-jnp.abs(acc)))
    u = la_ref[...]
    v = lc_ref[...] + ls
    return jnp.maximum(u, v) + jnp.log(1.0 + jnp.exp(-jnp.abs(u - v)))


def _ep_rope(acc, cos_ref, sin_ref, *, scale):
    cos = cos_ref[...]
    sin = sin_ref[...]
    outs = []
    for hh in range(acc.shape[1] // MOBA_HEAD_DIM):
        a = acc[:, hh * MOBA_HEAD_DIM:(hh + 1) * MOBA_HEAD_DIM]
        r = pltpu.roll(a, MOBA_HEAD_DIM // 2, axis=1)
        outs.append((a * cos + r * sin) * scale)
    return jnp.concatenate(outs, axis=1)


def _cast_weight_once(w_ref, wb_ref, row_axis):
    @pl.when(pl.program_id(row_axis) == 0)
    def _():
        wb_ref[...] = w_ref[...].astype(wb_ref.dtype)


def _proj_kernel(h_ref, w_ref, *refs, epilogue):
    *extra, o_ref, wb_ref = refs
    _cast_weight_once(w_ref, wb_ref, 1)
    acc = jnp.dot(h_ref[...], wb_ref[...], preferred_element_type=F32)
    o_ref[...] = epilogue(acc, *extra).astype(o_ref.dtype)


def _proj(h, w, col0, ncols, epilogue, out_dtype, row_extras=(), col_extras=(), tm=1024, tn=1024):
    T, K = h.shape
    tn = min(tn, ncols)
    tm = min(tm, T)
    cb = col0 // tn
    in_specs = [
        pl.BlockSpec((tm, K), lambda j, i: (i, 0)),
        pl.BlockSpec((K, tn), lambda j, i: (0, cb + j)),
    ]
    for e in row_extras:
        in_specs.append(pl.BlockSpec((tm, e.shape[1]), lambda j, i: (i, 0)))
    for e in col_extras:
        in_specs.append(pl.BlockSpec((1, tn), lambda j, i: (0, j)))
    vmem = 2 * (_nbytes((tm, K), h.dtype) + _nbytes((K, tn), w.dtype) + _nbytes((tm, tn), out_dtype))
    vmem += 3 * _nbytes((tm, tn), F32) + _nbytes((K, tn), BF16)
    return pl.pallas_call(
        functools.partial(_proj_kernel, epilogue=epilogue),
        out_shape=jax.ShapeDtypeStruct((T, ncols), out_dtype),
        grid=(ncols // tn, T // tm),
        in_specs=in_specs,
        out_specs=pl.BlockSpec((tm, tn), lambda j, i: (i, j)),
        scratch_shapes=[pltpu.VMEM((K, tn), BF16)],
        compiler_params=_cparams(2, vmem),
        name="proj",
    )(h, w, *row_extras, *col_extras)


def _norm_proj_kernel(x_ref, g_ref, w_ref, o_ref, h_ref, wb_ref, *, epilogue):
    _cast_weight_once(w_ref, wb_ref, 0)
    h = _rms(x_ref[...], g_ref[...]).astype(h_ref.dtype)
    h_ref[...] = h
    o_ref[...] = epilogue(jnp.dot(h, wb_ref[...], preferred_element_type=F32)).astype(o_ref.dtype)


def _norm_proj(x, g, w, col0, ncols, epilogue, out_dtype, tm=512):
    T, K = x.shape
    cb = col0 // ncols
    vmem = 2 * (_nbytes((tm, K), F32) + _nbytes((tm, ncols), out_dtype) + _nbytes((tm, K), BF16))
    vmem += _nbytes((K, ncols), F32) + _nbytes((K, ncols), BF16) + 2 * _nbytes((tm, K), F32) + 3 * _nbytes((tm, ncols), F32)
    return pl.pallas_call(
        functools.partial(_norm_proj_kernel, epilogue=epilogue),
        out_shape=(jax.ShapeDtypeStruct((T, ncols), out_dtype), jax.ShapeDtypeStruct((T, K), BF16)),
        grid=(T // tm,),
        in_specs=[
            pl.BlockSpec((tm, K), lambda i: (i, 0)),
            pl.BlockSpec((1, K), lambda i: (0, 0)),
            pl.BlockSpec((K, ncols), lambda i: (0, cb), pipeline_mode=pl.Buffered(1)),
        ],
        out_specs=(pl.BlockSpec((tm, ncols), lambda i: (i, 0)), pl.BlockSpec((tm, K), lambda i: (i, 0))),
        scratch_shapes=[pltpu.VMEM((K, ncols), BF16)],
        compiler_params=_cparams(1, vmem),
        name="norm_proj",
    )(x, g.reshape(1, K), w)


MOBA_VT_ROWS = MOBA_HEAD_DIM + 16
MOBA_BLOCKS_PER_STEP = 4


def _proj_t_kernel(h_ref, w_ref, *refs, epilogue, ones_rows):
    *extra, o_ref, wb_ref = refs
    BS, HD = MOBA_BLOCK, MOBA_HEAD_DIM
    _cast_weight_once(w_ref, wb_ref, 0)
    acc = epilogue(jnp.dot(h_ref[...], wb_ref[...], preferred_element_type=F32), *extra)
    ones = jnp.ones((ones_rows, BS), F32) if ones_rows else None
    for b in range(acc.shape[0] // BS):
        parts = []
        for hh in range(acc.shape[1] // HD):
            parts.append(acc[b * BS:(b + 1) * BS, hh * HD:(hh + 1) * HD].T)
            if ones_rows:
                parts.append(ones)
        o_ref[b] = jnp.concatenate(parts, axis=0).astype(o_ref.dtype)


def _proj_t(h, w, col0, epilogue, row_extras=(), ones_rows=0, tm=1024):
    T, K = h.shape
    tn = MOBA_WIDTH
    tm = min(tm, T)
    cb = col0 // tn
    rows = MOBA_HEADS * (MOBA_HEAD_DIM + ones_rows)
    in_specs = [pl.BlockSpec((tm, K), lambda i: (i, 0)), pl.BlockSpec((K, tn), lambda i: (0, cb))]
    for e in row_extras:
        in_specs.append(pl.BlockSpec((tm, e.shape[1]), lambda i: (i, 0)))
    vmem = 2 * (_nbytes((tm, K), h.dtype) + _nbytes((K, tn), w.dtype) + _nbytes((tm, 2 * tn), BF16))
    vmem += 4 * _nbytes((tm, tn), F32) + _nbytes((K, tn), BF16)
    return pl.pallas_call(
        functools.partial(_proj_t_kernel, epilogue=epilogue, ones_rows=ones_rows),
        out_shape=jax.ShapeDtypeStruct((T // MOBA_BLOCK, rows, MOBA_BLOCK), BF16),
        grid=(T // tm,),
        in_specs=in_specs,
        out_specs=pl.BlockSpec((tm // MOBA_BLOCK, rows, MOBA_BLOCK), lambda i: (i, 0, 0)),
        scratch_shapes=[pltpu.VMEM((K, tn), BF16)],
        compiler_params=_cparams(1, vmem),
        name="proj_t",
    )(h, w, *row_extras)


def _hgrn_kernel(q_ref, g_ref, v_ref, og_ref, ng_ref, o_ref, st_ref, code_ref):
    W, HD, NH, C, S = HGRN_WIDTH, HGRN_HEAD_DIM, HGRN_HEADS, HGRN_CHUNK, V7X_SUBLANES
    J = C // S

    @pl.when(pl.program_id(0) == 0)
    def _():
        st_ref[...] = jnp.zeros_like(st_ref)
        tr = lax.broadcasted_iota(I32, (C, C), 0)
        tc = lax.broadcasted_iota(I32, (C, C), 1)
        xr = tr ^ tc
        code = jnp.zeros((C, C), I32)
        for lvl in range(1, 8):
            code = jnp.where(xr >= (1 << (lvl - 1)), lvl, code)
        code_ref[...] = jnp.where(tc > tr, -1, code)

    def r3(x):
        return x.astype(F32).reshape(J, S, W)

    def sub_bcast(x3, r):
        return jnp.broadcast_to(x3[:, r:r + 1, :], x3.shape)

    g3, q3, v3 = r3(g_ref[...]), r3(q_ref[...]), r3(v_ref[...])
    sub = lax.broadcasted_iota(I32, (1, S, W), 1)

    c3 = g3
    for s in (1, 2, 4):
        c3 = c3 + jnp.where(sub >= s, pltpu.roll(c3, s, axis=1), 0.0)
    run = jnp.zeros((1, 1, W), F32)
    carry = []
    for j in range(J):
        carry.append(run)
        run = run + c3[j:j + 1, S - 1:S, :]
    b3 = c3 + jnp.concatenate(carry, axis=0)
    bC = run

    k3 = 1.0 - jnp.exp(g3)
    qe3 = q3 * jnp.exp(b3)
    ks3 = k3 * jnp.exp(bC - b3)

    levels = [(0, q3, k3)]
    ref1 = jnp.where(sub % 2 == 0, b3, pltpu.roll(b3, 1, axis=1))
    ref2 = jnp.where(sub < 4, sub_bcast(b3, 1), sub_bcast(b3, 5))
    ref4 = sub_bcast(b3, 3)
    for lvl, (ref, upper) in enumerate(((ref1, sub % 2 == 1), (ref2, sub % 4 >= 2), (ref4, sub >= 4)), start=1):
        e = jnp.exp(-jnp.abs(b3 - ref))
        levels.append((lvl, jnp.where(upper, q3 * e, 0.0), jnp.where(upper, 0.0, k3 * e)))
    zero_group = jnp.zeros((1, S, W), F32)
    for lvl, half in enumerate((1, 2, 4, 8), start=4):
        qparts, kparts = [], []
        for j in range(J):
            jr = (j // (2 * half)) * (2 * half) + half - 1
            ref = b3[jr:jr + 1, S - 1:S, :]
            if (j % (2 * half)) >= half:
                qparts.append(q3[j:j + 1] * jnp.exp(b3[j:j + 1] - ref))
                kparts.append(zero_group)
            else:
                qparts.append(zero_group)
                kparts.append(k3[j:j + 1] * jnp.exp(ref - b3[j:j + 1]))
        levels.append((lvl, jnp.concatenate(qparts, axis=0), jnp.concatenate(kparts, axis=0)))

    code = code_ref[...]

    def mat(x3, h):
        return x3.reshape(C, W)[:, h * HD:(h + 1) * HD].astype(BF16)

    nt = (((1,), (1,)), ((), ()))
    tn = (((0,), (0,)), ((), ()))
    ebc = jnp.exp(bC).reshape(1, W)
    ng = ng_ref[...]
    for h in range(NH):
        a_mat = jnp.zeros((C, C), F32)
        for lvl, qr, kr in levels:
            s = lax.dot_general(mat(qr, h), mat(kr, h), nt, preferred_element_type=F32)
            a_mat = jnp.where(code == lvl, s, a_mat)
        vh = mat(v3, h)
        st = st_ref[h]
        o = jnp.dot(a_mat.astype(BF16), vh, preferred_element_type=F32)
        o = o + lax.dot_general(mat(qe3, h), st.astype(BF16), nt, preferred_element_type=F32)
        o = _rms(o, ng) * og_ref[:, h * HD:(h + 1) * HD].astype(F32)
        o_ref[:, h * HD:(h + 1) * HD] = o.astype(o_ref.dtype)
        st_ref[h] = st * ebc[:, h * HD:(h + 1) * HD] + lax.dot_general(
            vh, mat(ks3, h), tn, preferred_element_type=F32)


def _hgrn(q, logf, v, og, norm_g):
    T, W = q.shape
    C = HGRN_CHUNK
    blk = pl.BlockSpec((C, W), lambda c: (c, 0))
    vmem = 64 * _nbytes((C, W), F32)
    return pl.pallas_call(
        _hgrn_kernel,
        out_shape=jax.ShapeDtypeStruct((T, W), BF16),
        grid=(T // C,),
        in_specs=[blk, blk, blk, blk, pl.BlockSpec((1, HGRN_HEAD_DIM), lambda c: (0, 0))],
        out_specs=blk,
        scratch_shapes=[pltpu.VMEM((HGRN_HEADS, HGRN_HEAD_DIM, HGRN_HEAD_DIM), F32), pltpu.VMEM((C, C), I32)],
        compiler_params=_cparams(1, vmem),
        name="hgrn",
    )(q, logf, v, og, norm_g.reshape(1, HGRN_HEAD_DIM))


def _moba_kernel(qt_ref, k_ref, oh_ref, vt_ref, o_ref, km_ref):
    BS, HD, VR = MOBA_BLOCK, MOBA_HEAD_DIM, MOBA_VT_ROWS
    T = k_ref.shape[0]
    NB = T // BS
    G = o_ref.shape[1] // HD
    cur = pl.program_id(1)

    @pl.when(cur == 0)
    def _():
        for g in range(G):
            kf = k_ref[:, g * HD:(g + 1) * HD].astype(F32).reshape(NB, BS, HD)
            km_ref[g] = jnp.sum(kf, axis=1) * (1.0 / BS)

    blk = lax.broadcasted_iota(I32, (NB, BS), 0)
    pad = jnp.zeros((V7X_LANES - NB, BS), F32)
    qts = [qt_ref[0, g * HD:(g + 1) * HD, :] for g in range(G)]
    gts = [jnp.dot(km_ref[g].astype(BF16), qts[g], preferred_element_type=F32) for g in range(G)]
    qcs = []
    for g in range(G):
        gt = jnp.where(blk < cur, gts[g], NEG_INF)
        sel = blk == cur
        for _ in range(MOBA_TOPK):
            mx = jnp.max(gt, axis=0, keepdims=True)
            idx = jnp.min(jnp.where(gt == mx, blk, NB), axis=0, keepdims=True)
            pick = (blk == idx) & (mx > 0.5 * NEG_INF)
            sel = sel | pick
            gt = jnp.where(pick, NEG_INF, gt)
        pen = jnp.concatenate([jnp.where(sel, 0.0, NEG_INF), pad], axis=0).astype(BF16)
        qcs.append(jnp.concatenate([qts[g], pen], axis=0))

    KB = MOBA_BLOCKS_PER_STEP

    def body(blk0, carry, own_step=False, nblk=KB):
        ms, accs = carry
        r = pl.multiple_of(blk0 * BS, BS)
        oh = oh_ref[pl.ds(r, nblk * BS), :]
        sns = [jnp.dot(jnp.concatenate([k_ref[pl.ds(r, nblk * BS), g * HD:(g + 1) * HD], oh], axis=1), qcs[g],
                       preferred_element_type=F32) for g in range(G)]
        if own_step:
            krow = lax.broadcasted_iota(I32, (nblk * BS, BS), 0)
            qcol = lax.broadcasted_iota(I32, (nblk * BS, BS), 1)
            keep = (blk0 + krow // BS != cur) | (krow % BS <= qcol)
            sns = [jnp.where(keep, s, NEG_INF) for s in sns]
        new_ms, alphas, pns = [], [], []
        for g in range(G):
            m_new = jnp.maximum(ms[g], jnp.max(sns[g], axis=0, keepdims=True))
            alphas.append(jnp.exp(ms[g] - m_new))
            pns.append(jnp.exp(sns[g] - m_new).astype(BF16))
            new_ms.append(m_new)
        new_accs = []
        for g in range(G):
            pv = alphas[g] * accs[g]
            for j in range(nblk):
                pv = pv + jnp.dot(vt_ref[blk0 + j, g * VR:(g + 1) * VR, :], pns[g][j * BS:(j + 1) * BS],
                                  preferred_element_type=F32)
            new_accs.append(pv)
        return tuple(new_ms), tuple(new_accs)

    ms = tuple(jnp.full((1, BS), NEG_INF, F32) for _ in range(G))
    accs = tuple(jnp.zeros((VR, BS), F32) for _ in range(G))
    carry = lax.fori_loop(0, cur // (2 * KB), lambda c, cr: body(c * 2 * KB, cr, nblk=2 * KB), (ms, accs))
    last = (cur // KB) * KB
    carry = lax.cond(cur % (2 * KB) >= KB, lambda cr: body(last - KB, cr), lambda cr: cr, carry)
    _, accs = lax.cond(
        cur % KB < KB // 2,
        lambda cr: body(last, cr, own_step=True, nblk=KB // 2),
        lambda cr: body(last, cr, own_step=True),
        carry)
    for g in range(G):
        ot = accs[g][:HD, :] / accs[g][HD:HD + 1, :]
        o_ref[:, g * HD:(g + 1) * HD] = ot.T.astype(o_ref.dtype)


def _moba(mqt, mk, block_onehot, mvt, heads_per_step=4):
    T = mk.shape[0]
    BS, HD, G, VR = MOBA_BLOCK, MOBA_HEAD_DIM, heads_per_step, MOBA_VT_ROWS
    NB = T // BS
    vmem = _nbytes((T, (G + 1) * HD), BF16) + _nbytes((NB, G * VR, BS), BF16) + 8 * _nbytes((BS, G * HD), BF16)
    vmem += 40 * G * _nbytes((BS, BS), F32)
    resident = pl.Buffered(1)
    return pl.pallas_call(
        _moba_kernel,
        out_shape=jax.ShapeDtypeStruct((T, MOBA_WIDTH), BF16),
        grid=(MOBA_HEADS // G, NB),
        in_specs=[
            pl.BlockSpec((1, G * HD, BS), lambda h, i: (i, h, 0)),
            pl.BlockSpec((T, G * HD), lambda h, i: (0, h), pipeline_mode=resident),
            pl.BlockSpec((T, V7X_LANES), lambda h, i: (0, 0), pipeline_mode=resident),
            pl.BlockSpec((NB, G * VR, BS), lambda h, i: (0, h, 0), pipeline_mode=resident),
        ],
        out_specs=pl.BlockSpec((BS, G * HD), lambda h, i: (i, h)),
        scratch_shapes=[pltpu.VMEM((G, NB, HD), F32)],
        compiler_params=_cparams(2, vmem),
        name="moba",
    )(mqt, mk, block_onehot, mvt)


def _merge_kernel(oh_ref, om_ref, ga_ref, gb_ref, wh_ref, wm_ref, o_ref):
    a = jnp.dot(oh_ref[...], wh_ref[...], preferred_element_type=F32)
    b = jnp.dot(om_ref[...], wm_ref[...], preferred_element_type=F32)
    o_ref[...] = (ga_ref[...].astype(F32) * a + gb_ref[...].astype(F32) * b).astype(o_ref.dtype)


def _merge(o_hgrn, o_moba, gates, w_up_hgrn, w_up_moba, tm=512):
    T = o_hgrn.shape[0]
    D = D_MODEL
    vmem = 2 * (_nbytes((tm, HGRN_WIDTH), F32) + _nbytes((tm, MOBA_WIDTH), BF16) + 3 * _nbytes((tm, D), BF16)
                + 2 * _nbytes((HGRN_WIDTH, D), BF16)) + 3 * _nbytes((tm, D), F32)
    return pl.pallas_call(
        _merge_kernel,
        out_shape=jax.ShapeDtypeStruct((T, D), BF16),
        grid=(T // tm,),
        in_specs=[
            pl.BlockSpec((tm, HGRN_WIDTH), lambda i: (i, 0)),
            pl.BlockSpec((tm, MOBA_WIDTH), lambda i: (i, 0)),
            pl.BlockSpec((tm, D), lambda i: (i, 0)),
            pl.BlockSpec((tm, D), lambda i: (i, 1)),
            pl.BlockSpec((HGRN_WIDTH, D), lambda i: (0, 0)),
            pl.BlockSpec((MOBA_WIDTH, D), lambda i: (0, 0)),
        ],
        out_specs=pl.BlockSpec((tm, D), lambda i: (i, 0)),
        compiler_params=_cparams(1, vmem),
        name="merge",
    )(o_hgrn, o_moba, gates, gates, w_up_hgrn, w_up_moba)


def _pack_bf16_pairs(x):
    C = x.shape[1] // 2
    b = lax.bitcast_convert_type(x, jnp.uint32)
    r = (b + jnp.uint32(0x7FFF) + ((b >> 16) & jnp.uint32(1))) >> 16
    return r[:, :C] | (r[:, C:] << 16)


def _unpack_bf16_pairs(p):
    lo = lax.bitcast_convert_type(p << 16, F32)
    hi = lax.bitcast_convert_type(p & jnp.uint32(0xFFFF0000), F32)
    return jnp.concatenate([lo, hi], axis=1)


def _outproj_kernel(m_ref, x_ref, w_ref, g_ref, wr_ref, x1_ref, h2p_ref, info_ref, cnt_ref, prev_ref, carry_ref):
    step = pl.program_id(0)
    n_tiles = pl.num_programs(0) - 1

    @pl.when(step == 0)
    def _():
        carry_ref[...] = jnp.zeros_like(carry_ref)

    def finish_previous():
        h2 = _rms(prev_ref[...], g_ref[...])
        h2p_ref[...] = _pack_bf16_pairs(h2)
        info_ref[...] = _route(h2, wr_ref, carry_ref)
        cnt_ref[...] = carry_ref[...]

    def project():
        return x_ref[...] + jnp.dot(m_ref[...], w_ref[...], preferred_element_type=F32)

    @pl.when(step == 0)
    def _():
        x1 = project()
        x1_ref[...] = x1
        prev_ref[...] = x1

    @pl.when((step > 0) & (step < n_tiles))
    def _():
        x1 = project()
        finish_previous()
        x1_ref[...] = x1
        prev_ref[...] = x1

    @pl.when(step == n_tiles)
    def _():
        finish_previous()


def _outproj(merged, x, w_out, g_ffn, w_router, tm=512):
    T, D = x.shape
    n_tiles = T // tm
    vmem = 2 * (_nbytes((tm, D), BF16) + 3 * _nbytes((tm, D), F32) + _nbytes((D, D), BF16)) + 7 * _nbytes((tm, D), F32)
    cur = lambda i: (jnp.minimum(i, n_tiles - 1), 0)
    prev = lambda i: (jnp.maximum(i - 1, 0), 0)
    fixed = lambda i: (0, 0)
    return pl.pallas_call(
        _outproj_kernel,
        out_shape=(jax.ShapeDtypeStruct((T, D), F32), jax.ShapeDtypeStruct((T, D // 2), jnp.uint32),
                   jax.ShapeDtypeStruct((T, V7X_LANES), F32), jax.ShapeDtypeStruct((1, V7X_LANES), F32)),
        grid=(n_tiles + 1,),
        in_specs=[
            pl.BlockSpec((tm, D), cur),
            pl.BlockSpec((tm, D), cur),
            pl.BlockSpec((D, D), fixed, pipeline_mode=pl.Buffered(1)),
            pl.BlockSpec((1, D), fixed),
            pl.BlockSpec((D, V7X_LANES), fixed),
        ],
        out_specs=(pl.BlockSpec((tm, D), cur), pl.BlockSpec((tm, D // 2), prev), pl.BlockSpec((tm, V7X_LANES), prev),
                   pl.BlockSpec((1, V7X_LANES), fixed)),
        scratch_shapes=[pltpu.VMEM((tm, D), F32), pltpu.VMEM((1, V7X_LANES), F32)],
        compiler_params=_cparams(1, vmem),
        name="outproj",
    )(merged, x, w_out, g_ffn.reshape(1, D), w_router)


def _route(h2, w_ref, carry_ref):
    tm = h2.shape[0]
    logits = jnp.dot(h2.astype(BF16), w_ref[...], preferred_element_type=F32)
    lane = lax.broadcasted_iota(I32, (tm, V7X_LANES), 1)
    is_g = lane < N_GROUPS
    gl = jnp.where(is_g, logits, NEG_INF)
    gmax = jnp.max(gl, axis=1, keepdims=True)
    g_sel = jnp.min(jnp.where(gl == gmax, lane, V7X_LANES), axis=1, keepdims=True)
    gsum = jnp.sum(jnp.where(is_g, jnp.exp(gl - gmax), 0.0), axis=1, keepdims=True)
    p_group = 1.0 / gsum
    lo = N_GROUPS + EXPERTS_PER_GROUP * g_sel
    emask = (lane >= lo) & (lane < lo + EXPERTS_PER_GROUP)
    el = jnp.where(emask, logits, NEG_INF)
    e1 = jnp.max(el, axis=1, keepdims=True)
    i1 = jnp.min(jnp.where((el == e1) & emask, lane, V7X_LANES), axis=1, keepdims=True)
    emask2 = emask & (lane != i1)
    el2 = jnp.where(emask2, logits, NEG_INF)
    e2 = jnp.max(el2, axis=1, keepdims=True)
    i2 = jnp.min(jnp.where((el2 == e2) & emask2, lane, V7X_LANES), axis=1, keepdims=True)
    r = jnp.exp(e2 - e1)
    w1 = p_group / (1.0 + r)
    w2 = p_group * r / (1.0 + r)
    eid1 = i1 - N_GROUPS
    eid2 = i2 - N_GROUPS
    oh1 = jnp.where(lane == eid1, 1.0, 0.0)
    oh2 = jnp.where(lane == eid2, 1.0, 0.0)
    cnt = oh1 + oh2
    tri = jnp.where(lax.broadcasted_iota(I32, (tm, tm), 0) > lax.broadcasted_iota(I32, (tm, tm), 1), 1.0, 0.0)
    before = jnp.dot(tri.astype(BF16), cnt.astype(BF16), preferred_element_type=F32) + carry_ref[...]
    rank1 = jnp.sum(oh1 * before, axis=1, keepdims=True)
    rank2 = jnp.sum(oh2 * before, axis=1, keepdims=True)
    carry_ref[...] = carry_ref[...] + jnp.sum(cnt, axis=0, keepdims=True)
    info = jnp.zeros((tm, V7X_LANES), F32)
    for k, val in enumerate((eid1.astype(F32), eid2.astype(F32), w1, w2, rank1, rank2)):
        info = jnp.where(lane == k, val, info)
    return info


def _row_copy(src_ref, src_row, dst_ref, dst_row, sem):
    return pltpu.make_async_copy(src_ref.at[pl.ds(src_row, 1), :], dst_ref.at[pl.ds(dst_row, 1), :], sem)


ROW_DMA_UNROLL = 8


def _slots_kernel(info_ref, ps_ref, o_ref):
    info = info_ref[...]
    lane = lax.broadcasted_iota(I32, info.shape, 1)
    lane_f = lane.astype(F32)
    ps = ps_ref[...]
    out = jnp.zeros(info.shape, F32)
    for k in range(TOPK_IN_GROUP):
        start = jnp.sum(jnp.where(lane_f == info[:, k:k + 1], ps, 0.0), axis=1, keepdims=True)
        out = jnp.where(lane == k, start + info[:, 4 + k:5 + k], out)
    o_ref[...] = out.astype(I32)


def _slots(info, pad_start, tm=1024):
    T = info.shape[0]
    ps = jnp.pad(pad_start.astype(F32), (0, V7X_LANES - N_EXPERTS)).reshape(1, V7X_LANES)
    out = pl.pallas_call(
        _slots_kernel,
        out_shape=jax.ShapeDtypeStruct((T, V7X_LANES), I32),
        grid=(T // tm,),
        in_specs=[pl.BlockSpec((tm, V7X_LANES), lambda i: (i, 0)), pl.BlockSpec((1, V7X_LANES), lambda i: (0, 0))],
        out_specs=pl.BlockSpec((tm, V7X_LANES), lambda i: (i, 0)),
        compiler_params=_cparams(1, 16 * _nbytes((tm, V7X_LANES), F32)),
        name="slots",
    )(info, ps)
    return out[:, :TOPK_IN_GROUP].reshape(-1)


def _dispatch_kernel(slot_ref, pend_ref, h_ref, xs_ref, zero_ref, sem):
    tm = h_ref.shape[0]
    step = pl.program_id(0)
    base = step * tm

    @pl.when(step == 0)
    def _():
        zero_ref[...] = jnp.zeros_like(zero_ref)

        def tail(e):
            return pltpu.make_async_copy(
                zero_ref, xs_ref.at[pl.ds(pl.multiple_of(pend_ref[e] - MOE_ROWS, MOE_ROWS), MOE_ROWS), :], sem)

        def nonempty(e):
            return pend_ref[e] > (pend_ref[e - 1] if e else 0)

        def unused(b):
            return pltpu.make_async_copy(
                zero_ref, xs_ref.at[pl.ds(pl.multiple_of(b * MOE_ROWS, MOE_ROWS), MOE_ROWS), :], sem)

        first_unused = pend_ref[N_EXPERTS - 1] // MOE_ROWS
        n_blocks = xs_ref.shape[0] // MOE_ROWS
        for e in range(N_EXPERTS):
            pl.when(nonempty(e))(lambda e=e: tail(e).start())
        lax.fori_loop(first_unused, n_blocks, lambda b, c: (unused(b).start(), c)[1], 0)
        for e in range(N_EXPERTS):
            pl.when(nonempty(e))(lambda e=e: tail(e).wait())
        lax.fori_loop(first_unused, n_blocks, lambda b, c: (unused(b).wait(), c)[1], 0)

    def issue(rb, c):
        for u in range(ROW_DMA_UNROLL):
            r = rb * ROW_DMA_UNROLL + u
            for k in range(TOPK_IN_GROUP):
                _row_copy(h_ref, r, xs_ref, slot_ref[(base + r) * TOPK_IN_GROUP + k], sem).start(priority=k % 2)
        return c

    lax.fori_loop(0, tm // ROW_DMA_UNROLL, issue, 0)
    for _ in range(tm * TOPK_IN_GROUP):
        _row_copy(h_ref, 0, xs_ref, 0, sem).wait()


def _dispatch(slots, pad_end, h2p, m_pad, tm=256):
    T, C = h2p.shape
    return pl.pallas_call(
        _dispatch_kernel,
        out_shape=jax.ShapeDtypeStruct((m_pad, C), h2p.dtype),
        grid_spec=pltpu.PrefetchScalarGridSpec(
            num_scalar_prefetch=2,
            grid=(T // tm,),
            in_specs=[pl.BlockSpec((tm, C), lambda i, sl, pe: (i, 0))],
            out_specs=pl.BlockSpec(memory_space=pl.ANY),
            scratch_shapes=[pltpu.VMEM((MOE_ROWS, C), h2p.dtype), pltpu.SemaphoreType.DMA(())],
        ),
        compiler_params=_cparams(1, 6 * _nbytes((tm, C), h2p.dtype)),
        name="dispatch",
    )(slots, pad_end, h2p)


def _expert_kernel(seg_ref, sexp_ref, nu_ref, xs_ref, wg_ref, wu_ref, wd_ref, y_ref, wgf_ref, wuf_ref, wdf_ref,
                   wgb_ref, wub_ref, wdb_ref, sem):
    b = pl.program_id(0)
    used = b < nu_ref[0]
    seg = seg_ref[b]
    first = used & ((b == 0) | (seg != seg_ref[jnp.maximum(b - 1, 0)]))

    def weights_copy(s, part):
        slot = s % 2
        src, dst = ((wg_ref, wgf_ref), (wu_ref, wuf_ref), (wd_ref, wdf_ref))[part]
        return pltpu.make_async_copy(src.at[sexp_ref[s]], dst.at[slot], sem.at[slot])

    @pl.when(b == 0)
    def _():
        for part in range(3):
            weights_copy(0, part).start()

    @pl.when(first)
    def _():
        for part in range(3):
            weights_copy(seg, part).wait()
        slot = seg % 2
        wgb_ref[...] = wgf_ref[slot].astype(BF16)
        wub_ref[...] = wuf_ref[slot].astype(BF16)
        wdb_ref[...] = wdf_ref[slot].astype(BF16)

    @pl.when(first & (seg + 1 < nu_ref[1]))
    def _():
        for part in range(3):
            weights_copy(seg + 1, part).start()

    @pl.when(used)
    def _():
        x = _unpack_bf16_pairs(xs_ref[...]).astype(BF16)
        a = jnp.dot(x, wgb_ref[...], preferred_element_type=F32)
        u = jnp.dot(x, wub_ref[...], preferred_element_type=F32)
        hm = (a * jax.nn.sigmoid(a) * u).astype(BF16)
        y_ref[...] = _pack_bf16_pairs(jnp.dot(hm, wdb_ref[...], preferred_element_type=F32))

    @pl.when(jnp.logical_not(used))
    def _():
        y_ref[...] = jnp.zeros_like(y_ref)


def _experts(block_segment, segment_expert, n_used, xs, w_gate, w_up, w_down):
    m_pad, C = xs.shape
    D = 2 * C
    R, Fd = MOE_ROWS, D_EXPERT
    vmem = 4 * _nbytes((R, C), xs.dtype) + 6 * _nbytes((D, Fd), F32) + 3 * _nbytes((D, Fd), BF16)
    vmem += 6 * _nbytes((R, D), F32)
    hbm = pl.BlockSpec(memory_space=pl.ANY)
    return pl.pallas_call(
        _expert_kernel,
        out_shape=jax.ShapeDtypeStruct((m_pad, C), xs.dtype),
        grid_spec=pltpu.PrefetchScalarGridSpec(
            num_scalar_prefetch=3,
            grid=(m_pad // R,),
            in_specs=[
                pl.BlockSpec((R, C), lambda b, sg, se, nu: (jnp.minimum(b, jnp.maximum(nu[0] - 1, 0)), 0)),
                hbm, hbm, hbm,
            ],
            out_specs=pl.BlockSpec((R, C), lambda b, sg, se, nu: (b, 0)),
            scratch_shapes=[pltpu.VMEM((2, D, Fd), F32), pltpu.VMEM((2, D, Fd), F32), pltpu.VMEM((2, Fd, D), F32),
                            pltpu.VMEM((D, Fd), BF16), pltpu.VMEM((D, Fd), BF16), pltpu.VMEM((Fd, D), BF16),
                            pltpu.SemaphoreType.DMA((2,))],
        ),
        compiler_params=_cparams(1, vmem),
        name="experts",
    )(block_segment, segment_expert, n_used, xs, w_gate, w_up, w_down)


def _combine_kernel(slot_ref, x1_ref, info_ref, yb_ref, p_ref, gp_ref, wpg_ref, wpp_ref, gf_ref, o_ref, ybuf, sem):
    tm, D = x1_ref.shape
    step = pl.program_id(0)
    n_tiles = pl.num_programs(0) - 2
    K = TOPK_IN_GROUP
    GROUPS = 8

    def fetch(tile, group=None):
        slot = tile % 2
        rows = range(tm) if group is None else range(group * tm // GROUPS, (group + 1) * tm // GROUPS)
        for r in rows:
            for k in range(K):
                _row_copy(yb_ref, slot_ref[(tile * tm + r) * K + k], ybuf.at[slot * K + k], r,
                          sem.at[slot]).start(priority=k % 2)

    def drain(tile):
        slot = tile % 2
        for _ in range(tm * K):
            _row_copy(yb_ref, 0, ybuf.at[0], 0, sem.at[slot]).wait()

    @pl.when(step == 0)
    def _():
        fetch(step)

    @pl.when(step > 0)
    def _():
        drain(step - 1)

    @pl.when((step > 0) & (step <= n_tiles))
    def _():
        tile = step - 1
        slot = tile % 2
        info = info_ref[...]
        x2 = (x1_ref[...] + info[:, 2:3] * _unpack_bf16_pairs(ybuf[slot * K])
              + info[:, 3:4] * _unpack_bf16_pairs(ybuf[slot * K + 1]))
        hp = _rms(x2, gp_ref[...]).astype(BF16)
        pp = jnp.dot(p_ref[...].astype(BF16), wpp_ref[...], preferred_element_type=F32)
        cw = D // GROUPS
        zs = []
        for c in range(GROUPS):
            fetch(step, c)
            zs.append(jnp.dot(hp, wpg_ref[:, c * cw:(c + 1) * cw], preferred_element_type=F32))
        x3 = x2 + jax.nn.sigmoid(jnp.concatenate(zs, axis=1)) * pp
        o_ref[...] = _rms(x3, gf_ref[...])


def _combine(slots, x1, info, yb, p, g_ple, w_ple_gate, w_ple_proj, g_final, tm=256):
    T, D = x1.shape
    n_tiles = T // tm
    slots_padded = jnp.concatenate([slots, jnp.zeros((tm * TOPK_IN_GROUP,), I32)])
    vmem = 4 * _nbytes((tm, D), F32) + _nbytes((D, D), BF16) + _nbytes((PLE_DIM, D), BF16)
    vmem += 2 * _nbytes((tm, PLE_DIM), F32) + 2 * TOPK_IN_GROUP * _nbytes((tm, D // 2), yb.dtype)
    vmem += 6 * _nbytes((tm, D), F32)
    tile = lambda i, sl: (jnp.clip(i - 1, 0, n_tiles - 1), 0)
    fixed = lambda i, sl: (0, 0)
    resident = pl.Buffered(1)
    return pl.pallas_call(
        _combine_kernel,
        out_shape=jax.ShapeDtypeStruct((T, D), F32),
        grid_spec=pltpu.PrefetchScalarGridSpec(
            num_scalar_prefetch=1,
            grid=(n_tiles + 2,),
            in_specs=[
                pl.BlockSpec((tm, D), tile),
                pl.BlockSpec((tm, V7X_LANES), tile),
                pl.BlockSpec(memory_space=pl.ANY),
                pl.BlockSpec((tm, PLE_DIM), tile),
                pl.BlockSpec((1, D), fixed),
                pl.BlockSpec((D, D), fixed, pipeline_mode=resident),
                pl.BlockSpec((PLE_DIM, D), fixed, pipeline_mode=resident),
                pl.BlockSpec((1, D), fixed),
            ],
            out_specs=pl.BlockSpec((tm, D), tile),
            scratch_shapes=[pltpu.VMEM((2 * TOPK_IN_GROUP, tm, yb.shape[1]), yb.dtype),
                            pltpu.SemaphoreType.DMA((2,))],
        ),
        compiler_params=_cparams(1, vmem),
        name="combine",
    )(slots_padded, x1, info, yb, p, g_ple.reshape(1, D), w_ple_gate, w_ple_proj, g_final.reshape(1, D))


def _rope_tables(T):
    half, BS = MOBA_HEAD_DIM // 2, MOBA_BLOCK
    inv_freq = ROPE_THETA ** (-jnp.arange(half, dtype=F32) / half)
    ang_a = (jnp.arange(T // BS, dtype=F32) * BS)[:, None, None] * inv_freq
    ang_b = jnp.arange(BS, dtype=F32)[None, :, None] * inv_freq
    ca, sa, cb, sb = jnp.cos(ang_a), jnp.sin(ang_a), jnp.cos(ang_b), jnp.sin(ang_b)
    cos = (ca * cb - sa * sb).reshape(T, half)
    sin = (sa * cb + ca * sb).reshape(T, half)
    return jnp.concatenate([cos, cos], axis=1), jnp.concatenate([-sin, sin], axis=1)


def _mixers(x2d, g_mix, w_in, lb, hgrn_norm_g):
    T = x2d.shape[0]
    W = HGRN_WIDTH
    w = w_in
    cos, sin = _rope_tables(T)
    log_lb = jnp.log(lb).reshape(1, W)
    log_1m = jnp.log1p(-lb).reshape(1, W)
    hq, h = _norm_proj(x2d, g_mix, w, 0 * W, W, _ep_silu, BF16)
    logf = _proj(h, w, 1 * W, W, _ep_logf, F32, col_extras=(log_lb, log_1m))
    hi = _proj(h, w, 2 * W, W, _ep_identity, BF16)
    hog = _proj(h, w, 3 * W, W, _ep_silu, BF16)
    scale = MOBA_HEAD_DIM ** -0.5
    mqt = _proj_t(h, w, 4 * W, functools.partial(_ep_rope, scale=scale), row_extras=(cos, sin))
    blk = jnp.arange(T, dtype=I32)[:, None] // MOBA_BLOCK
    blk_onehot = (blk == jnp.arange(V7X_LANES, dtype=I32)[None, :]).astype(BF16)
    mk = _proj(h, w, 5 * W, W, functools.partial(_ep_rope, scale=1.0), BF16, row_extras=(cos, sin))
    mvt = _proj_t(h, w, 6 * W, _ep_identity, ones_rows=MOBA_VT_ROWS - MOBA_HEAD_DIM)
    gates = _proj(h, w, 7 * W, 2 * D_MODEL, _ep_sigmoid, BF16)
    o_hgrn = _hgrn(hq, logf, hi, hog, hgrn_norm_g)
    o_moba = _moba(mqt, mk, blk_onehot, mvt)
    return o_hgrn, o_moba, gates


def _moe_plan(info, cnt, T):
    R = MOE_ROWS
    counts = cnt[0, :N_EXPERTS].astype(I32)
    padded = (counts + R - 1) // R * R
    pad_end = jnp.cumsum(padded)
    pad_start = pad_end - padded
    slots = _slots(info, pad_start)
    n_blocks = (T * TOPK_IN_GROUP) // R + N_EXPERTS
    block_row = jnp.arange(n_blocks, dtype=I32) * R
    block_expert = jnp.minimum(jnp.sum(pad_end[None, :] <= block_row[:, None], axis=1), N_EXPERTS - 1).astype(I32)
    has = counts > 0
    segment_expert = jnp.argsort(jnp.logical_not(has), stable=True).astype(I32)
    block_segment = (jnp.cumsum(has.astype(I32))[block_expert] - 1).astype(I32)
    n_used = jnp.stack([pad_end[-1] // R, jnp.sum(has.astype(I32))]).astype(I32)
    return slots, pad_end.astype(I32), block_segment, segment_expert, n_used, n_blocks * R


def kernel(x, p, norm_mix_g, w_in, hgrn_lb_raw, hgrn_norm_g, w_up_hgrn, w_up_moba, w_out, norm_ffn_g,
           w_router_group, w_router_expert, w_exp_gate, w_exp_up, w_exp_down, norm_ple_g, w_ple_gate,
           w_ple_proj, norm_final_g):
    B, T, D = x.shape
    assert B == 1 and D == D_MODEL and w_in.shape[0] == 1 and T % (4 * MOBA_BLOCK) == 0
    lower_bounds = jnp.cumsum(jax.nn.softmax(hgrn_lb_raw.astype(F32), axis=0), axis=0)
    x2d = x.reshape(T, D)
    o_hgrn, o_moba, gates = _mixers(x2d, norm_mix_g[0], w_in[0], lower_bounds[0], hgrn_norm_g[0])
    merged = _merge(o_hgrn, o_moba, gates, w_up_hgrn[0].astype(BF16), w_up_moba[0].astype(BF16))
    w_router = jnp.pad(jnp.concatenate([w_router_group[0], w_router_expert[0]], axis=1),
                       ((0, 0), (0, V7X_LANES - N_GROUPS - N_EXPERTS))).astype(BF16)
    x1, h2p, info, cnt = _outproj(merged, x2d, w_out[0].astype(BF16), norm_ffn_g[0], w_router)
    slots, pad_end, block_segment, segment_expert, n_used, m_pad = _moe_plan(info, cnt, T)
    xs = _dispatch(slots, pad_end, h2p, m_pad)
    yb = _experts(block_segment, segment_expert, n_used, xs, w_exp_gate[0], w_exp_up[0], w_exp_down[0])
    out = _combine(slots, x1, info, yb, p[0].reshape(T, PLE_DIM), norm_ple_g[0], w_ple_gate[0].astype(BF16),
                   w_ple_proj[0].astype(BF16), norm_final_g)
    return out.reshape(B, T, D)
```

```python
import functools

import jax
import jax.numpy as jnp
from jax import lax
from jax.experimental import pallas as pl
from jax.experimental.pallas import tpu as pltpu

F32 = jnp.float32
BF16 = jnp.bfloat16
I32 = jnp.int32

D_MODEL = 2048
PLE_DIM = 256
HGRN_HEADS = 8
HGRN_HEAD_DIM = 128
HGRN_WIDTH = HGRN_HEADS * HGRN_HEAD_DIM
MOBA_HEADS = 8
MOBA_HEAD_DIM = 128
MOBA_WIDTH = MOBA_HEADS * MOBA_HEAD_DIM
MOBA_BLOCK = 256
MOBA_TOPK = 3
ROPE_THETA = 10000.0
N_GROUPS = 4
EXPERTS_PER_GROUP = 8
N_EXPERTS = N_GROUPS * EXPERTS_PER_GROUP
TOPK_IN_GROUP = 2
D_EXPERT = 512
EPS = 1e-6
NEG_INF = -1e30

V7X_LANES = 128
V7X_SUBLANES = 8
V7X_VMEM_BUDGET_BYTES = 56 * 1024 * 1024

HGRN_CHUNK = 128
MOE_ROWS = 256


def _cparams(n_grid, vmem_bytes):
    return pltpu.CompilerParams(
        dimension_semantics=("arbitrary",) * n_grid,
        vmem_limit_bytes=int(min(max(vmem_bytes, 16 * 1024 * 1024), V7X_VMEM_BUDGET_BYTES)),
    )


def _nbytes(shape, dtype):
    n = 1
    for s in shape:
        n *= s
    return n * jnp.dtype(dtype).itemsize


def _rms(x, g):
    ms = jnp.mean(x * x, axis=-1, keepdims=True)
    return x * lax.rsqrt(ms + EPS) * g


def _ep_identity(acc):
    return acc


def _ep_silu(acc):
    return acc * jax.nn.sigmoid(acc)


def _ep_sigmoid(acc):
    return jax.nn.sigmoid(acc)


def _ep_logf(acc, la_ref, lc_ref):
    ls = jnp.minimum(acc, 0.0) - jnp.log(1.0 + jnp.exp(-jnp.abs(acc)))
    u = la_ref[...]
    v = lc_ref[...] + ls
    return jnp.maximum(u, v) + jnp.log(1.0 + jnp.exp(-jnp.abs(u - v)))


def _ep_rope(acc, cos_ref, sin_ref, *, scale):
    cos = cos_ref[...]
    sin = sin_ref[...]
    outs = []
    for hh in range(acc.shape[1] // MOBA_HEAD_DIM):
        a = acc[:, hh * MOBA_HEAD_DIM:(hh + 1) * MOBA_HEAD_DIM]
        r = pltpu.roll(a, MOBA_HEAD_DIM // 2, axis=1)
        outs.append((a * cos + r * sin) * scale)
    return jnp.concatenate(outs, axis=1)


def _cast_weight_once(w_ref, wb_ref, row_axis):
    @pl.when(pl.program_id(row_axis) == 0)
    def _():
        wb_ref[...] = w_ref[...].astype(wb_ref.dtype)


PROJ_ROW_SLOTS = 3


def _proj_kernel(h_hbm, w_ref, *refs, epilogue, tm):
    *extra, o_ref, wb_ref, hbuf, sem = refs
    n_row = pl.num_programs(1)
    total = pl.num_programs(0) * n_row
    s = pl.program_id(0) * n_row + pl.program_id(1)

    def row_copy(step):
        row0 = pl.multiple_of((step % n_row) * tm, tm)
        slot = step % PROJ_ROW_SLOTS
        return pltpu.make_async_copy(h_hbm.at[pl.ds(row0, tm)], hbuf.at[slot], sem.at[slot])

    @pl.when(s == 0)
    def _():
        row_copy(s).start()

        @pl.when(total > 1)
        def _():
            row_copy(s + 1).start()

    @pl.when(s + 2 < total)
    def _():
        row_copy(s + 2).start()

    _cast_weight_once(w_ref, wb_ref, 1)
    row_copy(s).wait()
    acc = jnp.dot(hbuf[s % PROJ_ROW_SLOTS], wb_ref[...], preferred_element_type=F32)
    o_ref[...] = epilogue(acc, *extra).astype(o_ref.dtype)


def _proj(h, w, col0, ncols, epilogue, out_dtype, row_extras=(), col_extras=(), tm=1024, tn=1024, widen=1):
    T, K = h.shape
    tn = min(tn, ncols)
    tm = min(tm, T)
    cb = col0 // tn
    otn = widen * tn
    in_specs = [
        pl.BlockSpec(memory_space=pl.ANY),
        pl.BlockSpec((K, tn), lambda j, i: (0, cb + j)),
    ]
    for e in row_extras:
        in_specs.append(pl.BlockSpec((tm, e.shape[1]), lambda j, i: (i, 0)))
    for e in col_extras:
        in_specs.append(pl.BlockSpec((1, tn), lambda j, i: (0, j)))
    vmem = PROJ_ROW_SLOTS * _nbytes((tm, K), h.dtype) + 2 * (_nbytes((K, tn), w.dtype) + _nbytes((tm, otn), out_dtype))
    vmem += 3 * _nbytes((tm, otn), F32) + _nbytes((K, tn), BF16)
    return pl.pallas_call(
        functools.partial(_proj_kernel, epilogue=epilogue, tm=tm),
        out_shape=jax.ShapeDtypeStruct((T, widen * ncols), out_dtype),
        grid=(ncols // tn, T // tm),
        in_specs=in_specs,
        out_specs=pl.BlockSpec((tm, otn), lambda j, i: (i, j)),
        scratch_shapes=[pltpu.VMEM((K, tn), BF16), pltpu.VMEM((PROJ_ROW_SLOTS, tm, K), h.dtype),
                        pltpu.SemaphoreType.DMA((PROJ_ROW_SLOTS,))],
        compiler_params=_cparams(2, vmem),
        name="proj",
    )(h, w, *row_extras, *col_extras)


def _norm_proj_kernel(x_ref, g_ref, w_ref, o_ref, h_ref, wb_ref, *, epilogue):
    _cast_weight_once(w_ref, wb_ref, 0)
    h = _rms(x_ref[...], g_ref[...]).astype(h_ref.dtype)
    h_ref[...] = h
    o_ref[...] = epilogue(jnp.dot(h, wb_ref[...], preferred_element_type=F32)).astype(o_ref.dtype)


def _norm_proj(x, g, w, col0, ncols, epilogue, out_dtype, tm=512):
    T, K = x.shape
    cb = col0 // ncols
    vmem = 2 * (_nbytes((tm, K), F32) + _nbytes((tm, ncols), out_dtype) + _nbytes((tm, K), BF16))
    vmem += _nbytes((K, ncols), F32) + _nbytes((K, ncols), BF16) + 2 * _nbytes((tm, K), F32) + 3 * _nbytes((tm, ncols), F32)
    return pl.pallas_call(
        functools.partial(_norm_proj_kernel, epilogue=epilogue),
        out_shape=(jax.ShapeDtypeStruct((T, ncols), out_dtype), jax.ShapeDtypeStruct((T, K), BF16)),
        grid=(T // tm,),
        in_specs=[
            pl.BlockSpec((tm, K), lambda i: (i, 0)),
            pl.BlockSpec((1, K), lambda i: (0, 0)),
            pl.BlockSpec((K, ncols), lambda i: (0, cb), pipeline_mode=pl.Buffered(1)),
        ],
        out_specs=(pl.BlockSpec((tm, ncols), lambda i: (i, 0)), pl.BlockSpec((tm, K), lambda i: (i, 0))),
        scratch_shapes=[pltpu.VMEM((K, ncols), BF16)],
        compiler_params=_cparams(1, vmem),
        name="norm_proj",
    )(x, g.reshape(1, K), w)


MOBA_VT_ROWS = MOBA_HEAD_DIM + 16
MOBA_BLOCKS_PER_STEP = 4


def _proj_t_kernel(h_ref, w_ref, *refs, epilogue, ones_rows):
    *extra, o_ref, wb_ref = refs
    BS, HD = MOBA_BLOCK, MOBA_HEAD_DIM
    _cast_weight_once(w_ref, wb_ref, 0)
    acc = epilogue(jnp.dot(h_ref[...], wb_ref[...], preferred_element_type=F32), *extra)
    ones = jnp.ones((ones_rows, BS), F32) if ones_rows else None
    for b in range(acc.shape[0] // BS):
        parts = []
        for hh in range(acc.shape[1] // HD):
            parts.append(acc[b * BS:(b + 1) * BS, hh * HD:(hh + 1) * HD].T)
            if ones_rows:
                parts.append(ones)
        o_ref[b] = jnp.concatenate(parts, axis=0).astype(o_ref.dtype)


def _proj_t(h, w, col0, epilogue, row_extras=(), ones_rows=0, tm=1024):
    T, K = h.shape
    tn = MOBA_WIDTH
    tm = min(tm, T)
    cb = col0 // tn
    rows = MOBA_HEADS * (MOBA_HEAD_DIM + ones_rows)
    in_specs = [pl.BlockSpec((tm, K), lambda i: (i, 0)), pl.BlockSpec((K, tn), lambda i: (0, cb))]
    for e in row_extras:
        in_specs.append(pl.BlockSpec((tm, e.shape[1]), lambda i: (i, 0)))
    vmem = 2 * (_nbytes((tm, K), h.dtype) + _nbytes((K, tn), w.dtype) + _nbytes((tm, 2 * tn), BF16))
    vmem += 4 * _nbytes((tm, tn), F32) + _nbytes((K, tn), BF16)
    return pl.pallas_call(
        functools.partial(_proj_t_kernel, epilogue=epilogue, ones_rows=ones_rows),
        out_shape=jax.ShapeDtypeStruct((T // MOBA_BLOCK, rows, MOBA_BLOCK), BF16),
        grid=(T // tm,),
        in_specs=in_specs,
        out_specs=pl.BlockSpec((tm // MOBA_BLOCK, rows, MOBA_BLOCK), lambda i: (i, 0, 0)),
        scratch_shapes=[pltpu.VMEM((K, tn), BF16)],
        compiler_params=_cparams(1, vmem),
        name="proj_t",
    )(h, w, *row_extras)


def _hgrn_kernel(q_ref, g_ref, v_ref, og_ref, ng_ref, o_ref, st_ref, code_ref):
    W, HD, NH, C, S = HGRN_WIDTH, HGRN_HEAD_DIM, HGRN_HEADS, HGRN_CHUNK, V7X_SUBLANES
    J = C // S

    @pl.when(pl.program_id(0) == 0)
    def _():
        st_ref[...] = jnp.zeros_like(st_ref)
        tr = lax.broadcasted_iota(I32, (C, C), 0)
        tc = lax.broadcasted_iota(I32, (C, C), 1)
        xr = tr ^ tc
        code = jnp.zeros((C, C), I32)
        for lvl in range(1, 8):
            code = jnp.where(xr >= (1 << (lvl - 1)), lvl, code)
        code_ref[...] = jnp.where(tc > tr, -1, code)

    def r3(x):
        return x.astype(F32).reshape(J, S, W)

    def sub_bcast(x3, r):
        return jnp.broadcast_to(x3[:, r:r + 1, :], x3.shape)

    g3, q3, v3 = r3(g_ref[...]), r3(q_ref[...]), r3(v_ref[...])
    sub = lax.broadcasted_iota(I32, (1, S, W), 1)

    c3 = g3
    for s in (1, 2, 4):
        c3 = c3 + jnp.where(sub >= s, pltpu.roll(c3, s, axis=1), 0.0)
    run = jnp.zeros((1, 1, W), F32)
    carry = []
    for j in range(J):
        carry.append(run)
        run = run + c3[j:j + 1, S - 1:S, :]
    b3 = c3 + jnp.concatenate(carry, axis=0)
    bC = run

    k3 = 1.0 - jnp.exp(g3)
    qe3 = q3 * jnp.exp(b3)
    ks3 = k3 * jnp.exp(bC - b3)

    levels = [(0, q3, k3)]
    ref1 = jnp.where(sub % 2 == 0, b3, pltpu.roll(b3, 1, axis=1))
    ref2 = jnp.where(sub < 4, sub_bcast(b3, 1), sub_bcast(b3, 5))
    ref4 = sub_bcast(b3, 3)
    for lvl, (ref, upper) in enumerate(((ref1, sub % 2 == 1), (ref2, sub % 4 >= 2), (ref4, sub >= 4)), start=1):
        e = jnp.exp(-jnp.abs(b3 - ref))
        levels.append((lvl, jnp.where(upper, q3 * e, 0.0), jnp.where(upper, 0.0, k3 * e)))
    zero_group = jnp.zeros((1, S, W), F32)
    for lvl, half in enumerate((1, 2, 4, 8), start=4):
        qparts, kparts = [], []
        for j in range(J):
            jr = (j // (2 * half)) * (2 * half) + half - 1
            ref = b3[jr:jr + 1, S - 1:S, :]
            if (j % (2 * half)) >= half:
                qparts.append(q3[j:j + 1] * jnp.exp(b3[j:j + 1] - ref))
                kparts.append(zero_group)
            else:
                qparts.append(zero_group)
                kparts.append(k3[j:j + 1] * jnp.exp(ref - b3[j:j + 1]))
        levels.append((lvl, jnp.concatenate(qparts, axis=0), jnp.concatenate(kparts, axis=0)))

    code = code_ref[...]

    def mat(x3, h):
        return x3.reshape(C, W)[:, h * HD:(h + 1) * HD].astype(BF16)

    nt = (((1,), (1,)), ((), ()))
    tn = (((0,), (0,)), ((), ()))
    ebc = jnp.exp(bC).reshape(1, W)
    ng = ng_ref[...]
    for h in range(NH):
        a_mat = jnp.zeros((C, C), F32)
        for lvl, qr, kr in levels:
            s = lax.dot_general(mat(qr, h), mat(kr, h), nt, preferred_element_type=F32)
            a_mat = jnp.where(code == lvl, s, a_mat)
        vh = mat(v3, h)
        st = st_ref[h]
        o = jnp.dot(a_mat.astype(BF16), vh, preferred_element_type=F32)
        o = o + lax.dot_general(mat(qe3, h), st.astype(BF16), nt, preferred_element_type=F32)
        o = _rms(o, ng) * og_ref[:, h * HD:(h + 1) * HD].astype(F32)
        o_ref[:, h * HD:(h + 1) * HD] = o.astype(o_ref.dtype)
        st_ref[h] = st * ebc[:, h * HD:(h + 1) * HD] + lax.dot_general(
            vh, mat(ks3, h), tn, preferred_element_type=F32)


def _hgrn(q, logf, v, og, norm_g):
    T, W = q.shape
    C = HGRN_CHUNK
    blk = pl.BlockSpec((C, W), lambda c: (c, 0))
    vmem = 64 * _nbytes((C, W), F32)
    return pl.pallas_call(
        _hgrn_kernel,
        out_shape=jax.ShapeDtypeStruct((T, W), BF16),
        grid=(T // C,),
        in_specs=[blk, blk, blk, blk, pl.BlockSpec((1, HGRN_HEAD_DIM), lambda c: (0, 0))],
        out_specs=blk,
        scratch_shapes=[pltpu.VMEM((HGRN_HEADS, HGRN_HEAD_DIM, HGRN_HEAD_DIM), F32), pltpu.VMEM((C, C), I32)],
        compiler_params=_cparams(1, vmem),
        name="hgrn",
    )(q, logf, v, og, norm_g.reshape(1, HGRN_HEAD_DIM))


def _moba_kernel(qt_ref, k_ref, oh_ref, vt_ref, o_ref, km_ref):
    BS, HD, VR = MOBA_BLOCK, MOBA_HEAD_DIM, MOBA_VT_ROWS
    T = k_ref.shape[0]
    NB = T // BS
    G = o_ref.shape[1] // HD
    cur = pl.program_id(1)

    @pl.when(cur == 0)
    def _():
        for g in range(G):
            kf = k_ref[:, g * HD:(g + 1) * HD].astype(F32).reshape(NB, BS, HD)
            km_ref[g] = jnp.sum(kf, axis=1) * (1.0 / BS)

    blk = lax.broadcasted_iota(I32, (NB, BS), 0)
    pad = jnp.zeros((V7X_LANES - NB, BS), F32)
    qts = [qt_ref[0, g * HD:(g + 1) * HD, :] for g in range(G)]
    gts = [jnp.dot(km_ref[g].astype(BF16), qts[g], preferred_element_type=F32) for g in range(G)]
    qcs = []
    for g in range(G):
        gt = jnp.where(blk < cur, gts[g], NEG_INF)
        sel = blk == cur
        for _ in range(MOBA_TOPK):
            mx = jnp.max(gt, axis=0, keepdims=True)
            idx = jnp.min(jnp.where(gt == mx, blk, NB), axis=0, keepdims=True)
            pick = (blk == idx) & (mx > 0.5 * NEG_INF)
            sel = sel | pick
            gt = jnp.where(pick, NEG_INF, gt)
        pen = jnp.concatenate([jnp.where(sel, 0.0, NEG_INF), pad], axis=0).astype(BF16)
        qcs.append(jnp.concatenate([qts[g], pen], axis=0))

    KB = MOBA_BLOCKS_PER_STEP

    def body(blk0, carry, own_step=False, nblk=KB):
        ms, accs = carry
        r = pl.multiple_of(blk0 * BS, BS)
        oh = oh_ref[pl.ds(r, nblk * BS), :]
        sns = [jnp.dot(jnp.concatenate([k_ref[pl.ds(r, nblk * BS), g * HD:(g + 1) * HD], oh], axis=1), qcs[g],
                       preferred_element_type=F32) for g in range(G)]
        if own_step:
            krow = lax.broadcasted_iota(I32, (nblk * BS, BS), 0)
            qcol = lax.broadcasted_iota(I32, (nblk * BS, BS), 1)
            keep = (blk0 + krow // BS != cur) | (krow % BS <= qcol)
            sns = [jnp.where(keep, s, NEG_INF) for s in sns]
        new_ms, alphas, pns = [], [], []
        for g in range(G):
            m_new = jnp.maximum(ms[g], jnp.max(sns[g], axis=0, keepdims=True))
            alphas.append(jnp.exp(ms[g] - m_new))
            pns.append(jnp.exp(sns[g] - m_new).astype(BF16))
            new_ms.append(m_new)
        new_accs = []
        for g in range(G):
            pv = alphas[g] * accs[g]
            for j in range(nblk):
                pv = pv + jnp.dot(vt_ref[blk0 + j, g * VR:(g + 1) * VR, :], pns[g][j * BS:(j + 1) * BS],
                                  preferred_element_type=F32)
            new_accs.append(pv)
        return tuple(new_ms), tuple(new_accs)

    ms = tuple(jnp.full((1, BS), NEG_INF, F32) for _ in range(G))
    accs = tuple(jnp.zeros((VR, BS), F32) for _ in range(G))
    carry = lax.fori_loop(0, cur // (2 * KB), lambda c, cr: body(c * 2 * KB, cr, nblk=2 * KB), (ms, accs))
    last = (cur // KB) * KB
    carry = lax.cond(cur % (2 * KB) >= KB, lambda cr: body(last - KB, cr), lambda cr: cr, carry)
    _, accs = lax.cond(
        cur % KB < KB // 2,
        lambda cr: body(last, cr, own_step=True, nblk=KB // 2),
        lambda cr: body(last, cr, own_step=True),
        carry)
    for g in range(G):
        ot = accs[g][:HD, :] / accs[g][HD:HD + 1, :]
        o_ref[:, g * HD:(g + 1) * HD] = ot.T.astype(o_ref.dtype)


def _moba(mqt, mk, block_onehot, mvt, heads_per_step=4):
    T = mk.shape[0]
    BS, HD, G, VR = MOBA_BLOCK, MOBA_HEAD_DIM, heads_per_step, MOBA_VT_ROWS
    NB = T // BS
    vmem = _nbytes((T, (G + 1) * HD), BF16) + _nbytes((NB, G * VR, BS), BF16) + 8 * _nbytes((BS, G * HD), BF16)
    vmem += 40 * G * _nbytes((BS, BS), F32)
    resident = pl.Buffered(1)
    return pl.pallas_call(
        _moba_kernel,
        out_shape=jax.ShapeDtypeStruct((T, MOBA_WIDTH), BF16),
        grid=(MOBA_HEADS // G, NB),
        in_specs=[
            pl.BlockSpec((1, G * HD, BS), lambda h, i: (i, h, 0)),
            pl.BlockSpec((T, G * HD), lambda h, i: (0, h), pipeline_mode=resident),
            pl.BlockSpec((T, V7X_LANES), lambda h, i: (0, 0), pipeline_mode=resident),
            pl.BlockSpec((NB, G * VR, BS), lambda h, i: (0, h, 0), pipeline_mode=resident),
        ],
        out_specs=pl.BlockSpec((BS, G * HD), lambda h, i: (i, h)),
        scratch_shapes=[pltpu.VMEM((G, NB, HD), F32)],
        compiler_params=_cparams(2, vmem),
        name="moba",
    )(mqt, mk, block_onehot, mvt)


def _merge_kernel(oh_ref, om_ref, ga_ref, gb_ref, wh_ref, wm_ref, o_ref):
    a = jnp.dot(oh_ref[...], wh_ref[...], preferred_element_type=F32)
    b = jnp.dot(om_ref[...], wm_ref[...], preferred_element_type=F32)
    o_ref[...] = (ga_ref[...].astype(F32) * a + gb_ref[...].astype(F32) * b).astype(o_ref.dtype)


def _merge(o_hgrn, o_moba, gates, w_up_hgrn, w_up_moba, tm=512):
    T = o_hgrn.shape[0]
    D = D_MODEL
    vmem = 2 * (_nbytes((tm, HGRN_WIDTH), F32) + _nbytes((tm, MOBA_WIDTH), BF16) + 3 * _nbytes((tm, D), BF16)
                + 2 * _nbytes((HGRN_WIDTH, D), BF16)) + 3 * _nbytes((tm, D), F32)
    return pl.pallas_call(
        _merge_kernel,
        out_shape=jax.ShapeDtypeStruct((T, D), BF16),
        grid=(T // tm,),
        in_specs=[
            pl.BlockSpec((tm, HGRN_WIDTH), lambda i: (i, 0)),
            pl.BlockSpec((tm, MOBA_WIDTH), lambda i: (i, 0)),
            pl.BlockSpec((tm, D), lambda i: (i, 0)),
            pl.BlockSpec((tm, D), lambda i: (i, 1)),
            pl.BlockSpec((HGRN_WIDTH, D), lambda i: (0, 0)),
            pl.BlockSpec((MOBA_WIDTH, D), lambda i: (0, 0)),
        ],
        out_specs=pl.BlockSpec((tm, D), lambda i: (i, 0)),
        compiler_params=_cparams(1, vmem),
        name="merge",
    )(o_hgrn, o_moba, gates, gates, w_up_hgrn, w_up_moba)


def _pack_bf16_pairs(x):
    C = x.shape[1] // 2
    b = lax.bitcast_convert_type(x, jnp.uint32)
    r = (b + jnp.uint32(0x7FFF) + ((b >> 16) & jnp.uint32(1))) >> 16
    return r[:, :C] | (r[:, C:] << 16)


def _unpack_bf16_pairs(p):
    lo = lax.bitcast_convert_type(p << 16, F32)
    hi = lax.bitcast_convert_type(p & jnp.uint32(0xFFFF0000), F32)
    return jnp.concatenate([lo, hi], axis=1)


def _outproj_kernel(m_ref, x_ref, w_ref, g_ref, wr_ref, x1_ref, h2p_ref, info_ref, cnt_ref, prev_ref, carry_ref):
    step = pl.program_id(0)
    n_tiles = pl.num_programs(0) - 1

    @pl.when(step == 0)
    def _():
        carry_ref[...] = jnp.zeros_like(carry_ref)

    def finish_previous():
        h2 = _rms(prev_ref[...], g_ref[...])
        h2p_ref[...] = _pack_bf16_pairs(h2)
        info_ref[...] = _route(h2, wr_ref, carry_ref)
        cnt_ref[...] = carry_ref[...]

    def project():
        return x_ref[...] + jnp.dot(m_ref[...], w_ref[...], preferred_element_type=F32)

    @pl.when(step == 0)
    def _():
        x1 = project()
        x1_ref[...] = x1
        prev_ref[...] = x1

    @pl.when((step > 0) & (step < n_tiles))
    def _():
        x1 = project()
        finish_previous()
        x1_ref[...] = x1
        prev_ref[...] = x1

    @pl.when(step == n_tiles)
    def _():
        finish_previous()


def _outproj(merged, x, w_out, g_ffn, w_router, tm=512):
    T, D = x.shape
    n_tiles = T // tm
    vmem = 2 * (_nbytes((tm, D), BF16) + 3 * _nbytes((tm, D), F32) + _nbytes((D, D), BF16)) + 7 * _nbytes((tm, D), F32)
    cur = lambda i: (jnp.minimum(i, n_tiles - 1), 0)
    prev = lambda i: (jnp.maximum(i - 1, 0), 0)
    fixed = lambda i: (0, 0)
    return pl.pallas_call(
        _outproj_kernel,
        out_shape=(jax.ShapeDtypeStruct((T, D), F32), jax.ShapeDtypeStruct((T, D // 2), jnp.uint32),
                   jax.ShapeDtypeStruct((T, V7X_LANES), F32), jax.ShapeDtypeStruct((1, V7X_LANES), F32)),
        grid=(n_tiles + 1,),
        in_specs=[
            pl.BlockSpec((tm, D), cur),
            pl.BlockSpec((tm, D), cur),
            pl.BlockSpec((D, D), fixed, pipeline_mode=pl.Buffered(1)),
            pl.BlockSpec((1, D), fixed),
            pl.BlockSpec((D, V7X_LANES), fixed),
        ],
        out_specs=(pl.BlockSpec((tm, D), cur), pl.BlockSpec((tm, D // 2), prev), pl.BlockSpec((tm, V7X_LANES), prev),
                   pl.BlockSpec((1, V7X_LANES), fixed)),
        scratch_shapes=[pltpu.VMEM((tm, D), F32), pltpu.VMEM((1, V7X_LANES), F32)],
        compiler_params=_cparams(1, vmem),
        name="outproj",
    )(merged, x, w_out, g_ffn.reshape(1, D), w_router)


def _route(h2, w_ref, carry_ref):
    tm = h2.shape[0]
    logits = jnp.dot(h2.astype(BF16), w_ref[...], preferred_element_type=F32)
    lane = lax.broadcasted_iota(I32, (tm, V7X_LANES), 1)
    is_g = lane < N_GROUPS
    gl = jnp.where(is_g, logits, NEG_INF)
    gmax = jnp.max(gl, axis=1, keepdims=True)
    g_sel = jnp.min(jnp.where(gl == gmax, lane, V7X_LANES), axis=1, keepdims=True)
    gsum = jnp.sum(jnp.where(is_g, jnp.exp(gl - gmax), 0.0), axis=1, keepdims=True)
    p_group = 1.0 / gsum
    lo = N_GROUPS + EXPERTS_PER_GROUP * g_sel
    emask = (lane >= lo) & (lane < lo + EXPERTS_PER_GROUP)
    el = jnp.where(emask, logits, NEG_INF)
    e1 = jnp.max(el, axis=1, keepdims=True)
    i1 = jnp.min(jnp.where((el == e1) & emask, lane, V7X_LANES), axis=1, keepdims=True)
    emask2 = emask & (lane != i1)
    el2 = jnp.where(emask2, logits, NEG_INF)
    e2 = jnp.max(el2, axis=1, keepdims=True)
    i2 = jnp.min(jnp.where((el2 == e2) & emask2, lane, V7X_LANES), axis=1, keepdims=True)
    r = jnp.exp(e2 - e1)
    w1 = p_group / (1.0 + r)
    w2 = p_group * r / (1.0 + r)
    eid1 = i1 - N_GROUPS
    eid2 = i2 - N_GROUPS
    oh1 = jnp.where(lane == eid1, 1.0, 0.0)
    oh2 = jnp.where(lane == eid2, 1.0, 0.0)
    cnt = oh1 + oh2
    tri = jnp.where(lax.broadcasted_iota(I32, (tm, tm), 0) > lax.broadcasted_iota(I32, (tm, tm), 1), 1.0, 0.0)
    before = jnp.dot(tri.astype(BF16), cnt.astype(BF16), preferred_element_type=F32) + carry_ref[...]
    rank1 = jnp.sum(oh1 * before, axis=1, keepdims=True)
    rank2 = jnp.sum(oh2 * before, axis=1, keepdims=True)
    carry_ref[...] = carry_ref[...] + jnp.sum(cnt, axis=0, keepdims=True)
    info = jnp.zeros((tm, V7X_LANES), F32)
    for k, val in enumerate((eid1.astype(F32), eid2.astype(F32), w1, w2, rank1, rank2)):
        info = jnp.where(lane == k, val, info)
    return info


def _row_copy(src_ref, src_row, dst_ref, dst_row, sem):
    return pltpu.make_async_copy(src_ref.at[pl.ds(src_row, 1), :], dst_ref.at[pl.ds(dst_row, 1), :], sem)


ROW_DMA_UNROLL = 8


def _slots_kernel(info_ref, ps_ref, o_ref):
    info = info_ref[...]
    lane = lax.broadcasted_iota(I32, info.shape, 1)
    lane_f = lane.astype(F32)
    ps = ps_ref[...]
    out = jnp.zeros(info.shape, F32)
    for k in range(TOPK_IN_GROUP):
        start = jnp.sum(jnp.where(lane_f == info[:, k:k + 1], ps, 0.0), axis=1, keepdims=True)
        out = jnp.where(lane == k, start + info[:, 4 + k:5 + k], out)
    o_ref[...] = out.astype(I32)


def _slots(info, pad_start, tm=1024):
    T = info.shape[0]
    ps = jnp.pad(pad_start.astype(F32), (0, V7X_LANES - N_EXPERTS)).reshape(1, V7X_LANES)
    out = pl.pallas_call(
        _slots_kernel,
        out_shape=jax.ShapeDtypeStruct((T, V7X_LANES), I32),
        grid=(T // tm,),
        in_specs=[pl.BlockSpec((tm, V7X_LANES), lambda i: (i, 0)), pl.BlockSpec((1, V7X_LANES), lambda i: (0, 0))],
        out_specs=pl.BlockSpec((tm, V7X_LANES), lambda i: (i, 0)),
        compiler_params=_cparams(1, 16 * _nbytes((tm, V7X_LANES), F32)),
        name="slots",
    )(info, ps)
    return out[:, :TOPK_IN_GROUP].reshape(-1)


def _dispatch_kernel(slot_ref, pend_ref, h_ref, xs_ref, zero_ref, sem):
    tm = h_ref.shape[0]
    step = pl.program_id(0)
    base = step * tm

    @pl.when(step == 0)
    def _():
        zero_ref[...] = jnp.zeros_like(zero_ref)

        def tail(e):
            return pltpu.make_async_copy(
                zero_ref, xs_ref.at[pl.ds(pl.multiple_of(pend_ref[e] - MOE_ROWS, MOE_ROWS), MOE_ROWS), :], sem)

        def nonempty(e):
            return pend_ref[e] > (pend_ref[e - 1] if e else 0)

        def unused(b):
            return pltpu.make_async_copy(
                zero_ref, xs_ref.at[pl.ds(pl.multiple_of(b * MOE_ROWS, MOE_ROWS), MOE_ROWS), :], sem)

        first_unused = pend_ref[N_EXPERTS - 1] // MOE_ROWS
        n_blocks = xs_ref.shape[0] // MOE_ROWS
        for e in range(N_EXPERTS):
            pl.when(nonempty(e))(lambda e=e: tail(e).start())
        lax.fori_loop(first_unused, n_blocks, lambda b, c: (unused(b).start(), c)[1], 0)
        for e in range(N_EXPERTS):
            pl.when(nonempty(e))(lambda e=e: tail(e).wait())
        lax.fori_loop(first_unused, n_blocks, lambda b, c: (unused(b).wait(), c)[1], 0)

    def issue(rb, c):
        for u in range(ROW_DMA_UNROLL):
            r = rb * ROW_DMA_UNROLL + u
            for k in range(TOPK_IN_GROUP):
                _row_copy(h_ref, r, xs_ref, slot_ref[(base + r) * TOPK_IN_GROUP + k], sem).start(priority=k % 2)
        return c

    lax.fori_loop(0, tm // ROW_DMA_UNROLL, issue, 0)
    for _ in range(tm * TOPK_IN_GROUP):
        _row_copy(h_ref, 0, xs_ref, 0, sem).wait()


def _dispatch(slots, pad_end, h2p, m_pad, tm=256):
    T, C = h2p.shape
    return pl.pallas_call(
        _dispatch_kernel,
        out_shape=jax.ShapeDtypeStruct((m_pad, C), h2p.dtype),
        grid_spec=pltpu.PrefetchScalarGridSpec(
            num_scalar_prefetch=2,
            grid=(T // tm,),
            in_specs=[pl.BlockSpec((tm, C), lambda i, sl, pe: (i, 0))],
            out_specs=pl.BlockSpec(memory_space=pl.ANY),
            scratch_shapes=[pltpu.VMEM((MOE_ROWS, C), h2p.dtype), pltpu.SemaphoreType.DMA(())],
        ),
        compiler_params=_cparams(1, 6 * _nbytes((tm, C), h2p.dtype)),
        name="dispatch",
    )(slots, pad_end, h2p)


def _expert_kernel(seg_ref, sexp_ref, nu_ref, xs_ref, wg_ref, wu_ref, wd_ref, y_ref, wgf_ref, wuf_ref, wdf_ref,
                   wgb_ref, wub_ref, wdb_ref, sem):
    b = pl.program_id(0)
    used = b < nu_ref[0]
    seg = seg_ref[b]
    first = used & ((b == 0) | (seg != seg_ref[jnp.maximum(b - 1, 0)]))

    def weights_copy(s, part):
        slot = s % 2
        src, dst = ((wg_ref, wgf_ref), (wu_ref, wuf_ref), (wd_ref, wdf_ref))[part]
        return pltpu.make_async_copy(src.at[sexp_ref[s]], dst.at[slot], sem.at[slot])

    @pl.when(b == 0)
    def _():
        for part in range(3):
            weights_copy(0, part).start()

    @pl.when(first)
    def _():
        for part in range(3):
            weights_copy(seg, part).wait()
        slot = seg % 2
        wgb_ref[...] = wgf_ref[slot].astype(BF16)
        wub_ref[...] = wuf_ref[slot].astype(BF16)
        wdb_ref[...] = wdf_ref[slot].astype(BF16)

    @pl.when(first & (seg + 1 < nu_ref[1]))
    def _():
        for part in range(3):
            weights_copy(seg + 1, part).start()

    @pl.when(used)
    def _():
        x = _unpack_bf16_pairs(xs_ref[...]).astype(BF16)
        a = jnp.dot(x, wgb_ref[...], preferred_element_type=F32)
        u = jnp.dot(x, wub_ref[...], preferred_element_type=F32)
        hm = (a * jax.nn.sigmoid(a) * u).astype(BF16)
        y_ref[...] = _pack_bf16_pairs(jnp.dot(hm, wdb_ref[...], preferred_element_type=F32))

    @pl.when(jnp.logical_not(used))
    def _():
        y_ref[...] = jnp.zeros_like(y_ref)


def _experts(block_segment, segment_expert, n_used, xs, w_gate, w_up, w_down):
    m_pad, C = xs.shape
    D = 2 * C
    R, Fd = MOE_ROWS, D_EXPERT
    vmem = 4 * _nbytes((R, C), xs.dtype) + 6 * _nbytes((D, Fd), F32) + 3 * _nbytes((D, Fd), BF16)
    vmem += 6 * _nbytes((R, D), F32)
    hbm = pl.BlockSpec(memory_space=pl.ANY)
    return pl.pallas_call(
        _expert_kernel,
        out_shape=jax.ShapeDtypeStruct((m_pad, C), xs.dtype),
        grid_spec=pltpu.PrefetchScalarGridSpec(
            num_scalar_prefetch=3,
            grid=(m_pad // R,),
            in_specs=[
                pl.BlockSpec((R, C), lambda b, sg, se, nu: (jnp.minimum(b, jnp.maximum(nu[0] - 1, 0)), 0)),
                hbm, hbm, hbm,
            ],
            out_specs=pl.BlockSpec((R, C), lambda b, sg, se, nu: (b, 0)),
            scratch_shapes=[pltpu.VMEM((2, D, Fd), F32), pltpu.VMEM((2, D, Fd), F32), pltpu.VMEM((2, Fd, D), F32),
                            pltpu.VMEM((D, Fd), BF16), pltpu.VMEM((D, Fd), BF16), pltpu.VMEM((Fd, D), BF16),
                            pltpu.SemaphoreType.DMA((2,))],
        ),
        compiler_params=_cparams(1, vmem),
        name="experts",
    )(block_segment, segment_expert, n_used, xs, w_gate, w_up, w_down)


def _combine_kernel(slot_ref, x1_ref, info_ref, yb_ref, p_ref, gp_ref, wpg_ref, wpp_ref, gf_ref, o_ref, ybuf, sem):
    tm, D = x1_ref.shape
    step = pl.program_id(0)
    n_tiles = pl.num_programs(0) - 2
    K = TOPK_IN_GROUP
    GROUPS = 8

    def fetch(tile, group=None):
        slot = tile % 2
        rows = range(tm) if group is None else range(group * tm // GROUPS, (group + 1) * tm // GROUPS)
        for r in rows:
            for k in range(K):
                _row_copy(yb_ref, slot_ref[(tile * tm + r) * K + k], ybuf.at[slot * K + k], r,
                          sem.at[slot]).start(priority=k % 2)

    def drain(tile):
        slot = tile % 2
        for _ in range(tm * K):
            _row_copy(yb_ref, 0, ybuf.at[0], 0, sem.at[slot]).wait()

    @pl.when(step == 0)
    def _():
        fetch(step)

    @pl.when(step > 0)
    def _():
        drain(step - 1)

    @pl.when((step > 0) & (step <= n_tiles))
    def _():
        tile = step - 1
        slot = tile % 2
        info = info_ref[...]
        x2 = (x1_ref[...] + info[:, 2:3] * _unpack_bf16_pairs(ybuf[slot * K])
              + info[:, 3:4] * _unpack_bf16_pairs(ybuf[slot * K + 1]))
        hp = _rms(x2, gp_ref[...]).astype(BF16)
        pp = jnp.dot(p_ref[...].astype(BF16), wpp_ref[...], preferred_element_type=F32)
        cw = D // GROUPS
        zs = []
        for c in range(GROUPS):
            fetch(step, c)
            zs.append(jnp.dot(hp, wpg_ref[:, c * cw:(c + 1) * cw], preferred_element_type=F32))
        x3 = x2 + jax.nn.sigmoid(jnp.concatenate(zs, axis=1)) * pp
        o_ref[...] = _rms(x3, gf_ref[...])


def _combine(slots, x1, info, yb, p, g_ple, w_ple_gate, w_ple_proj, g_final, tm=256):
    T, D = x1.shape
    n_tiles = T // tm
    slots_padded = jnp.concatenate([slots, jnp.zeros((tm * TOPK_IN_GROUP,), I32)])
    vmem = 4 * _nbytes((tm, D), F32) + _nbytes((D, D), BF16) + _nbytes((PLE_DIM, D), BF16)
    vmem += 2 * _nbytes((tm, PLE_DIM), F32) + 2 * TOPK_IN_GROUP * _nbytes((tm, D // 2), yb.dtype)
    vmem += 6 * _nbytes((tm, D), F32)
    tile = lambda i, sl: (jnp.clip(i - 1, 0, n_tiles - 1), 0)
    fixed = lambda i, sl: (0, 0)
    resident = pl.Buffered(1)
    return pl.pallas_call(
        _combine_kernel,
        out_shape=jax.ShapeDtypeStruct((T, D), F32),
        grid_spec=pltpu.PrefetchScalarGridSpec(
            num_scalar_prefetch=1,
            grid=(n_tiles + 2,),
            in_specs=[
                pl.BlockSpec((tm, D), tile),
                pl.BlockSpec((tm, V7X_LANES), tile),
                pl.BlockSpec(memory_space=pl.ANY),
                pl.BlockSpec((tm, PLE_DIM), tile),
                pl.BlockSpec((1, D), fixed),
                pl.BlockSpec((D, D), fixed, pipeline_mode=resident),
                pl.BlockSpec((PLE_DIM, D), fixed, pipeline_mode=resident),
                pl.BlockSpec((1, D), fixed),
            ],
            out_specs=pl.BlockSpec((tm, D), tile),
            scratch_shapes=[pltpu.VMEM((2 * TOPK_IN_GROUP, tm, yb.shape[1]), yb.dtype),
                            pltpu.SemaphoreType.DMA((2,))],
        ),
        compiler_params=_cparams(1, vmem),
        name="combine",
    )(slots_padded, x1, info, yb, p, g_ple.reshape(1, D), w_ple_gate, w_ple_proj, g_final.reshape(1, D))


def _rope_tables(T):
    half, BS = MOBA_HEAD_DIM // 2, MOBA_BLOCK
    inv_freq = ROPE_THETA ** (-jnp.arange(half, dtype=F32) / half)
    ang_a = (jnp.arange(T // BS, dtype=F32) * BS)[:, None, None] * inv_freq
    ang_b = jnp.arange(BS, dtype=F32)[None, :, None] * inv_freq
    ca, sa, cb, sb = jnp.cos(ang_a), jnp.sin(ang_a), jnp.cos(ang_b), jnp.sin(ang_b)
    cos = (ca * cb - sa * sb).reshape(T, half)
    sin = (sa * cb + ca * sb).reshape(T, half)
    return jnp.concatenate([cos, cos], axis=1), jnp.concatenate([-sin, sin], axis=1)


def _mixers(x2d, g_mix, w_in, lb, hgrn_norm_g):
    T = x2d.shape[0]
    W = HGRN_WIDTH
    w = w_in
    cos, sin = _rope_tables(T)
    log_lb = jnp.log(lb).reshape(1, W)
    log_1m = jnp.log1p(-lb).reshape(1, W)
    hq, h = _norm_proj(x2d, g_mix, w, 0 * W, W, _ep_silu, BF16)
    logf = _proj(h, w, 1 * W, W, _ep_logf, F32, col_extras=(log_lb, log_1m))
    hi = _proj(h, w, 2 * W, W, _ep_identity, BF16)
    hog = _proj(h, w, 3 * W, W, _ep_silu, BF16)
    scale = MOBA_HEAD_DIM ** -0.5
    mqt = _proj_t(h, w, 4 * W, functools.partial(_ep_rope, scale=scale), row_extras=(cos, sin))
    blk = jnp.arange(T, dtype=I32)[:, None] // MOBA_BLOCK
    blk_onehot = (blk == jnp.arange(V7X_LANES, dtype=I32)[None, :]).astype(BF16)
    mk = _proj(h, w, 5 * W, W, functools.partial(_ep_rope, scale=1.0), BF16, row_extras=(cos, sin))
    mvt = _proj_t(h, w, 6 * W, _ep_identity, ones_rows=MOBA_VT_ROWS - MOBA_HEAD_DIM)
    gates = _proj(h, w, 7 * W, 2 * D_MODEL, _ep_sigmoid, BF16)
    o_hgrn = _hgrn(hq, logf, hi, hog, hgrn_norm_g)
    o_moba = _moba(mqt, mk, blk_onehot, mvt)
    return o_hgrn, o_moba, gates


def _moe_plan(info, cnt, T):
    R = MOE_ROWS
    counts = cnt[0, :N_EXPERTS].astype(I32)
    padded = (counts + R - 1) // R * R
    pad_end = jnp.cumsum(padded)
    pad_start = pad_end - padded
    slots = _slots(info, pad_start)
    n_blocks = (T * TOPK_IN_GROUP) // R + N_EXPERTS
    block_row = jnp.arange(n_blocks, dtype=I32) * R
    block_expert = jnp.minimum(jnp.sum(pad_end[None, :] <= block_row[:, None], axis=1), N_EXPERTS - 1).astype(I32)
    has = counts > 0
    segment_expert = jnp.argsort(jnp.logical_not(has), stable=True).astype(I32)
    block_segment = (jnp.cumsum(has.astype(I32))[block_expert] - 1).astype(I32)
    n_used = jnp.stack([pad_end[-1] // R, jnp.sum(has.astype(I32))]).astype(I32)
    return slots, pad_end.astype(I32), block_segment, segment_expert, n_used, n_blocks * R


def kernel(x, p, norm_mix_g, w_in, hgrn_lb_raw, hgrn_norm_g, w_up_hgrn, w_up_moba, w_out, norm_ffn_g,
           w_router_group, w_router_expert, w_exp_gate, w_exp_up, w_exp_down, norm_ple_g, w_ple_gate,
           w_ple_proj, norm_final_g):
    B, T, D = x.shape
    assert B == 1 and D == D_MODEL and w_in.shape[0] == 1 and T % (4 * MOBA_BLOCK) == 0
    lower_bounds = jnp.cumsum(jax.nn.softmax(hgrn_lb_raw.astype(F32), axis=0), axis=0)
    x2d = x.reshape(T, D)
    o_hgrn, o_moba, gates = _mixers(x2d, norm_mix_g[0], w_in[0], lower_bounds[0], hgrn_norm_g[0])
    merged = _merge(o_hgrn, o_moba, gates, w_up_hgrn[0].astype(BF16), w_up_moba[0].astype(BF16))
    w_router = jnp.pad(jnp.concatenate([w_router_group[0], w_router_expert[0]], axis=1),
                       ((0, 0), (0, V7X_LANES - N_GROUPS - N_EXPERTS))).astype(BF16)
    x1, h2p, info, cnt = _outproj(merged, x2d, w_out[0].astype(BF16), norm_ffn_g[0], w_router)
    slots, pad_end, block_segment, segment_expert, n_used, m_pad = _moe_plan(info, cnt, T)
    xs = _dispatch(slots, pad_end, h2p, m_pad)
    yb = _experts(block_segment, segment_expert, n_used, xs, w_exp_gate[0], w_exp_up[0], w_exp_down[0])
    out = _combine(slots, x1, info, yb, p[0].reshape(T, PLE_DIM), norm_ple_g[0], w_ple_gate[0].astype(BF16),
                   w_ple_proj[0].astype(BF16), norm_final_g)
    return out.reshape(B, T, D)
```
